```python
import jax, jax.numpy as jnp
from jax import lax
import numpy as np

D_MODEL = 1024
BATCH = 8
SEQ = 2048
DEPTH = 1
DEC_BATCH = 128
DEC_SEQ = 4
PAST_LEN = 16384
PAGE_SIZE = 128

MIX_WIDTH = D_MODEL
SSM_WIDTH = MIX_WIDTH // 2
CONV_CH = MIX_WIDTH - SSM_WIDTH
SSM_GROUP_CH = 16
SSM_GROUPS = SSM_WIDTH // SSM_GROUP_CH
SSM_STATE = 64
CONV_WIDTH = 31
CONV_BUF = CONV_WIDTH - 1
IN_PROJ = SSM_WIDTH + 2 * CONV_CH
N_EXPERT_GROUPS = 4
EXPERTS_PER_GROUP = 8
N_EXPERTS = N_EXPERT_GROUPS * EXPERTS_PER_GROUP
TOP_K = 2
D_EXPERT = D_MODEL // 2
EPS = 1e-6
DT_MIN = 0.001
DT_MAX = 0.1

kernel_name = "hymba_s5_conformer_hiermoe_decode_step"


def rms_norm(x, g):
    xf = x.astype(jnp.float32)
    y = xf * lax.rsqrt(jnp.mean(xf * xf, axis=-1, keepdims=True) + EPS)
    return y * g.astype(jnp.float32)


def layer_norm(x, g, b):
    xf = x.astype(jnp.float32)
    mu = jnp.mean(xf, axis=-1, keepdims=True)
    var = jnp.mean(jnp.square(xf - mu), axis=-1, keepdims=True)
    return (xf - mu) * lax.rsqrt(var + EPS) * g.astype(jnp.float32) + b.astype(jnp.float32)


def _complex_affine_combine(e1, e2):
    a1r, a1i, b1r, b1i = e1
    a2r, a2i, b2r, b2i = e2
    ar = a2r * a1r - a2i * a1i
    ai = a2r * a1i + a2i * a1r
    br = a2r * b1r - a2i * b1i + b2r
    bi = a2r * b1i + a2i * b1r + b2i
    return (ar, ai, br, bi)


def ssm_mixer(u, h0_re, h0_im, a_re, a_im, log_dt, b_re, b_im, c_re, c_im, d_skip, w_glu, b_glu):
    f32 = jnp.float32
    bsz, t, _ = u.shape
    uf = u.astype(f32).reshape(bsz, t, SSM_GROUPS, SSM_GROUP_CH)
    lam_re = jnp.minimum(a_re.astype(f32), -1e-4)
    lam_im = a_im.astype(f32)
    dt = jnp.exp(log_dt.astype(f32))[:, None]
    mag = jnp.exp(lam_re * dt)
    ab_re = mag * jnp.cos(lam_im * dt)
    ab_im = mag * jnp.sin(lam_im * dt)
    den = lam_re * lam_re + lam_im * lam_im
    num_re = ab_re - 1.0
    coef_re = (num_re * lam_re + ab_im * lam_im) / den
    coef_im = (ab_im * lam_re - num_re * lam_im) / den
    b_re = b_re.astype(f32)
    b_im = b_im.astype(f32)
    bb_re = coef_re[..., None] * b_re - coef_im[..., None] * b_im
    bb_im = coef_re[..., None] * b_im + coef_im[..., None] * b_re
    bu_re = jnp.einsum('btgh,gph->btgp', uf, bb_re)
    bu_im = jnp.einsum('btgh,gph->btgp', uf, bb_im)
    h0_re = h0_re.astype(f32)
    h0_im = h0_im.astype(f32)
    bu_re = bu_re.at[:, 0].add(ab_re * h0_re - ab_im * h0_im)
    bu_im = bu_im.at[:, 0].add(ab_re * h0_im + ab_im * h0_re)
    a_seq_re = jnp.broadcast_to(ab_re, (1, t) + ab_re.shape)
    a_seq_im = jnp.broadcast_to(ab_im, (1, t) + ab_im.shape)
    _, _, hr, hi = lax.associative_scan(_complex_affine_combine,
                                        (a_seq_re, a_seq_im, bu_re, bu_im), axis=1)
    y = (jnp.einsum('ghp,btgp->btgh', c_re.astype(f32), hr)
         - jnp.einsum('ghp,btgp->btgh', c_im.astype(f32), hi)
         + d_skip.astype(f32) * uf)
    y = jax.nn.gelu(y.reshape(bsz, t, SSM_WIDTH))
    y = y * jax.nn.sigmoid(y @ w_glu + b_glu)
    return y, hr[:, -1], hi[:, -1]


def conv_mixer(v, buf, w_dw, b_dw, ln_g, ln_b):
    f32 = jnp.float32
    a, g = jnp.split(v.astype(f32), 2, axis=-1)
    glu = a * jax.nn.sigmoid(g)
    ext = jnp.concatenate([buf.astype(f32), glu], axis=1)
    out = lax.conv_general_dilated(ext, w_dw.astype(f32)[:, None, :], window_strides=(1,),
                                   padding='VALID', dimension_numbers=('NWC', 'WIO', 'NWC'),
                                   feature_group_count=CONV_CH) + b_dw
    out = jax.nn.silu(layer_norm(out, ln_g, ln_b))
    return out, ext[:, -CONV_BUF:]


def hier_moe(h, w_rg, b_rg, w_re, b_re, w1, w3, w2):
    f32 = jnp.float32
    bsz, t, d = h.shape
    hf = h.reshape(bsz * t, d)
    p_grp = jax.nn.softmax((hf @ w_rg + b_rg).astype(f32), axis=-1)
    p_top, g_idx = lax.top_k(p_grp, 1)
    le = (jnp.einsum('nd,dge->nge', hf, w_re) + b_re).astype(f32)
    le_sel = jnp.take_along_axis(le, g_idx[:, :, None], axis=1)[:, 0]
    ev, ei = lax.top_k(le_sel, TOP_K)
    ew = jax.nn.softmax(ev, axis=-1) * p_top
    eidx = g_idx * EXPERTS_PER_GROUP + ei
    combine = jnp.sum(jax.nn.one_hot(eidx, N_EXPERTS, dtype=f32) * ew[..., None], axis=1)
    out = jnp.zeros((bsz * t, d), f32)
    for e in range(N_EXPERTS):
        hid = jax.nn.silu(hf @ w1[e]) * (hf @ w3[e])
        out = out + combine[:, e:e + 1] * (hid @ w2[e])
    return out.reshape(bsz, t, d)


def layer(x, c, h0_re, h0_im, conv_buf,
          w_ada, b_ada, g_norm_mix, w_in,
          ssm_a_re, ssm_a_im, ssm_log_dt, ssm_b_re, ssm_b_im, ssm_c_re, ssm_c_im, ssm_d,
          w_ssm_glu, b_ssm_glu, w_dw, b_dw, ln_conv_g, ln_conv_b,
          g_out_ssm, g_out_conv, w_out, g_norm_ffn,
          w_router_grp, b_router_grp, w_router_exp, b_router_exp,
          w_exp_gate, w_exp_up, w_exp_down):
    mod = jax.nn.silu(c.astype(jnp.float32)) @ w_ada + b_ada
    sh1, sc1, gt1, sh2, sc2, gt2 = jnp.split(mod[:, None, :], 6, axis=-1)
    n = rms_norm(x, g_norm_mix) * (1.0 + sc1) + sh1
    proj = n @ w_in
    ys, hr, hi = ssm_mixer(proj[..., :SSM_WIDTH], h0_re, h0_im, ssm_a_re, ssm_a_im, ssm_log_dt,
                           ssm_b_re, ssm_b_im, ssm_c_re, ssm_c_im, ssm_d, w_ssm_glu, b_ssm_glu)
    yc, new_buf = conv_mixer(proj[..., SSM_WIDTH:], conv_buf, w_dw, b_dw, ln_conv_g, ln_conv_b)
    merged = jnp.concatenate([rms_norm(ys, g_out_ssm), rms_norm(yc, g_out_conv)], axis=-1) @ w_out
    x = x + gt1 * merged
    n2 = rms_norm(x, g_norm_ffn) * (1.0 + sc2) + sh2
    x = x + gt2 * hier_moe(n2, w_router_grp, b_router_grp, w_router_exp, b_router_exp,
                           w_exp_gate, w_exp_up, w_exp_down)
    return x, hr, hi, new_buf


def setup_inputs(seed: int = 0) -> dict:
    key = jax.random.key(seed)
    ks = jax.random.split(key, 40)
    f32 = jnp.float32
    nrm = lambda k, shape, s: jax.random.normal(k, shape, f32) * s
    L, D, G, P, H = DEPTH, D_MODEL, SSM_GROUPS, SSM_STATE, SSM_GROUP_CH
    n_idx = jnp.arange(P, dtype=f32)
    inp = {}
    inp['x_prompt'] = nrm(ks[0], (BATCH, SEQ, D), 1.0)
    inp['x_sample'] = nrm(ks[1], (DEC_BATCH, DEC_SEQ, D), 1.0)
    inp['c_prompt'] = nrm(ks[2], (BATCH, D), 1.0)
    inp['c_sample'] = nrm(ks[3], (DEC_BATCH, D), 1.0)
    inp['state_ssm_re'] = nrm(ks[4], (L, DEC_BATCH, G, P), 0.5)
    inp['state_ssm_im'] = nrm(ks[5], (L, DEC_BATCH, G, P), 0.5)
    inp['cache_conv'] = nrm(ks[6], (L, DEC_BATCH, CONV_BUF, CONV_CH), 0.5)
    inp['w_ada'] = nrm(ks[7], (L, D, 6 * D), 0.3 * D ** -0.5)
    inp['b_ada'] = nrm(ks[8], (L, 6 * D), 0.02)
    inp['g_norm_mix'] = 1.0 + nrm(ks[9], (L, D), 0.02)
    inp['w_in'] = nrm(ks[10], (L, D, IN_PROJ), D ** -0.5)
    inp['ssm_a_re'] = -0.5 + nrm(ks[11], (L, G, P), 0.01)
    inp['ssm_a_im'] = jnp.pi * n_idx + nrm(ks[12], (L, G, P), 0.01)
    inp['ssm_log_dt'] = jax.random.uniform(ks[13], (L, G), f32, float(np.log(DT_MIN)), float(np.log(DT_MAX)))
    inp['ssm_b_re'] = nrm(ks[14], (L, G, P, H), (2.0 * H) ** -0.5)
    inp['ssm_b_im'] = nrm(ks[15], (L, G, P, H), (2.0 * H) ** -0.5)
    inp['ssm_c_re'] = nrm(ks[16], (L, G, H, P), (2.0 * P) ** -0.5)
    inp['ssm_c_im'] = nrm(ks[17], (L, G, H, P), (2.0 * P) ** -0.5)
    inp['ssm_d'] = nrm(ks[18], (L, G, H), 1.0)
    inp['w_ssm_glu'] = nrm(ks[19], (L, SSM_WIDTH, SSM_WIDTH), SSM_WIDTH ** -0.5)
    inp['b_ssm_glu'] = nrm(ks[20], (L, SSM_WIDTH), 0.02)
    inp['w_dw'] = nrm(ks[21], (L, CONV_WIDTH, CONV_CH), CONV_WIDTH ** -0.5)
    inp['b_dw'] = nrm(ks[22], (L, CONV_CH), 0.02)
    inp['ln_conv_g'] = 1.0 + nrm(ks[23], (L, CONV_CH), 0.02)
    inp['ln_conv_b'] = nrm(ks[24], (L, CONV_CH), 0.02)
    inp['g_out_ssm'] = 1.0 + nrm(ks[25], (L, SSM_WIDTH), 0.02)
    inp['g_out_conv'] = 1.0 + nrm(ks[26], (L, CONV_CH), 0.02)
    inp['w_out'] = nrm(ks[27], (L, MIX_WIDTH, D), MIX_WIDTH ** -0.5)
    inp['g_norm_ffn'] = 1.0 + nrm(ks[28], (L, D), 0.02)
    inp['w_router_grp'] = nrm(ks[29], (L, D, N_EXPERT_GROUPS), D ** -0.5)
    inp['b_router_grp'] = nrm(ks[30], (L, N_EXPERT_GROUPS), 0.01)
    inp['w_router_exp'] = nrm(ks[31], (L, D, N_EXPERT_GROUPS, EXPERTS_PER_GROUP), D ** -0.5)
    inp['b_router_exp'] = nrm(ks[32], (L, N_EXPERT_GROUPS, EXPERTS_PER_GROUP), 0.01)
    inp['w_exp_gate'] = nrm(ks[33], (L, N_EXPERTS, D, D_EXPERT), D ** -0.5)
    inp['w_exp_up'] = nrm(ks[34], (L, N_EXPERTS, D, D_EXPERT), D ** -0.5)
    inp['w_exp_down'] = nrm(ks[35], (L, N_EXPERTS, D_EXPERT, D), D_EXPERT ** -0.5)
    inp['g_final'] = 1.0 + nrm(ks[36], (D,), 0.02)
    return inp


def reference(x_prompt, x_sample, c_prompt, c_sample, state_ssm_re, state_ssm_im, cache_conv,
              w_ada, b_ada, g_norm_mix, w_in,
              ssm_a_re, ssm_a_im, ssm_log_dt, ssm_b_re, ssm_b_im, ssm_c_re, ssm_c_im, ssm_d,
              w_ssm_glu, b_ssm_glu, w_dw, b_dw, ln_conv_g, ln_conv_b,
              g_out_ssm, g_out_conv, w_out, g_norm_ffn,
              w_router_grp, b_router_grp, w_router_exp, b_router_exp,
              w_exp_gate, w_exp_up, w_exp_down, g_final):
    stacked = (w_ada, b_ada, g_norm_mix, w_in,
               ssm_a_re, ssm_a_im, ssm_log_dt, ssm_b_re, ssm_b_im, ssm_c_re, ssm_c_im, ssm_d,
               w_ssm_glu, b_ssm_glu, w_dw, b_dw, ln_conv_g, ln_conv_b,
               g_out_ssm, g_out_conv, w_out, g_norm_ffn,
               w_router_grp, b_router_grp, w_router_exp, b_router_exp,
               w_exp_gate, w_exp_up, w_exp_down)
    xp, xs = x_prompt, x_sample
    zero_h = jnp.zeros((BATCH, SSM_GROUPS, SSM_STATE), jnp.float32)
    zero_buf = jnp.zeros((BATCH, CONV_BUF, CONV_CH), jnp.float32)
    p_re, p_im, p_buf, s_re, s_im, s_buf = [], [], [], [], [], []
    for l in range(DEPTH):
        lp = [w[l] for w in stacked]
        xp, hr, hi, nb = layer(xp, c_prompt, zero_h, zero_h, zero_buf, *lp)
        p_re.append(hr); p_im.append(hi); p_buf.append(nb)
        xs, hr, hi, nb = layer(xs, c_sample, state_ssm_re[l], state_ssm_im[l], cache_conv[l], *lp)
        s_re.append(hr); s_im.append(hi); s_buf.append(nb)
    y_prompt = rms_norm(xp, g_final)
    y_sample = rms_norm(xs, g_final)
    return (y_prompt, y_sample,
            jnp.stack(p_re), jnp.stack(p_im), jnp.stack(p_buf),
            jnp.stack(s_re), jnp.stack(s_im), jnp.stack(s_buf))
```

```python
import functools

import jax
import jax.numpy as jnp
import numpy as np
from jax import lax
from jax.experimental import pallas as pl
from jax.experimental.pallas import tpu as pltpu

F32 = jnp.float32
BF16 = jnp.bfloat16

D = 1024
SSM_W = 512
CONV_CH = 512
G = 32
H = 16
P = 64
KW = 31
CB = KW - 1
NE = 32
NG = 4
EPG = 8
DE = 512
EPS = 1e-6
LANES = 128
NSTATE = G * P
NCHUNK = NSTATE // LANES

PROMPT_TC = 256
MOE_TM = 1024
VMEM_LIMIT = 56 * 1024 * 1024


def _rms(x):
    return x * lax.rsqrt(jnp.mean(x * x, axis=-1, keepdims=True) + EPS)


def _gelu_tanh(y):
    c = np.sqrt(2.0 / np.pi).astype(np.float32)
    return y * (0.5 * (1.0 + jnp.tanh(c * (y + 0.044715 * (y * y * y)))))


def _bdot(a, b):
    return jnp.dot(a.astype(BF16), b, preferred_element_type=F32)


def _mod_kernel(c_ref, w_ref, b_ref, o_ref):
    c = c_ref[...]
    s = c * jax.nn.sigmoid(c)
    o_ref[...] = jnp.dot(s, w_ref[...], preferred_element_type=F32,
                         precision=lax.Precision.HIGHEST) + b_ref[...]


def _mod_call(c_all, w_ada, b_ada):
    n = c_all.shape[0]
    tn = 512
    return pl.pallas_call(
        _mod_kernel,
        grid=(6 * D // tn,),
        in_specs=[pl.BlockSpec((n, D), lambda j: (0, 0)),
                  pl.BlockSpec((D, tn), lambda j: (0, j)),
                  pl.BlockSpec((1, tn), lambda j: (0, j))],
        out_specs=pl.BlockSpec((n, tn), lambda j: (0, j)),
        out_shape=jax.ShapeDtypeStruct((n, 6 * D), F32),
        compiler_params=pltpu.CompilerParams(dimension_semantics=("arbitrary",)),
        name="mod",
    )(c_all, w_ada, b_ada.reshape(1, 6 * D))


def _ssm_prep_kernel(a_re, a_im, log_dt, b_re, b_im, c_im,
                     ab_re_o, ab_im_o, bb_re_o, bb_im_o, cneg_o):
    lam_re = jnp.minimum(a_re[...], -1e-4)
    lam_im = a_im[...]
    dt = jnp.exp(log_dt[...])
    mag = jnp.exp(lam_re * dt)
    ab_re = mag * jnp.cos(lam_im * dt)
    ab_im = mag * jnp.sin(lam_im * dt)
    den = lam_re * lam_re + lam_im * lam_im
    num_re = ab_re - 1.0
    coef_re = (num_re * lam_re + ab_im * lam_im) / den
    coef_im = (ab_im * lam_re - num_re * lam_im) / den
    ab_re_o[...] = ab_re
    ab_im_o[...] = ab_im
    br = b_re[...]
    bi = b_im[...]
    bb_re_o[...] = coef_re * br - coef_im * bi
    bb_im_o[...] = coef_re * bi + coef_im * br
    cneg_o[...] = -c_im[...]


def _ssm_prep_call(a_re, a_im, log_dt, b_re, b_im, c_im):
    flat = lambda v: v.reshape(1, NSTATE)
    b_hs = lambda v: jnp.transpose(v, (2, 0, 1)).reshape(H, NSTATE)
    dt_row = jnp.broadcast_to(log_dt[:, None], (G, P)).reshape(1, NSTATE)
    ab_re, ab_im, bb_re, bb_im, cneg = pl.pallas_call(
        _ssm_prep_kernel,
        out_shape=(jax.ShapeDtypeStruct((1, NSTATE), F32), jax.ShapeDtypeStruct((1, NSTATE), F32),
                   jax.ShapeDtypeStruct((H, NSTATE), F32), jax.ShapeDtypeStruct((H, NSTATE), F32),
                   jax.ShapeDtypeStruct((G * H, P), F32)),
        name="ssm_prep",
    )(flat(a_re), flat(a_im), dt_row, b_hs(b_re), b_hs(b_im), c_im.reshape(G * H, P))
    ghp = lambda v: jnp.transpose(v.reshape(H, G, P), (1, 0, 2))
    return (ab_re.reshape(G, P), ab_im.reshape(G, P), ghp(bb_re), ghp(bb_im), cneg.reshape(G, H, P))


def _block_diag_weights(bb_re, bb_im, c_re, c_im_neg):
    eye8 = jnp.eye(8, dtype=F32)
    eye4 = jnp.eye(4, dtype=F32)
    eye2 = jnp.eye(2, dtype=F32)

    def wb_part(bb):
        x = bb.reshape(4, 8, H, P)
        return jnp.einsum('qghp,gk->qghkp', x, eye8).reshape(4, 8 * H, 8 * P)

    wb = jnp.concatenate([wb_part(bb_re), wb_part(bb_im)], axis=-1).astype(BF16)

    def wc_part(c):
        x = c.reshape(4, 4, 2, H, P)
        y = jnp.einsum('qjghp,jk,gl->qjgpklh', x, eye4, eye2)
        return y.reshape(NCHUNK, 2 * P, 4 * 2 * H)

    wc = jnp.concatenate([wc_part(c_re), wc_part(c_im_neg)], axis=1).astype(BF16)
    return wb, wc


def _front(x, sc1, sh1, g_mix, w_in_ref):
    n = _rms(x) * g_mix * (1.0 + sc1) + sh1
    proj = _bdot(n, w_in_ref[...])
    u = proj[:, :SSM_W]
    glu = proj[:, SSM_W:SSM_W + CONV_CH] * jax.nn.sigmoid(proj[:, SSM_W + CONV_CH:])
    return u, glu


def _ssm_out(y_lin, u, d_skip, w_glu_ref, b_glu, g_out_ssm):
    y = _gelu_tanh(y_lin + d_skip * u)
    ys = y * jax.nn.sigmoid(_bdot(y, w_glu_ref[...]) + b_glu)
    return _rms(ys) * g_out_ssm


def _conv_out(conv, b_dw, ln_g, ln_b, g_out_conv):
    c = conv + b_dw
    mu = jnp.mean(c, axis=-1, keepdims=True)
    cc = c - mu
    var = jnp.mean(cc * cc, axis=-1, keepdims=True)
    ln = cc * lax.rsqrt(var + EPS) * ln_g + ln_b
    yc = ln * jax.nn.sigmoid(ln)
    return _rms(yc) * g_out_conv


def _route(n2, w_r_ref, b_r):
    rows = n2.shape[0]
    lg = jnp.dot(n2, w_r_ref[...], preferred_element_type=F32,
                 precision=lax.Precision.HIGHEST) + b_r
    lane = lax.broadcasted_iota(jnp.int32, (rows, LANES), 1).astype(F32)
    ninf = -jnp.inf
    big = 1e9
    gmask = jnp.logical_and(lane >= NE, lane < NE + NG)
    gl = jnp.where(gmask, lg, ninf)
    gmax = jnp.max(gl, axis=-1, keepdims=True)
    gsum = jnp.sum(jnp.where(gmask, jnp.exp(gl - gmax), 0.0), axis=-1, keepdims=True)
    p_top = 1.0 / gsum
    gi = jnp.min(jnp.where(gl == gmax, lane, big), axis=-1, keepdims=True) - NE
    lo = gi * EPG
    emask = jnp.logical_and(lane >= lo, lane < lo + EPG)
    el = jnp.where(emask, lg, ninf)
    m1 = jnp.max(el, axis=-1, keepdims=True)
    i1 = jnp.min(jnp.where(el == m1, lane, big), axis=-1, keepdims=True)
    el2 = jnp.where(lane == i1, ninf, el)
    m2 = jnp.max(el2, axis=-1, keepdims=True)
    i2 = jnp.min(jnp.where(el2 == m2, lane, big), axis=-1, keepdims=True)
    e2 = jnp.exp(m2 - m1)
    den = 1.0 + e2
    w1 = p_top / den
    w2 = p_top * e2 / den
    out = jnp.where(lane == i1, w1, 0.0) + jnp.where(lane == i2, w2, 0.0)
    out = out + jnp.where(lane == NE, i1, 0.0) + jnp.where(lane == NE + 1, i2, 0.0)
    out = out + jnp.where(lane == NE + 2, w1, 0.0) + jnp.where(lane == NE + 3, w2, 0.0)
    return out


def _tail(x, ns, nc, gt1, sc2, sh2, w_out_ref, g_ffn, w_r_ref, b_r):
    merged = _bdot(jnp.concatenate([ns, nc], axis=-1), w_out_ref[...])
    x1 = x + gt1 * merged
    n2 = _rms(x1) * g_ffn * (1.0 + sc2) + sh2
    return x1, n2, _route(n2, w_r_ref, b_r)


def _prompt_mixer_kernel(x_ref, mod_ref, g_mix_ref, w_in_ref, wb_ref, a_ref, wc_ref, dsk_ref,
                         w_glu_ref, b_glu_ref, w_dw_ref, b_dw_ref, ln_g_ref, ln_b_ref,
                         g_os_ref, g_oc_ref, w_out_ref, g_ffn_ref, w_r_ref, b_r_ref,
                         x1_ref, n2_ref, route_ref, hr_ref, hi_ref, cache_ref,
                         sre, sim, hst, ebuf, *, tc, pt):
    c = pl.program_id(1)
    nc_chunks = pl.num_programs(1)

    @pl.when(c == 0)
    def _():
        hst[...] = jnp.zeros_like(hst)
        ebuf[pl.ds(0, 32), :] = jnp.zeros((32, CONV_CH), F32)

    x = x_ref[...]
    mod = mod_ref[...]
    sh1, sc1, gt1 = mod[0:1], mod[1:2], mod[2:3]
    sh2, sc2 = mod[3:4], mod[4:5]

    u, glu = _front(x, sc1, sh1, g_mix_ref[...], w_in_ref)
    ebuf[pl.ds(32, tc), :] = glu

    ub = u.astype(BF16)
    for q in range(4):
        r = jnp.dot(ub[:, q * LANES:(q + 1) * LANES], wb_ref[q], preferred_element_type=F32)
        for k in range(4):
            j = 4 * q + k
            sre[pl.ds(j * pt, tc), :] = r[:, k * LANES:(k + 1) * LANES]
            sim[pl.ds(j * pt, tc), :] = r[:, SSM_W + k * LANES:SSM_W + (k + 1) * LANES]

    ar0, ar1 = a_ref[0, 0:8, :], a_ref[0, 8:16, :]
    ai0, ai1 = a_ref[1, 0:8, :], a_ref[1, 8:16, :]

    def step(t, carry):
        hr0, hr1, hi0, hi1 = carry
        i0 = pl.ds(t, 8, stride=pt)
        i1 = pl.ds(t + 8 * pt, 8, stride=pt)
        nr0 = ar0 * hr0 - ai0 * hi0 + sre[i0, :]
        ni0 = ar0 * hi0 + ai0 * hr0 + sim[i0, :]
        nr1 = ar1 * hr1 - ai1 * hi1 + sre[i1, :]
        ni1 = ar1 * hi1 + ai1 * hr1 + sim[i1, :]
        sre[i0, :] = nr0
        sim[i0, :] = ni0
        sre[i1, :] = nr1
        sim[i1, :] = ni1
        return nr0, nr1, ni0, ni1

    init = (hst[0, 0:8, :], hst[0, 8:16, :], hst[1, 0:8, :], hst[1, 8:16, :])
    hr0, hr1, hi0, hi1 = lax.fori_loop(0, tc, step, init)
    hst[0, 0:8, :] = hr0
    hst[0, 8:16, :] = hr1
    hst[1, 0:8, :] = hi0
    hst[1, 8:16, :] = hi1

    @pl.when(c == nc_chunks - 1)
    def _():
        hr_ref[...] = hst[0]
        hi_ref[...] = hst[1]

    ys = []
    for q in range(4):
        acc = None
        for jj in range(4):
            j = 4 * q + jj
            lhs = jnp.concatenate([sre[pl.ds(j * pt, tc), :], sim[pl.ds(j * pt, tc), :]], axis=-1)
            d = jnp.dot(lhs.astype(BF16), wc_ref[j], preferred_element_type=F32)
            acc = d if acc is None else acc + d
        ys.append(acc)
    y_lin = jnp.concatenate(ys, axis=-1)
    ns = _ssm_out(y_lin, u, dsk_ref[...], w_glu_ref, b_glu_ref[...], g_os_ref[...])

    rb = 64
    convs = []
    for r0 in range(0, tc, rb):
        acc = None
        for k in range(KW):
            term = w_dw_ref[k:k + 1, :] * ebuf[pl.ds(r0 + k + 2, rb), :]
            acc = term if acc is None else acc + term
        convs.append(acc)
    conv = jnp.concatenate(convs, axis=0)
    nc = _conv_out(conv, b_dw_ref[...], ln_g_ref[...], ln_b_ref[...], g_oc_ref[...])

    @pl.when(c == nc_chunks - 1)
    def _():
        cache_ref[...] = ebuf[pl.ds(tc + 2, CB), :]

    ebuf[pl.ds(0, 32), :] = ebuf[pl.ds(tc, 32), :]

    x1, n2, route = _tail(x, ns, nc, gt1, sc2, sh2, w_out_ref, g_ffn_ref[...], w_r_ref, b_r_ref[...])
    x1_ref[...] = x1
    n2_ref[...] = n2.astype(BF16)
    route_ref[...] = route


def _const_spec(shape):
    nd = len(shape)
    return pl.BlockSpec(shape, lambda b, c: (0,) * nd)


def _prompt_mixer_call(x, mod6, wts, tc):
    bsz, t, _ = x.shape
    pt = tc + 8
    nc = t // tc
    kern = functools.partial(_prompt_mixer_kernel, tc=tc, pt=pt)
    in_specs = [pl.BlockSpec((None, tc, D), lambda b, c: (b, c, 0)),
                pl.BlockSpec((None, 6, D), lambda b, c: (b, 0, 0))]
    in_specs += [_const_spec(w.shape) for w in wts]
    out_shape = (jax.ShapeDtypeStruct((bsz, t, D), F32),
                 jax.ShapeDtypeStruct((bsz, t, D), BF16),
                 jax.ShapeDtypeStruct((bsz, t, LANES), F32),
                 jax.ShapeDtypeStruct((bsz, NCHUNK, LANES), F32),
                 jax.ShapeDtypeStruct((bsz, NCHUNK, LANES), F32),
                 jax.ShapeDtypeStruct((bsz, CB, CONV_CH), F32))
    out_specs = (pl.BlockSpec((None, tc, D), lambda b, c: (b, c, 0)),
                 pl.BlockSpec((None, tc, D), lambda b, c: (b, c, 0)),
                 pl.BlockSpec((None, tc, LANES), lambda b, c: (b, c, 0)),
                 pl.BlockSpec((None, NCHUNK, LANES), lambda b, c: (b, 0, 0)),
                 pl.BlockSpec((None, NCHUNK, LANES), lambda b, c: (b, 0, 0)),
                 pl.BlockSpec((None, CB, CONV_CH), lambda b, c: (b, 0, 0)))
    scratch = [pltpu.VMEM((NCHUNK * pt, LANES), F32),
               pltpu.VMEM((NCHUNK * pt, LANES), F32),
               pltpu.VMEM((2, NCHUNK, LANES), F32),
               pltpu.VMEM((tc + 32, CONV_CH), F32)]
    return pl.pallas_call(
        kern, grid=(bsz, nc), in_specs=in_specs, out_specs=out_specs, out_shape=out_shape,
        scratch_shapes=scratch,
        compiler_params=pltpu.CompilerParams(dimension_semantics=("arbitrary", "arbitrary"),
                                             vmem_limit_bytes=VMEM_LIMIT),
        name="prompt_mixer",
    )(x, mod6, *wts)


def _sample_mixer_kernel(x_ref, mod_ref, h0r_ref, h0i_ref, cache_ref,
                         g_mix_ref, w_in_ref, wb_ref, a_ref, wc_ref, dsk_ref,
                         w_glu_ref, b_glu_ref, w_dw_ref, b_dw_ref, ln_g_ref, ln_b_ref,
                         g_os_ref, g_oc_ref, w_out_ref, g_ffn_ref, w_r_ref, b_r_ref,
                         x1_ref, n2_ref, route_ref, hr_ref, hi_ref, glu_ref,
                         sre, sim, *, nb, nt):
    x = x_ref[...]

    def rows(i):
        m = mod_ref[:, i * D:(i + 1) * D]
        return jnp.concatenate([m] * nt, axis=0)

    sh1, sc1, gt1, sh2, sc2 = rows(0), rows(1), rows(2), rows(3), rows(4)
    u, glu = _front(x, sc1, sh1, g_mix_ref[...], w_in_ref)
    glu_ref[...] = glu

    ub = u.astype(BF16)
    for q in range(4):
        r = jnp.dot(ub[:, q * LANES:(q + 1) * LANES], wb_ref[q], preferred_element_type=F32)
        sre[:, q * SSM_W:(q + 1) * SSM_W] = r[:, :SSM_W]
        sim[:, q * SSM_W:(q + 1) * SSM_W] = r[:, SSM_W:]

    ar = a_ref[0:1, :]
    ai = a_ref[1:2, :]
    hr = h0r_ref[...]
    hi = h0i_ref[...]
    for t in range(nt):
        rs = pl.ds(t * nb, nb)
        nr = ar * hr - ai * hi + sre[rs, :]
        ni = ar * hi + ai * hr + sim[rs, :]
        sre[rs, :] = nr
        sim[rs, :] = ni
        hr, hi = nr, ni
    hr_ref[...] = hr
    hi_ref[...] = hi

    ys = []
    for q in range(4):
        acc = None
        for jj in range(4):
            j = 4 * q + jj
            lhs = jnp.concatenate([sre[:, j * LANES:(j + 1) * LANES],
                                   sim[:, j * LANES:(j + 1) * LANES]], axis=-1)
            d = jnp.dot(lhs.astype(BF16), wc_ref[j], preferred_element_type=F32)
            acc = d if acc is None else acc + d
        ys.append(acc)
    y_lin = jnp.concatenate(ys, axis=-1)
    ns = _ssm_out(y_lin, u, dsk_ref[...], w_glu_ref, b_glu_ref[...], g_os_ref[...])

    def ext(jrow):
        if jrow < CB:
            return cache_ref[jrow]
        return glu[(jrow - CB) * nb:(jrow - CB + 1) * nb, :]

    convs = []
    for t in range(nt):
        acc = None
        for k in range(KW):
            term = w_dw_ref[k:k + 1, :] * ext(t + k)
            acc = term if acc is None else acc + term
        convs.append(acc)
    conv = jnp.concatenate(convs, axis=0)
    nc = _conv_out(conv, b_dw_ref[...], ln_g_ref[...], ln_b_ref[...], g_oc_ref[...])

    x1, n2, route = _tail(x, ns, nc, gt1, sc2, sh2, w_out_ref, g_ffn_ref[...], w_r_ref, b_r_ref[...])
    x1_ref[...] = x1
    n2_ref[...] = n2.astype(BF16)
    route_ref[...] = route


def _sample_mixer_call(x_tm, mod_s, h0r, h0i, cache_tm, wts, nb, nt):
    n = nb * nt
    kern = functools.partial(_sample_mixer_kernel, nb=nb, nt=nt)
    out_shape = (jax.ShapeDtypeStruct((n, D), F32),
                 jax.ShapeDtypeStruct((n, D), BF16),
                 jax.ShapeDtypeStruct((n, LANES), F32),
                 jax.ShapeDtypeStruct((nb, NSTATE), F32),
                 jax.ShapeDtypeStruct((nb, NSTATE), F32),
                 jax.ShapeDtypeStruct((n, CONV_CH), F32))
    scratch = [pltpu.VMEM((n, NSTATE), F32), pltpu.VMEM((n, NSTATE), F32)]
    return pl.pallas_call(
        kern, out_shape=out_shape, scratch_shapes=scratch,
        compiler_params=pltpu.CompilerParams(vmem_limit_bytes=VMEM_LIMIT),
        name="sample_mixer",
    )(x_tm, mod_s, h0r, h0i, cache_tm, *wts)


def _moe_dense_kernel(n2_ref, route_ref, x1_ref, gt2_ref, w1_ref, w3_ref, w2_ref, gf_ref,
                      y_ref, acc_ref):
    e = pl.program_id(1)

    @pl.when(e == 0)
    def _():
        acc_ref[...] = jnp.zeros_like(acc_ref)

    h = n2_ref[...]
    a = jnp.dot(h, w1_ref[...].astype(BF16), preferred_element_type=F32)
    b = jnp.dot(h, w3_ref[...].astype(BF16), preferred_element_type=F32)
    hid = a * jax.nn.sigmoid(a) * b
    route = route_ref[...]
    lane = lax.broadcasted_iota(jnp.int32, route.shape, 1)
    cw = jnp.sum(jnp.where(lane == e, route, 0.0), axis=-1, keepdims=True)
    o = jnp.dot(hid.astype(BF16), w2_ref[...].astype(BF16), preferred_element_type=F32)
    acc_ref[...] += cw * o

    @pl.when(e == pl.num_programs(1) - 1)
    def _():
        xo = x1_ref[...] + gt2_ref[...] * acc_ref[...]
        y_ref[...] = _rms(xo) * gf_ref[...]


def _moe_dense_call(n2, route, x1, gt2, gt2_spec, w1, w3, w2, g_final, tm):
    n = n2.shape[0]
    return pl.pallas_call(
        _moe_dense_kernel,
        grid=(n // tm, NE),
        in_specs=[pl.BlockSpec((tm, D), lambda i, e: (i, 0)),
                  pl.BlockSpec((tm, LANES), lambda i, e: (i, 0)),
                  pl.BlockSpec((tm, D), lambda i, e: (i, 0)),
                  gt2_spec,
                  pl.BlockSpec((None, D, DE), lambda i, e: (e, 0, 0)),
                  pl.BlockSpec((None, D, DE), lambda i, e: (e, 0, 0)),
                  pl.BlockSpec((None, DE, D), lambda i, e: (e, 0, 0)),
                  pl.BlockSpec((1, D), lambda i, e: (0, 0))],
        out_specs=pl.BlockSpec((tm, D), lambda i, e: (i, 0)),
        out_shape=jax.ShapeDtypeStruct((n, D), F32),
        scratch_shapes=[pltpu.VMEM((tm, D), F32)],
        compiler_params=pltpu.CompilerParams(dimension_semantics=("arbitrary", "arbitrary"),
                                             vmem_limit_bytes=VMEM_LIMIT),
        name="moe_dense",
    )(n2, route, x1, gt2, w1, w3, w2, g_final.reshape(1, D))


def kernel(x_prompt, x_sample, c_prompt, c_sample, state_ssm_re, state_ssm_im, cache_conv, w_ada, b_ada, g_norm_mix, w_in, ssm_a_re, ssm_a_im, ssm_log_dt, ssm_b_re, ssm_b_im, ssm_c_re, ssm_c_im, ssm_d, w_ssm_glu, b_ssm_glu, w_dw, b_dw, ln_conv_g, ln_conv_b, g_out_ssm, g_out_conv, w_out, g_norm_ffn, w_router_grp, b_router_grp, w_router_exp, b_router_exp, w_exp_gate, w_exp_up, w_exp_down, g_final):
    depth = w_ada.shape[0]
    assert depth == 1
    bsz, seq, _ = x_prompt.shape
    nb, nt, _ = x_sample.shape

    c_all = jnp.concatenate([c_prompt, c_sample], axis=0)
    mod = _mod_call(c_all, w_ada[0], b_ada[0])
    mod_p = mod[:bsz].reshape(bsz, 6, D)
    mod_s = mod[bsz:]

    ab_re, ab_im, bb_re, bb_im, c_im_neg = _ssm_prep_call(
        ssm_a_re[0], ssm_a_im[0], ssm_log_dt[0], ssm_b_re[0], ssm_b_im[0], ssm_c_im[0])
    wb, wc = _block_diag_weights(bb_re, bb_im, ssm_c_re[0], c_im_neg)
    a_tok = jnp.stack([ab_re.reshape(NCHUNK, LANES), ab_im.reshape(NCHUNK, LANES)])
    a_row = jnp.stack([ab_re.reshape(NSTATE), ab_im.reshape(NSTATE)])

    w_r = jnp.concatenate([w_router_exp[0].reshape(D, NE), w_router_grp[0],
                           jnp.zeros((D, LANES - NE - NG), F32)], axis=1)
    b_r = jnp.concatenate([b_router_exp[0].reshape(NE), b_router_grp[0],
                           jnp.zeros((LANES - NE - NG,), F32)]).reshape(1, LANES)
    w_dw_p = jnp.concatenate([w_dw[0], jnp.zeros((1, CONV_CH), F32)], axis=0)

    row = lambda v: v.reshape(1, -1)
    common_a = (row(g_norm_mix[0]), w_in[0].astype(BF16), wb)
    common_b = (wc, row(ssm_d[0].reshape(SSM_W)), w_ssm_glu[0].astype(BF16), row(b_ssm_glu[0]),
                w_dw_p, row(b_dw[0]), row(ln_conv_g[0]), row(ln_conv_b[0]),
                row(g_out_ssm[0]), row(g_out_conv[0]), w_out[0].astype(BF16),
                row(g_norm_ffn[0]), w_r, b_r)

    wts_p = common_a + (a_tok,) + common_b
    x1_p, n2_p, route_p, hr_p, hi_p, cache_p = _prompt_mixer_call(x_prompt, mod_p, wts_p, PROMPT_TC)

    x_tm = jnp.transpose(x_sample, (1, 0, 2)).reshape(nt * nb, D)
    cache_tm = jnp.transpose(cache_conv[0], (1, 0, 2))
    wts_s = common_a + (a_row,) + common_b
    x1_s, n2_s, route_s, hr_s, hi_s, glu_s = _sample_mixer_call(
        x_tm, mod_s, state_ssm_re[0].reshape(nb, NSTATE), state_ssm_im[0].reshape(nb, NSTATE),
        cache_tm, wts_s, nb, nt)

    w1, w3, w2 = w_exp_gate[0], w_exp_up[0], w_exp_down[0]
    n_p = bsz * seq
    tiles_per_b = seq // MOE_TM
    gt2_p = mod_p[:, 5:6, :]
    y_p = _moe_dense_call(n2_p.reshape(n_p, D), route_p.reshape(n_p, LANES), x1_p.reshape(n_p, D),
                          gt2_p, pl.BlockSpec((None, 1, D), lambda i, e: (i // tiles_per_b, 0, 0)),
                          w1, w3, w2, g_final, MOE_TM)
    gt2_s = jnp.tile(mod_s[:, 5 * D:], (nt, 1))
    y_s = _moe_dense_call(n2_s, route_s, x1_s, gt2_s,
                          pl.BlockSpec((nt * nb, D), lambda i, e: (0, 0)),
                          w1, w3, w2, g_final, nt * nb)

    y_prompt = y_p.reshape(bsz, seq, D)
    y_sample = jnp.transpose(y_s.reshape(nt, nb, D), (1, 0, 2))
    new_cache_s = jnp.concatenate(
        [cache_conv[0][:, nt:, :], jnp.transpose(glu_s.reshape(nt, nb, CONV_CH), (1, 0, 2))], axis=1)
    return (y_prompt, y_sample,
            hr_p.reshape(1, bsz, G, P), hi_p.reshape(1, bsz, G, P), cache_p[None],
            hr_s.reshape(1, nb, G, P), hi_s.reshape(1, nb, G, P), new_cache_s[None])
```

```python
import functools

import jax
import jax.numpy as jnp
import numpy as np
from jax import lax
from jax.experimental import pallas as pl
from jax.experimental.pallas import tpu as pltpu

F32 = jnp.float32
BF16 = jnp.bfloat16

D = 1024
SSM_W = 512
CONV_CH = 512
G = 32
H = 16
P = 64
KW = 31
CB = KW - 1
NE = 32
NG = 4
EPG = 8
DE = 512
EPS = 1e-6
LANES = 128
NSTATE = G * P
NCHUNK = NSTATE // LANES

PROMPT_TC = 256
VMEM_LIMIT = 56 * 1024 * 1024


def _rms(x):
    return x * lax.rsqrt(jnp.mean(x * x, axis=-1, keepdims=True) + EPS)


def _gelu_tanh(y):
    c = np.sqrt(2.0 / np.pi).astype(np.float32)
    return y * (0.5 * (1.0 + jnp.tanh(c * (y + 0.044715 * (y * y * y)))))


def _bdot(a, b):
    return jnp.dot(a.astype(BF16), b, preferred_element_type=F32)


def _mod_kernel(c_ref, w_ref, b_ref, o_ref):
    c = c_ref[...]
    s = c * jax.nn.sigmoid(c)
    o_ref[...] = jnp.dot(s, w_ref[...], preferred_element_type=F32,
                         precision=lax.Precision.HIGHEST) + b_ref[...]


def _mod_call(c_all, w_ada, b_ada):
    n = c_all.shape[0]
    tn = 512
    return pl.pallas_call(
        _mod_kernel,
        grid=(6 * D // tn,),
        in_specs=[pl.BlockSpec((n, D), lambda j: (0, 0)),
                  pl.BlockSpec((D, tn), lambda j: (0, j)),
                  pl.BlockSpec((1, tn), lambda j: (0, j))],
        out_specs=pl.BlockSpec((n, tn), lambda j: (0, j)),
        out_shape=jax.ShapeDtypeStruct((n, 6 * D), F32),
        compiler_params=pltpu.CompilerParams(dimension_semantics=("arbitrary",)),
        name="mod",
    )(c_all, w_ada, b_ada.reshape(1, 6 * D))


def _ssm_prep_kernel(a_re, a_im, log_dt, b_re, b_im, c_im,
                     ab_re_o, ab_im_o, bb_re_o, bb_im_o, cneg_o):
    lam_re = jnp.minimum(a_re[...], -1e-4)
    lam_im = a_im[...]
    dt = jnp.exp(log_dt[...])
    mag = jnp.exp(lam_re * dt)
    ab_re = mag * jnp.cos(lam_im * dt)
    ab_im = mag * jnp.sin(lam_im * dt)
    den = lam_re * lam_re + lam_im * lam_im
    num_re = ab_re - 1.0
    coef_re = (num_re * lam_re + ab_im * lam_im) / den
    coef_im = (ab_im * lam_re - num_re * lam_im) / den
    ab_re_o[...] = ab_re
    ab_im_o[...] = ab_im
    br = b_re[...]
    bi = b_im[...]
    bb_re_o[...] = coef_re * br - coef_im * bi
    bb_im_o[...] = coef_re * bi + coef_im * br
    cneg_o[...] = -c_im[...]


def _ssm_prep_call(a_re, a_im, log_dt, b_re, b_im, c_im):
    flat = lambda v: v.reshape(1, NSTATE)
    b_hs = lambda v: jnp.transpose(v, (2, 0, 1)).reshape(H, NSTATE)
    dt_row = jnp.broadcast_to(log_dt[:, None], (G, P)).reshape(1, NSTATE)
    ab_re, ab_im, bb_re, bb_im, cneg = pl.pallas_call(
        _ssm_prep_kernel,
        out_shape=(jax.ShapeDtypeStruct((1, NSTATE), F32), jax.ShapeDtypeStruct((1, NSTATE), F32),
                   jax.ShapeDtypeStruct((H, NSTATE), F32), jax.ShapeDtypeStruct((H, NSTATE), F32),
                   jax.ShapeDtypeStruct((G * H, P), F32)),
        name="ssm_prep",
    )(flat(a_re), flat(a_im), dt_row, b_hs(b_re), b_hs(b_im), c_im.reshape(G * H, P))
    ghp = lambda v: jnp.transpose(v.reshape(H, G, P), (1, 0, 2))
    return (ab_re.reshape(G, P), ab_im.reshape(G, P), ghp(bb_re), ghp(bb_im), cneg.reshape(G, H, P))


def _block_diag_weights(bb_re, bb_im, c_re, c_im_neg):
    eye8 = jnp.eye(8, dtype=F32)
    eye4 = jnp.eye(4, dtype=F32)
    eye2 = jnp.eye(2, dtype=F32)

    def wb_part(bb):
        x = bb.reshape(4, 8, H, P)
        return jnp.einsum('qghp,gk->qghkp', x, eye8).reshape(4, 8 * H, 8 * P)

    wb = jnp.concatenate([wb_part(bb_re), wb_part(bb_im)], axis=-1).astype(BF16)

    def wc_part(c):
        x = c.reshape(4, 4, 2, H, P)
        y = jnp.einsum('qjghp,jk,gl->qjgpklh', x, eye4, eye2)
        return y.reshape(NCHUNK, 2 * P, 4 * 2 * H)

    wc = jnp.concatenate([wc_part(c_re), wc_part(c_im_neg)], axis=1).astype(BF16)
    return wb, wc


def _front(x, sc1, sh1, g_mix, w_in_ref):
    n = _rms(x) * g_mix * (1.0 + sc1) + sh1
    proj = _bdot(n, w_in_ref[...])
    u = proj[:, :SSM_W]
    glu = proj[:, SSM_W:SSM_W + CONV_CH] * jax.nn.sigmoid(proj[:, SSM_W + CONV_CH:])
    return u, glu


def _ssm_out(y_lin, u, d_skip, w_glu_ref, b_glu, g_out_ssm):
    y = _gelu_tanh(y_lin + d_skip * u)
    ys = y * jax.nn.sigmoid(_bdot(y, w_glu_ref[...]) + b_glu)
    return _rms(ys) * g_out_ssm


def _conv_out(conv, b_dw, ln_g, ln_b, g_out_conv):
    c = conv + b_dw
    mu = jnp.mean(c, axis=-1, keepdims=True)
    cc = c - mu
    var = jnp.mean(cc * cc, axis=-1, keepdims=True)
    ln = cc * lax.rsqrt(var + EPS) * ln_g + ln_b
    yc = ln * jax.nn.sigmoid(ln)
    return _rms(yc) * g_out_conv


def _route(n2, w_r_ref, b_r):
    rows = n2.shape[0]
    lg = jnp.dot(n2, w_r_ref[...], preferred_element_type=F32,
                 precision=lax.Precision.HIGHEST) + b_r
    lane = lax.broadcasted_iota(jnp.int32, (rows, LANES), 1).astype(F32)
    ninf = -jnp.inf
    big = 1e9
    gmask = jnp.logical_and(lane >= NE, lane < NE + NG)
    gl = jnp.where(gmask, lg, ninf)
    gmax = jnp.max(gl, axis=-1, keepdims=True)
    gsum = jnp.sum(jnp.where(gmask, jnp.exp(gl - gmax), 0.0), axis=-1, keepdims=True)
    p_top = 1.0 / gsum
    gi = jnp.min(jnp.where(gl == gmax, lane, big), axis=-1, keepdims=True) - NE
    lo = gi * EPG
    emask = jnp.logical_and(lane >= lo, lane < lo + EPG)
    el = jnp.where(emask, lg, ninf)
    m1 = jnp.max(el, axis=-1, keepdims=True)
    i1 = jnp.min(jnp.where(el == m1, lane, big), axis=-1, keepdims=True)
    el2 = jnp.where(lane == i1, ninf, el)
    m2 = jnp.max(el2, axis=-1, keepdims=True)
    i2 = jnp.min(jnp.where(el2 == m2, lane, big), axis=-1, keepdims=True)
    e2 = jnp.exp(m2 - m1)
    den = 1.0 + e2
    w1 = p_top / den
    w2 = p_top * e2 / den
    cnt = jnp.sum(jnp.where(lane == i1, 1.0, 0.0) + jnp.where(lane == i2, 1.0, 0.0),
                  axis=0, keepdims=True)
    cols = (jnp.where(lane == 0.0, i1, 0.0) + jnp.where(lane == 1.0, i2, 0.0)
            + jnp.where(lane == 2.0, w1, 0.0) + jnp.where(lane == 3.0, w2, 0.0))
    return cols.T[0:8, :], cnt


def _tail(x, ns, nc, gt1, sc2, sh2, w_out_ref, g_ffn, w_r_ref, b_r):
    merged = _bdot(jnp.concatenate([ns, nc], axis=-1), w_out_ref[...])
    x1 = x + gt1 * merged
    n2 = _rms(x1) * g_ffn * (1.0 + sc2) + sh2
    rt, cnt = _route(n2, w_r_ref, b_r)
    return x1, n2, rt, cnt


def _prompt_mixer_kernel(x_ref, mod_ref, g_mix_ref, w_in_ref, wb_ref, a_ref, wc_ref, dsk_ref,
                         w_glu_ref, b_glu_ref, w_dw_ref, b_dw_ref, ln_g_ref, ln_b_ref,
                         g_os_ref, g_oc_ref, w_out_ref, g_ffn_ref, w_r_ref, b_r_ref,
                         x1_ref, n2_ref, rt_ref, cnt_ref, hr_ref, hi_ref, cache_ref,
                         sre, sim, hst, ebuf, *, tc, pt):
    c = pl.program_id(1)
    nc_chunks = pl.num_programs(1)

    @pl.when(c == 0)
    def _():
        hst[...] = jnp.zeros_like(hst)
        ebuf[pl.ds(0, 32), :] = jnp.zeros((32, CONV_CH), F32)

    x = x_ref[...]
    mod = mod_ref[...]
    sh1, sc1, gt1 = mod[0:1], mod[1:2], mod[2:3]
    sh2, sc2 = mod[3:4], mod[4:5]

    u, glu = _front(x, sc1, sh1, g_mix_ref[...], w_in_ref)
    ebuf[pl.ds(32, tc), :] = glu

    ub = u.astype(BF16)
    for q in range(4):
        r = jnp.dot(ub[:, q * LANES:(q + 1) * LANES], wb_ref[q], preferred_element_type=F32)
        for k in range(4):
            j = 4 * q + k
            sre[pl.ds(j * pt, tc), :] = r[:, k * LANES:(k + 1) * LANES]
            sim[pl.ds(j * pt, tc), :] = r[:, SSM_W + k * LANES:SSM_W + (k + 1) * LANES]

    ar0, ar1 = a_ref[0, 0:8, :], a_ref[0, 8:16, :]
    ai0, ai1 = a_ref[1, 0:8, :], a_ref[1, 8:16, :]

    def step(t, carry):
        hr0, hr1, hi0, hi1 = carry
        i0 = pl.ds(t, 8, stride=pt)
        i1 = pl.ds(t + 8 * pt, 8, stride=pt)
        nr0 = ar0 * hr0 - ai0 * hi0 + sre[i0, :]
        ni0 = ar0 * hi0 + ai0 * hr0 + sim[i0, :]
        nr1 = ar1 * hr1 - ai1 * hi1 + sre[i1, :]
        ni1 = ar1 * hi1 + ai1 * hr1 + sim[i1, :]
        sre[i0, :] = nr0
        sim[i0, :] = ni0
        sre[i1, :] = nr1
        sim[i1, :] = ni1
        return nr0, nr1, ni0, ni1

    init = (hst[0, 0:8, :], hst[0, 8:16, :], hst[1, 0:8, :], hst[1, 8:16, :])
    hr0, hr1, hi0, hi1 = lax.fori_loop(0, tc, step, init)
    hst[0, 0:8, :] = hr0
    hst[0, 8:16, :] = hr1
    hst[1, 0:8, :] = hi0
    hst[1, 8:16, :] = hi1

    @pl.when(c == nc_chunks - 1)
    def _():
        hr_ref[...] = hst[0]
        hi_ref[...] = hst[1]

    ys = []
    for q in range(4):
        acc = None
        for jj in range(4):
            j = 4 * q + jj
            lhs = jnp.concatenate([sre[pl.ds(j * pt, tc), :], sim[pl.ds(j * pt, tc), :]], axis=-1)
            d = jnp.dot(lhs.astype(BF16), wc_ref[j], preferred_element_type=F32)
            acc = d if acc is None else acc + d
        ys.append(acc)
    y_lin = jnp.concatenate(ys, axis=-1)
    ns = _ssm_out(y_lin, u, dsk_ref[...], w_glu_ref, b_glu_ref[...], g_os_ref[...])

    rb = 64
    convs = []
    for r0 in range(0, tc, rb):
        acc = None
        for k in range(KW):
            term = w_dw_ref[k:k + 1, :] * ebuf[pl.ds(r0 + k + 2, rb), :]
            acc = term if acc is None else acc + term
        convs.append(acc)
    conv = jnp.concatenate(convs, axis=0)
    nc = _conv_out(conv, b_dw_ref[...], ln_g_ref[...], ln_b_ref[...], g_oc_ref[...])

    @pl.when(c == nc_chunks - 1)
    def _():
        cache_ref[...] = ebuf[pl.ds(tc + 2, CB), :]

    ebuf[pl.ds(0, 32), :] = ebuf[pl.ds(tc, 32), :]

    x1, n2, rt, cnt = _tail(x, ns, nc, gt1, sc2, sh2, w_out_ref, g_ffn_ref[...], w_r_ref, b_r_ref[...])
    x1_ref[...] = x1
    n2_ref[...] = n2.astype(BF16)
    rt_ref[...] = rt
    cnt_ref[...] = cnt


def _const_spec(shape):
    nd = len(shape)
    return pl.BlockSpec(shape, lambda b, c: (0,) * nd)


def _prompt_mixer_call(x, mod6, wts, tc):
    bsz, t, _ = x.shape
    n_all = bsz * t
    pt = tc + 8
    nc = t // tc
    kern = functools.partial(_prompt_mixer_kernel, tc=tc, pt=pt)
    in_specs = [pl.BlockSpec((None, tc, D), lambda b, c: (b, c, 0)),
                pl.BlockSpec((None, 6, D), lambda b, c: (b, 0, 0))]
    in_specs += [_const_spec(w.shape) for w in wts]
    out_shape = (jax.ShapeDtypeStruct((n_all, D), F32),
                 jax.ShapeDtypeStruct((n_all, D), BF16),
                 jax.ShapeDtypeStruct((8, n_all), F32),
                 jax.ShapeDtypeStruct((n_all // tc, 1, LANES), F32),
                 jax.ShapeDtypeStruct((bsz, NCHUNK, LANES), F32),
                 jax.ShapeDtypeStruct((bsz, NCHUNK, LANES), F32),
                 jax.ShapeDtypeStruct((bsz, CB, CONV_CH), F32))
    out_specs = (pl.BlockSpec((tc, D), lambda b, c: (b * nc + c, 0)),
                 pl.BlockSpec((tc, D), lambda b, c: (b * nc + c, 0)),
                 pl.BlockSpec((8, tc), lambda b, c: (0, b * nc + c)),
                 pl.BlockSpec((None, 1, LANES), lambda b, c: (b * nc + c, 0, 0)),
                 pl.BlockSpec((None, NCHUNK, LANES), lambda b, c: (b, 0, 0)),
                 pl.BlockSpec((None, NCHUNK, LANES), lambda b, c: (b, 0, 0)),
                 pl.BlockSpec((None, CB, CONV_CH), lambda b, c: (b, 0, 0)))
    scratch = [pltpu.VMEM((NCHUNK * pt, LANES), F32),
               pltpu.VMEM((NCHUNK * pt, LANES), F32),
               pltpu.VMEM((2, NCHUNK, LANES), F32),
               pltpu.VMEM((tc + 32, CONV_CH), F32)]
    return pl.pallas_call(
        kern, grid=(bsz, nc), in_specs=in_specs, out_specs=out_specs, out_shape=out_shape,
        scratch_shapes=scratch,
        compiler_params=pltpu.CompilerParams(dimension_semantics=("arbitrary", "arbitrary"),
                                             vmem_limit_bytes=VMEM_LIMIT),
        name="prompt_mixer",
    )(x, mod6, *wts)


def _sample_mixer_kernel(x_ref, mod_ref, h0r_ref, h0i_ref, cache_ref,
                         g_mix_ref, w_in_ref, wb_ref, a_ref, wc_ref, dsk_ref,
                         w_glu_ref, b_glu_ref, w_dw_ref, b_dw_ref, ln_g_ref, ln_b_ref,
                         g_os_ref, g_oc_ref, w_out_ref, g_ffn_ref, w_r_ref, b_r_ref,
                         x1_ref, n2_ref, rt_ref, cnt_ref, hr_ref, hi_ref, glu_ref,
                         sre, sim, *, nb, nt):
    x = x_ref[...]

    def rows(i):
        m = mod_ref[:, i * D:(i + 1) * D]
        return jnp.concatenate([m] * nt, axis=0)

    sh1, sc1, gt1, sh2, sc2 = rows(0), rows(1), rows(2), rows(3), rows(4)
    u, glu = _front(x, sc1, sh1, g_mix_ref[...], w_in_ref)
    glu_ref[...] = glu

    ub = u.astype(BF16)
    for q in range(4):
        r = jnp.dot(ub[:, q * LANES:(q + 1) * LANES], wb_ref[q], preferred_element_type=F32)
        sre[:, q * SSM_W:(q + 1) * SSM_W] = r[:, :SSM_W]
        sim[:, q * SSM_W:(q + 1) * SSM_W] = r[:, SSM_W:]

    ar = a_ref[0:1, :]
    ai = a_ref[1:2, :]
    hr = h0r_ref[...]
    hi = h0i_ref[...]
    for t in range(nt):
        rs = pl.ds(t * nb, nb)
        nr = ar * hr - ai * hi + sre[rs, :]
        ni = ar * hi + ai * hr + sim[rs, :]
        sre[rs, :] = nr
        sim[rs, :] = ni
        hr, hi = nr, ni
    hr_ref[...] = hr
    hi_ref[...] = hi

    ys = []
    for q in range(4):
        acc = None
        for jj in range(4):
            j = 4 * q + jj
            lhs = jnp.concatenate([sre[:, j * LANES:(j + 1) * LANES],
                                   sim[:, j * LANES:(j + 1) * LANES]], axis=-1)
            d = jnp.dot(lhs.astype(BF16), wc_ref[j], preferred_element_type=F32)
            acc = d if acc is None else acc + d
        ys.append(acc)
    y_lin = jnp.concatenate(ys, axis=-1)
    ns = _ssm_out(y_lin, u, dsk_ref[...], w_glu_ref, b_glu_ref[...], g_os_ref[...])

    def ext(jrow):
        if jrow < CB:
            return cache_ref[jrow]
        return glu[(jrow - CB) * nb:(jrow - CB + 1) * nb, :]

    convs = []
    for t in range(nt):
        acc = None
        for k in range(KW):
            term = w_dw_ref[k:k + 1, :] * ext(t + k)
            acc = term if acc is None else acc + term
        convs.append(acc)
    conv = jnp.concatenate(convs, axis=0)
    nc = _conv_out(conv, b_dw_ref[...], ln_g_ref[...], ln_b_ref[...], g_oc_ref[...])

    x1, n2, rt, cnt = _tail(x, ns, nc, gt1, sc2, sh2, w_out_ref, g_ffn_ref[...], w_r_ref, b_r_ref[...])
    x1_ref[...] = x1
    n2_ref[...] = n2.astype(BF16)
    rt_ref[...] = rt
    cnt_ref[...] = cnt


def _sample_mixer_call(x_tm, mod_s, h0r, h0i, cache_tm, wts, nb, nt):
    n = nb * nt
    kern = functools.partial(_sample_mixer_kernel, nb=nb, nt=nt)
    out_shape = (jax.ShapeDtypeStruct((n, D), F32),
                 jax.ShapeDtypeStruct((n, D), BF16),
                 jax.ShapeDtypeStruct((8, n), F32),
                 jax.ShapeDtypeStruct((1, LANES), F32),
                 jax.ShapeDtypeStruct((nb, NSTATE), F32),
                 jax.ShapeDtypeStruct((nb, NSTATE), F32),
                 jax.ShapeDtypeStruct((n, CONV_CH), F32))
    scratch = [pltpu.VMEM((n, NSTATE), F32), pltpu.VMEM((n, NSTATE), F32)]
    return pl.pallas_call(
        kern, out_shape=out_shape, scratch_shapes=scratch,
        compiler_params=pltpu.CompilerParams(vmem_limit_bytes=VMEM_LIMIT),
        name="sample_mixer",
    )(x_tm, mod_s, h0r, h0i, cache_tm, *wts)


ROW_ALIGN = 16
MOE_TD = 512
MOE_BR = MOE_TD * 2 + NE * ROW_ALIGN
MOE_TG = 256


def _slot_positions(rt):
    t = rt.shape[1]
    e0 = rt[0:1, :]
    e1 = rt[1:2, :]
    sub = lax.broadcasted_iota(jnp.int32, (LANES, t), 0).astype(F32)
    a0 = jnp.where(sub == e0, 1.0, 0.0)
    a1 = jnp.where(sub == e1, 1.0, 0.0)
    at = a0 + a1
    r = lax.broadcasted_iota(jnp.int32, (t, t), 0)
    c = lax.broadcasted_iota(jnp.int32, (t, t), 1)
    before = jnp.where(r < c, 1.0, 0.0).astype(BF16)
    rank = jnp.dot(at.astype(BF16), before, preferred_element_type=F32)
    cnt = jnp.sum(at, axis=1, keepdims=True)
    cnt_al = jnp.ceil(cnt * (1.0 / ROW_ALIGN)) * float(ROW_ALIGN)
    er = lax.broadcasted_iota(jnp.int32, (LANES, LANES), 0)
    ec = lax.broadcasted_iota(jnp.int32, (LANES, LANES), 1)
    lower = jnp.where(ec < er, 1.0, 0.0)
    base = jnp.dot(lower, jnp.broadcast_to(cnt_al, (LANES, LANES)), preferred_element_type=F32,
                   precision=lax.Precision.HIGHEST)[:, 0:1]
    slot = rank + base
    pos0 = jnp.sum(a0 * slot, axis=0, keepdims=True)
    pos1 = jnp.sum(a1 * slot, axis=0, keepdims=True)
    return pos0, pos1


def _segment_copies(tab_ref, t, n_tiles, buf, hbm, sems, to_hbm):
    def copy(e):
        n = pl.multiple_of(tab_ref[t * NE + e], ROW_ALIGN)
        b = pl.multiple_of(tab_ref[(n_tiles + t) * NE + e], ROW_ALIGN)
        d = pl.multiple_of(tab_ref[(2 * n_tiles + t) * NE + e], ROW_ALIGN)
        if to_hbm:
            cp = pltpu.make_async_copy(buf.at[pl.ds(b, n)], hbm.at[pl.ds(d, n)], sems.at[e])
        else:
            cp = pltpu.make_async_copy(hbm.at[pl.ds(d, n)], buf.at[pl.ds(b, n)], sems.at[e])
        return n, cp

    for e in range(NE):
        n, cp = copy(e)

        @pl.when(n > 0)
        def _():
            cp.start()

    for e in range(NE):
        n, cp = copy(e)

        @pl.when(n > 0)
        def _():
            cp.wait()


def _dispatch_kernel(tab_ref, rtp_ref, n2p_ref, rts_ref, n2s_ref, xs_ref, buf, zbuf, sems, zsem,
                     *, n_tiles, n_ptiles, n_gtiles):
    t = pl.program_id(0)

    @pl.when(t == 0)
    def _():
        zbuf[...] = jnp.zeros_like(zbuf)
        for phase in range(2):
            for e in range(NE):
                d = pl.multiple_of(tab_ref[3 * n_tiles * NE + e], ROW_ALIGN)
                n = pl.multiple_of(tab_ref[3 * n_tiles * NE + NE + e], ROW_ALIGN)
                cp = pltpu.make_async_copy(zbuf.at[pl.ds(0, n)], xs_ref.at[pl.ds(d, n)], sems.at[e])

                @pl.when(n > 0)
                def _():
                    if phase == 0:
                        cp.start()
                    else:
                        cp.wait()

        first_free = tab_ref[3 * n_tiles * NE + 2 * NE]

        def fill_copy(j):
            d = pl.multiple_of(j * MOE_TG, MOE_TG)
            return pltpu.make_async_copy(zbuf, xs_ref.at[pl.ds(d, MOE_TG)], zsem)

        def fill_start(j, carry):
            fill_copy(j).start()
            return carry

        def fill_wait(j, carry):
            fill_copy(j).wait()
            return carry

        lax.fori_loop(first_free, n_gtiles, fill_start, 0)
        lax.fori_loop(first_free, n_gtiles, fill_wait, 0)

    is_sample = t >= n_ptiles
    rt = jnp.where(is_sample, rts_ref[...], rtp_ref[...])
    n2 = jnp.where(is_sample, n2s_ref[...], n2p_ref[...])
    pos0, pos1 = _slot_positions(rt)
    row = lax.broadcasted_iota(jnp.int32, (MOE_BR, MOE_TD), 0).astype(F32)
    q = (jnp.where(row == pos0, 1.0, 0.0) + jnp.where(row == pos1, 1.0, 0.0)).astype(BF16)
    buf[...] = jnp.dot(q, n2, preferred_element_type=F32).astype(BF16)
    _segment_copies(tab_ref, t, n_tiles, buf, xs_ref, sems, to_hbm=True)


def _dispatch_call(tab, rt_p, n2_p, rt_s, n2_s, r_max):
    n_ptiles = n2_p.shape[0] // MOE_TD
    n_tiles = n_ptiles + n2_s.shape[0] // MOE_TD
    last_p = n_ptiles - 1
    return pl.pallas_call(
        functools.partial(_dispatch_kernel, n_tiles=n_tiles, n_ptiles=n_ptiles,
                          n_gtiles=r_max // MOE_TG),
        grid_spec=pltpu.PrefetchScalarGridSpec(
            num_scalar_prefetch=1, grid=(n_tiles,),
            in_specs=[pl.BlockSpec((8, MOE_TD), lambda t, tab: (0, jnp.minimum(t, last_p))),
                      pl.BlockSpec((MOE_TD, D), lambda t, tab: (jnp.minimum(t, last_p), 0)),
                      pl.BlockSpec((8, MOE_TD), lambda t, tab: (0, 0)),
                      pl.BlockSpec((MOE_TD, D), lambda t, tab: (0, 0))],
            out_specs=pl.BlockSpec(memory_space=pl.ANY),
            scratch_shapes=[pltpu.VMEM((MOE_BR, D), BF16),
                            pltpu.VMEM((MOE_TG, D), BF16),
                            pltpu.SemaphoreType.DMA((NE,)),
                            pltpu.SemaphoreType.DMA(())]),
        out_shape=jax.ShapeDtypeStruct((r_max, D), BF16),
        compiler_params=pltpu.CompilerParams(dimension_semantics=("arbitrary",),
                                             vmem_limit_bytes=VMEM_LIMIT),
        name="moe_dispatch",
    )(tab, rt_p, n2_p, rt_s, n2_s)


def _experts_kernel(te_ref, first_ref, nact_ref, xs_ref, w1_ref, w3_ref, w2_ref, ys_ref,
                    w1b, w3b, w2b):
    j = pl.program_id(0)

    @pl.when(j < nact_ref[0])
    def _():
        @pl.when(first_ref[j] == 1)
        def _():
            w1b[...] = w1_ref[...].astype(BF16)
            w3b[...] = w3_ref[...].astype(BF16)
            w2b[...] = w2_ref[...].astype(BF16)

        x = xs_ref[...]
        a = jnp.dot(x, w1b[...], preferred_element_type=F32)
        b = jnp.dot(x, w3b[...], preferred_element_type=F32)
        hid = a * jax.nn.sigmoid(a) * b
        y = jnp.dot(hid.astype(BF16), w2b[...], preferred_element_type=F32)
        ys_ref[...] = y.astype(BF16)

    @pl.when(j >= nact_ref[0])
    def _():
        ys_ref[...] = jnp.zeros_like(ys_ref)


def _experts_call(tile_expert, tile_first, n_active, xs, w1, w3, w2):
    r_max = xs.shape[0]
    n_steps = r_max // MOE_TG

    def row_map(j, te, first, nact):
        return (j, 0)

    def w_map(j, te, first, nact):
        return (te[j], 0, 0)

    return pl.pallas_call(
        _experts_kernel,
        grid_spec=pltpu.PrefetchScalarGridSpec(
            num_scalar_prefetch=3, grid=(n_steps,),
            in_specs=[pl.BlockSpec((MOE_TG, D), row_map),
                      pl.BlockSpec((None, D, DE), w_map),
                      pl.BlockSpec((None, D, DE), w_map),
                      pl.BlockSpec((None, DE, D), w_map)],
            out_specs=pl.BlockSpec((MOE_TG, D), row_map),
            scratch_shapes=[pltpu.VMEM((D, DE), BF16), pltpu.VMEM((D, DE), BF16),
                            pltpu.VMEM((DE, D), BF16)]),
        out_shape=jax.ShapeDtypeStruct((r_max, D), BF16),
        compiler_params=pltpu.CompilerParams(dimension_semantics=("arbitrary",),
                                             vmem_limit_bytes=VMEM_LIMIT),
        name="moe_experts",
    )(tile_expert, tile_first, n_active, xs, w1, w3, w2)


def _combine_kernel(tab_ref, rt_ref, x1_ref, gt2_ref, gf_ref, ys_ref, y_ref, buf, sems,
                    *, n_tiles, t_off):
    t = pl.program_id(0) + t_off

    @pl.when(pl.program_id(0) == 0)
    def _():
        buf[...] = jnp.zeros_like(buf)

    _segment_copies(tab_ref, t, n_tiles, buf, ys_ref, sems, to_hbm=False)

    rt = rt_ref[...]
    pos0, pos1 = _slot_positions(rt)
    row = lax.broadcasted_iota(jnp.int32, (MOE_BR, MOE_TD), 0).astype(F32)
    m0 = row == pos0
    m1 = row == pos1
    q = (jnp.where(m0, 1.0, 0.0) + jnp.where(m1, 1.0, 0.0)).astype(BF16)
    gw = jnp.sum(jnp.where(m0, rt[2:3, :], 0.0) + jnp.where(m1, rt[3:4, :], 0.0),
                 axis=1, keepdims=True)
    yv = (buf[...].astype(F32) * gw).astype(BF16)
    moe = lax.dot_general(q, yv, (((0,), (0,)), ((), ())), preferred_element_type=F32)
    xo = x1_ref[...] + gt2_ref[...] * moe
    y_ref[...] = _rms(xo) * gf_ref[...]


def _combine_call(tab, rt, x1, gt2, gt2_spec, g_final, ys, n_tiles, t_off):
    n_out_tiles = x1.shape[0] // MOE_TD
    return pl.pallas_call(
        functools.partial(_combine_kernel, n_tiles=n_tiles, t_off=t_off),
        grid_spec=pltpu.PrefetchScalarGridSpec(
            num_scalar_prefetch=1, grid=(n_out_tiles,),
            in_specs=[pl.BlockSpec((8, MOE_TD), lambda t, tab: (0, t)),
                      pl.BlockSpec((MOE_TD, D), lambda t, tab: (t, 0)),
                      gt2_spec,
                      pl.BlockSpec((1, D), lambda t, tab: (0, 0)),
                      pl.BlockSpec(memory_space=pl.ANY)],
            out_specs=pl.BlockSpec((MOE_TD, D), lambda t, tab: (t, 0)),
            scratch_shapes=[pltpu.VMEM((MOE_BR, D), BF16),
                            pltpu.SemaphoreType.DMA((NE,))]),
        out_shape=jax.ShapeDtypeStruct((n_out_tiles * MOE_TD, D), F32),
        compiler_params=pltpu.CompilerParams(dimension_semantics=("arbitrary",),
                                             vmem_limit_bytes=VMEM_LIMIT),
        name="moe_combine",
    )(tab, rt, x1, gt2, g_final.reshape(1, D), ys)


def _moe_plan(cnt, r_max):
    cnt8 = (cnt + ROW_ALIGN - 1) // ROW_ALIGN * ROW_ALIGN
    seg_rows = cnt8.sum(axis=0)
    seg_pad = (seg_rows + MOE_TG - 1) // MOE_TG * MOE_TG
    seg_start = jnp.cumsum(seg_pad) - seg_pad
    dst = seg_start[None, :] + jnp.cumsum(cnt8, axis=0) - cnt8
    boff = jnp.cumsum(cnt8, axis=1) - cnt8
    tile_end = jnp.cumsum(seg_pad // MOE_TG)
    n_active = tile_end[-1:].astype(jnp.int32)
    tab = jnp.concatenate([cnt8.ravel(), boff.ravel(), dst.ravel(),
                           seg_start + seg_rows, seg_pad - seg_rows, n_active]).astype(jnp.int32)
    j = jnp.arange(r_max // MOE_TG, dtype=jnp.int32)
    tile_expert = jnp.minimum(jnp.sum(j[:, None] >= tile_end[None, :], axis=1), NE - 1).astype(jnp.int32)
    tile_first = jnp.concatenate([jnp.ones((1,), jnp.int32),
                                  (tile_expert[1:] != tile_expert[:-1]).astype(jnp.int32)])
    return tab, tile_expert, tile_first, n_active


def kernel(x_prompt, x_sample, c_prompt, c_sample, state_ssm_re, state_ssm_im, cache_conv, w_ada, b_ada, g_norm_mix, w_in, ssm_a_re, ssm_a_im, ssm_log_dt, ssm_b_re, ssm_b_im, ssm_c_re, ssm_c_im, ssm_d, w_ssm_glu, b_ssm_glu, w_dw, b_dw, ln_conv_g, ln_conv_b, g_out_ssm, g_out_conv, w_out, g_norm_ffn, w_router_grp, b_router_grp, w_router_exp, b_router_exp, w_exp_gate, w_exp_up, w_exp_down, g_final):
    depth = w_ada.shape[0]
    assert depth == 1
    bsz, seq, _ = x_prompt.shape
    nb, nt, _ = x_sample.shape

    c_all = jnp.concatenate([c_prompt, c_sample], axis=0)
    mod = _mod_call(c_all, w_ada[0], b_ada[0])
    mod_p = mod[:bsz].reshape(bsz, 6, D)
    mod_s = mod[bsz:]

    ab_re, ab_im, bb_re, bb_im, c_im_neg = _ssm_prep_call(
        ssm_a_re[0], ssm_a_im[0], ssm_log_dt[0], ssm_b_re[0], ssm_b_im[0], ssm_c_im[0])
    wb, wc = _block_diag_weights(bb_re, bb_im, ssm_c_re[0], c_im_neg)
    a_tok = jnp.stack([ab_re.reshape(NCHUNK, LANES), ab_im.reshape(NCHUNK, LANES)])
    a_row = jnp.stack([ab_re.reshape(NSTATE), ab_im.reshape(NSTATE)])

    w_r = jnp.concatenate([w_router_exp[0].reshape(D, NE), w_router_grp[0],
                           jnp.zeros((D, LANES - NE - NG), F32)], axis=1)
    b_r = jnp.concatenate([b_router_exp[0].reshape(NE), b_router_grp[0],
                           jnp.zeros((LANES - NE - NG,), F32)]).reshape(1, LANES)
    w_dw_p = jnp.concatenate([w_dw[0], jnp.zeros((1, CONV_CH), F32)], axis=0)

    row = lambda v: v.reshape(1, -1)
    common_a = (row(g_norm_mix[0]), w_in[0].astype(BF16), wb)
    common_b = (wc, row(ssm_d[0].reshape(SSM_W)), w_ssm_glu[0].astype(BF16), row(b_ssm_glu[0]),
                w_dw_p, row(b_dw[0]), row(ln_conv_g[0]), row(ln_conv_b[0]),
                row(g_out_ssm[0]), row(g_out_conv[0]), w_out[0].astype(BF16),
                row(g_norm_ffn[0]), w_r, b_r)

    n_p = bsz * seq
    n_s = nb * nt
    n_all = n_p + n_s
    assert n_s == MOE_TD and seq % MOE_TD == 0 and MOE_TD % PROMPT_TC == 0
    wts_p = common_a + (a_tok,) + common_b
    x1_p, n2_p, rt_p, cnt_p, hr_p, hi_p, cache_p = _prompt_mixer_call(x_prompt, mod_p, wts_p, PROMPT_TC)

    x_tm = jnp.transpose(x_sample, (1, 0, 2)).reshape(nt * nb, D)
    cache_tm = jnp.transpose(cache_conv[0], (1, 0, 2))
    wts_s = common_a + (a_row,) + common_b
    x1_s, n2_s, rt_s, cnt_s, hr_s, hi_s, glu_s = _sample_mixer_call(
        x_tm, mod_s, state_ssm_re[0].reshape(nb, NSTATE), state_ssm_im[0].reshape(nb, NSTATE),
        cache_tm, wts_s, nb, nt)

    n_ptiles = n_p // MOE_TD
    n_tiles = n_all // MOE_TD
    r_max = -(-(2 * n_all + n_tiles * NE * (ROW_ALIGN - 1) + NE * (MOE_TG - ROW_ALIGN)) // MOE_TG) * MOE_TG
    cnt = jnp.concatenate([cnt_p.reshape(n_ptiles, MOE_TD // PROMPT_TC, LANES).sum(axis=1), cnt_s])
    tab, tile_expert, tile_first, n_active = _moe_plan(cnt[:, :NE].astype(jnp.int32), r_max)
    xs = _dispatch_call(tab, rt_p, n2_p, rt_s, n2_s, r_max)
    ys = _experts_call(tile_expert, tile_first, n_active, xs,
                       w_exp_gate[0], w_exp_up[0], w_exp_down[0])
    tiles_per_b = seq // MOE_TD
    gt2_p = mod_p[:, 5:6, :]
    y_p = _combine_call(tab, rt_p, x1_p, gt2_p,
                        pl.BlockSpec((None, 1, D), lambda t, tab: (t // tiles_per_b, 0, 0)),
                        g_final, ys, n_tiles, 0)
    gt2_s = jnp.tile(mod_s[:, 5 * D:], (nt, 1))
    y_s = _combine_call(tab, rt_s, x1_s, gt2_s,
                        pl.BlockSpec((n_s, D), lambda t, tab: (0, 0)),
                        g_final, ys, n_tiles, n_ptiles)

    y_prompt = y_p.reshape(bsz, seq, D)
    y_sample = jnp.transpose(y_s.reshape(nt, nb, D), (1, 0, 2))
    new_cache_s = jnp.concatenate(
        [cache_conv[0][:, nt:, :], jnp.transpose(glu_s.reshape(nt, nb, CONV_CH), (1, 0, 2))], axis=1)
    return (y_prompt, y_sample,
            hr_p.reshape(1, bsz, G, P), hi_p.reshape(1, bsz, G, P), cache_p[None],
            hr_s.reshape(1, nb, G, P), hi_s.reshape(1, nb, G, P), new_cache_s[None])
```

```python
import functools

import jax
import jax.numpy as jnp
import numpy as np
from jax import lax
from jax.experimental import pallas as pl
from jax.experimental.pallas import tpu as pltpu

F32 = jnp.float32
BF16 = jnp.bfloat16

D = 1024
SSM_W = 512
CONV_CH = 512
G = 32
H = 16
P = 64
KW = 31
CB = KW - 1
NE = 32
NG = 4
EPG = 8
DE = 512
EPS = 1e-6
LANES = 128
NSTATE = G * P
NCHUNK = NSTATE // LANES

PROMPT_TC = 256
VMEM_LIMIT = 56 * 1024 * 1024


def _rms(x):
    return x * lax.rsqrt(jnp.mean(x * x, axis=-1, keepdims=True) + EPS)


def _gelu_tanh(y):
    c = np.sqrt(2.0 / np.pi).astype(np.float32)
    return y * (0.5 * (1.0 + jnp.tanh(c * (y + 0.044715 * (y * y * y)))))


def _bdot(a, b):
    return jnp.dot(a.astype(BF16), b, preferred_element_type=F32)


def _mod_kernel(c_ref, w_ref, b_ref, o_ref):
    c = c_ref[...]
    s = c * jax.nn.sigmoid(c)
    o_ref[...] = jnp.dot(s, w_ref[...], preferred_element_type=F32,
                         precision=lax.Precision.HIGHEST) + b_ref[...]


def _mod_call(c_all, w_ada, b_ada):
    n = c_all.shape[0]
    tn = 512
    return pl.pallas_call(
        _mod_kernel,
        grid=(6 * D // tn,),
        in_specs=[pl.BlockSpec((n, D), lambda j: (0, 0)),
                  pl.BlockSpec((D, tn), lambda j: (0, j)),
                  pl.BlockSpec((1, tn), lambda j: (0, j))],
        out_specs=pl.BlockSpec((n, tn), lambda j: (0, j)),
        out_shape=jax.ShapeDtypeStruct((n, 6 * D), F32),
        compiler_params=pltpu.CompilerParams(dimension_semantics=("arbitrary",)),
        name="mod",
    )(c_all, w_ada, b_ada.reshape(1, 6 * D))


def _ssm_prep_kernel(a_re, a_im, log_dt, b_re, b_im, c_im,
                     ab_re_o, ab_im_o, bb_re_o, bb_im_o, cneg_o):
    lam_re = jnp.minimum(a_re[...], -1e-4)
    lam_im = a_im[...]
    dt = jnp.exp(log_dt[...])
    mag = jnp.exp(lam_re * dt)
    ab_re = mag * jnp.cos(lam_im * dt)
    ab_im = mag * jnp.sin(lam_im * dt)
    den = lam_re * lam_re + lam_im * lam_im
    num_re = ab_re - 1.0
    coef_re = (num_re * lam_re + ab_im * lam_im) / den
    coef_im = (ab_im * lam_re - num_re * lam_im) / den
    ab_re_o[...] = ab_re
    ab_im_o[...] = ab_im
    br = b_re[...]
    bi = b_im[...]
    bb_re_o[...] = coef_re * br - coef_im * bi
    bb_im_o[...] = coef_re * bi + coef_im * br
    cneg_o[...] = -c_im[...]


def _ssm_prep_call(a_re, a_im, log_dt, b_re, b_im, c_im):
    flat = lambda v: v.reshape(1, NSTATE)
    b_hs = lambda v: jnp.transpose(v, (2, 0, 1)).reshape(H, NSTATE)
    dt_row = jnp.broadcast_to(log_dt[:, None], (G, P)).reshape(1, NSTATE)
    ab_re, ab_im, bb_re, bb_im, cneg = pl.pallas_call(
        _ssm_prep_kernel,
        out_shape=(jax.ShapeDtypeStruct((1, NSTATE), F32), jax.ShapeDtypeStruct((1, NSTATE), F32),
                   jax.ShapeDtypeStruct((H, NSTATE), F32), jax.ShapeDtypeStruct((H, NSTATE), F32),
                   jax.ShapeDtypeStruct((G * H, P), F32)),
        name="ssm_prep",
    )(flat(a_re), flat(a_im), dt_row, b_hs(b_re), b_hs(b_im), c_im.reshape(G * H, P))
    ghp = lambda v: jnp.transpose(v.reshape(H, G, P), (1, 0, 2))
    return (ab_re.reshape(G, P), ab_im.reshape(G, P), ghp(bb_re), ghp(bb_im), cneg.reshape(G, H, P))


def _block_diag_weights(bb_re, bb_im, c_re, c_im_neg):
    eye8 = jnp.eye(8, dtype=F32)
    eye4 = jnp.eye(4, dtype=F32)
    eye2 = jnp.eye(2, dtype=F32)

    def wb_part(bb):
        x = bb.reshape(4, 8, H, P)
        return jnp.einsum('qghp,gk->qghkp', x, eye8).reshape(4, 8 * H, 8 * P)

    wb = jnp.concatenate([wb_part(bb_re), wb_part(bb_im)], axis=-1).astype(BF16)

    def wc_part(c):
        x = c.reshape(4, 4, 2, H, P)
        y = jnp.einsum('qjghp,jk,gl->qjgpklh', x, eye4, eye2)
        return y.reshape(NCHUNK, 2 * P, 4 * 2 * H)

    wc = jnp.concatenate([wc_part(c_re), wc_part(c_im_neg)], axis=1).astype(BF16)
    return wb, wc


def _front(x, sc1, sh1, g_mix, w_in_ref):
    n = _rms(x) * g_mix * (1.0 + sc1) + sh1
    proj = _bdot(n, w_in_ref[...])
    u = proj[:, :SSM_W]
    glu = proj[:, SSM_W:SSM_W + CONV_CH] * jax.nn.sigmoid(proj[:, SSM_W + CONV_CH:])
    return u, glu


def _ssm_out(y_lin, u, d_skip, w_glu_ref, b_glu, g_out_ssm):
    y = _gelu_tanh(y_lin + d_skip * u)
    ys = y * jax.nn.sigmoid(_bdot(y, w_glu_ref[...]) + b_glu)
    return _rms(ys) * g_out_ssm


def _conv_out(conv, b_dw, ln_g, ln_b, g_out_conv):
    c = conv + b_dw
    mu = jnp.mean(c, axis=-1, keepdims=True)
    cc = c - mu
    var = jnp.mean(cc * cc, axis=-1, keepdims=True)
    ln = cc * lax.rsqrt(var + EPS) * ln_g + ln_b
    yc = ln * jax.nn.sigmoid(ln)
    return _rms(yc) * g_out_conv


def _route(n2, w_r_ref, b_r):
    rows = n2.shape[0]
    n_hi = n2.astype(BF16)
    n_lo = (n2 - n_hi.astype(F32)).astype(BF16)
    parts = jnp.dot(jnp.concatenate([n_hi, n_lo], axis=0), w_r_ref[...], preferred_element_type=F32)
    lg = (parts[:rows, :LANES] + parts[:rows, LANES:]) + (parts[rows:, :LANES] + parts[rows:, LANES:]) + b_r
    lane = lax.broadcasted_iota(jnp.int32, (rows, LANES), 1).astype(F32)
    ninf = -jnp.inf
    big = 1e9
    gmask = jnp.logical_and(lane >= NE, lane < NE + NG)
    gl = jnp.where(gmask, lg, ninf)
    gmax = jnp.max(gl, axis=-1, keepdims=True)
    gsum = jnp.sum(jnp.where(gmask, jnp.exp(gl - gmax), 0.0), axis=-1, keepdims=True)
    p_top = 1.0 / gsum
    gi = jnp.min(jnp.where(gl == gmax, lane, big), axis=-1, keepdims=True) - NE
    lo = gi * EPG
    emask = jnp.logical_and(lane >= lo, lane < lo + EPG)
    el = jnp.where(emask, lg, ninf)
    m1 = jnp.max(el, axis=-1, keepdims=True)
    i1 = jnp.min(jnp.where(el == m1, lane, big), axis=-1, keepdims=True)
    el2 = jnp.where(lane == i1, ninf, el)
    m2 = jnp.max(el2, axis=-1, keepdims=True)
    i2 = jnp.min(jnp.where(el2 == m2, lane, big), axis=-1, keepdims=True)
    e2 = jnp.exp(m2 - m1)
    den = 1.0 + e2
    w1 = p_top / den
    w2 = p_top * e2 / den
    cnt = jnp.sum(jnp.where(lane == i1, 1.0, 0.0) + jnp.where(lane == i2, 1.0, 0.0),
                  axis=0, keepdims=True)
    cols = (jnp.where(lane == 0.0, i1, 0.0) + jnp.where(lane == 1.0, i2, 0.0)
            + jnp.where(lane == 2.0, w1, 0.0) + jnp.where(lane == 3.0, w2, 0.0))
    return cols.T[0:8, :], cnt


def _tail(x, ns, nc, gt1, sc2, sh2, w_out_ref, g_ffn, w_r_ref, b_r):
    merged = _bdot(jnp.concatenate([ns, nc], axis=-1), w_out_ref[...])
    x1 = x + gt1 * merged
    n2 = _rms(x1) * g_ffn * (1.0 + sc2) + sh2
    rt, cnt = _route(n2, w_r_ref, b_r)
    return x1, n2, rt, cnt


def _prompt_mixer_kernel(x_ref, mod_ref, g_mix_ref, w_in_ref, wb_ref, a_ref, wc_ref, dsk_ref,
                         w_glu_ref, b_glu_ref, w_dw_ref, b_dw_ref, ln_g_ref, ln_b_ref,
                         g_os_ref, g_oc_ref, w_out_ref, g_ffn_ref, w_r_ref, b_r_ref,
                         x1_ref, n2_ref, rt_ref, cnt_ref, hr_ref, hi_ref, cache_ref,
                         sre, sim, hst, ebuf, eshift, *, tc, pt):
    c = pl.program_id(1)
    nc_chunks = pl.num_programs(1)

    @pl.when(c == 0)
    def _():
        hst[...] = jnp.zeros_like(hst)
        ebuf[pl.ds(0, 32), :] = jnp.zeros((32, CONV_CH), F32)

    x = x_ref[...]
    mod = mod_ref[...]
    sh1, sc1, gt1 = mod[0:1], mod[1:2], mod[2:3]
    sh2, sc2 = mod[3:4], mod[4:5]

    u, glu = _front(x, sc1, sh1, g_mix_ref[...], w_in_ref)
    ebuf[pl.ds(32, tc), :] = glu

    ub = u.astype(BF16)
    for q in range(4):
        r = jnp.dot(ub[:, q * LANES:(q + 1) * LANES], wb_ref[q], preferred_element_type=F32)
        for k in range(4):
            j = 4 * q + k
            sre[pl.ds(j * pt, tc), :] = r[:, k * LANES:(k + 1) * LANES]
            sim[pl.ds(j * pt, tc), :] = r[:, SSM_W + k * LANES:SSM_W + (k + 1) * LANES]

    ar0, ar1 = a_ref[0, 0:8, :], a_ref[0, 8:16, :]
    ai0, ai1 = a_ref[1, 0:8, :], a_ref[1, 8:16, :]

    def step(t, carry):
        hr0, hr1, hi0, hi1 = carry
        i0 = pl.ds(t, 8, stride=pt)
        i1 = pl.ds(t + 8 * pt, 8, stride=pt)
        nr0 = ar0 * hr0 - ai0 * hi0 + sre[i0, :]
        ni0 = ar0 * hi0 + ai0 * hr0 + sim[i0, :]
        nr1 = ar1 * hr1 - ai1 * hi1 + sre[i1, :]
        ni1 = ar1 * hi1 + ai1 * hr1 + sim[i1, :]
        sre[i0, :] = nr0
        sim[i0, :] = ni0
        sre[i1, :] = nr1
        sim[i1, :] = ni1
        return nr0, nr1, ni0, ni1

    init = (hst[0, 0:8, :], hst[0, 8:16, :], hst[1, 0:8, :], hst[1, 8:16, :])
    hr0, hr1, hi0, hi1 = lax.fori_loop(0, tc, step, init)
    hst[0, 0:8, :] = hr0
    hst[0, 8:16, :] = hr1
    hst[1, 0:8, :] = hi0
    hst[1, 8:16, :] = hi1

    @pl.when(c == nc_chunks - 1)
    def _():
        hr_ref[...] = hst[0]
        hi_ref[...] = hst[1]

    ys = []
    for q in range(4):
        acc = None
        for jj in range(4):
            j = 4 * q + jj
            lhs = jnp.concatenate([sre[pl.ds(j * pt, tc), :], sim[pl.ds(j * pt, tc), :]], axis=-1)
            d = jnp.dot(lhs.astype(BF16), wc_ref[j], preferred_element_type=F32)
            acc = d if acc is None else acc + d
        ys.append(acc)
    y_lin = jnp.concatenate(ys, axis=-1)
    ns = _ssm_out(y_lin, u, dsk_ref[...], w_glu_ref, b_glu_ref[...], g_os_ref[...])

    for s in range(1, 8):
        eshift[s - 1, pl.ds(0, tc + 24), :] = ebuf[pl.ds(s, tc + 24), :]
    rb = 64
    convs = []
    for r0 in range(0, tc, rb):
        acc = None
        for k in range(KW):
            a8, s = (k + 2) // 8 * 8, (k + 2) % 8
            win = ebuf[pl.ds(r0 + a8, rb), :] if s == 0 else eshift[s - 1, pl.ds(r0 + a8, rb), :]
            term = w_dw_ref[k:k + 1, :] * win
            acc = term if acc is None else acc + term
        convs.append(acc)
    conv = jnp.concatenate(convs, axis=0)
    nc = _conv_out(conv, b_dw_ref[...], ln_g_ref[...], ln_b_ref[...], g_oc_ref[...])

    @pl.when(c == nc_chunks - 1)
    def _():
        cache_ref[...] = ebuf[pl.ds(tc + 2, CB), :]

    ebuf[pl.ds(0, 32), :] = ebuf[pl.ds(tc, 32), :]

    x1, n2, rt, cnt = _tail(x, ns, nc, gt1, sc2, sh2, w_out_ref, g_ffn_ref[...], w_r_ref, b_r_ref[...])
    x1_ref[...] = x1
    n2_ref[...] = n2.astype(BF16)
    rt_ref[...] = rt
    cnt_ref[...] = cnt


def _const_spec(shape):
    nd = len(shape)
    return pl.BlockSpec(shape, lambda b, c: (0,) * nd)


def _prompt_mixer_call(x, mod6, wts, tc):
    bsz, t, _ = x.shape
    n_all = bsz * t
    pt = tc + 8
    nc = t // tc
    kern = functools.partial(_prompt_mixer_kernel, tc=tc, pt=pt)
    in_specs = [pl.BlockSpec((None, tc, D), lambda b, c: (b, c, 0)),
                pl.BlockSpec((None, 6, D), lambda b, c: (b, 0, 0))]
    in_specs += [_const_spec(w.shape) for w in wts]
    out_shape = (jax.ShapeDtypeStruct((n_all, D), F32),
                 jax.ShapeDtypeStruct((n_all, D), BF16),
                 jax.ShapeDtypeStruct((8, n_all), F32),
                 jax.ShapeDtypeStruct((n_all // tc, 1, LANES), F32),
                 jax.ShapeDtypeStruct((bsz, NCHUNK, LANES), F32),
                 jax.ShapeDtypeStruct((bsz, NCHUNK, LANES), F32),
                 jax.ShapeDtypeStruct((bsz, CB, CONV_CH), F32))
    out_specs = (pl.BlockSpec((tc, D), lambda b, c: (b * nc + c, 0)),
                 pl.BlockSpec((tc, D), lambda b, c: (b * nc + c, 0)),
                 pl.BlockSpec((8, tc), lambda b, c: (0, b * nc + c)),
                 pl.BlockSpec((None, 1, LANES), lambda b, c: (b * nc + c, 0, 0)),
                 pl.BlockSpec((None, NCHUNK, LANES), lambda b, c: (b, 0, 0)),
                 pl.BlockSpec((None, NCHUNK, LANES), lambda b, c: (b, 0, 0)),
                 pl.BlockSpec((None, CB, CONV_CH), lambda b, c: (b, 0, 0)))
    scratch = [pltpu.VMEM((NCHUNK * pt, LANES), F32),
               pltpu.VMEM((NCHUNK * pt, LANES), F32),
               pltpu.VMEM((2, NCHUNK, LANES), F32),
               pltpu.VMEM((tc + 32, CONV_CH), F32),
               pltpu.VMEM((7, tc + 32, CONV_CH), F32)]
    return pl.pallas_call(
        kern, grid=(bsz, nc), in_specs=in_specs, out_specs=out_specs, out_shape=out_shape,
        scratch_shapes=scratch,
        compiler_params=pltpu.CompilerParams(dimension_semantics=("arbitrary", "arbitrary"),
                                             vmem_limit_bytes=VMEM_LIMIT),
        name="prompt_mixer",
    )(x, mod6, *wts)


def _sample_mixer_kernel(x_ref, mod_ref, h0r_ref, h0i_ref, cache_ref,
                         g_mix_ref, w_in_ref, wb_ref, a_ref, wc_ref, dsk_ref,
                         w_glu_ref, b_glu_ref, w_dw_ref, b_dw_ref, ln_g_ref, ln_b_ref,
                         g_os_ref, g_oc_ref, w_out_ref, g_ffn_ref, w_r_ref, b_r_ref,
                         x1_ref, n2_ref, rt_ref, cnt_ref, hr_ref, hi_ref, glu_ref,
                         sre, sim, *, nb, nt):
    x = x_ref[...]

    def rows(i):
        m = mod_ref[:, i * D:(i + 1) * D]
        return jnp.concatenate([m] * nt, axis=0)

    sh1, sc1, gt1, sh2, sc2 = rows(0), rows(1), rows(2), rows(3), rows(4)
    u, glu = _front(x, sc1, sh1, g_mix_ref[...], w_in_ref)
    glu_ref[...] = glu

    ub = u.astype(BF16)
    for q in range(4):
        r = jnp.dot(ub[:, q * LANES:(q + 1) * LANES], wb_ref[q], preferred_element_type=F32)
        sre[:, q * SSM_W:(q + 1) * SSM_W] = r[:, :SSM_W]
        sim[:, q * SSM_W:(q + 1) * SSM_W] = r[:, SSM_W:]

    ar = a_ref[0:1, :]
    ai = a_ref[1:2, :]
    hr = h0r_ref[...]
    hi = h0i_ref[...]
    for t in range(nt):
        rs = pl.ds(t * nb, nb)
        nr = ar * hr - ai * hi + sre[rs, :]
        ni = ar * hi + ai * hr + sim[rs, :]
        sre[rs, :] = nr
        sim[rs, :] = ni
        hr, hi = nr, ni
    hr_ref[...] = hr
    hi_ref[...] = hi

    ys = []
    for q in range(4):
        acc = None
        for jj in range(4):
            j = 4 * q + jj
            lhs = jnp.concatenate([sre[:, j * LANES:(j + 1) * LANES],
                                   sim[:, j * LANES:(j + 1) * LANES]], axis=-1)
            d = jnp.dot(lhs.astype(BF16), wc_ref[j], preferred_element_type=F32)
            acc = d if acc is None else acc + d
        ys.append(acc)
    y_lin = jnp.concatenate(ys, axis=-1)
    ns = _ssm_out(y_lin, u, dsk_ref[...], w_glu_ref, b_glu_ref[...], g_os_ref[...])

    def ext(jrow):
        if jrow < CB:
            return cache_ref[jrow]
        return glu[(jrow - CB) * nb:(jrow - CB + 1) * nb, :]

    convs = []
    for t in range(nt):
        acc = None
        for k in range(KW):
            term = w_dw_ref[k:k + 1, :] * ext(t + k)
            acc = term if acc is None else acc + term
        convs.append(acc)
    conv = jnp.concatenate(convs, axis=0)
    nc = _conv_out(conv, b_dw_ref[...], ln_g_ref[...], ln_b_ref[...], g_oc_ref[...])

    x1, n2, rt, cnt = _tail(x, ns, nc, gt1, sc2, sh2, w_out_ref, g_ffn_ref[...], w_r_ref, b_r_ref[...])
    x1_ref[...] = x1
    n2_ref[...] = n2.astype(BF16)
    rt_ref[...] = rt
    cnt_ref[...] = cnt


def _sample_mixer_call(x_tm, mod_s, h0r, h0i, cache_tm, wts, nb, nt):
    n = nb * nt
    kern = functools.partial(_sample_mixer_kernel, nb=nb, nt=nt)
    out_shape = (jax.ShapeDtypeStruct((n, D), F32),
                 jax.ShapeDtypeStruct((n, D), BF16),
                 jax.ShapeDtypeStruct((8, n), F32),
                 jax.ShapeDtypeStruct((1, LANES), F32),
                 jax.ShapeDtypeStruct((nb, NSTATE), F32),
                 jax.ShapeDtypeStruct((nb, NSTATE), F32),
                 jax.ShapeDtypeStruct((n, CONV_CH), F32))
    scratch = [pltpu.VMEM((n, NSTATE), F32), pltpu.VMEM((n, NSTATE), F32)]
    return pl.pallas_call(
        kern, out_shape=out_shape, scratch_shapes=scratch,
        compiler_params=pltpu.CompilerParams(vmem_limit_bytes=VMEM_LIMIT),
        name="sample_mixer",
    )(x_tm, mod_s, h0r, h0i, cache_tm, *wts)


ROW_ALIGN = 16
MOE_TD = 512
MOE_BR = MOE_TD * 2 + NE * ROW_ALIGN
MOE_TG = 512


def _slot_positions(rt):
    t = rt.shape[1]
    e0 = rt[0:1, :]
    e1 = rt[1:2, :]
    sub = lax.broadcasted_iota(jnp.int32, (LANES, t), 0).astype(F32)
    a0 = jnp.where(sub == e0, 1.0, 0.0)
    a1 = jnp.where(sub == e1, 1.0, 0.0)
    at = a0 + a1
    r = lax.broadcasted_iota(jnp.int32, (t, t), 0)
    c = lax.broadcasted_iota(jnp.int32, (t, t), 1)
    before = jnp.where(r < c, 1.0, 0.0).astype(BF16)
    rank = jnp.dot(at.astype(BF16), before, preferred_element_type=F32)
    cnt = jnp.sum(at, axis=1, keepdims=True)
    cnt_al = jnp.ceil(cnt * (1.0 / ROW_ALIGN)) * float(ROW_ALIGN)
    er = lax.broadcasted_iota(jnp.int32, (LANES, LANES), 0)
    ec = lax.broadcasted_iota(jnp.int32, (LANES, LANES), 1)
    lower = jnp.where(ec < er, 1.0, 0.0)
    base = jnp.dot(lower, jnp.broadcast_to(cnt_al, (LANES, LANES)), preferred_element_type=F32,
                   precision=lax.Precision.HIGHEST)[:, 0:1]
    slot = rank + base
    pos0 = jnp.sum(a0 * slot, axis=0, keepdims=True)
    pos1 = jnp.sum(a1 * slot, axis=0, keepdims=True)
    return pos0, pos1


def _segment_copies(tab_ref, t, n_tiles, buf, hbm, sems, to_hbm):
    def copy(e):
        n = pl.multiple_of(tab_ref[t * NE + e], ROW_ALIGN)
        b = pl.multiple_of(tab_ref[(n_tiles + t) * NE + e], ROW_ALIGN)
        d = pl.multiple_of(tab_ref[(2 * n_tiles + t) * NE + e], ROW_ALIGN)
        if to_hbm:
            cp = pltpu.make_async_copy(buf.at[pl.ds(b, n)], hbm.at[pl.ds(d, n)], sems.at[e])
        else:
            cp = pltpu.make_async_copy(hbm.at[pl.ds(d, n)], buf.at[pl.ds(b, n)], sems.at[e])
        return n, cp

    for e in range(NE):
        n, cp = copy(e)

        @pl.when(n > 0)
        def _():
            cp.start()

    for e in range(NE):
        n, cp = copy(e)

        @pl.when(n > 0)
        def _():
            cp.wait()


def _dispatch_kernel(tab_ref, rtp_ref, n2p_ref, rts_ref, n2s_ref, xs_ref, buf, zbuf, sems, zsem,
                     *, n_tiles, n_ptiles, n_gtiles):
    t = pl.program_id(0)

    @pl.when(t == 0)
    def _():
        zbuf[...] = jnp.zeros_like(zbuf)
        for phase in range(2):
            for e in range(NE):
                d = pl.multiple_of(tab_ref[3 * n_tiles * NE + e], ROW_ALIGN)
                n = pl.multiple_of(tab_ref[3 * n_tiles * NE + NE + e], ROW_ALIGN)
                cp = pltpu.make_async_copy(zbuf.at[pl.ds(0, n)], xs_ref.at[pl.ds(d, n)], sems.at[e])

                @pl.when(n > 0)
                def _():
                    if phase == 0:
                        cp.start()
                    else:
                        cp.wait()

        first_free = tab_ref[3 * n_tiles * NE + 2 * NE]

        def fill_copy(j):
            d = pl.multiple_of(j * MOE_TG, MOE_TG)
            return pltpu.make_async_copy(zbuf, xs_ref.at[pl.ds(d, MOE_TG)], zsem)

        def fill_start(j, carry):
            fill_copy(j).start()
            return carry

        def fill_wait(j, carry):
            fill_copy(j).wait()
            return carry

        lax.fori_loop(first_free, n_gtiles, fill_start, 0)
        lax.fori_loop(first_free, n_gtiles, fill_wait, 0)

    is_sample = t >= n_ptiles
    rt = jnp.where(is_sample, rts_ref[...], rtp_ref[...])
    n2 = jnp.where(is_sample, n2s_ref[...], n2p_ref[...])
    pos0, pos1 = _slot_positions(rt)
    row = lax.broadcasted_iota(jnp.int32, (MOE_BR, MOE_TD), 0).astype(F32)
    q = (jnp.where(row == pos0, 1.0, 0.0) + jnp.where(row == pos1, 1.0, 0.0)).astype(BF16)
    buf[...] = jnp.dot(q, n2, preferred_element_type=F32).astype(BF16)
    _segment_copies(tab_ref, t, n_tiles, buf, xs_ref, sems, to_hbm=True)


def _dispatch_call(tab, rt_p, n2_p, rt_s, n2_s, r_max):
    n_ptiles = n2_p.shape[0] // MOE_TD
    n_tiles = n_ptiles + n2_s.shape[0] // MOE_TD
    last_p = n_ptiles - 1
    return pl.pallas_call(
        functools.partial(_dispatch_kernel, n_tiles=n_tiles, n_ptiles=n_ptiles,
                          n_gtiles=r_max // MOE_TG),
        grid_spec=pltpu.PrefetchScalarGridSpec(
            num_scalar_prefetch=1, grid=(n_tiles,),
            in_specs=[pl.BlockSpec((8, MOE_TD), lambda t, tab: (0, jnp.minimum(t, last_p))),
                      pl.BlockSpec((MOE_TD, D), lambda t, tab: (jnp.minimum(t, last_p), 0)),
                      pl.BlockSpec((8, MOE_TD), lambda t, tab: (0, 0)),
                      pl.BlockSpec((MOE_TD, D), lambda t, tab: (0, 0))],
            out_specs=pl.BlockSpec(memory_space=pl.ANY),
            scratch_shapes=[pltpu.VMEM((MOE_BR, D), BF16),
                            pltpu.VMEM((MOE_TG, D), BF16),
                            pltpu.SemaphoreType.DMA((NE,)),
                            pltpu.SemaphoreType.DMA(())]),
        out_shape=jax.ShapeDtypeStruct((r_max, D), BF16),
        compiler_params=pltpu.CompilerParams(dimension_semantics=("arbitrary",),
                                             vmem_limit_bytes=VMEM_LIMIT),
        name="moe_dispatch",
    )(tab, rt_p, n2_p, rt_s, n2_s)


def _experts_kernel(te_ref, first_ref, nact_ref, xs_ref, w1_ref, w3_ref, w2_ref, ys_ref,
                    w1b, w3b, w2b):
    j = pl.program_id(0)

    @pl.when(j < nact_ref[0])
    def _():
        @pl.when(first_ref[j] == 1)
        def _():
            w1b[...] = w1_ref[...].astype(BF16)
            w3b[...] = w3_ref[...].astype(BF16)
            w2b[...] = w2_ref[...].astype(BF16)

        x = xs_ref[...]
        a = jnp.dot(x, w1b[...], preferred_element_type=F32)
        b = jnp.dot(x, w3b[...], preferred_element_type=F32)
        hid = a * jax.nn.sigmoid(a) * b
        y = jnp.dot(hid.astype(BF16), w2b[...], preferred_element_type=F32)
        ys_ref[...] = y.astype(BF16)

    @pl.when(j >= nact_ref[0])
    def _():
        ys_ref[...] = jnp.zeros_like(ys_ref)


def _experts_call(tile_expert, tile_first, n_active, xs, w1, w3, w2):
    r_max = xs.shape[0]
    n_steps = r_max // MOE_TG

    def row_map(j, te, first, nact):
        return (j, 0)

    def w_map(j, te, first, nact):
        return (te[j], 0, 0)

    return pl.pallas_call(
        _experts_kernel,
        grid_spec=pltpu.PrefetchScalarGridSpec(
            num_scalar_prefetch=3, grid=(n_steps,),
            in_specs=[pl.BlockSpec((MOE_TG, D), row_map),
                      pl.BlockSpec((None, D, DE), w_map),
                      pl.BlockSpec((None, D, DE), w_map),
                      pl.BlockSpec((None, DE, D), w_map)],
            out_specs=pl.BlockSpec((MOE_TG, D), row_map),
            scratch_shapes=[pltpu.VMEM((D, DE), BF16), pltpu.VMEM((D, DE), BF16),
                            pltpu.VMEM((DE, D), BF16)]),
        out_shape=jax.ShapeDtypeStruct((r_max, D), BF16),
        compiler_params=pltpu.CompilerParams(dimension_semantics=("arbitrary",),
                                             vmem_limit_bytes=VMEM_LIMIT),
        name="moe_experts",
    )(tile_expert, tile_first, n_active, xs, w1, w3, w2)


def _combine_kernel(tab_ref, rt_ref, x1_ref, gt2_ref, gf_ref, ys_ref, y_ref, buf, sems,
                    *, n_tiles, t_off):
    t = pl.program_id(0) + t_off

    @pl.when(pl.program_id(0) == 0)
    def _():
        buf[...] = jnp.zeros_like(buf)

    _segment_copies(tab_ref, t, n_tiles, buf, ys_ref, sems, to_hbm=False)

    rt = rt_ref[...]
    pos0, pos1 = _slot_positions(rt)
    row = lax.broadcasted_iota(jnp.int32, (MOE_BR, MOE_TD), 0).astype(F32)
    m0 = row == pos0
    m1 = row == pos1
    q = (jnp.where(m0, 1.0, 0.0) + jnp.where(m1, 1.0, 0.0)).astype(BF16)
    gw = jnp.sum(jnp.where(m0, rt[2:3, :], 0.0) + jnp.where(m1, rt[3:4, :], 0.0),
                 axis=1, keepdims=True)
    yv = (buf[...].astype(F32) * gw).astype(BF16)
    moe = lax.dot_general(q, yv, (((0,), (0,)), ((), ())), preferred_element_type=F32)
    xo = x1_ref[...] + gt2_ref[...] * moe
    y_ref[...] = _rms(xo) * gf_ref[...]


def _combine_call(tab, rt, x1, gt2, gt2_spec, g_final, ys, n_tiles, t_off):
    n_out_tiles = x1.shape[0] // MOE_TD
    return pl.pallas_call(
        functools.partial(_combine_kernel, n_tiles=n_tiles, t_off=t_off),
        grid_spec=pltpu.PrefetchScalarGridSpec(
            num_scalar_prefetch=1, grid=(n_out_tiles,),
            in_specs=[pl.BlockSpec((8, MOE_TD), lambda t, tab: (0, t)),
                      pl.BlockSpec((MOE_TD, D), lambda t, tab: (t, 0)),
                      gt2_spec,
                      pl.BlockSpec((1, D), lambda t, tab: (0, 0)),
                      pl.BlockSpec(memory_space=pl.ANY)],
            out_specs=pl.BlockSpec((MOE_TD, D), lambda t, tab: (t, 0)),
            scratch_shapes=[pltpu.VMEM((MOE_BR, D), BF16),
                            pltpu.SemaphoreType.DMA((NE,))]),
        out_shape=jax.ShapeDtypeStruct((n_out_tiles * MOE_TD, D), F32),
        compiler_params=pltpu.CompilerParams(dimension_semantics=("arbitrary",),
                                             vmem_limit_bytes=VMEM_LIMIT),
        name="moe_combine",
    )(tab, rt, x1, gt2, g_final.reshape(1, D), ys)


def _moe_plan(cnt, r_max):
    cnt8 = (cnt + ROW_ALIGN - 1) // ROW_ALIGN * ROW_ALIGN
    seg_rows = cnt8.sum(axis=0)
    seg_pad = (seg_rows + MOE_TG - 1) // MOE_TG * MOE_TG
    seg_start = jnp.cumsum(seg_pad) - seg_pad
    dst = seg_start[None, :] + jnp.cumsum(cnt8, axis=0) - cnt8
    boff = jnp.cumsum(cnt8, axis=1) - cnt8
    tile_end = jnp.cumsum(seg_pad // MOE_TG)
    n_active = tile_end[-1:].astype(jnp.int32)
    tab = jnp.concatenate([cnt8.ravel(), boff.ravel(), dst.ravel(),
                           seg_start + seg_rows, seg_pad - seg_rows, n_active]).astype(jnp.int32)
    j = jnp.arange(r_max // MOE_TG, dtype=jnp.int32)
    tile_expert = jnp.minimum(jnp.sum(j[:, None] >= tile_end[None, :], axis=1), NE - 1).astype(jnp.int32)
    tile_first = jnp.concatenate([jnp.ones((1,), jnp.int32),
                                  (tile_expert[1:] != tile_expert[:-1]).astype(jnp.int32)])
    return tab, tile_expert, tile_first, n_active


def kernel(x_prompt, x_sample, c_prompt, c_sample, state_ssm_re, state_ssm_im, cache_conv, w_ada, b_ada, g_norm_mix, w_in, ssm_a_re, ssm_a_im, ssm_log_dt, ssm_b_re, ssm_b_im, ssm_c_re, ssm_c_im, ssm_d, w_ssm_glu, b_ssm_glu, w_dw, b_dw, ln_conv_g, ln_conv_b, g_out_ssm, g_out_conv, w_out, g_norm_ffn, w_router_grp, b_router_grp, w_router_exp, b_router_exp, w_exp_gate, w_exp_up, w_exp_down, g_final):
    depth = w_ada.shape[0]
    assert depth == 1
    bsz, seq, _ = x_prompt.shape
    nb, nt, _ = x_sample.shape

    c_all = jnp.concatenate([c_prompt, c_sample], axis=0)
    mod = _mod_call(c_all, w_ada[0], b_ada[0])
    mod_p = mod[:bsz].reshape(bsz, 6, D)
    mod_s = mod[bsz:]

    ab_re, ab_im, bb_re, bb_im, c_im_neg = _ssm_prep_call(
        ssm_a_re[0], ssm_a_im[0], ssm_log_dt[0], ssm_b_re[0], ssm_b_im[0], ssm_c_im[0])
    wb, wc = _block_diag_weights(bb_re, bb_im, ssm_c_re[0], c_im_neg)
    a_tok = jnp.stack([ab_re.reshape(NCHUNK, LANES), ab_im.reshape(NCHUNK, LANES)])
    a_row = jnp.stack([ab_re.reshape(NSTATE), ab_im.reshape(NSTATE)])

    w_r = jnp.concatenate([w_router_exp[0].reshape(D, NE), w_router_grp[0],
                           jnp.zeros((D, LANES - NE - NG), F32)], axis=1)
    w_r_hi = w_r.astype(BF16)
    w_r = jnp.concatenate([w_r_hi, (w_r - w_r_hi.astype(F32)).astype(BF16)], axis=1)
    b_r = jnp.concatenate([b_router_exp[0].reshape(NE), b_router_grp[0],
                           jnp.zeros((LANES - NE - NG,), F32)]).reshape(1, LANES)
    w_dw_p = jnp.concatenate([w_dw[0], jnp.zeros((1, CONV_CH), F32)], axis=0)

    row = lambda v: v.reshape(1, -1)
    common_a = (row(g_norm_mix[0]), w_in[0].astype(BF16), wb)
    common_b = (wc, row(ssm_d[0].reshape(SSM_W)), w_ssm_glu[0].astype(BF16), row(b_ssm_glu[0]),
                w_dw_p, row(b_dw[0]), row(ln_conv_g[0]), row(ln_conv_b[0]),
                row(g_out_ssm[0]), row(g_out_conv[0]), w_out[0].astype(BF16),
                row(g_norm_ffn[0]), w_r, b_r)

    n_p = bsz * seq
    n_s = nb * nt
    n_all = n_p + n_s
    assert n_s == MOE_TD and seq % MOE_TD == 0 and MOE_TD % PROMPT_TC == 0
    wts_p = common_a + (a_tok,) + common_b
    x1_p, n2_p, rt_p, cnt_p, hr_p, hi_p, cache_p = _prompt_mixer_call(x_prompt, mod_p, wts_p, PROMPT_TC)

    x_tm = jnp.transpose(x_sample, (1, 0, 2)).reshape(nt * nb, D)
    cache_tm = jnp.transpose(cache_conv[0], (1, 0, 2))
    wts_s = common_a + (a_row,) + common_b
    x1_s, n2_s, rt_s, cnt_s, hr_s, hi_s, glu_s = _sample_mixer_call(
        x_tm, mod_s, state_ssm_re[0].reshape(nb, NSTATE), state_ssm_im[0].reshape(nb, NSTATE),
        cache_tm, wts_s, nb, nt)

    n_ptiles = n_p // MOE_TD
    n_tiles = n_all // MOE_TD
    r_max = -(-(2 * n_all + n_tiles * NE * (ROW_ALIGN - 1) + NE * (MOE_TG - ROW_ALIGN)) // MOE_TG) * MOE_TG
    cnt = jnp.concatenate([cnt_p.reshape(n_ptiles, MOE_TD // PROMPT_TC, LANES).sum(axis=1), cnt_s])
    tab, tile_expert, tile_first, n_active = _moe_plan(cnt[:, :NE].astype(jnp.int32), r_max)
    xs = _dispatch_call(tab, rt_p, n2_p, rt_s, n2_s, r_max)
    ys = _experts_call(tile_expert, tile_first, n_active, xs,
                       w_exp_gate[0], w_exp_up[0], w_exp_down[0])
    tiles_per_b = seq // MOE_TD
    gt2_p = mod_p[:, 5:6, :]
    y_p = _combine_call(tab, rt_p, x1_p, gt2_p,
                        pl.BlockSpec((None, 1, D), lambda t, tab: (t // tiles_per_b, 0, 0)),
                        g_final, ys, n_tiles, 0)
    gt2_s = jnp.tile(mod_s[:, 5 * D:], (nt, 1))
    y_s = _combine_call(tab, rt_s, x1_s, gt2_s,
                        pl.BlockSpec((n_s, D), lambda t, tab: (0, 0)),
                        g_final, ys, n_tiles, n_ptiles)

    y_prompt = y_p.reshape(bsz, seq, D)
    y_sample = jnp.transpose(y_s.reshape(nt, nb, D), (1, 0, 2))
    new_cache_s = jnp.concatenate(
        [cache_conv[0][:, nt:, :], jnp.transpose(glu_s.reshape(nt, nb, CONV_CH), (1, 0, 2))], axis=1)
    return (y_prompt, y_sample,
            hr_p.reshape(1, bsz, G, P), hi_p.reshape(1, bsz, G, P), cache_p[None],
            hr_s.reshape(1, nb, G, P), hi_s.reshape(1, nb, G, P), new_cache_s[None])
```

```python
import functools

import jax
import jax.numpy as jnp
import numpy as np
from jax import lax
from jax.experimental import pallas as pl
from jax.experimental.pallas import tpu as pltpu

F32 = jnp.float32
BF16 = jnp.bfloat16

D = 1024
SSM_W = 512
CONV_CH = 512
G = 32
H = 16
P = 64
KW = 31
CB = KW - 1
NE = 32
NG = 4
EPG = 8
DE = 512
EPS = 1e-6
LANES = 128
NSTATE = G * P
NCHUNK = NSTATE // LANES

PROMPT_TC = 512
VMEM_LIMIT = 56 * 1024 * 1024


def _rms(x):
    return x * lax.rsqrt(jnp.mean(x * x, axis=-1, keepdims=True) + EPS)


def _gelu_tanh(y):
    c = np.sqrt(2.0 / np.pi).astype(np.float32)
    return y * (0.5 * (1.0 + jnp.tanh(c * (y + 0.044715 * (y * y * y)))))


def _bdot(a, b):
    return jnp.dot(a.astype(BF16), b, preferred_element_type=F32)


def _mod_kernel(c_ref, w_ref, b_ref, o_ref):
    c = c_ref[...]
    s = c * jax.nn.sigmoid(c)
    o_ref[...] = jnp.dot(s, w_ref[...], preferred_element_type=F32,
                         precision=lax.Precision.HIGHEST) + b_ref[...]


def _mod_call(c_all, w_ada, b_ada):
    n = c_all.shape[0]
    tn = 512
    return pl.pallas_call(
        _mod_kernel,
        grid=(6 * D // tn,),
        in_specs=[pl.BlockSpec((n, D), lambda j: (0, 0)),
                  pl.BlockSpec((D, tn), lambda j: (0, j)),
                  pl.BlockSpec((1, tn), lambda j: (0, j))],
        out_specs=pl.BlockSpec((n, tn), lambda j: (0, j)),
        out_shape=jax.ShapeDtypeStruct((n, 6 * D), F32),
        compiler_params=pltpu.CompilerParams(dimension_semantics=("arbitrary",)),
        name="mod",
    )(c_all, w_ada, b_ada.reshape(1, 6 * D))


def _ssm_prep_kernel(a_re, a_im, log_dt, b_re, b_im, c_im,
                     ab_re_o, ab_im_o, bb_re_o, bb_im_o, cneg_o):
    lam_re = jnp.minimum(a_re[...], -1e-4)
    lam_im = a_im[...]
    dt = jnp.exp(log_dt[...])
    mag = jnp.exp(lam_re * dt)
    ab_re = mag * jnp.cos(lam_im * dt)
    ab_im = mag * jnp.sin(lam_im * dt)
    den = lam_re * lam_re + lam_im * lam_im
    num_re = ab_re - 1.0
    coef_re = (num_re * lam_re + ab_im * lam_im) / den
    coef_im = (ab_im * lam_re - num_re * lam_im) / den
    ab_re_o[...] = ab_re
    ab_im_o[...] = ab_im
    br = b_re[...]
    bi = b_im[...]
    bb_re_o[...] = coef_re * br - coef_im * bi
    bb_im_o[...] = coef_re * bi + coef_im * br
    cneg_o[...] = -c_im[...]


def _ssm_prep_call(a_re, a_im, log_dt, b_re, b_im, c_im):
    flat = lambda v: v.reshape(1, NSTATE)
    b_hs = lambda v: jnp.transpose(v, (2, 0, 1)).reshape(H, NSTATE)
    dt_row = jnp.broadcast_to(log_dt[:, None], (G, P)).reshape(1, NSTATE)
    ab_re, ab_im, bb_re, bb_im, cneg = pl.pallas_call(
        _ssm_prep_kernel,
        out_shape=(jax.ShapeDtypeStruct((1, NSTATE), F32), jax.ShapeDtypeStruct((1, NSTATE), F32),
                   jax.ShapeDtypeStruct((H, NSTATE), F32), jax.ShapeDtypeStruct((H, NSTATE), F32),
                   jax.ShapeDtypeStruct((G * H, P), F32)),
        name="ssm_prep",
    )(flat(a_re), flat(a_im), dt_row, b_hs(b_re), b_hs(b_im), c_im.reshape(G * H, P))
    ghp = lambda v: jnp.transpose(v.reshape(H, G, P), (1, 0, 2))
    return (ab_re.reshape(G, P), ab_im.reshape(G, P), ghp(bb_re), ghp(bb_im), cneg.reshape(G, H, P))


def _block_diag_weights(bb_re, bb_im, c_re, c_im_neg):
    eye8 = jnp.eye(8, dtype=F32)
    eye4 = jnp.eye(4, dtype=F32)
    eye2 = jnp.eye(2, dtype=F32)

    def wb_part(bb):
        x = bb.reshape(4, 8, H, P)
        return jnp.einsum('qghp,gk->qghkp', x, eye8).reshape(4, 8 * H, 8 * P)

    wb = jnp.concatenate([wb_part(bb_re), wb_part(bb_im)], axis=-1).astype(BF16)

    def wc_part(c):
        x = c.reshape(4, 4, 2, H, P)
        y = jnp.einsum('qjghp,jk,gl->qjgpklh', x, eye4, eye2)
        return y.reshape(NCHUNK, 2 * P, 4 * 2 * H)

    wc = jnp.concatenate([wc_part(c_re), wc_part(c_im_neg)], axis=1).astype(BF16)
    return wb, wc


def _front(x, sc1, sh1, g_mix, w_in_ref):
    n = _rms(x) * g_mix * (1.0 + sc1) + sh1
    proj = _bdot(n, w_in_ref[...])
    u = proj[:, :SSM_W]
    glu = proj[:, SSM_W:SSM_W + CONV_CH] * jax.nn.sigmoid(proj[:, SSM_W + CONV_CH:])
    return u, glu


def _ssm_out(y_lin, u, d_skip, w_glu_ref, b_glu, g_out_ssm):
    y = _gelu_tanh(y_lin + d_skip * u)
    ys = y * jax.nn.sigmoid(_bdot(y, w_glu_ref[...]) + b_glu)
    return _rms(ys) * g_out_ssm


def _conv_out(conv, b_dw, ln_g, ln_b, g_out_conv):
    c = conv + b_dw
    mu = jnp.mean(c, axis=-1, keepdims=True)
    cc = c - mu
    var = jnp.mean(cc * cc, axis=-1, keepdims=True)
    ln = cc * lax.rsqrt(var + EPS) * ln_g + ln_b
    yc = ln * jax.nn.sigmoid(ln)
    return _rms(yc) * g_out_conv


def _route(n2, w_r_ref, b_r):
    rows = n2.shape[0]
    n_hi = n2.astype(BF16)
    n_lo = (n2 - n_hi.astype(F32)).astype(BF16)
    parts = jnp.dot(jnp.concatenate([n_hi, n_lo], axis=0), w_r_ref[...], preferred_element_type=F32)
    lg = (parts[:rows, :LANES] + parts[:rows, LANES:]) + (parts[rows:, :LANES] + parts[rows:, LANES:]) + b_r
    lane = lax.broadcasted_iota(jnp.int32, (rows, LANES), 1).astype(F32)
    ninf = -jnp.inf
    big = 1e9
    gmask = jnp.logical_and(lane >= NE, lane < NE + NG)
    gl = jnp.where(gmask, lg, ninf)
    gmax = jnp.max(gl, axis=-1, keepdims=True)
    gsum = jnp.sum(jnp.where(gmask, jnp.exp(gl - gmax), 0.0), axis=-1, keepdims=True)
    p_top = 1.0 / gsum
    gi = jnp.min(jnp.where(gl == gmax, lane, big), axis=-1, keepdims=True) - NE
    lo = gi * EPG
    emask = jnp.logical_and(lane >= lo, lane < lo + EPG)
    el = jnp.where(emask, lg, ninf)
    m1 = jnp.max(el, axis=-1, keepdims=True)
    i1 = jnp.min(jnp.where(el == m1, lane, big), axis=-1, keepdims=True)
    el2 = jnp.where(lane == i1, ninf, el)
    m2 = jnp.max(el2, axis=-1, keepdims=True)
    i2 = jnp.min(jnp.where(el2 == m2, lane, big), axis=-1, keepdims=True)
    e2 = jnp.exp(m2 - m1)
    den = 1.0 + e2
    w1 = p_top / den
    w2 = p_top * e2 / den
    cnt = jnp.sum(jnp.where(lane == i1, 1.0, 0.0) + jnp.where(lane == i2, 1.0, 0.0),
                  axis=0, keepdims=True)
    cols = (jnp.where(lane == 0.0, i1, 0.0) + jnp.where(lane == 1.0, i2, 0.0)
            + jnp.where(lane == 2.0, w1, 0.0) + jnp.where(lane == 3.0, w2, 0.0))
    return cols.T[0:8, :], cnt


def _tail(x, ns, nc, gt1, sc2, sh2, w_out_ref, g_ffn, w_r_ref, b_r):
    merged = _bdot(jnp.concatenate([ns, nc], axis=-1), w_out_ref[...])
    x1 = x + gt1 * merged
    n2 = _rms(x1) * g_ffn * (1.0 + sc2) + sh2
    rt, cnt = _route(n2, w_r_ref, b_r)
    return x1, n2, rt, cnt


def _prompt_mixer_kernel(x_ref, mod_ref, g_mix_ref, w_in_ref, wb_ref, a_ref, wc_ref, dsk_ref,
                         w_glu_ref, b_glu_ref, w_dw_ref, b_dw_ref, ln_g_ref, ln_b_ref,
                         g_os_ref, g_oc_ref, w_out_ref, g_ffn_ref, w_r_ref, b_r_ref,
                         x1_ref, n2_ref, rt_ref, cnt_ref, hr_ref, hi_ref, cache_ref,
                         sre, sim, hst, ebuf, eshift, *, tc, pt):
    c = pl.program_id(1)
    nc_chunks = pl.num_programs(1)

    @pl.when(c == 0)
    def _():
        hst[...] = jnp.zeros_like(hst)
        ebuf[pl.ds(0, 32), :] = jnp.zeros((32, CONV_CH), F32)

    x = x_ref[...]
    mod = mod_ref[...]
    sh1, sc1, gt1 = mod[0:1], mod[1:2], mod[2:3]
    sh2, sc2 = mod[3:4], mod[4:5]

    u, glu = _front(x, sc1, sh1, g_mix_ref[...], w_in_ref)
    ebuf[pl.ds(32, tc), :] = glu

    ub = u.astype(BF16)
    for q in range(4):
        r = jnp.dot(ub[:, q * LANES:(q + 1) * LANES], wb_ref[q], preferred_element_type=F32)
        for k in range(4):
            j = 4 * q + k
            sre[pl.ds(j * pt, tc), :] = r[:, k * LANES:(k + 1) * LANES]
            sim[pl.ds(j * pt, tc), :] = r[:, SSM_W + k * LANES:SSM_W + (k + 1) * LANES]

    ar0, ar1 = a_ref[0, 0:8, :], a_ref[0, 8:16, :]
    ai0, ai1 = a_ref[1, 0:8, :], a_ref[1, 8:16, :]

    def step(t, carry):
        hr0, hr1, hi0, hi1 = carry
        i0 = pl.ds(t, 8, stride=pt)
        i1 = pl.ds(t + 8 * pt, 8, stride=pt)
        nr0 = ar0 * hr0 - ai0 * hi0 + sre[i0, :]
        ni0 = ar0 * hi0 + ai0 * hr0 + sim[i0, :]
        nr1 = ar1 * hr1 - ai1 * hi1 + sre[i1, :]
        ni1 = ar1 * hi1 + ai1 * hr1 + sim[i1, :]
        sre[i0, :] = nr0
        sim[i0, :] = ni0
        sre[i1, :] = nr1
        sim[i1, :] = ni1
        return nr0, nr1, ni0, ni1

    init = (hst[0, 0:8, :], hst[0, 8:16, :], hst[1, 0:8, :], hst[1, 8:16, :])
    hr0, hr1, hi0, hi1 = lax.fori_loop(0, tc, step, init)
    hst[0, 0:8, :] = hr0
    hst[0, 8:16, :] = hr1
    hst[1, 0:8, :] = hi0
    hst[1, 8:16, :] = hi1

    @pl.when(c == nc_chunks - 1)
    def _():
        hr_ref[...] = hst[0]
        hi_ref[...] = hst[1]

    ys = []
    for q in range(4):
        acc = None
        for jj in range(4):
            j = 4 * q + jj
            lhs = jnp.concatenate([sre[pl.ds(j * pt, tc), :], sim[pl.ds(j * pt, tc), :]], axis=-1)
            d = jnp.dot(lhs.astype(BF16), wc_ref[j], preferred_element_type=F32)
            acc = d if acc is None else acc + d
        ys.append(acc)
    y_lin = jnp.concatenate(ys, axis=-1)
    ns = _ssm_out(y_lin, u, dsk_ref[...], w_glu_ref, b_glu_ref[...], g_os_ref[...])

    for s in range(1, 8):
        eshift[s - 1, pl.ds(0, tc + 24), :] = ebuf[pl.ds(s, tc + 24), :]
    rb = 64
    convs = []
    for r0 in range(0, tc, rb):
        acc = None
        for k in range(KW):
            a8, s = (k + 2) // 8 * 8, (k + 2) % 8
            win = ebuf[pl.ds(r0 + a8, rb), :] if s == 0 else eshift[s - 1, pl.ds(r0 + a8, rb), :]
            term = w_dw_ref[k:k + 1, :] * win
            acc = term if acc is None else acc + term
        convs.append(acc)
    conv = jnp.concatenate(convs, axis=0)
    nc = _conv_out(conv, b_dw_ref[...], ln_g_ref[...], ln_b_ref[...], g_oc_ref[...])

    @pl.when(c == nc_chunks - 1)
    def _():
        cache_ref[...] = ebuf[pl.ds(tc + 2, CB), :]

    ebuf[pl.ds(0, 32), :] = ebuf[pl.ds(tc, 32), :]

    x1, n2, rt, cnt = _tail(x, ns, nc, gt1, sc2, sh2, w_out_ref, g_ffn_ref[...], w_r_ref, b_r_ref[...])
    x1_ref[...] = x1
    n2_ref[...] = n2.astype(BF16)
    rt_ref[...] = rt
    cnt_ref[...] = cnt


def _const_spec(shape):
    nd = len(shape)
    return pl.BlockSpec(shape, lambda b, c: (0,) * nd)


def _prompt_mixer_call(x, mod6, wts, tc):
    bsz, t, _ = x.shape
    n_all = bsz * t
    pt = tc + 8
    nc = t // tc
    kern = functools.partial(_prompt_mixer_kernel, tc=tc, pt=pt)
    in_specs = [pl.BlockSpec((None, tc, D), lambda b, c: (b, c, 0)),
                pl.BlockSpec((None, 6, D), lambda b, c: (b, 0, 0))]
    in_specs += [_const_spec(w.shape) for w in wts]
    out_shape = (jax.ShapeDtypeStruct((n_all, D), F32),
                 jax.ShapeDtypeStruct((n_all, D), BF16),
                 jax.ShapeDtypeStruct((8, n_all), F32),
                 jax.ShapeDtypeStruct((n_all // tc, 1, LANES), F32),
                 jax.ShapeDtypeStruct((bsz, NCHUNK, LANES), F32),
                 jax.ShapeDtypeStruct((bsz, NCHUNK, LANES), F32),
                 jax.ShapeDtypeStruct((bsz, CB, CONV_CH), F32))
    out_specs = (pl.BlockSpec((tc, D), lambda b, c: (b * nc + c, 0)),
                 pl.BlockSpec((tc, D), lambda b, c: (b * nc + c, 0)),
                 pl.BlockSpec((8, tc), lambda b, c: (0, b * nc + c)),
                 pl.BlockSpec((None, 1, LANES), lambda b, c: (b * nc + c, 0, 0)),
                 pl.BlockSpec((None, NCHUNK, LANES), lambda b, c: (b, 0, 0)),
                 pl.BlockSpec((None, NCHUNK, LANES), lambda b, c: (b, 0, 0)),
                 pl.BlockSpec((None, CB, CONV_CH), lambda b, c: (b, 0, 0)))
    scratch = [pltpu.VMEM((NCHUNK * pt, LANES), F32),
               pltpu.VMEM((NCHUNK * pt, LANES), F32),
               pltpu.VMEM((2, NCHUNK, LANES), F32),
               pltpu.VMEM((tc + 32, CONV_CH), F32),
               pltpu.VMEM((7, tc + 32, CONV_CH), F32)]
    return pl.pallas_call(
        kern, grid=(bsz, nc), in_specs=in_specs, out_specs=out_specs, out_shape=out_shape,
        scratch_shapes=scratch,
        compiler_params=pltpu.CompilerParams(dimension_semantics=("arbitrary", "arbitrary"),
                                             vmem_limit_bytes=VMEM_LIMIT),
        name="prompt_mixer",
    )(x, mod6, *wts)


def _sample_mixer_kernel(x_ref, mod_ref, h0r_ref, h0i_ref, cache_ref,
                         g_mix_ref, w_in_ref, wb_ref, a_ref, wc_ref, dsk_ref,
                         w_glu_ref, b_glu_ref, w_dw_ref, b_dw_ref, ln_g_ref, ln_b_ref,
                         g_os_ref, g_oc_ref, w_out_ref, g_ffn_ref, w_r_ref, b_r_ref,
                         x1_ref, n2_ref, rt_ref, cnt_ref, hr_ref, hi_ref, glu_ref,
                         sre, sim, *, nb, nt):
    x = x_ref[...]

    def rows(i):
        m = mod_ref[:, i * D:(i + 1) * D]
        return jnp.concatenate([m] * nt, axis=0)

    sh1, sc1, gt1, sh2, sc2 = rows(0), rows(1), rows(2), rows(3), rows(4)
    u, glu = _front(x, sc1, sh1, g_mix_ref[...], w_in_ref)
    glu_ref[...] = glu

    ub = u.astype(BF16)
    for q in range(4):
        r = jnp.dot(ub[:, q * LANES:(q + 1) * LANES], wb_ref[q], preferred_element_type=F32)
        sre[:, q * SSM_W:(q + 1) * SSM_W] = r[:, :SSM_W]
        sim[:, q * SSM_W:(q + 1) * SSM_W] = r[:, SSM_W:]

    ar = a_ref[0:1, :]
    ai = a_ref[1:2, :]
    hr = h0r_ref[...]
    hi = h0i_ref[...]
    for t in range(nt):
        rs = pl.ds(t * nb, nb)
        nr = ar * hr - ai * hi + sre[rs, :]
        ni = ar * hi + ai * hr + sim[rs, :]
        sre[rs, :] = nr
        sim[rs, :] = ni
        hr, hi = nr, ni
    hr_ref[...] = hr
    hi_ref[...] = hi

    ys = []
    for q in range(4):
        acc = None
        for jj in range(4):
            j = 4 * q + jj
            lhs = jnp.concatenate([sre[:, j * LANES:(j + 1) * LANES],
                                   sim[:, j * LANES:(j + 1) * LANES]], axis=-1)
            d = jnp.dot(lhs.astype(BF16), wc_ref[j], preferred_element_type=F32)
            acc = d if acc is None else acc + d
        ys.append(acc)
    y_lin = jnp.concatenate(ys, axis=-1)
    ns = _ssm_out(y_lin, u, dsk_ref[...], w_glu_ref, b_glu_ref[...], g_os_ref[...])

    def ext(jrow):
        if jrow < CB:
            return cache_ref[jrow]
        return glu[(jrow - CB) * nb:(jrow - CB + 1) * nb, :]

    convs = []
    for t in range(nt):
        acc = None
        for k in range(KW):
            term = w_dw_ref[k:k + 1, :] * ext(t + k)
            acc = term if acc is None else acc + term
        convs.append(acc)
    conv = jnp.concatenate(convs, axis=0)
    nc = _conv_out(conv, b_dw_ref[...], ln_g_ref[...], ln_b_ref[...], g_oc_ref[...])

    x1, n2, rt, cnt = _tail(x, ns, nc, gt1, sc2, sh2, w_out_ref, g_ffn_ref[...], w_r_ref, b_r_ref[...])
    x1_ref[...] = x1
    n2_ref[...] = n2.astype(BF16)
    rt_ref[...] = rt
    cnt_ref[...] = cnt


def _sample_mixer_call(x_tm, mod_s, h0r, h0i, cache_tm, wts, nb, nt):
    n = nb * nt
    kern = functools.partial(_sample_mixer_kernel, nb=nb, nt=nt)
    out_shape = (jax.ShapeDtypeStruct((n, D), F32),
                 jax.ShapeDtypeStruct((n, D), BF16),
                 jax.ShapeDtypeStruct((8, n), F32),
                 jax.ShapeDtypeStruct((1, LANES), F32),
                 jax.ShapeDtypeStruct((nb, NSTATE), F32),
                 jax.ShapeDtypeStruct((nb, NSTATE), F32),
                 jax.ShapeDtypeStruct((n, CONV_CH), F32))
    scratch = [pltpu.VMEM((n, NSTATE), F32), pltpu.VMEM((n, NSTATE), F32)]
    return pl.pallas_call(
        kern, out_shape=out_shape, scratch_shapes=scratch,
        compiler_params=pltpu.CompilerParams(vmem_limit_bytes=VMEM_LIMIT),
        name="sample_mixer",
    )(x_tm, mod_s, h0r, h0i, cache_tm, *wts)


ROW_ALIGN = 16
MOE_TD = 512
MOE_BR = MOE_TD * 2 + NE * ROW_ALIGN
MOE_TG = 512


def _slot_positions(rt):
    t = rt.shape[1]
    e0 = rt[0:1, :]
    e1 = rt[1:2, :]
    sub = lax.broadcasted_iota(jnp.int32, (LANES, t), 0).astype(F32)
    a0 = jnp.where(sub == e0, 1.0, 0.0)
    a1 = jnp.where(sub == e1, 1.0, 0.0)
    at = a0 + a1
    r = lax.broadcasted_iota(jnp.int32, (t, t), 0)
    c = lax.broadcasted_iota(jnp.int32, (t, t), 1)
    before = jnp.where(r < c, 1.0, 0.0).astype(BF16)
    rank = jnp.dot(at.astype(BF16), before, preferred_element_type=F32)
    cnt = jnp.sum(at, axis=1, keepdims=True)
    cnt_al = jnp.ceil(cnt * (1.0 / ROW_ALIGN)) * float(ROW_ALIGN)
    er = lax.broadcasted_iota(jnp.int32, (LANES, LANES), 0)
    ec = lax.broadcasted_iota(jnp.int32, (LANES, LANES), 1)
    lower = jnp.where(ec < er, 1.0, 0.0)
    base = jnp.dot(lower, jnp.broadcast_to(cnt_al, (LANES, LANES)), preferred_element_type=F32,
                   precision=lax.Precision.HIGHEST)[:, 0:1]
    slot = rank + base
    pos0 = jnp.sum(a0 * slot, axis=0, keepdims=True)
    pos1 = jnp.sum(a1 * slot, axis=0, keepdims=True)
    return pos0, pos1


def _segment_copies(tab_ref, t, n_tiles, buf, hbm, sems, to_hbm):
    def copy(e):
        n = pl.multiple_of(tab_ref[t * NE + e], ROW_ALIGN)
        b = pl.multiple_of(tab_ref[(n_tiles + t) * NE + e], ROW_ALIGN)
        d = pl.multiple_of(tab_ref[(2 * n_tiles + t) * NE + e], ROW_ALIGN)
        if to_hbm:
            cp = pltpu.make_async_copy(buf.at[pl.ds(b, n)], hbm.at[pl.ds(d, n)], sems.at[e])
        else:
            cp = pltpu.make_async_copy(hbm.at[pl.ds(d, n)], buf.at[pl.ds(b, n)], sems.at[e])
        return n, cp

    for e in range(NE):
        n, cp = copy(e)

        @pl.when(n > 0)
        def _():
            cp.start()

    for e in range(NE):
        n, cp = copy(e)

        @pl.when(n > 0)
        def _():
            cp.wait()


def _dispatch_kernel(tab_ref, rtp_ref, n2p_ref, rts_ref, n2s_ref, xs_ref, buf, zbuf, sems, zsem,
                     *, n_tiles, n_ptiles, n_gtiles):
    t = pl.program_id(0)

    @pl.when(t == 0)
    def _():
        zbuf[...] = jnp.zeros_like(zbuf)
        for phase in range(2):
            for e in range(NE):
                d = pl.multiple_of(tab_ref[3 * n_tiles * NE + e], ROW_ALIGN)
                n = pl.multiple_of(tab_ref[3 * n_tiles * NE + NE + e], ROW_ALIGN)
                cp = pltpu.make_async_copy(zbuf.at[pl.ds(0, n)], xs_ref.at[pl.ds(d, n)], sems.at[e])

                @pl.when(n > 0)
                def _():
                    if phase == 0:
                        cp.start()
                    else:
                        cp.wait()

        first_free = tab_ref[3 * n_tiles * NE + 2 * NE]

        def fill_copy(j):
            d = pl.multiple_of(j * MOE_TG, MOE_TG)
            return pltpu.make_async_copy(zbuf, xs_ref.at[pl.ds(d, MOE_TG)], zsem)

        def fill_start(j, carry):
            fill_copy(j).start()
            return carry

        def fill_wait(j, carry):
            fill_copy(j).wait()
            return carry

        lax.fori_loop(first_free, n_gtiles, fill_start, 0)
        lax.fori_loop(first_free, n_gtiles, fill_wait, 0)

    is_sample = t >= n_ptiles
    rt = jnp.where(is_sample, rts_ref[...], rtp_ref[...])
    n2 = jnp.where(is_sample, n2s_ref[...], n2p_ref[...])
    pos0, pos1 = _slot_positions(rt)
    row = lax.broadcasted_iota(jnp.int32, (MOE_BR, MOE_TD), 0).astype(F32)
    q = (jnp.where(row == pos0, 1.0, 0.0) + jnp.where(row == pos1, 1.0, 0.0)).astype(BF16)
    buf[...] = jnp.dot(q, n2, preferred_element_type=F32).astype(BF16)
    _segment_copies(tab_ref, t, n_tiles, buf, xs_ref, sems, to_hbm=True)


def _dispatch_call(tab, rt_p, n2_p, rt_s, n2_s, r_max):
    n_ptiles = n2_p.shape[0] // MOE_TD
    n_tiles = n_ptiles + n2_s.shape[0] // MOE_TD
    last_p = n_ptiles - 1
    return pl.pallas_call(
        functools.partial(_dispatch_kernel, n_tiles=n_tiles, n_ptiles=n_ptiles,
                          n_gtiles=r_max // MOE_TG),
        grid_spec=pltpu.PrefetchScalarGridSpec(
            num_scalar_prefetch=1, grid=(n_tiles,),
            in_specs=[pl.BlockSpec((8, MOE_TD), lambda t, tab: (0, jnp.minimum(t, last_p))),
                      pl.BlockSpec((MOE_TD, D), lambda t, tab: (jnp.minimum(t, last_p), 0)),
                      pl.BlockSpec((8, MOE_TD), lambda t, tab: (0, 0)),
                      pl.BlockSpec((MOE_TD, D), lambda t, tab: (0, 0))],
            out_specs=pl.BlockSpec(memory_space=pl.ANY),
            scratch_shapes=[pltpu.VMEM((MOE_BR, D), BF16),
                            pltpu.VMEM((MOE_TG, D), BF16),
                            pltpu.SemaphoreType.DMA((NE,)),
                            pltpu.SemaphoreType.DMA(())]),
        out_shape=jax.ShapeDtypeStruct((r_max, D), BF16),
        compiler_params=pltpu.CompilerParams(dimension_semantics=("arbitrary",),
                                             vmem_limit_bytes=VMEM_LIMIT),
        name="moe_dispatch",
    )(tab, rt_p, n2_p, rt_s, n2_s)


def _experts_kernel(te_ref, first_ref, nact_ref, xs_ref, w1_ref, w3_ref, w2_ref, ys_ref,
                    w1b, w3b, w2b):
    j = pl.program_id(0)

    @pl.when(j < nact_ref[0])
    def _():
        @pl.when(first_ref[j] == 1)
        def _():
            w1b[...] = w1_ref[...].astype(BF16)
            w3b[...] = w3_ref[...].astype(BF16)
            w2b[...] = w2_ref[...].astype(BF16)

        x = xs_ref[...]
        a = jnp.dot(x, w1b[...], preferred_element_type=F32)
        b = jnp.dot(x, w3b[...], preferred_element_type=F32)
        hid = a * jax.nn.sigmoid(a) * b
        y = jnp.dot(hid.astype(BF16), w2b[...], preferred_element_type=F32)
        ys_ref[...] = y.astype(BF16)

    @pl.when(j >= nact_ref[0])
    def _():
        ys_ref[...] = jnp.zeros_like(ys_ref)


def _experts_call(tile_expert, tile_first, n_active, xs, w1, w3, w2):
    r_max = xs.shape[0]
    n_steps = r_max // MOE_TG

    def row_map(j, te, first, nact):
        return (j, 0)

    def w_map(j, te, first, nact):
        return (te[j], 0, 0)

    return pl.pallas_call(
        _experts_kernel,
        grid_spec=pltpu.PrefetchScalarGridSpec(
            num_scalar_prefetch=3, grid=(n_steps,),
            in_specs=[pl.BlockSpec((MOE_TG, D), row_map),
                      pl.BlockSpec((None, D, DE), w_map),
                      pl.BlockSpec((None, D, DE), w_map),
                      pl.BlockSpec((None, DE, D), w_map)],
            out_specs=pl.BlockSpec((MOE_TG, D), row_map),
            scratch_shapes=[pltpu.VMEM((D, DE), BF16), pltpu.VMEM((D, DE), BF16),
                            pltpu.VMEM((DE, D), BF16)]),
        out_shape=jax.ShapeDtypeStruct((r_max, D), BF16),
        compiler_params=pltpu.CompilerParams(dimension_semantics=("arbitrary",),
                                             vmem_limit_bytes=VMEM_LIMIT),
        name="moe_experts",
    )(tile_expert, tile_first, n_active, xs, w1, w3, w2)


def _combine_kernel(tab_ref, rt_ref, x1_ref, gt2_ref, gf_ref, ys_ref, y_ref, buf, sems,
                    *, n_tiles, t_off):
    t = pl.program_id(0) + t_off

    @pl.when(pl.program_id(0) == 0)
    def _():
        buf[...] = jnp.zeros_like(buf)

    _segment_copies(tab_ref, t, n_tiles, buf, ys_ref, sems, to_hbm=False)

    rt = rt_ref[...]
    pos0, pos1 = _slot_positions(rt)
    row = lax.broadcasted_iota(jnp.int32, (MOE_BR, MOE_TD), 0).astype(F32)
    m0 = row == pos0
    m1 = row == pos1
    q = (jnp.where(m0, 1.0, 0.0) + jnp.where(m1, 1.0, 0.0)).astype(BF16)
    gw = jnp.sum(jnp.where(m0, rt[2:3, :], 0.0) + jnp.where(m1, rt[3:4, :], 0.0),
                 axis=1, keepdims=True)
    yv = (buf[...].astype(F32) * gw).astype(BF16)
    moe = lax.dot_general(q, yv, (((0,), (0,)), ((), ())), preferred_element_type=F32)
    xo = x1_ref[...] + gt2_ref[...] * moe
    y_ref[...] = _rms(xo) * gf_ref[...]


def _combine_call(tab, rt, x1, gt2, gt2_spec, g_final, ys, n_tiles, t_off):
    n_out_tiles = x1.shape[0] // MOE_TD
    return pl.pallas_call(
        functools.partial(_combine_kernel, n_tiles=n_tiles, t_off=t_off),
        grid_spec=pltpu.PrefetchScalarGridSpec(
            num_scalar_prefetch=1, grid=(n_out_tiles,),
            in_specs=[pl.BlockSpec((8, MOE_TD), lambda t, tab: (0, t)),
                      pl.BlockSpec((MOE_TD, D), lambda t, tab: (t, 0)),
                      gt2_spec,
                      pl.BlockSpec((1, D), lambda t, tab: (0, 0)),
                      pl.BlockSpec(memory_space=pl.ANY)],
            out_specs=pl.BlockSpec((MOE_TD, D), lambda t, tab: (t, 0)),
            scratch_shapes=[pltpu.VMEM((MOE_BR, D), BF16),
                            pltpu.SemaphoreType.DMA((NE,))]),
        out_shape=jax.ShapeDtypeStruct((n_out_tiles * MOE_TD, D), F32),
        compiler_params=pltpu.CompilerParams(dimension_semantics=("arbitrary",),
                                             vmem_limit_bytes=VMEM_LIMIT),
        name="moe_combine",
    )(tab, rt, x1, gt2, g_final.reshape(1, D), ys)


def _moe_plan(cnt, r_max):
    cnt8 = (cnt + ROW_ALIGN - 1) // ROW_ALIGN * ROW_ALIGN
    seg_rows = cnt8.sum(axis=0)
    seg_pad = (seg_rows + MOE_TG - 1) // MOE_TG * MOE_TG
    seg_start = jnp.cumsum(seg_pad) - seg_pad
    dst = seg_start[None, :] + jnp.cumsum(cnt8, axis=0) - cnt8
    boff = jnp.cumsum(cnt8, axis=1) - cnt8
    tile_end = jnp.cumsum(seg_pad // MOE_TG)
    n_active = tile_end[-1:].astype(jnp.int32)
    tab = jnp.concatenate([cnt8.ravel(), boff.ravel(), dst.ravel(),
                           seg_start + seg_rows, seg_pad - seg_rows, n_active]).astype(jnp.int32)
    j = jnp.arange(r_max // MOE_TG, dtype=jnp.int32)
    tile_expert = jnp.minimum(jnp.sum(j[:, None] >= tile_end[None, :], axis=1), NE - 1).astype(jnp.int32)
    tile_first = jnp.concatenate([jnp.ones((1,), jnp.int32),
                                  (tile_expert[1:] != tile_expert[:-1]).astype(jnp.int32)])
    return tab, tile_expert, tile_first, n_active


def kernel(x_prompt, x_sample, c_prompt, c_sample, state_ssm_re, state_ssm_im, cache_conv, w_ada, b_ada, g_norm_mix, w_in, ssm_a_re, ssm_a_im, ssm_log_dt, ssm_b_re, ssm_b_im, ssm_c_re, ssm_c_im, ssm_d, w_ssm_glu, b_ssm_glu, w_dw, b_dw, ln_conv_g, ln_conv_b, g_out_ssm, g_out_conv, w_out, g_norm_ffn, w_router_grp, b_router_grp, w_router_exp, b_router_exp, w_exp_gate, w_exp_up, w_exp_down, g_final):
    depth = w_ada.shape[0]
    assert depth == 1
    bsz, seq, _ = x_prompt.shape
    nb, nt, _ = x_sample.shape

    c_all = jnp.concatenate([c_prompt, c_sample], axis=0)
    mod = _mod_call(c_all, w_ada[0], b_ada[0])
    mod_p = mod[:bsz].reshape(bsz, 6, D)
    mod_s = mod[bsz:]

    ab_re, ab_im, bb_re, bb_im, c_im_neg = _ssm_prep_call(
        ssm_a_re[0], ssm_a_im[0], ssm_log_dt[0], ssm_b_re[0], ssm_b_im[0], ssm_c_im[0])
    wb, wc = _block_diag_weights(bb_re, bb_im, ssm_c_re[0], c_im_neg)
    a_tok = jnp.stack([ab_re.reshape(NCHUNK, LANES), ab_im.reshape(NCHUNK, LANES)])
    a_row = jnp.stack([ab_re.reshape(NSTATE), ab_im.reshape(NSTATE)])

    w_r = jnp.concatenate([w_router_exp[0].reshape(D, NE), w_router_grp[0],
                           jnp.zeros((D, LANES - NE - NG), F32)], axis=1)
    w_r_hi = w_r.astype(BF16)
    w_r = jnp.concatenate([w_r_hi, (w_r - w_r_hi.astype(F32)).astype(BF16)], axis=1)
    b_r = jnp.concatenate([b_router_exp[0].reshape(NE), b_router_grp[0],
                           jnp.zeros((LANES - NE - NG,), F32)]).reshape(1, LANES)
    w_dw_p = jnp.concatenate([w_dw[0], jnp.zeros((1, CONV_CH), F32)], axis=0)

    row = lambda v: v.reshape(1, -1)
    common_a = (row(g_norm_mix[0]), w_in[0].astype(BF16), wb)
    common_b = (wc, row(ssm_d[0].reshape(SSM_W)), w_ssm_glu[0].astype(BF16), row(b_ssm_glu[0]),
                w_dw_p, row(b_dw[0]), row(ln_conv_g[0]), row(ln_conv_b[0]),
                row(g_out_ssm[0]), row(g_out_conv[0]), w_out[0].astype(BF16),
                row(g_norm_ffn[0]), w_r, b_r)

    n_p = bsz * seq
    n_s = nb * nt
    n_all = n_p + n_s
    assert n_s == MOE_TD and seq % MOE_TD == 0 and MOE_TD % PROMPT_TC == 0
    wts_p = common_a + (a_tok,) + common_b
    x1_p, n2_p, rt_p, cnt_p, hr_p, hi_p, cache_p = _prompt_mixer_call(x_prompt, mod_p, wts_p, PROMPT_TC)

    x_tm = jnp.transpose(x_sample, (1, 0, 2)).reshape(nt * nb, D)
    cache_tm = jnp.transpose(cache_conv[0], (1, 0, 2))
    wts_s = common_a + (a_row,) + common_b
    x1_s, n2_s, rt_s, cnt_s, hr_s, hi_s, glu_s = _sample_mixer_call(
        x_tm, mod_s, state_ssm_re[0].reshape(nb, NSTATE), state_ssm_im[0].reshape(nb, NSTATE),
        cache_tm, wts_s, nb, nt)

    n_ptiles = n_p // MOE_TD
    n_tiles = n_all // MOE_TD
    r_max = -(-(2 * n_all + n_tiles * NE * (ROW_ALIGN - 1) + NE * (MOE_TG - ROW_ALIGN)) // MOE_TG) * MOE_TG
    cnt = jnp.concatenate([cnt_p.reshape(n_ptiles, MOE_TD // PROMPT_TC, LANES).sum(axis=1), cnt_s])
    tab, tile_expert, tile_first, n_active = _moe_plan(cnt[:, :NE].astype(jnp.int32), r_max)
    xs = _dispatch_call(tab, rt_p, n2_p, rt_s, n2_s, r_max)
    ys = _experts_call(tile_expert, tile_first, n_active, xs,
                       w_exp_gate[0], w_exp_up[0], w_exp_down[0])
    tiles_per_b = seq // MOE_TD
    gt2_p = mod_p[:, 5:6, :]
    y_p = _combine_call(tab, rt_p, x1_p, gt2_p,
                        pl.BlockSpec((None, 1, D), lambda t, tab: (t // tiles_per_b, 0, 0)),
                        g_final, ys, n_tiles, 0)
    gt2_s = jnp.tile(mod_s[:, 5 * D:], (nt, 1))
    y_s = _combine_call(tab, rt_s, x1_s, gt2_s,
                        pl.BlockSpec((n_s, D), lambda t, tab: (0, 0)),
                        g_final, ys, n_tiles, n_ptiles)

    y_prompt = y_p.reshape(bsz, seq, D)
    y_sample = jnp.transpose(y_s.reshape(nt, nb, D), (1, 0, 2))
    new_cache_s = jnp.concatenate(
        [cache_conv[0][:, nt:, :], jnp.transpose(glu_s.reshape(nt, nb, CONV_CH), (1, 0, 2))], axis=1)
    return (y_prompt, y_sample,
            hr_p.reshape(1, bsz, G, P), hi_p.reshape(1, bsz, G, P), cache_p[None],
            hr_s.reshape(1, nb, G, P), hi_s.reshape(1, nb, G, P), new_cache_s[None])
```

```python
import functools

import jax
import jax.numpy as jnp
import numpy as np
from jax import lax
from jax.experimental import pallas as pl
from jax.experimental.pallas import tpu as pltpu

F32 = jnp.float32
BF16 = jnp.bfloat16

D = 1024
SSM_W = 512
CONV_CH = 512
G = 32
H = 16
P = 64
KW = 31
CB = KW - 1
NE = 32
NG = 4
EPG = 8
DE = 512
EPS = 1e-6
LANES = 128
NSTATE = G * P
NCHUNK = NSTATE // LANES

PROMPT_TC = 512
VMEM_LIMIT = 56 * 1024 * 1024


def _rms(x):
    return x * lax.rsqrt(jnp.mean(x * x, axis=-1, keepdims=True) + EPS)


def _gelu_tanh(y):
    c = np.sqrt(2.0 / np.pi).astype(np.float32)
    return y * (0.5 * (1.0 + jnp.tanh(c * (y + 0.044715 * (y * y * y)))))


def _bdot(a, b):
    return jnp.dot(a.astype(BF16), b, preferred_element_type=F32)


def _mod_kernel(c_ref, w_ref, b_ref, o_ref):
    c = c_ref[...]
    s = c * jax.nn.sigmoid(c)
    o_ref[...] = jnp.dot(s, w_ref[...], preferred_element_type=F32,
                         precision=lax.Precision.HIGHEST) + b_ref[...]


def _mod_call(c_all, w_ada, b_ada):
    n = c_all.shape[0]
    tn = 512
    return pl.pallas_call(
        _mod_kernel,
        grid=(6 * D // tn,),
        in_specs=[pl.BlockSpec((n, D), lambda j: (0, 0)),
                  pl.BlockSpec((D, tn), lambda j: (0, j)),
                  pl.BlockSpec((1, tn), lambda j: (0, j))],
        out_specs=pl.BlockSpec((n, tn), lambda j: (0, j)),
        out_shape=jax.ShapeDtypeStruct((n, 6 * D), F32),
        compiler_params=pltpu.CompilerParams(dimension_semantics=("arbitrary",)),
        name="mod",
    )(c_all, w_ada, b_ada.reshape(1, 6 * D))


def _ssm_prep_kernel(a_re, a_im, log_dt, b_re, b_im, c_im,
                     ab_re_o, ab_im_o, bb_re_o, bb_im_o, cneg_o):
    lam_re = jnp.minimum(a_re[...], -1e-4)
    lam_im = a_im[...]
    dt = jnp.exp(log_dt[...])
    mag = jnp.exp(lam_re * dt)
    ab_re = mag * jnp.cos(lam_im * dt)
    ab_im = mag * jnp.sin(lam_im * dt)
    den = lam_re * lam_re + lam_im * lam_im
    num_re = ab_re - 1.0
    coef_re = (num_re * lam_re + ab_im * lam_im) / den
    coef_im = (ab_im * lam_re - num_re * lam_im) / den
    ab_re_o[...] = ab_re
    ab_im_o[...] = ab_im
    br = b_re[...]
    bi = b_im[...]
    bb_re_o[...] = coef_re * br - coef_im * bi
    bb_im_o[...] = coef_re * bi + coef_im * br
    cneg_o[...] = -c_im[...]


def _ssm_prep_call(a_re, a_im, log_dt, b_re, b_im, c_im):
    flat = lambda v: v.reshape(1, NSTATE)
    b_hs = lambda v: jnp.transpose(v, (2, 0, 1)).reshape(H, NSTATE)
    dt_row = jnp.broadcast_to(log_dt[:, None], (G, P)).reshape(1, NSTATE)
    ab_re, ab_im, bb_re, bb_im, cneg = pl.pallas_call(
        _ssm_prep_kernel,
        out_shape=(jax.ShapeDtypeStruct((1, NSTATE), F32), jax.ShapeDtypeStruct((1, NSTATE), F32),
                   jax.ShapeDtypeStruct((H, NSTATE), F32), jax.ShapeDtypeStruct((H, NSTATE), F32),
                   jax.ShapeDtypeStruct((G * H, P), F32)),
        name="ssm_prep",
    )(flat(a_re), flat(a_im), dt_row, b_hs(b_re), b_hs(b_im), c_im.reshape(G * H, P))
    ghp = lambda v: jnp.transpose(v.reshape(H, G, P), (1, 0, 2))
    return (ab_re.reshape(G, P), ab_im.reshape(G, P), ghp(bb_re), ghp(bb_im), cneg.reshape(G, H, P))


def _block_diag_weights(bb_re, bb_im, c_re, c_im_neg):
    eye8 = jnp.eye(8, dtype=F32)
    eye4 = jnp.eye(4, dtype=F32)
    eye2 = jnp.eye(2, dtype=F32)

    def wb_part(bb):
        x = bb.reshape(4, 8, H, P)
        return jnp.einsum('qghp,gk->qghkp', x, eye8).reshape(4, 8 * H, 8 * P)

    wb = jnp.concatenate([wb_part(bb_re), wb_part(bb_im)], axis=-1).astype(BF16)

    def wc_part(c):
        x = c.reshape(4, 4, 2, H, P)
        y = jnp.einsum('qjghp,jk,gl->qjgpklh', x, eye4, eye2)
        return y.reshape(NCHUNK, 2 * P, 4 * 2 * H)

    wc = jnp.concatenate([wc_part(c_re), wc_part(c_im_neg)], axis=1).astype(BF16)
    return wb, wc


def _front(x, sc1, sh1, g_mix, w_in_ref):
    n = _rms(x) * g_mix * (1.0 + sc1) + sh1
    proj = _bdot(n, w_in_ref[...])
    u = proj[:, :SSM_W]
    glu = proj[:, SSM_W:SSM_W + CONV_CH] * jax.nn.sigmoid(proj[:, SSM_W + CONV_CH:])
    return u, glu


def _ssm_out(y_lin, u, d_skip, w_glu_ref, b_glu, g_out_ssm):
    y = _gelu_tanh(y_lin + d_skip * u)
    ys = y * jax.nn.sigmoid(_bdot(y, w_glu_ref[...]) + b_glu)
    return _rms(ys) * g_out_ssm


def _conv_out(conv, b_dw, ln_g, ln_b, g_out_conv):
    c = conv + b_dw
    mu = jnp.mean(c, axis=-1, keepdims=True)
    cc = c - mu
    var = jnp.mean(cc * cc, axis=-1, keepdims=True)
    ln = cc * lax.rsqrt(var + EPS) * ln_g + ln_b
    yc = ln * jax.nn.sigmoid(ln)
    return _rms(yc) * g_out_conv


def _route(n2, w_r_ref, b_r):
    rows = n2.shape[0]
    n_hi = n2.astype(BF16)
    n_lo = (n2 - n_hi.astype(F32)).astype(BF16)
    parts = jnp.dot(jnp.concatenate([n_hi, n_lo], axis=0), w_r_ref[...], preferred_element_type=F32)
    lg = (parts[:rows, :LANES] + parts[:rows, LANES:]) + (parts[rows:, :LANES] + parts[rows:, LANES:]) + b_r
    lane = lax.broadcasted_iota(jnp.int32, (rows, LANES), 1).astype(F32)
    ninf = -jnp.inf
    big = 1e9
    gmask = jnp.logical_and(lane >= NE, lane < NE + NG)
    gl = jnp.where(gmask, lg, ninf)
    gmax = jnp.max(gl, axis=-1, keepdims=True)
    gsum = jnp.sum(jnp.where(gmask, jnp.exp(gl - gmax), 0.0), axis=-1, keepdims=True)
    p_top = 1.0 / gsum
    gi = jnp.min(jnp.where(gl == gmax, lane, big), axis=-1, keepdims=True) - NE
    lo = gi * EPG
    emask = jnp.logical_and(lane >= lo, lane < lo + EPG)
    el = jnp.where(emask, lg, ninf)
    m1 = jnp.max(el, axis=-1, keepdims=True)
    i1 = jnp.min(jnp.where(el == m1, lane, big), axis=-1, keepdims=True)
    el2 = jnp.where(lane == i1, ninf, el)
    m2 = jnp.max(el2, axis=-1, keepdims=True)
    i2 = jnp.min(jnp.where(el2 == m2, lane, big), axis=-1, keepdims=True)
    e2 = jnp.exp(m2 - m1)
    den = 1.0 + e2
    w1 = p_top / den
    w2 = p_top * e2 / den
    cnt = jnp.sum(jnp.where(lane == i1, 1.0, 0.0) + jnp.where(lane == i2, 1.0, 0.0),
                  axis=0, keepdims=True)
    cols = (jnp.where(lane == 0.0, i1, 0.0) + jnp.where(lane == 1.0, i2, 0.0)
            + jnp.where(lane == 2.0, w1, 0.0) + jnp.where(lane == 3.0, w2, 0.0))
    return cols.T[0:8, :], cnt


def _tail(x, ns, nc, gt1, sc2, sh2, w_out_ref, g_ffn, w_r_ref, b_r):
    merged = _bdot(jnp.concatenate([ns, nc], axis=-1), w_out_ref[...])
    x1 = x + gt1 * merged
    n2 = _rms(x1) * g_ffn * (1.0 + sc2) + sh2
    rt, cnt = _route(n2, w_r_ref, b_r)
    return x1, n2, rt, cnt


def _prompt_mixer_kernel(x_ref, mod_ref, g_mix_ref, w_in_ref, wb_ref, a_ref, wc_ref, dsk_ref,
                         w_glu_ref, b_glu_ref, w_dw_ref, b_dw_ref, ln_g_ref, ln_b_ref,
                         g_os_ref, g_oc_ref, w_out_ref, g_ffn_ref, w_r_ref, b_r_ref,
                         x1_ref, n2_ref, rt_ref, cnt_ref, hr_ref, hi_ref, cache_ref,
                         sre, sim, hst, ebuf, eshift, *, tc, pt):
    c = pl.program_id(1)
    nc_chunks = pl.num_programs(1)

    @pl.when(c == 0)
    def _():
        hst[...] = jnp.zeros_like(hst)
        ebuf[pl.ds(0, 32), :] = jnp.zeros((32, CONV_CH), F32)

    x = x_ref[...]
    mod = mod_ref[...]
    sh1, sc1, gt1 = mod[0:1], mod[1:2], mod[2:3]
    sh2, sc2 = mod[3:4], mod[4:5]

    u, glu = _front(x, sc1, sh1, g_mix_ref[...], w_in_ref)
    ebuf[pl.ds(32, tc), :] = glu

    ub = u.astype(BF16)
    for q in range(4):
        r = jnp.dot(ub[:, q * LANES:(q + 1) * LANES], wb_ref[q], preferred_element_type=F32)
        for k in range(4):
            j = 4 * q + k
            sre[pl.ds(j * pt, tc), :] = r[:, k * LANES:(k + 1) * LANES]
            sim[pl.ds(j * pt, tc), :] = r[:, SSM_W + k * LANES:SSM_W + (k + 1) * LANES]

    ar0, ar1 = a_ref[0, 0:8, :], a_ref[0, 8:16, :]
    ai0, ai1 = a_ref[1, 0:8, :], a_ref[1, 8:16, :]

    def step(t, carry):
        hr0, hr1, hi0, hi1 = carry
        i0 = pl.ds(t, 8, stride=pt)
        i1 = pl.ds(t + 8 * pt, 8, stride=pt)
        nr0 = ar0 * hr0 - ai0 * hi0 + sre[i0, :]
        ni0 = ar0 * hi0 + ai0 * hr0 + sim[i0, :]
        nr1 = ar1 * hr1 - ai1 * hi1 + sre[i1, :]
        ni1 = ar1 * hi1 + ai1 * hr1 + sim[i1, :]
        sre[i0, :] = nr0
        sim[i0, :] = ni0
        sre[i1, :] = nr1
        sim[i1, :] = ni1
        return nr0, nr1, ni0, ni1

    init = (hst[0, 0:8, :], hst[0, 8:16, :], hst[1, 0:8, :], hst[1, 8:16, :])
    hr0, hr1, hi0, hi1 = lax.fori_loop(0, tc, step, init)
    hst[0, 0:8, :] = hr0
    hst[0, 8:16, :] = hr1
    hst[1, 0:8, :] = hi0
    hst[1, 8:16, :] = hi1

    @pl.when(c == nc_chunks - 1)
    def _():
        hr_ref[...] = hst[0]
        hi_ref[...] = hst[1]

    ys = []
    for q in range(4):
        acc = None
        for jj in range(4):
            j = 4 * q + jj
            lhs = jnp.concatenate([sre[pl.ds(j * pt, tc), :], sim[pl.ds(j * pt, tc), :]], axis=-1)
            d = jnp.dot(lhs.astype(BF16), wc_ref[j], preferred_element_type=F32)
            acc = d if acc is None else acc + d
        ys.append(acc)
    y_lin = jnp.concatenate(ys, axis=-1)
    ns = _ssm_out(y_lin, u, dsk_ref[...], w_glu_ref, b_glu_ref[...], g_os_ref[...])

    for s in range(1, 8):
        eshift[s - 1, pl.ds(0, tc + 24), :] = ebuf[pl.ds(s, tc + 24), :]
    rb = 64
    convs = []
    for r0 in range(0, tc, rb):
        acc = None
        for k in range(KW):
            a8, s = (k + 2) // 8 * 8, (k + 2) % 8
            win = ebuf[pl.ds(r0 + a8, rb), :] if s == 0 else eshift[s - 1, pl.ds(r0 + a8, rb), :]
            term = w_dw_ref[k:k + 1, :] * win
            acc = term if acc is None else acc + term
        convs.append(acc)
    conv = jnp.concatenate(convs, axis=0)
    nc = _conv_out(conv, b_dw_ref[...], ln_g_ref[...], ln_b_ref[...], g_oc_ref[...])

    @pl.when(c == nc_chunks - 1)
    def _():
        cache_ref[...] = ebuf[pl.ds(tc + 2, CB), :]

    ebuf[pl.ds(0, 32), :] = ebuf[pl.ds(tc, 32), :]

    x1, n2, rt, cnt = _tail(x, ns, nc, gt1, sc2, sh2, w_out_ref, g_ffn_ref[...], w_r_ref, b_r_ref[...])
    x1_ref[...] = x1
    n2_ref[...] = n2.astype(BF16)
    rt_ref[...] = rt
    cnt_ref[...] = cnt


def _const_spec(shape):
    nd = len(shape)
    return pl.BlockSpec(shape, lambda b, c: (0,) * nd)


def _prompt_mixer_call(x, mod6, wts, tc):
    bsz, t, _ = x.shape
    n_all = bsz * t
    pt = tc + 8
    nc = t // tc
    kern = functools.partial(_prompt_mixer_kernel, tc=tc, pt=pt)
    in_specs = [pl.BlockSpec((None, tc, D), lambda b, c: (b, c, 0)),
                pl.BlockSpec((None, 6, D), lambda b, c: (b, 0, 0))]
    in_specs += [_const_spec(w.shape) for w in wts]
    out_shape = (jax.ShapeDtypeStruct((n_all, D), F32),
                 jax.ShapeDtypeStruct((n_all, D), BF16),
                 jax.ShapeDtypeStruct((8, n_all), F32),
                 jax.ShapeDtypeStruct((n_all // tc, 1, LANES), F32),
                 jax.ShapeDtypeStruct((bsz, NCHUNK, LANES), F32),
                 jax.ShapeDtypeStruct((bsz, NCHUNK, LANES), F32),
                 jax.ShapeDtypeStruct((bsz, CB, CONV_CH), F32))
    out_specs = (pl.BlockSpec((tc, D), lambda b, c: (b * nc + c, 0)),
                 pl.BlockSpec((tc, D), lambda b, c: (b * nc + c, 0)),
                 pl.BlockSpec((8, tc), lambda b, c: (0, b * nc + c)),
                 pl.BlockSpec((None, 1, LANES), lambda b, c: (b * nc + c, 0, 0)),
                 pl.BlockSpec((None, NCHUNK, LANES), lambda b, c: (b, 0, 0)),
                 pl.BlockSpec((None, NCHUNK, LANES), lambda b, c: (b, 0, 0)),
                 pl.BlockSpec((None, CB, CONV_CH), lambda b, c: (b, 0, 0)))
    scratch = [pltpu.VMEM((NCHUNK * pt, LANES), F32),
               pltpu.VMEM((NCHUNK * pt, LANES), F32),
               pltpu.VMEM((2, NCHUNK, LANES), F32),
               pltpu.VMEM((tc + 32, CONV_CH), F32),
               pltpu.VMEM((7, tc + 32, CONV_CH), F32)]
    return pl.pallas_call(
        kern, grid=(bsz, nc), in_specs=in_specs, out_specs=out_specs, out_shape=out_shape,
        scratch_shapes=scratch,
        compiler_params=pltpu.CompilerParams(dimension_semantics=("arbitrary", "arbitrary"),
                                             vmem_limit_bytes=VMEM_LIMIT),
        name="prompt_mixer",
    )(x, mod6, *wts)


def _sample_mixer_kernel(x_ref, mod_ref, h0r_ref, h0i_ref, cache_ref,
                         g_mix_ref, w_in_ref, wb_ref, a_ref, wc_ref, dsk_ref,
                         w_glu_ref, b_glu_ref, w_dw_ref, b_dw_ref, ln_g_ref, ln_b_ref,
                         g_os_ref, g_oc_ref, w_out_ref, g_ffn_ref, w_r_ref, b_r_ref,
                         x1_ref, n2_ref, rt_ref, cnt_ref, hr_ref, hi_ref, glu_ref,
                         sre, sim, *, nb, nt):
    x = x_ref[...]

    def rows(i):
        m = mod_ref[:, i * D:(i + 1) * D]
        return jnp.concatenate([m] * nt, axis=0)

    sh1, sc1, gt1, sh2, sc2 = rows(0), rows(1), rows(2), rows(3), rows(4)
    u, glu = _front(x, sc1, sh1, g_mix_ref[...], w_in_ref)
    glu_ref[...] = glu

    ub = u.astype(BF16)
    for q in range(4):
        r = jnp.dot(ub[:, q * LANES:(q + 1) * LANES], wb_ref[q], preferred_element_type=F32)
        sre[:, q * SSM_W:(q + 1) * SSM_W] = r[:, :SSM_W]
        sim[:, q * SSM_W:(q + 1) * SSM_W] = r[:, SSM_W:]

    ar = a_ref[0:1, :]
    ai = a_ref[1:2, :]
    hr = h0r_ref[...]
    hi = h0i_ref[...]
    for t in range(nt):
        rs = pl.ds(t * nb, nb)
        nr = ar * hr - ai * hi + sre[rs, :]
        ni = ar * hi + ai * hr + sim[rs, :]
        sre[rs, :] = nr
        sim[rs, :] = ni
        hr, hi = nr, ni
    hr_ref[...] = hr
    hi_ref[...] = hi

    ys = []
    for q in range(4):
        acc = None
        for jj in range(4):
            j = 4 * q + jj
            lhs = jnp.concatenate([sre[:, j * LANES:(j + 1) * LANES],
                                   sim[:, j * LANES:(j + 1) * LANES]], axis=-1)
            d = jnp.dot(lhs.astype(BF16), wc_ref[j], preferred_element_type=F32)
            acc = d if acc is None else acc + d
        ys.append(acc)
    y_lin = jnp.concatenate(ys, axis=-1)
    ns = _ssm_out(y_lin, u, dsk_ref[...], w_glu_ref, b_glu_ref[...], g_os_ref[...])

    def ext(jrow):
        if jrow < CB:
            return cache_ref[jrow]
        return glu[(jrow - CB) * nb:(jrow - CB + 1) * nb, :]

    convs = []
    for t in range(nt):
        acc = None
        for k in range(KW):
            term = w_dw_ref[k:k + 1, :] * ext(t + k)
            acc = term if acc is None else acc + term
        convs.append(acc)
    conv = jnp.concatenate(convs, axis=0)
    nc = _conv_out(conv, b_dw_ref[...], ln_g_ref[...], ln_b_ref[...], g_oc_ref[...])

    x1, n2, rt, cnt = _tail(x, ns, nc, gt1, sc2, sh2, w_out_ref, g_ffn_ref[...], w_r_ref, b_r_ref[...])
    x1_ref[...] = x1
    n2_ref[...] = n2.astype(BF16)
    rt_ref[...] = rt
    cnt_ref[...] = cnt


def _sample_mixer_call(x_tm, mod_s, h0r, h0i, cache_tm, wts, nb, nt):
    n = nb * nt
    kern = functools.partial(_sample_mixer_kernel, nb=nb, nt=nt)
    out_shape = (jax.ShapeDtypeStruct((n, D), F32),
                 jax.ShapeDtypeStruct((n, D), BF16),
                 jax.ShapeDtypeStruct((8, n), F32),
                 jax.ShapeDtypeStruct((1, LANES), F32),
                 jax.ShapeDtypeStruct((nb, NSTATE), F32),
                 jax.ShapeDtypeStruct((nb, NSTATE), F32),
                 jax.ShapeDtypeStruct((n, CONV_CH), F32))
    scratch = [pltpu.VMEM((n, NSTATE), F32), pltpu.VMEM((n, NSTATE), F32)]
    return pl.pallas_call(
        kern, out_shape=out_shape, scratch_shapes=scratch,
        compiler_params=pltpu.CompilerParams(vmem_limit_bytes=VMEM_LIMIT),
        name="sample_mixer",
    )(x_tm, mod_s, h0r, h0i, cache_tm, *wts)


ROW_ALIGN = 16
MOE_TD = 512
MOE_BR = MOE_TD * 2 + NE * ROW_ALIGN
MOE_TG = 512


def _slot_positions(rt):
    t = rt.shape[1]
    e0 = rt[0:1, :]
    e1 = rt[1:2, :]
    sub = lax.broadcasted_iota(jnp.int32, (LANES, t), 0).astype(F32)
    a0 = jnp.where(sub == e0, 1.0, 0.0)
    a1 = jnp.where(sub == e1, 1.0, 0.0)
    at = a0 + a1
    r = lax.broadcasted_iota(jnp.int32, (t, t), 0)
    c = lax.broadcasted_iota(jnp.int32, (t, t), 1)
    before = jnp.where(r < c, 1.0, 0.0).astype(BF16)
    rank = jnp.dot(at.astype(BF16), before, preferred_element_type=F32)
    cnt = jnp.sum(at, axis=1, keepdims=True)
    cnt_al = jnp.ceil(cnt * (1.0 / ROW_ALIGN)) * float(ROW_ALIGN)
    er = lax.broadcasted_iota(jnp.int32, (LANES, LANES), 0)
    ec = lax.broadcasted_iota(jnp.int32, (LANES, LANES), 1)
    lower = jnp.where(ec < er, 1.0, 0.0)
    base = jnp.dot(lower, jnp.broadcast_to(cnt_al, (LANES, LANES)), preferred_element_type=F32,
                   precision=lax.Precision.HIGHEST)[:, 0:1]
    slot = rank + base
    pos0 = jnp.sum(a0 * slot, axis=0, keepdims=True)
    pos1 = jnp.sum(a1 * slot, axis=0, keepdims=True)
    return pos0, pos1


def _segment_copies(tab_ref, t, n_tiles, buf, hbm, sems, slot, to_hbm, wait):
    for e in range(NE):
        n = pl.multiple_of(tab_ref[t * NE + e], ROW_ALIGN)
        b = pl.multiple_of(tab_ref[(n_tiles + t) * NE + e], ROW_ALIGN)
        d = pl.multiple_of(tab_ref[(2 * n_tiles + t) * NE + e], ROW_ALIGN)
        vm = buf.at[slot, pl.ds(b, n)]
        hb = hbm.at[pl.ds(d, n)]
        cp = pltpu.make_async_copy(vm, hb, sems.at[slot, e]) if to_hbm else \
            pltpu.make_async_copy(hb, vm, sems.at[slot, e])

        @pl.when(n > 0)
        def _():
            if wait:
                cp.wait()
            else:
                cp.start()


def _tile_rows(tab_ref, t, n_tiles):
    return tab_ref[3 * n_tiles * NE + 2 * NE + 1 + t]


def _one_hot_rows(r0, nrows, pos0, pos1):
    row = (lax.broadcasted_iota(jnp.int32, (nrows, MOE_TD), 0) + r0).astype(F32)
    return row == pos0, row == pos1


MOE_BLK = 256


def _dispatch_kernel(tab_ref, rtp_ref, n2p_ref, rts_ref, n2s_ref, xs_ref, buf, zbuf, sems, zsem,
                     *, n_tiles, n_ptiles, n_gtiles):
    t = pl.program_id(0)
    slot = lax.rem(t, 2)
    first_free = tab_ref[3 * n_tiles * NE + 2 * NE]

    def fill_copy(j):
        d = pl.multiple_of(j * MOE_TG, MOE_TG)
        return pltpu.make_async_copy(zbuf, xs_ref.at[pl.ds(d, MOE_TG)], zsem)

    def fill_start(j, carry):
        fill_copy(j).start()
        return carry

    def fill_wait(j, carry):
        fill_copy(j).wait()
        return carry

    @pl.when(t == 0)
    def _():
        zbuf[...] = jnp.zeros_like(zbuf)
        for phase in range(2):
            for e in range(NE):
                d = pl.multiple_of(tab_ref[3 * n_tiles * NE + e], ROW_ALIGN)
                n = pl.multiple_of(tab_ref[3 * n_tiles * NE + NE + e], ROW_ALIGN)
                cp = pltpu.make_async_copy(zbuf.at[pl.ds(0, n)], xs_ref.at[pl.ds(d, n)], sems.at[1, e])

                @pl.when(n > 0)
                def _():
                    if phase == 0:
                        cp.start()
                    else:
                        cp.wait()

        lax.fori_loop(first_free, n_gtiles, fill_start, 0)

    @pl.when(t >= 2)
    def _():
        _segment_copies(tab_ref, t - 2, n_tiles, buf, xs_ref, sems, slot, to_hbm=True, wait=True)

    is_sample = t >= n_ptiles
    rt = jnp.where(is_sample, rts_ref[...], rtp_ref[...])
    n2 = jnp.where(is_sample, n2s_ref[...], n2p_ref[...])
    pos0, pos1 = _slot_positions(rt)
    used = _tile_rows(tab_ref, t, n_tiles)

    def group(r0, nrows):
        m0, m1 = _one_hot_rows(r0, nrows, pos0, pos1)
        q = (jnp.where(m0, 1.0, 0.0) + jnp.where(m1, 1.0, 0.0)).astype(BF16)
        buf[slot, pl.ds(r0, nrows), :] = jnp.dot(q, n2, preferred_element_type=F32).astype(BF16)

    group(0, 2 * MOE_TD)
    for r0 in range(2 * MOE_TD, MOE_BR, MOE_BLK):
        @pl.when(used > r0)
        def _():
            group(r0, MOE_BLK)

    _segment_copies(tab_ref, t, n_tiles, buf, xs_ref, sems, slot, to_hbm=True, wait=False)

    @pl.when(t == n_tiles - 1)
    def _():
        if n_tiles >= 2:
            _segment_copies(tab_ref, t - 1, n_tiles, buf, xs_ref, sems, 1 - slot, to_hbm=True, wait=True)
        _segment_copies(tab_ref, t, n_tiles, buf, xs_ref, sems, slot, to_hbm=True, wait=True)
        lax.fori_loop(first_free, n_gtiles, fill_wait, 0)


def _dispatch_call(tab, rt_p, n2_p, rt_s, n2_s, r_max):
    n_ptiles = n2_p.shape[0] // MOE_TD
    n_tiles = n_ptiles + n2_s.shape[0] // MOE_TD
    last_p = n_ptiles - 1
    return pl.pallas_call(
        functools.partial(_dispatch_kernel, n_tiles=n_tiles, n_ptiles=n_ptiles,
                          n_gtiles=r_max // MOE_TG),
        grid_spec=pltpu.PrefetchScalarGridSpec(
            num_scalar_prefetch=1, grid=(n_tiles,),
            in_specs=[pl.BlockSpec((8, MOE_TD), lambda t, tab: (0, jnp.minimum(t, last_p))),
                      pl.BlockSpec((MOE_TD, D), lambda t, tab: (jnp.minimum(t, last_p), 0)),
                      pl.BlockSpec((8, MOE_TD), lambda t, tab: (0, 0)),
                      pl.BlockSpec((MOE_TD, D), lambda t, tab: (0, 0))],
            out_specs=pl.BlockSpec(memory_space=pl.ANY),
            scratch_shapes=[pltpu.VMEM((2, MOE_BR, D), BF16),
                            pltpu.VMEM((MOE_TG, D), BF16),
                            pltpu.SemaphoreType.DMA((2, NE)),
                            pltpu.SemaphoreType.DMA(())]),
        out_shape=jax.ShapeDtypeStruct((r_max, D), BF16),
        compiler_params=pltpu.CompilerParams(dimension_semantics=("arbitrary",),
                                             vmem_limit_bytes=VMEM_LIMIT),
        name="moe_dispatch",
    )(tab, rt_p, n2_p, rt_s, n2_s)


def _experts_kernel(etab_ref, xs_ref, w1_ref, w3_ref, w2_ref, ys_ref,
                    w1b, w3b, w2b, xbuf, ybuf, zbuf, in_sem, out_sem, zsem, *, n_gtiles):
    e = pl.program_id(0)
    nt = etab_ref[e]
    base = etab_ref[NE + e]
    first_free = etab_ref[2 * NE]

    def fill_copy(j):
        d = pl.multiple_of(j * MOE_TG, MOE_TG)
        return pltpu.make_async_copy(zbuf, ys_ref.at[pl.ds(d, MOE_TG)], zsem)

    def fill_start(j, carry):
        fill_copy(j).start()
        return carry

    def fill_wait(j, carry):
        fill_copy(j).wait()
        return carry

    @pl.when(e == 0)
    def _():
        zbuf[...] = jnp.zeros_like(zbuf)
        lax.fori_loop(first_free, n_gtiles, fill_start, 0)

    def in_copy(i, slot):
        r = pl.multiple_of(base + i * MOE_TG, MOE_TG)
        return pltpu.make_async_copy(xs_ref.at[pl.ds(r, MOE_TG)], xbuf.at[slot], in_sem.at[slot])

    def out_copy(i, slot):
        r = pl.multiple_of(base + i * MOE_TG, MOE_TG)
        return pltpu.make_async_copy(ybuf.at[slot], ys_ref.at[pl.ds(r, MOE_TG)], out_sem.at[slot])

    @pl.when(nt > 0)
    def _():
        in_copy(0, 0).start()
        w1b[...] = w1_ref[...].astype(BF16)
        w3b[...] = w3_ref[...].astype(BF16)
        w2b[...] = w2_ref[...].astype(BF16)

        def tile(i, carry):
            slot = lax.rem(i, 2)

            @pl.when(i + 1 < nt)
            def _():
                in_copy(i + 1, 1 - slot).start()

            in_copy(i, slot).wait()

            @pl.when(i >= 2)
            def _():
                out_copy(i - 2, slot).wait()

            x = xbuf[slot]
            a = jnp.dot(x, w1b[...], preferred_element_type=F32)
            b = jnp.dot(x, w3b[...], preferred_element_type=F32)
            hid = a * jax.nn.sigmoid(a) * b
            y = jnp.dot(hid.astype(BF16), w2b[...], preferred_element_type=F32)
            ybuf[slot] = y.astype(BF16)
            out_copy(i, slot).start()
            return carry

        lax.fori_loop(0, nt, tile, 0)

        @pl.when(nt >= 2)
        def _():
            out_copy(nt - 2, lax.rem(nt, 2)).wait()

        out_copy(nt - 1, lax.rem(nt - 1, 2)).wait()

    @pl.when(e == NE - 1)
    def _():
        lax.fori_loop(first_free, n_gtiles, fill_wait, 0)


def _experts_call(etab, xs, w1, w3, w2):
    r_max = xs.shape[0]
    w_map = lambda e, etab: (e, 0, 0)
    return pl.pallas_call(
        functools.partial(_experts_kernel, n_gtiles=r_max // MOE_TG),
        grid_spec=pltpu.PrefetchScalarGridSpec(
            num_scalar_prefetch=1, grid=(NE,),
            in_specs=[pl.BlockSpec(memory_space=pl.ANY),
                      pl.BlockSpec((None, D, DE), w_map),
                      pl.BlockSpec((None, D, DE), w_map),
                      pl.BlockSpec((None, DE, D), w_map)],
            out_specs=pl.BlockSpec(memory_space=pl.ANY),
            scratch_shapes=[pltpu.VMEM((D, DE), BF16), pltpu.VMEM((D, DE), BF16),
                            pltpu.VMEM((DE, D), BF16),
                            pltpu.VMEM((2, MOE_TG, D), BF16), pltpu.VMEM((2, MOE_TG, D), BF16),
                            pltpu.VMEM((MOE_TG, D), BF16),
                            pltpu.SemaphoreType.DMA((2,)), pltpu.SemaphoreType.DMA((2,)),
                            pltpu.SemaphoreType.DMA(())]),
        out_shape=jax.ShapeDtypeStruct((r_max, D), BF16),
        compiler_params=pltpu.CompilerParams(dimension_semantics=("arbitrary",),
                                             vmem_limit_bytes=VMEM_LIMIT),
        name="moe_experts",
    )(etab, xs, w1, w3, w2)


def _combine_kernel(tab_ref, rt_ref, x1_ref, gt2_ref, gf_ref, ys_ref, y_ref, buf, acc, sems,
                    *, n_tiles, t_off):
    i = pl.program_id(0)
    t = i + t_off
    slot = lax.rem(i, 2)

    @pl.when(i == 0)
    def _():
        buf[...] = jnp.zeros_like(buf)
        _segment_copies(tab_ref, t, n_tiles, buf, ys_ref, sems, slot, to_hbm=False, wait=False)

    @pl.when(i + 1 < pl.num_programs(0))
    def _():
        _segment_copies(tab_ref, t + 1, n_tiles, buf, ys_ref, sems, 1 - slot, to_hbm=False, wait=False)

    _segment_copies(tab_ref, t, n_tiles, buf, ys_ref, sems, slot, to_hbm=False, wait=True)

    rt = rt_ref[...]
    pos0, pos1 = _slot_positions(rt)
    used = _tile_rows(tab_ref, t, n_tiles)

    def ungroup(r0, nrows):
        m0, m1 = _one_hot_rows(r0, nrows, pos0, pos1)
        q = (jnp.where(m0, 1.0, 0.0) + jnp.where(m1, 1.0, 0.0)).astype(BF16)
        gw = jnp.sum(jnp.where(m0, rt[2:3, :], 0.0) + jnp.where(m1, rt[3:4, :], 0.0),
                     axis=1, keepdims=True)
        yv = (buf[slot, pl.ds(r0, nrows), :].astype(F32) * gw).astype(BF16)
        return lax.dot_general(q, yv, (((0,), (0,)), ((), ())), preferred_element_type=F32)

    acc[...] = ungroup(0, 2 * MOE_TD)
    for r0 in range(2 * MOE_TD, MOE_BR, MOE_BLK):
        @pl.when(used > r0)
        def _():
            acc[...] += ungroup(r0, MOE_BLK)

    xo = x1_ref[...] + gt2_ref[...] * acc[...]
    y_ref[...] = _rms(xo) * gf_ref[...]


def _combine_call(tab, rt, x1, gt2, gt2_spec, g_final, ys, n_tiles, t_off):
    n_out_tiles = x1.shape[0] // MOE_TD
    return pl.pallas_call(
        functools.partial(_combine_kernel, n_tiles=n_tiles, t_off=t_off),
        grid_spec=pltpu.PrefetchScalarGridSpec(
            num_scalar_prefetch=1, grid=(n_out_tiles,),
            in_specs=[pl.BlockSpec((8, MOE_TD), lambda t, tab: (0, t)),
                      pl.BlockSpec((MOE_TD, D), lambda t, tab: (t, 0)),
                      gt2_spec,
                      pl.BlockSpec((1, D), lambda t, tab: (0, 0)),
                      pl.BlockSpec(memory_space=pl.ANY)],
            out_specs=pl.BlockSpec((MOE_TD, D), lambda t, tab: (t, 0)),
            scratch_shapes=[pltpu.VMEM((2, MOE_BR, D), BF16),
                            pltpu.VMEM((MOE_TD, D), F32),
                            pltpu.SemaphoreType.DMA((2, NE))]),
        out_shape=jax.ShapeDtypeStruct((n_out_tiles * MOE_TD, D), F32),
        compiler_params=pltpu.CompilerParams(dimension_semantics=("arbitrary",),
                                             vmem_limit_bytes=VMEM_LIMIT),
        name="moe_combine",
    )(tab, rt, x1, gt2, g_final.reshape(1, D), ys)


def _moe_plan(cnt):
    cnt8 = (cnt + ROW_ALIGN - 1) // ROW_ALIGN * ROW_ALIGN
    seg_rows = cnt8.sum(axis=0)
    seg_pad = (seg_rows + MOE_TG - 1) // MOE_TG * MOE_TG
    seg_start = jnp.cumsum(seg_pad) - seg_pad
    dst = seg_start[None, :] + jnp.cumsum(cnt8, axis=0) - cnt8
    boff = jnp.cumsum(cnt8, axis=1) - cnt8
    tile_end = jnp.cumsum(seg_pad // MOE_TG)
    n_active = tile_end[-1:].astype(jnp.int32)
    tab = jnp.concatenate([cnt8.ravel(), boff.ravel(), dst.ravel(),
                           seg_start + seg_rows, seg_pad - seg_rows, n_active,
                           cnt8.sum(axis=1)]).astype(jnp.int32)
    etab = jnp.concatenate([seg_pad // MOE_TG, seg_start, n_active]).astype(jnp.int32)
    return tab, etab


def kernel(x_prompt, x_sample, c_prompt, c_sample, state_ssm_re, state_ssm_im, cache_conv, w_ada, b_ada, g_norm_mix, w_in, ssm_a_re, ssm_a_im, ssm_log_dt, ssm_b_re, ssm_b_im, ssm_c_re, ssm_c_im, ssm_d, w_ssm_glu, b_ssm_glu, w_dw, b_dw, ln_conv_g, ln_conv_b, g_out_ssm, g_out_conv, w_out, g_norm_ffn, w_router_grp, b_router_grp, w_router_exp, b_router_exp, w_exp_gate, w_exp_up, w_exp_down, g_final):
    depth = w_ada.shape[0]
    assert depth == 1
    bsz, seq, _ = x_prompt.shape
    nb, nt, _ = x_sample.shape

    c_all = jnp.concatenate([c_prompt, c_sample], axis=0)
    mod = _mod_call(c_all, w_ada[0], b_ada[0])
    mod_p = mod[:bsz].reshape(bsz, 6, D)
    mod_s = mod[bsz:]

    ab_re, ab_im, bb_re, bb_im, c_im_neg = _ssm_prep_call(
        ssm_a_re[0], ssm_a_im[0], ssm_log_dt[0], ssm_b_re[0], ssm_b_im[0], ssm_c_im[0])
    wb, wc = _block_diag_weights(bb_re, bb_im, ssm_c_re[0], c_im_neg)
    a_tok = jnp.stack([ab_re.reshape(NCHUNK, LANES), ab_im.reshape(NCHUNK, LANES)])
    a_row = jnp.stack([ab_re.reshape(NSTATE), ab_im.reshape(NSTATE)])

    w_r = jnp.concatenate([w_router_exp[0].reshape(D, NE), w_router_grp[0],
                           jnp.zeros((D, LANES - NE - NG), F32)], axis=1)
    w_r_hi = w_r.astype(BF16)
    w_r = jnp.concatenate([w_r_hi, (w_r - w_r_hi.astype(F32)).astype(BF16)], axis=1)
    b_r = jnp.concatenate([b_router_exp[0].reshape(NE), b_router_grp[0],
                           jnp.zeros((LANES - NE - NG,), F32)]).reshape(1, LANES)
    w_dw_p = jnp.concatenate([w_dw[0], jnp.zeros((1, CONV_CH), F32)], axis=0)

    row = lambda v: v.reshape(1, -1)
    common_a = (row(g_norm_mix[0]), w_in[0].astype(BF16), wb)
    common_b = (wc, row(ssm_d[0].reshape(SSM_W)), w_ssm_glu[0].astype(BF16), row(b_ssm_glu[0]),
                w_dw_p, row(b_dw[0]), row(ln_conv_g[0]), row(ln_conv_b[0]),
                row(g_out_ssm[0]), row(g_out_conv[0]), w_out[0].astype(BF16),
                row(g_norm_ffn[0]), w_r, b_r)

    n_p = bsz * seq
    n_s = nb * nt
    n_all = n_p + n_s
    assert n_s == MOE_TD and seq % MOE_TD == 0 and MOE_TD % PROMPT_TC == 0
    wts_p = common_a + (a_tok,) + common_b
    x1_p, n2_p, rt_p, cnt_p, hr_p, hi_p, cache_p = _prompt_mixer_call(x_prompt, mod_p, wts_p, PROMPT_TC)

    x_tm = jnp.transpose(x_sample, (1, 0, 2)).reshape(nt * nb, D)
    cache_tm = jnp.transpose(cache_conv[0], (1, 0, 2))
    wts_s = common_a + (a_row,) + common_b
    x1_s, n2_s, rt_s, cnt_s, hr_s, hi_s, glu_s = _sample_mixer_call(
        x_tm, mod_s, state_ssm_re[0].reshape(nb, NSTATE), state_ssm_im[0].reshape(nb, NSTATE),
        cache_tm, wts_s, nb, nt)

    n_ptiles = n_p // MOE_TD
    n_tiles = n_all // MOE_TD
    r_max = -(-(2 * n_all + n_tiles * NE * (ROW_ALIGN - 1) + NE * (MOE_TG - ROW_ALIGN)) // MOE_TG) * MOE_TG
    cnt = jnp.concatenate([cnt_p.reshape(n_ptiles, MOE_TD // PROMPT_TC, LANES).sum(axis=1), cnt_s])
    tab, etab = _moe_plan(cnt[:, :NE].astype(jnp.int32))
    xs = _dispatch_call(tab, rt_p, n2_p, rt_s, n2_s, r_max)
    ys = _experts_call(etab, xs, w_exp_gate[0], w_exp_up[0], w_exp_down[0])
    tiles_per_b = seq // MOE_TD
    gt2_p = mod_p[:, 5:6, :]
    y_p = _combine_call(tab, rt_p, x1_p, gt2_p,
                        pl.BlockSpec((None, 1, D), lambda t, tab: (t // tiles_per_b, 0, 0)),
                        g_final, ys, n_tiles, 0)
    gt2_s = jnp.tile(mod_s[:, 5 * D:], (nt, 1))
    y_s = _combine_call(tab, rt_s, x1_s, gt2_s,
                        pl.BlockSpec((n_s, D), lambda t, tab: (0, 0)),
                        g_final, ys, n_tiles, n_ptiles)

    y_prompt = y_p.reshape(bsz, seq, D)
    y_sample = jnp.transpose(y_s.reshape(nt, nb, D), (1, 0, 2))
    new_cache_s = jnp.concatenate(
        [cache_conv[0][:, nt:, :], jnp.transpose(glu_s.reshape(nt, nb, CONV_CH), (1, 0, 2))], axis=1)
    return (y_prompt, y_sample,
            hr_p.reshape(1, bsz, G, P), hi_p.reshape(1, bsz, G, P), cache_p[None],
            hr_s.reshape(1, nb, G, P), hi_s.reshape(1, nb, G, P), new_cache_s[None])
```

```python
import functools

import jax
import jax.numpy as jnp
import numpy as np
from jax import lax
from jax.experimental import pallas as pl
from jax.experimental.pallas import tpu as pltpu

F32 = jnp.float32
BF16 = jnp.bfloat16

D = 1024
SSM_W = 512
CONV_CH = 512
G = 32
H = 16
P = 64
KW = 31
CB = KW - 1
NE = 32
NG = 4
EPG = 8
DE = 512
EPS = 1e-6
LANES = 128
NSTATE = G * P
NCHUNK = NSTATE // LANES

PROMPT_TC = 512
VMEM_LIMIT = 56 * 1024 * 1024


def _rms(x):
    return x * lax.rsqrt(jnp.mean(x * x, axis=-1, keepdims=True) + EPS)


def _gelu_tanh(y):
    c = np.sqrt(2.0 / np.pi).astype(np.float32)
    return y * (0.5 * (1.0 + jnp.tanh(c * (y + 0.044715 * (y * y * y)))))


def _bdot(a, b):
    return jnp.dot(a.astype(BF16), b, preferred_element_type=F32)


def _mod_kernel(c_ref, w_ref, b_ref, o_ref):
    c = c_ref[...]
    s = c * jax.nn.sigmoid(c)
    o_ref[...] = jnp.dot(s, w_ref[...], preferred_element_type=F32,
                         precision=lax.Precision.HIGHEST) + b_ref[...]


def _mod_call(c_all, w_ada, b_ada):
    n = c_all.shape[0]
    tn = 512
    return pl.pallas_call(
        _mod_kernel,
        grid=(6 * D // tn,),
        in_specs=[pl.BlockSpec((n, D), lambda j: (0, 0)),
                  pl.BlockSpec((D, tn), lambda j: (0, j)),
                  pl.BlockSpec((1, tn), lambda j: (0, j))],
        out_specs=pl.BlockSpec((n, tn), lambda j: (0, j)),
        out_shape=jax.ShapeDtypeStruct((n, 6 * D), F32),
        compiler_params=pltpu.CompilerParams(dimension_semantics=("arbitrary",)),
        name="mod",
    )(c_all, w_ada, b_ada.reshape(1, 6 * D))


def _ssm_prep_kernel(a_re, a_im, log_dt, b_re, b_im, c_im,
                     ab_re_o, ab_im_o, bb_re_o, bb_im_o, cneg_o):
    lam_re = jnp.minimum(a_re[...], -1e-4)
    lam_im = a_im[...]
    dt = jnp.exp(log_dt[...])
    mag = jnp.exp(lam_re * dt)
    ab_re = mag * jnp.cos(lam_im * dt)
    ab_im = mag * jnp.sin(lam_im * dt)
    den = lam_re * lam_re + lam_im * lam_im
    num_re = ab_re - 1.0
    coef_re = (num_re * lam_re + ab_im * lam_im) / den
    coef_im = (ab_im * lam_re - num_re * lam_im) / den
    ab_re_o[...] = ab_re
    ab_im_o[...] = ab_im
    br = b_re[...]
    bi = b_im[...]
    bb_re_o[...] = coef_re * br - coef_im * bi
    bb_im_o[...] = coef_re * bi + coef_im * br
    cneg_o[...] = -c_im[...]


def _ssm_prep_call(a_re, a_im, log_dt, b_re, b_im, c_im):
    flat = lambda v: v.reshape(1, NSTATE)
    b_hs = lambda v: jnp.transpose(v, (2, 0, 1)).reshape(H, NSTATE)
    dt_row = jnp.broadcast_to(log_dt[:, None], (G, P)).reshape(1, NSTATE)
    ab_re, ab_im, bb_re, bb_im, cneg = pl.pallas_call(
        _ssm_prep_kernel,
        out_shape=(jax.ShapeDtypeStruct((1, NSTATE), F32), jax.ShapeDtypeStruct((1, NSTATE), F32),
                   jax.ShapeDtypeStruct((H, NSTATE), F32), jax.ShapeDtypeStruct((H, NSTATE), F32),
                   jax.ShapeDtypeStruct((G * H, P), F32)),
        name="ssm_prep",
    )(flat(a_re), flat(a_im), dt_row, b_hs(b_re), b_hs(b_im), c_im.reshape(G * H, P))
    ghp = lambda v: jnp.transpose(v.reshape(H, G, P), (1, 0, 2))
    return (ab_re.reshape(G, P), ab_im.reshape(G, P), ghp(bb_re), ghp(bb_im), cneg.reshape(G, H, P))


def _block_diag_weights(bb_re, bb_im, c_re, c_im_neg):
    eye8 = jnp.eye(8, dtype=F32)
    eye4 = jnp.eye(4, dtype=F32)
    eye2 = jnp.eye(2, dtype=F32)

    def wb_part(bb):
        x = bb.reshape(4, 8, H, P)
        return jnp.einsum('qghp,gk->qghkp', x, eye8).reshape(4, 8 * H, 8 * P)

    wb = jnp.concatenate([wb_part(bb_re), wb_part(bb_im)], axis=-1).astype(BF16)

    def wc_part(c):
        x = c.reshape(4, 4, 2, H, P)
        y = jnp.einsum('qjghp,jk,gl->qjgpklh', x, eye4, eye2)
        return y.reshape(NCHUNK, 2 * P, 4 * 2 * H)

    wc = jnp.concatenate([wc_part(c_re), wc_part(c_im_neg)], axis=1).astype(BF16)
    return wb, wc


def _front(x, sc1, sh1, g_mix, w_in_ref):
    n = _rms(x) * g_mix * (1.0 + sc1) + sh1
    proj = _bdot(n, w_in_ref[...])
    u = proj[:, :SSM_W]
    glu = proj[:, SSM_W:SSM_W + CONV_CH] * jax.nn.sigmoid(proj[:, SSM_W + CONV_CH:])
    return u, glu


def _ssm_out(y_lin, u, d_skip, w_glu_ref, b_glu, g_out_ssm):
    y = _gelu_tanh(y_lin + d_skip * u)
    ys = y * jax.nn.sigmoid(_bdot(y, w_glu_ref[...]) + b_glu)
    return _rms(ys) * g_out_ssm


def _conv_out(conv, b_dw, ln_g, ln_b, g_out_conv):
    c = conv + b_dw
    mu = jnp.mean(c, axis=-1, keepdims=True)
    cc = c - mu
    var = jnp.mean(cc * cc, axis=-1, keepdims=True)
    ln = cc * lax.rsqrt(var + EPS) * ln_g + ln_b
    yc = ln * jax.nn.sigmoid(ln)
    return _rms(yc) * g_out_conv


def _route(n2, w_r_ref, b_r):
    rows = n2.shape[0]
    n_hi = n2.astype(BF16)
    n_lo = (n2 - n_hi.astype(F32)).astype(BF16)
    parts = jnp.dot(jnp.concatenate([n_hi, n_lo], axis=0), w_r_ref[...], preferred_element_type=F32)
    lg = (parts[:rows, :LANES] + parts[:rows, LANES:]) + (parts[rows:, :LANES] + parts[rows:, LANES:]) + b_r
    lane = lax.broadcasted_iota(jnp.int32, (rows, LANES), 1).astype(F32)
    ninf = -jnp.inf
    big = 1e9
    gmask = jnp.logical_and(lane >= NE, lane < NE + NG)
    gl = jnp.where(gmask, lg, ninf)
    gmax = jnp.max(gl, axis=-1, keepdims=True)
    gsum = jnp.sum(jnp.where(gmask, jnp.exp(gl - gmax), 0.0), axis=-1, keepdims=True)
    p_top = 1.0 / gsum
    gi = jnp.min(jnp.where(gl == gmax, lane, big), axis=-1, keepdims=True) - NE
    lo = gi * EPG
    emask = jnp.logical_and(lane >= lo, lane < lo + EPG)
    el = jnp.where(emask, lg, ninf)
    m1 = jnp.max(el, axis=-1, keepdims=True)
    i1 = jnp.min(jnp.where(el == m1, lane, big), axis=-1, keepdims=True)
    el2 = jnp.where(lane == i1, ninf, el)
    m2 = jnp.max(el2, axis=-1, keepdims=True)
    i2 = jnp.min(jnp.where(el2 == m2, lane, big), axis=-1, keepdims=True)
    e2 = jnp.exp(m2 - m1)
    den = 1.0 + e2
    w1 = p_top / den
    w2 = p_top * e2 / den
    cnt = jnp.sum(jnp.where(lane == i1, 1.0, 0.0) + jnp.where(lane == i2, 1.0, 0.0),
                  axis=0, keepdims=True)
    cols = (jnp.where(lane == 0.0, i1, 0.0) + jnp.where(lane == 1.0, i2, 0.0)
            + jnp.where(lane == 2.0, w1, 0.0) + jnp.where(lane == 3.0, w2, 0.0))
    return cols.T[0:8, :], cnt


def _tail(x, ns, nc, gt1, sc2, sh2, w_out_ref, g_ffn, w_r_ref, b_r):
    merged = _bdot(jnp.concatenate([ns, nc], axis=-1), w_out_ref[...])
    x1 = x + gt1 * merged
    n2 = _rms(x1) * g_ffn * (1.0 + sc2) + sh2
    rt, cnt = _route(n2, w_r_ref, b_r)
    return x1, n2, rt, cnt


def _prompt_mixer_kernel(x_ref, mod_ref, g_mix_ref, w_in_ref, wb_ref, a_ref, wc_ref, dsk_ref,
                         w_glu_ref, b_glu_ref, w_dw_ref, b_dw_ref, ln_g_ref, ln_b_ref,
                         g_os_ref, g_oc_ref, w_out_ref, g_ffn_ref, w_r_ref, b_r_ref,
                         x1_ref, n2_ref, rt_ref, cnt_ref, hr_ref, hi_ref, cache_ref,
                         sre, sim, hst, ebuf, eshift, *, tc, pt):
    c = pl.program_id(1)
    nc_chunks = pl.num_programs(1)

    @pl.when(c == 0)
    def _():
        hst[...] = jnp.zeros_like(hst)
        ebuf[pl.ds(0, 32), :] = jnp.zeros((32, CONV_CH), F32)

    x = x_ref[...]
    mod = mod_ref[...]
    sh1, sc1, gt1 = mod[0:1], mod[1:2], mod[2:3]
    sh2, sc2 = mod[3:4], mod[4:5]

    u, glu = _front(x, sc1, sh1, g_mix_ref[...], w_in_ref)
    ebuf[pl.ds(32, tc), :] = glu

    ub = u.astype(BF16)
    for q in range(4):
        r = jnp.dot(ub[:, q * LANES:(q + 1) * LANES], wb_ref[q], preferred_element_type=F32)
        for k in range(4):
            j = 4 * q + k
            sre[pl.ds(j * pt, tc), :] = r[:, k * LANES:(k + 1) * LANES]
            sim[pl.ds(j * pt, tc), :] = r[:, SSM_W + k * LANES:SSM_W + (k + 1) * LANES]

    ar0, ar1 = a_ref[0, 0:8, :], a_ref[0, 8:16, :]
    ai0, ai1 = a_ref[1, 0:8, :], a_ref[1, 8:16, :]

    def step(t, carry):
        hr0, hr1, hi0, hi1 = carry
        i0 = pl.ds(t, 8, stride=pt)
        i1 = pl.ds(t + 8 * pt, 8, stride=pt)
        nr0 = ar0 * hr0 - ai0 * hi0 + sre[i0, :]
        ni0 = ar0 * hi0 + ai0 * hr0 + sim[i0, :]
        nr1 = ar1 * hr1 - ai1 * hi1 + sre[i1, :]
        ni1 = ar1 * hi1 + ai1 * hr1 + sim[i1, :]
        sre[i0, :] = nr0
        sim[i0, :] = ni0
        sre[i1, :] = nr1
        sim[i1, :] = ni1
        return nr0, nr1, ni0, ni1

    init = (hst[0, 0:8, :], hst[0, 8:16, :], hst[1, 0:8, :], hst[1, 8:16, :])
    hr0, hr1, hi0, hi1 = lax.fori_loop(0, tc, step, init)
    hst[0, 0:8, :] = hr0
    hst[0, 8:16, :] = hr1
    hst[1, 0:8, :] = hi0
    hst[1, 8:16, :] = hi1

    @pl.when(c == nc_chunks - 1)
    def _():
        hr_ref[...] = hst[0]
        hi_ref[...] = hst[1]

    ys = []
    for q in range(4):
        acc = None
        for jj in range(4):
            j = 4 * q + jj
            lhs = jnp.concatenate([sre[pl.ds(j * pt, tc), :], sim[pl.ds(j * pt, tc), :]], axis=-1)
            d = jnp.dot(lhs.astype(BF16), wc_ref[j], preferred_element_type=F32)
            acc = d if acc is None else acc + d
        ys.append(acc)
    y_lin = jnp.concatenate(ys, axis=-1)
    ns = _ssm_out(y_lin, u, dsk_ref[...], w_glu_ref, b_glu_ref[...], g_os_ref[...])

    for s in range(1, 8):
        eshift[s - 1, pl.ds(0, tc + 24), :] = ebuf[pl.ds(s, tc + 24), :]
    rb = 64
    convs = []
    for r0 in range(0, tc, rb):
        acc = None
        for k in range(KW):
            a8, s = (k + 2) // 8 * 8, (k + 2) % 8
            win = ebuf[pl.ds(r0 + a8, rb), :] if s == 0 else eshift[s - 1, pl.ds(r0 + a8, rb), :]
            term = w_dw_ref[k:k + 1, :] * win
            acc = term if acc is None else acc + term
        convs.append(acc)
    conv = jnp.concatenate(convs, axis=0)
    nc = _conv_out(conv, b_dw_ref[...], ln_g_ref[...], ln_b_ref[...], g_oc_ref[...])

    @pl.when(c == nc_chunks - 1)
    def _():
        cache_ref[...] = ebuf[pl.ds(tc + 2, CB), :]

    ebuf[pl.ds(0, 32), :] = ebuf[pl.ds(tc, 32), :]

    x1, n2, rt, cnt = _tail(x, ns, nc, gt1, sc2, sh2, w_out_ref, g_ffn_ref[...], w_r_ref, b_r_ref[...])
    x1_ref[...] = x1
    n2_ref[...] = n2.astype(BF16)
    rt_ref[...] = rt
    cnt_ref[...] = cnt


def _const_spec(shape):
    nd = len(shape)
    return pl.BlockSpec(shape, lambda b, c: (0,) * nd)


def _prompt_mixer_call(x, mod6, wts, tc):
    bsz, t, _ = x.shape
    n_all = bsz * t
    pt = tc + 8
    nc = t // tc
    kern = functools.partial(_prompt_mixer_kernel, tc=tc, pt=pt)
    in_specs = [pl.BlockSpec((None, tc, D), lambda b, c: (b, c, 0)),
                pl.BlockSpec((None, 6, D), lambda b, c: (b, 0, 0))]
    in_specs += [_const_spec(w.shape) for w in wts]
    out_shape = (jax.ShapeDtypeStruct((n_all, D), F32),
                 jax.ShapeDtypeStruct((n_all, D), BF16),
                 jax.ShapeDtypeStruct((8, n_all), F32),
                 jax.ShapeDtypeStruct((n_all // tc, 1, LANES), F32),
                 jax.ShapeDtypeStruct((bsz, NCHUNK, LANES), F32),
                 jax.ShapeDtypeStruct((bsz, NCHUNK, LANES), F32),
                 jax.ShapeDtypeStruct((bsz, CB, CONV_CH), F32))
    out_specs = (pl.BlockSpec((tc, D), lambda b, c: (b * nc + c, 0)),
                 pl.BlockSpec((tc, D), lambda b, c: (b * nc + c, 0)),
                 pl.BlockSpec((8, tc), lambda b, c: (0, b * nc + c)),
                 pl.BlockSpec((None, 1, LANES), lambda b, c: (b * nc + c, 0, 0)),
                 pl.BlockSpec((None, NCHUNK, LANES), lambda b, c: (b, 0, 0)),
                 pl.BlockSpec((None, NCHUNK, LANES), lambda b, c: (b, 0, 0)),
                 pl.BlockSpec((None, CB, CONV_CH), lambda b, c: (b, 0, 0)))
    scratch = [pltpu.VMEM((NCHUNK * pt, LANES), F32),
               pltpu.VMEM((NCHUNK * pt, LANES), F32),
               pltpu.VMEM((2, NCHUNK, LANES), F32),
               pltpu.VMEM((tc + 32, CONV_CH), F32),
               pltpu.VMEM((7, tc + 32, CONV_CH), F32)]
    return pl.pallas_call(
        kern, grid=(bsz, nc), in_specs=in_specs, out_specs=out_specs, out_shape=out_shape,
        scratch_shapes=scratch,
        compiler_params=pltpu.CompilerParams(dimension_semantics=("arbitrary", "arbitrary"),
                                             vmem_limit_bytes=VMEM_LIMIT),
        name="prompt_mixer",
    )(x, mod6, *wts)


def _sample_mixer_kernel(x_ref, mod_ref, h0r_ref, h0i_ref, cache_ref,
                         g_mix_ref, w_in_ref, wb_ref, a_ref, wc_ref, dsk_ref,
                         w_glu_ref, b_glu_ref, w_dw_ref, b_dw_ref, ln_g_ref, ln_b_ref,
                         g_os_ref, g_oc_ref, w_out_ref, g_ffn_ref, w_r_ref, b_r_ref,
                         x1_ref, n2_ref, rt_ref, cnt_ref, hr_ref, hi_ref, glu_ref,
                         sre, sim, *, nb, nt):
    x = x_ref[...]

    def rows(i):
        m = mod_ref[:, i * D:(i + 1) * D]
        return jnp.concatenate([m] * nt, axis=0)

    sh1, sc1, gt1, sh2, sc2 = rows(0), rows(1), rows(2), rows(3), rows(4)
    u, glu = _front(x, sc1, sh1, g_mix_ref[...], w_in_ref)
    glu_ref[...] = glu

    ub = u.astype(BF16)
    for q in range(4):
        r = jnp.dot(ub[:, q * LANES:(q + 1) * LANES], wb_ref[q], preferred_element_type=F32)
        sre[:, q * SSM_W:(q + 1) * SSM_W] = r[:, :SSM_W]
        sim[:, q * SSM_W:(q + 1) * SSM_W] = r[:, SSM_W:]

    ar = a_ref[0:1, :]
    ai = a_ref[1:2, :]
    hr = h0r_ref[...]
    hi = h0i_ref[...]
    for t in range(nt):
        rs = pl.ds(t * nb, nb)
        nr = ar * hr - ai * hi + sre[rs, :]
        ni = ar * hi + ai * hr + sim[rs, :]
        sre[rs, :] = nr
        sim[rs, :] = ni
        hr, hi = nr, ni
    hr_ref[...] = hr
    hi_ref[...] = hi

    ys = []
    for q in range(4):
        acc = None
        for jj in range(4):
            j = 4 * q + jj
            lhs = jnp.concatenate([sre[:, j * LANES:(j + 1) * LANES],
                                   sim[:, j * LANES:(j + 1) * LANES]], axis=-1)
            d = jnp.dot(lhs.astype(BF16), wc_ref[j], preferred_element_type=F32)
            acc = d if acc is None else acc + d
        ys.append(acc)
    y_lin = jnp.concatenate(ys, axis=-1)
    ns = _ssm_out(y_lin, u, dsk_ref[...], w_glu_ref, b_glu_ref[...], g_os_ref[...])

    def ext(jrow):
        if jrow < CB:
            return cache_ref[jrow]
        return glu[(jrow - CB) * nb:(jrow - CB + 1) * nb, :]

    convs = []
    for t in range(nt):
        acc = None
        for k in range(KW):
            term = w_dw_ref[k:k + 1, :] * ext(t + k)
            acc = term if acc is None else acc + term
        convs.append(acc)
    conv = jnp.concatenate(convs, axis=0)
    nc = _conv_out(conv, b_dw_ref[...], ln_g_ref[...], ln_b_ref[...], g_oc_ref[...])

    x1, n2, rt, cnt = _tail(x, ns, nc, gt1, sc2, sh2, w_out_ref, g_ffn_ref[...], w_r_ref, b_r_ref[...])
    x1_ref[...] = x1
    n2_ref[...] = n2.astype(BF16)
    rt_ref[...] = rt
    cnt_ref[...] = cnt


def _sample_mixer_call(x_tm, mod_s, h0r, h0i, cache_tm, wts, nb, nt):
    n = nb * nt
    kern = functools.partial(_sample_mixer_kernel, nb=nb, nt=nt)
    out_shape = (jax.ShapeDtypeStruct((n, D), F32),
                 jax.ShapeDtypeStruct((n, D), BF16),
                 jax.ShapeDtypeStruct((8, n), F32),
                 jax.ShapeDtypeStruct((1, LANES), F32),
                 jax.ShapeDtypeStruct((nb, NSTATE), F32),
                 jax.ShapeDtypeStruct((nb, NSTATE), F32),
                 jax.ShapeDtypeStruct((n, CONV_CH), F32))
    scratch = [pltpu.VMEM((n, NSTATE), F32), pltpu.VMEM((n, NSTATE), F32)]
    return pl.pallas_call(
        kern, out_shape=out_shape, scratch_shapes=scratch,
        compiler_params=pltpu.CompilerParams(vmem_limit_bytes=VMEM_LIMIT),
        name="sample_mixer",
    )(x_tm, mod_s, h0r, h0i, cache_tm, *wts)


ROW_ALIGN = 16
MOE_TD = 512
MOE_BR = MOE_TD * 2 + NE * ROW_ALIGN
MOE_TG = 512


def _slot_positions(rt):
    t = rt.shape[1]
    e0 = rt[0:1, :]
    e1 = rt[1:2, :]
    sub = lax.broadcasted_iota(jnp.int32, (LANES, t), 0).astype(F32)
    a0 = jnp.where(sub == e0, 1.0, 0.0)
    a1 = jnp.where(sub == e1, 1.0, 0.0)
    at = a0 + a1
    r = lax.broadcasted_iota(jnp.int32, (t, t), 0)
    c = lax.broadcasted_iota(jnp.int32, (t, t), 1)
    before = jnp.where(r < c, 1.0, 0.0).astype(BF16)
    rank = jnp.dot(at.astype(BF16), before, preferred_element_type=F32)
    cnt = jnp.sum(at, axis=1, keepdims=True)
    cnt_al = jnp.ceil(cnt * (1.0 / ROW_ALIGN)) * float(ROW_ALIGN)
    er = lax.broadcasted_iota(jnp.int32, (LANES, LANES), 0)
    ec = lax.broadcasted_iota(jnp.int32, (LANES, LANES), 1)
    lower = jnp.where(ec < er, 1.0, 0.0)
    base = jnp.dot(lower, jnp.broadcast_to(cnt_al, (LANES, LANES)), preferred_element_type=F32,
                   precision=lax.Precision.HIGHEST)[:, 0:1]
    slot = rank + base
    pos0 = jnp.sum(a0 * slot, axis=0, keepdims=True)
    pos1 = jnp.sum(a1 * slot, axis=0, keepdims=True)
    return pos0, pos1


def _segment_copies(tab_ref, t, n_tiles, buf, hbm, sems, slot, to_hbm, wait):
    for e in range(NE):
        n = pl.multiple_of(tab_ref[t * NE + e], ROW_ALIGN)
        b = pl.multiple_of(tab_ref[(n_tiles + t) * NE + e], ROW_ALIGN)
        d = pl.multiple_of(tab_ref[(2 * n_tiles + t) * NE + e], ROW_ALIGN)
        vm = buf.at[slot, pl.ds(b, n)]
        hb = hbm.at[pl.ds(d, n)]
        cp = pltpu.make_async_copy(vm, hb, sems.at[slot, e]) if to_hbm else \
            pltpu.make_async_copy(hb, vm, sems.at[slot, e])

        @pl.when(n > 0)
        def _():
            if wait:
                cp.wait()
            else:
                cp.start()


def _tile_rows(tab_ref, t, n_tiles):
    return tab_ref[3 * n_tiles * NE + 2 * NE + 1 + t]


def _one_hot_rows(r0, nrows, pos0, pos1):
    row = (lax.broadcasted_iota(jnp.int32, (nrows, MOE_TD), 0) + r0).astype(F32)
    return row == pos0, row == pos1


MOE_BLK = 256


def _dispatch_kernel(tab_ref, rtp_ref, n2p_ref, rts_ref, n2s_ref, xs_ref, buf, zbuf, sems, zsem,
                     *, n_tiles, n_ptiles, n_gtiles):
    t = pl.program_id(0)
    slot = lax.rem(t, 2)
    first_free = tab_ref[3 * n_tiles * NE + 2 * NE]

    def fill_copy(j):
        d = pl.multiple_of(j * MOE_TG, MOE_TG)
        return pltpu.make_async_copy(zbuf, xs_ref.at[pl.ds(d, MOE_TG)], zsem)

    def fill_start(j, carry):
        fill_copy(j).start()
        return carry

    def fill_wait(j, carry):
        fill_copy(j).wait()
        return carry

    @pl.when(t == 0)
    def _():
        zbuf[...] = jnp.zeros_like(zbuf)
        for phase in range(2):
            for e in range(NE):
                d = pl.multiple_of(tab_ref[3 * n_tiles * NE + e], ROW_ALIGN)
                n = pl.multiple_of(tab_ref[3 * n_tiles * NE + NE + e], ROW_ALIGN)
                cp = pltpu.make_async_copy(zbuf.at[pl.ds(0, n)], xs_ref.at[pl.ds(d, n)], sems.at[1, e])

                @pl.when(n > 0)
                def _():
                    if phase == 0:
                        cp.start()
                    else:
                        cp.wait()

        lax.fori_loop(first_free, n_gtiles, fill_start, 0)

    @pl.when(t >= 2)
    def _():
        _segment_copies(tab_ref, t - 2, n_tiles, buf, xs_ref, sems, slot, to_hbm=True, wait=True)

    is_sample = t >= n_ptiles
    rt = jnp.where(is_sample, rts_ref[...], rtp_ref[...])
    n2 = jnp.where(is_sample, n2s_ref[...], n2p_ref[...])
    pos0, pos1 = _slot_positions(rt)
    used = _tile_rows(tab_ref, t, n_tiles)

    def group(r0, nrows):
        m0, m1 = _one_hot_rows(r0, nrows, pos0, pos1)
        q = (jnp.where(m0, 1.0, 0.0) + jnp.where(m1, 1.0, 0.0)).astype(BF16)
        buf[slot, pl.ds(r0, nrows), :] = jnp.dot(q, n2, preferred_element_type=F32).astype(BF16)

    group(0, 2 * MOE_TD)
    for r0 in range(2 * MOE_TD, MOE_BR, MOE_BLK):
        @pl.when(used > r0)
        def _():
            group(r0, MOE_BLK)

    _segment_copies(tab_ref, t, n_tiles, buf, xs_ref, sems, slot, to_hbm=True, wait=False)

    @pl.when(t == n_tiles - 1)
    def _():
        if n_tiles >= 2:
            _segment_copies(tab_ref, t - 1, n_tiles, buf, xs_ref, sems, 1 - slot, to_hbm=True, wait=True)
        _segment_copies(tab_ref, t, n_tiles, buf, xs_ref, sems, slot, to_hbm=True, wait=True)
        lax.fori_loop(first_free, n_gtiles, fill_wait, 0)


def _dispatch_call(tab, rt_p, n2_p, rt_s, n2_s, r_max):
    n_ptiles = n2_p.shape[0] // MOE_TD
    n_tiles = n_ptiles + n2_s.shape[0] // MOE_TD
    last_p = n_ptiles - 1
    return pl.pallas_call(
        functools.partial(_dispatch_kernel, n_tiles=n_tiles, n_ptiles=n_ptiles,
                          n_gtiles=r_max // MOE_TG),
        grid_spec=pltpu.PrefetchScalarGridSpec(
            num_scalar_prefetch=1, grid=(n_tiles,),
            in_specs=[pl.BlockSpec((8, MOE_TD), lambda t, tab: (0, jnp.minimum(t, last_p))),
                      pl.BlockSpec((MOE_TD, D), lambda t, tab: (jnp.minimum(t, last_p), 0)),
                      pl.BlockSpec((8, MOE_TD), lambda t, tab: (0, 0)),
                      pl.BlockSpec((MOE_TD, D), lambda t, tab: (0, 0))],
            out_specs=pl.BlockSpec(memory_space=pl.ANY),
            scratch_shapes=[pltpu.VMEM((2, MOE_BR, D), BF16),
                            pltpu.VMEM((MOE_TG, D), BF16),
                            pltpu.SemaphoreType.DMA((2, NE)),
                            pltpu.SemaphoreType.DMA(())]),
        out_shape=jax.ShapeDtypeStruct((r_max, D), BF16),
        compiler_params=pltpu.CompilerParams(dimension_semantics=("arbitrary",),
                                             vmem_limit_bytes=VMEM_LIMIT),
        name="moe_dispatch",
    )(tab, rt_p, n2_p, rt_s, n2_s)


MOE_CK = 3


def _experts_kernel(ctab_ref, xs_ref, w1_ref, w3_ref, w2_ref, ys_ref,
                    w1b, w3b, w2b, xbuf, ybuf, zbuf, in_sem, out_sem, zsem, *, n_gtiles, nc_max):
    e = pl.program_id(0)
    c0 = ctab_ref[e]
    c1 = ctab_ref[e + 1]
    n_chunks = ctab_ref[NE]
    first_free = ctab_ref[NE + 1 + 2 * nc_max]

    def fill_copy(j):
        d = pl.multiple_of(j * MOE_TG, MOE_TG)
        return pltpu.make_async_copy(zbuf, ys_ref.at[pl.ds(d, MOE_TG)], zsem)

    def fill_start(j, carry):
        fill_copy(j).start()
        return carry

    def fill_wait(j, carry):
        fill_copy(j).wait()
        return carry

    def span(c):
        r = pl.multiple_of(ctab_ref[NE + 1 + c], MOE_TG)
        n = pl.multiple_of(ctab_ref[NE + 1 + nc_max + c] * MOE_TG, MOE_TG)
        return r, n

    def in_copy(c, slot):
        r, n = span(c)
        return pltpu.make_async_copy(xs_ref.at[pl.ds(r, n)], xbuf.at[slot, pl.ds(0, n)], in_sem.at[slot])

    def out_copy(c, slot):
        r, n = span(c)
        return pltpu.make_async_copy(ybuf.at[slot, pl.ds(0, n)], ys_ref.at[pl.ds(r, n)], out_sem.at[slot])

    @pl.when(e == 0)
    def _():
        zbuf[...] = jnp.zeros_like(zbuf)
        lax.fori_loop(first_free, n_gtiles, fill_start, 0)

        @pl.when(n_chunks > 0)
        def _():
            in_copy(0, 0).start()

    def compute(slot, rows):
        x = xbuf[slot, pl.ds(0, rows), :]
        a = jnp.dot(x, w1b[...], preferred_element_type=F32)
        b = jnp.dot(x, w3b[...], preferred_element_type=F32)
        hid = a * jax.nn.sigmoid(a) * b
        y = jnp.dot(hid.astype(BF16), w2b[...], preferred_element_type=F32)
        ybuf[slot, pl.ds(0, rows), :] = y.astype(BF16)

    @pl.when(c1 > c0)
    def _():
        w1b[...] = w1_ref[...].astype(BF16)
        w3b[...] = w3_ref[...].astype(BF16)
        w2b[...] = w2_ref[...].astype(BF16)

        def chunk(c, carry):
            slot = lax.rem(c, 2)

            @pl.when(c + 1 < n_chunks)
            def _():
                in_copy(c + 1, 1 - slot).start()

            in_copy(c, slot).wait()

            @pl.when(c >= 2)
            def _():
                out_copy(c - 2, slot).wait()

            k = ctab_ref[NE + 1 + nc_max + c]
            for kk in range(1, MOE_CK + 1):
                @pl.when(k == kk)
                def _():
                    compute(slot, kk * MOE_TG)

            out_copy(c, slot).start()
            return carry

        lax.fori_loop(c0, c1, chunk, 0)

    @pl.when(e == NE - 1)
    def _():
        @pl.when(n_chunks >= 2)
        def _():
            out_copy(n_chunks - 2, lax.rem(n_chunks, 2)).wait()

        @pl.when(n_chunks >= 1)
        def _():
            out_copy(n_chunks - 1, lax.rem(n_chunks - 1, 2)).wait()

        lax.fori_loop(first_free, n_gtiles, fill_wait, 0)


def _experts_call(ctab, xs, w1, w3, w2, nc_max):
    r_max = xs.shape[0]
    w_map = lambda e, ctab: (e, 0, 0)
    ring = pltpu.VMEM((2, MOE_CK * MOE_TG, D), BF16)
    return pl.pallas_call(
        functools.partial(_experts_kernel, n_gtiles=r_max // MOE_TG, nc_max=nc_max),
        grid_spec=pltpu.PrefetchScalarGridSpec(
            num_scalar_prefetch=1, grid=(NE,),
            in_specs=[pl.BlockSpec(memory_space=pl.ANY),
                      pl.BlockSpec((None, D, DE), w_map),
                      pl.BlockSpec((None, D, DE), w_map),
                      pl.BlockSpec((None, DE, D), w_map)],
            out_specs=pl.BlockSpec(memory_space=pl.ANY),
            scratch_shapes=[pltpu.VMEM((D, DE), BF16), pltpu.VMEM((D, DE), BF16),
                            pltpu.VMEM((DE, D), BF16), ring, ring,
                            pltpu.VMEM((MOE_TG, D), BF16),
                            pltpu.SemaphoreType.DMA((2,)), pltpu.SemaphoreType.DMA((2,)),
                            pltpu.SemaphoreType.DMA(())]),
        out_shape=jax.ShapeDtypeStruct((r_max, D), BF16),
        compiler_params=pltpu.CompilerParams(dimension_semantics=("arbitrary",),
                                             vmem_limit_bytes=VMEM_LIMIT),
        name="moe_experts",
    )(ctab, xs, w1, w3, w2)


def _combine_kernel(tab_ref, rt_ref, x1_ref, gt2_ref, gf_ref, ys_ref, y_ref, buf, acc, sems,
                    *, n_tiles, t_off):
    i = pl.program_id(0)
    t = i + t_off
    slot = lax.rem(i, 2)

    @pl.when(i == 0)
    def _():
        buf[...] = jnp.zeros_like(buf)
        _segment_copies(tab_ref, t, n_tiles, buf, ys_ref, sems, slot, to_hbm=False, wait=False)

    @pl.when(i + 1 < pl.num_programs(0))
    def _():
        _segment_copies(tab_ref, t + 1, n_tiles, buf, ys_ref, sems, 1 - slot, to_hbm=False, wait=False)

    _segment_copies(tab_ref, t, n_tiles, buf, ys_ref, sems, slot, to_hbm=False, wait=True)

    rt = rt_ref[...]
    pos0, pos1 = _slot_positions(rt)
    used = _tile_rows(tab_ref, t, n_tiles)

    def ungroup(r0, nrows):
        m0, m1 = _one_hot_rows(r0, nrows, pos0, pos1)
        q = (jnp.where(m0, 1.0, 0.0) + jnp.where(m1, 1.0, 0.0)).astype(BF16)
        gw = jnp.sum(jnp.where(m0, rt[2:3, :], 0.0) + jnp.where(m1, rt[3:4, :], 0.0),
                     axis=1, keepdims=True)
        yv = (buf[slot, pl.ds(r0, nrows), :].astype(F32) * gw).astype(BF16)
        return lax.dot_general(q, yv, (((0,), (0,)), ((), ())), preferred_element_type=F32)

    acc[...] = ungroup(0, 2 * MOE_TD)
    for r0 in range(2 * MOE_TD, MOE_BR, MOE_BLK):
        @pl.when(used > r0)
        def _():
            acc[...] += ungroup(r0, MOE_BLK)

    xo = x1_ref[...] + gt2_ref[...] * acc[...]
    y_ref[...] = _rms(xo) * gf_ref[...]


def _combine_call(tab, rt, x1, gt2, gt2_spec, g_final, ys, n_tiles, t_off):
    n_out_tiles = x1.shape[0] // MOE_TD
    return pl.pallas_call(
        functools.partial(_combine_kernel, n_tiles=n_tiles, t_off=t_off),
        grid_spec=pltpu.PrefetchScalarGridSpec(
            num_scalar_prefetch=1, grid=(n_out_tiles,),
            in_specs=[pl.BlockSpec((8, MOE_TD), lambda t, tab: (0, t)),
                      pl.BlockSpec((MOE_TD, D), lambda t, tab: (t, 0)),
                      gt2_spec,
                      pl.BlockSpec((1, D), lambda t, tab: (0, 0)),
                      pl.BlockSpec(memory_space=pl.ANY)],
            out_specs=pl.BlockSpec((MOE_TD, D), lambda t, tab: (t, 0)),
            scratch_shapes=[pltpu.VMEM((2, MOE_BR, D), BF16),
                            pltpu.VMEM((MOE_TD, D), F32),
                            pltpu.SemaphoreType.DMA((2, NE))]),
        out_shape=jax.ShapeDtypeStruct((n_out_tiles * MOE_TD, D), F32),
        compiler_params=pltpu.CompilerParams(dimension_semantics=("arbitrary",),
                                             vmem_limit_bytes=VMEM_LIMIT),
        name="moe_combine",
    )(tab, rt, x1, gt2, g_final.reshape(1, D), ys)


def _moe_plan(cnt, nc_max):
    cnt8 = (cnt + ROW_ALIGN - 1) // ROW_ALIGN * ROW_ALIGN
    seg_rows = cnt8.sum(axis=0)
    seg_pad = (seg_rows + MOE_TG - 1) // MOE_TG * MOE_TG
    seg_start = jnp.cumsum(seg_pad) - seg_pad
    dst = seg_start[None, :] + jnp.cumsum(cnt8, axis=0) - cnt8
    boff = jnp.cumsum(cnt8, axis=1) - cnt8
    tile_end = jnp.cumsum(seg_pad // MOE_TG)
    n_active = tile_end[-1:].astype(jnp.int32)
    tab = jnp.concatenate([cnt8.ravel(), boff.ravel(), dst.ravel(),
                           seg_start + seg_rows, seg_pad - seg_rows, n_active,
                           cnt8.sum(axis=1)]).astype(jnp.int32)
    nt = seg_pad // MOE_TG
    nfull = nt // MOE_CK
    rem = nt % MOE_CK
    nch = nfull + (rem > 0)
    cend = jnp.cumsum(nch)
    cstart = cend - nch
    c = jnp.arange(nc_max, dtype=jnp.int32)
    ce = jnp.minimum(jnp.sum(c[:, None] >= cend[None, :], axis=1), NE - 1)
    local = c - cstart[ce]
    valid = c < cend[-1]
    ck = jnp.where(valid, jnp.where(local < nfull[ce], MOE_CK, rem[ce]), 0)
    crow = jnp.where(valid, seg_start[ce] + local * (MOE_CK * MOE_TG), 0)
    ctab = jnp.concatenate([cstart, cend[-1:], crow, ck, n_active]).astype(jnp.int32)
    return tab, ctab


def kernel(x_prompt, x_sample, c_prompt, c_sample, state_ssm_re, state_ssm_im, cache_conv, w_ada, b_ada, g_norm_mix, w_in, ssm_a_re, ssm_a_im, ssm_log_dt, ssm_b_re, ssm_b_im, ssm_c_re, ssm_c_im, ssm_d, w_ssm_glu, b_ssm_glu, w_dw, b_dw, ln_conv_g, ln_conv_b, g_out_ssm, g_out_conv, w_out, g_norm_ffn, w_router_grp, b_router_grp, w_router_exp, b_router_exp, w_exp_gate, w_exp_up, w_exp_down, g_final):
    depth = w_ada.shape[0]
    assert depth == 1
    bsz, seq, _ = x_prompt.shape
    nb, nt, _ = x_sample.shape

    c_all = jnp.concatenate([c_prompt, c_sample], axis=0)
    mod = _mod_call(c_all, w_ada[0], b_ada[0])
    mod_p = mod[:bsz].reshape(bsz, 6, D)
    mod_s = mod[bsz:]

    ab_re, ab_im, bb_re, bb_im, c_im_neg = _ssm_prep_call(
        ssm_a_re[0], ssm_a_im[0], ssm_log_dt[0], ssm_b_re[0], ssm_b_im[0], ssm_c_im[0])
    wb, wc = _block_diag_weights(bb_re, bb_im, ssm_c_re[0], c_im_neg)
    a_tok = jnp.stack([ab_re.reshape(NCHUNK, LANES), ab_im.reshape(NCHUNK, LANES)])
    a_row = jnp.stack([ab_re.reshape(NSTATE), ab_im.reshape(NSTATE)])

    w_r = jnp.concatenate([w_router_exp[0].reshape(D, NE), w_router_grp[0],
                           jnp.zeros((D, LANES - NE - NG), F32)], axis=1)
    w_r_hi = w_r.astype(BF16)
    w_r = jnp.concatenate([w_r_hi, (w_r - w_r_hi.astype(F32)).astype(BF16)], axis=1)
    b_r = jnp.concatenate([b_router_exp[0].reshape(NE), b_router_grp[0],
                           jnp.zeros((LANES - NE - NG,), F32)]).reshape(1, LANES)
    w_dw_p = jnp.concatenate([w_dw[0], jnp.zeros((1, CONV_CH), F32)], axis=0)

    row = lambda v: v.reshape(1, -1)
    common_a = (row(g_norm_mix[0]), w_in[0].astype(BF16), wb)
    common_b = (wc, row(ssm_d[0].reshape(SSM_W)), w_ssm_glu[0].astype(BF16), row(b_ssm_glu[0]),
                w_dw_p, row(b_dw[0]), row(ln_conv_g[0]), row(ln_conv_b[0]),
                row(g_out_ssm[0]), row(g_out_conv[0]), w_out[0].astype(BF16),
                row(g_norm_ffn[0]), w_r, b_r)

    n_p = bsz * seq
    n_s = nb * nt
    n_all = n_p + n_s
    assert n_s == MOE_TD and seq % MOE_TD == 0 and MOE_TD % PROMPT_TC == 0
    wts_p = common_a + (a_tok,) + common_b
    x1_p, n2_p, rt_p, cnt_p, hr_p, hi_p, cache_p = _prompt_mixer_call(x_prompt, mod_p, wts_p, PROMPT_TC)

    x_tm = jnp.transpose(x_sample, (1, 0, 2)).reshape(nt * nb, D)
    cache_tm = jnp.transpose(cache_conv[0], (1, 0, 2))
    wts_s = common_a + (a_row,) + common_b
    x1_s, n2_s, rt_s, cnt_s, hr_s, hi_s, glu_s = _sample_mixer_call(
        x_tm, mod_s, state_ssm_re[0].reshape(nb, NSTATE), state_ssm_im[0].reshape(nb, NSTATE),
        cache_tm, wts_s, nb, nt)

    n_ptiles = n_p // MOE_TD
    n_tiles = n_all // MOE_TD
    r_max = -(-(2 * n_all + n_tiles * NE * (ROW_ALIGN - 1) + NE * (MOE_TG - ROW_ALIGN)) // MOE_TG) * MOE_TG
    cnt = jnp.concatenate([cnt_p.reshape(n_ptiles, MOE_TD // PROMPT_TC, LANES).sum(axis=1), cnt_s])
    nc_max = r_max // MOE_TG // MOE_CK + NE
    tab, ctab = _moe_plan(cnt[:, :NE].astype(jnp.int32), nc_max)
    xs = _dispatch_call(tab, rt_p, n2_p, rt_s, n2_s, r_max)
    ys = _experts_call(ctab, xs, w_exp_gate[0], w_exp_up[0], w_exp_down[0], nc_max)
    tiles_per_b = seq // MOE_TD
    gt2_p = mod_p[:, 5:6, :]
    y_p = _combine_call(tab, rt_p, x1_p, gt2_p,
                        pl.BlockSpec((None, 1, D), lambda t, tab: (t // tiles_per_b, 0, 0)),
                        g_final, ys, n_tiles, 0)
    gt2_s = jnp.tile(mod_s[:, 5 * D:], (nt, 1))
    y_s = _combine_call(tab, rt_s, x1_s, gt2_s,
                        pl.BlockSpec((n_s, D), lambda t, tab: (0, 0)),
                        g_final, ys, n_tiles, n_ptiles)

    y_prompt = y_p.reshape(bsz, seq, D)
    y_sample = jnp.transpose(y_s.reshape(nt, nb, D), (1, 0, 2))
    new_cache_s = jnp.concatenate(
        [cache_conv[0][:, nt:, :], jnp.transpose(glu_s.reshape(nt, nb, CONV_CH), (1, 0, 2))], axis=1)
    return (y_prompt, y_sample,
            hr_p.reshape(1, bsz, G, P), hi_p.reshape(1, bsz, G, P), cache_p[None],
            hr_s.reshape(1, nb, G, P), hi_s.reshape(1, nb, G, P), new_cache_s[None])
```

```python
import functools

import jax
import jax.numpy as jnp
import numpy as np
from jax import lax
from jax.experimental import pallas as pl
from jax.experimental.pallas import tpu as pltpu

F32 = jnp.float32
BF16 = jnp.bfloat16

D = 1024
SSM_W = 512
CONV_CH = 512
G = 32
H = 16
P = 64
KW = 31
CB = KW - 1
NE = 32
NG = 4
EPG = 8
DE = 512
EPS = 1e-6
LANES = 128
NSTATE = G * P
NCHUNK = NSTATE // LANES

PROMPT_TC = 512
SCAN_GROUP = 8
VMEM_LIMIT = 56 * 1024 * 1024


def _rms(x):
    return x * lax.rsqrt(jnp.mean(x * x, axis=-1, keepdims=True) + EPS)


def _gelu_tanh(y):
    c = np.sqrt(2.0 / np.pi).astype(np.float32)
    return y * (0.5 * (1.0 + jnp.tanh(c * (y + 0.044715 * (y * y * y)))))


def _bdot(a, b):
    return jnp.dot(a.astype(BF16), b, preferred_element_type=F32)


def _mod_kernel(c_ref, w_ref, b_ref, o_ref):
    c = c_ref[...]
    s = c * jax.nn.sigmoid(c)
    o_ref[...] = jnp.dot(s, w_ref[...], preferred_element_type=F32,
                         precision=lax.Precision.HIGHEST) + b_ref[...]


def _mod_call(c_all, w_ada, b_ada):
    n = c_all.shape[0]
    tn = 512
    return pl.pallas_call(
        _mod_kernel,
        grid=(6 * D // tn,),
        in_specs=[pl.BlockSpec((n, D), lambda j: (0, 0)),
                  pl.BlockSpec((D, tn), lambda j: (0, j)),
                  pl.BlockSpec((1, tn), lambda j: (0, j))],
        out_specs=pl.BlockSpec((n, tn), lambda j: (0, j)),
        out_shape=jax.ShapeDtypeStruct((n, 6 * D), F32),
        compiler_params=pltpu.CompilerParams(dimension_semantics=("arbitrary",)),
        name="mod",
    )(c_all, w_ada, b_ada.reshape(1, 6 * D))


def _ssm_prep_kernel(a_re, a_im, log_dt, b_re, b_im, c_im,
                     ab_re_o, ab_im_o, bb_re_o, bb_im_o, cneg_o):
    lam_re = jnp.minimum(a_re[...], -1e-4)
    lam_im = a_im[...]
    dt = jnp.exp(log_dt[...])
    mag = jnp.exp(lam_re * dt)
    ab_re = mag * jnp.cos(lam_im * dt)
    ab_im = mag * jnp.sin(lam_im * dt)
    den = lam_re * lam_re + lam_im * lam_im
    num_re = ab_re - 1.0
    coef_re = (num_re * lam_re + ab_im * lam_im) / den
    coef_im = (ab_im * lam_re - num_re * lam_im) / den
    ab_re_o[...] = ab_re
    ab_im_o[...] = ab_im
    br = b_re[...]
    bi = b_im[...]
    bb_re_o[...] = coef_re * br - coef_im * bi
    bb_im_o[...] = coef_re * bi + coef_im * br
    cneg_o[...] = -c_im[...]


def _ssm_prep_call(a_re, a_im, log_dt, b_re, b_im, c_im):
    flat = lambda v: v.reshape(1, NSTATE)
    b_hs = lambda v: jnp.transpose(v, (2, 0, 1)).reshape(H, NSTATE)
    dt_row = jnp.broadcast_to(log_dt[:, None], (G, P)).reshape(1, NSTATE)
    ab_re, ab_im, bb_re, bb_im, cneg = pl.pallas_call(
        _ssm_prep_kernel,
        out_shape=(jax.ShapeDtypeStruct((1, NSTATE), F32), jax.ShapeDtypeStruct((1, NSTATE), F32),
                   jax.ShapeDtypeStruct((H, NSTATE), F32), jax.ShapeDtypeStruct((H, NSTATE), F32),
                   jax.ShapeDtypeStruct((G * H, P), F32)),
        name="ssm_prep",
    )(flat(a_re), flat(a_im), dt_row, b_hs(b_re), b_hs(b_im), c_im.reshape(G * H, P))
    ghp = lambda v: jnp.transpose(v.reshape(H, G, P), (1, 0, 2))
    return (ab_re.reshape(G, P), ab_im.reshape(G, P), ghp(bb_re), ghp(bb_im), cneg.reshape(G, H, P))


def _block_diag_weights(bb_re, bb_im, c_re, c_im_neg):
    eye8 = jnp.eye(8, dtype=F32)
    eye4 = jnp.eye(4, dtype=F32)
    eye2 = jnp.eye(2, dtype=F32)

    def wb_part(bb):
        x = bb.reshape(4, 8, H, P)
        return jnp.einsum('qghp,gk->qghkp', x, eye8).reshape(4, 8 * H, 8 * P)

    wb = jnp.concatenate([wb_part(bb_re), wb_part(bb_im)], axis=-1).astype(BF16)

    def wc_part(c):
        x = c.reshape(4, 4, 2, H, P)
        y = jnp.einsum('qjghp,jk,gl->qjgpklh', x, eye4, eye2)
        return y.reshape(NCHUNK, 2 * P, 4 * 2 * H)

    wc = jnp.concatenate([wc_part(c_re), wc_part(c_im_neg)], axis=1).astype(BF16)
    return wb, wc


def _front(x, sc1, sh1, g_mix, w_in_ref):
    n = _rms(x) * g_mix * (1.0 + sc1) + sh1
    proj = _bdot(n, w_in_ref[...])
    u = proj[:, :SSM_W]
    glu = proj[:, SSM_W:SSM_W + CONV_CH] * jax.nn.sigmoid(proj[:, SSM_W + CONV_CH:])
    return u, glu


def _ssm_out(y_lin, u, d_skip, w_glu_ref, b_glu, g_out_ssm):
    y = _gelu_tanh(y_lin + d_skip * u)
    ys = y * jax.nn.sigmoid(_bdot(y, w_glu_ref[...]) + b_glu)
    return _rms(ys) * g_out_ssm


def _conv_out(conv, b_dw, ln_g, ln_b, g_out_conv):
    c = conv + b_dw
    mu = jnp.mean(c, axis=-1, keepdims=True)
    cc = c - mu
    var = jnp.mean(cc * cc, axis=-1, keepdims=True)
    ln = cc * lax.rsqrt(var + EPS) * ln_g + ln_b
    yc = ln * jax.nn.sigmoid(ln)
    return _rms(yc) * g_out_conv


def _route(n2, w_r_ref, b_r):
    rows = n2.shape[0]
    n_hi = n2.astype(BF16)
    n_lo = (n2 - n_hi.astype(F32)).astype(BF16)
    parts = jnp.dot(jnp.concatenate([n_hi, n_lo], axis=0), w_r_ref[...], preferred_element_type=F32)
    lg = (parts[:rows, :LANES] + parts[:rows, LANES:]) + (parts[rows:, :LANES] + parts[rows:, LANES:]) + b_r
    lane = lax.broadcasted_iota(jnp.int32, (rows, LANES), 1).astype(F32)
    ninf = -jnp.inf
    big = 1e9
    gmask = jnp.logical_and(lane >= NE, lane < NE + NG)
    gl = jnp.where(gmask, lg, ninf)
    gmax = jnp.max(gl, axis=-1, keepdims=True)
    gsum = jnp.sum(jnp.where(gmask, jnp.exp(gl - gmax), 0.0), axis=-1, keepdims=True)
    p_top = 1.0 / gsum
    gi = jnp.min(jnp.where(gl == gmax, lane, big), axis=-1, keepdims=True) - NE
    lo = gi * EPG
    emask = jnp.logical_and(lane >= lo, lane < lo + EPG)
    el = jnp.where(emask, lg, ninf)
    m1 = jnp.max(el, axis=-1, keepdims=True)
    i1 = jnp.min(jnp.where(el == m1, lane, big), axis=-1, keepdims=True)
    el2 = jnp.where(lane == i1, ninf, el)
    m2 = jnp.max(el2, axis=-1, keepdims=True)
    i2 = jnp.min(jnp.where(el2 == m2, lane, big), axis=-1, keepdims=True)
    e2 = jnp.exp(m2 - m1)
    den = 1.0 + e2
    w1 = p_top / den
    w2 = p_top * e2 / den
    cnt = jnp.sum(jnp.where(lane == i1, 1.0, 0.0) + jnp.where(lane == i2, 1.0, 0.0),
                  axis=0, keepdims=True)
    cols = (jnp.where(lane == 0.0, i1, 0.0) + jnp.where(lane == 1.0, i2, 0.0)
            + jnp.where(lane == 2.0, w1, 0.0) + jnp.where(lane == 3.0, w2, 0.0))
    return cols.T[0:8, :], cnt


def _tail(x, ns, nc, gt1, sc2, sh2, w_out_ref, g_ffn, w_r_ref, b_r):
    merged = _bdot(jnp.concatenate([ns, nc], axis=-1), w_out_ref[...])
    x1 = x + gt1 * merged
    n2 = _rms(x1) * g_ffn * (1.0 + sc2) + sh2
    rt, cnt = _route(n2, w_r_ref, b_r)
    return x1, n2, rt, cnt


def _prompt_mixer_kernel(x_ref, mod_ref, g_mix_ref, w_in_ref, wb_ref, a_ref, wc_ref, dsk_ref,
                         w_glu_ref, b_glu_ref, w_dw_ref, b_dw_ref, ln_g_ref, ln_b_ref,
                         g_os_ref, g_oc_ref, w_out_ref, g_ffn_ref, w_r_ref, b_r_ref,
                         x1_ref, n2_ref, rt_ref, cnt_ref, hr_ref, hi_ref, cache_ref,
                         sre, sim, hst, ebuf, eshift, *, tc, pt):
    c = pl.program_id(1)
    nc_chunks = pl.num_programs(1)

    @pl.when(c == 0)
    def _():
        hst[...] = jnp.zeros_like(hst)
        ebuf[pl.ds(0, 32), :] = jnp.zeros((32, CONV_CH), F32)

    x = x_ref[...]
    mod = mod_ref[...]
    sh1, sc1, gt1 = mod[0:1], mod[1:2], mod[2:3]
    sh2, sc2 = mod[3:4], mod[4:5]

    u, glu = _front(x, sc1, sh1, g_mix_ref[...], w_in_ref)
    ebuf[pl.ds(32, tc), :] = glu

    ub = u.astype(BF16)
    for q in range(4):
        r = jnp.dot(ub[:, q * LANES:(q + 1) * LANES], wb_ref[q], preferred_element_type=F32)
        for k in range(4):
            j = 4 * q + k
            sre[pl.ds(j * pt, tc), :] = r[:, k * LANES:(k + 1) * LANES]
            sim[pl.ds(j * pt, tc), :] = r[:, SSM_W + k * LANES:SSM_W + (k + 1) * LANES]

    ar0, ar1 = a_ref[0, 0:8, :], a_ref[0, 8:16, :]
    ai0, ai1 = a_ref[1, 0:8, :], a_ref[1, 8:16, :]

    def step(g, carry):
        hr0, hr1, hi0, hi1 = carry
        t0 = g * SCAN_GROUP
        idx = [(pl.ds(t0 + u, 8, stride=pt), pl.ds(t0 + u + 8 * pt, 8, stride=pt))
               for u in range(SCAN_GROUP)]
        bu = [(sre[i0, :], sre[i1, :], sim[i0, :], sim[i1, :]) for (i0, i1) in idx]
        hs = []
        for br0, br1, bi0, bi1 in bu:
            nr0 = ar0 * hr0 - ai0 * hi0 + br0
            ni0 = ar0 * hi0 + ai0 * hr0 + bi0
            nr1 = ar1 * hr1 - ai1 * hi1 + br1
            ni1 = ar1 * hi1 + ai1 * hr1 + bi1
            hr0, hr1, hi0, hi1 = nr0, nr1, ni0, ni1
            hs.append((nr0, nr1, ni0, ni1))
        for (i0, i1), (nr0, nr1, ni0, ni1) in zip(idx, hs):
            sre[i0, :] = nr0
            sim[i0, :] = ni0
            sre[i1, :] = nr1
            sim[i1, :] = ni1
        return hr0, hr1, hi0, hi1

    init = (hst[0, 0:8, :], hst[0, 8:16, :], hst[1, 0:8, :], hst[1, 8:16, :])
    hr0, hr1, hi0, hi1 = lax.fori_loop(0, tc // SCAN_GROUP, step, init)
    hst[0, 0:8, :] = hr0
    hst[0, 8:16, :] = hr1
    hst[1, 0:8, :] = hi0
    hst[1, 8:16, :] = hi1

    @pl.when(c == nc_chunks - 1)
    def _():
        hr_ref[...] = hst[0]
        hi_ref[...] = hst[1]

    ys = []
    for q in range(4):
        acc = None
        for jj in range(4):
            j = 4 * q + jj
            lhs = jnp.concatenate([sre[pl.ds(j * pt, tc), :], sim[pl.ds(j * pt, tc), :]], axis=-1)
            d = jnp.dot(lhs.astype(BF16), wc_ref[j], preferred_element_type=F32)
            acc = d if acc is None else acc + d
        ys.append(acc)
    y_lin = jnp.concatenate(ys, axis=-1)
    ns = _ssm_out(y_lin, u, dsk_ref[...], w_glu_ref, b_glu_ref[...], g_os_ref[...])

    for s in range(1, 8):
        eshift[s - 1, pl.ds(0, tc + 24), :] = ebuf[pl.ds(s, tc + 24), :]
    rb = 64
    convs = []
    for r0 in range(0, tc, rb):
        acc = None
        for k in range(KW):
            a8, s = (k + 2) // 8 * 8, (k + 2) % 8
            win = ebuf[pl.ds(r0 + a8, rb), :] if s == 0 else eshift[s - 1, pl.ds(r0 + a8, rb), :]
            term = w_dw_ref[k:k + 1, :] * win
            acc = term if acc is None else acc + term
        convs.append(acc)
    conv = jnp.concatenate(convs, axis=0)
    nc = _conv_out(conv, b_dw_ref[...], ln_g_ref[...], ln_b_ref[...], g_oc_ref[...])

    @pl.when(c == nc_chunks - 1)
    def _():
        cache_ref[...] = ebuf[pl.ds(tc + 2, CB), :]

    ebuf[pl.ds(0, 32), :] = ebuf[pl.ds(tc, 32), :]

    x1, n2, rt, cnt = _tail(x, ns, nc, gt1, sc2, sh2, w_out_ref, g_ffn_ref[...], w_r_ref, b_r_ref[...])
    x1_ref[...] = x1
    n2_ref[...] = n2.astype(BF16)
    rt_ref[...] = rt
    cnt_ref[...] = cnt


def _const_spec(shape):
    nd = len(shape)
    return pl.BlockSpec(shape, lambda b, c: (0,) * nd)


def _prompt_mixer_call(x, mod6, wts, tc):
    bsz, t, _ = x.shape
    n_all = bsz * t
    pt = tc + 8
    nc = t // tc
    kern = functools.partial(_prompt_mixer_kernel, tc=tc, pt=pt)
    in_specs = [pl.BlockSpec((None, tc, D), lambda b, c: (b, c, 0)),
                pl.BlockSpec((None, 6, D), lambda b, c: (b, 0, 0))]
    in_specs += [_const_spec(w.shape) for w in wts]
    out_shape = (jax.ShapeDtypeStruct((n_all, D), F32),
                 jax.ShapeDtypeStruct((n_all, D), BF16),
                 jax.ShapeDtypeStruct((8, n_all), F32),
                 jax.ShapeDtypeStruct((n_all // tc, 1, LANES), F32),
                 jax.ShapeDtypeStruct((bsz, NCHUNK, LANES), F32),
                 jax.ShapeDtypeStruct((bsz, NCHUNK, LANES), F32),
                 jax.ShapeDtypeStruct((bsz, CB, CONV_CH), F32))
    out_specs = (pl.BlockSpec((tc, D), lambda b, c: (b * nc + c, 0)),
                 pl.BlockSpec((tc, D), lambda b, c: (b * nc + c, 0)),
                 pl.BlockSpec((8, tc), lambda b, c: (0, b * nc + c)),
                 pl.BlockSpec((None, 1, LANES), lambda b, c: (b * nc + c, 0, 0)),
                 pl.BlockSpec((None, NCHUNK, LANES), lambda b, c: (b, 0, 0)),
                 pl.BlockSpec((None, NCHUNK, LANES), lambda b, c: (b, 0, 0)),
                 pl.BlockSpec((None, CB, CONV_CH), lambda b, c: (b, 0, 0)))
    scratch = [pltpu.VMEM((NCHUNK * pt, LANES), F32),
               pltpu.VMEM((NCHUNK * pt, LANES), F32),
               pltpu.VMEM((2, NCHUNK, LANES), F32),
               pltpu.VMEM((tc + 32, CONV_CH), F32),
               pltpu.VMEM((7, tc + 32, CONV_CH), F32)]
    return pl.pallas_call(
        kern, grid=(bsz, nc), in_specs=in_specs, out_specs=out_specs, out_shape=out_shape,
        scratch_shapes=scratch,
        compiler_params=pltpu.CompilerParams(dimension_semantics=("arbitrary", "arbitrary"),
                                             vmem_limit_bytes=VMEM_LIMIT),
        name="prompt_mixer",
    )(x, mod6, *wts)


def _sample_mixer_kernel(x_ref, mod_ref, h0r_ref, h0i_ref, cache_ref,
                         g_mix_ref, w_in_ref, wb_ref, a_ref, wc_ref, dsk_ref,
                         w_glu_ref, b_glu_ref, w_dw_ref, b_dw_ref, ln_g_ref, ln_b_ref,
                         g_os_ref, g_oc_ref, w_out_ref, g_ffn_ref, w_r_ref, b_r_ref,
                         x1_ref, n2_ref, rt_ref, cnt_ref, hr_ref, hi_ref, glu_ref,
                         sre, sim, *, nb, nt):
    x = x_ref[...]

    def rows(i):
        m = mod_ref[:, i * D:(i + 1) * D]
        return jnp.concatenate([m] * nt, axis=0)

    sh1, sc1, gt1, sh2, sc2 = rows(0), rows(1), rows(2), rows(3), rows(4)
    u, glu = _front(x, sc1, sh1, g_mix_ref[...], w_in_ref)
    glu_ref[...] = glu

    ub = u.astype(BF16)
    for q in range(4):
        r = jnp.dot(ub[:, q * LANES:(q + 1) * LANES], wb_ref[q], preferred_element_type=F32)
        sre[:, q * SSM_W:(q + 1) * SSM_W] = r[:, :SSM_W]
        sim[:, q * SSM_W:(q + 1) * SSM_W] = r[:, SSM_W:]

    ar = a_ref[0:1, :]
    ai = a_ref[1:2, :]
    hr = h0r_ref[...]
    hi = h0i_ref[...]
    for t in range(nt):
        rs = pl.ds(t * nb, nb)
        nr = ar * hr - ai * hi + sre[rs, :]
        ni = ar * hi + ai * hr + sim[rs, :]
        sre[rs, :] = nr
        sim[rs, :] = ni
        hr, hi = nr, ni
    hr_ref[...] = hr
    hi_ref[...] = hi

    ys = []
    for q in range(4):
        acc = None
        for jj in range(4):
            j = 4 * q + jj
            lhs = jnp.concatenate([sre[:, j * LANES:(j + 1) * LANES],
                                   sim[:, j * LANES:(j + 1) * LANES]], axis=-1)
            d = jnp.dot(lhs.astype(BF16), wc_ref[j], preferred_element_type=F32)
            acc = d if acc is None else acc + d
        ys.append(acc)
    y_lin = jnp.concatenate(ys, axis=-1)
    ns = _ssm_out(y_lin, u, dsk_ref[...], w_glu_ref, b_glu_ref[...], g_os_ref[...])

    def ext(jrow):
        if jrow < CB:
            return cache_ref[jrow]
        return glu[(jrow - CB) * nb:(jrow - CB + 1) * nb, :]

    convs = []
    for t in range(nt):
        acc = None
        for k in range(KW):
            term = w_dw_ref[k:k + 1, :] * ext(t + k)
            acc = term if acc is None else acc + term
        convs.append(acc)
    conv = jnp.concatenate(convs, axis=0)
    nc = _conv_out(conv, b_dw_ref[...], ln_g_ref[...], ln_b_ref[...], g_oc_ref[...])

    x1, n2, rt, cnt = _tail(x, ns, nc, gt1, sc2, sh2, w_out_ref, g_ffn_ref[...], w_r_ref, b_r_ref[...])
    x1_ref[...] = x1
    n2_ref[...] = n2.astype(BF16)
    rt_ref[...] = rt
    cnt_ref[...] = cnt


def _sample_mixer_call(x_tm, mod_s, h0r, h0i, cache_tm, wts, nb, nt):
    n = nb * nt
    kern = functools.partial(_sample_mixer_kernel, nb=nb, nt=nt)
    out_shape = (jax.ShapeDtypeStruct((n, D), F32),
                 jax.ShapeDtypeStruct((n, D), BF16),
                 jax.ShapeDtypeStruct((8, n), F32),
                 jax.ShapeDtypeStruct((1, LANES), F32),
                 jax.ShapeDtypeStruct((nb, NSTATE), F32),
                 jax.ShapeDtypeStruct((nb, NSTATE), F32),
                 jax.ShapeDtypeStruct((n, CONV_CH), F32))
    scratch = [pltpu.VMEM((n, NSTATE), F32), pltpu.VMEM((n, NSTATE), F32)]
    return pl.pallas_call(
        kern, out_shape=out_shape, scratch_shapes=scratch,
        compiler_params=pltpu.CompilerParams(vmem_limit_bytes=VMEM_LIMIT),
        name="sample_mixer",
    )(x_tm, mod_s, h0r, h0i, cache_tm, *wts)


ROW_ALIGN = 16
MOE_TD = 512
MOE_BR = MOE_TD * 2 + NE * ROW_ALIGN
MOE_TG = 512


def _slot_positions(rt):
    t = rt.shape[1]
    e0 = rt[0:1, :]
    e1 = rt[1:2, :]
    sub = lax.broadcasted_iota(jnp.int32, (LANES, t), 0).astype(F32)
    a0 = jnp.where(sub == e0, 1.0, 0.0)
    a1 = jnp.where(sub == e1, 1.0, 0.0)
    at = a0 + a1
    r = lax.broadcasted_iota(jnp.int32, (t, t), 0)
    c = lax.broadcasted_iota(jnp.int32, (t, t), 1)
    before = jnp.where(r < c, 1.0, 0.0).astype(BF16)
    rank = jnp.dot(at.astype(BF16), before, preferred_element_type=F32)
    cnt = jnp.sum(at, axis=1, keepdims=True)
    cnt_al = jnp.ceil(cnt * (1.0 / ROW_ALIGN)) * float(ROW_ALIGN)
    er = lax.broadcasted_iota(jnp.int32, (LANES, LANES), 0)
    ec = lax.broadcasted_iota(jnp.int32, (LANES, LANES), 1)
    lower = jnp.where(ec < er, 1.0, 0.0)
    base = jnp.dot(lower, jnp.broadcast_to(cnt_al, (LANES, LANES)), preferred_element_type=F32,
                   precision=lax.Precision.HIGHEST)[:, 0:1]
    slot = rank + base
    pos0 = jnp.sum(a0 * slot, axis=0, keepdims=True)
    pos1 = jnp.sum(a1 * slot, axis=0, keepdims=True)
    return pos0, pos1


def _segment_copies(tab_ref, t, n_tiles, buf, hbm, sems, slot, to_hbm, wait):
    for e in range(NE):
        n = pl.multiple_of(tab_ref[t * NE + e], ROW_ALIGN)
        b = pl.multiple_of(tab_ref[(n_tiles + t) * NE + e], ROW_ALIGN)
        d = pl.multiple_of(tab_ref[(2 * n_tiles + t) * NE + e], ROW_ALIGN)
        vm = buf.at[slot, pl.ds(b, n)]
        hb = hbm.at[pl.ds(d, n)]
        cp = pltpu.make_async_copy(vm, hb, sems.at[slot, e]) if to_hbm else \
            pltpu.make_async_copy(hb, vm, sems.at[slot, e])

        @pl.when(n > 0)
        def _():
            if wait:
                cp.wait()
            else:
                cp.start()


def _tile_rows(tab_ref, t, n_tiles):
    return tab_ref[3 * n_tiles * NE + 2 * NE + 1 + t]


def _one_hot_rows(r0, nrows, pos0, pos1):
    row = (lax.broadcasted_iota(jnp.int32, (nrows, MOE_TD), 0) + r0).astype(F32)
    return row == pos0, row == pos1


MOE_BLK = 256


def _dispatch_kernel(tab_ref, rtp_ref, n2p_ref, rts_ref, n2s_ref, xs_ref, buf, zbuf, sems, zsem,
                     *, n_tiles, n_ptiles, n_gtiles):
    t = pl.program_id(0)
    slot = lax.rem(t, 2)
    first_free = tab_ref[3 * n_tiles * NE + 2 * NE]

    def fill_copy(j):
        d = pl.multiple_of(j * MOE_TG, MOE_TG)
        return pltpu.make_async_copy(zbuf, xs_ref.at[pl.ds(d, MOE_TG)], zsem)

    def fill_start(j, carry):
        fill_copy(j).start()
        return carry

    def fill_wait(j, carry):
        fill_copy(j).wait()
        return carry

    @pl.when(t == 0)
    def _():
        zbuf[...] = jnp.zeros_like(zbuf)
        for phase in range(2):
            for e in range(NE):
                d = pl.multiple_of(tab_ref[3 * n_tiles * NE + e], ROW_ALIGN)
                n = pl.multiple_of(tab_ref[3 * n_tiles * NE + NE + e], ROW_ALIGN)
                cp = pltpu.make_async_copy(zbuf.at[pl.ds(0, n)], xs_ref.at[pl.ds(d, n)], sems.at[1, e])

                @pl.when(n > 0)
                def _():
                    if phase == 0:
                        cp.start()
                    else:
                        cp.wait()

        lax.fori_loop(first_free, n_gtiles, fill_start, 0)

    @pl.when(t >= 2)
    def _():
        _segment_copies(tab_ref, t - 2, n_tiles, buf, xs_ref, sems, slot, to_hbm=True, wait=True)

    is_sample = t >= n_ptiles
    rt = jnp.where(is_sample, rts_ref[...], rtp_ref[...])
    n2 = jnp.where(is_sample, n2s_ref[...], n2p_ref[...])
    pos0, pos1 = _slot_positions(rt)
    used = _tile_rows(tab_ref, t, n_tiles)

    def group(r0, nrows):
        m0, m1 = _one_hot_rows(r0, nrows, pos0, pos1)
        q = (jnp.where(m0, 1.0, 0.0) + jnp.where(m1, 1.0, 0.0)).astype(BF16)
        buf[slot, pl.ds(r0, nrows), :] = jnp.dot(q, n2, preferred_element_type=F32).astype(BF16)

    group(0, 2 * MOE_TD)
    for r0 in range(2 * MOE_TD, MOE_BR, MOE_BLK):
        @pl.when(used > r0)
        def _():
            group(r0, MOE_BLK)

    _segment_copies(tab_ref, t, n_tiles, buf, xs_ref, sems, slot, to_hbm=True, wait=False)

    @pl.when(t == n_tiles - 1)
    def _():
        if n_tiles >= 2:
            _segment_copies(tab_ref, t - 1, n_tiles, buf, xs_ref, sems, 1 - slot, to_hbm=True, wait=True)
        _segment_copies(tab_ref, t, n_tiles, buf, xs_ref, sems, slot, to_hbm=True, wait=True)
        lax.fori_loop(first_free, n_gtiles, fill_wait, 0)


def _dispatch_call(tab, rt_p, n2_p, rt_s, n2_s, r_max):
    n_ptiles = n2_p.shape[0] // MOE_TD
    n_tiles = n_ptiles + n2_s.shape[0] // MOE_TD
    last_p = n_ptiles - 1
    return pl.pallas_call(
        functools.partial(_dispatch_kernel, n_tiles=n_tiles, n_ptiles=n_ptiles,
                          n_gtiles=r_max // MOE_TG),
        grid_spec=pltpu.PrefetchScalarGridSpec(
            num_scalar_prefetch=1, grid=(n_tiles,),
            in_specs=[pl.BlockSpec((8, MOE_TD), lambda t, tab: (0, jnp.minimum(t, last_p))),
                      pl.BlockSpec((MOE_TD, D), lambda t, tab: (jnp.minimum(t, last_p), 0)),
                      pl.BlockSpec((8, MOE_TD), lambda t, tab: (0, 0)),
                      pl.BlockSpec((MOE_TD, D), lambda t, tab: (0, 0))],
            out_specs=pl.BlockSpec(memory_space=pl.ANY),
            scratch_shapes=[pltpu.VMEM((2, MOE_BR, D), BF16),
                            pltpu.VMEM((MOE_TG, D), BF16),
                            pltpu.SemaphoreType.DMA((2, NE)),
                            pltpu.SemaphoreType.DMA(())]),
        out_shape=jax.ShapeDtypeStruct((r_max, D), BF16),
        compiler_params=pltpu.CompilerParams(dimension_semantics=("arbitrary",),
                                             vmem_limit_bytes=VMEM_LIMIT),
        name="moe_dispatch",
    )(tab, rt_p, n2_p, rt_s, n2_s)


MOE_CK = 3


def _experts_kernel(ctab_ref, xs_ref, w1_ref, w3_ref, w2_ref, ys_ref,
                    w1b, w3b, w2b, xbuf, ybuf, zbuf, in_sem, out_sem, zsem, *, n_gtiles, nc_max):
    e = pl.program_id(0)
    c0 = ctab_ref[e]
    c1 = ctab_ref[e + 1]
    n_chunks = ctab_ref[NE]
    first_free = ctab_ref[NE + 1 + 2 * nc_max]

    def fill_copy(j):
        d = pl.multiple_of(j * MOE_TG, MOE_TG)
        return pltpu.make_async_copy(zbuf, ys_ref.at[pl.ds(d, MOE_TG)], zsem)

    def fill_start(j, carry):
        fill_copy(j).start()
        return carry

    def fill_wait(j, carry):
        fill_copy(j).wait()
        return carry

    def span(c):
        r = pl.multiple_of(ctab_ref[NE + 1 + c], MOE_TG)
        n = pl.multiple_of(ctab_ref[NE + 1 + nc_max + c] * MOE_TG, MOE_TG)
        return r, n

    def in_copy(c, slot):
        r, n = span(c)
        return pltpu.make_async_copy(xs_ref.at[pl.ds(r, n)], xbuf.at[slot, pl.ds(0, n)], in_sem.at[slot])

    def out_copy(c, slot):
        r, n = span(c)
        return pltpu.make_async_copy(ybuf.at[slot, pl.ds(0, n)], ys_ref.at[pl.ds(r, n)], out_sem.at[slot])

    @pl.when(e == 0)
    def _():
        zbuf[...] = jnp.zeros_like(zbuf)
        lax.fori_loop(first_free, n_gtiles, fill_start, 0)

        @pl.when(n_chunks > 0)
        def _():
            in_copy(0, 0).start()

    def compute(slot, rows):
        x = xbuf[slot, pl.ds(0, rows), :]
        a = jnp.dot(x, w1b[...], preferred_element_type=F32)
        b = jnp.dot(x, w3b[...], preferred_element_type=F32)
        hid = a * jax.nn.sigmoid(a) * b
        y = jnp.dot(hid.astype(BF16), w2b[...], preferred_element_type=F32)
        ybuf[slot, pl.ds(0, rows), :] = y.astype(BF16)

    @pl.when(c1 > c0)
    def _():
        w1b[...] = w1_ref[...].astype(BF16)
        w3b[...] = w3_ref[...].astype(BF16)
        w2b[...] = w2_ref[...].astype(BF16)

        def chunk(c, carry):
            slot = lax.rem(c, 2)

            @pl.when(c + 1 < n_chunks)
            def _():
                in_copy(c + 1, 1 - slot).start()

            in_copy(c, slot).wait()

            @pl.when(c >= 2)
            def _():
                out_copy(c - 2, slot).wait()

            k = ctab_ref[NE + 1 + nc_max + c]
            for kk in range(1, MOE_CK + 1):
                @pl.when(k == kk)
                def _():
                    compute(slot, kk * MOE_TG)

            out_copy(c, slot).start()
            return carry

        lax.fori_loop(c0, c1, chunk, 0)

    @pl.when(e == NE - 1)
    def _():
        @pl.when(n_chunks >= 2)
        def _():
            out_copy(n_chunks - 2, lax.rem(n_chunks, 2)).wait()

        @pl.when(n_chunks >= 1)
        def _():
            out_copy(n_chunks - 1, lax.rem(n_chunks - 1, 2)).wait()

        lax.fori_loop(first_free, n_gtiles, fill_wait, 0)


def _experts_call(ctab, xs, w1, w3, w2, nc_max):
    r_max = xs.shape[0]
    w_map = lambda e, ctab: (e, 0, 0)
    ring = pltpu.VMEM((2, MOE_CK * MOE_TG, D), BF16)
    return pl.pallas_call(
        functools.partial(_experts_kernel, n_gtiles=r_max // MOE_TG, nc_max=nc_max),
        grid_spec=pltpu.PrefetchScalarGridSpec(
            num_scalar_prefetch=1, grid=(NE,),
            in_specs=[pl.BlockSpec(memory_space=pl.ANY),
                      pl.BlockSpec((None, D, DE), w_map),
                      pl.BlockSpec((None, D, DE), w_map),
                      pl.BlockSpec((None, DE, D), w_map)],
            out_specs=pl.BlockSpec(memory_space=pl.ANY),
            scratch_shapes=[pltpu.VMEM((D, DE), BF16), pltpu.VMEM((D, DE), BF16),
                            pltpu.VMEM((DE, D), BF16), ring, ring,
                            pltpu.VMEM((MOE_TG, D), BF16),
                            pltpu.SemaphoreType.DMA((2,)), pltpu.SemaphoreType.DMA((2,)),
                            pltpu.SemaphoreType.DMA(())]),
        out_shape=jax.ShapeDtypeStruct((r_max, D), BF16),
        compiler_params=pltpu.CompilerParams(dimension_semantics=("arbitrary",),
                                             vmem_limit_bytes=VMEM_LIMIT),
        name="moe_experts",
    )(ctab, xs, w1, w3, w2)


def _combine_kernel(tab_ref, rt_ref, x1_ref, gt2_ref, gf_ref, ys_ref, y_ref, buf, acc, sems,
                    *, n_tiles, t_off):
    i = pl.program_id(0)
    t = i + t_off
    slot = lax.rem(i, 2)

    @pl.when(i == 0)
    def _():
        buf[...] = jnp.zeros_like(buf)
        _segment_copies(tab_ref, t, n_tiles, buf, ys_ref, sems, slot, to_hbm=False, wait=False)

    @pl.when(i + 1 < pl.num_programs(0))
    def _():
        _segment_copies(tab_ref, t + 1, n_tiles, buf, ys_ref, sems, 1 - slot, to_hbm=False, wait=False)

    _segment_copies(tab_ref, t, n_tiles, buf, ys_ref, sems, slot, to_hbm=False, wait=True)

    rt = rt_ref[...]
    pos0, pos1 = _slot_positions(rt)
    used = _tile_rows(tab_ref, t, n_tiles)

    def ungroup(r0, nrows):
        m0, m1 = _one_hot_rows(r0, nrows, pos0, pos1)
        q = (jnp.where(m0, 1.0, 0.0) + jnp.where(m1, 1.0, 0.0)).astype(BF16)
        gw = jnp.sum(jnp.where(m0, rt[2:3, :], 0.0) + jnp.where(m1, rt[3:4, :], 0.0),
                     axis=1, keepdims=True)
        yv = (buf[slot, pl.ds(r0, nrows), :].astype(F32) * gw).astype(BF16)
        return lax.dot_general(q, yv, (((0,), (0,)), ((), ())), preferred_element_type=F32)

    acc[...] = ungroup(0, 2 * MOE_TD)
    for r0 in range(2 * MOE_TD, MOE_BR, MOE_BLK):
        @pl.when(used > r0)
        def _():
            acc[...] += ungroup(r0, MOE_BLK)

    xo = x1_ref[...] + gt2_ref[...] * acc[...]
    y_ref[...] = _rms(xo) * gf_ref[...]


def _combine_call(tab, rt, x1, gt2, gt2_spec, g_final, ys, n_tiles, t_off):
    n_out_tiles = x1.shape[0] // MOE_TD
    return pl.pallas_call(
        functools.partial(_combine_kernel, n_tiles=n_tiles, t_off=t_off),
        grid_spec=pltpu.PrefetchScalarGridSpec(
            num_scalar_prefetch=1, grid=(n_out_tiles,),
            in_specs=[pl.BlockSpec((8, MOE_TD), lambda t, tab: (0, t)),
                      pl.BlockSpec((MOE_TD, D), lambda t, tab: (t, 0)),
                      gt2_spec,
                      pl.BlockSpec((1, D), lambda t, tab: (0, 0)),
                      pl.BlockSpec(memory_space=pl.ANY)],
            out_specs=pl.BlockSpec((MOE_TD, D), lambda t, tab: (t, 0)),
            scratch_shapes=[pltpu.VMEM((2, MOE_BR, D), BF16),
                            pltpu.VMEM((MOE_TD, D), F32),
                            pltpu.SemaphoreType.DMA((2, NE))]),
        out_shape=jax.ShapeDtypeStruct((n_out_tiles * MOE_TD, D), F32),
        compiler_params=pltpu.CompilerParams(dimension_semantics=("arbitrary",),
                                             vmem_limit_bytes=VMEM_LIMIT),
        name="moe_combine",
    )(tab, rt, x1, gt2, g_final.reshape(1, D), ys)


def _moe_plan(cnt, nc_max):
    cnt8 = (cnt + ROW_ALIGN - 1) // ROW_ALIGN * ROW_ALIGN
    seg_rows = cnt8.sum(axis=0)
    seg_pad = (seg_rows + MOE_TG - 1) // MOE_TG * MOE_TG
    seg_start = jnp.cumsum(seg_pad) - seg_pad
    dst = seg_start[None, :] + jnp.cumsum(cnt8, axis=0) - cnt8
    boff = jnp.cumsum(cnt8, axis=1) - cnt8
    tile_end = jnp.cumsum(seg_pad // MOE_TG)
    n_active = tile_end[-1:].astype(jnp.int32)
    tab = jnp.concatenate([cnt8.ravel(), boff.ravel(), dst.ravel(),
                           seg_start + seg_rows, seg_pad - seg_rows, n_active,
                           cnt8.sum(axis=1)]).astype(jnp.int32)
    nt = seg_pad // MOE_TG
    nfull = nt // MOE_CK
    rem = nt % MOE_CK
    nch = nfull + (rem > 0)
    cend = jnp.cumsum(nch)
    cstart = cend - nch
    c = jnp.arange(nc_max, dtype=jnp.int32)
    ce = jnp.minimum(jnp.sum(c[:, None] >= cend[None, :], axis=1), NE - 1)
    local = c - cstart[ce]
    valid = c < cend[-1]
    ck = jnp.where(valid, jnp.where(local < nfull[ce], MOE_CK, rem[ce]), 0)
    crow = jnp.where(valid, seg_start[ce] + local * (MOE_CK * MOE_TG), 0)
    ctab = jnp.concatenate([cstart, cend[-1:], crow, ck, n_active]).astype(jnp.int32)
    return tab, ctab


def kernel(x_prompt, x_sample, c_prompt, c_sample, state_ssm_re, state_ssm_im, cache_conv, w_ada, b_ada, g_norm_mix, w_in, ssm_a_re, ssm_a_im, ssm_log_dt, ssm_b_re, ssm_b_im, ssm_c_re, ssm_c_im, ssm_d, w_ssm_glu, b_ssm_glu, w_dw, b_dw, ln_conv_g, ln_conv_b, g_out_ssm, g_out_conv, w_out, g_norm_ffn, w_router_grp, b_router_grp, w_router_exp, b_router_exp, w_exp_gate, w_exp_up, w_exp_down, g_final):
    depth = w_ada.shape[0]
    assert depth == 1
    bsz, seq, _ = x_prompt.shape
    nb, nt, _ = x_sample.shape

    c_all = jnp.concatenate([c_prompt, c_sample], axis=0)
    mod = _mod_call(c_all, w_ada[0], b_ada[0])
    mod_p = mod[:bsz].reshape(bsz, 6, D)
    mod_s = mod[bsz:]

    ab_re, ab_im, bb_re, bb_im, c_im_neg = _ssm_prep_call(
        ssm_a_re[0], ssm_a_im[0], ssm_log_dt[0], ssm_b_re[0], ssm_b_im[0], ssm_c_im[0])
    wb, wc = _block_diag_weights(bb_re, bb_im, ssm_c_re[0], c_im_neg)
    a_tok = jnp.stack([ab_re.reshape(NCHUNK, LANES), ab_im.reshape(NCHUNK, LANES)])
    a_row = jnp.stack([ab_re.reshape(NSTATE), ab_im.reshape(NSTATE)])

    w_r = jnp.concatenate([w_router_exp[0].reshape(D, NE), w_router_grp[0],
                           jnp.zeros((D, LANES - NE - NG), F32)], axis=1)
    w_r_hi = w_r.astype(BF16)
    w_r = jnp.concatenate([w_r_hi, (w_r - w_r_hi.astype(F32)).astype(BF16)], axis=1)
    b_r = jnp.concatenate([b_router_exp[0].reshape(NE), b_router_grp[0],
                           jnp.zeros((LANES - NE - NG,), F32)]).reshape(1, LANES)
    w_dw_p = jnp.concatenate([w_dw[0], jnp.zeros((1, CONV_CH), F32)], axis=0)

    row = lambda v: v.reshape(1, -1)
    common_a = (row(g_norm_mix[0]), w_in[0].astype(BF16), wb)
    common_b = (wc, row(ssm_d[0].reshape(SSM_W)), w_ssm_glu[0].astype(BF16), row(b_ssm_glu[0]),
                w_dw_p, row(b_dw[0]), row(ln_conv_g[0]), row(ln_conv_b[0]),
                row(g_out_ssm[0]), row(g_out_conv[0]), w_out[0].astype(BF16),
                row(g_norm_ffn[0]), w_r, b_r)

    n_p = bsz * seq
    n_s = nb * nt
    n_all = n_p + n_s
    assert n_s == MOE_TD and seq % MOE_TD == 0 and MOE_TD % PROMPT_TC == 0
    wts_p = common_a + (a_tok,) + common_b
    x1_p, n2_p, rt_p, cnt_p, hr_p, hi_p, cache_p = _prompt_mixer_call(x_prompt, mod_p, wts_p, PROMPT_TC)

    x_tm = jnp.transpose(x_sample, (1, 0, 2)).reshape(nt * nb, D)
    cache_tm = jnp.transpose(cache_conv[0], (1, 0, 2))
    wts_s = common_a + (a_row,) + common_b
    x1_s, n2_s, rt_s, cnt_s, hr_s, hi_s, glu_s = _sample_mixer_call(
        x_tm, mod_s, state_ssm_re[0].reshape(nb, NSTATE), state_ssm_im[0].reshape(nb, NSTATE),
        cache_tm, wts_s, nb, nt)

    n_ptiles = n_p // MOE_TD
    n_tiles = n_all // MOE_TD
    r_max = -(-(2 * n_all + n_tiles * NE * (ROW_ALIGN - 1) + NE * (MOE_TG - ROW_ALIGN)) // MOE_TG) * MOE_TG
    cnt = jnp.concatenate([cnt_p.reshape(n_ptiles, MOE_TD // PROMPT_TC, LANES).sum(axis=1), cnt_s])
    nc_max = r_max // MOE_TG // MOE_CK + NE
    tab, ctab = _moe_plan(cnt[:, :NE].astype(jnp.int32), nc_max)
    xs = _dispatch_call(tab, rt_p, n2_p, rt_s, n2_s, r_max)
    ys = _experts_call(ctab, xs, w_exp_gate[0], w_exp_up[0], w_exp_down[0], nc_max)
    tiles_per_b = seq // MOE_TD
    gt2_p = mod_p[:, 5:6, :]
    y_p = _combine_call(tab, rt_p, x1_p, gt2_p,
                        pl.BlockSpec((None, 1, D), lambda t, tab: (t // tiles_per_b, 0, 0)),
                        g_final, ys, n_tiles, 0)
    gt2_s = jnp.tile(mod_s[:, 5 * D:], (nt, 1))
    y_s = _combine_call(tab, rt_s, x1_s, gt2_s,
                        pl.BlockSpec((n_s, D), lambda t, tab: (0, 0)),
                        g_final, ys, n_tiles, n_ptiles)

    y_prompt = y_p.reshape(bsz, seq, D)
    y_sample = jnp.transpose(y_s.reshape(nt, nb, D), (1, 0, 2))
    new_cache_s = jnp.concatenate(
        [cache_conv[0][:, nt:, :], jnp.transpose(glu_s.reshape(nt, nb, CONV_CH), (1, 0, 2))], axis=1)
    return (y_prompt, y_sample,
            hr_p.reshape(1, bsz, G, P), hi_p.reshape(1, bsz, G, P), cache_p[None],
            hr_s.reshape(1, nb, G, P), hi_s.reshape(1, nb, G, P), new_cache_s[None])
```

```python
import functools

import jax
import jax.numpy as jnp
import numpy as np
from jax import lax
from jax.experimental import pallas as pl
from jax.experimental.pallas import tpu as pltpu

F32 = jnp.float32
BF16 = jnp.bfloat16

D = 1024
SSM_W = 512
CONV_CH = 512
G = 32
H = 16
P = 64
KW = 31
CB = KW - 1
NE = 32
NG = 4
EPG = 8
DE = 512
EPS = 1e-6
LANES = 128
NSTATE = G * P
NCHUNK = NSTATE // LANES

PROMPT_TC = 512
SCAN_GROUP = 8
VMEM_LIMIT = 56 * 1024 * 1024


def _rms(x):
    return x * lax.rsqrt(jnp.mean(x * x, axis=-1, keepdims=True) + EPS)


def _sigmoid(x):
    return 0.5 * jnp.tanh(0.5 * x) + 0.5


def _gelu_tanh(y):
    c = np.sqrt(2.0 / np.pi).astype(np.float32)
    return y * (0.5 * (1.0 + jnp.tanh(c * (y + 0.044715 * (y * y * y)))))


def _bdot(a, b):
    return jnp.dot(a.astype(BF16), b, preferred_element_type=F32)


def _mod_kernel(c_ref, w_ref, b_ref, o_ref):
    c = c_ref[...]
    s = c * _sigmoid(c)
    o_ref[...] = jnp.dot(s, w_ref[...], preferred_element_type=F32,
                         precision=lax.Precision.HIGHEST) + b_ref[...]


def _mod_call(c_all, w_ada, b_ada):
    n = c_all.shape[0]
    tn = 512
    return pl.pallas_call(
        _mod_kernel,
        grid=(6 * D // tn,),
        in_specs=[pl.BlockSpec((n, D), lambda j: (0, 0)),
                  pl.BlockSpec((D, tn), lambda j: (0, j)),
                  pl.BlockSpec((1, tn), lambda j: (0, j))],
        out_specs=pl.BlockSpec((n, tn), lambda j: (0, j)),
        out_shape=jax.ShapeDtypeStruct((n, 6 * D), F32),
        compiler_params=pltpu.CompilerParams(dimension_semantics=("arbitrary",)),
        name="mod",
    )(c_all, w_ada, b_ada.reshape(1, 6 * D))


def _ssm_prep_kernel(a_re, a_im, log_dt, b_re, b_im, c_im,
                     ab_re_o, ab_im_o, bb_re_o, bb_im_o, cneg_o):
    lam_re = jnp.minimum(a_re[...], -1e-4)
    lam_im = a_im[...]
    dt = jnp.exp(log_dt[...])
    mag = jnp.exp(lam_re * dt)
    ab_re = mag * jnp.cos(lam_im * dt)
    ab_im = mag * jnp.sin(lam_im * dt)
    den = lam_re * lam_re + lam_im * lam_im
    num_re = ab_re - 1.0
    coef_re = (num_re * lam_re + ab_im * lam_im) / den
    coef_im = (ab_im * lam_re - num_re * lam_im) / den
    ab_re_o[...] = ab_re
    ab_im_o[...] = ab_im
    br = b_re[...]
    bi = b_im[...]
    bb_re_o[...] = coef_re * br - coef_im * bi
    bb_im_o[...] = coef_re * bi + coef_im * br
    cneg_o[...] = -c_im[...]


def _ssm_prep_call(a_re, a_im, log_dt, b_re, b_im, c_im):
    flat = lambda v: v.reshape(1, NSTATE)
    b_hs = lambda v: jnp.transpose(v, (2, 0, 1)).reshape(H, NSTATE)
    dt_row = jnp.broadcast_to(log_dt[:, None], (G, P)).reshape(1, NSTATE)
    ab_re, ab_im, bb_re, bb_im, cneg = pl.pallas_call(
        _ssm_prep_kernel,
        out_shape=(jax.ShapeDtypeStruct((1, NSTATE), F32), jax.ShapeDtypeStruct((1, NSTATE), F32),
                   jax.ShapeDtypeStruct((H, NSTATE), F32), jax.ShapeDtypeStruct((H, NSTATE), F32),
                   jax.ShapeDtypeStruct((G * H, P), F32)),
        name="ssm_prep",
    )(flat(a_re), flat(a_im), dt_row, b_hs(b_re), b_hs(b_im), c_im.reshape(G * H, P))
    ghp = lambda v: jnp.transpose(v.reshape(H, G, P), (1, 0, 2))
    return (ab_re.reshape(G, P), ab_im.reshape(G, P), ghp(bb_re), ghp(bb_im), cneg.reshape(G, H, P))


def _block_diag_weights(bb_re, bb_im, c_re, c_im_neg):
    eye8 = jnp.eye(8, dtype=F32)
    eye4 = jnp.eye(4, dtype=F32)
    eye2 = jnp.eye(2, dtype=F32)

    def wb_part(bb):
        x = bb.reshape(4, 8, H, P)
        return jnp.einsum('qghp,gk->qghkp', x, eye8).reshape(4, 8 * H, 8 * P)

    wb = jnp.concatenate([wb_part(bb_re), wb_part(bb_im)], axis=-1).astype(BF16)

    def wc_part(c):
        x = c.reshape(4, 4, 2, H, P)
        y = jnp.einsum('qjghp,jk,gl->qjgpklh', x, eye4, eye2)
        return y.reshape(NCHUNK, 2 * P, 4 * 2 * H)

    wc = jnp.concatenate([wc_part(c_re), wc_part(c_im_neg)], axis=1).astype(BF16)
    return wb, wc


def _front(x, sc1, sh1, g_mix, w_in_ref):
    n = _rms(x) * g_mix * (1.0 + sc1) + sh1
    proj = _bdot(n, w_in_ref[...])
    u = proj[:, :SSM_W]
    glu = proj[:, SSM_W:SSM_W + CONV_CH] * _sigmoid(proj[:, SSM_W + CONV_CH:])
    return u, glu


def _ssm_out(y_lin, u, d_skip, w_glu_ref, b_glu, g_out_ssm):
    y = _gelu_tanh(y_lin + d_skip * u)
    ys = y * _sigmoid(_bdot(y, w_glu_ref[...]) + b_glu)
    return _rms(ys) * g_out_ssm


def _conv_out(conv, b_dw, ln_g, ln_b, g_out_conv):
    c = conv + b_dw
    mu = jnp.mean(c, axis=-1, keepdims=True)
    cc = c - mu
    var = jnp.mean(cc * cc, axis=-1, keepdims=True)
    ln = cc * lax.rsqrt(var + EPS) * ln_g + ln_b
    yc = ln * _sigmoid(ln)
    return _rms(yc) * g_out_conv


def _route(n2, w_r_ref, b_r):
    rows = n2.shape[0]
    n_hi = n2.astype(BF16)
    n_lo = (n2 - n_hi.astype(F32)).astype(BF16)
    parts = jnp.dot(jnp.concatenate([n_hi, n_lo], axis=0), w_r_ref[...], preferred_element_type=F32)
    lg = (parts[:rows, :LANES] + parts[:rows, LANES:]) + (parts[rows:, :LANES] + parts[rows:, LANES:]) + b_r
    lane = lax.broadcasted_iota(jnp.int32, (rows, LANES), 1).astype(F32)
    ninf = -jnp.inf
    big = 1e9
    gmask = jnp.logical_and(lane >= NE, lane < NE + NG)
    gl = jnp.where(gmask, lg, ninf)
    gmax = jnp.max(gl, axis=-1, keepdims=True)
    gsum = jnp.sum(jnp.where(gmask, jnp.exp(gl - gmax), 0.0), axis=-1, keepdims=True)
    p_top = 1.0 / gsum
    gi = jnp.min(jnp.where(gl == gmax, lane, big), axis=-1, keepdims=True) - NE
    lo = gi * EPG
    emask = jnp.logical_and(lane >= lo, lane < lo + EPG)
    el = jnp.where(emask, lg, ninf)
    m1 = jnp.max(el, axis=-1, keepdims=True)
    i1 = jnp.min(jnp.where(el == m1, lane, big), axis=-1, keepdims=True)
    el2 = jnp.where(lane == i1, ninf, el)
    m2 = jnp.max(el2, axis=-1, keepdims=True)
    i2 = jnp.min(jnp.where(el2 == m2, lane, big), axis=-1, keepdims=True)
    e2 = jnp.exp(m2 - m1)
    den = 1.0 + e2
    w1 = p_top / den
    w2 = p_top * e2 / den
    cnt = jnp.sum(jnp.where(lane == i1, 1.0, 0.0) + jnp.where(lane == i2, 1.0, 0.0),
                  axis=0, keepdims=True)
    cols = (jnp.where(lane == 0.0, i1, 0.0) + jnp.where(lane == 1.0, i2, 0.0)
            + jnp.where(lane == 2.0, w1, 0.0) + jnp.where(lane == 3.0, w2, 0.0))
    return cols.T[0:8, :], cnt


def _tail(x, ns, nc, gt1, sc2, sh2, w_out_ref, g_ffn, w_r_ref, b_r):
    merged = _bdot(jnp.concatenate([ns, nc], axis=-1), w_out_ref[...])
    x1 = x + gt1 * merged
    n2 = _rms(x1) * g_ffn * (1.0 + sc2) + sh2
    rt, cnt = _route(n2, w_r_ref, b_r)
    return x1, n2, rt, cnt


def _prompt_mixer_kernel(x_ref, mod_ref, g_mix_ref, w_in_ref, wb_ref, a_ref, wc_ref, dsk_ref,
                         w_glu_ref, b_glu_ref, w_dw_ref, b_dw_ref, ln_g_ref, ln_b_ref,
                         g_os_ref, g_oc_ref, w_out_ref, g_ffn_ref, w_r_ref, b_r_ref,
                         x1_ref, n2_ref, rt_ref, cnt_ref, hr_ref, hi_ref, cache_ref,
                         sre, sim, hst, ebuf, eshift, *, tc, pt):
    c = pl.program_id(1)
    nc_chunks = pl.num_programs(1)

    @pl.when(c == 0)
    def _():
        hst[...] = jnp.zeros_like(hst)
        ebuf[pl.ds(0, 32), :] = jnp.zeros((32, CONV_CH), F32)

    x = x_ref[...]
    mod = mod_ref[...]
    sh1, sc1, gt1 = mod[0:1], mod[1:2], mod[2:3]
    sh2, sc2 = mod[3:4], mod[4:5]

    u, glu = _front(x, sc1, sh1, g_mix_ref[...], w_in_ref)
    ebuf[pl.ds(32, tc), :] = glu

    ub = u.astype(BF16)
    for q in range(4):
        r = jnp.dot(ub[:, q * LANES:(q + 1) * LANES], wb_ref[q], preferred_element_type=F32)
        for k in range(4):
            j = 4 * q + k
            sre[pl.ds(j * pt, tc), :] = r[:, k * LANES:(k + 1) * LANES]
            sim[pl.ds(j * pt, tc), :] = r[:, SSM_W + k * LANES:SSM_W + (k + 1) * LANES]

    ar0, ar1 = a_ref[0, 0:8, :], a_ref[0, 8:16, :]
    ai0, ai1 = a_ref[1, 0:8, :], a_ref[1, 8:16, :]

    def step(g, carry):
        hr0, hr1, hi0, hi1 = carry
        t0 = g * SCAN_GROUP
        idx = [(pl.ds(t0 + u, 8, stride=pt), pl.ds(t0 + u + 8 * pt, 8, stride=pt))
               for u in range(SCAN_GROUP)]
        bu = [(sre[i0, :], sre[i1, :], sim[i0, :], sim[i1, :]) for (i0, i1) in idx]
        hs = []
        for br0, br1, bi0, bi1 in bu:
            nr0 = ar0 * hr0 - ai0 * hi0 + br0
            ni0 = ar0 * hi0 + ai0 * hr0 + bi0
            nr1 = ar1 * hr1 - ai1 * hi1 + br1
            ni1 = ar1 * hi1 + ai1 * hr1 + bi1
            hr0, hr1, hi0, hi1 = nr0, nr1, ni0, ni1
            hs.append((nr0, nr1, ni0, ni1))
        for (i0, i1), (nr0, nr1, ni0, ni1) in zip(idx, hs):
            sre[i0, :] = nr0
            sim[i0, :] = ni0
            sre[i1, :] = nr1
            sim[i1, :] = ni1
        return hr0, hr1, hi0, hi1

    init = (hst[0, 0:8, :], hst[0, 8:16, :], hst[1, 0:8, :], hst[1, 8:16, :])
    hr0, hr1, hi0, hi1 = lax.fori_loop(0, tc // SCAN_GROUP, step, init)
    hst[0, 0:8, :] = hr0
    hst[0, 8:16, :] = hr1
    hst[1, 0:8, :] = hi0
    hst[1, 8:16, :] = hi1

    @pl.when(c == nc_chunks - 1)
    def _():
        hr_ref[...] = hst[0]
        hi_ref[...] = hst[1]

    ys = []
    for q in range(4):
        acc = None
        for jj in range(4):
            j = 4 * q + jj
            lhs = jnp.concatenate([sre[pl.ds(j * pt, tc), :], sim[pl.ds(j * pt, tc), :]], axis=-1)
            d = jnp.dot(lhs.astype(BF16), wc_ref[j], preferred_element_type=F32)
            acc = d if acc is None else acc + d
        ys.append(acc)
    y_lin = jnp.concatenate(ys, axis=-1)
    ns = _ssm_out(y_lin, u, dsk_ref[...], w_glu_ref, b_glu_ref[...], g_os_ref[...])

    for s in range(1, 8):
        eshift[s - 1, pl.ds(0, tc + 24), :] = ebuf[pl.ds(s, tc + 24), :]
    rb = 64
    convs = []
    for r0 in range(0, tc, rb):
        acc = None
        for k in range(KW):
            a8, s = (k + 2) // 8 * 8, (k + 2) % 8
            win = ebuf[pl.ds(r0 + a8, rb), :] if s == 0 else eshift[s - 1, pl.ds(r0 + a8, rb), :]
            term = w_dw_ref[k:k + 1, :] * win
            acc = term if acc is None else acc + term
        convs.append(acc)
    conv = jnp.concatenate(convs, axis=0)
    nc = _conv_out(conv, b_dw_ref[...], ln_g_ref[...], ln_b_ref[...], g_oc_ref[...])

    @pl.when(c == nc_chunks - 1)
    def _():
        cache_ref[...] = ebuf[pl.ds(tc + 2, CB), :]

    ebuf[pl.ds(0, 32), :] = ebuf[pl.ds(tc, 32), :]

    x1, n2, rt, cnt = _tail(x, ns, nc, gt1, sc2, sh2, w_out_ref, g_ffn_ref[...], w_r_ref, b_r_ref[...])
    x1_ref[...] = x1
    n2_ref[...] = n2.astype(BF16)
    rt_ref[...] = rt
    cnt_ref[...] = cnt


def _const_spec(shape):
    nd = len(shape)
    return pl.BlockSpec(shape, lambda b, c: (0,) * nd)


def _prompt_mixer_call(x, mod6, wts, tc):
    bsz, t, _ = x.shape
    n_all = bsz * t
    pt = tc + 8
    nc = t // tc
    kern = functools.partial(_prompt_mixer_kernel, tc=tc, pt=pt)
    in_specs = [pl.BlockSpec((None, tc, D), lambda b, c: (b, c, 0)),
                pl.BlockSpec((None, 6, D), lambda b, c: (b, 0, 0))]
    in_specs += [_const_spec(w.shape) for w in wts]
    out_shape = (jax.ShapeDtypeStruct((n_all, D), F32),
                 jax.ShapeDtypeStruct((n_all, D), BF16),
                 jax.ShapeDtypeStruct((8, n_all), F32),
                 jax.ShapeDtypeStruct((n_all // tc, 1, LANES), F32),
                 jax.ShapeDtypeStruct((bsz, NCHUNK, LANES), F32),
                 jax.ShapeDtypeStruct((bsz, NCHUNK, LANES), F32),
                 jax.ShapeDtypeStruct((bsz, CB, CONV_CH), F32))
    out_specs = (pl.BlockSpec((tc, D), lambda b, c: (b * nc + c, 0)),
                 pl.BlockSpec((tc, D), lambda b, c: (b * nc + c, 0)),
                 pl.BlockSpec((8, tc), lambda b, c: (0, b * nc + c)),
                 pl.BlockSpec((None, 1, LANES), lambda b, c: (b * nc + c, 0, 0)),
                 pl.BlockSpec((None, NCHUNK, LANES), lambda b, c: (b, 0, 0)),
                 pl.BlockSpec((None, NCHUNK, LANES), lambda b, c: (b, 0, 0)),
                 pl.BlockSpec((None, CB, CONV_CH), lambda b, c: (b, 0, 0)))
    scratch = [pltpu.VMEM((NCHUNK * pt, LANES), F32),
               pltpu.VMEM((NCHUNK * pt, LANES), F32),
               pltpu.VMEM((2, NCHUNK, LANES), F32),
               pltpu.VMEM((tc + 32, CONV_CH), F32),
               pltpu.VMEM((7, tc + 32, CONV_CH), F32)]
    return pl.pallas_call(
        kern, grid=(bsz, nc), in_specs=in_specs, out_specs=out_specs, out_shape=out_shape,
        scratch_shapes=scratch,
        compiler_params=pltpu.CompilerParams(dimension_semantics=("arbitrary", "arbitrary"),
                                             vmem_limit_bytes=VMEM_LIMIT),
        name="prompt_mixer",
    )(x, mod6, *wts)


def _sample_mixer_kernel(x_ref, mod_ref, h0r_ref, h0i_ref, cache_ref,
                         g_mix_ref, w_in_ref, wb_ref, a_ref, wc_ref, dsk_ref,
                         w_glu_ref, b_glu_ref, w_dw_ref, b_dw_ref, ln_g_ref, ln_b_ref,
                         g_os_ref, g_oc_ref, w_out_ref, g_ffn_ref, w_r_ref, b_r_ref,
                         x1_ref, n2_ref, rt_ref, cnt_ref, hr_ref, hi_ref, glu_ref,
                         sre, sim, *, nb, nt):
    x = x_ref[...]

    def rows(i):
        m = mod_ref[:, i * D:(i + 1) * D]
        return jnp.concatenate([m] * nt, axis=0)

    sh1, sc1, gt1, sh2, sc2 = rows(0), rows(1), rows(2), rows(3), rows(4)
    u, glu = _front(x, sc1, sh1, g_mix_ref[...], w_in_ref)
    glu_ref[...] = glu

    ub = u.astype(BF16)
    for q in range(4):
        r = jnp.dot(ub[:, q * LANES:(q + 1) * LANES], wb_ref[q], preferred_element_type=F32)
        sre[:, q * SSM_W:(q + 1) * SSM_W] = r[:, :SSM_W]
        sim[:, q * SSM_W:(q + 1) * SSM_W] = r[:, SSM_W:]

    ar = a_ref[0:1, :]
    ai = a_ref[1:2, :]
    hr = h0r_ref[...]
    hi = h0i_ref[...]
    for t in range(nt):
        rs = pl.ds(t * nb, nb)
        nr = ar * hr - ai * hi + sre[rs, :]
        ni = ar * hi + ai * hr + sim[rs, :]
        sre[rs, :] = nr
        sim[rs, :] = ni
        hr, hi = nr, ni
    hr_ref[...] = hr
    hi_ref[...] = hi

    ys = []
    for q in range(4):
        acc = None
        for jj in range(4):
            j = 4 * q + jj
            lhs = jnp.concatenate([sre[:, j * LANES:(j + 1) * LANES],
                                   sim[:, j * LANES:(j + 1) * LANES]], axis=-1)
            d = jnp.dot(lhs.astype(BF16), wc_ref[j], preferred_element_type=F32)
            acc = d if acc is None else acc + d
        ys.append(acc)
    y_lin = jnp.concatenate(ys, axis=-1)
    ns = _ssm_out(y_lin, u, dsk_ref[...], w_glu_ref, b_glu_ref[...], g_os_ref[...])

    def ext(jrow):
        if jrow < CB:
            return cache_ref[jrow]
        return glu[(jrow - CB) * nb:(jrow - CB + 1) * nb, :]

    convs = []
    for t in range(nt):
        acc = None
        for k in range(KW):
            term = w_dw_ref[k:k + 1, :] * ext(t + k)
            acc = term if acc is None else acc + term
        convs.append(acc)
    conv = jnp.concatenate(convs, axis=0)
    nc = _conv_out(conv, b_dw_ref[...], ln_g_ref[...], ln_b_ref[...], g_oc_ref[...])

    x1, n2, rt, cnt = _tail(x, ns, nc, gt1, sc2, sh2, w_out_ref, g_ffn_ref[...], w_r_ref, b_r_ref[...])
    x1_ref[...] = x1
    n2_ref[...] = n2.astype(BF16)
    rt_ref[...] = rt
    cnt_ref[...] = cnt


def _sample_mixer_call(x_tm, mod_s, h0r, h0i, cache_tm, wts, nb, nt):
    n = nb * nt
    kern = functools.partial(_sample_mixer_kernel, nb=nb, nt=nt)
    out_shape = (jax.ShapeDtypeStruct((n, D), F32),
                 jax.ShapeDtypeStruct((n, D), BF16),
                 jax.ShapeDtypeStruct((8, n), F32),
                 jax.ShapeDtypeStruct((1, LANES), F32),
                 jax.ShapeDtypeStruct((nb, NSTATE), F32),
                 jax.ShapeDtypeStruct((nb, NSTATE), F32),
                 jax.ShapeDtypeStruct((n, CONV_CH), F32))
    scratch = [pltpu.VMEM((n, NSTATE), F32), pltpu.VMEM((n, NSTATE), F32)]
    return pl.pallas_call(
        kern, out_shape=out_shape, scratch_shapes=scratch,
        compiler_params=pltpu.CompilerParams(vmem_limit_bytes=VMEM_LIMIT),
        name="sample_mixer",
    )(x_tm, mod_s, h0r, h0i, cache_tm, *wts)


ROW_ALIGN = 16
MOE_TD = 512
MOE_BR = MOE_TD * 2 + NE * ROW_ALIGN
MOE_TG = 512


def _slot_positions(rt):
    t = rt.shape[1]
    e0 = rt[0:1, :]
    e1 = rt[1:2, :]
    sub = lax.broadcasted_iota(jnp.int32, (LANES, t), 0).astype(F32)
    a0 = jnp.where(sub == e0, 1.0, 0.0)
    a1 = jnp.where(sub == e1, 1.0, 0.0)
    at = a0 + a1
    r = lax.broadcasted_iota(jnp.int32, (t, t), 0)
    c = lax.broadcasted_iota(jnp.int32, (t, t), 1)
    before = jnp.where(r < c, 1.0, 0.0).astype(BF16)
    rank = jnp.dot(at.astype(BF16), before, preferred_element_type=F32)
    cnt = jnp.sum(at, axis=1, keepdims=True)
    cnt_al = jnp.ceil(cnt * (1.0 / ROW_ALIGN)) * float(ROW_ALIGN)
    er = lax.broadcasted_iota(jnp.int32, (LANES, LANES), 0)
    ec = lax.broadcasted_iota(jnp.int32, (LANES, LANES), 1)
    lower = jnp.where(ec < er, 1.0, 0.0)
    base = jnp.dot(lower, jnp.broadcast_to(cnt_al, (LANES, LANES)), preferred_element_type=F32,
                   precision=lax.Precision.HIGHEST)[:, 0:1]
    slot = rank + base
    pos0 = jnp.sum(a0 * slot, axis=0, keepdims=True)
    pos1 = jnp.sum(a1 * slot, axis=0, keepdims=True)
    return pos0, pos1


def _segment_copies(tab_ref, t, n_tiles, buf, hbm, sems, slot, to_hbm, wait):
    for e in range(NE):
        n = pl.multiple_of(tab_ref[t * NE + e], ROW_ALIGN)
        b = pl.multiple_of(tab_ref[(n_tiles + t) * NE + e], ROW_ALIGN)
        d = pl.multiple_of(tab_ref[(2 * n_tiles + t) * NE + e], ROW_ALIGN)
        vm = buf.at[slot, pl.ds(b, n)]
        hb = hbm.at[pl.ds(d, n)]
        cp = pltpu.make_async_copy(vm, hb, sems.at[slot, e]) if to_hbm else \
            pltpu.make_async_copy(hb, vm, sems.at[slot, e])

        @pl.when(n > 0)
        def _():
            if wait:
                cp.wait()
            else:
                cp.start()


def _tile_rows(tab_ref, t, n_tiles):
    return tab_ref[3 * n_tiles * NE + 2 * NE + 1 + t]


def _one_hot_rows(r0, nrows, pos0, pos1):
    row = (lax.broadcasted_iota(jnp.int32, (nrows, MOE_TD), 0) + r0).astype(F32)
    return row == pos0, row == pos1


MOE_BLK = 256


def _dispatch_kernel(tab_ref, rtp_ref, n2p_ref, rts_ref, n2s_ref, xs_ref, buf, zbuf, sems, zsem,
                     *, n_tiles, n_ptiles, n_gtiles):
    t = pl.program_id(0)
    slot = lax.rem(t, 2)
    first_free = tab_ref[3 * n_tiles * NE + 2 * NE]

    def fill_copy(j):
        d = pl.multiple_of(j * MOE_TG, MOE_TG)
        return pltpu.make_async_copy(zbuf, xs_ref.at[pl.ds(d, MOE_TG)], zsem)

    def fill_start(j, carry):
        fill_copy(j).start()
        return carry

    def fill_wait(j, carry):
        fill_copy(j).wait()
        return carry

    @pl.when(t == 0)
    def _():
        zbuf[...] = jnp.zeros_like(zbuf)
        for phase in range(2):
            for e in range(NE):
                d = pl.multiple_of(tab_ref[3 * n_tiles * NE + e], ROW_ALIGN)
                n = pl.multiple_of(tab_ref[3 * n_tiles * NE + NE + e], ROW_ALIGN)
                cp = pltpu.make_async_copy(zbuf.at[pl.ds(0, n)], xs_ref.at[pl.ds(d, n)], sems.at[1, e])

                @pl.when(n > 0)
                def _():
                    if phase == 0:
                        cp.start()
                    else:
                        cp.wait()

        lax.fori_loop(first_free, n_gtiles, fill_start, 0)

    @pl.when(t >= 2)
    def _():
        _segment_copies(tab_ref, t - 2, n_tiles, buf, xs_ref, sems, slot, to_hbm=True, wait=True)

    is_sample = t >= n_ptiles
    rt = jnp.where(is_sample, rts_ref[...], rtp_ref[...])
    n2 = jnp.where(is_sample, n2s_ref[...], n2p_ref[...])
    pos0, pos1 = _slot_positions(rt)
    used = _tile_rows(tab_ref, t, n_tiles)

    def group(r0, nrows):
        m0, m1 = _one_hot_rows(r0, nrows, pos0, pos1)
        q = (jnp.where(m0, 1.0, 0.0) + jnp.where(m1, 1.0, 0.0)).astype(BF16)
        buf[slot, pl.ds(r0, nrows), :] = jnp.dot(q, n2, preferred_element_type=F32).astype(BF16)

    group(0, 2 * MOE_TD)
    for r0 in range(2 * MOE_TD, MOE_BR, MOE_BLK):
        @pl.when(used > r0)
        def _():
            group(r0, MOE_BLK)

    _segment_copies(tab_ref, t, n_tiles, buf, xs_ref, sems, slot, to_hbm=True, wait=False)

    @pl.when(t == n_tiles - 1)
    def _():
        if n_tiles >= 2:
            _segment_copies(tab_ref, t - 1, n_tiles, buf, xs_ref, sems, 1 - slot, to_hbm=True, wait=True)
        _segment_copies(tab_ref, t, n_tiles, buf, xs_ref, sems, slot, to_hbm=True, wait=True)
        lax.fori_loop(first_free, n_gtiles, fill_wait, 0)


def _dispatch_call(tab, rt_p, n2_p, rt_s, n2_s, r_max):
    n_ptiles = n2_p.shape[0] // MOE_TD
    n_tiles = n_ptiles + n2_s.shape[0] // MOE_TD
    last_p = n_ptiles - 1
    return pl.pallas_call(
        functools.partial(_dispatch_kernel, n_tiles=n_tiles, n_ptiles=n_ptiles,
                          n_gtiles=r_max // MOE_TG),
        grid_spec=pltpu.PrefetchScalarGridSpec(
            num_scalar_prefetch=1, grid=(n_tiles,),
            in_specs=[pl.BlockSpec((8, MOE_TD), lambda t, tab: (0, jnp.minimum(t, last_p))),
                      pl.BlockSpec((MOE_TD, D), lambda t, tab: (jnp.minimum(t, last_p), 0)),
                      pl.BlockSpec((8, MOE_TD), lambda t, tab: (0, 0)),
                      pl.BlockSpec((MOE_TD, D), lambda t, tab: (0, 0))],
            out_specs=pl.BlockSpec(memory_space=pl.ANY),
            scratch_shapes=[pltpu.VMEM((2, MOE_BR, D), BF16),
                            pltpu.VMEM((MOE_TG, D), BF16),
                            pltpu.SemaphoreType.DMA((2, NE)),
                            pltpu.SemaphoreType.DMA(())]),
        out_shape=jax.ShapeDtypeStruct((r_max, D), BF16),
        compiler_params=pltpu.CompilerParams(dimension_semantics=("arbitrary",),
                                             vmem_limit_bytes=VMEM_LIMIT),
        name="moe_dispatch",
    )(tab, rt_p, n2_p, rt_s, n2_s)


MOE_CK = 3


def _experts_kernel(ctab_ref, xs_ref, w1_ref, w3_ref, w2_ref, ys_ref,
                    w1b, w3b, w2b, xbuf, ybuf, zbuf, in_sem, out_sem, zsem, *, n_gtiles, nc_max):
    e = pl.program_id(0)
    c0 = ctab_ref[e]
    c1 = ctab_ref[e + 1]
    n_chunks = ctab_ref[NE]
    first_free = ctab_ref[NE + 1 + 2 * nc_max]

    def fill_copy(j):
        d = pl.multiple_of(j * MOE_TG, MOE_TG)
        return pltpu.make_async_copy(zbuf, ys_ref.at[pl.ds(d, MOE_TG)], zsem)

    def fill_start(j, carry):
        fill_copy(j).start()
        return carry

    def fill_wait(j, carry):
        fill_copy(j).wait()
        return carry

    def span(c):
        r = pl.multiple_of(ctab_ref[NE + 1 + c], MOE_TG)
        n = pl.multiple_of(ctab_ref[NE + 1 + nc_max + c] * MOE_TG, MOE_TG)
        return r, n

    def in_copy(c, slot):
        r, n = span(c)
        return pltpu.make_async_copy(xs_ref.at[pl.ds(r, n)], xbuf.at[slot, pl.ds(0, n)], in_sem.at[slot])

    def out_copy(c, slot):
        r, n = span(c)
        return pltpu.make_async_copy(ybuf.at[slot, pl.ds(0, n)], ys_ref.at[pl.ds(r, n)], out_sem.at[slot])

    @pl.when(e == 0)
    def _():
        zbuf[...] = jnp.zeros_like(zbuf)
        lax.fori_loop(first_free, n_gtiles, fill_start, 0)

        @pl.when(n_chunks > 0)
        def _():
            in_copy(0, 0).start()

    def compute(slot, rows):
        x = xbuf[slot, pl.ds(0, rows), :]
        a = jnp.dot(x, w1b[...], preferred_element_type=F32)
        b = jnp.dot(x, w3b[...], preferred_element_type=F32)
        hid = a * _sigmoid(a) * b
        y = jnp.dot(hid.astype(BF16), w2b[...], preferred_element_type=F32)
        ybuf[slot, pl.ds(0, rows), :] = y.astype(BF16)

    @pl.when(c1 > c0)
    def _():
        w1b[...] = w1_ref[...].astype(BF16)
        w3b[...] = w3_ref[...].astype(BF16)
        w2b[...] = w2_ref[...].astype(BF16)

        def chunk(c, carry):
            slot = lax.rem(c, 2)

            @pl.when(c + 1 < n_chunks)
            def _():
                in_copy(c + 1, 1 - slot).start()

            in_copy(c, slot).wait()

            @pl.when(c >= 2)
            def _():
                out_copy(c - 2, slot).wait()

            k = ctab_ref[NE + 1 + nc_max + c]
            for kk in range(1, MOE_CK + 1):
                @pl.when(k == kk)
                def _():
                    compute(slot, kk * MOE_TG)

            out_copy(c, slot).start()
            return carry

        lax.fori_loop(c0, c1, chunk, 0)

    @pl.when(e == NE - 1)
    def _():
        @pl.when(n_chunks >= 2)
        def _():
            out_copy(n_chunks - 2, lax.rem(n_chunks, 2)).wait()

        @pl.when(n_chunks >= 1)
        def _():
            out_copy(n_chunks - 1, lax.rem(n_chunks - 1, 2)).wait()

        lax.fori_loop(first_free, n_gtiles, fill_wait, 0)


def _experts_call(ctab, xs, w1, w3, w2, nc_max):
    r_max = xs.shape[0]
    w_map = lambda e, ctab: (e, 0, 0)
    ring = pltpu.VMEM((2, MOE_CK * MOE_TG, D), BF16)
    return pl.pallas_call(
        functools.partial(_experts_kernel, n_gtiles=r_max // MOE_TG, nc_max=nc_max),
        grid_spec=pltpu.PrefetchScalarGridSpec(
            num_scalar_prefetch=1, grid=(NE,),
            in_specs=[pl.BlockSpec(memory_space=pl.ANY),
                      pl.BlockSpec((None, D, DE), w_map),
                      pl.BlockSpec((None, D, DE), w_map),
                      pl.BlockSpec((None, DE, D), w_map)],
            out_specs=pl.BlockSpec(memory_space=pl.ANY),
            scratch_shapes=[pltpu.VMEM((D, DE), BF16), pltpu.VMEM((D, DE), BF16),
                            pltpu.VMEM((DE, D), BF16), ring, ring,
                            pltpu.VMEM((MOE_TG, D), BF16),
                            pltpu.SemaphoreType.DMA((2,)), pltpu.SemaphoreType.DMA((2,)),
                            pltpu.SemaphoreType.DMA(())]),
        out_shape=jax.ShapeDtypeStruct((r_max, D), BF16),
        compiler_params=pltpu.CompilerParams(dimension_semantics=("arbitrary",),
                                             vmem_limit_bytes=VMEM_LIMIT),
        name="moe_experts",
    )(ctab, xs, w1, w3, w2)


def _combine_kernel(tab_ref, rt_ref, x1_ref, gt2_ref, gf_ref, ys_ref, y_ref, buf, acc, sems,
                    *, n_tiles, t_off):
    i = pl.program_id(0)
    t = i + t_off
    slot = lax.rem(i, 2)

    @pl.when(i == 0)
    def _():
        buf[...] = jnp.zeros_like(buf)
        _segment_copies(tab_ref, t, n_tiles, buf, ys_ref, sems, slot, to_hbm=False, wait=False)

    @pl.when(i + 1 < pl.num_programs(0))
    def _():
        _segment_copies(tab_ref, t + 1, n_tiles, buf, ys_ref, sems, 1 - slot, to_hbm=False, wait=False)

    _segment_copies(tab_ref, t, n_tiles, buf, ys_ref, sems, slot, to_hbm=False, wait=True)

    rt = rt_ref[...]
    pos0, pos1 = _slot_positions(rt)
    used = _tile_rows(tab_ref, t, n_tiles)

    def ungroup(r0, nrows):
        m0, m1 = _one_hot_rows(r0, nrows, pos0, pos1)
        q = (jnp.where(m0, 1.0, 0.0) + jnp.where(m1, 1.0, 0.0)).astype(BF16)
        gw = jnp.sum(jnp.where(m0, rt[2:3, :], 0.0) + jnp.where(m1, rt[3:4, :], 0.0),
                     axis=1, keepdims=True)
        yv = (buf[slot, pl.ds(r0, nrows), :].astype(F32) * gw).astype(BF16)
        return lax.dot_general(q, yv, (((0,), (0,)), ((), ())), preferred_element_type=F32)

    acc[...] = ungroup(0, 2 * MOE_TD)
    for r0 in range(2 * MOE_TD, MOE_BR, MOE_BLK):
        @pl.when(used > r0)
        def _():
            acc[...] += ungroup(r0, MOE_BLK)

    xo = x1_ref[...] + gt2_ref[...] * acc[...]
    y_ref[...] = _rms(xo) * gf_ref[...]


def _combine_call(tab, rt, x1, gt2, gt2_spec, g_final, ys, n_tiles, t_off):
    n_out_tiles = x1.shape[0] // MOE_TD
    return pl.pallas_call(
        functools.partial(_combine_kernel, n_tiles=n_tiles, t_off=t_off),
        grid_spec=pltpu.PrefetchScalarGridSpec(
            num_scalar_prefetch=1, grid=(n_out_tiles,),
            in_specs=[pl.BlockSpec((8, MOE_TD), lambda t, tab: (0, t)),
                      pl.BlockSpec((MOE_TD, D), lambda t, tab: (t, 0)),
                      gt2_spec,
                      pl.BlockSpec((1, D), lambda t, tab: (0, 0)),
                      pl.BlockSpec(memory_space=pl.ANY)],
            out_specs=pl.BlockSpec((MOE_TD, D), lambda t, tab: (t, 0)),
            scratch_shapes=[pltpu.VMEM((2, MOE_BR, D), BF16),
                            pltpu.VMEM((MOE_TD, D), F32),
                            pltpu.SemaphoreType.DMA((2, NE))]),
        out_shape=jax.ShapeDtypeStruct((n_out_tiles * MOE_TD, D), F32),
        compiler_params=pltpu.CompilerParams(dimension_semantics=("arbitrary",),
                                             vmem_limit_bytes=VMEM_LIMIT),
        name="moe_combine",
    )(tab, rt, x1, gt2, g_final.reshape(1, D), ys)


def _moe_plan(cnt, nc_max):
    cnt8 = (cnt + ROW_ALIGN - 1) // ROW_ALIGN * ROW_ALIGN
    seg_rows = cnt8.sum(axis=0)
    seg_pad = (seg_rows + MOE_TG - 1) // MOE_TG * MOE_TG
    seg_start = jnp.cumsum(seg_pad) - seg_pad
    dst = seg_start[None, :] + jnp.cumsum(cnt8, axis=0) - cnt8
    boff = jnp.cumsum(cnt8, axis=1) - cnt8
    tile_end = jnp.cumsum(seg_pad // MOE_TG)
    n_active = tile_end[-1:].astype(jnp.int32)
    tab = jnp.concatenate([cnt8.ravel(), boff.ravel(), dst.ravel(),
                           seg_start + seg_rows, seg_pad - seg_rows, n_active,
                           cnt8.sum(axis=1)]).astype(jnp.int32)
    nt = seg_pad // MOE_TG
    nfull = nt // MOE_CK
    rem = nt % MOE_CK
    nch = nfull + (rem > 0)
    cend = jnp.cumsum(nch)
    cstart = cend - nch
    c = jnp.arange(nc_max, dtype=jnp.int32)
    ce = jnp.minimum(jnp.sum(c[:, None] >= cend[None, :], axis=1), NE - 1)
    local = c - cstart[ce]
    valid = c < cend[-1]
    ck = jnp.where(valid, jnp.where(local < nfull[ce], MOE_CK, rem[ce]), 0)
    crow = jnp.where(valid, seg_start[ce] + local * (MOE_CK * MOE_TG), 0)
    ctab = jnp.concatenate([cstart, cend[-1:], crow, ck, n_active]).astype(jnp.int32)
    return tab, ctab


def kernel(x_prompt, x_sample, c_prompt, c_sample, state_ssm_re, state_ssm_im, cache_conv, w_ada, b_ada, g_norm_mix, w_in, ssm_a_re, ssm_a_im, ssm_log_dt, ssm_b_re, ssm_b_im, ssm_c_re, ssm_c_im, ssm_d, w_ssm_glu, b_ssm_glu, w_dw, b_dw, ln_conv_g, ln_conv_b, g_out_ssm, g_out_conv, w_out, g_norm_ffn, w_router_grp, b_router_grp, w_router_exp, b_router_exp, w_exp_gate, w_exp_up, w_exp_down, g_final):
    depth = w_ada.shape[0]
    assert depth == 1
    bsz, seq, _ = x_prompt.shape
    nb, nt, _ = x_sample.shape

    c_all = jnp.concatenate([c_prompt, c_sample], axis=0)
    mod = _mod_call(c_all, w_ada[0], b_ada[0])
    mod_p = mod[:bsz].reshape(bsz, 6, D)
    mod_s = mod[bsz:]

    ab_re, ab_im, bb_re, bb_im, c_im_neg = _ssm_prep_call(
        ssm_a_re[0], ssm_a_im[0], ssm_log_dt[0], ssm_b_re[0], ssm_b_im[0], ssm_c_im[0])
    wb, wc = _block_diag_weights(bb_re, bb_im, ssm_c_re[0], c_im_neg)
    a_tok = jnp.stack([ab_re.reshape(NCHUNK, LANES), ab_im.reshape(NCHUNK, LANES)])
    a_row = jnp.stack([ab_re.reshape(NSTATE), ab_im.reshape(NSTATE)])

    w_r = jnp.concatenate([w_router_exp[0].reshape(D, NE), w_router_grp[0],
                           jnp.zeros((D, LANES - NE - NG), F32)], axis=1)
    w_r_hi = w_r.astype(BF16)
    w_r = jnp.concatenate([w_r_hi, (w_r - w_r_hi.astype(F32)).astype(BF16)], axis=1)
    b_r = jnp.concatenate([b_router_exp[0].reshape(NE), b_router_grp[0],
                           jnp.zeros((LANES - NE - NG,), F32)]).reshape(1, LANES)
    w_dw_p = jnp.concatenate([w_dw[0], jnp.zeros((1, CONV_CH), F32)], axis=0)

    row = lambda v: v.reshape(1, -1)
    common_a = (row(g_norm_mix[0]), w_in[0].astype(BF16), wb)
    common_b = (wc, row(ssm_d[0].reshape(SSM_W)), w_ssm_glu[0].astype(BF16), row(b_ssm_glu[0]),
                w_dw_p, row(b_dw[0]), row(ln_conv_g[0]), row(ln_conv_b[0]),
                row(g_out_ssm[0]), row(g_out_conv[0]), w_out[0].astype(BF16),
                row(g_norm_ffn[0]), w_r, b_r)

    n_p = bsz * seq
    n_s = nb * nt
    n_all = n_p + n_s
    assert n_s == MOE_TD and seq % MOE_TD == 0 and MOE_TD % PROMPT_TC == 0
    wts_p = common_a + (a_tok,) + common_b
    x1_p, n2_p, rt_p, cnt_p, hr_p, hi_p, cache_p = _prompt_mixer_call(x_prompt, mod_p, wts_p, PROMPT_TC)

    x_tm = jnp.transpose(x_sample, (1, 0, 2)).reshape(nt * nb, D)
    cache_tm = jnp.transpose(cache_conv[0], (1, 0, 2))
    wts_s = common_a + (a_row,) + common_b
    x1_s, n2_s, rt_s, cnt_s, hr_s, hi_s, glu_s = _sample_mixer_call(
        x_tm, mod_s, state_ssm_re[0].reshape(nb, NSTATE), state_ssm_im[0].reshape(nb, NSTATE),
        cache_tm, wts_s, nb, nt)

    n_ptiles = n_p // MOE_TD
    n_tiles = n_all // MOE_TD
    r_max = -(-(2 * n_all + n_tiles * NE * (ROW_ALIGN - 1) + NE * (MOE_TG - ROW_ALIGN)) // MOE_TG) * MOE_TG
    cnt = jnp.concatenate([cnt_p.reshape(n_ptiles, MOE_TD // PROMPT_TC, LANES).sum(axis=1), cnt_s])
    nc_max = r_max // MOE_TG // MOE_CK + NE
    tab, ctab = _moe_plan(cnt[:, :NE].astype(jnp.int32), nc_max)
    xs = _dispatch_call(tab, rt_p, n2_p, rt_s, n2_s, r_max)
    ys = _experts_call(ctab, xs, w_exp_gate[0], w_exp_up[0], w_exp_down[0], nc_max)
    tiles_per_b = seq // MOE_TD
    gt2_p = mod_p[:, 5:6, :]
    y_p = _combine_call(tab, rt_p, x1_p, gt2_p,
                        pl.BlockSpec((None, 1, D), lambda t, tab: (t // tiles_per_b, 0, 0)),
                        g_final, ys, n_tiles, 0)
    gt2_s = jnp.tile(mod_s[:, 5 * D:], (nt, 1))
    y_s = _combine_call(tab, rt_s, x1_s, gt2_s,
                        pl.BlockSpec((n_s, D), lambda t, tab: (0, 0)),
                        g_final, ys, n_tiles, n_ptiles)

    y_prompt = y_p.reshape(bsz, seq, D)
    y_sample = jnp.transpose(y_s.reshape(nt, nb, D), (1, 0, 2))
    new_cache_s = jnp.concatenate(
        [cache_conv[0][:, nt:, :], jnp.transpose(glu_s.reshape(nt, nb, CONV_CH), (1, 0, 2))], axis=1)
    return (y_prompt, y_sample,
            hr_p.reshape(1, bsz, G, P), hi_p.reshape(1, bsz, G, P), cache_p[None],
            hr_s.reshape(1, nb, G, P), hi_s.reshape(1, nb, G, P), new_cache_s[None])
```

```python
import functools

import jax
import jax.numpy as jnp
import numpy as np
from jax import lax
from jax.experimental import pallas as pl
from jax.experimental.pallas import tpu as pltpu

F32 = jnp.float32
BF16 = jnp.bfloat16

D = 1024
SSM_W = 512
CONV_CH = 512
G = 32
H = 16
P = 64
KW = 31
CB = KW - 1
NE = 32
NG = 4
EPG = 8
DE = 512
EPS = 1e-6
LANES = 128
NSTATE = G * P
NCHUNK = NSTATE // LANES

PROMPT_TC = 512
SCAN_GROUP = 8
VMEM_LIMIT = 56 * 1024 * 1024


def _rms(x):
    return x * lax.rsqrt(jnp.mean(x * x, axis=-1, keepdims=True) + EPS)


def _sigmoid(x):
    return 0.5 * jnp.tanh(0.5 * x) + 0.5


def _gelu_tanh(y):
    c = np.sqrt(2.0 / np.pi).astype(np.float32)
    return y * (0.5 * (1.0 + jnp.tanh(c * (y + 0.044715 * (y * y * y)))))


def _bdot(a, b):
    return jnp.dot(a.astype(BF16), b, preferred_element_type=F32)


def _mod_kernel(c_ref, w_ref, b_ref, o_ref):
    c = c_ref[...]
    s = c * _sigmoid(c)
    n = s.shape[0]
    s_hi = s.astype(BF16)
    lhs = jnp.concatenate([s_hi, (s - s_hi.astype(F32)).astype(BF16)], axis=0)
    w = w_ref[...]
    w_hi = w.astype(BF16)
    w_lo = (w - w_hi.astype(F32)).astype(BF16)
    p_hi = jnp.dot(lhs, w_hi, preferred_element_type=F32)
    p_lo = jnp.dot(lhs, w_lo, preferred_element_type=F32)
    o_ref[...] = (p_hi[:n] + p_lo[:n]) + (p_hi[n:] + p_lo[n:]) + b_ref[...]


def _mod_call(c_all, w_ada, b_ada):
    n = c_all.shape[0]
    tn = 512
    return pl.pallas_call(
        _mod_kernel,
        grid=(6 * D // tn,),
        in_specs=[pl.BlockSpec((n, D), lambda j: (0, 0)),
                  pl.BlockSpec((D, tn), lambda j: (0, j)),
                  pl.BlockSpec((1, tn), lambda j: (0, j))],
        out_specs=pl.BlockSpec((n, tn), lambda j: (0, j)),
        out_shape=jax.ShapeDtypeStruct((n, 6 * D), F32),
        compiler_params=pltpu.CompilerParams(dimension_semantics=("arbitrary",)),
        name="mod",
    )(c_all, w_ada, b_ada.reshape(1, 6 * D))


def _ssm_prep_kernel(a_re, a_im, log_dt, b_re, b_im, c_im,
                     ab_re_o, ab_im_o, bb_re_o, bb_im_o, cneg_o):
    lam_re = jnp.minimum(a_re[...], -1e-4)
    lam_im = a_im[...]
    dt = jnp.exp(log_dt[...])
    mag = jnp.exp(lam_re * dt)
    ab_re = mag * jnp.cos(lam_im * dt)
    ab_im = mag * jnp.sin(lam_im * dt)
    den = lam_re * lam_re + lam_im * lam_im
    num_re = ab_re - 1.0
    coef_re = (num_re * lam_re + ab_im * lam_im) / den
    coef_im = (ab_im * lam_re - num_re * lam_im) / den
    ab_re_o[...] = ab_re
    ab_im_o[...] = ab_im
    br = b_re[...]
    bi = b_im[...]
    bb_re_o[...] = coef_re * br - coef_im * bi
    bb_im_o[...] = coef_re * bi + coef_im * br
    cneg_o[...] = -c_im[...]


def _ssm_prep_call(a_re, a_im, log_dt, b_re, b_im, c_im):
    flat = lambda v: v.reshape(1, NSTATE)
    b_hs = lambda v: jnp.transpose(v, (2, 0, 1)).reshape(H, NSTATE)
    dt_row = jnp.broadcast_to(log_dt[:, None], (G, P)).reshape(1, NSTATE)
    ab_re, ab_im, bb_re, bb_im, cneg = pl.pallas_call(
        _ssm_prep_kernel,
        out_shape=(jax.ShapeDtypeStruct((1, NSTATE), F32), jax.ShapeDtypeStruct((1, NSTATE), F32),
                   jax.ShapeDtypeStruct((H, NSTATE), F32), jax.ShapeDtypeStruct((H, NSTATE), F32),
                   jax.ShapeDtypeStruct((G * H, P), F32)),
        name="ssm_prep",
    )(flat(a_re), flat(a_im), dt_row, b_hs(b_re), b_hs(b_im), c_im.reshape(G * H, P))
    ghp = lambda v: jnp.transpose(v.reshape(H, G, P), (1, 0, 2))
    return (ab_re.reshape(G, P), ab_im.reshape(G, P), ghp(bb_re), ghp(bb_im), cneg.reshape(G, H, P))


def _block_diag_weights(bb_re, bb_im, c_re, c_im_neg):
    eye8 = jnp.eye(8, dtype=F32)
    eye4 = jnp.eye(4, dtype=F32)
    eye2 = jnp.eye(2, dtype=F32)

    def wb_part(bb):
        x = bb.reshape(4, 8, H, P)
        return jnp.einsum('qghp,gk->qghkp', x, eye8).reshape(4, 8 * H, 8 * P)

    wb = jnp.concatenate([wb_part(bb_re), wb_part(bb_im)], axis=-1).astype(BF16)

    def wc_part(c):
        x = c.reshape(4, 4, 2, H, P)
        y = jnp.einsum('qjghp,jk,gl->qjgpklh', x, eye4, eye2)
        return y.reshape(NCHUNK, 2 * P, 4 * 2 * H)

    wc = jnp.concatenate([wc_part(c_re), wc_part(c_im_neg)], axis=1).astype(BF16)
    return wb, wc


def _front(x, sc1, sh1, g_mix, w_in_ref):
    n = _rms(x) * g_mix * (1.0 + sc1) + sh1
    proj = _bdot(n, w_in_ref[...])
    u = proj[:, :SSM_W]
    glu = proj[:, SSM_W:SSM_W + CONV_CH] * _sigmoid(proj[:, SSM_W + CONV_CH:])
    return u, glu


def _ssm_out(y_lin, u, d_skip, w_glu_ref, b_glu, g_out_ssm):
    y = _gelu_tanh(y_lin + d_skip * u)
    ys = y * _sigmoid(_bdot(y, w_glu_ref[...]) + b_glu)
    return _rms(ys) * g_out_ssm


def _conv_out(conv, b_dw, ln_g, ln_b, g_out_conv):
    c = conv + b_dw
    mu = jnp.mean(c, axis=-1, keepdims=True)
    cc = c - mu
    var = jnp.mean(cc * cc, axis=-1, keepdims=True)
    ln = cc * lax.rsqrt(var + EPS) * ln_g + ln_b
    yc = ln * _sigmoid(ln)
    return _rms(yc) * g_out_conv


def _route(n2, w_r_ref, b_r):
    rows = n2.shape[0]
    n_hi = n2.astype(BF16)
    n_lo = (n2 - n_hi.astype(F32)).astype(BF16)
    parts = jnp.dot(jnp.concatenate([n_hi, n_lo], axis=0), w_r_ref[...], preferred_element_type=F32)
    lg = (parts[:rows, :LANES] + parts[:rows, LANES:]) + (parts[rows:, :LANES] + parts[rows:, LANES:]) + b_r
    lane = lax.broadcasted_iota(jnp.int32, (rows, LANES), 1).astype(F32)
    ninf = -jnp.inf
    big = 1e9
    gmask = jnp.logical_and(lane >= NE, lane < NE + NG)
    gl = jnp.where(gmask, lg, ninf)
    gmax = jnp.max(gl, axis=-1, keepdims=True)
    gsum = jnp.sum(jnp.where(gmask, jnp.exp(gl - gmax), 0.0), axis=-1, keepdims=True)
    p_top = 1.0 / gsum
    gi = jnp.min(jnp.where(gl == gmax, lane, big), axis=-1, keepdims=True) - NE
    lo = gi * EPG
    emask = jnp.logical_and(lane >= lo, lane < lo + EPG)
    el = jnp.where(emask, lg, ninf)
    m1 = jnp.max(el, axis=-1, keepdims=True)
    i1 = jnp.min(jnp.where(el == m1, lane, big), axis=-1, keepdims=True)
    el2 = jnp.where(lane == i1, ninf, el)
    m2 = jnp.max(el2, axis=-1, keepdims=True)
    i2 = jnp.min(jnp.where(el2 == m2, lane, big), axis=-1, keepdims=True)
    e2 = jnp.exp(m2 - m1)
    den = 1.0 + e2
    w1 = p_top / den
    w2 = p_top * e2 / den
    cnt = jnp.sum(jnp.where(lane == i1, 1.0, 0.0) + jnp.where(lane == i2, 1.0, 0.0),
                  axis=0, keepdims=True)
    cols = (jnp.where(lane == 0.0, i1, 0.0) + jnp.where(lane == 1.0, i2, 0.0)
            + jnp.where(lane == 2.0, w1, 0.0) + jnp.where(lane == 3.0, w2, 0.0))
    return cols.T[0:8, :], cnt


def _tail(x, ns, nc, gt1, sc2, sh2, w_out_ref, g_ffn, w_r_ref, b_r):
    merged = _bdot(jnp.concatenate([ns, nc], axis=-1), w_out_ref[...])
    x1 = x + gt1 * merged
    n2 = _rms(x1) * g_ffn * (1.0 + sc2) + sh2
    rt, cnt = _route(n2, w_r_ref, b_r)
    return x1, n2, rt, cnt


def _prompt_mixer_kernel(x_ref, mod_ref, g_mix_ref, w_in_ref, wb_ref, a_ref, wc_ref, dsk_ref,
                         w_glu_ref, b_glu_ref, w_dw_ref, b_dw_ref, ln_g_ref, ln_b_ref,
                         g_os_ref, g_oc_ref, w_out_ref, g_ffn_ref, w_r_ref, b_r_ref,
                         x1_ref, n2_ref, rt_ref, cnt_ref, hr_ref, hi_ref, cache_ref,
                         sre, sim, hst, ebuf, eshift, ncbuf, *, tc, pt):
    c = pl.program_id(1)
    nc_chunks = pl.num_programs(1)

    @pl.when(c == 0)
    def _():
        hst[...] = jnp.zeros_like(hst)
        ebuf[pl.ds(0, 32), :] = jnp.zeros((32, CONV_CH), F32)

    x = x_ref[...]
    mod = mod_ref[...]
    sh1, sc1, gt1 = mod[0:1], mod[1:2], mod[2:3]
    sh2, sc2 = mod[3:4], mod[4:5]

    u, glu = _front(x, sc1, sh1, g_mix_ref[...], w_in_ref)
    ebuf[pl.ds(32, tc), :] = glu

    for s in range(1, 8):
        eshift[s - 1, pl.ds(0, tc + 24), :] = ebuf[pl.ds(s, tc + 24), :]
    rb = 64
    convs = []
    for r0 in range(0, tc, rb):
        acc = None
        for k in range(KW):
            a8, s = (k + 2) // 8 * 8, (k + 2) % 8
            win = ebuf[pl.ds(r0 + a8, rb), :] if s == 0 else eshift[s - 1, pl.ds(r0 + a8, rb), :]
            term = w_dw_ref[k:k + 1, :] * win
            acc = term if acc is None else acc + term
        convs.append(acc)
    conv = jnp.concatenate(convs, axis=0)
    ncbuf[...] = _conv_out(conv, b_dw_ref[...], ln_g_ref[...], ln_b_ref[...], g_oc_ref[...])

    @pl.when(c == nc_chunks - 1)
    def _():
        cache_ref[...] = ebuf[pl.ds(tc + 2, CB), :]

    ebuf[pl.ds(0, 32), :] = ebuf[pl.ds(tc, 32), :]

    ub = u.astype(BF16)
    for q in range(4):
        r = jnp.dot(ub[:, q * LANES:(q + 1) * LANES], wb_ref[q], preferred_element_type=F32)
        for k in range(4):
            j = 4 * q + k
            sre[pl.ds(j * pt, tc), :] = r[:, k * LANES:(k + 1) * LANES]
            sim[pl.ds(j * pt, tc), :] = r[:, SSM_W + k * LANES:SSM_W + (k + 1) * LANES]

    ar0, ar1 = a_ref[0, 0:8, :], a_ref[0, 8:16, :]
    ai0, ai1 = a_ref[1, 0:8, :], a_ref[1, 8:16, :]

    def step(g, carry):
        hr0, hr1, hi0, hi1 = carry
        t0 = g * SCAN_GROUP
        idx = [(pl.ds(t0 + u, 8, stride=pt), pl.ds(t0 + u + 8 * pt, 8, stride=pt))
               for u in range(SCAN_GROUP)]
        bu = [(sre[i0, :], sre[i1, :], sim[i0, :], sim[i1, :]) for (i0, i1) in idx]
        hs = []
        for br0, br1, bi0, bi1 in bu:
            nr0 = ar0 * hr0 - ai0 * hi0 + br0
            ni0 = ar0 * hi0 + ai0 * hr0 + bi0
            nr1 = ar1 * hr1 - ai1 * hi1 + br1
            ni1 = ar1 * hi1 + ai1 * hr1 + bi1
            hr0, hr1, hi0, hi1 = nr0, nr1, ni0, ni1
            hs.append((nr0, nr1, ni0, ni1))
        for (i0, i1), (nr0, nr1, ni0, ni1) in zip(idx, hs):
            sre[i0, :] = nr0
            sim[i0, :] = ni0
            sre[i1, :] = nr1
            sim[i1, :] = ni1
        return hr0, hr1, hi0, hi1

    init = (hst[0, 0:8, :], hst[0, 8:16, :], hst[1, 0:8, :], hst[1, 8:16, :])
    hr0, hr1, hi0, hi1 = lax.fori_loop(0, tc // SCAN_GROUP, step, init)
    hst[0, 0:8, :] = hr0
    hst[0, 8:16, :] = hr1
    hst[1, 0:8, :] = hi0
    hst[1, 8:16, :] = hi1

    @pl.when(c == nc_chunks - 1)
    def _():
        hr_ref[...] = hst[0]
        hi_ref[...] = hst[1]

    ys = []
    for q in range(4):
        acc = None
        for jj in range(4):
            j = 4 * q + jj
            lhs = jnp.concatenate([sre[pl.ds(j * pt, tc), :], sim[pl.ds(j * pt, tc), :]], axis=-1)
            d = jnp.dot(lhs.astype(BF16), wc_ref[j], preferred_element_type=F32)
            acc = d if acc is None else acc + d
        ys.append(acc)
    y_lin = jnp.concatenate(ys, axis=-1)
    ns = _ssm_out(y_lin, u, dsk_ref[...], w_glu_ref, b_glu_ref[...], g_os_ref[...])
    nc = ncbuf[...]

    x1, n2, rt, cnt = _tail(x, ns, nc, gt1, sc2, sh2, w_out_ref, g_ffn_ref[...], w_r_ref, b_r_ref[...])
    x1_ref[...] = x1
    n2_ref[...] = n2.astype(BF16)
    rt_ref[...] = rt
    cnt_ref[...] = cnt


def _const_spec(shape):
    nd = len(shape)
    return pl.BlockSpec(shape, lambda b, c: (0,) * nd)


def _prompt_mixer_call(x, mod6, wts, tc):
    bsz, t, _ = x.shape
    n_all = bsz * t
    pt = tc + 8
    nc = t // tc
    kern = functools.partial(_prompt_mixer_kernel, tc=tc, pt=pt)
    in_specs = [pl.BlockSpec((None, tc, D), lambda b, c: (b, c, 0)),
                pl.BlockSpec((None, 6, D), lambda b, c: (b, 0, 0))]
    in_specs += [_const_spec(w.shape) for w in wts]
    out_shape = (jax.ShapeDtypeStruct((n_all, D), F32),
                 jax.ShapeDtypeStruct((n_all, D), BF16),
                 jax.ShapeDtypeStruct((8, n_all), F32),
                 jax.ShapeDtypeStruct((n_all // tc, 1, LANES), F32),
                 jax.ShapeDtypeStruct((bsz, NCHUNK, LANES), F32),
                 jax.ShapeDtypeStruct((bsz, NCHUNK, LANES), F32),
                 jax.ShapeDtypeStruct((bsz, CB, CONV_CH), F32))
    out_specs = (pl.BlockSpec((tc, D), lambda b, c: (b * nc + c, 0)),
                 pl.BlockSpec((tc, D), lambda b, c: (b * nc + c, 0)),
                 pl.BlockSpec((8, tc), lambda b, c: (0, b * nc + c)),
                 pl.BlockSpec((None, 1, LANES), lambda b, c: (b * nc + c, 0, 0)),
                 pl.BlockSpec((None, NCHUNK, LANES), lambda b, c: (b, 0, 0)),
                 pl.BlockSpec((None, NCHUNK, LANES), lambda b, c: (b, 0, 0)),
                 pl.BlockSpec((None, CB, CONV_CH), lambda b, c: (b, 0, 0)))
    scratch = [pltpu.VMEM((NCHUNK * pt, LANES), F32),
               pltpu.VMEM((NCHUNK * pt, LANES), F32),
               pltpu.VMEM((2, NCHUNK, LANES), F32),
               pltpu.VMEM((tc + 32, CONV_CH), F32),
               pltpu.VMEM((7, tc + 32, CONV_CH), F32),
               pltpu.VMEM((tc, CONV_CH), F32)]
    return pl.pallas_call(
        kern, grid=(bsz, nc), in_specs=in_specs, out_specs=out_specs, out_shape=out_shape,
        scratch_shapes=scratch,
        compiler_params=pltpu.CompilerParams(dimension_semantics=("arbitrary", "arbitrary"),
                                             vmem_limit_bytes=VMEM_LIMIT),
        name="prompt_mixer",
    )(x, mod6, *wts)


def _sample_mixer_kernel(x_ref, mod_ref, h0r_ref, h0i_ref, cache_ref,
                         g_mix_ref, w_in_ref, wb_ref, a_ref, wc_ref, dsk_ref,
                         w_glu_ref, b_glu_ref, w_dw_ref, b_dw_ref, ln_g_ref, ln_b_ref,
                         g_os_ref, g_oc_ref, w_out_ref, g_ffn_ref, w_r_ref, b_r_ref,
                         x1_ref, n2_ref, rt_ref, cnt_ref, hr_ref, hi_ref, glu_ref,
                         sre, sim, *, nb, nt):
    x = x_ref[...]

    def rows(i):
        m = mod_ref[:, i * D:(i + 1) * D]
        return jnp.concatenate([m] * nt, axis=0)

    sh1, sc1, gt1, sh2, sc2 = rows(0), rows(1), rows(2), rows(3), rows(4)
    u, glu = _front(x, sc1, sh1, g_mix_ref[...], w_in_ref)
    glu_ref[...] = glu

    ub = u.astype(BF16)
    for q in range(4):
        r = jnp.dot(ub[:, q * LANES:(q + 1) * LANES], wb_ref[q], preferred_element_type=F32)
        sre[:, q * SSM_W:(q + 1) * SSM_W] = r[:, :SSM_W]
        sim[:, q * SSM_W:(q + 1) * SSM_W] = r[:, SSM_W:]

    ar = a_ref[0:1, :]
    ai = a_ref[1:2, :]
    hr = h0r_ref[...]
    hi = h0i_ref[...]
    for t in range(nt):
        rs = pl.ds(t * nb, nb)
        nr = ar * hr - ai * hi + sre[rs, :]
        ni = ar * hi + ai * hr + sim[rs, :]
        sre[rs, :] = nr
        sim[rs, :] = ni
        hr, hi = nr, ni
    hr_ref[...] = hr
    hi_ref[...] = hi

    ys = []
    for q in range(4):
        acc = None
        for jj in range(4):
            j = 4 * q + jj
            lhs = jnp.concatenate([sre[:, j * LANES:(j + 1) * LANES],
                                   sim[:, j * LANES:(j + 1) * LANES]], axis=-1)
            d = jnp.dot(lhs.astype(BF16), wc_ref[j], preferred_element_type=F32)
            acc = d if acc is None else acc + d
        ys.append(acc)
    y_lin = jnp.concatenate(ys, axis=-1)
    ns = _ssm_out(y_lin, u, dsk_ref[...], w_glu_ref, b_glu_ref[...], g_os_ref[...])

    def ext(jrow):
        if jrow < CB:
            return cache_ref[jrow]
        return glu[(jrow - CB) * nb:(jrow - CB + 1) * nb, :]

    convs = []
    for t in range(nt):
        acc = None
        for k in range(KW):
            term = w_dw_ref[k:k + 1, :] * ext(t + k)
            acc = term if acc is None else acc + term
        convs.append(acc)
    conv = jnp.concatenate(convs, axis=0)
    nc = _conv_out(conv, b_dw_ref[...], ln_g_ref[...], ln_b_ref[...], g_oc_ref[...])

    x1, n2, rt, cnt = _tail(x, ns, nc, gt1, sc2, sh2, w_out_ref, g_ffn_ref[...], w_r_ref, b_r_ref[...])
    x1_ref[...] = x1
    n2_ref[...] = n2.astype(BF16)
    rt_ref[...] = rt
    cnt_ref[...] = cnt


def _sample_mixer_call(x_tm, mod_s, h0r, h0i, cache_tm, wts, nb, nt):
    n = nb * nt
    kern = functools.partial(_sample_mixer_kernel, nb=nb, nt=nt)
    out_shape = (jax.ShapeDtypeStruct((n, D), F32),
                 jax.ShapeDtypeStruct((n, D), BF16),
                 jax.ShapeDtypeStruct((8, n), F32),
                 jax.ShapeDtypeStruct((1, LANES), F32),
                 jax.ShapeDtypeStruct((nb, NSTATE), F32),
                 jax.ShapeDtypeStruct((nb, NSTATE), F32),
                 jax.ShapeDtypeStruct((n, CONV_CH), F32))
    scratch = [pltpu.VMEM((n, NSTATE), F32), pltpu.VMEM((n, NSTATE), F32)]
    return pl.pallas_call(
        kern, out_shape=out_shape, scratch_shapes=scratch,
        compiler_params=pltpu.CompilerParams(vmem_limit_bytes=VMEM_LIMIT),
        name="sample_mixer",
    )(x_tm, mod_s, h0r, h0i, cache_tm, *wts)


ROW_ALIGN = 16
MOE_TD = 512
MOE_BR = MOE_TD * 2 + NE * ROW_ALIGN
MOE_TG = 512


def _slot_positions(rt):
    t = rt.shape[1]
    e0 = rt[0:1, :]
    e1 = rt[1:2, :]
    sub = lax.broadcasted_iota(jnp.int32, (LANES, t), 0).astype(F32)
    a0 = jnp.where(sub == e0, 1.0, 0.0)
    a1 = jnp.where(sub == e1, 1.0, 0.0)
    at = a0 + a1
    r = lax.broadcasted_iota(jnp.int32, (t, t), 0)
    c = lax.broadcasted_iota(jnp.int32, (t, t), 1)
    before = jnp.where(r < c, 1.0, 0.0).astype(BF16)
    rank = jnp.dot(at.astype(BF16), before, preferred_element_type=F32)
    cnt = jnp.sum(at, axis=1, keepdims=True)
    cnt_al = jnp.ceil(cnt * (1.0 / ROW_ALIGN)) * float(ROW_ALIGN)
    er = lax.broadcasted_iota(jnp.int32, (LANES, LANES), 0)
    ec = lax.broadcasted_iota(jnp.int32, (LANES, LANES), 1)
    lower = jnp.where(ec < er, 1.0, 0.0)
    base = jnp.dot(lower, jnp.broadcast_to(cnt_al, (LANES, LANES)), preferred_element_type=F32,
                   precision=lax.Precision.HIGHEST)[:, 0:1]
    slot = rank + base
    pos0 = jnp.sum(a0 * slot, axis=0, keepdims=True)
    pos1 = jnp.sum(a1 * slot, axis=0, keepdims=True)
    return pos0, pos1


def _segment_copies(tab_ref, t, n_tiles, buf, hbm, sems, slot, to_hbm, wait):
    for e in range(NE):
        n = pl.multiple_of(tab_ref[t * NE + e], ROW_ALIGN)
        b = pl.multiple_of(tab_ref[(n_tiles + t) * NE + e], ROW_ALIGN)
        d = pl.multiple_of(tab_ref[(2 * n_tiles + t) * NE + e], ROW_ALIGN)
        vm = buf.at[slot, pl.ds(b, n)]
        hb = hbm.at[pl.ds(d, n)]
        cp = pltpu.make_async_copy(vm, hb, sems.at[slot, e]) if to_hbm else \
            pltpu.make_async_copy(hb, vm, sems.at[slot, e])

        @pl.when(n > 0)
        def _():
            if wait:
                cp.wait()
            else:
                cp.start()


def _tile_rows(tab_ref, t, n_tiles):
    return tab_ref[3 * n_tiles * NE + 2 * NE + 1 + t]


def _one_hot_rows(r0, nrows, pos0, pos1):
    row = (lax.broadcasted_iota(jnp.int32, (nrows, MOE_TD), 0) + r0).astype(F32)
    return row == pos0, row == pos1


MOE_BLK = 256


def _dispatch_kernel(tab_ref, rtp_ref, n2p_ref, rts_ref, n2s_ref, xs_ref, buf, zbuf, sems, zsem,
                     *, n_tiles, n_ptiles, n_gtiles):
    t = pl.program_id(0)
    slot = lax.rem(t, 2)
    first_free = tab_ref[3 * n_tiles * NE + 2 * NE]

    def fill_copy(j):
        d = pl.multiple_of(j * MOE_TG, MOE_TG)
        return pltpu.make_async_copy(zbuf, xs_ref.at[pl.ds(d, MOE_TG)], zsem)

    def fill_start(j, carry):
        fill_copy(j).start()
        return carry

    def fill_wait(j, carry):
        fill_copy(j).wait()
        return carry

    @pl.when(t == 0)
    def _():
        zbuf[...] = jnp.zeros_like(zbuf)
        for phase in range(2):
            for e in range(NE):
                d = pl.multiple_of(tab_ref[3 * n_tiles * NE + e], ROW_ALIGN)
                n = pl.multiple_of(tab_ref[3 * n_tiles * NE + NE + e], ROW_ALIGN)
                cp = pltpu.make_async_copy(zbuf.at[pl.ds(0, n)], xs_ref.at[pl.ds(d, n)], sems.at[1, e])

                @pl.when(n > 0)
                def _():
                    if phase == 0:
                        cp.start()
                    else:
                        cp.wait()

        lax.fori_loop(first_free, n_gtiles, fill_start, 0)

    @pl.when(t >= 2)
    def _():
        _segment_copies(tab_ref, t - 2, n_tiles, buf, xs_ref, sems, slot, to_hbm=True, wait=True)

    is_sample = t >= n_ptiles
    rt = jnp.where(is_sample, rts_ref[...], rtp_ref[...])
    n2 = jnp.where(is_sample, n2s_ref[...], n2p_ref[...])
    pos0, pos1 = _slot_positions(rt)
    used = _tile_rows(tab_ref, t, n_tiles)

    def group(r0, nrows):
        m0, m1 = _one_hot_rows(r0, nrows, pos0, pos1)
        q = (jnp.where(m0, 1.0, 0.0) + jnp.where(m1, 1.0, 0.0)).astype(BF16)
        buf[slot, pl.ds(r0, nrows), :] = jnp.dot(q, n2, preferred_element_type=F32).astype(BF16)

    group(0, 2 * MOE_TD)
    for r0 in range(2 * MOE_TD, MOE_BR, MOE_BLK):
        @pl.when(used > r0)
        def _():
            group(r0, MOE_BLK)

    _segment_copies(tab_ref, t, n_tiles, buf, xs_ref, sems, slot, to_hbm=True, wait=False)

    @pl.when(t == n_tiles - 1)
    def _():
        if n_tiles >= 2:
            _segment_copies(tab_ref, t - 1, n_tiles, buf, xs_ref, sems, 1 - slot, to_hbm=True, wait=True)
        _segment_copies(tab_ref, t, n_tiles, buf, xs_ref, sems, slot, to_hbm=True, wait=True)
        lax.fori_loop(first_free, n_gtiles, fill_wait, 0)


def _dispatch_call(tab, rt_p, n2_p, rt_s, n2_s, r_max):
    n_ptiles = n2_p.shape[0] // MOE_TD
    n_tiles = n_ptiles + n2_s.shape[0] // MOE_TD
    last_p = n_ptiles - 1
    return pl.pallas_call(
        functools.partial(_dispatch_kernel, n_tiles=n_tiles, n_ptiles=n_ptiles,
                          n_gtiles=r_max // MOE_TG),
        grid_spec=pltpu.PrefetchScalarGridSpec(
            num_scalar_prefetch=1, grid=(n_tiles,),
            in_specs=[pl.BlockSpec((8, MOE_TD), lambda t, tab: (0, jnp.minimum(t, last_p))),
                      pl.BlockSpec((MOE_TD, D), lambda t, tab: (jnp.minimum(t, last_p), 0)),
                      pl.BlockSpec((8, MOE_TD), lambda t, tab: (0, 0)),
                      pl.BlockSpec((MOE_TD, D), lambda t, tab: (0, 0))],
            out_specs=pl.BlockSpec(memory_space=pl.ANY),
            scratch_shapes=[pltpu.VMEM((2, MOE_BR, D), BF16),
                            pltpu.VMEM((MOE_TG, D), BF16),
                            pltpu.SemaphoreType.DMA((2, NE)),
                            pltpu.SemaphoreType.DMA(())]),
        out_shape=jax.ShapeDtypeStruct((r_max, D), BF16),
        compiler_params=pltpu.CompilerParams(dimension_semantics=("arbitrary",),
                                             vmem_limit_bytes=VMEM_LIMIT),
        name="moe_dispatch",
    )(tab, rt_p, n2_p, rt_s, n2_s)


MOE_CK = 3


def _experts_kernel(ctab_ref, xs_ref, w1_ref, w3_ref, w2_ref, ys_ref,
                    w1b, w3b, w2b, xbuf, ybuf, zbuf, in_sem, out_sem, zsem, *, n_gtiles, nc_max):
    e = pl.program_id(0)
    c0 = ctab_ref[e]
    c1 = ctab_ref[e + 1]
    n_chunks = ctab_ref[NE]
    first_free = ctab_ref[NE + 1 + 2 * nc_max]

    def fill_copy(j):
        d = pl.multiple_of(j * MOE_TG, MOE_TG)
        return pltpu.make_async_copy(zbuf, ys_ref.at[pl.ds(d, MOE_TG)], zsem)

    def fill_start(j, carry):
        fill_copy(j).start()
        return carry

    def fill_wait(j, carry):
        fill_copy(j).wait()
        return carry

    def span(c):
        r = pl.multiple_of(ctab_ref[NE + 1 + c], MOE_TG)
        n = pl.multiple_of(ctab_ref[NE + 1 + nc_max + c] * MOE_TG, MOE_TG)
        return r, n

    def in_copy(c, slot):
        r, n = span(c)
        return pltpu.make_async_copy(xs_ref.at[pl.ds(r, n)], xbuf.at[slot, pl.ds(0, n)], in_sem.at[slot])

    def out_copy(c, slot):
        r, n = span(c)
        return pltpu.make_async_copy(ybuf.at[slot, pl.ds(0, n)], ys_ref.at[pl.ds(r, n)], out_sem.at[slot])

    @pl.when(e == 0)
    def _():
        zbuf[...] = jnp.zeros_like(zbuf)
        lax.fori_loop(first_free, n_gtiles, fill_start, 0)

        @pl.when(n_chunks > 0)
        def _():
            in_copy(0, 0).start()

    def compute(slot, rows):
        x = xbuf[slot, pl.ds(0, rows), :]
        a = jnp.dot(x, w1b[...], preferred_element_type=F32)
        b = jnp.dot(x, w3b[...], preferred_element_type=F32)
        hid = a * _sigmoid(a) * b
        y = jnp.dot(hid.astype(BF16), w2b[...], preferred_element_type=F32)
        ybuf[slot, pl.ds(0, rows), :] = y.astype(BF16)

    @pl.when(c1 > c0)
    def _():
        w1b[...] = w1_ref[...].astype(BF16)
        w3b[...] = w3_ref[...].astype(BF16)
        w2b[...] = w2_ref[...].astype(BF16)

        def chunk(c, carry):
            slot = lax.rem(c, 2)

            @pl.when(c + 1 < n_chunks)
            def _():
                in_copy(c + 1, 1 - slot).start()

            in_copy(c, slot).wait()

            @pl.when(c >= 2)
            def _():
                out_copy(c - 2, slot).wait()

            k = ctab_ref[NE + 1 + nc_max + c]
            for kk in range(1, MOE_CK + 1):
                @pl.when(k == kk)
                def _():
                    compute(slot, kk * MOE_TG)

            out_copy(c, slot).start()
            return carry

        lax.fori_loop(c0, c1, chunk, 0)

    @pl.when(e == NE - 1)
    def _():
        @pl.when(n_chunks >= 2)
        def _():
            out_copy(n_chunks - 2, lax.rem(n_chunks, 2)).wait()

        @pl.when(n_chunks >= 1)
        def _():
            out_copy(n_chunks - 1, lax.rem(n_chunks - 1, 2)).wait()

        lax.fori_loop(first_free, n_gtiles, fill_wait, 0)


def _experts_call(ctab, xs, w1, w3, w2, nc_max):
    r_max = xs.shape[0]
    w_map = lambda e, ctab: (e, 0, 0)
    ring = pltpu.VMEM((2, MOE_CK * MOE_TG, D), BF16)
    return pl.pallas_call(
        functools.partial(_experts_kernel, n_gtiles=r_max // MOE_TG, nc_max=nc_max),
        grid_spec=pltpu.PrefetchScalarGridSpec(
            num_scalar_prefetch=1, grid=(NE,),
            in_specs=[pl.BlockSpec(memory_space=pl.ANY),
                      pl.BlockSpec((None, D, DE), w_map),
                      pl.BlockSpec((None, D, DE), w_map),
                      pl.BlockSpec((None, DE, D), w_map)],
            out_specs=pl.BlockSpec(memory_space=pl.ANY),
            scratch_shapes=[pltpu.VMEM((D, DE), BF16), pltpu.VMEM((D, DE), BF16),
                            pltpu.VMEM((DE, D), BF16), ring, ring,
                            pltpu.VMEM((MOE_TG, D), BF16),
                            pltpu.SemaphoreType.DMA((2,)), pltpu.SemaphoreType.DMA((2,)),
                            pltpu.SemaphoreType.DMA(())]),
        out_shape=jax.ShapeDtypeStruct((r_max, D), BF16),
        compiler_params=pltpu.CompilerParams(dimension_semantics=("arbitrary",),
                                             vmem_limit_bytes=VMEM_LIMIT),
        name="moe_experts",
    )(ctab, xs, w1, w3, w2)


def _combine_kernel(tab_ref, rt_ref, x1_ref, gt2_ref, gf_ref, ys_ref, y_ref, buf, acc, sems,
                    *, n_tiles, t_off):
    i = pl.program_id(0)
    t = i + t_off
    slot = lax.rem(i, 2)

    @pl.when(i == 0)
    def _():
        buf[...] = jnp.zeros_like(buf)
        _segment_copies(tab_ref, t, n_tiles, buf, ys_ref, sems, slot, to_hbm=False, wait=False)

    @pl.when(i + 1 < pl.num_programs(0))
    def _():
        _segment_copies(tab_ref, t + 1, n_tiles, buf, ys_ref, sems, 1 - slot, to_hbm=False, wait=False)

    _segment_copies(tab_ref, t, n_tiles, buf, ys_ref, sems, slot, to_hbm=False, wait=True)

    rt = rt_ref[...]
    pos0, pos1 = _slot_positions(rt)
    used = _tile_rows(tab_ref, t, n_tiles)

    def ungroup(r0, nrows):
        m0, m1 = _one_hot_rows(r0, nrows, pos0, pos1)
        q = (jnp.where(m0, 1.0, 0.0) + jnp.where(m1, 1.0, 0.0)).astype(BF16)
        gw = jnp.sum(jnp.where(m0, rt[2:3, :], 0.0) + jnp.where(m1, rt[3:4, :], 0.0),
                     axis=1, keepdims=True)
        yv = (buf[slot, pl.ds(r0, nrows), :].astype(F32) * gw).astype(BF16)
        return lax.dot_general(q, yv, (((0,), (0,)), ((), ())), preferred_element_type=F32)

    acc[...] = ungroup(0, 2 * MOE_TD)
    for r0 in range(2 * MOE_TD, MOE_BR, MOE_BLK):
        @pl.when(used > r0)
        def _():
            acc[...] += ungroup(r0, MOE_BLK)

    xo = x1_ref[...] + gt2_ref[...] * acc[...]
    y_ref[...] = _rms(xo) * gf_ref[...]


def _combine_call(tab, rt, x1, gt2, gt2_spec, g_final, ys, n_tiles, t_off):
    n_out_tiles = x1.shape[0] // MOE_TD
    return pl.pallas_call(
        functools.partial(_combine_kernel, n_tiles=n_tiles, t_off=t_off),
        grid_spec=pltpu.PrefetchScalarGridSpec(
            num_scalar_prefetch=1, grid=(n_out_tiles,),
            in_specs=[pl.BlockSpec((8, MOE_TD), lambda t, tab: (0, t)),
                      pl.BlockSpec((MOE_TD, D), lambda t, tab: (t, 0)),
                      gt2_spec,
                      pl.BlockSpec((1, D), lambda t, tab: (0, 0)),
                      pl.BlockSpec(memory_space=pl.ANY)],
            out_specs=pl.BlockSpec((MOE_TD, D), lambda t, tab: (t, 0)),
            scratch_shapes=[pltpu.VMEM((2, MOE_BR, D), BF16),
                            pltpu.VMEM((MOE_TD, D), F32),
                            pltpu.SemaphoreType.DMA((2, NE))]),
        out_shape=jax.ShapeDtypeStruct((n_out_tiles * MOE_TD, D), F32),
        compiler_params=pltpu.CompilerParams(dimension_semantics=("arbitrary",),
                                             vmem_limit_bytes=VMEM_LIMIT),
        name="moe_combine",
    )(tab, rt, x1, gt2, g_final.reshape(1, D), ys)


def _moe_plan(cnt, nc_max):
    cnt8 = (cnt + ROW_ALIGN - 1) // ROW_ALIGN * ROW_ALIGN
    seg_rows = cnt8.sum(axis=0)
    seg_pad = (seg_rows + MOE_TG - 1) // MOE_TG * MOE_TG
    seg_start = jnp.cumsum(seg_pad) - seg_pad
    dst = seg_start[None, :] + jnp.cumsum(cnt8, axis=0) - cnt8
    boff = jnp.cumsum(cnt8, axis=1) - cnt8
    tile_end = jnp.cumsum(seg_pad // MOE_TG)
    n_active = tile_end[-1:].astype(jnp.int32)
    tab = jnp.concatenate([cnt8.ravel(), boff.ravel(), dst.ravel(),
                           seg_start + seg_rows, seg_pad - seg_rows, n_active,
                           cnt8.sum(axis=1)]).astype(jnp.int32)
    nt = seg_pad // MOE_TG
    nfull = nt // MOE_CK
    rem = nt % MOE_CK
    nch = nfull + (rem > 0)
    cend = jnp.cumsum(nch)
    cstart = cend - nch
    c = jnp.arange(nc_max, dtype=jnp.int32)
    ce = jnp.minimum(jnp.sum(c[:, None] >= cend[None, :], axis=1), NE - 1)
    local = c - cstart[ce]
    valid = c < cend[-1]
    ck = jnp.where(valid, jnp.where(local < nfull[ce], MOE_CK, rem[ce]), 0)
    crow = jnp.where(valid, seg_start[ce] + local * (MOE_CK * MOE_TG), 0)
    ctab = jnp.concatenate([cstart, cend[-1:], crow, ck, n_active]).astype(jnp.int32)
    return tab, ctab


def kernel(x_prompt, x_sample, c_prompt, c_sample, state_ssm_re, state_ssm_im, cache_conv, w_ada, b_ada, g_norm_mix, w_in, ssm_a_re, ssm_a_im, ssm_log_dt, ssm_b_re, ssm_b_im, ssm_c_re, ssm_c_im, ssm_d, w_ssm_glu, b_ssm_glu, w_dw, b_dw, ln_conv_g, ln_conv_b, g_out_ssm, g_out_conv, w_out, g_norm_ffn, w_router_grp, b_router_grp, w_router_exp, b_router_exp, w_exp_gate, w_exp_up, w_exp_down, g_final):
    depth = w_ada.shape[0]
    assert depth == 1
    bsz, seq, _ = x_prompt.shape
    nb, nt, _ = x_sample.shape

    n_c = bsz + nb
    c_pad = -n_c % 16
    c_all = jnp.concatenate([c_prompt, c_sample, jnp.zeros((c_pad, D), F32)], axis=0)
    mod = _mod_call(c_all, w_ada[0], b_ada[0])[:n_c]
    mod_p = mod[:bsz].reshape(bsz, 6, D)
    mod_s = mod[bsz:]

    ab_re, ab_im, bb_re, bb_im, c_im_neg = _ssm_prep_call(
        ssm_a_re[0], ssm_a_im[0], ssm_log_dt[0], ssm_b_re[0], ssm_b_im[0], ssm_c_im[0])
    wb, wc = _block_diag_weights(bb_re, bb_im, ssm_c_re[0], c_im_neg)
    a_tok = jnp.stack([ab_re.reshape(NCHUNK, LANES), ab_im.reshape(NCHUNK, LANES)])
    a_row = jnp.stack([ab_re.reshape(NSTATE), ab_im.reshape(NSTATE)])

    w_r = jnp.concatenate([w_router_exp[0].reshape(D, NE), w_router_grp[0],
                           jnp.zeros((D, LANES - NE - NG), F32)], axis=1)
    w_r_hi = w_r.astype(BF16)
    w_r = jnp.concatenate([w_r_hi, (w_r - w_r_hi.astype(F32)).astype(BF16)], axis=1)
    b_r = jnp.concatenate([b_router_exp[0].reshape(NE), b_router_grp[0],
                           jnp.zeros((LANES - NE - NG,), F32)]).reshape(1, LANES)
    w_dw_p = jnp.concatenate([w_dw[0], jnp.zeros((1, CONV_CH), F32)], axis=0)

    row = lambda v: v.reshape(1, -1)
    common_a = (row(g_norm_mix[0]), w_in[0].astype(BF16), wb)
    common_b = (wc, row(ssm_d[0].reshape(SSM_W)), w_ssm_glu[0].astype(BF16), row(b_ssm_glu[0]),
                w_dw_p, row(b_dw[0]), row(ln_conv_g[0]), row(ln_conv_b[0]),
                row(g_out_ssm[0]), row(g_out_conv[0]), w_out[0].astype(BF16),
                row(g_norm_ffn[0]), w_r, b_r)

    n_p = bsz * seq
    n_s = nb * nt
    n_all = n_p + n_s
    assert n_s == MOE_TD and seq % MOE_TD == 0 and MOE_TD % PROMPT_TC == 0
    wts_p = common_a + (a_tok,) + common_b
    x1_p, n2_p, rt_p, cnt_p, hr_p, hi_p, cache_p = _prompt_mixer_call(x_prompt, mod_p, wts_p, PROMPT_TC)

    x_tm = jnp.transpose(x_sample, (1, 0, 2)).reshape(nt * nb, D)
    cache_tm = jnp.transpose(cache_conv[0], (1, 0, 2))
    wts_s = common_a + (a_row,) + common_b
    x1_s, n2_s, rt_s, cnt_s, hr_s, hi_s, glu_s = _sample_mixer_call(
        x_tm, mod_s, state_ssm_re[0].reshape(nb, NSTATE), state_ssm_im[0].reshape(nb, NSTATE),
        cache_tm, wts_s, nb, nt)

    n_ptiles = n_p // MOE_TD
    n_tiles = n_all // MOE_TD
    r_max = -(-(2 * n_all + n_tiles * NE * (ROW_ALIGN - 1) + NE * (MOE_TG - ROW_ALIGN)) // MOE_TG) * MOE_TG
    cnt = jnp.concatenate([cnt_p.reshape(n_ptiles, MOE_TD // PROMPT_TC, LANES).sum(axis=1), cnt_s])
    nc_max = r_max // MOE_TG // MOE_CK + NE
    tab, ctab = _moe_plan(cnt[:, :NE].astype(jnp.int32), nc_max)
    xs = _dispatch_call(tab, rt_p, n2_p, rt_s, n2_s, r_max)
    ys = _experts_call(ctab, xs, w_exp_gate[0], w_exp_up[0], w_exp_down[0], nc_max)
    tiles_per_b = seq // MOE_TD
    gt2_p = mod_p[:, 5:6, :]
    y_p = _combine_call(tab, rt_p, x1_p, gt2_p,
                        pl.BlockSpec((None, 1, D), lambda t, tab: (t // tiles_per_b, 0, 0)),
                        g_final, ys, n_tiles, 0)
    gt2_s = jnp.tile(mod_s[:, 5 * D:], (nt, 1))
    y_s = _combine_call(tab, rt_s, x1_s, gt2_s,
                        pl.BlockSpec((n_s, D), lambda t, tab: (0, 0)),
                        g_final, ys, n_tiles, n_ptiles)

    y_prompt = y_p.reshape(bsz, seq, D)
    y_sample = jnp.transpose(y_s.reshape(nt, nb, D), (1, 0, 2))
    new_cache_s = jnp.concatenate(
        [cache_conv[0][:, nt:, :], jnp.transpose(glu_s.reshape(nt, nb, CONV_CH), (1, 0, 2))], axis=1)
    return (y_prompt, y_sample,
            hr_p.reshape(1, bsz, G, P), hi_p.reshape(1, bsz, G, P), cache_p[None],
            hr_s.reshape(1, nb, G, P), hi_s.reshape(1, nb, G, P), new_cache_s[None])
```

```python
import functools

import jax
import jax.numpy as jnp
import numpy as np
from jax import lax
from jax.experimental import pallas as pl
from jax.experimental.pallas import tpu as pltpu

F32 = jnp.float32
BF16 = jnp.bfloat16

D = 1024
SSM_W = 512
CONV_CH = 512
G = 32
H = 16
P = 64
KW = 31
CB = KW - 1
NE = 32
NG = 4
EPG = 8
DE = 512
EPS = 1e-6
LANES = 128
NSTATE = G * P
NCHUNK = NSTATE // LANES

PROMPT_TC = 512
SCAN_GROUP = 8
VMEM_LIMIT = 56 * 1024 * 1024


def _rms(x):
    return x * lax.rsqrt(jnp.mean(x * x, axis=-1, keepdims=True) + EPS)


def _sigmoid(x):
    return 0.5 * jnp.tanh(0.5 * x) + 0.5


def _gelu_tanh(y):
    c = np.sqrt(2.0 / np.pi).astype(np.float32)
    return y * (0.5 * (1.0 + jnp.tanh(c * (y + 0.044715 * (y * y * y)))))


def _bdot(a, b):
    return jnp.dot(a.astype(BF16), b, preferred_element_type=F32)


def _mod_kernel(c_ref, w_ref, b_ref, o_ref):
    c = c_ref[...]
    s = c * _sigmoid(c)
    n = s.shape[0]
    s_hi = s.astype(BF16)
    lhs = jnp.concatenate([s_hi, (s - s_hi.astype(F32)).astype(BF16)], axis=0)
    w = w_ref[...]
    w_hi = w.astype(BF16)
    w_lo = (w - w_hi.astype(F32)).astype(BF16)
    p_hi = jnp.dot(lhs, w_hi, preferred_element_type=F32)
    p_lo = jnp.dot(lhs, w_lo, preferred_element_type=F32)
    o_ref[...] = (p_hi[:n] + p_lo[:n]) + (p_hi[n:] + p_lo[n:]) + b_ref[...]


def _mod_call(c_all, w_ada, b_ada):
    n = c_all.shape[0]
    tn = 512
    return pl.pallas_call(
        _mod_kernel,
        grid=(6 * D // tn,),
        in_specs=[pl.BlockSpec((n, D), lambda j: (0, 0)),
                  pl.BlockSpec((D, tn), lambda j: (0, j)),
                  pl.BlockSpec((1, tn), lambda j: (0, j))],
        out_specs=pl.BlockSpec((n, tn), lambda j: (0, j)),
        out_shape=jax.ShapeDtypeStruct((n, 6 * D), F32),
        compiler_params=pltpu.CompilerParams(dimension_semantics=("arbitrary",)),
        name="mod",
    )(c_all, w_ada, b_ada.reshape(1, 6 * D))


def _ssm_prep_kernel(a_re, a_im, log_dt, b_re, b_im, c_im,
                     ab_re_o, ab_im_o, bb_re_o, bb_im_o, cneg_o):
    lam_re = jnp.minimum(a_re[...], -1e-4)
    lam_im = a_im[...]
    dt = jnp.exp(log_dt[...])
    mag = jnp.exp(lam_re * dt)
    ab_re = mag * jnp.cos(lam_im * dt)
    ab_im = mag * jnp.sin(lam_im * dt)
    den = lam_re * lam_re + lam_im * lam_im
    num_re = ab_re - 1.0
    coef_re = (num_re * lam_re + ab_im * lam_im) / den
    coef_im = (ab_im * lam_re - num_re * lam_im) / den
    ab_re_o[...] = ab_re
    ab_im_o[...] = ab_im
    br = b_re[...]
    bi = b_im[...]
    bb_re_o[...] = coef_re * br - coef_im * bi
    bb_im_o[...] = coef_re * bi + coef_im * br
    cneg_o[...] = -c_im[...]


def _ssm_prep_call(a_re, a_im, log_dt, b_re, b_im, c_im):
    flat = lambda v: v.reshape(1, NSTATE)
    b_hs = lambda v: jnp.transpose(v, (2, 0, 1)).reshape(H, NSTATE)
    dt_row = jnp.broadcast_to(log_dt[:, None], (G, P)).reshape(1, NSTATE)
    ab_re, ab_im, bb_re, bb_im, cneg = pl.pallas_call(
        _ssm_prep_kernel,
        out_shape=(jax.ShapeDtypeStruct((1, NSTATE), F32), jax.ShapeDtypeStruct((1, NSTATE), F32),
                   jax.ShapeDtypeStruct((H, NSTATE), F32), jax.ShapeDtypeStruct((H, NSTATE), F32),
                   jax.ShapeDtypeStruct((G * H, P), F32)),
        name="ssm_prep",
    )(flat(a_re), flat(a_im), dt_row, b_hs(b_re), b_hs(b_im), c_im.reshape(G * H, P))
    ghp = lambda v: jnp.transpose(v.reshape(H, G, P), (1, 0, 2))
    return (ab_re.reshape(G, P), ab_im.reshape(G, P), ghp(bb_re), ghp(bb_im), cneg.reshape(G, H, P))


def _block_diag_weights(bb_re, bb_im, c_re, c_im_neg):
    eye8 = jnp.eye(8, dtype=F32)
    eye4 = jnp.eye(4, dtype=F32)
    eye2 = jnp.eye(2, dtype=F32)

    def wb_part(bb):
        x = bb.reshape(4, 8, H, P)
        return jnp.einsum('qghp,gk->qghkp', x, eye8).reshape(4, 8 * H, 8 * P)

    wb = jnp.concatenate([wb_part(bb_re), wb_part(bb_im)], axis=-1).astype(BF16)

    def wc_part(c):
        x = c.reshape(4, 4, 2, H, P)
        y = jnp.einsum('qjghp,jk,gl->qjgpklh', x, eye4, eye2)
        return y.reshape(NCHUNK, 2 * P, 4 * 2 * H)

    wc = jnp.concatenate([wc_part(c_re), wc_part(c_im_neg)], axis=1).astype(BF16)
    return wb, wc


def _front(x, sc1, sh1, g_mix, w_in_ref):
    n = _rms(x) * g_mix * (1.0 + sc1) + sh1
    proj = _bdot(n, w_in_ref[...])
    u = proj[:, :SSM_W]
    glu = proj[:, SSM_W:SSM_W + CONV_CH] * _sigmoid(proj[:, SSM_W + CONV_CH:])
    return u, glu


def _ssm_out(y_lin, u, d_skip, w_glu_ref, b_glu, g_out_ssm):
    y = _gelu_tanh(y_lin + d_skip * u)
    ys = y * _sigmoid(_bdot(y, w_glu_ref[...]) + b_glu)
    return _rms(ys) * g_out_ssm


def _conv_out(conv, b_dw, ln_g, ln_b, g_out_conv):
    c = conv + b_dw
    mu = jnp.mean(c, axis=-1, keepdims=True)
    cc = c - mu
    var = jnp.mean(cc * cc, axis=-1, keepdims=True)
    ln = cc * lax.rsqrt(var + EPS) * ln_g + ln_b
    yc = ln * _sigmoid(ln)
    return _rms(yc) * g_out_conv


def _route(n2, w_r_ref, b_r):
    rows = n2.shape[0]
    n_hi = n2.astype(BF16)
    n_lo = (n2 - n_hi.astype(F32)).astype(BF16)
    parts = jnp.dot(jnp.concatenate([n_hi, n_lo], axis=0), w_r_ref[...], preferred_element_type=F32)
    lg = (parts[:rows, :LANES] + parts[:rows, LANES:]) + (parts[rows:, :LANES] + parts[rows:, LANES:]) + b_r
    lane = lax.broadcasted_iota(jnp.int32, (rows, LANES), 1).astype(F32)
    ninf = -jnp.inf
    big = 1e9
    gmask = jnp.logical_and(lane >= NE, lane < NE + NG)
    gl = jnp.where(gmask, lg, ninf)
    gmax = jnp.max(gl, axis=-1, keepdims=True)
    gsum = jnp.sum(jnp.where(gmask, jnp.exp(gl - gmax), 0.0), axis=-1, keepdims=True)
    p_top = 1.0 / gsum
    gi = jnp.min(jnp.where(gl == gmax, lane, big), axis=-1, keepdims=True) - NE
    lo = gi * EPG
    emask = jnp.logical_and(lane >= lo, lane < lo + EPG)
    el = jnp.where(emask, lg, ninf)
    m1 = jnp.max(el, axis=-1, keepdims=True)
    i1 = jnp.min(jnp.where(el == m1, lane, big), axis=-1, keepdims=True)
    el2 = jnp.where(lane == i1, ninf, el)
    m2 = jnp.max(el2, axis=-1, keepdims=True)
    i2 = jnp.min(jnp.where(el2 == m2, lane, big), axis=-1, keepdims=True)
    e2 = jnp.exp(m2 - m1)
    den = 1.0 + e2
    w1 = p_top / den
    w2 = p_top * e2 / den
    cnt = jnp.sum(jnp.where(lane == i1, 1.0, 0.0) + jnp.where(lane == i2, 1.0, 0.0),
                  axis=0, keepdims=True)
    cols = (jnp.where(lane == 0.0, i1, 0.0) + jnp.where(lane == 1.0, i2, 0.0)
            + jnp.where(lane == 2.0, w1, 0.0) + jnp.where(lane == 3.0, w2, 0.0))
    return cols.T[0:8, :], cnt


def _tail(x, ns, nc, gt1, sc2, sh2, w_out_ref, g_ffn, w_r_ref, b_r):
    merged = _bdot(jnp.concatenate([ns, nc], axis=-1), w_out_ref[...])
    x1 = x + gt1 * merged
    n2 = _rms(x1) * g_ffn * (1.0 + sc2) + sh2
    rt, cnt = _route(n2, w_r_ref, b_r)
    return x1, n2, rt, cnt


def _prompt_mixer_kernel(x_ref, mod_ref, g_mix_ref, w_in_ref, wb_ref, a_ref, wc_ref, dsk_ref,
                         w_glu_ref, b_glu_ref, w_dw_ref, b_dw_ref, ln_g_ref, ln_b_ref,
                         g_os_ref, g_oc_ref, w_out_ref, g_ffn_ref, w_r_ref, b_r_ref,
                         x1_ref, n2_ref, rt_ref, cnt_ref, hr_ref, hi_ref, cache_ref,
                         sre, sim, hst, ebuf, eshift, ncbuf, *, tc, pt):
    c = pl.program_id(1)
    nc_chunks = pl.num_programs(1)

    @pl.when(c == 0)
    def _():
        hst[...] = jnp.zeros_like(hst)
        ebuf[pl.ds(0, 32), :] = jnp.zeros((32, CONV_CH), F32)

    x = x_ref[...]
    mod = mod_ref[...]
    sh1, sc1, gt1 = mod[0:1], mod[1:2], mod[2:3]
    sh2, sc2 = mod[3:4], mod[4:5]

    u, glu = _front(x, sc1, sh1, g_mix_ref[...], w_in_ref)
    ebuf[pl.ds(32, tc), :] = glu

    for s in range(1, 8):
        eshift[s - 1, pl.ds(0, tc + 24), :] = ebuf[pl.ds(s, tc + 24), :]
    rb = 64
    convs = []
    for r0 in range(0, tc, rb):
        acc = None
        for k in range(KW):
            a8, s = (k + 2) // 8 * 8, (k + 2) % 8
            win = ebuf[pl.ds(r0 + a8, rb), :] if s == 0 else eshift[s - 1, pl.ds(r0 + a8, rb), :]
            term = w_dw_ref[k:k + 1, :] * win
            acc = term if acc is None else acc + term
        convs.append(acc)
    conv = jnp.concatenate(convs, axis=0)
    ncbuf[...] = _conv_out(conv, b_dw_ref[...], ln_g_ref[...], ln_b_ref[...], g_oc_ref[...])

    @pl.when(c == nc_chunks - 1)
    def _():
        cache_ref[...] = ebuf[pl.ds(tc + 2, CB), :]

    ebuf[pl.ds(0, 32), :] = ebuf[pl.ds(tc, 32), :]

    ub = u.astype(BF16)
    for q in range(4):
        r = jnp.dot(ub[:, q * LANES:(q + 1) * LANES], wb_ref[q], preferred_element_type=F32)
        for k in range(4):
            j = 4 * q + k
            sre[pl.ds(j * pt, tc), :] = r[:, k * LANES:(k + 1) * LANES]
            sim[pl.ds(j * pt, tc), :] = r[:, SSM_W + k * LANES:SSM_W + (k + 1) * LANES]

    ar0, ar1 = a_ref[0, 0:8, :], a_ref[0, 8:16, :]
    ai0, ai1 = a_ref[1, 0:8, :], a_ref[1, 8:16, :]

    def step(g, carry):
        hr0, hr1, hi0, hi1 = carry
        t0 = g * SCAN_GROUP
        idx = [(pl.ds(t0 + u, 8, stride=pt), pl.ds(t0 + u + 8 * pt, 8, stride=pt))
               for u in range(SCAN_GROUP)]
        bu = [(sre[i0, :], sre[i1, :], sim[i0, :], sim[i1, :]) for (i0, i1) in idx]
        hs = []
        for br0, br1, bi0, bi1 in bu:
            nr0 = ar0 * hr0 - ai0 * hi0 + br0
            ni0 = ar0 * hi0 + ai0 * hr0 + bi0
            nr1 = ar1 * hr1 - ai1 * hi1 + br1
            ni1 = ar1 * hi1 + ai1 * hr1 + bi1
            hr0, hr1, hi0, hi1 = nr0, nr1, ni0, ni1
            hs.append((nr0, nr1, ni0, ni1))
        for (i0, i1), (nr0, nr1, ni0, ni1) in zip(idx, hs):
            sre[i0, :] = nr0
            sim[i0, :] = ni0
            sre[i1, :] = nr1
            sim[i1, :] = ni1
        return hr0, hr1, hi0, hi1

    init = (hst[0, 0:8, :], hst[0, 8:16, :], hst[1, 0:8, :], hst[1, 8:16, :])
    hr0, hr1, hi0, hi1 = lax.fori_loop(0, tc // SCAN_GROUP, step, init)
    hst[0, 0:8, :] = hr0
    hst[0, 8:16, :] = hr1
    hst[1, 0:8, :] = hi0
    hst[1, 8:16, :] = hi1

    @pl.when(c == nc_chunks - 1)
    def _():
        hr_ref[...] = hst[0]
        hi_ref[...] = hst[1]

    ys = []
    for q in range(4):
        acc = None
        for jj in range(4):
            j = 4 * q + jj
            lhs = jnp.concatenate([sre[pl.ds(j * pt, tc), :], sim[pl.ds(j * pt, tc), :]], axis=-1)
            d = jnp.dot(lhs.astype(BF16), wc_ref[j], preferred_element_type=F32)
            acc = d if acc is None else acc + d
        ys.append(acc)
    y_lin = jnp.concatenate(ys, axis=-1)
    ns = _ssm_out(y_lin, u, dsk_ref[...], w_glu_ref, b_glu_ref[...], g_os_ref[...])
    nc = ncbuf[...]

    x1, n2, rt, cnt = _tail(x, ns, nc, gt1, sc2, sh2, w_out_ref, g_ffn_ref[...], w_r_ref, b_r_ref[...])
    x1_ref[...] = x1
    n2_ref[...] = n2.astype(BF16)
    rt_ref[...] = rt
    cnt_ref[...] = cnt


def _const_spec(shape):
    nd = len(shape)
    return pl.BlockSpec(shape, lambda b, c: (0,) * nd)


def _prompt_mixer_call(x, mod6, wts, tc):
    bsz, t, _ = x.shape
    n_all = bsz * t
    pt = tc + 8
    nc = t // tc
    kern = functools.partial(_prompt_mixer_kernel, tc=tc, pt=pt)
    in_specs = [pl.BlockSpec((None, tc, D), lambda b, c: (b, c, 0)),
                pl.BlockSpec((None, 6, D), lambda b, c: (b, 0, 0))]
    in_specs += [_const_spec(w.shape) for w in wts]
    out_shape = (jax.ShapeDtypeStruct((n_all, D), F32),
                 jax.ShapeDtypeStruct((n_all, D), BF16),
                 jax.ShapeDtypeStruct((8, n_all), F32),
                 jax.ShapeDtypeStruct((n_all // tc, 1, LANES), F32),
                 jax.ShapeDtypeStruct((bsz, NCHUNK, LANES), F32),
                 jax.ShapeDtypeStruct((bsz, NCHUNK, LANES), F32),
                 jax.ShapeDtypeStruct((bsz, CB, CONV_CH), F32))
    out_specs = (pl.BlockSpec((tc, D), lambda b, c: (b * nc + c, 0)),
                 pl.BlockSpec((tc, D), lambda b, c: (b * nc + c, 0)),
                 pl.BlockSpec((8, tc), lambda b, c: (0, b * nc + c)),
                 pl.BlockSpec((None, 1, LANES), lambda b, c: (b * nc + c, 0, 0)),
                 pl.BlockSpec((None, NCHUNK, LANES), lambda b, c: (b, 0, 0)),
                 pl.BlockSpec((None, NCHUNK, LANES), lambda b, c: (b, 0, 0)),
                 pl.BlockSpec((None, CB, CONV_CH), lambda b, c: (b, 0, 0)))
    scratch = [pltpu.VMEM((NCHUNK * pt, LANES), F32),
               pltpu.VMEM((NCHUNK * pt, LANES), F32),
               pltpu.VMEM((2, NCHUNK, LANES), F32),
               pltpu.VMEM((tc + 32, CONV_CH), F32),
               pltpu.VMEM((7, tc + 32, CONV_CH), F32),
               pltpu.VMEM((tc, CONV_CH), F32)]
    return pl.pallas_call(
        kern, grid=(bsz, nc), in_specs=in_specs, out_specs=out_specs, out_shape=out_shape,
        scratch_shapes=scratch,
        compiler_params=pltpu.CompilerParams(dimension_semantics=("arbitrary", "arbitrary"),
                                             vmem_limit_bytes=VMEM_LIMIT),
        name="prompt_mixer",
    )(x, mod6, *wts)


def _sample_mixer_kernel(x_ref, mod_ref, h0r_ref, h0i_ref, cache_ref,
                         g_mix_ref, w_in_ref, wb_ref, a_ref, wc_ref, dsk_ref,
                         w_glu_ref, b_glu_ref, w_dw_ref, b_dw_ref, ln_g_ref, ln_b_ref,
                         g_os_ref, g_oc_ref, w_out_ref, g_ffn_ref, w_r_ref, b_r_ref,
                         x1_ref, n2_ref, rt_ref, cnt_ref, hr_ref, hi_ref, glu_ref,
                         sre, sim, *, nb, nt):
    x = x_ref[...]

    def rows(i):
        m = mod_ref[:, i * D:(i + 1) * D]
        return jnp.concatenate([m] * nt, axis=0)

    sh1, sc1, gt1, sh2, sc2 = rows(0), rows(1), rows(2), rows(3), rows(4)
    u, glu = _front(x, sc1, sh1, g_mix_ref[...], w_in_ref)
    glu_ref[...] = glu

    ub = u.astype(BF16)
    for q in range(4):
        r = jnp.dot(ub[:, q * LANES:(q + 1) * LANES], wb_ref[q], preferred_element_type=F32)
        sre[:, q * SSM_W:(q + 1) * SSM_W] = r[:, :SSM_W]
        sim[:, q * SSM_W:(q + 1) * SSM_W] = r[:, SSM_W:]

    ar = a_ref[0:1, :]
    ai = a_ref[1:2, :]
    hr = h0r_ref[...]
    hi = h0i_ref[...]
    for t in range(nt):
        rs = pl.ds(t * nb, nb)
        nr = ar * hr - ai * hi + sre[rs, :]
        ni = ar * hi + ai * hr + sim[rs, :]
        sre[rs, :] = nr
        sim[rs, :] = ni
        hr, hi = nr, ni
    hr_ref[...] = hr
    hi_ref[...] = hi

    ys = []
    for q in range(4):
        acc = None
        for jj in range(4):
            j = 4 * q + jj
            lhs = jnp.concatenate([sre[:, j * LANES:(j + 1) * LANES],
                                   sim[:, j * LANES:(j + 1) * LANES]], axis=-1)
            d = jnp.dot(lhs.astype(BF16), wc_ref[j], preferred_element_type=F32)
            acc = d if acc is None else acc + d
        ys.append(acc)
    y_lin = jnp.concatenate(ys, axis=-1)
    ns = _ssm_out(y_lin, u, dsk_ref[...], w_glu_ref, b_glu_ref[...], g_os_ref[...])

    def ext(jrow):
        if jrow < CB:
            return cache_ref[jrow]
        return glu[(jrow - CB) * nb:(jrow - CB + 1) * nb, :]

    convs = []
    for t in range(nt):
        acc = None
        for k in range(KW):
            term = w_dw_ref[k:k + 1, :] * ext(t + k)
            acc = term if acc is None else acc + term
        convs.append(acc)
    conv = jnp.concatenate(convs, axis=0)
    nc = _conv_out(conv, b_dw_ref[...], ln_g_ref[...], ln_b_ref[...], g_oc_ref[...])

    x1, n2, rt, cnt = _tail(x, ns, nc, gt1, sc2, sh2, w_out_ref, g_ffn_ref[...], w_r_ref, b_r_ref[...])
    x1_ref[...] = x1
    n2_ref[...] = n2.astype(BF16)
    rt_ref[...] = rt
    cnt_ref[...] = cnt


def _sample_mixer_call(x_tm, mod_s, h0r, h0i, cache_tm, wts, nb, nt):
    n = nb * nt
    kern = functools.partial(_sample_mixer_kernel, nb=nb, nt=nt)
    out_shape = (jax.ShapeDtypeStruct((n, D), F32),
                 jax.ShapeDtypeStruct((n, D), BF16),
                 jax.ShapeDtypeStruct((8, n), F32),
                 jax.ShapeDtypeStruct((1, LANES), F32),
                 jax.ShapeDtypeStruct((nb, NSTATE), F32),
                 jax.ShapeDtypeStruct((nb, NSTATE), F32),
                 jax.ShapeDtypeStruct((n, CONV_CH), F32))
    scratch = [pltpu.VMEM((n, NSTATE), F32), pltpu.VMEM((n, NSTATE), F32)]
    return pl.pallas_call(
        kern, out_shape=out_shape, scratch_shapes=scratch,
        compiler_params=pltpu.CompilerParams(vmem_limit_bytes=VMEM_LIMIT),
        name="sample_mixer",
    )(x_tm, mod_s, h0r, h0i, cache_tm, *wts)


ROW_ALIGN = 16
MOE_TD = 512
MOE_BR = MOE_TD * 2 + NE * ROW_ALIGN
MOE_TG = 256


def _slot_positions(rt):
    t = rt.shape[1]
    e0 = rt[0:1, :]
    e1 = rt[1:2, :]
    sub = lax.broadcasted_iota(jnp.int32, (LANES, t), 0).astype(F32)
    a0 = jnp.where(sub == e0, 1.0, 0.0)
    a1 = jnp.where(sub == e1, 1.0, 0.0)
    at = a0 + a1
    r = lax.broadcasted_iota(jnp.int32, (t, t), 0)
    c = lax.broadcasted_iota(jnp.int32, (t, t), 1)
    before = jnp.where(r < c, 1.0, 0.0).astype(BF16)
    rank = jnp.dot(at.astype(BF16), before, preferred_element_type=F32)
    cnt = jnp.sum(at, axis=1, keepdims=True)
    cnt_al = jnp.ceil(cnt * (1.0 / ROW_ALIGN)) * float(ROW_ALIGN)
    er = lax.broadcasted_iota(jnp.int32, (LANES, LANES), 0)
    ec = lax.broadcasted_iota(jnp.int32, (LANES, LANES), 1)
    lower = jnp.where(ec < er, 1.0, 0.0)
    base = jnp.dot(lower, jnp.broadcast_to(cnt_al, (LANES, LANES)), preferred_element_type=F32,
                   precision=lax.Precision.HIGHEST)[:, 0:1]
    slot = rank + base
    pos0 = jnp.sum(a0 * slot, axis=0, keepdims=True)
    pos1 = jnp.sum(a1 * slot, axis=0, keepdims=True)
    return pos0, pos1


def _segment_copies(tab_ref, t, n_tiles, buf, hbm, sems, slot, to_hbm, wait):
    for e in range(NE):
        n = pl.multiple_of(tab_ref[t * NE + e], ROW_ALIGN)
        b = pl.multiple_of(tab_ref[(n_tiles + t) * NE + e], ROW_ALIGN)
        d = pl.multiple_of(tab_ref[(2 * n_tiles + t) * NE + e], ROW_ALIGN)
        vm = buf.at[slot, pl.ds(b, n)]
        hb = hbm.at[pl.ds(d, n)]
        cp = pltpu.make_async_copy(vm, hb, sems.at[slot, e]) if to_hbm else \
            pltpu.make_async_copy(hb, vm, sems.at[slot, e])

        @pl.when(n > 0)
        def _():
            if wait:
                cp.wait()
            else:
                cp.start()


def _tile_rows(tab_ref, t, n_tiles):
    return tab_ref[3 * n_tiles * NE + 2 * NE + 1 + t]


def _one_hot_rows(r0, nrows, pos0, pos1):
    row = (lax.broadcasted_iota(jnp.int32, (nrows, MOE_TD), 0) + r0).astype(F32)
    return row == pos0, row == pos1


MOE_BLK = 256


def _dispatch_kernel(tab_ref, rtp_ref, n2p_ref, rts_ref, n2s_ref, xs_ref, pos_ref, buf, zbuf, sems, zsem,
                     *, n_tiles, n_ptiles, n_gtiles):
    t = pl.program_id(0)
    slot = lax.rem(t, 2)
    first_free = tab_ref[3 * n_tiles * NE + 2 * NE]

    def fill_copy(j):
        d = pl.multiple_of(j * MOE_TG, MOE_TG)
        return pltpu.make_async_copy(zbuf, xs_ref.at[pl.ds(d, MOE_TG)], zsem)

    def fill_start(j, carry):
        fill_copy(j).start()
        return carry

    def fill_wait(j, carry):
        fill_copy(j).wait()
        return carry

    @pl.when(t == 0)
    def _():
        zbuf[...] = jnp.zeros_like(zbuf)
        for phase in range(2):
            for e in range(NE):
                d = pl.multiple_of(tab_ref[3 * n_tiles * NE + e], ROW_ALIGN)
                n = pl.multiple_of(tab_ref[3 * n_tiles * NE + NE + e], ROW_ALIGN)
                cp = pltpu.make_async_copy(zbuf.at[pl.ds(0, n)], xs_ref.at[pl.ds(d, n)], sems.at[1, e])

                @pl.when(n > 0)
                def _():
                    if phase == 0:
                        cp.start()
                    else:
                        cp.wait()

        lax.fori_loop(first_free, n_gtiles, fill_start, 0)

    @pl.when(t >= 2)
    def _():
        _segment_copies(tab_ref, t - 2, n_tiles, buf, xs_ref, sems, slot, to_hbm=True, wait=True)

    is_sample = t >= n_ptiles
    rt = jnp.where(is_sample, rts_ref[...], rtp_ref[...])
    n2 = jnp.where(is_sample, n2s_ref[...], n2p_ref[...])
    pos0, pos1 = _slot_positions(rt)
    pos_ref[...] = jnp.concatenate([pos0, pos1, jnp.zeros((6, MOE_TD), F32)], axis=0)
    used = _tile_rows(tab_ref, t, n_tiles)

    def group(r0, nrows):
        m0, m1 = _one_hot_rows(r0, nrows, pos0, pos1)
        q = (jnp.where(m0, 1.0, 0.0) + jnp.where(m1, 1.0, 0.0)).astype(BF16)
        buf[slot, pl.ds(r0, nrows), :] = jnp.dot(q, n2, preferred_element_type=F32).astype(BF16)

    group(0, 2 * MOE_TD)
    for r0 in range(2 * MOE_TD, MOE_BR, MOE_BLK):
        @pl.when(used > r0)
        def _():
            group(r0, MOE_BLK)

    _segment_copies(tab_ref, t, n_tiles, buf, xs_ref, sems, slot, to_hbm=True, wait=False)

    @pl.when(t == n_tiles - 1)
    def _():
        if n_tiles >= 2:
            _segment_copies(tab_ref, t - 1, n_tiles, buf, xs_ref, sems, 1 - slot, to_hbm=True, wait=True)
        _segment_copies(tab_ref, t, n_tiles, buf, xs_ref, sems, slot, to_hbm=True, wait=True)
        lax.fori_loop(first_free, n_gtiles, fill_wait, 0)


def _dispatch_call(tab, rt_p, n2_p, rt_s, n2_s, r_max):
    n_ptiles = n2_p.shape[0] // MOE_TD
    n_tiles = n_ptiles + n2_s.shape[0] // MOE_TD
    last_p = n_ptiles - 1
    return pl.pallas_call(
        functools.partial(_dispatch_kernel, n_tiles=n_tiles, n_ptiles=n_ptiles,
                          n_gtiles=r_max // MOE_TG),
        grid_spec=pltpu.PrefetchScalarGridSpec(
            num_scalar_prefetch=1, grid=(n_tiles,),
            in_specs=[pl.BlockSpec((8, MOE_TD), lambda t, tab: (0, jnp.minimum(t, last_p))),
                      pl.BlockSpec((MOE_TD, D), lambda t, tab: (jnp.minimum(t, last_p), 0)),
                      pl.BlockSpec((8, MOE_TD), lambda t, tab: (0, 0)),
                      pl.BlockSpec((MOE_TD, D), lambda t, tab: (0, 0))],
            out_specs=(pl.BlockSpec(memory_space=pl.ANY),
                       pl.BlockSpec((None, 8, MOE_TD), lambda t, tab: (t, 0, 0))),
            scratch_shapes=[pltpu.VMEM((2, MOE_BR, D), BF16),
                            pltpu.VMEM((MOE_TG, D), BF16),
                            pltpu.SemaphoreType.DMA((2, NE)),
                            pltpu.SemaphoreType.DMA(())]),
        out_shape=(jax.ShapeDtypeStruct((r_max, D), BF16),
                   jax.ShapeDtypeStruct((n_tiles, 8, MOE_TD), F32)),
        compiler_params=pltpu.CompilerParams(dimension_semantics=("arbitrary",),
                                             vmem_limit_bytes=VMEM_LIMIT),
        name="moe_dispatch",
    )(tab, rt_p, n2_p, rt_s, n2_s)


MOE_CK = 6


def _experts_kernel(ctab_ref, xs_ref, w1_ref, w3_ref, w2_ref, ys_ref,
                    w1b, w3b, w2b, xbuf, ybuf, zbuf, in_sem, out_sem, zsem, *, n_gtiles, nc_max):
    e = pl.program_id(0)
    c0 = ctab_ref[e]
    c1 = ctab_ref[e + 1]
    n_chunks = ctab_ref[NE]
    first_free = ctab_ref[NE + 1 + 2 * nc_max]

    def fill_copy(j):
        d = pl.multiple_of(j * MOE_TG, MOE_TG)
        return pltpu.make_async_copy(zbuf, ys_ref.at[pl.ds(d, MOE_TG)], zsem)

    def fill_start(j, carry):
        fill_copy(j).start()
        return carry

    def fill_wait(j, carry):
        fill_copy(j).wait()
        return carry

    def span(c):
        r = pl.multiple_of(ctab_ref[NE + 1 + c], MOE_TG)
        n = pl.multiple_of(ctab_ref[NE + 1 + nc_max + c] * MOE_TG, MOE_TG)
        return r, n

    def in_copy(c, slot):
        r, n = span(c)
        return pltpu.make_async_copy(xs_ref.at[pl.ds(r, n)], xbuf.at[slot, pl.ds(0, n)], in_sem.at[slot])

    def out_copy(c, slot):
        r, n = span(c)
        return pltpu.make_async_copy(ybuf.at[slot, pl.ds(0, n)], ys_ref.at[pl.ds(r, n)], out_sem.at[slot])

    @pl.when(e == 0)
    def _():
        zbuf[...] = jnp.zeros_like(zbuf)
        lax.fori_loop(first_free, n_gtiles, fill_start, 0)

        @pl.when(n_chunks > 0)
        def _():
            in_copy(0, 0).start()

    def compute(slot, rows):
        x = xbuf[slot, pl.ds(0, rows), :]
        a = jnp.dot(x, w1b[...], preferred_element_type=F32)
        b = jnp.dot(x, w3b[...], preferred_element_type=F32)
        hid = a * _sigmoid(a) * b
        y = jnp.dot(hid.astype(BF16), w2b[...], preferred_element_type=F32)
        ybuf[slot, pl.ds(0, rows), :] = y.astype(BF16)

    @pl.when(c1 > c0)
    def _():
        w1b[...] = w1_ref[...].astype(BF16)
        w3b[...] = w3_ref[...].astype(BF16)
        w2b[...] = w2_ref[...].astype(BF16)

        def chunk(c, carry):
            slot = lax.rem(c, 2)

            @pl.when(c + 1 < n_chunks)
            def _():
                in_copy(c + 1, 1 - slot).start()

            in_copy(c, slot).wait()

            @pl.when(c >= 2)
            def _():
                out_copy(c - 2, slot).wait()

            k = ctab_ref[NE + 1 + nc_max + c]
            for kk in range(1, MOE_CK + 1):
                @pl.when(k == kk)
                def _():
                    compute(slot, kk * MOE_TG)

            out_copy(c, slot).start()
            return carry

        lax.fori_loop(c0, c1, chunk, 0)

    @pl.when(e == NE - 1)
    def _():
        @pl.when(n_chunks >= 2)
        def _():
            out_copy(n_chunks - 2, lax.rem(n_chunks, 2)).wait()

        @pl.when(n_chunks >= 1)
        def _():
            out_copy(n_chunks - 1, lax.rem(n_chunks - 1, 2)).wait()

        lax.fori_loop(first_free, n_gtiles, fill_wait, 0)


def _experts_call(ctab, xs, w1, w3, w2, nc_max):
    r_max = xs.shape[0]
    w_map = lambda e, ctab: (e, 0, 0)
    ring = pltpu.VMEM((2, MOE_CK * MOE_TG, D), BF16)
    return pl.pallas_call(
        functools.partial(_experts_kernel, n_gtiles=r_max // MOE_TG, nc_max=nc_max),
        grid_spec=pltpu.PrefetchScalarGridSpec(
            num_scalar_prefetch=1, grid=(NE,),
            in_specs=[pl.BlockSpec(memory_space=pl.ANY),
                      pl.BlockSpec((None, D, DE), w_map),
                      pl.BlockSpec((None, D, DE), w_map),
                      pl.BlockSpec((None, DE, D), w_map)],
            out_specs=pl.BlockSpec(memory_space=pl.ANY),
            scratch_shapes=[pltpu.VMEM((D, DE), BF16), pltpu.VMEM((D, DE), BF16),
                            pltpu.VMEM((DE, D), BF16), ring, ring,
                            pltpu.VMEM((MOE_TG, D), BF16),
                            pltpu.SemaphoreType.DMA((2,)), pltpu.SemaphoreType.DMA((2,)),
                            pltpu.SemaphoreType.DMA(())]),
        out_shape=jax.ShapeDtypeStruct((r_max, D), BF16),
        compiler_params=pltpu.CompilerParams(dimension_semantics=("arbitrary",),
                                             vmem_limit_bytes=VMEM_LIMIT),
        name="moe_experts",
    )(ctab, xs, w1, w3, w2)


def _combine_kernel(tab_ref, rt_ref, pos_ref, x1_ref, gt2_ref, gf_ref, ys_ref, y_ref, buf, acc, sems,
                    *, n_tiles, t_off):
    i = pl.program_id(0)
    t = i + t_off
    slot = lax.rem(i, 2)

    @pl.when(i == 0)
    def _():
        buf[...] = jnp.zeros_like(buf)
        _segment_copies(tab_ref, t, n_tiles, buf, ys_ref, sems, slot, to_hbm=False, wait=False)

    @pl.when(i + 1 < pl.num_programs(0))
    def _():
        _segment_copies(tab_ref, t + 1, n_tiles, buf, ys_ref, sems, 1 - slot, to_hbm=False, wait=False)

    _segment_copies(tab_ref, t, n_tiles, buf, ys_ref, sems, slot, to_hbm=False, wait=True)

    rt = rt_ref[...]
    pos0 = pos_ref[0:1, :]
    pos1 = pos_ref[1:2, :]
    used = _tile_rows(tab_ref, t, n_tiles)

    def ungroup(r0, nrows):
        m0, m1 = _one_hot_rows(r0, nrows, pos0, pos1)
        q = (jnp.where(m0, 1.0, 0.0) + jnp.where(m1, 1.0, 0.0)).astype(BF16)
        gw = jnp.sum(jnp.where(m0, rt[2:3, :], 0.0) + jnp.where(m1, rt[3:4, :], 0.0),
                     axis=1, keepdims=True)
        yv = (buf[slot, pl.ds(r0, nrows), :].astype(F32) * gw).astype(BF16)
        return lax.dot_general(q, yv, (((0,), (0,)), ((), ())), preferred_element_type=F32)

    acc[...] = ungroup(0, 2 * MOE_TD)
    for r0 in range(2 * MOE_TD, MOE_BR, MOE_BLK):
        @pl.when(used > r0)
        def _():
            acc[...] += ungroup(r0, MOE_BLK)

    xo = x1_ref[...] + gt2_ref[...] * acc[...]
    y_ref[...] = _rms(xo) * gf_ref[...]


def _combine_call(tab, rt, pos, x1, gt2, gt2_spec, g_final, ys, n_tiles, t_off):
    n_out_tiles = x1.shape[0] // MOE_TD
    return pl.pallas_call(
        functools.partial(_combine_kernel, n_tiles=n_tiles, t_off=t_off),
        grid_spec=pltpu.PrefetchScalarGridSpec(
            num_scalar_prefetch=1, grid=(n_out_tiles,),
            in_specs=[pl.BlockSpec((8, MOE_TD), lambda t, tab: (0, t)),
                      pl.BlockSpec((None, 8, MOE_TD), lambda t, tab: (t + t_off, 0, 0)),
                      pl.BlockSpec((MOE_TD, D), lambda t, tab: (t, 0)),
                      gt2_spec,
                      pl.BlockSpec((1, D), lambda t, tab: (0, 0)),
                      pl.BlockSpec(memory_space=pl.ANY)],
            out_specs=pl.BlockSpec((MOE_TD, D), lambda t, tab: (t, 0)),
            scratch_shapes=[pltpu.VMEM((2, MOE_BR, D), BF16),
                            pltpu.VMEM((MOE_TD, D), F32),
                            pltpu.SemaphoreType.DMA((2, NE))]),
        out_shape=jax.ShapeDtypeStruct((n_out_tiles * MOE_TD, D), F32),
        compiler_params=pltpu.CompilerParams(dimension_semantics=("arbitrary",),
                                             vmem_limit_bytes=VMEM_LIMIT),
        name="moe_combine",
    )(tab, rt, pos, x1, gt2, g_final.reshape(1, D), ys)


def _moe_plan(cnt, nc_max):
    cnt8 = (cnt + ROW_ALIGN - 1) // ROW_ALIGN * ROW_ALIGN
    seg_rows = cnt8.sum(axis=0)
    seg_pad = (seg_rows + MOE_TG - 1) // MOE_TG * MOE_TG
    seg_start = jnp.cumsum(seg_pad) - seg_pad
    dst = seg_start[None, :] + jnp.cumsum(cnt8, axis=0) - cnt8
    boff = jnp.cumsum(cnt8, axis=1) - cnt8
    tile_end = jnp.cumsum(seg_pad // MOE_TG)
    n_active = tile_end[-1:].astype(jnp.int32)
    tab = jnp.concatenate([cnt8.ravel(), boff.ravel(), dst.ravel(),
                           seg_start + seg_rows, seg_pad - seg_rows, n_active,
                           cnt8.sum(axis=1)]).astype(jnp.int32)
    nt = seg_pad // MOE_TG
    nfull = nt // MOE_CK
    rem = nt % MOE_CK
    nch = nfull + (rem > 0)
    cend = jnp.cumsum(nch)
    cstart = cend - nch
    c = jnp.arange(nc_max, dtype=jnp.int32)
    ce = jnp.minimum(jnp.sum(c[:, None] >= cend[None, :], axis=1), NE - 1)
    local = c - cstart[ce]
    valid = c < cend[-1]
    ck = jnp.where(valid, jnp.where(local < nfull[ce], MOE_CK, rem[ce]), 0)
    crow = jnp.where(valid, seg_start[ce] + local * (MOE_CK * MOE_TG), 0)
    ctab = jnp.concatenate([cstart, cend[-1:], crow, ck, n_active]).astype(jnp.int32)
    return tab, ctab


def kernel(x_prompt, x_sample, c_prompt, c_sample, state_ssm_re, state_ssm_im, cache_conv, w_ada, b_ada, g_norm_mix, w_in, ssm_a_re, ssm_a_im, ssm_log_dt, ssm_b_re, ssm_b_im, ssm_c_re, ssm_c_im, ssm_d, w_ssm_glu, b_ssm_glu, w_dw, b_dw, ln_conv_g, ln_conv_b, g_out_ssm, g_out_conv, w_out, g_norm_ffn, w_router_grp, b_router_grp, w_router_exp, b_router_exp, w_exp_gate, w_exp_up, w_exp_down, g_final):
    depth = w_ada.shape[0]
    assert depth == 1
    bsz, seq, _ = x_prompt.shape
    nb, nt, _ = x_sample.shape

    n_c = bsz + nb
    c_pad = -n_c % 16
    c_all = jnp.concatenate([c_prompt, c_sample, jnp.zeros((c_pad, D), F32)], axis=0)
    mod = _mod_call(c_all, w_ada[0], b_ada[0])[:n_c]
    mod_p = mod[:bsz].reshape(bsz, 6, D)
    mod_s = mod[bsz:]

    ab_re, ab_im, bb_re, bb_im, c_im_neg = _ssm_prep_call(
        ssm_a_re[0], ssm_a_im[0], ssm_log_dt[0], ssm_b_re[0], ssm_b_im[0], ssm_c_im[0])
    wb, wc = _block_diag_weights(bb_re, bb_im, ssm_c_re[0], c_im_neg)
    a_tok = jnp.stack([ab_re.reshape(NCHUNK, LANES), ab_im.reshape(NCHUNK, LANES)])
    a_row = jnp.stack([ab_re.reshape(NSTATE), ab_im.reshape(NSTATE)])

    w_r = jnp.concatenate([w_router_exp[0].reshape(D, NE), w_router_grp[0],
                           jnp.zeros((D, LANES - NE - NG), F32)], axis=1)
    w_r_hi = w_r.astype(BF16)
    w_r = jnp.concatenate([w_r_hi, (w_r - w_r_hi.astype(F32)).astype(BF16)], axis=1)
    b_r = jnp.concatenate([b_router_exp[0].reshape(NE), b_router_grp[0],
                           jnp.zeros((LANES - NE - NG,), F32)]).reshape(1, LANES)
    w_dw_p = jnp.concatenate([w_dw[0], jnp.zeros((1, CONV_CH), F32)], axis=0)

    row = lambda v: v.reshape(1, -1)
    common_a = (row(g_norm_mix[0]), w_in[0].astype(BF16), wb)
    common_b = (wc, row(ssm_d[0].reshape(SSM_W)), w_ssm_glu[0].astype(BF16), row(b_ssm_glu[0]),
                w_dw_p, row(b_dw[0]), row(ln_conv_g[0]), row(ln_conv_b[0]),
                row(g_out_ssm[0]), row(g_out_conv[0]), w_out[0].astype(BF16),
                row(g_norm_ffn[0]), w_r, b_r)

    n_p = bsz * seq
    n_s = nb * nt
    n_all = n_p + n_s
    assert n_s == MOE_TD and seq % MOE_TD == 0 and MOE_TD % PROMPT_TC == 0
    wts_p = common_a + (a_tok,) + common_b
    x1_p, n2_p, rt_p, cnt_p, hr_p, hi_p, cache_p = _prompt_mixer_call(x_prompt, mod_p, wts_p, PROMPT_TC)

    x_tm = jnp.transpose(x_sample, (1, 0, 2)).reshape(nt * nb, D)
    cache_tm = jnp.transpose(cache_conv[0], (1, 0, 2))
    wts_s = common_a + (a_row,) + common_b
    x1_s, n2_s, rt_s, cnt_s, hr_s, hi_s, glu_s = _sample_mixer_call(
        x_tm, mod_s, state_ssm_re[0].reshape(nb, NSTATE), state_ssm_im[0].reshape(nb, NSTATE),
        cache_tm, wts_s, nb, nt)

    n_ptiles = n_p // MOE_TD
    n_tiles = n_all // MOE_TD
    r_max = -(-(2 * n_all + n_tiles * NE * (ROW_ALIGN - 1) + NE * (MOE_TG - ROW_ALIGN)) // MOE_TG) * MOE_TG
    cnt = jnp.concatenate([cnt_p.reshape(n_ptiles, MOE_TD // PROMPT_TC, LANES).sum(axis=1), cnt_s])
    nc_max = r_max // MOE_TG // MOE_CK + NE
    tab, ctab = _moe_plan(cnt[:, :NE].astype(jnp.int32), nc_max)
    xs, pos = _dispatch_call(tab, rt_p, n2_p, rt_s, n2_s, r_max)
    ys = _experts_call(ctab, xs, w_exp_gate[0], w_exp_up[0], w_exp_down[0], nc_max)
    tiles_per_b = seq // MOE_TD
    gt2_p = mod_p[:, 5:6, :]
    y_p = _combine_call(tab, rt_p, pos, x1_p, gt2_p,
                        pl.BlockSpec((None, 1, D), lambda t, tab: (t // tiles_per_b, 0, 0)),
                        g_final, ys, n_tiles, 0)
    gt2_s = jnp.tile(mod_s[:, 5 * D:], (nt, 1))
    y_s = _combine_call(tab, rt_s, pos, x1_s, gt2_s,
                        pl.BlockSpec((n_s, D), lambda t, tab: (0, 0)),
                        g_final, ys, n_tiles, n_ptiles)

    y_prompt = y_p.reshape(bsz, seq, D)
    y_sample = jnp.transpose(y_s.reshape(nt, nb, D), (1, 0, 2))
    new_cache_s = jnp.concatenate(
        [cache_conv[0][:, nt:, :], jnp.transpose(glu_s.reshape(nt, nb, CONV_CH), (1, 0, 2))], axis=1)
    return (y_prompt, y_sample,
            hr_p.reshape(1, bsz, G, P), hi_p.reshape(1, bsz, G, P), cache_p[None],
            hr_s.reshape(1, nb, G, P), hi_s.reshape(1, nb, G, P), new_cache_s[None])
```

```python
import functools

import jax
import jax.numpy as jnp
import numpy as np
from jax import lax
from jax.experimental import pallas as pl
from jax.experimental.pallas import tpu as pltpu

F32 = jnp.float32
BF16 = jnp.bfloat16

D = 1024
SSM_W = 512
CONV_CH = 512
G = 32
H = 16
P = 64
KW = 31
CB = KW - 1
NE = 32
NG = 4
EPG = 8
DE = 512
EPS = 1e-6
LANES = 128
NSTATE = G * P
NCHUNK = NSTATE // LANES

PROMPT_TC = 512
SCAN_GROUP = 8
VMEM_LIMIT = 56 * 1024 * 1024


def _rms(x):
    return x * lax.rsqrt(jnp.mean(x * x, axis=-1, keepdims=True) + EPS)


def _sigmoid(x):
    return 0.5 * jnp.tanh(0.5 * x) + 0.5


def _gelu_tanh(y):
    c = np.sqrt(2.0 / np.pi).astype(np.float32)
    return y * (0.5 * (1.0 + jnp.tanh(c * (y + 0.044715 * (y * y * y)))))


def _bdot(a, b):
    return jnp.dot(a.astype(BF16), b, preferred_element_type=F32)


def _mod_kernel(c_ref, w_ref, b_ref, op_ref, os_ref):
    c = c_ref[...]
    s = c * _sigmoid(c)
    n = s.shape[0]
    s_hi = s.astype(BF16)
    lhs = jnp.concatenate([s_hi, (s - s_hi.astype(F32)).astype(BF16)], axis=0)
    w = w_ref[...]
    w_hi = w.astype(BF16)
    w_lo = (w - w_hi.astype(F32)).astype(BF16)
    p_hi = jnp.dot(lhs, w_hi, preferred_element_type=F32)
    p_lo = jnp.dot(lhs, w_lo, preferred_element_type=F32)
    res = (p_hi[:n] + p_lo[:n]) + (p_hi[n:] + p_lo[n:]) + b_ref[...]
    n_p = op_ref.shape[0]
    op_ref[...] = res[:n_p]
    os_ref[...] = res[n_p:n_p + os_ref.shape[0]]


def _mod_call(c_all, w_ada, b_ada, n_p, n_s):
    n = c_all.shape[0]
    tn = 512
    return pl.pallas_call(
        _mod_kernel,
        grid=(6 * D // tn,),
        in_specs=[pl.BlockSpec((n, D), lambda j: (0, 0)),
                  pl.BlockSpec((D, tn), lambda j: (0, j)),
                  pl.BlockSpec((1, tn), lambda j: (0, j))],
        out_specs=(pl.BlockSpec((n_p, tn), lambda j: (0, j)),
                   pl.BlockSpec((n_s, tn), lambda j: (0, j))),
        out_shape=(jax.ShapeDtypeStruct((n_p, 6 * D), F32),
                   jax.ShapeDtypeStruct((n_s, 6 * D), F32)),
        compiler_params=pltpu.CompilerParams(dimension_semantics=("arbitrary",)),
        name="mod",
    )(c_all, w_ada, b_ada.reshape(1, 6 * D))


def _ssm_prep_kernel(a_re, a_im, log_dt, b_re, b_im, c_im,
                     ab_re_o, ab_im_o, bb_re_o, bb_im_o, cneg_o):
    lam_re = jnp.minimum(a_re[...], -1e-4)
    lam_im = a_im[...]
    dt = jnp.exp(log_dt[...])
    mag = jnp.exp(lam_re * dt)
    ab_re = mag * jnp.cos(lam_im * dt)
    ab_im = mag * jnp.sin(lam_im * dt)
    den = lam_re * lam_re + lam_im * lam_im
    num_re = ab_re - 1.0
    coef_re = (num_re * lam_re + ab_im * lam_im) / den
    coef_im = (ab_im * lam_re - num_re * lam_im) / den
    ab_re_o[...] = ab_re
    ab_im_o[...] = ab_im
    br = b_re[...]
    bi = b_im[...]
    bb_re_o[...] = coef_re * br - coef_im * bi
    bb_im_o[...] = coef_re * bi + coef_im * br
    cneg_o[...] = -c_im[...]


def _ssm_prep_call(a_re, a_im, log_dt, b_re, b_im, c_im):
    flat = lambda v: v.reshape(1, NSTATE)
    b_hs = lambda v: jnp.transpose(v, (2, 0, 1)).reshape(H, NSTATE)
    dt_row = jnp.broadcast_to(log_dt[:, None], (G, P)).reshape(1, NSTATE)
    ab_re, ab_im, bb_re, bb_im, cneg = pl.pallas_call(
        _ssm_prep_kernel,
        out_shape=(jax.ShapeDtypeStruct((1, NSTATE), F32), jax.ShapeDtypeStruct((1, NSTATE), F32),
                   jax.ShapeDtypeStruct((H, NSTATE), F32), jax.ShapeDtypeStruct((H, NSTATE), F32),
                   jax.ShapeDtypeStruct((G * H, P), F32)),
        name="ssm_prep",
    )(flat(a_re), flat(a_im), dt_row, b_hs(b_re), b_hs(b_im), c_im.reshape(G * H, P))
    ghp = lambda v: jnp.transpose(v.reshape(H, G, P), (1, 0, 2))
    return (ab_re.reshape(G, P), ab_im.reshape(G, P), ghp(bb_re), ghp(bb_im), cneg.reshape(G, H, P))


def _block_diag_weights(bb_re, bb_im, c_re, c_im_neg):
    eye8 = jnp.eye(8, dtype=F32)
    eye4 = jnp.eye(4, dtype=F32)
    eye2 = jnp.eye(2, dtype=F32)

    def wb_part(bb):
        x = bb.reshape(4, 8, H, P)
        return jnp.einsum('qghp,gk->qghkp', x, eye8).reshape(4, 8 * H, 8 * P)

    wb = jnp.concatenate([wb_part(bb_re), wb_part(bb_im)], axis=-1).astype(BF16)

    def wc_part(c):
        x = c.reshape(4, 4, 2, H, P)
        y = jnp.einsum('qjghp,jk,gl->qjgpklh', x, eye4, eye2)
        return y.reshape(NCHUNK, 2 * P, 4 * 2 * H)

    wc = jnp.concatenate([wc_part(c_re), wc_part(c_im_neg)], axis=1).astype(BF16)
    return wb, wc


def _front(x, sc1, sh1, g_mix, w_in_ref):
    n = _rms(x) * g_mix * (1.0 + sc1) + sh1
    proj = _bdot(n, w_in_ref[...])
    u = proj[:, :SSM_W]
    glu = proj[:, SSM_W:SSM_W + CONV_CH] * _sigmoid(proj[:, SSM_W + CONV_CH:])
    return u, glu


def _ssm_out(y_lin, u, d_skip, w_glu_ref, b_glu, g_out_ssm):
    y = _gelu_tanh(y_lin + d_skip * u)
    ys = y * _sigmoid(_bdot(y, w_glu_ref[...]) + b_glu)
    return _rms(ys) * g_out_ssm


def _conv_out(conv, b_dw, ln_g, ln_b, g_out_conv):
    c = conv + b_dw
    mu = jnp.mean(c, axis=-1, keepdims=True)
    cc = c - mu
    var = jnp.mean(cc * cc, axis=-1, keepdims=True)
    ln = cc * lax.rsqrt(var + EPS) * ln_g + ln_b
    yc = ln * _sigmoid(ln)
    return _rms(yc) * g_out_conv


def _route(n2, w_r_ref, b_r):
    rows = n2.shape[0]
    n_hi = n2.astype(BF16)
    n_lo = (n2 - n_hi.astype(F32)).astype(BF16)
    parts = jnp.dot(jnp.concatenate([n_hi, n_lo], axis=0), w_r_ref[...], preferred_element_type=F32)
    lg = (parts[:rows, :LANES] + parts[:rows, LANES:]) + (parts[rows:, :LANES] + parts[rows:, LANES:]) + b_r
    lane = lax.broadcasted_iota(jnp.int32, (rows, LANES), 1).astype(F32)
    ninf = -jnp.inf
    big = 1e9
    gmask = jnp.logical_and(lane >= NE, lane < NE + NG)
    gl = jnp.where(gmask, lg, ninf)
    gmax = jnp.max(gl, axis=-1, keepdims=True)
    gsum = jnp.sum(jnp.where(gmask, jnp.exp(gl - gmax), 0.0), axis=-1, keepdims=True)
    p_top = 1.0 / gsum
    gi = jnp.min(jnp.where(gl == gmax, lane, big), axis=-1, keepdims=True) - NE
    lo = gi * EPG
    emask = jnp.logical_and(lane >= lo, lane < lo + EPG)
    el = jnp.where(emask, lg, ninf)
    m1 = jnp.max(el, axis=-1, keepdims=True)
    i1 = jnp.min(jnp.where(el == m1, lane, big), axis=-1, keepdims=True)
    el2 = jnp.where(lane == i1, ninf, el)
    m2 = jnp.max(el2, axis=-1, keepdims=True)
    i2 = jnp.min(jnp.where(el2 == m2, lane, big), axis=-1, keepdims=True)
    e2 = jnp.exp(m2 - m1)
    den = 1.0 + e2
    w1 = p_top / den
    w2 = p_top * e2 / den
    cnt = jnp.sum(jnp.where(lane == i1, 1.0, 0.0) + jnp.where(lane == i2, 1.0, 0.0),
                  axis=0, keepdims=True)
    cols = (jnp.where(lane == 0.0, i1, 0.0) + jnp.where(lane == 1.0, i2, 0.0)
            + jnp.where(lane == 2.0, w1, 0.0) + jnp.where(lane == 3.0, w2, 0.0))
    return cols.T[0:8, :], cnt


def _tail(x, ns, nc, gt1, sc2, sh2, w_out_ref, g_ffn, w_r_ref, b_r):
    merged = _bdot(jnp.concatenate([ns, nc], axis=-1), w_out_ref[...])
    x1 = x + gt1 * merged
    n2 = _rms(x1) * g_ffn * (1.0 + sc2) + sh2
    rt, cnt = _route(n2, w_r_ref, b_r)
    return x1, n2, rt, cnt


def _prompt_mixer_kernel(x_ref, mod_ref, g_mix_ref, w_in_ref, wb_ref, a_ref, wc_ref, dsk_ref,
                         w_glu_ref, b_glu_ref, w_dw_ref, b_dw_ref, ln_g_ref, ln_b_ref,
                         g_os_ref, g_oc_ref, w_out_ref, g_ffn_ref, w_r_ref, b_r_ref,
                         x1_ref, n2_ref, rt_ref, cnt_ref, hr_ref, hi_ref, cache_ref,
                         sre, sim, hst, ebuf, eshift, ncbuf, *, tc, pt):
    c = pl.program_id(1)
    nc_chunks = pl.num_programs(1)

    @pl.when(c == 0)
    def _():
        hst[...] = jnp.zeros_like(hst)
        ebuf[pl.ds(0, 32), :] = jnp.zeros((32, CONV_CH), F32)

    x = x_ref[...]
    mod = mod_ref[...]
    sh1, sc1, gt1 = mod[0:1], mod[1:2], mod[2:3]
    sh2, sc2 = mod[3:4], mod[4:5]

    u, glu = _front(x, sc1, sh1, g_mix_ref[...], w_in_ref)
    ebuf[pl.ds(32, tc), :] = glu

    for s in range(1, 8):
        eshift[s - 1, pl.ds(0, tc + 24), :] = ebuf[pl.ds(s, tc + 24), :]
    rb = 64
    convs = []
    for r0 in range(0, tc, rb):
        acc = None
        for k in range(KW):
            a8, s = (k + 2) // 8 * 8, (k + 2) % 8
            win = ebuf[pl.ds(r0 + a8, rb), :] if s == 0 else eshift[s - 1, pl.ds(r0 + a8, rb), :]
            term = w_dw_ref[k:k + 1, :] * win
            acc = term if acc is None else acc + term
        convs.append(acc)
    conv = jnp.concatenate(convs, axis=0)
    ncbuf[...] = _conv_out(conv, b_dw_ref[...], ln_g_ref[...], ln_b_ref[...], g_oc_ref[...])

    @pl.when(c == nc_chunks - 1)
    def _():
        cache_ref[...] = ebuf[pl.ds(tc + 2, CB), :]

    ebuf[pl.ds(0, 32), :] = ebuf[pl.ds(tc, 32), :]

    ub = u.astype(BF16)
    for q in range(4):
        r = jnp.dot(ub[:, q * LANES:(q + 1) * LANES], wb_ref[q], preferred_element_type=F32)
        for k in range(4):
            j = 4 * q + k
            sre[pl.ds(j * pt, tc), :] = r[:, k * LANES:(k + 1) * LANES]
            sim[pl.ds(j * pt, tc), :] = r[:, SSM_W + k * LANES:SSM_W + (k + 1) * LANES]

    ar0, ar1 = a_ref[0, 0:8, :], a_ref[0, 8:16, :]
    ai0, ai1 = a_ref[1, 0:8, :], a_ref[1, 8:16, :]

    def step(g, carry):
        hr0, hr1, hi0, hi1 = carry
        t0 = g * SCAN_GROUP
        idx = [(pl.ds(t0 + u, 8, stride=pt), pl.ds(t0 + u + 8 * pt, 8, stride=pt))
               for u in range(SCAN_GROUP)]
        bu = [(sre[i0, :], sre[i1, :], sim[i0, :], sim[i1, :]) for (i0, i1) in idx]
        hs = []
        for br0, br1, bi0, bi1 in bu:
            nr0 = ar0 * hr0 - ai0 * hi0 + br0
            ni0 = ar0 * hi0 + ai0 * hr0 + bi0
            nr1 = ar1 * hr1 - ai1 * hi1 + br1
            ni1 = ar1 * hi1 + ai1 * hr1 + bi1
            hr0, hr1, hi0, hi1 = nr0, nr1, ni0, ni1
            hs.append((nr0, nr1, ni0, ni1))
        for (i0, i1), (nr0, nr1, ni0, ni1) in zip(idx, hs):
            sre[i0, :] = nr0
            sim[i0, :] = ni0
            sre[i1, :] = nr1
            sim[i1, :] = ni1
        return hr0, hr1, hi0, hi1

    init = (hst[0, 0:8, :], hst[0, 8:16, :], hst[1, 0:8, :], hst[1, 8:16, :])
    hr0, hr1, hi0, hi1 = lax.fori_loop(0, tc // SCAN_GROUP, step, init)
    hst[0, 0:8, :] = hr0
    hst[0, 8:16, :] = hr1
    hst[1, 0:8, :] = hi0
    hst[1, 8:16, :] = hi1

    @pl.when(c == nc_chunks - 1)
    def _():
        hr_ref[...] = hst[0]
        hi_ref[...] = hst[1]

    ys = []
    for q in range(4):
        acc = None
        for jj in range(4):
            j = 4 * q + jj
            lhs = jnp.concatenate([sre[pl.ds(j * pt, tc), :], sim[pl.ds(j * pt, tc), :]], axis=-1)
            d = jnp.dot(lhs.astype(BF16), wc_ref[j], preferred_element_type=F32)
            acc = d if acc is None else acc + d
        ys.append(acc)
    y_lin = jnp.concatenate(ys, axis=-1)
    ns = _ssm_out(y_lin, u, dsk_ref[...], w_glu_ref, b_glu_ref[...], g_os_ref[...])
    nc = ncbuf[...]

    x1, n2, rt, cnt = _tail(x, ns, nc, gt1, sc2, sh2, w_out_ref, g_ffn_ref[...], w_r_ref, b_r_ref[...])
    x1_ref[...] = x1
    n2_ref[...] = n2.astype(BF16)
    rt_ref[...] = rt
    cnt_ref[...] = cnt


def _const_spec(shape):
    nd = len(shape)
    return pl.BlockSpec(shape, lambda b, c: (0,) * nd)


def _prompt_mixer_call(x, mod6, wts, tc):
    bsz, t, _ = x.shape
    n_all = bsz * t
    pt = tc + 8
    nc = t // tc
    kern = functools.partial(_prompt_mixer_kernel, tc=tc, pt=pt)
    in_specs = [pl.BlockSpec((None, tc, D), lambda b, c: (b, c, 0)),
                pl.BlockSpec((None, 6, D), lambda b, c: (b, 0, 0))]
    in_specs += [_const_spec(w.shape) for w in wts]
    out_shape = (jax.ShapeDtypeStruct((n_all, D), F32),
                 jax.ShapeDtypeStruct((n_all, D), BF16),
                 jax.ShapeDtypeStruct((8, n_all), F32),
                 jax.ShapeDtypeStruct((n_all // tc, 1, LANES), F32),
                 jax.ShapeDtypeStruct((bsz, NCHUNK, LANES), F32),
                 jax.ShapeDtypeStruct((bsz, NCHUNK, LANES), F32),
                 jax.ShapeDtypeStruct((bsz, CB, CONV_CH), F32))
    out_specs = (pl.BlockSpec((tc, D), lambda b, c: (b * nc + c, 0)),
                 pl.BlockSpec((tc, D), lambda b, c: (b * nc + c, 0)),
                 pl.BlockSpec((8, tc), lambda b, c: (0, b * nc + c)),
                 pl.BlockSpec((None, 1, LANES), lambda b, c: (b * nc + c, 0, 0)),
                 pl.BlockSpec((None, NCHUNK, LANES), lambda b, c: (b, 0, 0)),
                 pl.BlockSpec((None, NCHUNK, LANES), lambda b, c: (b, 0, 0)),
                 pl.BlockSpec((None, CB, CONV_CH), lambda b, c: (b, 0, 0)))
    scratch = [pltpu.VMEM((NCHUNK * pt, LANES), F32),
               pltpu.VMEM((NCHUNK * pt, LANES), F32),
               pltpu.VMEM((2, NCHUNK, LANES), F32),
               pltpu.VMEM((tc + 32, CONV_CH), F32),
               pltpu.VMEM((7, tc + 32, CONV_CH), F32),
               pltpu.VMEM((tc, CONV_CH), F32)]
    return pl.pallas_call(
        kern, grid=(bsz, nc), in_specs=in_specs, out_specs=out_specs, out_shape=out_shape,
        scratch_shapes=scratch,
        compiler_params=pltpu.CompilerParams(dimension_semantics=("arbitrary", "arbitrary"),
                                             vmem_limit_bytes=VMEM_LIMIT),
        name="prompt_mixer",
    )(x, mod6, *wts)


def _sample_mixer_kernel(x_ref, mod_ref, h0r_ref, h0i_ref, cache_ref,
                         g_mix_ref, w_in_ref, wb_ref, a_ref, wc_ref, dsk_ref,
                         w_glu_ref, b_glu_ref, w_dw_ref, b_dw_ref, ln_g_ref, ln_b_ref,
                         g_os_ref, g_oc_ref, w_out_ref, g_ffn_ref, w_r_ref, b_r_ref,
                         x1_ref, n2_ref, rt_ref, cnt_ref, hr_ref, hi_ref, glu_ref,
                         sre, sim, *, nb, nt):
    x = x_ref[...]

    def rows(i):
        m = mod_ref[:, i * D:(i + 1) * D]
        return jnp.concatenate([m] * nt, axis=0)

    sh1, sc1, gt1, sh2, sc2 = rows(0), rows(1), rows(2), rows(3), rows(4)
    u, glu = _front(x, sc1, sh1, g_mix_ref[...], w_in_ref)
    glu_ref[...] = glu

    ub = u.astype(BF16)
    for q in range(4):
        r = jnp.dot(ub[:, q * LANES:(q + 1) * LANES], wb_ref[q], preferred_element_type=F32)
        sre[:, q * SSM_W:(q + 1) * SSM_W] = r[:, :SSM_W]
        sim[:, q * SSM_W:(q + 1) * SSM_W] = r[:, SSM_W:]

    ar = a_ref[0:1, :]
    ai = a_ref[1:2, :]
    hr = h0r_ref[...]
    hi = h0i_ref[...]
    for t in range(nt):
        rs = pl.ds(t * nb, nb)
        nr = ar * hr - ai * hi + sre[rs, :]
        ni = ar * hi + ai * hr + sim[rs, :]
        sre[rs, :] = nr
        sim[rs, :] = ni
        hr, hi = nr, ni
    hr_ref[...] = hr
    hi_ref[...] = hi

    ys = []
    for q in range(4):
        acc = None
        for jj in range(4):
            j = 4 * q + jj
            lhs = jnp.concatenate([sre[:, j * LANES:(j + 1) * LANES],
                                   sim[:, j * LANES:(j + 1) * LANES]], axis=-1)
            d = jnp.dot(lhs.astype(BF16), wc_ref[j], preferred_element_type=F32)
            acc = d if acc is None else acc + d
        ys.append(acc)
    y_lin = jnp.concatenate(ys, axis=-1)
    ns = _ssm_out(y_lin, u, dsk_ref[...], w_glu_ref, b_glu_ref[...], g_os_ref[...])

    def ext(jrow):
        if jrow < CB:
            return cache_ref[jrow]
        return glu[(jrow - CB) * nb:(jrow - CB + 1) * nb, :]

    convs = []
    for t in range(nt):
        acc = None
        for k in range(KW):
            term = w_dw_ref[k:k + 1, :] * ext(t + k)
            acc = term if acc is None else acc + term
        convs.append(acc)
    conv = jnp.concatenate(convs, axis=0)
    nc = _conv_out(conv, b_dw_ref[...], ln_g_ref[...], ln_b_ref[...], g_oc_ref[...])

    x1, n2, rt, cnt = _tail(x, ns, nc, gt1, sc2, sh2, w_out_ref, g_ffn_ref[...], w_r_ref, b_r_ref[...])
    x1_ref[...] = x1
    n2_ref[...] = n2.astype(BF16)
    rt_ref[...] = rt
    cnt_ref[...] = cnt


def _sample_mixer_call(x_tm, mod_s, h0r, h0i, cache_tm, wts, nb, nt):
    n = nb * nt
    kern = functools.partial(_sample_mixer_kernel, nb=nb, nt=nt)
    out_shape = (jax.ShapeDtypeStruct((n, D), F32),
                 jax.ShapeDtypeStruct((n, D), BF16),
                 jax.ShapeDtypeStruct((8, n), F32),
                 jax.ShapeDtypeStruct((1, LANES), F32),
                 jax.ShapeDtypeStruct((nb, NSTATE), F32),
                 jax.ShapeDtypeStruct((nb, NSTATE), F32),
                 jax.ShapeDtypeStruct((n, CONV_CH), F32))
    scratch = [pltpu.VMEM((n, NSTATE), F32), pltpu.VMEM((n, NSTATE), F32)]
    return pl.pallas_call(
        kern, out_shape=out_shape, scratch_shapes=scratch,
        compiler_params=pltpu.CompilerParams(vmem_limit_bytes=VMEM_LIMIT),
        name="sample_mixer",
    )(x_tm, mod_s, h0r, h0i, cache_tm, *wts)


ROW_ALIGN = 16
MOE_TD = 512
MOE_BR = MOE_TD * 2 + NE * ROW_ALIGN
MOE_TG = 256


def _slot_positions(rt):
    t = rt.shape[1]
    e0 = rt[0:1, :]
    e1 = rt[1:2, :]
    sub = lax.broadcasted_iota(jnp.int32, (LANES, t), 0).astype(F32)
    a0 = jnp.where(sub == e0, 1.0, 0.0)
    a1 = jnp.where(sub == e1, 1.0, 0.0)
    at = a0 + a1
    r = lax.broadcasted_iota(jnp.int32, (t, t), 0)
    c = lax.broadcasted_iota(jnp.int32, (t, t), 1)
    before = jnp.where(r < c, 1.0, 0.0).astype(BF16)
    rank = jnp.dot(at.astype(BF16), before, preferred_element_type=F32)
    cnt = jnp.sum(at, axis=1, keepdims=True)
    cnt_al = jnp.ceil(cnt * (1.0 / ROW_ALIGN)) * float(ROW_ALIGN)
    er = lax.broadcasted_iota(jnp.int32, (LANES, LANES), 0)
    ec = lax.broadcasted_iota(jnp.int32, (LANES, LANES), 1)
    lower = jnp.where(ec < er, 1.0, 0.0)
    base = jnp.dot(lower, jnp.broadcast_to(cnt_al, (LANES, LANES)), preferred_element_type=F32,
                   precision=lax.Precision.HIGHEST)[:, 0:1]
    slot = rank + base
    pos0 = jnp.sum(a0 * slot, axis=0, keepdims=True)
    pos1 = jnp.sum(a1 * slot, axis=0, keepdims=True)
    return pos0, pos1


def _segment_copies(tab_ref, t, n_tiles, buf, hbm, sems, slot, to_hbm, wait):
    for e in range(NE):
        n = pl.multiple_of(tab_ref[t * NE + e], ROW_ALIGN)
        b = pl.multiple_of(tab_ref[(n_tiles + t) * NE + e], ROW_ALIGN)
        d = pl.multiple_of(tab_ref[(2 * n_tiles + t) * NE + e], ROW_ALIGN)
        vm = buf.at[slot, pl.ds(b, n)]
        hb = hbm.at[pl.ds(d, n)]
        cp = pltpu.make_async_copy(vm, hb, sems.at[slot, e]) if to_hbm else \
            pltpu.make_async_copy(hb, vm, sems.at[slot, e])

        @pl.when(n > 0)
        def _():
            if wait:
                cp.wait()
            else:
                cp.start()


def _tile_rows(tab_ref, t, n_tiles):
    return tab_ref[3 * n_tiles * NE + 2 * NE + 1 + t]


def _one_hot_rows(r0, nrows, pos0, pos1):
    row = (lax.broadcasted_iota(jnp.int32, (nrows, MOE_TD), 0) + r0).astype(F32)
    return row == pos0, row == pos1


MOE_BLK = 256


def _dispatch_kernel(tab_ref, rtp_ref, n2p_ref, rts_ref, n2s_ref, xs_ref, pos_ref, buf, zbuf, sems, zsem,
                     *, n_tiles, n_ptiles, n_gtiles):
    t = pl.program_id(0)
    slot = lax.rem(t, 2)
    first_free = tab_ref[3 * n_tiles * NE + 2 * NE]

    def fill_copy(j):
        d = pl.multiple_of(j * MOE_TG, MOE_TG)
        return pltpu.make_async_copy(zbuf, xs_ref.at[pl.ds(d, MOE_TG)], zsem)

    def fill_start(j, carry):
        fill_copy(j).start()
        return carry

    def fill_wait(j, carry):
        fill_copy(j).wait()
        return carry

    @pl.when(t == 0)
    def _():
        zbuf[...] = jnp.zeros_like(zbuf)
        for phase in range(2):
            for e in range(NE):
                d = pl.multiple_of(tab_ref[3 * n_tiles * NE + e], ROW_ALIGN)
                n = pl.multiple_of(tab_ref[3 * n_tiles * NE + NE + e], ROW_ALIGN)
                cp = pltpu.make_async_copy(zbuf.at[pl.ds(0, n)], xs_ref.at[pl.ds(d, n)], sems.at[1, e])

                @pl.when(n > 0)
                def _():
                    if phase == 0:
                        cp.start()
                    else:
                        cp.wait()

        lax.fori_loop(first_free, n_gtiles, fill_start, 0)

    @pl.when(t >= 2)
    def _():
        _segment_copies(tab_ref, t - 2, n_tiles, buf, xs_ref, sems, slot, to_hbm=True, wait=True)

    is_sample = t >= n_ptiles
    rt = jnp.where(is_sample, rts_ref[...], rtp_ref[...])
    n2 = jnp.where(is_sample, n2s_ref[...], n2p_ref[...])
    pos0, pos1 = _slot_positions(rt)
    pos_ref[...] = jnp.concatenate([pos0, pos1, jnp.zeros((6, MOE_TD), F32)], axis=0)
    used = _tile_rows(tab_ref, t, n_tiles)

    def group(r0, nrows):
        m0, m1 = _one_hot_rows(r0, nrows, pos0, pos1)
        q = (jnp.where(m0, 1.0, 0.0) + jnp.where(m1, 1.0, 0.0)).astype(BF16)
        buf[slot, pl.ds(r0, nrows), :] = jnp.dot(q, n2, preferred_element_type=F32).astype(BF16)

    group(0, 2 * MOE_TD)
    for r0 in range(2 * MOE_TD, MOE_BR, MOE_BLK):
        @pl.when(used > r0)
        def _():
            group(r0, MOE_BLK)

    _segment_copies(tab_ref, t, n_tiles, buf, xs_ref, sems, slot, to_hbm=True, wait=False)

    @pl.when(t == n_tiles - 1)
    def _():
        if n_tiles >= 2:
            _segment_copies(tab_ref, t - 1, n_tiles, buf, xs_ref, sems, 1 - slot, to_hbm=True, wait=True)
        _segment_copies(tab_ref, t, n_tiles, buf, xs_ref, sems, slot, to_hbm=True, wait=True)
        lax.fori_loop(first_free, n_gtiles, fill_wait, 0)


def _dispatch_call(tab, rt_p, n2_p, rt_s, n2_s, r_max):
    n_ptiles = n2_p.shape[0] // MOE_TD
    n_tiles = n_ptiles + n2_s.shape[0] // MOE_TD
    last_p = n_ptiles - 1
    return pl.pallas_call(
        functools.partial(_dispatch_kernel, n_tiles=n_tiles, n_ptiles=n_ptiles,
                          n_gtiles=r_max // MOE_TG),
        grid_spec=pltpu.PrefetchScalarGridSpec(
            num_scalar_prefetch=1, grid=(n_tiles,),
            in_specs=[pl.BlockSpec((8, MOE_TD), lambda t, tab: (0, jnp.minimum(t, last_p))),
                      pl.BlockSpec((MOE_TD, D), lambda t, tab: (jnp.minimum(t, last_p), 0)),
                      pl.BlockSpec((8, MOE_TD), lambda t, tab: (0, 0)),
                      pl.BlockSpec((MOE_TD, D), lambda t, tab: (0, 0))],
            out_specs=(pl.BlockSpec(memory_space=pl.ANY),
                       pl.BlockSpec((None, 8, MOE_TD), lambda t, tab: (t, 0, 0))),
            scratch_shapes=[pltpu.VMEM((2, MOE_BR, D), BF16),
                            pltpu.VMEM((MOE_TG, D), BF16),
                            pltpu.SemaphoreType.DMA((2, NE)),
                            pltpu.SemaphoreType.DMA(())]),
        out_shape=(jax.ShapeDtypeStruct((r_max, D), BF16),
                   jax.ShapeDtypeStruct((n_tiles, 8, MOE_TD), F32)),
        compiler_params=pltpu.CompilerParams(dimension_semantics=("arbitrary",),
                                             vmem_limit_bytes=VMEM_LIMIT),
        name="moe_dispatch",
    )(tab, rt_p, n2_p, rt_s, n2_s)


MOE_CK = 6


def _experts_kernel(ctab_ref, xs_ref, w1_ref, w3_ref, w2_ref, ys_ref,
                    w1b, w3b, w2b, xbuf, ybuf, zbuf, in_sem, out_sem, zsem, *, n_gtiles, nc_max):
    e = pl.program_id(0)
    c0 = ctab_ref[e]
    c1 = ctab_ref[e + 1]
    n_chunks = ctab_ref[NE]
    first_free = ctab_ref[NE + 1 + 2 * nc_max]

    def fill_copy(j):
        d = pl.multiple_of(j * MOE_TG, MOE_TG)
        return pltpu.make_async_copy(zbuf, ys_ref.at[pl.ds(d, MOE_TG)], zsem)

    def fill_start(j, carry):
        fill_copy(j).start()
        return carry

    def fill_wait(j, carry):
        fill_copy(j).wait()
        return carry

    def span(c):
        r = pl.multiple_of(ctab_ref[NE + 1 + c], MOE_TG)
        n = pl.multiple_of(ctab_ref[NE + 1 + nc_max + c] * MOE_TG, MOE_TG)
        return r, n

    def in_copy(c, slot):
        r, n = span(c)
        return pltpu.make_async_copy(xs_ref.at[pl.ds(r, n)], xbuf.at[slot, pl.ds(0, n)], in_sem.at[slot])

    def out_copy(c, slot):
        r, n = span(c)
        return pltpu.make_async_copy(ybuf.at[slot, pl.ds(0, n)], ys_ref.at[pl.ds(r, n)], out_sem.at[slot])

    @pl.when(e == 0)
    def _():
        zbuf[...] = jnp.zeros_like(zbuf)
        lax.fori_loop(first_free, n_gtiles, fill_start, 0)

        @pl.when(n_chunks > 0)
        def _():
            in_copy(0, 0).start()

    def compute(slot, rows):
        x = xbuf[slot, pl.ds(0, rows), :]
        a = jnp.dot(x, w1b[...], preferred_element_type=F32)
        b = jnp.dot(x, w3b[...], preferred_element_type=F32)
        hid = a * _sigmoid(a) * b
        y = jnp.dot(hid.astype(BF16), w2b[...], preferred_element_type=F32)
        ybuf[slot, pl.ds(0, rows), :] = y.astype(BF16)

    @pl.when(c1 > c0)
    def _():
        w1b[...] = w1_ref[...].astype(BF16)
        w3b[...] = w3_ref[...].astype(BF16)
        w2b[...] = w2_ref[...].astype(BF16)

        def chunk(c, carry):
            slot = lax.rem(c, 2)

            @pl.when(c + 1 < n_chunks)
            def _():
                in_copy(c + 1, 1 - slot).start()

            in_copy(c, slot).wait()

            @pl.when(c >= 2)
            def _():
                out_copy(c - 2, slot).wait()

            k = ctab_ref[NE + 1 + nc_max + c]
            for kk in range(1, MOE_CK + 1):
                @pl.when(k == kk)
                def _():
                    compute(slot, kk * MOE_TG)

            out_copy(c, slot).start()
            return carry

        lax.fori_loop(c0, c1, chunk, 0)

    @pl.when(e == NE - 1)
    def _():
        @pl.when(n_chunks >= 2)
        def _():
            out_copy(n_chunks - 2, lax.rem(n_chunks, 2)).wait()

        @pl.when(n_chunks >= 1)
        def _():
            out_copy(n_chunks - 1, lax.rem(n_chunks - 1, 2)).wait()

        lax.fori_loop(first_free, n_gtiles, fill_wait, 0)


def _experts_call(ctab, xs, w1, w3, w2, nc_max):
    r_max = xs.shape[0]
    w_map = lambda e, ctab: (e, 0, 0)
    ring = pltpu.VMEM((2, MOE_CK * MOE_TG, D), BF16)
    return pl.pallas_call(
        functools.partial(_experts_kernel, n_gtiles=r_max // MOE_TG, nc_max=nc_max),
        grid_spec=pltpu.PrefetchScalarGridSpec(
            num_scalar_prefetch=1, grid=(NE,),
            in_specs=[pl.BlockSpec(memory_space=pl.ANY),
                      pl.BlockSpec((None, D, DE), w_map),
                      pl.BlockSpec((None, D, DE), w_map),
                      pl.BlockSpec((None, DE, D), w_map)],
            out_specs=pl.BlockSpec(memory_space=pl.ANY),
            scratch_shapes=[pltpu.VMEM((D, DE), BF16), pltpu.VMEM((D, DE), BF16),
                            pltpu.VMEM((DE, D), BF16), ring, ring,
                            pltpu.VMEM((MOE_TG, D), BF16),
                            pltpu.SemaphoreType.DMA((2,)), pltpu.SemaphoreType.DMA((2,)),
                            pltpu.SemaphoreType.DMA(())]),
        out_shape=jax.ShapeDtypeStruct((r_max, D), BF16),
        compiler_params=pltpu.CompilerParams(dimension_semantics=("arbitrary",),
                                             vmem_limit_bytes=VMEM_LIMIT),
        name="moe_experts",
    )(ctab, xs, w1, w3, w2)


def _combine_kernel(tab_ref, rt_ref, pos_ref, x1_ref, gt2_ref, gf_ref, ys_ref, y_ref, buf, acc, sems,
                    *, n_tiles, t_off):
    i = pl.program_id(0)
    t = i + t_off
    slot = lax.rem(i, 2)

    @pl.when(i == 0)
    def _():
        buf[...] = jnp.zeros_like(buf)
        _segment_copies(tab_ref, t, n_tiles, buf, ys_ref, sems, slot, to_hbm=False, wait=False)

    @pl.when(i + 1 < pl.num_programs(0))
    def _():
        _segment_copies(tab_ref, t + 1, n_tiles, buf, ys_ref, sems, 1 - slot, to_hbm=False, wait=False)

    _segment_copies(tab_ref, t, n_tiles, buf, ys_ref, sems, slot, to_hbm=False, wait=True)

    rt = rt_ref[...]
    pos0 = pos_ref[0:1, :]
    pos1 = pos_ref[1:2, :]
    used = _tile_rows(tab_ref, t, n_tiles)

    def ungroup(r0, nrows):
        m0, m1 = _one_hot_rows(r0, nrows, pos0, pos1)
        q = (jnp.where(m0, 1.0, 0.0) + jnp.where(m1, 1.0, 0.0)).astype(BF16)
        gw = jnp.sum(jnp.where(m0, rt[2:3, :], 0.0) + jnp.where(m1, rt[3:4, :], 0.0),
                     axis=1, keepdims=True)
        yv = (buf[slot, pl.ds(r0, nrows), :].astype(F32) * gw).astype(BF16)
        return lax.dot_general(q, yv, (((0,), (0,)), ((), ())), preferred_element_type=F32)

    acc[...] = ungroup(0, 2 * MOE_TD)
    for r0 in range(2 * MOE_TD, MOE_BR, MOE_BLK):
        @pl.when(used > r0)
        def _():
            acc[...] += ungroup(r0, MOE_BLK)

    gt2 = gt2_ref[...]
    if gt2.shape[0] not in (1, MOE_TD):
        gt2 = jnp.concatenate([gt2] * (MOE_TD // gt2.shape[0]), axis=0)
    xo = x1_ref[...] + gt2 * acc[...]
    y_ref[...] = _rms(xo) * gf_ref[...]


def _combine_call(tab, rt, pos, x1, gt2, gt2_spec, g_final, ys, n_tiles, t_off):
    n_out_tiles = x1.shape[0] // MOE_TD
    return pl.pallas_call(
        functools.partial(_combine_kernel, n_tiles=n_tiles, t_off=t_off),
        grid_spec=pltpu.PrefetchScalarGridSpec(
            num_scalar_prefetch=1, grid=(n_out_tiles,),
            in_specs=[pl.BlockSpec((8, MOE_TD), lambda t, tab: (0, t)),
                      pl.BlockSpec((None, 8, MOE_TD), lambda t, tab: (t + t_off, 0, 0)),
                      pl.BlockSpec((MOE_TD, D), lambda t, tab: (t, 0)),
                      gt2_spec,
                      pl.BlockSpec((1, D), lambda t, tab: (0, 0)),
                      pl.BlockSpec(memory_space=pl.ANY)],
            out_specs=pl.BlockSpec((MOE_TD, D), lambda t, tab: (t, 0)),
            scratch_shapes=[pltpu.VMEM((2, MOE_BR, D), BF16),
                            pltpu.VMEM((MOE_TD, D), F32),
                            pltpu.SemaphoreType.DMA((2, NE))]),
        out_shape=jax.ShapeDtypeStruct((n_out_tiles * MOE_TD, D), F32),
        compiler_params=pltpu.CompilerParams(dimension_semantics=("arbitrary",),
                                             vmem_limit_bytes=VMEM_LIMIT),
        name="moe_combine",
    )(tab, rt, pos, x1, gt2, g_final.reshape(1, D), ys)


def _moe_plan(cnt, nc_max):
    cnt8 = (cnt + ROW_ALIGN - 1) // ROW_ALIGN * ROW_ALIGN
    seg_rows = cnt8.sum(axis=0)
    seg_pad = (seg_rows + MOE_TG - 1) // MOE_TG * MOE_TG
    seg_start = jnp.cumsum(seg_pad) - seg_pad
    dst = seg_start[None, :] + jnp.cumsum(cnt8, axis=0) - cnt8
    boff = jnp.cumsum(cnt8, axis=1) - cnt8
    tile_end = jnp.cumsum(seg_pad // MOE_TG)
    n_active = tile_end[-1:].astype(jnp.int32)
    tab = jnp.concatenate([cnt8.ravel(), boff.ravel(), dst.ravel(),
                           seg_start + seg_rows, seg_pad - seg_rows, n_active,
                           cnt8.sum(axis=1)]).astype(jnp.int32)
    nt = seg_pad // MOE_TG
    nfull = nt // MOE_CK
    rem = nt % MOE_CK
    nch = nfull + (rem > 0)
    cend = jnp.cumsum(nch)
    cstart = cend - nch
    c = jnp.arange(nc_max, dtype=jnp.int32)
    ce = jnp.minimum(jnp.sum(c[:, None] >= cend[None, :], axis=1), NE - 1)
    local = c - cstart[ce]
    valid = c < cend[-1]
    ck = jnp.where(valid, jnp.where(local < nfull[ce], MOE_CK, rem[ce]), 0)
    crow = jnp.where(valid, seg_start[ce] + local * (MOE_CK * MOE_TG), 0)
    ctab = jnp.concatenate([cstart, cend[-1:], crow, ck, n_active]).astype(jnp.int32)
    return tab, ctab


def kernel(x_prompt, x_sample, c_prompt, c_sample, state_ssm_re, state_ssm_im, cache_conv, w_ada, b_ada, g_norm_mix, w_in, ssm_a_re, ssm_a_im, ssm_log_dt, ssm_b_re, ssm_b_im, ssm_c_re, ssm_c_im, ssm_d, w_ssm_glu, b_ssm_glu, w_dw, b_dw, ln_conv_g, ln_conv_b, g_out_ssm, g_out_conv, w_out, g_norm_ffn, w_router_grp, b_router_grp, w_router_exp, b_router_exp, w_exp_gate, w_exp_up, w_exp_down, g_final):
    depth = w_ada.shape[0]
    assert depth == 1
    bsz, seq, _ = x_prompt.shape
    nb, nt, _ = x_sample.shape

    n_c = bsz + nb
    c_pad = -n_c % 16
    c_all = jnp.concatenate([c_prompt, c_sample, jnp.zeros((c_pad, D), F32)], axis=0)
    mod_p, mod_s = _mod_call(c_all, w_ada[0], b_ada[0], bsz, nb)
    mod_p = mod_p.reshape(bsz, 6, D)

    ab_re, ab_im, bb_re, bb_im, c_im_neg = _ssm_prep_call(
        ssm_a_re[0], ssm_a_im[0], ssm_log_dt[0], ssm_b_re[0], ssm_b_im[0], ssm_c_im[0])
    wb, wc = _block_diag_weights(bb_re, bb_im, ssm_c_re[0], c_im_neg)
    a_tok = jnp.stack([ab_re.reshape(NCHUNK, LANES), ab_im.reshape(NCHUNK, LANES)])
    a_row = jnp.stack([ab_re.reshape(NSTATE), ab_im.reshape(NSTATE)])

    w_r = jnp.concatenate([w_router_exp[0].reshape(D, NE), w_router_grp[0],
                           jnp.zeros((D, LANES - NE - NG), F32)], axis=1)
    w_r_hi = w_r.astype(BF16)
    w_r = jnp.concatenate([w_r_hi, (w_r - w_r_hi.astype(F32)).astype(BF16)], axis=1)
    b_r = jnp.concatenate([b_router_exp[0].reshape(NE), b_router_grp[0],
                           jnp.zeros((LANES - NE - NG,), F32)]).reshape(1, LANES)
    w_dw_p = jnp.concatenate([w_dw[0], jnp.zeros((1, CONV_CH), F32)], axis=0)

    row = lambda v: v.reshape(1, -1)
    common_a = (row(g_norm_mix[0]), w_in[0].astype(BF16), wb)
    common_b = (wc, row(ssm_d[0].reshape(SSM_W)), w_ssm_glu[0].astype(BF16), row(b_ssm_glu[0]),
                w_dw_p, row(b_dw[0]), row(ln_conv_g[0]), row(ln_conv_b[0]),
                row(g_out_ssm[0]), row(g_out_conv[0]), w_out[0].astype(BF16),
                row(g_norm_ffn[0]), w_r, b_r)

    n_p = bsz * seq
    n_s = nb * nt
    n_all = n_p + n_s
    assert n_s == MOE_TD and seq % MOE_TD == 0 and MOE_TD % PROMPT_TC == 0
    wts_p = common_a + (a_tok,) + common_b
    x1_p, n2_p, rt_p, cnt_p, hr_p, hi_p, cache_p = _prompt_mixer_call(x_prompt, mod_p, wts_p, PROMPT_TC)

    x_tm = jnp.transpose(x_sample, (1, 0, 2)).reshape(nt * nb, D)
    cache_tm = jnp.transpose(cache_conv[0], (1, 0, 2))
    wts_s = common_a + (a_row,) + common_b
    x1_s, n2_s, rt_s, cnt_s, hr_s, hi_s, glu_s = _sample_mixer_call(
        x_tm, mod_s, state_ssm_re[0].reshape(nb, NSTATE), state_ssm_im[0].reshape(nb, NSTATE),
        cache_tm, wts_s, nb, nt)

    n_ptiles = n_p // MOE_TD
    n_tiles = n_all // MOE_TD
    r_max = -(-(2 * n_all + n_tiles * NE * (ROW_ALIGN - 1) + NE * (MOE_TG - ROW_ALIGN)) // MOE_TG) * MOE_TG
    cnt = jnp.concatenate([cnt_p.reshape(n_ptiles, MOE_TD // PROMPT_TC, LANES).sum(axis=1), cnt_s])
    nc_max = r_max // MOE_TG // MOE_CK + NE
    tab, ctab = _moe_plan(cnt[:, :NE].astype(jnp.int32), nc_max)
    xs, pos = _dispatch_call(tab, rt_p, n2_p, rt_s, n2_s, r_max)
    ys = _experts_call(ctab, xs, w_exp_gate[0], w_exp_up[0], w_exp_down[0], nc_max)
    tiles_per_b = seq // MOE_TD
    gt2_p = mod_p[:, 5:6, :]
    y_p = _combine_call(tab, rt_p, pos, x1_p, gt2_p,
                        pl.BlockSpec((None, 1, D), lambda t, tab: (t // tiles_per_b, 0, 0)),
                        g_final, ys, n_tiles, 0)
    y_s = _combine_call(tab, rt_s, pos, x1_s, mod_s,
                        pl.BlockSpec((nb, D), lambda t, tab: (0, 5)),
                        g_final, ys, n_tiles, n_ptiles)

    y_prompt = y_p.reshape(bsz, seq, D)
    y_sample = jnp.transpose(y_s.reshape(nt, nb, D), (1, 0, 2))
    new_cache_s = jnp.concatenate(
        [cache_conv[0][:, nt:, :], jnp.transpose(glu_s.reshape(nt, nb, CONV_CH), (1, 0, 2))], axis=1)
    return (y_prompt, y_sample,
            hr_p.reshape(1, bsz, G, P), hi_p.reshape(1, bsz, G, P), cache_p[None],
            hr_s.reshape(1, nb, G, P), hi_s.reshape(1, nb, G, P), new_cache_s[None])
```

```python
import functools

import jax
import jax.numpy as jnp
import numpy as np
from jax import lax
from jax.experimental import pallas as pl
from jax.experimental.pallas import tpu as pltpu

F32 = jnp.float32
BF16 = jnp.bfloat16

D = 1024
SSM_W = 512
CONV_CH = 512
G = 32
H = 16
P = 64
KW = 31
CB = KW - 1
NE = 32
NG = 4
EPG = 8
DE = 512
EPS = 1e-6
LANES = 128
SUBLANES = 8
NSTATE = G * P
NCHUNK = NSTATE // LANES

PROMPT_TC = 512
SCAN_GROUP = 8
V7X_VMEM_BYTES = 64 * 1024 * 1024
VMEM_LIMIT = V7X_VMEM_BYTES - 8 * 1024 * 1024


def _rms(x):
    return x * lax.rsqrt(jnp.mean(x * x, axis=-1, keepdims=True) + EPS)


def _sigmoid(x):
    return 0.5 * jnp.tanh(0.5 * x) + 0.5


def _gelu_tanh(y):
    c = np.sqrt(2.0 / np.pi).astype(np.float32)
    return y * (0.5 * (1.0 + jnp.tanh(c * (y + 0.044715 * (y * y * y)))))


def _bdot(a, b):
    return jnp.dot(a.astype(BF16), b, preferred_element_type=F32)


def _mod_kernel(c_ref, w_ref, b_ref, op_ref, os_ref):
    c = c_ref[...]
    s = c * _sigmoid(c)
    n = s.shape[0]
    s_hi = s.astype(BF16)
    lhs = jnp.concatenate([s_hi, (s - s_hi.astype(F32)).astype(BF16)], axis=0)
    w = w_ref[...]
    w_hi = w.astype(BF16)
    w_lo = (w - w_hi.astype(F32)).astype(BF16)
    p_hi = jnp.dot(lhs, w_hi, preferred_element_type=F32)
    p_lo = jnp.dot(lhs, w_lo, preferred_element_type=F32)
    res = (p_hi[:n] + p_lo[:n]) + (p_hi[n:] + p_lo[n:]) + b_ref[...]
    n_p = op_ref.shape[0]
    op_ref[...] = res[:n_p]
    os_ref[...] = res[n_p:n_p + os_ref.shape[0]]


def _mod_call(c_all, w_ada, b_ada, n_p, n_s):
    n = c_all.shape[0]
    tn = 512
    return pl.pallas_call(
        _mod_kernel,
        grid=(6 * D // tn,),
        in_specs=[pl.BlockSpec((n, D), lambda j: (0, 0)),
                  pl.BlockSpec((D, tn), lambda j: (0, j)),
                  pl.BlockSpec((1, tn), lambda j: (0, j))],
        out_specs=(pl.BlockSpec((n_p, tn), lambda j: (0, j)),
                   pl.BlockSpec((n_s, tn), lambda j: (0, j))),
        out_shape=(jax.ShapeDtypeStruct((n_p, 6 * D), F32),
                   jax.ShapeDtypeStruct((n_s, 6 * D), F32)),
        compiler_params=pltpu.CompilerParams(dimension_semantics=("arbitrary",)),
        name="mod",
    )(c_all, w_ada, b_ada.reshape(1, 6 * D))


def _ssm_prep_kernel(a_re, a_im, log_dt, b_re, b_im, c_im,
                     ab_re_o, ab_im_o, bb_re_o, bb_im_o, cneg_o):
    lam_re = jnp.minimum(a_re[...], -1e-4)
    lam_im = a_im[...]
    dt = jnp.exp(log_dt[...])
    mag = jnp.exp(lam_re * dt)
    ab_re = mag * jnp.cos(lam_im * dt)
    ab_im = mag * jnp.sin(lam_im * dt)
    den = lam_re * lam_re + lam_im * lam_im
    num_re = ab_re - 1.0
    coef_re = (num_re * lam_re + ab_im * lam_im) / den
    coef_im = (ab_im * lam_re - num_re * lam_im) / den
    ab_re_o[...] = ab_re
    ab_im_o[...] = ab_im
    br = b_re[...]
    bi = b_im[...]
    bb_re_o[...] = coef_re * br - coef_im * bi
    bb_im_o[...] = coef_re * bi + coef_im * br
    cneg_o[...] = -c_im[...]


def _ssm_prep_call(a_re, a_im, log_dt, b_re, b_im, c_im):
    flat = lambda v: v.reshape(1, NSTATE)
    b_hs = lambda v: jnp.transpose(v, (2, 0, 1)).reshape(H, NSTATE)
    dt_row = jnp.broadcast_to(log_dt[:, None], (G, P)).reshape(1, NSTATE)
    ab_re, ab_im, bb_re, bb_im, cneg = pl.pallas_call(
        _ssm_prep_kernel,
        out_shape=(jax.ShapeDtypeStruct((1, NSTATE), F32), jax.ShapeDtypeStruct((1, NSTATE), F32),
                   jax.ShapeDtypeStruct((H, NSTATE), F32), jax.ShapeDtypeStruct((H, NSTATE), F32),
                   jax.ShapeDtypeStruct((G * H, P), F32)),
        name="ssm_prep",
    )(flat(a_re), flat(a_im), dt_row, b_hs(b_re), b_hs(b_im), c_im.reshape(G * H, P))
    ghp = lambda v: jnp.transpose(v.reshape(H, G, P), (1, 0, 2))
    return (ab_re.reshape(G, P), ab_im.reshape(G, P), ghp(bb_re), ghp(bb_im), cneg.reshape(G, H, P))


def _block_diag_weights(bb_re, bb_im, c_re, c_im_neg):
    eye8 = jnp.eye(8, dtype=F32)
    eye4 = jnp.eye(4, dtype=F32)
    eye2 = jnp.eye(2, dtype=F32)

    def wb_part(bb):
        x = bb.reshape(4, 8, H, P)
        return jnp.einsum('qghp,gk->qghkp', x, eye8).reshape(4, 8 * H, 8 * P)

    wb = jnp.concatenate([wb_part(bb_re), wb_part(bb_im)], axis=-1).astype(BF16)

    def wc_part(c):
        x = c.reshape(4, 4, 2, H, P)
        y = jnp.einsum('qjghp,jk,gl->qjgpklh', x, eye4, eye2)
        return y.reshape(NCHUNK, 2 * P, 4 * 2 * H)

    wc = jnp.concatenate([wc_part(c_re), wc_part(c_im_neg)], axis=1).astype(BF16)
    return wb, wc


def _front(x, sc1, sh1, g_mix, w_in_ref):
    n = _rms(x) * g_mix * (1.0 + sc1) + sh1
    proj = _bdot(n, w_in_ref[...])
    u = proj[:, :SSM_W]
    glu = proj[:, SSM_W:SSM_W + CONV_CH] * _sigmoid(proj[:, SSM_W + CONV_CH:])
    return u, glu


def _ssm_out(y_lin, u, d_skip, w_glu_ref, b_glu, g_out_ssm):
    y = _gelu_tanh(y_lin + d_skip * u)
    ys = y * _sigmoid(_bdot(y, w_glu_ref[...]) + b_glu)
    return _rms(ys) * g_out_ssm


def _conv_out(conv, b_dw, ln_g, ln_b, g_out_conv):
    c = conv + b_dw
    mu = jnp.mean(c, axis=-1, keepdims=True)
    cc = c - mu
    var = jnp.mean(cc * cc, axis=-1, keepdims=True)
    ln = cc * lax.rsqrt(var + EPS) * ln_g + ln_b
    yc = ln * _sigmoid(ln)
    return _rms(yc) * g_out_conv


def _route(n2, w_r_ref, b_r):
    rows = n2.shape[0]
    n_hi = n2.astype(BF16)
    n_lo = (n2 - n_hi.astype(F32)).astype(BF16)
    parts = jnp.dot(jnp.concatenate([n_hi, n_lo], axis=0), w_r_ref[...], preferred_element_type=F32)
    lg = (parts[:rows, :LANES] + parts[:rows, LANES:]) + (parts[rows:, :LANES] + parts[rows:, LANES:]) + b_r
    lane = lax.broadcasted_iota(jnp.int32, (rows, LANES), 1).astype(F32)
    ninf = -jnp.inf
    big = 1e9
    gmask = jnp.logical_and(lane >= NE, lane < NE + NG)
    gl = jnp.where(gmask, lg, ninf)
    gmax = jnp.max(gl, axis=-1, keepdims=True)
    gsum = jnp.sum(jnp.where(gmask, jnp.exp(gl - gmax), 0.0), axis=-1, keepdims=True)
    p_top = 1.0 / gsum
    gi = jnp.min(jnp.where(gl == gmax, lane, big), axis=-1, keepdims=True) - NE
    lo = gi * EPG
    emask = jnp.logical_and(lane >= lo, lane < lo + EPG)
    el = jnp.where(emask, lg, ninf)
    m1 = jnp.max(el, axis=-1, keepdims=True)
    i1 = jnp.min(jnp.where(el == m1, lane, big), axis=-1, keepdims=True)
    el2 = jnp.where(lane == i1, ninf, el)
    m2 = jnp.max(el2, axis=-1, keepdims=True)
    i2 = jnp.min(jnp.where(el2 == m2, lane, big), axis=-1, keepdims=True)
    e2 = jnp.exp(m2 - m1)
    den = 1.0 + e2
    w1 = p_top / den
    w2 = p_top * e2 / den
    cnt = jnp.sum(jnp.where(lane == i1, 1.0, 0.0) + jnp.where(lane == i2, 1.0, 0.0),
                  axis=0, keepdims=True)
    cols = (jnp.where(lane == 0.0, i1, 0.0) + jnp.where(lane == 1.0, i2, 0.0)
            + jnp.where(lane == 2.0, w1, 0.0) + jnp.where(lane == 3.0, w2, 0.0))
    return cols.T[0:8, :], cnt


def _tail(x, ns, nc, gt1, sc2, sh2, w_out_ref, g_ffn, w_r_ref, b_r):
    merged = _bdot(jnp.concatenate([ns, nc], axis=-1), w_out_ref[...])
    x1 = x + gt1 * merged
    n2 = _rms(x1) * g_ffn * (1.0 + sc2) + sh2
    rt, cnt = _route(n2, w_r_ref, b_r)
    return x1, n2, rt, cnt


def _prompt_mixer_kernel(x_ref, mod_ref, g_mix_ref, w_in_ref, wb_ref, a_ref, wc_ref, dsk_ref,
                         w_glu_ref, b_glu_ref, w_dw_ref, b_dw_ref, ln_g_ref, ln_b_ref,
                         g_os_ref, g_oc_ref, w_out_ref, g_ffn_ref, w_r_ref, b_r_ref,
                         x1_ref, n2_ref, rt_ref, cnt_ref, hr_ref, hi_ref, cache_ref,
                         sre, sim, hst, ebuf, eshift, ncbuf, *, tc, pt):
    c = pl.program_id(1)
    nc_chunks = pl.num_programs(1)

    @pl.when(c == 0)
    def _():
        hst[...] = jnp.zeros_like(hst)
        ebuf[pl.ds(0, 32), :] = jnp.zeros((32, CONV_CH), F32)

    x = x_ref[...]
    mod = mod_ref[...]
    sh1, sc1, gt1 = mod[0:1], mod[1:2], mod[2:3]
    sh2, sc2 = mod[3:4], mod[4:5]

    u, glu = _front(x, sc1, sh1, g_mix_ref[...], w_in_ref)
    ebuf[pl.ds(32, tc), :] = glu

    for s in range(1, 8):
        eshift[s - 1, pl.ds(0, tc + 24), :] = ebuf[pl.ds(s, tc + 24), :]
    rb = 64
    convs = []
    for r0 in range(0, tc, rb):
        acc = None
        for k in range(KW):
            a8, s = (k + 2) // 8 * 8, (k + 2) % 8
            win = ebuf[pl.ds(r0 + a8, rb), :] if s == 0 else eshift[s - 1, pl.ds(r0 + a8, rb), :]
            term = w_dw_ref[k:k + 1, :] * win
            acc = term if acc is None else acc + term
        convs.append(acc)
    conv = jnp.concatenate(convs, axis=0)
    ncbuf[...] = _conv_out(conv, b_dw_ref[...], ln_g_ref[...], ln_b_ref[...], g_oc_ref[...])

    @pl.when(c == nc_chunks - 1)
    def _():
        cache_ref[...] = ebuf[pl.ds(tc + 2, CB), :]

    ebuf[pl.ds(0, 32), :] = ebuf[pl.ds(tc, 32), :]

    ub = u.astype(BF16)
    for q in range(4):
        r = jnp.dot(ub[:, q * LANES:(q + 1) * LANES], wb_ref[q], preferred_element_type=F32)
        for k in range(4):
            j = 4 * q + k
            sre[pl.ds(j * pt, tc), :] = r[:, k * LANES:(k + 1) * LANES]
            sim[pl.ds(j * pt, tc), :] = r[:, SSM_W + k * LANES:SSM_W + (k + 1) * LANES]

    ar0, ar1 = a_ref[0, 0:8, :], a_ref[0, 8:16, :]
    ai0, ai1 = a_ref[1, 0:8, :], a_ref[1, 8:16, :]

    def step(g, carry):
        hr0, hr1, hi0, hi1 = carry
        t0 = g * SCAN_GROUP
        idx = [(pl.ds(t0 + u, 8, stride=pt), pl.ds(t0 + u + 8 * pt, 8, stride=pt))
               for u in range(SCAN_GROUP)]
        bu = [(sre[i0, :], sre[i1, :], sim[i0, :], sim[i1, :]) for (i0, i1) in idx]
        hs = []
        for br0, br1, bi0, bi1 in bu:
            nr0 = ar0 * hr0 - ai0 * hi0 + br0
            ni0 = ar0 * hi0 + ai0 * hr0 + bi0
            nr1 = ar1 * hr1 - ai1 * hi1 + br1
            ni1 = ar1 * hi1 + ai1 * hr1 + bi1
            hr0, hr1, hi0, hi1 = nr0, nr1, ni0, ni1
            hs.append((nr0, nr1, ni0, ni1))
        for (i0, i1), (nr0, nr1, ni0, ni1) in zip(idx, hs):
            sre[i0, :] = nr0
            sim[i0, :] = ni0
            sre[i1, :] = nr1
            sim[i1, :] = ni1
        return hr0, hr1, hi0, hi1

    init = (hst[0, 0:8, :], hst[0, 8:16, :], hst[1, 0:8, :], hst[1, 8:16, :])
    hr0, hr1, hi0, hi1 = lax.fori_loop(0, tc // SCAN_GROUP, step, init)
    hst[0, 0:8, :] = hr0
    hst[0, 8:16, :] = hr1
    hst[1, 0:8, :] = hi0
    hst[1, 8:16, :] = hi1

    @pl.when(c == nc_chunks - 1)
    def _():
        hr_ref[...] = hst[0]
        hi_ref[...] = hst[1]

    ys = []
    for q in range(4):
        acc = None
        for jj in range(4):
            j = 4 * q + jj
            lhs = jnp.concatenate([sre[pl.ds(j * pt, tc), :], sim[pl.ds(j * pt, tc), :]], axis=-1)
            d = jnp.dot(lhs.astype(BF16), wc_ref[j], preferred_element_type=F32)
            acc = d if acc is None else acc + d
        ys.append(acc)
    y_lin = jnp.concatenate(ys, axis=-1)
    ns = _ssm_out(y_lin, u, dsk_ref[...], w_glu_ref, b_glu_ref[...], g_os_ref[...])
    nc = ncbuf[...]

    x1, n2, rt, cnt = _tail(x, ns, nc, gt1, sc2, sh2, w_out_ref, g_ffn_ref[...], w_r_ref, b_r_ref[...])
    x1_ref[...] = x1
    n2_ref[...] = n2.astype(BF16)
    rt_ref[...] = rt
    cnt_ref[...] = cnt


def _const_spec(shape):
    nd = len(shape)
    return pl.BlockSpec(shape, lambda b, c: (0,) * nd)


def _prompt_mixer_call(x, mod6, wts, tc):
    bsz, t, _ = x.shape
    n_all = bsz * t
    pt = tc + SUBLANES
    assert (pt // SUBLANES) % 2 == 1
    nc = t // tc
    kern = functools.partial(_prompt_mixer_kernel, tc=tc, pt=pt)
    in_specs = [pl.BlockSpec((None, tc, D), lambda b, c: (b, c, 0)),
                pl.BlockSpec((None, 6, D), lambda b, c: (b, 0, 0))]
    in_specs += [_const_spec(w.shape) for w in wts]
    out_shape = (jax.ShapeDtypeStruct((n_all, D), F32),
                 jax.ShapeDtypeStruct((n_all, D), BF16),
                 jax.ShapeDtypeStruct((8, n_all), F32),
                 jax.ShapeDtypeStruct((n_all // tc, 1, LANES), F32),
                 jax.ShapeDtypeStruct((bsz, NCHUNK, LANES), F32),
                 jax.ShapeDtypeStruct((bsz, NCHUNK, LANES), F32),
                 jax.ShapeDtypeStruct((bsz, CB, CONV_CH), F32))
    out_specs = (pl.BlockSpec((tc, D), lambda b, c: (b * nc + c, 0)),
                 pl.BlockSpec((tc, D), lambda b, c: (b * nc + c, 0)),
                 pl.BlockSpec((8, tc), lambda b, c: (0, b * nc + c)),
                 pl.BlockSpec((None, 1, LANES), lambda b, c: (b * nc + c, 0, 0)),
                 pl.BlockSpec((None, NCHUNK, LANES), lambda b, c: (b, 0, 0)),
                 pl.BlockSpec((None, NCHUNK, LANES), lambda b, c: (b, 0, 0)),
                 pl.BlockSpec((None, CB, CONV_CH), lambda b, c: (b, 0, 0)))
    scratch = [pltpu.VMEM((NCHUNK * pt, LANES), F32),
               pltpu.VMEM((NCHUNK * pt, LANES), F32),
               pltpu.VMEM((2, NCHUNK, LANES), F32),
               pltpu.VMEM((tc + 32, CONV_CH), F32),
               pltpu.VMEM((7, tc + 32, CONV_CH), F32),
               pltpu.VMEM((tc, CONV_CH), F32)]
    return pl.pallas_call(
        kern, grid=(bsz, nc), in_specs=in_specs, out_specs=out_specs, out_shape=out_shape,
        scratch_shapes=scratch,
        compiler_params=pltpu.CompilerParams(dimension_semantics=("arbitrary", "arbitrary"),
                                             vmem_limit_bytes=VMEM_LIMIT),
        name="prompt_mixer",
    )(x, mod6, *wts)


def _sample_mixer_kernel(x_ref, mod_ref, h0r_ref, h0i_ref, cache_ref,
                         g_mix_ref, w_in_ref, wb_ref, a_ref, wc_ref, dsk_ref,
                         w_glu_ref, b_glu_ref, w_dw_ref, b_dw_ref, ln_g_ref, ln_b_ref,
                         g_os_ref, g_oc_ref, w_out_ref, g_ffn_ref, w_r_ref, b_r_ref,
                         x1_ref, n2_ref, rt_ref, cnt_ref, hr_ref, hi_ref, glu_ref,
                         sre, sim, *, nb, nt):
    x = x_ref[...]

    def rows(i):
        m = mod_ref[:, i * D:(i + 1) * D]
        return jnp.concatenate([m] * nt, axis=0)

    sh1, sc1, gt1, sh2, sc2 = rows(0), rows(1), rows(2), rows(3), rows(4)
    u, glu = _front(x, sc1, sh1, g_mix_ref[...], w_in_ref)
    glu_ref[...] = glu

    ub = u.astype(BF16)
    for q in range(4):
        r = jnp.dot(ub[:, q * LANES:(q + 1) * LANES], wb_ref[q], preferred_element_type=F32)
        sre[:, q * SSM_W:(q + 1) * SSM_W] = r[:, :SSM_W]
        sim[:, q * SSM_W:(q + 1) * SSM_W] = r[:, SSM_W:]

    ar = a_ref[0:1, :]
    ai = a_ref[1:2, :]
    hr = h0r_ref[...]
    hi = h0i_ref[...]
    for t in range(nt):
        rs = pl.ds(t * nb, nb)
        nr = ar * hr - ai * hi + sre[rs, :]
        ni = ar * hi + ai * hr + sim[rs, :]
        sre[rs, :] = nr
        sim[rs, :] = ni
        hr, hi = nr, ni
    hr_ref[...] = hr
    hi_ref[...] = hi

    ys = []
    for q in range(4):
        acc = None
        for jj in range(4):
            j = 4 * q + jj
            lhs = jnp.concatenate([sre[:, j * LANES:(j + 1) * LANES],
                                   sim[:, j * LANES:(j + 1) * LANES]], axis=-1)
            d = jnp.dot(lhs.astype(BF16), wc_ref[j], preferred_element_type=F32)
            acc = d if acc is None else acc + d
        ys.append(acc)
    y_lin = jnp.concatenate(ys, axis=-1)
    ns = _ssm_out(y_lin, u, dsk_ref[...], w_glu_ref, b_glu_ref[...], g_os_ref[...])

    def ext(jrow):
        if jrow < CB:
            return cache_ref[jrow]
        return glu[(jrow - CB) * nb:(jrow - CB + 1) * nb, :]

    convs = []
    for t in range(nt):
        acc = None
        for k in range(KW):
            term = w_dw_ref[k:k + 1, :] * ext(t + k)
            acc = term if acc is None else acc + term
        convs.append(acc)
    conv = jnp.concatenate(convs, axis=0)
    nc = _conv_out(conv, b_dw_ref[...], ln_g_ref[...], ln_b_ref[...], g_oc_ref[...])

    x1, n2, rt, cnt = _tail(x, ns, nc, gt1, sc2, sh2, w_out_ref, g_ffn_ref[...], w_r_ref, b_r_ref[...])
    x1_ref[...] = x1
    n2_ref[...] = n2.astype(BF16)
    rt_ref[...] = rt
    cnt_ref[...] = cnt


def _sample_mixer_call(x_tm, mod_s, h0r, h0i, cache_tm, wts, nb, nt):
    n = nb * nt
    kern = functools.partial(_sample_mixer_kernel, nb=nb, nt=nt)
    out_shape = (jax.ShapeDtypeStruct((n, D), F32),
                 jax.ShapeDtypeStruct((n, D), BF16),
                 jax.ShapeDtypeStruct((8, n), F32),
                 jax.ShapeDtypeStruct((1, LANES), F32),
                 jax.ShapeDtypeStruct((nb, NSTATE), F32),
                 jax.ShapeDtypeStruct((nb, NSTATE), F32),
                 jax.ShapeDtypeStruct((n, CONV_CH), F32))
    scratch = [pltpu.VMEM((n, NSTATE), F32), pltpu.VMEM((n, NSTATE), F32)]
    return pl.pallas_call(
        kern, out_shape=out_shape, scratch_shapes=scratch,
        compiler_params=pltpu.CompilerParams(vmem_limit_bytes=VMEM_LIMIT),
        name="sample_mixer",
    )(x_tm, mod_s, h0r, h0i, cache_tm, *wts)


ROW_ALIGN = 16
MOE_TD = 512
MOE_BR = MOE_TD * 2 + NE * ROW_ALIGN
MOE_TG = 256


def _slot_positions(rt):
    t = rt.shape[1]
    e0 = rt[0:1, :]
    e1 = rt[1:2, :]
    sub = lax.broadcasted_iota(jnp.int32, (LANES, t), 0).astype(F32)
    a0 = jnp.where(sub == e0, 1.0, 0.0)
    a1 = jnp.where(sub == e1, 1.0, 0.0)
    at = a0 + a1
    r = lax.broadcasted_iota(jnp.int32, (t, t), 0)
    c = lax.broadcasted_iota(jnp.int32, (t, t), 1)
    before = jnp.where(r < c, 1.0, 0.0).astype(BF16)
    rank = jnp.dot(at.astype(BF16), before, preferred_element_type=F32)
    cnt = jnp.sum(at, axis=1, keepdims=True)
    cnt_al = jnp.ceil(cnt * (1.0 / ROW_ALIGN)) * float(ROW_ALIGN)
    er = lax.broadcasted_iota(jnp.int32, (LANES, LANES), 0)
    ec = lax.broadcasted_iota(jnp.int32, (LANES, LANES), 1)
    lower = jnp.where(ec < er, 1.0, 0.0)
    base = jnp.dot(lower, jnp.broadcast_to(cnt_al, (LANES, LANES)), preferred_element_type=F32,
                   precision=lax.Precision.HIGHEST)[:, 0:1]
    slot = rank + base
    pos0 = jnp.sum(a0 * slot, axis=0, keepdims=True)
    pos1 = jnp.sum(a1 * slot, axis=0, keepdims=True)
    return pos0, pos1


def _segment_copies(tab_ref, t, n_tiles, buf, hbm, sems, slot, to_hbm, wait):
    for e in range(NE):
        n = pl.multiple_of(tab_ref[t * NE + e], ROW_ALIGN)
        b = pl.multiple_of(tab_ref[(n_tiles + t) * NE + e], ROW_ALIGN)
        d = pl.multiple_of(tab_ref[(2 * n_tiles + t) * NE + e], ROW_ALIGN)
        vm = buf.at[slot, pl.ds(b, n)]
        hb = hbm.at[pl.ds(d, n)]
        cp = pltpu.make_async_copy(vm, hb, sems.at[slot, e]) if to_hbm else \
            pltpu.make_async_copy(hb, vm, sems.at[slot, e])

        @pl.when(n > 0)
        def _():
            if wait:
                cp.wait()
            else:
                cp.start()


def _tile_rows(tab_ref, t, n_tiles):
    return tab_ref[3 * n_tiles * NE + 2 * NE + 1 + t]


def _one_hot_rows(r0, nrows, pos0, pos1):
    row = (lax.broadcasted_iota(jnp.int32, (nrows, MOE_TD), 0) + r0).astype(F32)
    return row == pos0, row == pos1


MOE_BLK = 256


def _dispatch_kernel(tab_ref, rtp_ref, n2p_ref, rts_ref, n2s_ref, xs_ref, pos_ref, buf, zbuf, sems, zsem,
                     *, n_tiles, n_ptiles, n_gtiles):
    t = pl.program_id(0)
    slot = lax.rem(t, 2)
    first_free = tab_ref[3 * n_tiles * NE + 2 * NE]

    def fill_copy(j):
        d = pl.multiple_of(j * MOE_TG, MOE_TG)
        return pltpu.make_async_copy(zbuf, xs_ref.at[pl.ds(d, MOE_TG)], zsem)

    def fill_start(j, carry):
        fill_copy(j).start()
        return carry

    def fill_wait(j, carry):
        fill_copy(j).wait()
        return carry

    @pl.when(t == 0)
    def _():
        zbuf[...] = jnp.zeros_like(zbuf)
        for phase in range(2):
            for e in range(NE):
                d = pl.multiple_of(tab_ref[3 * n_tiles * NE + e], ROW_ALIGN)
                n = pl.multiple_of(tab_ref[3 * n_tiles * NE + NE + e], ROW_ALIGN)
                cp = pltpu.make_async_copy(zbuf.at[pl.ds(0, n)], xs_ref.at[pl.ds(d, n)], sems.at[1, e])

                @pl.when(n > 0)
                def _():
                    if phase == 0:
                        cp.start()
                    else:
                        cp.wait()

        lax.fori_loop(first_free, n_gtiles, fill_start, 0)

    @pl.when(t >= 2)
    def _():
        _segment_copies(tab_ref, t - 2, n_tiles, buf, xs_ref, sems, slot, to_hbm=True, wait=True)

    is_sample = t >= n_ptiles
    rt = jnp.where(is_sample, rts_ref[...], rtp_ref[...])
    n2 = jnp.where(is_sample, n2s_ref[...], n2p_ref[...])
    pos0, pos1 = _slot_positions(rt)
    pos_ref[...] = jnp.concatenate([pos0, pos1, jnp.zeros((6, MOE_TD), F32)], axis=0)
    used = _tile_rows(tab_ref, t, n_tiles)

    def group(r0, nrows):
        m0, m1 = _one_hot_rows(r0, nrows, pos0, pos1)
        q = (jnp.where(m0, 1.0, 0.0) + jnp.where(m1, 1.0, 0.0)).astype(BF16)
        buf[slot, pl.ds(r0, nrows), :] = jnp.dot(q, n2, preferred_element_type=F32).astype(BF16)

    group(0, 2 * MOE_TD)
    for r0 in range(2 * MOE_TD, MOE_BR, MOE_BLK):
        @pl.when(used > r0)
        def _():
            group(r0, MOE_BLK)

    _segment_copies(tab_ref, t, n_tiles, buf, xs_ref, sems, slot, to_hbm=True, wait=False)

    @pl.when(t == n_tiles - 1)
    def _():
        if n_tiles >= 2:
            _segment_copies(tab_ref, t - 1, n_tiles, buf, xs_ref, sems, 1 - slot, to_hbm=True, wait=True)
        _segment_copies(tab_ref, t, n_tiles, buf, xs_ref, sems, slot, to_hbm=True, wait=True)
        lax.fori_loop(first_free, n_gtiles, fill_wait, 0)


def _dispatch_call(tab, rt_p, n2_p, rt_s, n2_s, r_max):
    n_ptiles = n2_p.shape[0] // MOE_TD
    n_tiles = n_ptiles + n2_s.shape[0] // MOE_TD
    last_p = n_ptiles - 1
    return pl.pallas_call(
        functools.partial(_dispatch_kernel, n_tiles=n_tiles, n_ptiles=n_ptiles,
                          n_gtiles=r_max // MOE_TG),
        grid_spec=pltpu.PrefetchScalarGridSpec(
            num_scalar_prefetch=1, grid=(n_tiles,),
            in_specs=[pl.BlockSpec((8, MOE_TD), lambda t, tab: (0, jnp.minimum(t, last_p))),
                      pl.BlockSpec((MOE_TD, D), lambda t, tab: (jnp.minimum(t, last_p), 0)),
                      pl.BlockSpec((8, MOE_TD), lambda t, tab: (0, 0)),
                      pl.BlockSpec((MOE_TD, D), lambda t, tab: (0, 0))],
            out_specs=(pl.BlockSpec(memory_space=pl.ANY),
                       pl.BlockSpec((None, 8, MOE_TD), lambda t, tab: (t, 0, 0))),
            scratch_shapes=[pltpu.VMEM((2, MOE_BR, D), BF16),
                            pltpu.VMEM((MOE_TG, D), BF16),
                            pltpu.SemaphoreType.DMA((2, NE)),
                            pltpu.SemaphoreType.DMA(())]),
        out_shape=(jax.ShapeDtypeStruct((r_max, D), BF16),
                   jax.ShapeDtypeStruct((n_tiles, 8, MOE_TD), F32)),
        compiler_params=pltpu.CompilerParams(dimension_semantics=("arbitrary",),
                                             vmem_limit_bytes=VMEM_LIMIT),
        name="moe_dispatch",
    )(tab, rt_p, n2_p, rt_s, n2_s)


MOE_CK = 6


def _experts_kernel(ctab_ref, xs_ref, w1_ref, w3_ref, w2_ref, ys_ref,
                    w1b, w3b, w2b, xbuf, ybuf, zbuf, in_sem, out_sem, zsem, *, n_gtiles, nc_max):
    e = pl.program_id(0)
    c0 = ctab_ref[e]
    c1 = ctab_ref[e + 1]
    n_chunks = ctab_ref[NE]
    first_free = ctab_ref[NE + 1 + 2 * nc_max]

    def fill_copy(j):
        d = pl.multiple_of(j * MOE_TG, MOE_TG)
        return pltpu.make_async_copy(zbuf, ys_ref.at[pl.ds(d, MOE_TG)], zsem)

    def fill_start(j, carry):
        fill_copy(j).start()
        return carry

    def fill_wait(j, carry):
        fill_copy(j).wait()
        return carry

    def span(c):
        r = pl.multiple_of(ctab_ref[NE + 1 + c], MOE_TG)
        n = pl.multiple_of(ctab_ref[NE + 1 + nc_max + c] * MOE_TG, MOE_TG)
        return r, n

    def in_copy(c, slot):
        r, n = span(c)
        return pltpu.make_async_copy(xs_ref.at[pl.ds(r, n)], xbuf.at[slot, pl.ds(0, n)], in_sem.at[slot])

    def out_copy(c, slot):
        r, n = span(c)
        return pltpu.make_async_copy(ybuf.at[slot, pl.ds(0, n)], ys_ref.at[pl.ds(r, n)], out_sem.at[slot])

    @pl.when(e == 0)
    def _():
        zbuf[...] = jnp.zeros_like(zbuf)
        lax.fori_loop(first_free, n_gtiles, fill_start, 0)

        @pl.when(n_chunks > 0)
        def _():
            in_copy(0, 0).start()

    def compute(slot, rows):
        x = xbuf[slot, pl.ds(0, rows), :]
        a = jnp.dot(x, w1b[...], preferred_element_type=F32)
        b = jnp.dot(x, w3b[...], preferred_element_type=F32)
        hid = a * _sigmoid(a) * b
        y = jnp.dot(hid.astype(BF16), w2b[...], preferred_element_type=F32)
        ybuf[slot, pl.ds(0, rows), :] = y.astype(BF16)

    @pl.when(c1 > c0)
    def _():
        w1b[...] = w1_ref[...].astype(BF16)
        w3b[...] = w3_ref[...].astype(BF16)
        w2b[...] = w2_ref[...].astype(BF16)

        def chunk(c, carry):
            slot = lax.rem(c, 2)

            @pl.when(c + 1 < n_chunks)
            def _():
                in_copy(c + 1, 1 - slot).start()

            in_copy(c, slot).wait()

            @pl.when(c >= 2)
            def _():
                out_copy(c - 2, slot).wait()

            k = ctab_ref[NE + 1 + nc_max + c]
            for kk in range(1, MOE_CK + 1):
                @pl.when(k == kk)
                def _():
                    compute(slot, kk * MOE_TG)

            out_copy(c, slot).start()
            return carry

        lax.fori_loop(c0, c1, chunk, 0)

    @pl.when(e == NE - 1)
    def _():
        @pl.when(n_chunks >= 2)
        def _():
            out_copy(n_chunks - 2, lax.rem(n_chunks, 2)).wait()

        @pl.when(n_chunks >= 1)
        def _():
            out_copy(n_chunks - 1, lax.rem(n_chunks - 1, 2)).wait()

        lax.fori_loop(first_free, n_gtiles, fill_wait, 0)


def _experts_call(ctab, xs, w1, w3, w2, nc_max):
    r_max = xs.shape[0]
    w_map = lambda e, ctab: (e, 0, 0)
    ring = pltpu.VMEM((2, MOE_CK * MOE_TG, D), BF16)
    return pl.pallas_call(
        functools.partial(_experts_kernel, n_gtiles=r_max // MOE_TG, nc_max=nc_max),
        grid_spec=pltpu.PrefetchScalarGridSpec(
            num_scalar_prefetch=1, grid=(NE,),
            in_specs=[pl.BlockSpec(memory_space=pl.ANY),
                      pl.BlockSpec((None, D, DE), w_map),
                      pl.BlockSpec((None, D, DE), w_map),
                      pl.BlockSpec((None, DE, D), w_map)],
            out_specs=pl.BlockSpec(memory_space=pl.ANY),
            scratch_shapes=[pltpu.VMEM((D, DE), BF16), pltpu.VMEM((D, DE), BF16),
                            pltpu.VMEM((DE, D), BF16), ring, ring,
                            pltpu.VMEM((MOE_TG, D), BF16),
                            pltpu.SemaphoreType.DMA((2,)), pltpu.SemaphoreType.DMA((2,)),
                            pltpu.SemaphoreType.DMA(())]),
        out_shape=jax.ShapeDtypeStruct((r_max, D), BF16),
        compiler_params=pltpu.CompilerParams(dimension_semantics=("arbitrary",),
                                             vmem_limit_bytes=VMEM_LIMIT),
        name="moe_experts",
    )(ctab, xs, w1, w3, w2)


def _combine_kernel(tab_ref, rt_ref, pos_ref, x1_ref, gt2_ref, gf_ref, ys_ref, y_ref, buf, acc, sems,
                    *, n_tiles, t_off):
    i = pl.program_id(0)
    t = i + t_off
    slot = lax.rem(i, 2)

    @pl.when(i == 0)
    def _():
        buf[...] = jnp.zeros_like(buf)
        _segment_copies(tab_ref, t, n_tiles, buf, ys_ref, sems, slot, to_hbm=False, wait=False)

    @pl.when(i + 1 < pl.num_programs(0))
    def _():
        _segment_copies(tab_ref, t + 1, n_tiles, buf, ys_ref, sems, 1 - slot, to_hbm=False, wait=False)

    _segment_copies(tab_ref, t, n_tiles, buf, ys_ref, sems, slot, to_hbm=False, wait=True)

    rt = rt_ref[...]
    pos0 = pos_ref[0:1, :]
    pos1 = pos_ref[1:2, :]
    used = _tile_rows(tab_ref, t, n_tiles)

    def ungroup(r0, nrows):
        m0, m1 = _one_hot_rows(r0, nrows, pos0, pos1)
        q = (jnp.where(m0, 1.0, 0.0) + jnp.where(m1, 1.0, 0.0)).astype(BF16)
        gw = jnp.sum(jnp.where(m0, rt[2:3, :], 0.0) + jnp.where(m1, rt[3:4, :], 0.0),
                     axis=1, keepdims=True)
        yv = (buf[slot, pl.ds(r0, nrows), :].astype(F32) * gw).astype(BF16)
        return lax.dot_general(q, yv, (((0,), (0,)), ((), ())), preferred_element_type=F32)

    acc[...] = ungroup(0, 2 * MOE_TD)
    for r0 in range(2 * MOE_TD, MOE_BR, MOE_BLK):
        @pl.when(used > r0)
        def _():
            acc[...] += ungroup(r0, MOE_BLK)

    gt2 = gt2_ref[...]
    if gt2.shape[0] not in (1, MOE_TD):
        gt2 = jnp.concatenate([gt2] * (MOE_TD // gt2.shape[0]), axis=0)
    xo = x1_ref[...] + gt2 * acc[...]
    y_ref[...] = _rms(xo) * gf_ref[...]


def _combine_call(tab, rt, pos, x1, gt2, gt2_spec, g_final, ys, n_tiles, t_off):
    n_out_tiles = x1.shape[0] // MOE_TD
    return pl.pallas_call(
        functools.partial(_combine_kernel, n_tiles=n_tiles, t_off=t_off),
        grid_spec=pltpu.PrefetchScalarGridSpec(
            num_scalar_prefetch=1, grid=(n_out_tiles,),
            in_specs=[pl.BlockSpec((8, MOE_TD), lambda t, tab: (0, t)),
                      pl.BlockSpec((None, 8, MOE_TD), lambda t, tab: (t + t_off, 0, 0)),
                      pl.BlockSpec((MOE_TD, D), lambda t, tab: (t, 0)),
                      gt2_spec,
                      pl.BlockSpec((1, D), lambda t, tab: (0, 0)),
                      pl.BlockSpec(memory_space=pl.ANY)],
            out_specs=pl.BlockSpec((MOE_TD, D), lambda t, tab: (t, 0)),
            scratch_shapes=[pltpu.VMEM((2, MOE_BR, D), BF16),
                            pltpu.VMEM((MOE_TD, D), F32),
                            pltpu.SemaphoreType.DMA((2, NE))]),
        out_shape=jax.ShapeDtypeStruct((n_out_tiles * MOE_TD, D), F32),
        compiler_params=pltpu.CompilerParams(dimension_semantics=("arbitrary",),
                                             vmem_limit_bytes=VMEM_LIMIT),
        name="moe_combine",
    )(tab, rt, pos, x1, gt2, g_final.reshape(1, D), ys)


def _moe_plan(cnt, nc_max):
    cnt_al = (cnt + ROW_ALIGN - 1) // ROW_ALIGN * ROW_ALIGN
    seg_rows = cnt_al.sum(axis=0)
    seg_pad = (seg_rows + MOE_TG - 1) // MOE_TG * MOE_TG
    seg_start = jnp.cumsum(seg_pad) - seg_pad
    dst = seg_start[None, :] + jnp.cumsum(cnt_al, axis=0) - cnt_al
    boff = jnp.cumsum(cnt_al, axis=1) - cnt_al
    tile_end = jnp.cumsum(seg_pad // MOE_TG)
    n_active = tile_end[-1:].astype(jnp.int32)
    tab = jnp.concatenate([cnt_al.ravel(), boff.ravel(), dst.ravel(),
                           seg_start + seg_rows, seg_pad - seg_rows, n_active,
                           cnt_al.sum(axis=1)]).astype(jnp.int32)
    nt = seg_pad // MOE_TG
    nfull = nt // MOE_CK
    rem = nt % MOE_CK
    nch = nfull + (rem > 0)
    cend = jnp.cumsum(nch)
    cstart = cend - nch
    c = jnp.arange(nc_max, dtype=jnp.int32)
    ce = jnp.minimum(jnp.sum(c[:, None] >= cend[None, :], axis=1), NE - 1)
    local = c - cstart[ce]
    valid = c < cend[-1]
    ck = jnp.where(valid, jnp.where(local < nfull[ce], MOE_CK, rem[ce]), 0)
    crow = jnp.where(valid, seg_start[ce] + local * (MOE_CK * MOE_TG), 0)
    ctab = jnp.concatenate([cstart, cend[-1:], crow, ck, n_active]).astype(jnp.int32)
    return tab, ctab


def kernel(x_prompt, x_sample, c_prompt, c_sample, state_ssm_re, state_ssm_im, cache_conv, w_ada, b_ada, g_norm_mix, w_in, ssm_a_re, ssm_a_im, ssm_log_dt, ssm_b_re, ssm_b_im, ssm_c_re, ssm_c_im, ssm_d, w_ssm_glu, b_ssm_glu, w_dw, b_dw, ln_conv_g, ln_conv_b, g_out_ssm, g_out_conv, w_out, g_norm_ffn, w_router_grp, b_router_grp, w_router_exp, b_router_exp, w_exp_gate, w_exp_up, w_exp_down, g_final):
    depth = w_ada.shape[0]
    assert depth == 1
    bsz, seq, _ = x_prompt.shape
    nb, nt, _ = x_sample.shape

    n_c = bsz + nb
    c_pad = -n_c % 16
    c_all = jnp.concatenate([c_prompt, c_sample, jnp.zeros((c_pad, D), F32)], axis=0)
    mod_p, mod_s = _mod_call(c_all, w_ada[0], b_ada[0], bsz, nb)
    mod_p = mod_p.reshape(bsz, 6, D)

    ab_re, ab_im, bb_re, bb_im, c_im_neg = _ssm_prep_call(
        ssm_a_re[0], ssm_a_im[0], ssm_log_dt[0], ssm_b_re[0], ssm_b_im[0], ssm_c_im[0])
    wb, wc = _block_diag_weights(bb_re, bb_im, ssm_c_re[0], c_im_neg)
    a_tok = jnp.stack([ab_re.reshape(NCHUNK, LANES), ab_im.reshape(NCHUNK, LANES)])
    a_row = jnp.stack([ab_re.reshape(NSTATE), ab_im.reshape(NSTATE)])

    w_r = jnp.concatenate([w_router_exp[0].reshape(D, NE), w_router_grp[0],
                           jnp.zeros((D, LANES - NE - NG), F32)], axis=1)
    w_r_hi = w_r.astype(BF16)
    w_r = jnp.concatenate([w_r_hi, (w_r - w_r_hi.astype(F32)).astype(BF16)], axis=1)
    b_r = jnp.concatenate([b_router_exp[0].reshape(NE), b_router_grp[0],
                           jnp.zeros((LANES - NE - NG,), F32)]).reshape(1, LANES)
    w_dw_p = jnp.concatenate([w_dw[0], jnp.zeros((1, CONV_CH), F32)], axis=0)

    row = lambda v: v.reshape(1, -1)
    common_a = (row(g_norm_mix[0]), w_in[0].astype(BF16), wb)
    common_b = (wc, row(ssm_d[0].reshape(SSM_W)), w_ssm_glu[0].astype(BF16), row(b_ssm_glu[0]),
                w_dw_p, row(b_dw[0]), row(ln_conv_g[0]), row(ln_conv_b[0]),
                row(g_out_ssm[0]), row(g_out_conv[0]), w_out[0].astype(BF16),
                row(g_norm_ffn[0]), w_r, b_r)

    n_p = bsz * seq
    n_s = nb * nt
    n_all = n_p + n_s
    assert n_s == MOE_TD and seq % MOE_TD == 0 and MOE_TD % PROMPT_TC == 0
    wts_p = common_a + (a_tok,) + common_b
    x1_p, n2_p, rt_p, cnt_p, hr_p, hi_p, cache_p = _prompt_mixer_call(x_prompt, mod_p, wts_p, PROMPT_TC)

    x_tm = jnp.transpose(x_sample, (1, 0, 2)).reshape(nt * nb, D)
    cache_tm = jnp.transpose(cache_conv[0], (1, 0, 2))
    wts_s = common_a + (a_row,) + common_b
    x1_s, n2_s, rt_s, cnt_s, hr_s, hi_s, glu_s = _sample_mixer_call(
        x_tm, mod_s, state_ssm_re[0].reshape(nb, NSTATE), state_ssm_im[0].reshape(nb, NSTATE),
        cache_tm, wts_s, nb, nt)

    n_ptiles = n_p // MOE_TD
    n_tiles = n_all // MOE_TD
    r_max = -(-(2 * n_all + n_tiles * NE * (ROW_ALIGN - 1) + NE * (MOE_TG - ROW_ALIGN)) // MOE_TG) * MOE_TG
    cnt = jnp.concatenate([cnt_p.reshape(n_ptiles, MOE_TD // PROMPT_TC, LANES).sum(axis=1), cnt_s])
    nc_max = r_max // MOE_TG // MOE_CK + NE
    tab, ctab = _moe_plan(cnt[:, :NE].astype(jnp.int32), nc_max)
    xs, pos = _dispatch_call(tab, rt_p, n2_p, rt_s, n2_s, r_max)
    ys = _experts_call(ctab, xs, w_exp_gate[0], w_exp_up[0], w_exp_down[0], nc_max)
    tiles_per_b = seq // MOE_TD
    gt2_p = mod_p[:, 5:6, :]
    y_p = _combine_call(tab, rt_p, pos, x1_p, gt2_p,
                        pl.BlockSpec((None, 1, D), lambda t, tab: (t // tiles_per_b, 0, 0)),
                        g_final, ys, n_tiles, 0)
    y_s = _combine_call(tab, rt_s, pos, x1_s, mod_s,
                        pl.BlockSpec((nb, D), lambda t, tab: (0, 5)),
                        g_final, ys, n_tiles, n_ptiles)

    y_prompt = y_p.reshape(bsz, seq, D)
    y_sample = jnp.transpose(y_s.reshape(nt, nb, D), (1, 0, 2))
    new_cache_s = jnp.concatenate(
        [cache_conv[0][:, nt:, :], jnp.transpose(glu_s.reshape(nt, nb, CONV_CH), (1, 0, 2))], axis=1)
    return (y_prompt, y_sample,
            hr_p.reshape(1, bsz, G, P), hi_p.reshape(1, bsz, G, P), cache_p[None],
            hr_s.reshape(1, nb, G, P), hi_s.reshape(1, nb, G, P), new_cache_s[None])
```

```python
import functools

import jax
import jax.numpy as jnp
import numpy as np
from jax import lax
from jax.experimental import pallas as pl
from jax.experimental.pallas import tpu as pltpu

F32 = jnp.float32
BF16 = jnp.bfloat16

D = 1024
SSM_W = 512
CONV_CH = 512
G = 32
H = 16
P = 64
KW = 31
CB = KW - 1
NE = 32
NG = 4
EPG = 8
DE = 512
EPS = 1e-6
LANES = 128
SUBLANES = 8
NSTATE = G * P
NCHUNK = NSTATE // LANES

PROMPT_TC = 512
SCAN_GROUP = 8
V7X_VMEM_BYTES = 64 * 1024 * 1024
VMEM_LIMIT = V7X_VMEM_BYTES - 8 * 1024 * 1024


def _rms(x):
    return x * lax.rsqrt(jnp.mean(x * x, axis=-1, keepdims=True) + EPS)


def _sigmoid(x):
    return 0.5 * jnp.tanh(0.5 * x) + 0.5


def _gelu_tanh(y):
    c = np.sqrt(2.0 / np.pi).astype(np.float32)
    return y * (0.5 * (1.0 + jnp.tanh(c * (y + 0.044715 * (y * y * y)))))


def _bdot(a, b):
    return jnp.dot(a.astype(BF16), b, preferred_element_type=F32)


def _mod_kernel(c_ref, w_ref, b_ref, op_ref, os_ref):
    c = c_ref[...]
    s = c * _sigmoid(c)
    n = s.shape[0]
    s_hi = s.astype(BF16)
    lhs = jnp.concatenate([s_hi, (s - s_hi.astype(F32)).astype(BF16)], axis=0)
    w = w_ref[...]
    w_hi = w.astype(BF16)
    w_lo = (w - w_hi.astype(F32)).astype(BF16)
    p_hi = jnp.dot(lhs, w_hi, preferred_element_type=F32)
    p_lo = jnp.dot(lhs, w_lo, preferred_element_type=F32)
    res = (p_hi[:n] + p_lo[:n]) + (p_hi[n:] + p_lo[n:]) + b_ref[...]
    n_p = op_ref.shape[0]
    op_ref[...] = res[:n_p]
    os_ref[...] = res[n_p:n_p + os_ref.shape[0]]


def _mod_call(c_all, w_ada, b_ada, n_p, n_s):
    n = c_all.shape[0]
    tn = 512
    return pl.pallas_call(
        _mod_kernel,
        grid=(6 * D // tn,),
        in_specs=[pl.BlockSpec((n, D), lambda j: (0, 0)),
                  pl.BlockSpec((D, tn), lambda j: (0, j)),
                  pl.BlockSpec((1, tn), lambda j: (0, j))],
        out_specs=(pl.BlockSpec((n_p, tn), lambda j: (0, j)),
                   pl.BlockSpec((n_s, tn), lambda j: (0, j))),
        out_shape=(jax.ShapeDtypeStruct((n_p, 6 * D), F32),
                   jax.ShapeDtypeStruct((n_s, 6 * D), F32)),
        compiler_params=pltpu.CompilerParams(dimension_semantics=("arbitrary",)),
        name="mod",
    )(c_all, w_ada, b_ada.reshape(1, 6 * D))


def _ssm_prep_kernel(a_re, a_im, log_dt, b_re, b_im, c_im,
                     ab_re_o, ab_im_o, bb_re_o, bb_im_o, cneg_o):
    lam_re = jnp.minimum(a_re[...], -1e-4)
    lam_im = a_im[...]
    dt = jnp.exp(log_dt[...])
    mag = jnp.exp(lam_re * dt)
    ab_re = mag * jnp.cos(lam_im * dt)
    ab_im = mag * jnp.sin(lam_im * dt)
    den = lam_re * lam_re + lam_im * lam_im
    num_re = ab_re - 1.0
    coef_re = (num_re * lam_re + ab_im * lam_im) / den
    coef_im = (ab_im * lam_re - num_re * lam_im) / den
    ab_re_o[...] = ab_re
    ab_im_o[...] = ab_im
    br = b_re[...]
    bi = b_im[...]
    bb_re_o[...] = coef_re * br - coef_im * bi
    bb_im_o[...] = coef_re * bi + coef_im * br
    cneg_o[...] = -c_im[...]


def _ssm_prep_call(a_re, a_im, log_dt, b_re, b_im, c_im):
    flat = lambda v: v.reshape(1, NSTATE)
    b_hs = lambda v: jnp.transpose(v, (2, 0, 1)).reshape(H, NSTATE)
    dt_row = jnp.broadcast_to(log_dt[:, None], (G, P)).reshape(1, NSTATE)
    ab_re, ab_im, bb_re, bb_im, cneg = pl.pallas_call(
        _ssm_prep_kernel,
        out_shape=(jax.ShapeDtypeStruct((1, NSTATE), F32), jax.ShapeDtypeStruct((1, NSTATE), F32),
                   jax.ShapeDtypeStruct((H, NSTATE), F32), jax.ShapeDtypeStruct((H, NSTATE), F32),
                   jax.ShapeDtypeStruct((G * H, P), F32)),
        name="ssm_prep",
    )(flat(a_re), flat(a_im), dt_row, b_hs(b_re), b_hs(b_im), c_im.reshape(G * H, P))
    ghp = lambda v: jnp.transpose(v.reshape(H, G, P), (1, 0, 2))
    return (ab_re.reshape(G, P), ab_im.reshape(G, P), ghp(bb_re), ghp(bb_im), cneg.reshape(G, H, P))


def _block_diag_weights(bb_re, bb_im, c_re, c_im_neg):
    eye8 = jnp.eye(8, dtype=F32)
    eye4 = jnp.eye(4, dtype=F32)
    eye2 = jnp.eye(2, dtype=F32)

    def wb_part(bb):
        x = bb.reshape(4, 8, H, P)
        return jnp.einsum('qghp,gk->qghkp', x, eye8).reshape(4, 8 * H, 8 * P)

    wb = jnp.concatenate([wb_part(bb_re), wb_part(bb_im)], axis=-1).astype(BF16)

    def wc_part(c):
        x = c.reshape(4, 4, 2, H, P)
        y = jnp.einsum('qjghp,jk,gl->qjgpklh', x, eye4, eye2)
        return y.reshape(NCHUNK, 2 * P, 4 * 2 * H)

    wc = jnp.concatenate([wc_part(c_re), wc_part(c_im_neg)], axis=1).astype(BF16)
    return wb, wc


def _front(x, sc1, sh1, g_mix, w_in_ref):
    n = _rms(x) * g_mix * (1.0 + sc1) + sh1
    proj = _bdot(n, w_in_ref[...])
    u = proj[:, :SSM_W]
    glu = proj[:, SSM_W:SSM_W + CONV_CH] * _sigmoid(proj[:, SSM_W + CONV_CH:])
    return u, glu


def _ssm_out(y_lin, u, d_skip, w_glu_ref, b_glu, g_out_ssm):
    y = _gelu_tanh(y_lin + d_skip * u)
    ys = y * _sigmoid(_bdot(y, w_glu_ref[...]) + b_glu)
    return _rms(ys) * g_out_ssm


def _conv_out(conv, b_dw, ln_g, ln_b, g_out_conv):
    c = conv + b_dw
    mu = jnp.mean(c, axis=-1, keepdims=True)
    cc = c - mu
    var = jnp.mean(cc * cc, axis=-1, keepdims=True)
    ln = cc * lax.rsqrt(var + EPS) * ln_g + ln_b
    yc = ln * _sigmoid(ln)
    return _rms(yc) * g_out_conv


def _route(n2, w_r_ref, b_r):
    rows = n2.shape[0]
    n_hi = n2.astype(BF16)
    n_lo = (n2 - n_hi.astype(F32)).astype(BF16)
    parts = jnp.dot(jnp.concatenate([n_hi, n_lo], axis=0), w_r_ref[...], preferred_element_type=F32)
    lg = (parts[:rows, :LANES] + parts[:rows, LANES:]) + (parts[rows:, :LANES] + parts[rows:, LANES:]) + b_r
    lane = lax.broadcasted_iota(jnp.int32, (rows, LANES), 1).astype(F32)
    ninf = -jnp.inf
    big = 1e9
    gmask = jnp.logical_and(lane >= NE, lane < NE + NG)
    gl = jnp.where(gmask, lg, ninf)
    gmax = jnp.max(gl, axis=-1, keepdims=True)
    gsum = jnp.sum(jnp.where(gmask, jnp.exp(gl - gmax), 0.0), axis=-1, keepdims=True)
    p_top = 1.0 / gsum
    gi = jnp.min(jnp.where(gl == gmax, lane, big), axis=-1, keepdims=True) - NE
    lo = gi * EPG
    emask = jnp.logical_and(lane >= lo, lane < lo + EPG)
    el = jnp.where(emask, lg, ninf)
    m1 = jnp.max(el, axis=-1, keepdims=True)
    i1 = jnp.min(jnp.where(el == m1, lane, big), axis=-1, keepdims=True)
    el2 = jnp.where(lane == i1, ninf, el)
    m2 = jnp.max(el2, axis=-1, keepdims=True)
    i2 = jnp.min(jnp.where(el2 == m2, lane, big), axis=-1, keepdims=True)
    e2 = jnp.exp(m2 - m1)
    den = 1.0 + e2
    w1 = p_top / den
    w2 = p_top * e2 / den
    cnt = jnp.sum(jnp.where(lane == i1, 1.0, 0.0) + jnp.where(lane == i2, 1.0, 0.0),
                  axis=0, keepdims=True)
    cols = (jnp.where(lane == 0.0, i1, 0.0) + jnp.where(lane == 1.0, i2, 0.0)
            + jnp.where(lane == 2.0, w1, 0.0) + jnp.where(lane == 3.0, w2, 0.0))
    return cols.T[0:8, :], cnt


def _tail(x, ns, nc, gt1, sc2, sh2, w_out_ref, g_ffn, w_r_ref, b_r):
    merged = _bdot(jnp.concatenate([ns, nc], axis=-1), w_out_ref[...])
    x1 = x + gt1 * merged
    n2 = _rms(x1) * g_ffn * (1.0 + sc2) + sh2
    rt, cnt = _route(n2, w_r_ref, b_r)
    return x1, n2, rt, cnt


def _prompt_mixer_kernel(x_ref, mod_ref, g_mix_ref, w_in_ref, wb_ref, a_ref, wc_ref, dsk_ref,
                         w_glu_ref, b_glu_ref, w_dw_ref, b_dw_ref, ln_g_ref, ln_b_ref,
                         g_os_ref, g_oc_ref, w_out_ref, g_ffn_ref, w_r_ref, b_r_ref,
                         x1_ref, n2_ref, rt_ref, cnt_ref, hr_ref, hi_ref, cache_ref,
                         sre, sim, hst, ebuf, eshift, ncbuf, *, tc, pt):
    c = pl.program_id(1)
    nc_chunks = pl.num_programs(1)

    @pl.when(c == 0)
    def _():
        hst[...] = jnp.zeros_like(hst)
        ebuf[pl.ds(0, 32), :] = jnp.zeros((32, CONV_CH), F32)

    x = x_ref[...]
    mod = mod_ref[...]
    sh1, sc1, gt1 = mod[0:1], mod[1:2], mod[2:3]
    sh2, sc2 = mod[3:4], mod[4:5]

    u, glu = _front(x, sc1, sh1, g_mix_ref[...], w_in_ref)
    ebuf[pl.ds(32, tc), :] = glu

    for s in range(1, 8):
        eshift[s - 1, pl.ds(0, tc + 24), :] = ebuf[pl.ds(s, tc + 24), :]
    rb = 64
    convs = []
    for r0 in range(0, tc, rb):
        acc = None
        for k in range(KW):
            a8, s = (k + 2) // 8 * 8, (k + 2) % 8
            win = ebuf[pl.ds(r0 + a8, rb), :] if s == 0 else eshift[s - 1, pl.ds(r0 + a8, rb), :]
            term = w_dw_ref[k:k + 1, :] * win
            acc = term if acc is None else acc + term
        convs.append(acc)
    conv = jnp.concatenate(convs, axis=0)
    ncbuf[...] = _conv_out(conv, b_dw_ref[...], ln_g_ref[...], ln_b_ref[...], g_oc_ref[...])

    @pl.when(c == nc_chunks - 1)
    def _():
        cache_ref[...] = ebuf[pl.ds(tc + 2, CB), :]

    ebuf[pl.ds(0, 32), :] = ebuf[pl.ds(tc, 32), :]

    ub = u.astype(BF16)
    for q in range(4):
        r = jnp.dot(ub[:, q * LANES:(q + 1) * LANES], wb_ref[q], preferred_element_type=F32)
        for k in range(4):
            j = 4 * q + k
            sre[pl.ds(j * pt, tc), :] = r[:, k * LANES:(k + 1) * LANES]
            sim[pl.ds(j * pt, tc), :] = r[:, SSM_W + k * LANES:SSM_W + (k + 1) * LANES]

    ar0, ar1 = a_ref[0, 0:8, :], a_ref[0, 8:16, :]
    ai0, ai1 = a_ref[1, 0:8, :], a_ref[1, 8:16, :]

    def step(g, carry):
        hr0, hr1, hi0, hi1 = carry
        t0 = g * SCAN_GROUP
        idx = [(pl.ds(t0 + u, 8, stride=pt), pl.ds(t0 + u + 8 * pt, 8, stride=pt))
               for u in range(SCAN_GROUP)]
        bu = [(sre[i0, :], sre[i1, :], sim[i0, :], sim[i1, :]) for (i0, i1) in idx]
        hs = []
        for br0, br1, bi0, bi1 in bu:
            nr0 = ar0 * hr0 - ai0 * hi0 + br0
            ni0 = ar0 * hi0 + ai0 * hr0 + bi0
            nr1 = ar1 * hr1 - ai1 * hi1 + br1
            ni1 = ar1 * hi1 + ai1 * hr1 + bi1
            hr0, hr1, hi0, hi1 = nr0, nr1, ni0, ni1
            hs.append((nr0, nr1, ni0, ni1))
        for (i0, i1), (nr0, nr1, ni0, ni1) in zip(idx, hs):
            sre[i0, :] = nr0
            sim[i0, :] = ni0
            sre[i1, :] = nr1
            sim[i1, :] = ni1
        return hr0, hr1, hi0, hi1

    init = (hst[0, 0:8, :], hst[0, 8:16, :], hst[1, 0:8, :], hst[1, 8:16, :])
    hr0, hr1, hi0, hi1 = lax.fori_loop(0, tc // SCAN_GROUP, step, init)
    hst[0, 0:8, :] = hr0
    hst[0, 8:16, :] = hr1
    hst[1, 0:8, :] = hi0
    hst[1, 8:16, :] = hi1

    @pl.when(c == nc_chunks - 1)
    def _():
        hr_ref[...] = hst[0]
        hi_ref[...] = hst[1]

    ys = []
    for q in range(4):
        acc = None
        for jj in range(4):
            j = 4 * q + jj
            lhs = jnp.concatenate([sre[pl.ds(j * pt, tc), :], sim[pl.ds(j * pt, tc), :]], axis=-1)
            d = jnp.dot(lhs.astype(BF16), wc_ref[j], preferred_element_type=F32)
            acc = d if acc is None else acc + d
        ys.append(acc)
    y_lin = jnp.concatenate(ys, axis=-1)
    ns = _ssm_out(y_lin, u, dsk_ref[...], w_glu_ref, b_glu_ref[...], g_os_ref[...])
    nc = ncbuf[...]

    x1, n2, rt, cnt = _tail(x, ns, nc, gt1, sc2, sh2, w_out_ref, g_ffn_ref[...], w_r_ref, b_r_ref[...])
    x1_ref[...] = x1
    n2_ref[...] = n2.astype(BF16)
    rt_ref[...] = rt
    cnt_ref[...] = cnt


def _const_spec(shape):
    nd = len(shape)
    return pl.BlockSpec(shape, lambda b, c: (0,) * nd)


def _prompt_mixer_call(x, mod6, wts, tc):
    bsz, t, _ = x.shape
    n_all = bsz * t
    pt = tc + SUBLANES
    assert (pt // SUBLANES) % 2 == 1
    nc = t // tc
    kern = functools.partial(_prompt_mixer_kernel, tc=tc, pt=pt)
    in_specs = [pl.BlockSpec((None, tc, D), lambda b, c: (b, c, 0)),
                pl.BlockSpec((None, 6, D), lambda b, c: (b, 0, 0))]
    in_specs += [_const_spec(w.shape) for w in wts]
    out_shape = (jax.ShapeDtypeStruct((n_all, D), F32),
                 jax.ShapeDtypeStruct((n_all, D), BF16),
                 jax.ShapeDtypeStruct((8, n_all), F32),
                 jax.ShapeDtypeStruct((n_all // tc, 1, LANES), F32),
                 jax.ShapeDtypeStruct((bsz, NCHUNK, LANES), F32),
                 jax.ShapeDtypeStruct((bsz, NCHUNK, LANES), F32),
                 jax.ShapeDtypeStruct((bsz, CB, CONV_CH), F32))
    out_specs = (pl.BlockSpec((tc, D), lambda b, c: (b * nc + c, 0)),
                 pl.BlockSpec((tc, D), lambda b, c: (b * nc + c, 0)),
                 pl.BlockSpec((8, tc), lambda b, c: (0, b * nc + c)),
                 pl.BlockSpec((None, 1, LANES), lambda b, c: (b * nc + c, 0, 0)),
                 pl.BlockSpec((None, NCHUNK, LANES), lambda b, c: (b, 0, 0)),
                 pl.BlockSpec((None, NCHUNK, LANES), lambda b, c: (b, 0, 0)),
                 pl.BlockSpec((None, CB, CONV_CH), lambda b, c: (b, 0, 0)))
    scratch = [pltpu.VMEM((NCHUNK * pt, LANES), F32),
               pltpu.VMEM((NCHUNK * pt, LANES), F32),
               pltpu.VMEM((2, NCHUNK, LANES), F32),
               pltpu.VMEM((tc + 32, CONV_CH), F32),
               pltpu.VMEM((7, tc + 32, CONV_CH), F32),
               pltpu.VMEM((tc, CONV_CH), F32)]
    return pl.pallas_call(
        kern, grid=(bsz, nc), in_specs=in_specs, out_specs=out_specs, out_shape=out_shape,
        scratch_shapes=scratch,
        compiler_params=pltpu.CompilerParams(dimension_semantics=("arbitrary", "arbitrary"),
                                             vmem_limit_bytes=VMEM_LIMIT),
        name="prompt_mixer",
    )(x, mod6, *wts)


def _sample_mixer_kernel(x_ref, mod_ref, h0r_ref, h0i_ref, cache_ref,
                         g_mix_ref, w_in_ref, wb_ref, a_ref, wc_ref, dsk_ref,
                         w_glu_ref, b_glu_ref, w_dw_ref, b_dw_ref, ln_g_ref, ln_b_ref,
                         g_os_ref, g_oc_ref, w_out_ref, g_ffn_ref, w_r_ref, b_r_ref,
                         x1_ref, n2_ref, rt_ref, cnt_ref, hr_ref, hi_ref, glu_ref,
                         sre, sim, *, nb, nt):
    x = x_ref[...]

    def rows(i):
        m = mod_ref[:, i * D:(i + 1) * D]
        return jnp.concatenate([m] * nt, axis=0)

    sh1, sc1, gt1, sh2, sc2 = rows(0), rows(1), rows(2), rows(3), rows(4)
    u, glu = _front(x, sc1, sh1, g_mix_ref[...], w_in_ref)
    glu_ref[...] = glu

    ub = u.astype(BF16)
    for q in range(4):
        r = jnp.dot(ub[:, q * LANES:(q + 1) * LANES], wb_ref[q], preferred_element_type=F32)
        sre[:, q * SSM_W:(q + 1) * SSM_W] = r[:, :SSM_W]
        sim[:, q * SSM_W:(q + 1) * SSM_W] = r[:, SSM_W:]

    ar = a_ref[0:1, :]
    ai = a_ref[1:2, :]
    hr = h0r_ref[...]
    hi = h0i_ref[...]
    for t in range(nt):
        rs = pl.ds(t * nb, nb)
        nr = ar * hr - ai * hi + sre[rs, :]
        ni = ar * hi + ai * hr + sim[rs, :]
        sre[rs, :] = nr
        sim[rs, :] = ni
        hr, hi = nr, ni
    hr_ref[...] = hr
    hi_ref[...] = hi

    ys = []
    for q in range(4):
        acc = None
        for jj in range(4):
            j = 4 * q + jj
            lhs = jnp.concatenate([sre[:, j * LANES:(j + 1) * LANES],
                                   sim[:, j * LANES:(j + 1) * LANES]], axis=-1)
            d = jnp.dot(lhs.astype(BF16), wc_ref[j], preferred_element_type=F32)
            acc = d if acc is None else acc + d
        ys.append(acc)
    y_lin = jnp.concatenate(ys, axis=-1)
    ns = _ssm_out(y_lin, u, dsk_ref[...], w_glu_ref, b_glu_ref[...], g_os_ref[...])

    def ext(jrow):
        if jrow < CB:
            return cache_ref[jrow]
        return glu[(jrow - CB) * nb:(jrow - CB + 1) * nb, :]

    convs = []
    for t in range(nt):
        acc = None
        for k in range(KW):
            term = w_dw_ref[k:k + 1, :] * ext(t + k)
            acc = term if acc is None else acc + term
        convs.append(acc)
    conv = jnp.concatenate(convs, axis=0)
    nc = _conv_out(conv, b_dw_ref[...], ln_g_ref[...], ln_b_ref[...], g_oc_ref[...])

    x1, n2, rt, cnt = _tail(x, ns, nc, gt1, sc2, sh2, w_out_ref, g_ffn_ref[...], w_r_ref, b_r_ref[...])
    x1_ref[...] = x1
    n2_ref[...] = n2.astype(BF16)
    rt_ref[...] = rt
    cnt_ref[...] = cnt


def _sample_mixer_call(x_tm, mod_s, h0r, h0i, cache_tm, wts, nb, nt):
    n = nb * nt
    kern = functools.partial(_sample_mixer_kernel, nb=nb, nt=nt)
    out_shape = (jax.ShapeDtypeStruct((n, D), F32),
                 jax.ShapeDtypeStruct((n, D), BF16),
                 jax.ShapeDtypeStruct((8, n), F32),
                 jax.ShapeDtypeStruct((1, LANES), F32),
                 jax.ShapeDtypeStruct((nb, NSTATE), F32),
                 jax.ShapeDtypeStruct((nb, NSTATE), F32),
                 jax.ShapeDtypeStruct((n, CONV_CH), F32))
    scratch = [pltpu.VMEM((n, NSTATE), F32), pltpu.VMEM((n, NSTATE), F32)]
    return pl.pallas_call(
        kern, out_shape=out_shape, scratch_shapes=scratch,
        compiler_params=pltpu.CompilerParams(vmem_limit_bytes=VMEM_LIMIT),
        name="sample_mixer",
    )(x_tm, mod_s, h0r, h0i, cache_tm, *wts)


ROW_ALIGN = 16
MOE_TD = 512
MOE_BR = MOE_TD * 2 + NE * ROW_ALIGN
MOE_TG = 256


def _slot_positions(rt):
    t = rt.shape[1]
    e0 = rt[0:1, :]
    e1 = rt[1:2, :]
    sub = lax.broadcasted_iota(jnp.int32, (LANES, t), 0).astype(F32)
    a0 = jnp.where(sub == e0, 1.0, 0.0)
    a1 = jnp.where(sub == e1, 1.0, 0.0)
    at = a0 + a1
    r = lax.broadcasted_iota(jnp.int32, (t, t), 0)
    c = lax.broadcasted_iota(jnp.int32, (t, t), 1)
    before = jnp.where(r < c, 1.0, 0.0).astype(BF16)
    rank = jnp.dot(at.astype(BF16), before, preferred_element_type=F32)
    cnt = jnp.sum(at, axis=1, keepdims=True)
    cnt_al = jnp.maximum(jnp.ceil(cnt * (1.0 / ROW_ALIGN)), 1.0) * float(ROW_ALIGN)
    er = lax.broadcasted_iota(jnp.int32, (LANES, LANES), 0)
    ec = lax.broadcasted_iota(jnp.int32, (LANES, LANES), 1)
    lower = jnp.where(ec < er, 1.0, 0.0)
    base = jnp.dot(lower, jnp.broadcast_to(cnt_al, (LANES, LANES)), preferred_element_type=F32,
                   precision=lax.Precision.HIGHEST)[:, 0:1]
    slot = rank + base
    pos0 = jnp.sum(a0 * slot, axis=0, keepdims=True)
    pos1 = jnp.sum(a1 * slot, axis=0, keepdims=True)
    return pos0, pos1


def _segment_copies(tab_ref, t, n_tiles, buf, hbm, sems, slot, to_hbm, wait):
    for e in range(NE):
        n = pl.multiple_of(tab_ref[t * NE + e], ROW_ALIGN)
        b = pl.multiple_of(tab_ref[(n_tiles + t) * NE + e], ROW_ALIGN)
        d = pl.multiple_of(tab_ref[(2 * n_tiles + t) * NE + e], ROW_ALIGN)
        vm = buf.at[slot, pl.ds(b, n)]
        hb = hbm.at[pl.ds(d, n)]
        cp = pltpu.make_async_copy(vm, hb, sems.at[slot, e]) if to_hbm else \
            pltpu.make_async_copy(hb, vm, sems.at[slot, e])

        if wait:
            cp.wait()
        else:
            cp.start()


def _tile_rows(tab_ref, t, n_tiles):
    return tab_ref[3 * n_tiles * NE + 2 * NE + 1 + t]


def _one_hot_rows(r0, nrows, pos0, pos1):
    row = (lax.broadcasted_iota(jnp.int32, (nrows, MOE_TD), 0) + r0).astype(F32)
    return row == pos0, row == pos1


MOE_BLK = 256


def _dispatch_kernel(tab_ref, rtp_ref, n2p_ref, rts_ref, n2s_ref, xs_ref, pos_ref, buf, zbuf, sems, zsem,
                     *, n_tiles, n_ptiles, n_gtiles):
    t = pl.program_id(0)
    slot = lax.rem(t, 2)
    first_free = tab_ref[3 * n_tiles * NE + 2 * NE]

    def fill_copy(j):
        d = pl.multiple_of(j * MOE_TG, MOE_TG)
        return pltpu.make_async_copy(zbuf, xs_ref.at[pl.ds(d, MOE_TG)], zsem)

    def fill_start(j, carry):
        fill_copy(j).start()
        return carry

    def fill_wait(j, carry):
        fill_copy(j).wait()
        return carry

    @pl.when(t == 0)
    def _():
        zbuf[...] = jnp.zeros_like(zbuf)
        for phase in range(2):
            for e in range(NE):
                d = pl.multiple_of(tab_ref[3 * n_tiles * NE + e], ROW_ALIGN)
                n = pl.multiple_of(tab_ref[3 * n_tiles * NE + NE + e], ROW_ALIGN)
                cp = pltpu.make_async_copy(zbuf.at[pl.ds(0, n)], xs_ref.at[pl.ds(d, n)], sems.at[1, e])

                @pl.when(n > 0)
                def _():
                    if phase == 0:
                        cp.start()
                    else:
                        cp.wait()

        lax.fori_loop(first_free, n_gtiles, fill_start, 0)

    @pl.when(t >= 2)
    def _():
        _segment_copies(tab_ref, t - 2, n_tiles, buf, xs_ref, sems, slot, to_hbm=True, wait=True)

    is_sample = t >= n_ptiles
    rt = jnp.where(is_sample, rts_ref[...], rtp_ref[...])
    n2 = jnp.where(is_sample, n2s_ref[...], n2p_ref[...])
    pos0, pos1 = _slot_positions(rt)
    pos_ref[...] = jnp.concatenate([pos0, pos1, jnp.zeros((6, MOE_TD), F32)], axis=0)
    used = _tile_rows(tab_ref, t, n_tiles)

    def group(r0, nrows):
        m0, m1 = _one_hot_rows(r0, nrows, pos0, pos1)
        q = (jnp.where(m0, 1.0, 0.0) + jnp.where(m1, 1.0, 0.0)).astype(BF16)
        buf[slot, pl.ds(r0, nrows), :] = jnp.dot(q, n2, preferred_element_type=F32).astype(BF16)

    group(0, 2 * MOE_TD)
    for r0 in range(2 * MOE_TD, MOE_BR, MOE_BLK):
        @pl.when(used > r0)
        def _():
            group(r0, MOE_BLK)

    _segment_copies(tab_ref, t, n_tiles, buf, xs_ref, sems, slot, to_hbm=True, wait=False)

    @pl.when(t == n_tiles - 1)
    def _():
        if n_tiles >= 2:
            _segment_copies(tab_ref, t - 1, n_tiles, buf, xs_ref, sems, 1 - slot, to_hbm=True, wait=True)
        _segment_copies(tab_ref, t, n_tiles, buf, xs_ref, sems, slot, to_hbm=True, wait=True)
        lax.fori_loop(first_free, n_gtiles, fill_wait, 0)


def _dispatch_call(tab, rt_p, n2_p, rt_s, n2_s, r_max):
    n_ptiles = n2_p.shape[0] // MOE_TD
    n_tiles = n_ptiles + n2_s.shape[0] // MOE_TD
    last_p = n_ptiles - 1
    return pl.pallas_call(
        functools.partial(_dispatch_kernel, n_tiles=n_tiles, n_ptiles=n_ptiles,
                          n_gtiles=r_max // MOE_TG),
        grid_spec=pltpu.PrefetchScalarGridSpec(
            num_scalar_prefetch=1, grid=(n_tiles,),
            in_specs=[pl.BlockSpec((8, MOE_TD), lambda t, tab: (0, jnp.minimum(t, last_p))),
                      pl.BlockSpec((MOE_TD, D), lambda t, tab: (jnp.minimum(t, last_p), 0)),
                      pl.BlockSpec((8, MOE_TD), lambda t, tab: (0, 0)),
                      pl.BlockSpec((MOE_TD, D), lambda t, tab: (0, 0))],
            out_specs=(pl.BlockSpec(memory_space=pl.ANY),
                       pl.BlockSpec((None, 8, MOE_TD), lambda t, tab: (t, 0, 0))),
            scratch_shapes=[pltpu.VMEM((2, MOE_BR, D), BF16),
                            pltpu.VMEM((MOE_TG, D), BF16),
                            pltpu.SemaphoreType.DMA((2, NE)),
                            pltpu.SemaphoreType.DMA(())]),
        out_shape=(jax.ShapeDtypeStruct((r_max, D), BF16),
                   jax.ShapeDtypeStruct((n_tiles, 8, MOE_TD), F32)),
        compiler_params=pltpu.CompilerParams(dimension_semantics=("arbitrary",),
                                             vmem_limit_bytes=VMEM_LIMIT),
        name="moe_dispatch",
    )(tab, rt_p, n2_p, rt_s, n2_s)


MOE_CK = 6


def _experts_kernel(ctab_ref, xs_ref, w1_ref, w3_ref, w2_ref, ys_ref,
                    w1b, w3b, w2b, xbuf, ybuf, zbuf, in_sem, out_sem, zsem, *, n_gtiles, nc_max):
    e = pl.program_id(0)
    c0 = ctab_ref[e]
    c1 = ctab_ref[e + 1]
    n_chunks = ctab_ref[NE]
    first_free = ctab_ref[NE + 1 + 2 * nc_max]

    def fill_copy(j):
        d = pl.multiple_of(j * MOE_TG, MOE_TG)
        return pltpu.make_async_copy(zbuf, ys_ref.at[pl.ds(d, MOE_TG)], zsem)

    def fill_start(j, carry):
        fill_copy(j).start()
        return carry

    def fill_wait(j, carry):
        fill_copy(j).wait()
        return carry

    def span(c):
        r = pl.multiple_of(ctab_ref[NE + 1 + c], MOE_TG)
        n = pl.multiple_of(ctab_ref[NE + 1 + nc_max + c] * MOE_TG, MOE_TG)
        return r, n

    def in_copy(c, slot):
        r, n = span(c)
        return pltpu.make_async_copy(xs_ref.at[pl.ds(r, n)], xbuf.at[slot, pl.ds(0, n)], in_sem.at[slot])

    def out_copy(c, slot):
        r, n = span(c)
        return pltpu.make_async_copy(ybuf.at[slot, pl.ds(0, n)], ys_ref.at[pl.ds(r, n)], out_sem.at[slot])

    @pl.when(e == 0)
    def _():
        zbuf[...] = jnp.zeros_like(zbuf)
        lax.fori_loop(first_free, n_gtiles, fill_start, 0)

        @pl.when(n_chunks > 0)
        def _():
            in_copy(0, 0).start()

    def compute(slot, rows):
        x = xbuf[slot, pl.ds(0, rows), :]
        a = jnp.dot(x, w1b[...], preferred_element_type=F32)
        b = jnp.dot(x, w3b[...], preferred_element_type=F32)
        hid = a * _sigmoid(a) * b
        y = jnp.dot(hid.astype(BF16), w2b[...], preferred_element_type=F32)
        ybuf[slot, pl.ds(0, rows), :] = y.astype(BF16)

    @pl.when(c1 > c0)
    def _():
        w1b[...] = w1_ref[...].astype(BF16)
        w3b[...] = w3_ref[...].astype(BF16)
        w2b[...] = w2_ref[...].astype(BF16)

        def chunk(c, carry):
            slot = lax.rem(c, 2)

            @pl.when(c + 1 < n_chunks)
            def _():
                in_copy(c + 1, 1 - slot).start()

            in_copy(c, slot).wait()

            @pl.when(c >= 2)
            def _():
                out_copy(c - 2, slot).wait()

            k = ctab_ref[NE + 1 + nc_max + c]
            for kk in range(1, MOE_CK + 1):
                @pl.when(k == kk)
                def _():
                    compute(slot, kk * MOE_TG)

            out_copy(c, slot).start()
            return carry

        lax.fori_loop(c0, c1, chunk, 0)

    @pl.when(e == NE - 1)
    def _():
        @pl.when(n_chunks >= 2)
        def _():
            out_copy(n_chunks - 2, lax.rem(n_chunks, 2)).wait()

        @pl.when(n_chunks >= 1)
        def _():
            out_copy(n_chunks - 1, lax.rem(n_chunks - 1, 2)).wait()

        lax.fori_loop(first_free, n_gtiles, fill_wait, 0)


def _experts_call(ctab, xs, w1, w3, w2, nc_max):
    r_max = xs.shape[0]
    w_map = lambda e, ctab: (e, 0, 0)
    ring = pltpu.VMEM((2, MOE_CK * MOE_TG, D), BF16)
    return pl.pallas_call(
        functools.partial(_experts_kernel, n_gtiles=r_max // MOE_TG, nc_max=nc_max),
        grid_spec=pltpu.PrefetchScalarGridSpec(
            num_scalar_prefetch=1, grid=(NE,),
            in_specs=[pl.BlockSpec(memory_space=pl.ANY),
                      pl.BlockSpec((None, D, DE), w_map),
                      pl.BlockSpec((None, D, DE), w_map),
                      pl.BlockSpec((None, DE, D), w_map)],
            out_specs=pl.BlockSpec(memory_space=pl.ANY),
            scratch_shapes=[pltpu.VMEM((D, DE), BF16), pltpu.VMEM((D, DE), BF16),
                            pltpu.VMEM((DE, D), BF16), ring, ring,
                            pltpu.VMEM((MOE_TG, D), BF16),
                            pltpu.SemaphoreType.DMA((2,)), pltpu.SemaphoreType.DMA((2,)),
                            pltpu.SemaphoreType.DMA(())]),
        out_shape=jax.ShapeDtypeStruct((r_max, D), BF16),
        compiler_params=pltpu.CompilerParams(dimension_semantics=("arbitrary",),
                                             vmem_limit_bytes=VMEM_LIMIT),
        name="moe_experts",
    )(ctab, xs, w1, w3, w2)


def _combine_kernel(tab_ref, rt_ref, pos_ref, x1_ref, gt2_ref, gf_ref, ys_ref, y_ref, buf, acc, sems,
                    *, n_tiles, t_off):
    i = pl.program_id(0)
    t = i + t_off
    slot = lax.rem(i, 2)

    @pl.when(i == 0)
    def _():
        buf[...] = jnp.zeros_like(buf)
        _segment_copies(tab_ref, t, n_tiles, buf, ys_ref, sems, slot, to_hbm=False, wait=False)

    @pl.when(i + 1 < pl.num_programs(0))
    def _():
        _segment_copies(tab_ref, t + 1, n_tiles, buf, ys_ref, sems, 1 - slot, to_hbm=False, wait=False)

    _segment_copies(tab_ref, t, n_tiles, buf, ys_ref, sems, slot, to_hbm=False, wait=True)

    rt = rt_ref[...]
    pos0 = pos_ref[0:1, :]
    pos1 = pos_ref[1:2, :]
    used = _tile_rows(tab_ref, t, n_tiles)

    def ungroup(r0, nrows):
        m0, m1 = _one_hot_rows(r0, nrows, pos0, pos1)
        q = (jnp.where(m0, 1.0, 0.0) + jnp.where(m1, 1.0, 0.0)).astype(BF16)
        gw = jnp.sum(jnp.where(m0, rt[2:3, :], 0.0) + jnp.where(m1, rt[3:4, :], 0.0),
                     axis=1, keepdims=True)
        yv = (buf[slot, pl.ds(r0, nrows), :].astype(F32) * gw).astype(BF16)
        return lax.dot_general(q, yv, (((0,), (0,)), ((), ())), preferred_element_type=F32)

    acc[...] = ungroup(0, 2 * MOE_TD)
    for r0 in range(2 * MOE_TD, MOE_BR, MOE_BLK):
        @pl.when(used > r0)
        def _():
            acc[...] += ungroup(r0, MOE_BLK)

    gt2 = gt2_ref[...]
    if gt2.shape[0] not in (1, MOE_TD):
        gt2 = jnp.concatenate([gt2] * (MOE_TD // gt2.shape[0]), axis=0)
    xo = x1_ref[...] + gt2 * acc[...]
    y_ref[...] = _rms(xo) * gf_ref[...]


def _combine_call(tab, rt, pos, x1, gt2, gt2_spec, g_final, ys, n_tiles, t_off):
    n_out_tiles = x1.shape[0] // MOE_TD
    return pl.pallas_call(
        functools.partial(_combine_kernel, n_tiles=n_tiles, t_off=t_off),
        grid_spec=pltpu.PrefetchScalarGridSpec(
            num_scalar_prefetch=1, grid=(n_out_tiles,),
            in_specs=[pl.BlockSpec((8, MOE_TD), lambda t, tab: (0, t)),
                      pl.BlockSpec((None, 8, MOE_TD), lambda t, tab: (t + t_off, 0, 0)),
                      pl.BlockSpec((MOE_TD, D), lambda t, tab: (t, 0)),
                      gt2_spec,
                      pl.BlockSpec((1, D), lambda t, tab: (0, 0)),
                      pl.BlockSpec(memory_space=pl.ANY)],
            out_specs=pl.BlockSpec((MOE_TD, D), lambda t, tab: (t, 0)),
            scratch_shapes=[pltpu.VMEM((2, MOE_BR, D), BF16),
                            pltpu.VMEM((MOE_TD, D), F32),
                            pltpu.SemaphoreType.DMA((2, NE))]),
        out_shape=jax.ShapeDtypeStruct((n_out_tiles * MOE_TD, D), F32),
        compiler_params=pltpu.CompilerParams(dimension_semantics=("arbitrary",),
                                             vmem_limit_bytes=VMEM_LIMIT),
        name="moe_combine",
    )(tab, rt, pos, x1, gt2, g_final.reshape(1, D), ys)


def _moe_plan(cnt, nc_max):
    cnt_al = jnp.maximum((cnt + ROW_ALIGN - 1) // ROW_ALIGN, 1) * ROW_ALIGN
    seg_rows = cnt_al.sum(axis=0)
    seg_pad = (seg_rows + MOE_TG - 1) // MOE_TG * MOE_TG
    seg_start = jnp.cumsum(seg_pad) - seg_pad
    dst = seg_start[None, :] + jnp.cumsum(cnt_al, axis=0) - cnt_al
    boff = jnp.cumsum(cnt_al, axis=1) - cnt_al
    tile_end = jnp.cumsum(seg_pad // MOE_TG)
    n_active = tile_end[-1:].astype(jnp.int32)
    tab = jnp.concatenate([cnt_al.ravel(), boff.ravel(), dst.ravel(),
                           seg_start + seg_rows, seg_pad - seg_rows, n_active,
                           cnt_al.sum(axis=1)]).astype(jnp.int32)
    nt = seg_pad // MOE_TG
    nfull = nt // MOE_CK
    rem = nt % MOE_CK
    nch = nfull + (rem > 0)
    cend = jnp.cumsum(nch)
    cstart = cend - nch
    c = jnp.arange(nc_max, dtype=jnp.int32)
    ce = jnp.minimum(jnp.sum(c[:, None] >= cend[None, :], axis=1), NE - 1)
    local = c - cstart[ce]
    valid = c < cend[-1]
    ck = jnp.where(valid, jnp.where(local < nfull[ce], MOE_CK, rem[ce]), 0)
    crow = jnp.where(valid, seg_start[ce] + local * (MOE_CK * MOE_TG), 0)
    ctab = jnp.concatenate([cstart, cend[-1:], crow, ck, n_active]).astype(jnp.int32)
    return tab, ctab


def kernel(x_prompt, x_sample, c_prompt, c_sample, state_ssm_re, state_ssm_im, cache_conv, w_ada, b_ada, g_norm_mix, w_in, ssm_a_re, ssm_a_im, ssm_log_dt, ssm_b_re, ssm_b_im, ssm_c_re, ssm_c_im, ssm_d, w_ssm_glu, b_ssm_glu, w_dw, b_dw, ln_conv_g, ln_conv_b, g_out_ssm, g_out_conv, w_out, g_norm_ffn, w_router_grp, b_router_grp, w_router_exp, b_router_exp, w_exp_gate, w_exp_up, w_exp_down, g_final):
    depth = w_ada.shape[0]
    assert depth == 1
    bsz, seq, _ = x_prompt.shape
    nb, nt, _ = x_sample.shape

    n_c = bsz + nb
    c_pad = -n_c % 16
    c_all = jnp.concatenate([c_prompt, c_sample, jnp.zeros((c_pad, D), F32)], axis=0)
    mod_p, mod_s = _mod_call(c_all, w_ada[0], b_ada[0], bsz, nb)
    mod_p = mod_p.reshape(bsz, 6, D)

    ab_re, ab_im, bb_re, bb_im, c_im_neg = _ssm_prep_call(
        ssm_a_re[0], ssm_a_im[0], ssm_log_dt[0], ssm_b_re[0], ssm_b_im[0], ssm_c_im[0])
    wb, wc = _block_diag_weights(bb_re, bb_im, ssm_c_re[0], c_im_neg)
    a_tok = jnp.stack([ab_re.reshape(NCHUNK, LANES), ab_im.reshape(NCHUNK, LANES)])
    a_row = jnp.stack([ab_re.reshape(NSTATE), ab_im.reshape(NSTATE)])

    w_r = jnp.concatenate([w_router_exp[0].reshape(D, NE), w_router_grp[0],
                           jnp.zeros((D, LANES - NE - NG), F32)], axis=1)
    w_r_hi = w_r.astype(BF16)
    w_r = jnp.concatenate([w_r_hi, (w_r - w_r_hi.astype(F32)).astype(BF16)], axis=1)
    b_r = jnp.concatenate([b_router_exp[0].reshape(NE), b_router_grp[0],
                           jnp.zeros((LANES - NE - NG,), F32)]).reshape(1, LANES)
    w_dw_p = jnp.concatenate([w_dw[0], jnp.zeros((1, CONV_CH), F32)], axis=0)

    row = lambda v: v.reshape(1, -1)
    common_a = (row(g_norm_mix[0]), w_in[0].astype(BF16), wb)
    common_b = (wc, row(ssm_d[0].reshape(SSM_W)), w_ssm_glu[0].astype(BF16), row(b_ssm_glu[0]),
                w_dw_p, row(b_dw[0]), row(ln_conv_g[0]), row(ln_conv_b[0]),
                row(g_out_ssm[0]), row(g_out_conv[0]), w_out[0].astype(BF16),
                row(g_norm_ffn[0]), w_r, b_r)

    n_p = bsz * seq
    n_s = nb * nt
    n_all = n_p + n_s
    assert n_s == MOE_TD and seq % MOE_TD == 0 and MOE_TD % PROMPT_TC == 0
    wts_p = common_a + (a_tok,) + common_b
    x1_p, n2_p, rt_p, cnt_p, hr_p, hi_p, cache_p = _prompt_mixer_call(x_prompt, mod_p, wts_p, PROMPT_TC)

    x_tm = jnp.transpose(x_sample, (1, 0, 2)).reshape(nt * nb, D)
    cache_tm = jnp.transpose(cache_conv[0], (1, 0, 2))
    wts_s = common_a + (a_row,) + common_b
    x1_s, n2_s, rt_s, cnt_s, hr_s, hi_s, glu_s = _sample_mixer_call(
        x_tm, mod_s, state_ssm_re[0].reshape(nb, NSTATE), state_ssm_im[0].reshape(nb, NSTATE),
        cache_tm, wts_s, nb, nt)

    n_ptiles = n_p // MOE_TD
    n_tiles = n_all // MOE_TD
    r_max = -(-(2 * n_all + n_tiles * NE * ROW_ALIGN + NE * (MOE_TG - ROW_ALIGN)) // MOE_TG) * MOE_TG
    cnt = jnp.concatenate([cnt_p.reshape(n_ptiles, MOE_TD // PROMPT_TC, LANES).sum(axis=1), cnt_s])
    nc_max = r_max // MOE_TG // MOE_CK + NE
    tab, ctab = _moe_plan(cnt[:, :NE].astype(jnp.int32), nc_max)
    xs, pos = _dispatch_call(tab, rt_p, n2_p, rt_s, n2_s, r_max)
    ys = _experts_call(ctab, xs, w_exp_gate[0], w_exp_up[0], w_exp_down[0], nc_max)
    tiles_per_b = seq // MOE_TD
    gt2_p = mod_p[:, 5:6, :]
    y_p = _combine_call(tab, rt_p, pos, x1_p, gt2_p,
                        pl.BlockSpec((None, 1, D), lambda t, tab: (t // tiles_per_b, 0, 0)),
                        g_final, ys, n_tiles, 0)
    y_s = _combine_call(tab, rt_s, pos, x1_s, mod_s,
                        pl.BlockSpec((nb, D), lambda t, tab: (0, 5)),
                        g_final, ys, n_tiles, n_ptiles)

    y_prompt = y_p.reshape(bsz, seq, D)
    y_sample = jnp.transpose(y_s.reshape(nt, nb, D), (1, 0, 2))
    new_cache_s = jnp.concatenate(
        [cache_conv[0][:, nt:, :], jnp.transpose(glu_s.reshape(nt, nb, CONV_CH), (1, 0, 2))], axis=1)
    return (y_prompt, y_sample,
            hr_p.reshape(1, bsz, G, P), hi_p.reshape(1, bsz, G, P), cache_p[None],
            hr_s.reshape(1, nb, G, P), hi_s.reshape(1, nb, G, P), new_cache_s[None])
```

```python
import functools

import jax
import jax.numpy as jnp
import numpy as np
from jax import lax
from jax.experimental import pallas as pl
from jax.experimental.pallas import tpu as pltpu

F32 = jnp.float32
BF16 = jnp.bfloat16

D = 1024
SSM_W = 512
CONV_CH = 512
G = 32
H = 16
P = 64
KW = 31
CB = KW - 1
NE = 32
NG = 4
EPG = 8
DE = 512
EPS = 1e-6
LANES = 128
SUBLANES = 8
NSTATE = G * P
NCHUNK = NSTATE // LANES

PROMPT_TC = 512
SCAN_GROUP = 8
V7X_VMEM_BYTES = 64 * 1024 * 1024
VMEM_LIMIT = V7X_VMEM_BYTES - 8 * 1024 * 1024


def _rms(x):
    return x * lax.rsqrt(jnp.mean(x * x, axis=-1, keepdims=True) + EPS)


def _sigmoid(x):
    return 0.5 * jnp.tanh(0.5 * x) + 0.5


def _gelu_tanh(y):
    c = np.sqrt(2.0 / np.pi).astype(np.float32)
    return y * (0.5 * (1.0 + jnp.tanh(c * (y + 0.044715 * (y * y * y)))))


def _bdot(a, b):
    return jnp.dot(a.astype(BF16), b, preferred_element_type=F32)


def _mod_kernel(c_ref, w_ref, b_ref, op_ref, os_ref):
    c = c_ref[...]
    s = c * _sigmoid(c)
    n = s.shape[0]
    s_hi = s.astype(BF16)
    lhs = jnp.concatenate([s_hi, (s - s_hi.astype(F32)).astype(BF16)], axis=0)
    w = w_ref[...]
    w_hi = w.astype(BF16)
    w_lo = (w - w_hi.astype(F32)).astype(BF16)
    p_hi = jnp.dot(lhs, w_hi, preferred_element_type=F32)
    p_lo = jnp.dot(lhs, w_lo, preferred_element_type=F32)
    res = (p_hi[:n] + p_lo[:n]) + (p_hi[n:] + p_lo[n:]) + b_ref[...]
    n_p = op_ref.shape[0]
    op_ref[...] = res[:n_p]
    os_ref[...] = res[n_p:n_p + os_ref.shape[0]]


def _mod_call(c_all, w_ada, b_ada, n_p, n_s):
    n = c_all.shape[0]
    tn = 512
    return pl.pallas_call(
        _mod_kernel,
        grid=(6 * D // tn,),
        in_specs=[pl.BlockSpec((n, D), lambda j: (0, 0)),
                  pl.BlockSpec((D, tn), lambda j: (0, j)),
                  pl.BlockSpec((1, tn), lambda j: (0, j))],
        out_specs=(pl.BlockSpec((n_p, tn), lambda j: (0, j)),
                   pl.BlockSpec((n_s, tn), lambda j: (0, j))),
        out_shape=(jax.ShapeDtypeStruct((n_p, 6 * D), F32),
                   jax.ShapeDtypeStruct((n_s, 6 * D), F32)),
        compiler_params=pltpu.CompilerParams(dimension_semantics=("arbitrary",)),
        name="mod",
    )(c_all, w_ada, b_ada.reshape(1, 6 * D))


def _ssm_prep_kernel(a_re, a_im, log_dt, b_re, b_im, c_im,
                     ab_re_o, ab_im_o, bb_re_o, bb_im_o, cneg_o):
    lam_re = jnp.minimum(a_re[...], -1e-4)
    lam_im = a_im[...]
    dt = jnp.exp(log_dt[...])
    mag = jnp.exp(lam_re * dt)
    ab_re = mag * jnp.cos(lam_im * dt)
    ab_im = mag * jnp.sin(lam_im * dt)
    den = lam_re * lam_re + lam_im * lam_im
    num_re = ab_re - 1.0
    coef_re = (num_re * lam_re + ab_im * lam_im) / den
    coef_im = (ab_im * lam_re - num_re * lam_im) / den
    ab_re_o[...] = ab_re
    ab_im_o[...] = ab_im
    br = b_re[...]
    bi = b_im[...]
    bb_re_o[...] = coef_re * br - coef_im * bi
    bb_im_o[...] = coef_re * bi + coef_im * br
    cneg_o[...] = -c_im[...]


def _ssm_prep_call(a_re, a_im, log_dt, b_re, b_im, c_im):
    flat = lambda v: v.reshape(1, NSTATE)
    b_hs = lambda v: jnp.transpose(v, (2, 0, 1)).reshape(H, NSTATE)
    dt_row = jnp.broadcast_to(log_dt[:, None], (G, P)).reshape(1, NSTATE)
    ab_re, ab_im, bb_re, bb_im, cneg = pl.pallas_call(
        _ssm_prep_kernel,
        out_shape=(jax.ShapeDtypeStruct((1, NSTATE), F32), jax.ShapeDtypeStruct((1, NSTATE), F32),
                   jax.ShapeDtypeStruct((H, NSTATE), F32), jax.ShapeDtypeStruct((H, NSTATE), F32),
                   jax.ShapeDtypeStruct((G * H, P), F32)),
        name="ssm_prep",
    )(flat(a_re), flat(a_im), dt_row, b_hs(b_re), b_hs(b_im), c_im.reshape(G * H, P))
    ghp = lambda v: jnp.transpose(v.reshape(H, G, P), (1, 0, 2))
    return (ab_re.reshape(G, P), ab_im.reshape(G, P), ghp(bb_re), ghp(bb_im), cneg.reshape(G, H, P))


def _block_diag_weights(bb_re, bb_im, c_re, c_im_neg):
    eye8 = jnp.eye(8, dtype=F32)
    eye4 = jnp.eye(4, dtype=F32)
    eye2 = jnp.eye(2, dtype=F32)

    def wb_part(bb):
        x = bb.reshape(4, 8, H, P)
        return jnp.einsum('qghp,gk->qghkp', x, eye8).reshape(4, 8 * H, 8 * P)

    wb = jnp.concatenate([wb_part(bb_re), wb_part(bb_im)], axis=-1).astype(BF16)

    def wc_part(c):
        x = c.reshape(4, 4, 2, H, P)
        y = jnp.einsum('qjghp,jk,gl->qjgpklh', x, eye4, eye2)
        return y.reshape(NCHUNK, 2 * P, 4 * 2 * H)

    wc = jnp.concatenate([wc_part(c_re), wc_part(c_im_neg)], axis=1).astype(BF16)
    return wb, wc


def _front(x, sc1, sh1, g_mix, w_in_ref):
    n = _rms(x) * g_mix * (1.0 + sc1) + sh1
    proj = _bdot(n, w_in_ref[...])
    u = proj[:, :SSM_W]
    glu = proj[:, SSM_W:SSM_W + CONV_CH] * _sigmoid(proj[:, SSM_W + CONV_CH:])
    return u, glu


def _ssm_out(y_lin, u, d_skip, w_glu_ref, b_glu, g_out_ssm):
    y = _gelu_tanh(y_lin + d_skip * u)
    ys = y * _sigmoid(_bdot(y, w_glu_ref[...]) + b_glu)
    return _rms(ys) * g_out_ssm


def _conv_out(conv, b_dw, ln_g, ln_b, g_out_conv):
    c = conv + b_dw
    mu = jnp.mean(c, axis=-1, keepdims=True)
    cc = c - mu
    var = jnp.mean(cc * cc, axis=-1, keepdims=True)
    ln = cc * lax.rsqrt(var + EPS) * ln_g + ln_b
    yc = ln * _sigmoid(ln)
    return _rms(yc) * g_out_conv


def _route(n2, w_r_ref, b_r):
    rows = n2.shape[0]
    n_hi = n2.astype(BF16)
    n_lo = (n2 - n_hi.astype(F32)).astype(BF16)
    parts = jnp.dot(jnp.concatenate([n_hi, n_lo], axis=0), w_r_ref[...], preferred_element_type=F32)
    lg = (parts[:rows, :LANES] + parts[:rows, LANES:]) + (parts[rows:, :LANES] + parts[rows:, LANES:]) + b_r
    lane = lax.broadcasted_iota(jnp.int32, (rows, LANES), 1).astype(F32)
    ninf = -jnp.inf
    big = 1e9
    gmask = jnp.logical_and(lane >= NE, lane < NE + NG)
    gl = jnp.where(gmask, lg, ninf)
    gmax = jnp.max(gl, axis=-1, keepdims=True)
    gsum = jnp.sum(jnp.where(gmask, jnp.exp(gl - gmax), 0.0), axis=-1, keepdims=True)
    p_top = 1.0 / gsum
    gi = jnp.min(jnp.where(gl == gmax, lane, big), axis=-1, keepdims=True) - NE
    lo = gi * EPG
    emask = jnp.logical_and(lane >= lo, lane < lo + EPG)
    el = jnp.where(emask, lg, ninf)
    m1 = jnp.max(el, axis=-1, keepdims=True)
    i1 = jnp.min(jnp.where(el == m1, lane, big), axis=-1, keepdims=True)
    el2 = jnp.where(lane == i1, ninf, el)
    m2 = jnp.max(el2, axis=-1, keepdims=True)
    i2 = jnp.min(jnp.where(el2 == m2, lane, big), axis=-1, keepdims=True)
    e2 = jnp.exp(m2 - m1)
    den = 1.0 + e2
    w1 = p_top / den
    w2 = p_top * e2 / den
    cnt = jnp.sum(jnp.where(lane == i1, 1.0, 0.0) + jnp.where(lane == i2, 1.0, 0.0),
                  axis=0, keepdims=True)
    cols = (jnp.where(lane == 0.0, i1, 0.0) + jnp.where(lane == 1.0, i2, 0.0)
            + jnp.where(lane == 2.0, w1, 0.0) + jnp.where(lane == 3.0, w2, 0.0))
    return cols.T[0:8, :], cnt


def _tail(x, ns, nc, gt1, sc2, sh2, w_out_ref, g_ffn, w_r_ref, b_r):
    merged = _bdot(jnp.concatenate([ns, nc], axis=-1), w_out_ref[...])
    x1 = x + gt1 * merged
    n2 = _rms(x1) * g_ffn * (1.0 + sc2) + sh2
    rt, cnt = _route(n2, w_r_ref, b_r)
    return x1, n2, rt, cnt


def _prompt_mixer_kernel(x_ref, mod_ref, g_mix_ref, w_in_ref, wb_ref, a_ref, wc_ref, dsk_ref,
                         w_glu_ref, b_glu_ref, w_dw_ref, b_dw_ref, ln_g_ref, ln_b_ref,
                         g_os_ref, g_oc_ref, w_out_ref, g_ffn_ref, w_r_ref, b_r_ref,
                         x1_ref, n2_ref, rt_ref, cnt_ref, hr_ref, hi_ref, cache_ref,
                         sre, sim, hst, ebuf, eshift, ncbuf, *, tc, pt):
    c = pl.program_id(1)
    nc_chunks = pl.num_programs(1)

    @pl.when(c == 0)
    def _():
        hst[...] = jnp.zeros_like(hst)
        ebuf[pl.ds(0, 32), :] = jnp.zeros((32, CONV_CH), F32)

    x = x_ref[...]
    mod = mod_ref[...]
    sh1, sc1, gt1 = mod[0:1], mod[1:2], mod[2:3]
    sh2, sc2 = mod[3:4], mod[4:5]

    u, glu = _front(x, sc1, sh1, g_mix_ref[...], w_in_ref)
    ebuf[pl.ds(32, tc), :] = glu

    for s in range(1, 8):
        eshift[s - 1, pl.ds(0, tc + 24), :] = ebuf[pl.ds(s, tc + 24), :]
    rb = 64
    convs = []
    for r0 in range(0, tc, rb):
        acc = None
        for k in range(KW):
            a8, s = (k + 2) // 8 * 8, (k + 2) % 8
            win = ebuf[pl.ds(r0 + a8, rb), :] if s == 0 else eshift[s - 1, pl.ds(r0 + a8, rb), :]
            term = w_dw_ref[k:k + 1, :] * win
            acc = term if acc is None else acc + term
        convs.append(acc)
    conv = jnp.concatenate(convs, axis=0)
    ncbuf[...] = _conv_out(conv, b_dw_ref[...], ln_g_ref[...], ln_b_ref[...], g_oc_ref[...])

    @pl.when(c == nc_chunks - 1)
    def _():
        cache_ref[...] = ebuf[pl.ds(tc + 2, CB), :]

    ebuf[pl.ds(0, 32), :] = ebuf[pl.ds(tc, 32), :]

    ub = u.astype(BF16)
    for q in range(4):
        r = jnp.dot(ub[:, q * LANES:(q + 1) * LANES], wb_ref[q], preferred_element_type=F32)
        for k in range(4):
            j = 4 * q + k
            sre[pl.ds(j * pt, tc), :] = r[:, k * LANES:(k + 1) * LANES]
            sim[pl.ds(j * pt, tc), :] = r[:, SSM_W + k * LANES:SSM_W + (k + 1) * LANES]

    ar0, ar1 = a_ref[0, 0:8, :], a_ref[0, 8:16, :]
    ai0, ai1 = a_ref[1, 0:8, :], a_ref[1, 8:16, :]

    def step(g, carry):
        hr0, hr1, hi0, hi1 = carry
        t0 = g * SCAN_GROUP
        idx = [(pl.ds(t0 + u, 8, stride=pt), pl.ds(t0 + u + 8 * pt, 8, stride=pt))
               for u in range(SCAN_GROUP)]
        bu = [(sre[i0, :], sre[i1, :], sim[i0, :], sim[i1, :]) for (i0, i1) in idx]
        hs = []
        for br0, br1, bi0, bi1 in bu:
            nr0 = ar0 * hr0 - ai0 * hi0 + br0
            ni0 = ar0 * hi0 + ai0 * hr0 + bi0
            nr1 = ar1 * hr1 - ai1 * hi1 + br1
            ni1 = ar1 * hi1 + ai1 * hr1 + bi1
            hr0, hr1, hi0, hi1 = nr0, nr1, ni0, ni1
            hs.append((nr0, nr1, ni0, ni1))
        for (i0, i1), (nr0, nr1, ni0, ni1) in zip(idx, hs):
            sre[i0, :] = nr0
            sim[i0, :] = ni0
            sre[i1, :] = nr1
            sim[i1, :] = ni1
        return hr0, hr1, hi0, hi1

    init = (hst[0, 0:8, :], hst[0, 8:16, :], hst[1, 0:8, :], hst[1, 8:16, :])
    hr0, hr1, hi0, hi1 = lax.fori_loop(0, tc // SCAN_GROUP, step, init)
    hst[0, 0:8, :] = hr0
    hst[0, 8:16, :] = hr1
    hst[1, 0:8, :] = hi0
    hst[1, 8:16, :] = hi1

    @pl.when(c == nc_chunks - 1)
    def _():
        hr_ref[...] = hst[0]
        hi_ref[...] = hst[1]

    ys = []
    for q in range(4):
        acc = None
        for jj in range(4):
            j = 4 * q + jj
            lhs = jnp.concatenate([sre[pl.ds(j * pt, tc), :], sim[pl.ds(j * pt, tc), :]], axis=-1)
            d = jnp.dot(lhs.astype(BF16), wc_ref[j], preferred_element_type=F32)
            acc = d if acc is None else acc + d
        ys.append(acc)
    y_lin = jnp.concatenate(ys, axis=-1)
    ns = _ssm_out(y_lin, u, dsk_ref[...], w_glu_ref, b_glu_ref[...], g_os_ref[...])
    nc = ncbuf[...]

    x1, n2, rt, cnt = _tail(x, ns, nc, gt1, sc2, sh2, w_out_ref, g_ffn_ref[...], w_r_ref, b_r_ref[...])
    x1_ref[...] = x1
    n2_ref[...] = n2.astype(BF16)
    rt_ref[...] = rt
    cnt_ref[...] = cnt


def _const_spec(shape):
    nd = len(shape)
    return pl.BlockSpec(shape, lambda b, c: (0,) * nd)


def _prompt_mixer_call(x, mod6, wts, tc):
    bsz, t, _ = x.shape
    n_all = bsz * t
    pt = tc + SUBLANES
    assert (pt // SUBLANES) % 2 == 1
    nc = t // tc
    kern = functools.partial(_prompt_mixer_kernel, tc=tc, pt=pt)
    in_specs = [pl.BlockSpec((None, tc, D), lambda b, c: (b, c, 0)),
                pl.BlockSpec((None, 6, D), lambda b, c: (b, 0, 0))]
    in_specs += [_const_spec(w.shape) for w in wts]
    out_shape = (jax.ShapeDtypeStruct((n_all, D), F32),
                 jax.ShapeDtypeStruct((n_all, D), BF16),
                 jax.ShapeDtypeStruct((8, n_all), F32),
                 jax.ShapeDtypeStruct((n_all // tc, 1, LANES), F32),
                 jax.ShapeDtypeStruct((bsz, NCHUNK, LANES), F32),
                 jax.ShapeDtypeStruct((bsz, NCHUNK, LANES), F32),
                 jax.ShapeDtypeStruct((bsz, CB, CONV_CH), F32))
    out_specs = (pl.BlockSpec((tc, D), lambda b, c: (b * nc + c, 0)),
                 pl.BlockSpec((tc, D), lambda b, c: (b * nc + c, 0)),
                 pl.BlockSpec((8, tc), lambda b, c: (0, b * nc + c)),
                 pl.BlockSpec((None, 1, LANES), lambda b, c: (b * nc + c, 0, 0)),
                 pl.BlockSpec((None, NCHUNK, LANES), lambda b, c: (b, 0, 0)),
                 pl.BlockSpec((None, NCHUNK, LANES), lambda b, c: (b, 0, 0)),
                 pl.BlockSpec((None, CB, CONV_CH), lambda b, c: (b, 0, 0)))
    scratch = [pltpu.VMEM((NCHUNK * pt, LANES), F32),
               pltpu.VMEM((NCHUNK * pt, LANES), F32),
               pltpu.VMEM((2, NCHUNK, LANES), F32),
               pltpu.VMEM((tc + 32, CONV_CH), F32),
               pltpu.VMEM((7, tc + 32, CONV_CH), F32),
               pltpu.VMEM((tc, CONV_CH), F32)]
    return pl.pallas_call(
        kern, grid=(bsz, nc), in_specs=in_specs, out_specs=out_specs, out_shape=out_shape,
        scratch_shapes=scratch,
        compiler_params=pltpu.CompilerParams(dimension_semantics=("arbitrary", "arbitrary"),
                                             vmem_limit_bytes=VMEM_LIMIT),
        name="prompt_mixer",
    )(x, mod6, *wts)


def _sample_mixer_kernel(x_ref, mod_ref, h0r_ref, h0i_ref, cache_ref,
                         g_mix_ref, w_in_ref, wb_ref, a_ref, wc_ref, dsk_ref,
                         w_glu_ref, b_glu_ref, w_dw_ref, b_dw_ref, ln_g_ref, ln_b_ref,
                         g_os_ref, g_oc_ref, w_out_ref, g_ffn_ref, w_r_ref, b_r_ref,
                         x1_ref, n2_ref, rt_ref, cnt_ref, hr_ref, hi_ref, glu_ref,
                         sre, sim, *, nb, nt):
    x = x_ref[...]

    def rows(i):
        m = mod_ref[:, i * D:(i + 1) * D]
        return jnp.concatenate([m] * nt, axis=0)

    sh1, sc1, gt1, sh2, sc2 = rows(0), rows(1), rows(2), rows(3), rows(4)
    u, glu = _front(x, sc1, sh1, g_mix_ref[...], w_in_ref)
    glu_ref[...] = glu

    ub = u.astype(BF16)
    for q in range(4):
        r = jnp.dot(ub[:, q * LANES:(q + 1) * LANES], wb_ref[q], preferred_element_type=F32)
        sre[:, q * SSM_W:(q + 1) * SSM_W] = r[:, :SSM_W]
        sim[:, q * SSM_W:(q + 1) * SSM_W] = r[:, SSM_W:]

    ar = a_ref[0:1, :]
    ai = a_ref[1:2, :]
    hr = h0r_ref[...]
    hi = h0i_ref[...]
    for t in range(nt):
        rs = pl.ds(t * nb, nb)
        nr = ar * hr - ai * hi + sre[rs, :]
        ni = ar * hi + ai * hr + sim[rs, :]
        sre[rs, :] = nr
        sim[rs, :] = ni
        hr, hi = nr, ni
    hr_ref[...] = hr
    hi_ref[...] = hi

    ys = []
    for q in range(4):
        acc = None
        for jj in range(4):
            j = 4 * q + jj
            lhs = jnp.concatenate([sre[:, j * LANES:(j + 1) * LANES],
                                   sim[:, j * LANES:(j + 1) * LANES]], axis=-1)
            d = jnp.dot(lhs.astype(BF16), wc_ref[j], preferred_element_type=F32)
            acc = d if acc is None else acc + d
        ys.append(acc)
    y_lin = jnp.concatenate(ys, axis=-1)
    ns = _ssm_out(y_lin, u, dsk_ref[...], w_glu_ref, b_glu_ref[...], g_os_ref[...])

    def ext(jrow):
        if jrow < CB:
            return cache_ref[jrow]
        return glu[(jrow - CB) * nb:(jrow - CB + 1) * nb, :]

    convs = []
    for t in range(nt):
        acc = None
        for k in range(KW):
            term = w_dw_ref[k:k + 1, :] * ext(t + k)
            acc = term if acc is None else acc + term
        convs.append(acc)
    conv = jnp.concatenate(convs, axis=0)
    nc = _conv_out(conv, b_dw_ref[...], ln_g_ref[...], ln_b_ref[...], g_oc_ref[...])

    x1, n2, rt, cnt = _tail(x, ns, nc, gt1, sc2, sh2, w_out_ref, g_ffn_ref[...], w_r_ref, b_r_ref[...])
    x1_ref[...] = x1
    n2_ref[...] = n2.astype(BF16)
    rt_ref[...] = rt
    cnt_ref[...] = cnt


def _sample_mixer_call(x_tm, mod_s, h0r, h0i, cache_tm, wts, nb, nt):
    n = nb * nt
    kern = functools.partial(_sample_mixer_kernel, nb=nb, nt=nt)
    out_shape = (jax.ShapeDtypeStruct((n, D), F32),
                 jax.ShapeDtypeStruct((n, D), BF16),
                 jax.ShapeDtypeStruct((8, n), F32),
                 jax.ShapeDtypeStruct((1, LANES), F32),
                 jax.ShapeDtypeStruct((nb, NSTATE), F32),
                 jax.ShapeDtypeStruct((nb, NSTATE), F32),
                 jax.ShapeDtypeStruct((n, CONV_CH), F32))
    scratch = [pltpu.VMEM((n, NSTATE), F32), pltpu.VMEM((n, NSTATE), F32)]
    return pl.pallas_call(
        kern, out_shape=out_shape, scratch_shapes=scratch,
        compiler_params=pltpu.CompilerParams(vmem_limit_bytes=VMEM_LIMIT),
        name="sample_mixer",
    )(x_tm, mod_s, h0r, h0i, cache_tm, *wts)


ROW_ALIGN = 16
MOE_TD = 512
MOE_BR = MOE_TD * 2 + NE * ROW_ALIGN
MOE_TG = 256


def _slot_positions(rt):
    t = rt.shape[1]
    e0 = rt[0:1, :]
    e1 = rt[1:2, :]
    sub = lax.broadcasted_iota(jnp.int32, (LANES, t), 0).astype(F32)
    a0 = jnp.where(sub == e0, 1.0, 0.0)
    a1 = jnp.where(sub == e1, 1.0, 0.0)
    at = a0 + a1
    r = lax.broadcasted_iota(jnp.int32, (t, t), 0)
    c = lax.broadcasted_iota(jnp.int32, (t, t), 1)
    before = jnp.where(r < c, 1.0, 0.0).astype(BF16)
    rank = jnp.dot(at.astype(BF16), before, preferred_element_type=F32)
    cnt = jnp.sum(at, axis=1, keepdims=True)
    cnt_al = jnp.ceil(cnt * (1.0 / ROW_ALIGN)) * float(ROW_ALIGN)
    er = lax.broadcasted_iota(jnp.int32, (LANES, LANES), 0)
    ec = lax.broadcasted_iota(jnp.int32, (LANES, LANES), 1)
    lower = jnp.where(ec < er, 1.0, 0.0)
    base = jnp.dot(lower, jnp.broadcast_to(cnt_al, (LANES, LANES)), preferred_element_type=F32,
                   precision=lax.Precision.HIGHEST)[:, 0:1]
    slot = rank + base
    pos0 = jnp.sum(a0 * slot, axis=0, keepdims=True)
    pos1 = jnp.sum(a1 * slot, axis=0, keepdims=True)
    return pos0, pos1


def _segment_copies(tab_ref, t, n_tiles, buf, hbm, sems, slot, to_hbm, wait):
    for e in range(NE):
        n = pl.multiple_of(tab_ref[t * NE + e], ROW_ALIGN)
        b = pl.multiple_of(tab_ref[(n_tiles + t) * NE + e], ROW_ALIGN)
        d = pl.multiple_of(tab_ref[(2 * n_tiles + t) * NE + e], ROW_ALIGN)
        vm = buf.at[slot, pl.ds(b, n)]
        hb = hbm.at[pl.ds(d, n)]
        cp = pltpu.make_async_copy(vm, hb, sems.at[slot, e]) if to_hbm else \
            pltpu.make_async_copy(hb, vm, sems.at[slot, e])

        @pl.when(n > 0)
        def _():
            if wait:
                cp.wait()
            else:
                cp.start()


def _tile_rows(tab_ref, t, n_tiles):
    return tab_ref[3 * n_tiles * NE + 2 * NE + 1 + t]


def _one_hot_rows(r0, nrows, pos0, pos1):
    row = (lax.broadcasted_iota(jnp.int32, (nrows, MOE_TD), 0) + r0).astype(F32)
    return row == pos0, row == pos1


MOE_BLK = 256


def _dispatch_kernel(tab_ref, rtp_ref, n2p_ref, rts_ref, n2s_ref, xs_ref, pos_ref, buf, zbuf, sems, zsem,
                     *, n_tiles, n_ptiles, n_gtiles):
    t = pl.program_id(0)
    slot = lax.rem(t, 2)
    first_free = tab_ref[3 * n_tiles * NE + 2 * NE]

    def fill_copy(j):
        d = pl.multiple_of(j * MOE_TG, MOE_TG)
        return pltpu.make_async_copy(zbuf, xs_ref.at[pl.ds(d, MOE_TG)], zsem)

    def fill_start(j, carry):
        fill_copy(j).start()
        return carry

    def fill_wait(j, carry):
        fill_copy(j).wait()
        return carry

    @pl.when(t == 0)
    def _():
        zbuf[...] = jnp.zeros_like(zbuf)
        for phase in range(2):
            for e in range(NE):
                d = pl.multiple_of(tab_ref[3 * n_tiles * NE + e], ROW_ALIGN)
                n = pl.multiple_of(tab_ref[3 * n_tiles * NE + NE + e], ROW_ALIGN)
                cp = pltpu.make_async_copy(zbuf.at[pl.ds(0, n)], xs_ref.at[pl.ds(d, n)], sems.at[1, e])

                @pl.when(n > 0)
                def _():
                    if phase == 0:
                        cp.start()
                    else:
                        cp.wait()

        lax.fori_loop(first_free, n_gtiles, fill_start, 0)

    @pl.when(t >= 2)
    def _():
        _segment_copies(tab_ref, t - 2, n_tiles, buf, xs_ref, sems, slot, to_hbm=True, wait=True)

    is_sample = t >= n_ptiles
    rt = jnp.where(is_sample, rts_ref[...], rtp_ref[...])
    n2 = jnp.where(is_sample, n2s_ref[...], n2p_ref[...])
    pos0, pos1 = _slot_positions(rt)
    pos_ref[...] = jnp.concatenate([pos0, pos1, jnp.zeros((6, MOE_TD), F32)], axis=0)
    used = _tile_rows(tab_ref, t, n_tiles)

    def group(r0, nrows):
        m0, m1 = _one_hot_rows(r0, nrows, pos0, pos1)
        q = (jnp.where(m0, 1.0, 0.0) + jnp.where(m1, 1.0, 0.0)).astype(BF16)
        buf[slot, pl.ds(r0, nrows), :] = jnp.dot(q, n2, preferred_element_type=F32).astype(BF16)

    group(0, 2 * MOE_TD)
    for r0 in range(2 * MOE_TD, MOE_BR, MOE_BLK):
        @pl.when(used > r0)
        def _():
            group(r0, MOE_BLK)

    _segment_copies(tab_ref, t, n_tiles, buf, xs_ref, sems, slot, to_hbm=True, wait=False)

    @pl.when(t == n_tiles - 1)
    def _():
        if n_tiles >= 2:
            _segment_copies(tab_ref, t - 1, n_tiles, buf, xs_ref, sems, 1 - slot, to_hbm=True, wait=True)
        _segment_copies(tab_ref, t, n_tiles, buf, xs_ref, sems, slot, to_hbm=True, wait=True)
        lax.fori_loop(first_free, n_gtiles, fill_wait, 0)


def _dispatch_call(tab, rt_p, n2_p, rt_s, n2_s, r_max):
    n_ptiles = n2_p.shape[0] // MOE_TD
    n_tiles = n_ptiles + n2_s.shape[0] // MOE_TD
    last_p = n_ptiles - 1
    return pl.pallas_call(
        functools.partial(_dispatch_kernel, n_tiles=n_tiles, n_ptiles=n_ptiles,
                          n_gtiles=r_max // MOE_TG),
        grid_spec=pltpu.PrefetchScalarGridSpec(
            num_scalar_prefetch=1, grid=(n_tiles,),
            in_specs=[pl.BlockSpec((8, MOE_TD), lambda t, tab: (0, jnp.minimum(t, last_p))),
                      pl.BlockSpec((MOE_TD, D), lambda t, tab: (jnp.minimum(t, last_p), 0)),
                      pl.BlockSpec((8, MOE_TD), lambda t, tab: (0, 0)),
                      pl.BlockSpec((MOE_TD, D), lambda t, tab: (0, 0))],
            out_specs=(pl.BlockSpec(memory_space=pl.ANY),
                       pl.BlockSpec((None, 8, MOE_TD), lambda t, tab: (t, 0, 0))),
            scratch_shapes=[pltpu.VMEM((2, MOE_BR, D), BF16),
                            pltpu.VMEM((MOE_TG, D), BF16),
                            pltpu.SemaphoreType.DMA((2, NE)),
                            pltpu.SemaphoreType.DMA(())]),
        out_shape=(jax.ShapeDtypeStruct((r_max, D), BF16),
                   jax.ShapeDtypeStruct((n_tiles, 8, MOE_TD), F32)),
        compiler_params=pltpu.CompilerParams(dimension_semantics=("arbitrary",),
                                             vmem_limit_bytes=VMEM_LIMIT),
        name="moe_dispatch",
    )(tab, rt_p, n2_p, rt_s, n2_s)


MOE_CK = 8


def _experts_kernel(ctab_ref, xs_ref, w1_ref, w3_ref, w2_ref, ys_ref,
                    w1b, w3b, w2b, xbuf, ybuf, zbuf, in_sem, out_sem, zsem, *, n_gtiles, nc_max):
    e = pl.program_id(0)
    c0 = ctab_ref[e]
    c1 = ctab_ref[e + 1]
    n_chunks = ctab_ref[NE]
    first_free = ctab_ref[NE + 1 + 2 * nc_max]

    def fill_copy(j):
        d = pl.multiple_of(j * MOE_TG, MOE_TG)
        return pltpu.make_async_copy(zbuf, ys_ref.at[pl.ds(d, MOE_TG)], zsem)

    def fill_start(j, carry):
        fill_copy(j).start()
        return carry

    def fill_wait(j, carry):
        fill_copy(j).wait()
        return carry

    def span(c):
        r = pl.multiple_of(ctab_ref[NE + 1 + c], MOE_TG)
        n = pl.multiple_of(ctab_ref[NE + 1 + nc_max + c] * MOE_TG, MOE_TG)
        return r, n

    def in_copy(c, slot):
        r, n = span(c)
        return pltpu.make_async_copy(xs_ref.at[pl.ds(r, n)], xbuf.at[slot, pl.ds(0, n)], in_sem.at[slot])

    def out_copy(c, slot):
        r, n = span(c)
        return pltpu.make_async_copy(ybuf.at[slot, pl.ds(0, n)], ys_ref.at[pl.ds(r, n)], out_sem.at[slot])

    @pl.when(e == 0)
    def _():
        zbuf[...] = jnp.zeros_like(zbuf)
        lax.fori_loop(first_free, n_gtiles, fill_start, 0)

        @pl.when(n_chunks > 0)
        def _():
            in_copy(0, 0).start()

    def compute(slot, rows):
        x = xbuf[slot, pl.ds(0, rows), :]
        a = jnp.dot(x, w1b[...], preferred_element_type=F32)
        b = jnp.dot(x, w3b[...], preferred_element_type=F32)
        hid = a * _sigmoid(a) * b
        y = jnp.dot(hid.astype(BF16), w2b[...], preferred_element_type=F32)
        ybuf[slot, pl.ds(0, rows), :] = y.astype(BF16)

    @pl.when(c1 > c0)
    def _():
        w1b[...] = w1_ref[...].astype(BF16)
        w3b[...] = w3_ref[...].astype(BF16)
        w2b[...] = w2_ref[...].astype(BF16)

        def chunk(c, carry):
            slot = lax.rem(c, 2)

            @pl.when(c + 1 < n_chunks)
            def _():
                in_copy(c + 1, 1 - slot).start()

            in_copy(c, slot).wait()

            @pl.when(c >= 2)
            def _():
                out_copy(c - 2, slot).wait()

            k = ctab_ref[NE + 1 + nc_max + c]
            for kk in range(1, MOE_CK + 1):
                @pl.when(k == kk)
                def _():
                    compute(slot, kk * MOE_TG)

            out_copy(c, slot).start()
            return carry

        lax.fori_loop(c0, c1, chunk, 0)

    @pl.when(e == NE - 1)
    def _():
        @pl.when(n_chunks >= 2)
        def _():
            out_copy(n_chunks - 2, lax.rem(n_chunks, 2)).wait()

        @pl.when(n_chunks >= 1)
        def _():
            out_copy(n_chunks - 1, lax.rem(n_chunks - 1, 2)).wait()

        lax.fori_loop(first_free, n_gtiles, fill_wait, 0)


def _experts_call(ctab, xs, w1, w3, w2, nc_max):
    r_max = xs.shape[0]
    w_map = lambda e, ctab: (e, 0, 0)
    ring = pltpu.VMEM((2, MOE_CK * MOE_TG, D), BF16)
    return pl.pallas_call(
        functools.partial(_experts_kernel, n_gtiles=r_max // MOE_TG, nc_max=nc_max),
        grid_spec=pltpu.PrefetchScalarGridSpec(
            num_scalar_prefetch=1, grid=(NE,),
            in_specs=[pl.BlockSpec(memory_space=pl.ANY),
                      pl.BlockSpec((None, D, DE), w_map),
                      pl.BlockSpec((None, D, DE), w_map),
                      pl.BlockSpec((None, DE, D), w_map)],
            out_specs=pl.BlockSpec(memory_space=pl.ANY),
            scratch_shapes=[pltpu.VMEM((D, DE), BF16), pltpu.VMEM((D, DE), BF16),
                            pltpu.VMEM((DE, D), BF16), ring, ring,
                            pltpu.VMEM((MOE_TG, D), BF16),
                            pltpu.SemaphoreType.DMA((2,)), pltpu.SemaphoreType.DMA((2,)),
                            pltpu.SemaphoreType.DMA(())]),
        out_shape=jax.ShapeDtypeStruct((r_max, D), BF16),
        compiler_params=pltpu.CompilerParams(dimension_semantics=("arbitrary",),
                                             vmem_limit_bytes=VMEM_LIMIT),
        name="moe_experts",
    )(ctab, xs, w1, w3, w2)


def _combine_kernel(tab_ref, rt_ref, pos_ref, x1_ref, gt2_ref, gf_ref, ys_ref, y_ref, buf, acc, sems,
                    *, n_tiles, t_off):
    i = pl.program_id(0)
    t = i + t_off
    slot = lax.rem(i, 2)

    @pl.when(i == 0)
    def _():
        buf[...] = jnp.zeros_like(buf)
        _segment_copies(tab_ref, t, n_tiles, buf, ys_ref, sems, slot, to_hbm=False, wait=False)

    @pl.when(i + 1 < pl.num_programs(0))
    def _():
        _segment_copies(tab_ref, t + 1, n_tiles, buf, ys_ref, sems, 1 - slot, to_hbm=False, wait=False)

    _segment_copies(tab_ref, t, n_tiles, buf, ys_ref, sems, slot, to_hbm=False, wait=True)

    rt = rt_ref[...]
    pos0 = pos_ref[0:1, :]
    pos1 = pos_ref[1:2, :]
    used = _tile_rows(tab_ref, t, n_tiles)

    def ungroup(r0, nrows):
        m0, m1 = _one_hot_rows(r0, nrows, pos0, pos1)
        q = (jnp.where(m0, 1.0, 0.0) + jnp.where(m1, 1.0, 0.0)).astype(BF16)
        gw = jnp.sum(jnp.where(m0, rt[2:3, :], 0.0) + jnp.where(m1, rt[3:4, :], 0.0),
                     axis=1, keepdims=True)
        yv = (buf[slot, pl.ds(r0, nrows), :].astype(F32) * gw).astype(BF16)
        return lax.dot_general(q, yv, (((0,), (0,)), ((), ())), preferred_element_type=F32)

    acc[...] = ungroup(0, 2 * MOE_TD)
    for r0 in range(2 * MOE_TD, MOE_BR, MOE_BLK):
        @pl.when(used > r0)
        def _():
            acc[...] += ungroup(r0, MOE_BLK)

    gt2 = gt2_ref[...]
    if gt2.shape[0] not in (1, MOE_TD):
        gt2 = jnp.concatenate([gt2] * (MOE_TD // gt2.shape[0]), axis=0)
    xo = x1_ref[...] + gt2 * acc[...]
    y_ref[...] = _rms(xo) * gf_ref[...]


def _combine_call(tab, rt, pos, x1, gt2, gt2_spec, g_final, ys, n_tiles, t_off):
    n_out_tiles = x1.shape[0] // MOE_TD
    return pl.pallas_call(
        functools.partial(_combine_kernel, n_tiles=n_tiles, t_off=t_off),
        grid_spec=pltpu.PrefetchScalarGridSpec(
            num_scalar_prefetch=1, grid=(n_out_tiles,),
            in_specs=[pl.BlockSpec((8, MOE_TD), lambda t, tab: (0, t)),
                      pl.BlockSpec((None, 8, MOE_TD), lambda t, tab: (t + t_off, 0, 0)),
                      pl.BlockSpec((MOE_TD, D), lambda t, tab: (t, 0)),
                      gt2_spec,
                      pl.BlockSpec((1, D), lambda t, tab: (0, 0)),
                      pl.BlockSpec(memory_space=pl.ANY)],
            out_specs=pl.BlockSpec((MOE_TD, D), lambda t, tab: (t, 0)),
            scratch_shapes=[pltpu.VMEM((2, MOE_BR, D), BF16),
                            pltpu.VMEM((MOE_TD, D), F32),
                            pltpu.SemaphoreType.DMA((2, NE))]),
        out_shape=jax.ShapeDtypeStruct((n_out_tiles * MOE_TD, D), F32),
        compiler_params=pltpu.CompilerParams(dimension_semantics=("arbitrary",),
                                             vmem_limit_bytes=VMEM_LIMIT),
        name="moe_combine",
    )(tab, rt, pos, x1, gt2, g_final.reshape(1, D), ys)


def _moe_plan(cnt, nc_max):
    cnt_al = (cnt + ROW_ALIGN - 1) // ROW_ALIGN * ROW_ALIGN
    seg_rows = cnt_al.sum(axis=0)
    seg_pad = (seg_rows + MOE_TG - 1) // MOE_TG * MOE_TG
    seg_start = jnp.cumsum(seg_pad) - seg_pad
    dst = seg_start[None, :] + jnp.cumsum(cnt_al, axis=0) - cnt_al
    boff = jnp.cumsum(cnt_al, axis=1) - cnt_al
    tile_end = jnp.cumsum(seg_pad // MOE_TG)
    n_active = tile_end[-1:].astype(jnp.int32)
    tab = jnp.concatenate([cnt_al.ravel(), boff.ravel(), dst.ravel(),
                           seg_start + seg_rows, seg_pad - seg_rows, n_active,
                           cnt_al.sum(axis=1)]).astype(jnp.int32)
    nt = seg_pad // MOE_TG
    nfull = nt // MOE_CK
    rem = nt % MOE_CK
    nch = nfull + (rem > 0)
    cend = jnp.cumsum(nch)
    cstart = cend - nch
    c = jnp.arange(nc_max, dtype=jnp.int32)
    ce = jnp.minimum(jnp.sum(c[:, None] >= cend[None, :], axis=1), NE - 1)
    local = c - cstart[ce]
    valid = c < cend[-1]
    ck = jnp.where(valid, jnp.where(local < nfull[ce], MOE_CK, rem[ce]), 0)
    crow = jnp.where(valid, seg_start[ce] + local * (MOE_CK * MOE_TG), 0)
    ctab = jnp.concatenate([cstart, cend[-1:], crow, ck, n_active]).astype(jnp.int32)
    return tab, ctab


def kernel(x_prompt, x_sample, c_prompt, c_sample, state_ssm_re, state_ssm_im, cache_conv, w_ada, b_ada, g_norm_mix, w_in, ssm_a_re, ssm_a_im, ssm_log_dt, ssm_b_re, ssm_b_im, ssm_c_re, ssm_c_im, ssm_d, w_ssm_glu, b_ssm_glu, w_dw, b_dw, ln_conv_g, ln_conv_b, g_out_ssm, g_out_conv, w_out, g_norm_ffn, w_router_grp, b_router_grp, w_router_exp, b_router_exp, w_exp_gate, w_exp_up, w_exp_down, g_final):
    depth = w_ada.shape[0]
    assert depth == 1
    bsz, seq, _ = x_prompt.shape
    nb, nt, _ = x_sample.shape

    n_c = bsz + nb
    c_pad = -n_c % 16
    c_all = jnp.concatenate([c_prompt, c_sample, jnp.zeros((c_pad, D), F32)], axis=0)
    mod_p, mod_s = _mod_call(c_all, w_ada[0], b_ada[0], bsz, nb)
    mod_p = mod_p.reshape(bsz, 6, D)

    ab_re, ab_im, bb_re, bb_im, c_im_neg = _ssm_prep_call(
        ssm_a_re[0], ssm_a_im[0], ssm_log_dt[0], ssm_b_re[0], ssm_b_im[0], ssm_c_im[0])
    wb, wc = _block_diag_weights(bb_re, bb_im, ssm_c_re[0], c_im_neg)
    a_tok = jnp.stack([ab_re.reshape(NCHUNK, LANES), ab_im.reshape(NCHUNK, LANES)])
    a_row = jnp.stack([ab_re.reshape(NSTATE), ab_im.reshape(NSTATE)])

    w_r = jnp.concatenate([w_router_exp[0].reshape(D, NE), w_router_grp[0],
                           jnp.zeros((D, LANES - NE - NG), F32)], axis=1)
    w_r_hi = w_r.astype(BF16)
    w_r = jnp.concatenate([w_r_hi, (w_r - w_r_hi.astype(F32)).astype(BF16)], axis=1)
    b_r = jnp.concatenate([b_router_exp[0].reshape(NE), b_router_grp[0],
                           jnp.zeros((LANES - NE - NG,), F32)]).reshape(1, LANES)
    w_dw_p = jnp.concatenate([w_dw[0], jnp.zeros((1, CONV_CH), F32)], axis=0)

    row = lambda v: v.reshape(1, -1)
    common_a = (row(g_norm_mix[0]), w_in[0].astype(BF16), wb)
    common_b = (wc, row(ssm_d[0].reshape(SSM_W)), w_ssm_glu[0].astype(BF16), row(b_ssm_glu[0]),
                w_dw_p, row(b_dw[0]), row(ln_conv_g[0]), row(ln_conv_b[0]),
                row(g_out_ssm[0]), row(g_out_conv[0]), w_out[0].astype(BF16),
                row(g_norm_ffn[0]), w_r, b_r)

    n_p = bsz * seq
    n_s = nb * nt
    n_all = n_p + n_s
    assert n_s == MOE_TD and seq % MOE_TD == 0 and MOE_TD % PROMPT_TC == 0
    wts_p = common_a + (a_tok,) + common_b
    x1_p, n2_p, rt_p, cnt_p, hr_p, hi_p, cache_p = _prompt_mixer_call(x_prompt, mod_p, wts_p, PROMPT_TC)

    x_tm = jnp.transpose(x_sample, (1, 0, 2)).reshape(nt * nb, D)
    cache_tm = jnp.transpose(cache_conv[0], (1, 0, 2))
    wts_s = common_a + (a_row,) + common_b
    x1_s, n2_s, rt_s, cnt_s, hr_s, hi_s, glu_s = _sample_mixer_call(
        x_tm, mod_s, state_ssm_re[0].reshape(nb, NSTATE), state_ssm_im[0].reshape(nb, NSTATE),
        cache_tm, wts_s, nb, nt)

    n_ptiles = n_p // MOE_TD
    n_tiles = n_all // MOE_TD
    r_max = -(-(2 * n_all + n_tiles * NE * (ROW_ALIGN - 1) + NE * (MOE_TG - ROW_ALIGN)) // MOE_TG) * MOE_TG
    cnt = jnp.concatenate([cnt_p.reshape(n_ptiles, MOE_TD // PROMPT_TC, LANES).sum(axis=1), cnt_s])
    nc_max = r_max // MOE_TG // MOE_CK + NE
    tab, ctab = _moe_plan(cnt[:, :NE].astype(jnp.int32), nc_max)
    xs, pos = _dispatch_call(tab, rt_p, n2_p, rt_s, n2_s, r_max)
    ys = _experts_call(ctab, xs, w_exp_gate[0], w_exp_up[0], w_exp_down[0], nc_max)
    tiles_per_b = seq // MOE_TD
    gt2_p = mod_p[:, 5:6, :]
    y_p = _combine_call(tab, rt_p, pos, x1_p, gt2_p,
                        pl.BlockSpec((None, 1, D), lambda t, tab: (t // tiles_per_b, 0, 0)),
                        g_final, ys, n_tiles, 0)
    y_s = _combine_call(tab, rt_s, pos, x1_s, mod_s,
                        pl.BlockSpec((nb, D), lambda t, tab: (0, 5)),
                        g_final, ys, n_tiles, n_ptiles)

    y_prompt = y_p.reshape(bsz, seq, D)
    y_sample = jnp.transpose(y_s.reshape(nt, nb, D), (1, 0, 2))
    new_cache_s = jnp.concatenate(
        [cache_conv[0][:, nt:, :], jnp.transpose(glu_s.reshape(nt, nb, CONV_CH), (1, 0, 2))], axis=1)
    return (y_prompt, y_sample,
            hr_p.reshape(1, bsz, G, P), hi_p.reshape(1, bsz, G, P), cache_p[None],
            hr_s.reshape(1, nb, G, P), hi_s.reshape(1, nb, G, P), new_cache_s[None])
```

```python
import functools

import jax
import jax.numpy as jnp
import numpy as np
from jax import lax
from jax.experimental import pallas as pl
from jax.experimental.pallas import tpu as pltpu

F32 = jnp.float32
BF16 = jnp.bfloat16

D = 1024
SSM_W = 512
CONV_CH = 512
G = 32
H = 16
P = 64
KW = 31
CB = KW - 1
NE = 32
NG = 4
EPG = 8
DE = 512
EPS = 1e-6
LANES = 128
SUBLANES = 8
NSTATE = G * P
NCHUNK = NSTATE // LANES

PROMPT_TC = 512
SCAN_GROUP = 8
V7X_VMEM_BYTES = 64 * 1024 * 1024
VMEM_LIMIT = V7X_VMEM_BYTES - 8 * 1024 * 1024


def _rms(x):
    return x * lax.rsqrt(jnp.mean(x * x, axis=-1, keepdims=True) + EPS)


def _sigmoid(x):
    return 0.5 * jnp.tanh(0.5 * x) + 0.5


def _gelu_tanh(y):
    c = np.sqrt(2.0 / np.pi).astype(np.float32)
    return y * (0.5 * (1.0 + jnp.tanh(c * (y + 0.044715 * (y * y * y)))))


def _bdot(a, b):
    return jnp.dot(a.astype(BF16), b, preferred_element_type=F32)


def _mod_kernel(c_ref, w_ref, b_ref, op_ref, os_ref):
    c = c_ref[...]
    s = c * _sigmoid(c)
    n = s.shape[0]
    s_hi = s.astype(BF16)
    lhs = jnp.concatenate([s_hi, (s - s_hi.astype(F32)).astype(BF16)], axis=0)
    w = w_ref[...]
    w_hi = w.astype(BF16)
    w_lo = (w - w_hi.astype(F32)).astype(BF16)
    p_hi = jnp.dot(lhs, w_hi, preferred_element_type=F32)
    p_lo = jnp.dot(lhs, w_lo, preferred_element_type=F32)
    res = (p_hi[:n] + p_lo[:n]) + (p_hi[n:] + p_lo[n:]) + b_ref[...]
    n_p = op_ref.shape[0]
    op_ref[...] = res[:n_p]
    os_ref[...] = res[n_p:n_p + os_ref.shape[0]]


def _mod_call(c_all, w_ada, b_ada, n_p, n_s):
    n = c_all.shape[0]
    tn = 512
    return pl.pallas_call(
        _mod_kernel,
        grid=(6 * D // tn,),
        in_specs=[pl.BlockSpec((n, D), lambda j: (0, 0)),
                  pl.BlockSpec((D, tn), lambda j: (0, j)),
                  pl.BlockSpec((1, tn), lambda j: (0, j))],
        out_specs=(pl.BlockSpec((n_p, tn), lambda j: (0, j)),
                   pl.BlockSpec((n_s, tn), lambda j: (0, j))),
        out_shape=(jax.ShapeDtypeStruct((n_p, 6 * D), F32),
                   jax.ShapeDtypeStruct((n_s, 6 * D), F32)),
        compiler_params=pltpu.CompilerParams(dimension_semantics=("arbitrary",)),
        name="mod",
    )(c_all, w_ada, b_ada.reshape(1, 6 * D))


def _ssm_prep_kernel(a_re, a_im, log_dt, b_re, b_im, c_im,
                     ab_re_o, ab_im_o, bb_re_o, bb_im_o, cneg_o):
    lam_re = jnp.minimum(a_re[...], -1e-4)
    lam_im = a_im[...]
    dt = jnp.exp(log_dt[...])
    mag = jnp.exp(lam_re * dt)
    ab_re = mag * jnp.cos(lam_im * dt)
    ab_im = mag * jnp.sin(lam_im * dt)
    den = lam_re * lam_re + lam_im * lam_im
    num_re = ab_re - 1.0
    coef_re = (num_re * lam_re + ab_im * lam_im) / den
    coef_im = (ab_im * lam_re - num_re * lam_im) / den
    ab_re_o[...] = ab_re
    ab_im_o[...] = ab_im
    br = b_re[...]
    bi = b_im[...]
    bb_re_o[...] = coef_re * br - coef_im * bi
    bb_im_o[...] = coef_re * bi + coef_im * br
    cneg_o[...] = -c_im[...]


def _ssm_prep_call(a_re, a_im, log_dt, b_re, b_im, c_im):
    flat = lambda v: v.reshape(1, NSTATE)
    b_hs = lambda v: jnp.transpose(v, (2, 0, 1)).reshape(H, NSTATE)
    dt_row = jnp.broadcast_to(log_dt[:, None], (G, P)).reshape(1, NSTATE)
    ab_re, ab_im, bb_re, bb_im, cneg = pl.pallas_call(
        _ssm_prep_kernel,
        out_shape=(jax.ShapeDtypeStruct((1, NSTATE), F32), jax.ShapeDtypeStruct((1, NSTATE), F32),
                   jax.ShapeDtypeStruct((H, NSTATE), F32), jax.ShapeDtypeStruct((H, NSTATE), F32),
                   jax.ShapeDtypeStruct((G * H, P), F32)),
        name="ssm_prep",
    )(flat(a_re), flat(a_im), dt_row, b_hs(b_re), b_hs(b_im), c_im.reshape(G * H, P))
    ghp = lambda v: jnp.transpose(v.reshape(H, G, P), (1, 0, 2))
    return (ab_re.reshape(G, P), ab_im.reshape(G, P), ghp(bb_re), ghp(bb_im), cneg.reshape(G, H, P))


def _block_diag_weights(bb_re, bb_im, c_re, c_im_neg):
    eye8 = jnp.eye(8, dtype=F32)
    eye4 = jnp.eye(4, dtype=F32)
    eye2 = jnp.eye(2, dtype=F32)

    def wb_part(bb):
        x = bb.reshape(4, 8, H, P)
        return jnp.einsum('qghp,gk->qghkp', x, eye8).reshape(4, 8 * H, 8 * P)

    wb = jnp.concatenate([wb_part(bb_re), wb_part(bb_im)], axis=-1).astype(BF16)

    def wc_part(c):
        x = c.reshape(4, 4, 2, H, P)
        y = jnp.einsum('qjghp,jk,gl->qjgpklh', x, eye4, eye2)
        return y.reshape(NCHUNK, 2 * P, 4 * 2 * H)

    wc = jnp.concatenate([wc_part(c_re), wc_part(c_im_neg)], axis=1).astype(BF16)
    return wb, wc


def _front(x, sc1, sh1, g_mix, w_in_ref):
    n = _rms(x) * g_mix * (1.0 + sc1) + sh1
    proj = _bdot(n, w_in_ref[...])
    u = proj[:, :SSM_W]
    glu = proj[:, SSM_W:SSM_W + CONV_CH] * _sigmoid(proj[:, SSM_W + CONV_CH:])
    return u, glu


def _ssm_out(y_lin, u, d_skip, w_glu_ref, b_glu, g_out_ssm):
    y = _gelu_tanh(y_lin + d_skip * u)
    ys = y * _sigmoid(_bdot(y, w_glu_ref[...]) + b_glu)
    return _rms(ys) * g_out_ssm


def _conv_out(conv, b_dw, ln_g, ln_b, g_out_conv):
    c = conv + b_dw
    mu = jnp.mean(c, axis=-1, keepdims=True)
    cc = c - mu
    var = jnp.mean(cc * cc, axis=-1, keepdims=True)
    ln = cc * lax.rsqrt(var + EPS) * ln_g + ln_b
    yc = ln * _sigmoid(ln)
    return _rms(yc) * g_out_conv


def _route(n2, w_r_ref, b_r):
    rows = n2.shape[0]
    n_hi = n2.astype(BF16)
    n_lo = (n2 - n_hi.astype(F32)).astype(BF16)
    parts = jnp.dot(jnp.concatenate([n_hi, n_lo], axis=0), w_r_ref[...], preferred_element_type=F32)
    lg = (parts[:rows, :LANES] + parts[:rows, LANES:]) + (parts[rows:, :LANES] + parts[rows:, LANES:]) + b_r
    lane = lax.broadcasted_iota(jnp.int32, (rows, LANES), 1).astype(F32)
    ninf = -jnp.inf
    big = 1e9
    gmask = jnp.logical_and(lane >= NE, lane < NE + NG)
    gl = jnp.where(gmask, lg, ninf)
    gmax = jnp.max(gl, axis=-1, keepdims=True)
    gsum = jnp.sum(jnp.where(gmask, jnp.exp(gl - gmax), 0.0), axis=-1, keepdims=True)
    p_top = 1.0 / gsum
    gi = jnp.min(jnp.where(gl == gmax, lane, big), axis=-1, keepdims=True) - NE
    lo = gi * EPG
    emask = jnp.logical_and(lane >= lo, lane < lo + EPG)
    el = jnp.where(emask, lg, ninf)
    m1 = jnp.max(el, axis=-1, keepdims=True)
    i1 = jnp.min(jnp.where(el == m1, lane, big), axis=-1, keepdims=True)
    el2 = jnp.where(lane == i1, ninf, el)
    m2 = jnp.max(el2, axis=-1, keepdims=True)
    i2 = jnp.min(jnp.where(el2 == m2, lane, big), axis=-1, keepdims=True)
    e2 = jnp.exp(m2 - m1)
    den = 1.0 + e2
    w1 = p_top / den
    w2 = p_top * e2 / den
    cnt = jnp.sum(jnp.where(lane == i1, 1.0, 0.0) + jnp.where(lane == i2, 1.0, 0.0),
                  axis=0, keepdims=True)
    cols = (jnp.where(lane == 0.0, i1, 0.0) + jnp.where(lane == 1.0, i2, 0.0)
            + jnp.where(lane == 2.0, w1, 0.0) + jnp.where(lane == 3.0, w2, 0.0))
    return cols.T[0:8, :], cnt


def _tail(x, ns, nc, gt1, sc2, sh2, w_out_ref, g_ffn, w_r_ref, b_r):
    merged = _bdot(jnp.concatenate([ns, nc], axis=-1), w_out_ref[...])
    x1 = x + gt1 * merged
    n2 = _rms(x1) * g_ffn * (1.0 + sc2) + sh2
    rt, cnt = _route(n2, w_r_ref, b_r)
    return x1, n2, rt, cnt


def _prompt_mixer_kernel(x_ref, mod_ref, g_mix_ref, w_in_ref, wb_ref, a_ref, wc_ref, dsk_ref,
                         w_glu_ref, b_glu_ref, w_dw_ref, b_dw_ref, ln_g_ref, ln_b_ref,
                         g_os_ref, g_oc_ref, w_out_ref, g_ffn_ref, w_r_ref, b_r_ref,
                         x1_ref, n2_ref, rt_ref, cnt_ref, hr_ref, hi_ref, cache_ref,
                         sre, sim, hst, ebuf, eshift, ncbuf, *, tc, pt):
    c = pl.program_id(1)
    nc_chunks = pl.num_programs(1)

    @pl.when(c == 0)
    def _():
        hst[...] = jnp.zeros_like(hst)
        ebuf[pl.ds(0, 32), :] = jnp.zeros((32, CONV_CH), F32)

    x = x_ref[...]
    mod = mod_ref[...]
    sh1, sc1, gt1 = mod[0:1], mod[1:2], mod[2:3]
    sh2, sc2 = mod[3:4], mod[4:5]

    u, glu = _front(x, sc1, sh1, g_mix_ref[...], w_in_ref)
    ebuf[pl.ds(32, tc), :] = glu

    for s in range(1, 8):
        eshift[s - 1, pl.ds(0, tc + 24), :] = ebuf[pl.ds(s, tc + 24), :]
    rb = 64
    convs = []
    for r0 in range(0, tc, rb):
        acc = None
        for k in range(KW):
            a8, s = (k + 2) // 8 * 8, (k + 2) % 8
            win = ebuf[pl.ds(r0 + a8, rb), :] if s == 0 else eshift[s - 1, pl.ds(r0 + a8, rb), :]
            term = w_dw_ref[k:k + 1, :] * win
            acc = term if acc is None else acc + term
        convs.append(acc)
    conv = jnp.concatenate(convs, axis=0)
    ncbuf[...] = _conv_out(conv, b_dw_ref[...], ln_g_ref[...], ln_b_ref[...], g_oc_ref[...])

    @pl.when(c == nc_chunks - 1)
    def _():
        cache_ref[...] = ebuf[pl.ds(tc + 2, CB), :]

    ebuf[pl.ds(0, 32), :] = ebuf[pl.ds(tc, 32), :]

    ub = u.astype(BF16)
    for q in range(4):
        r = jnp.dot(ub[:, q * LANES:(q + 1) * LANES], wb_ref[q], preferred_element_type=F32)
        for k in range(4):
            j = 4 * q + k
            sre[pl.ds(j * pt, tc), :] = r[:, k * LANES:(k + 1) * LANES]
            sim[pl.ds(j * pt, tc), :] = r[:, SSM_W + k * LANES:SSM_W + (k + 1) * LANES]

    ar0, ar1 = a_ref[0, 0:8, :], a_ref[0, 8:16, :]
    ai0, ai1 = a_ref[1, 0:8, :], a_ref[1, 8:16, :]

    def step(g, carry):
        hr0, hr1, hi0, hi1 = carry
        t0 = g * SCAN_GROUP
        idx = [(pl.ds(t0 + u, 8, stride=pt), pl.ds(t0 + u + 8 * pt, 8, stride=pt))
               for u in range(SCAN_GROUP)]
        bu = [(sre[i0, :], sre[i1, :], sim[i0, :], sim[i1, :]) for (i0, i1) in idx]
        hs = []
        for br0, br1, bi0, bi1 in bu:
            nr0 = ar0 * hr0 - ai0 * hi0 + br0
            ni0 = ar0 * hi0 + ai0 * hr0 + bi0
            nr1 = ar1 * hr1 - ai1 * hi1 + br1
            ni1 = ar1 * hi1 + ai1 * hr1 + bi1
            hr0, hr1, hi0, hi1 = nr0, nr1, ni0, ni1
            hs.append((nr0, nr1, ni0, ni1))
        for (i0, i1), (nr0, nr1, ni0, ni1) in zip(idx, hs):
            sre[i0, :] = nr0
            sim[i0, :] = ni0
            sre[i1, :] = nr1
            sim[i1, :] = ni1
        return hr0, hr1, hi0, hi1

    init = (hst[0, 0:8, :], hst[0, 8:16, :], hst[1, 0:8, :], hst[1, 8:16, :])
    hr0, hr1, hi0, hi1 = lax.fori_loop(0, tc // SCAN_GROUP, step, init)
    hst[0, 0:8, :] = hr0
    hst[0, 8:16, :] = hr1
    hst[1, 0:8, :] = hi0
    hst[1, 8:16, :] = hi1

    @pl.when(c == nc_chunks - 1)
    def _():
        hr_ref[...] = hst[0]
        hi_ref[...] = hst[1]

    ys = []
    for q in range(4):
        acc = None
        for jj in range(4):
            j = 4 * q + jj
            lhs = jnp.concatenate([sre[pl.ds(j * pt, tc), :], sim[pl.ds(j * pt, tc), :]], axis=-1)
            d = jnp.dot(lhs.astype(BF16), wc_ref[j], preferred_element_type=F32)
            acc = d if acc is None else acc + d
        ys.append(acc)
    y_lin = jnp.concatenate(ys, axis=-1)
    ns = _ssm_out(y_lin, u, dsk_ref[...], w_glu_ref, b_glu_ref[...], g_os_ref[...])
    nc = ncbuf[...]

    x1, n2, rt, cnt = _tail(x, ns, nc, gt1, sc2, sh2, w_out_ref, g_ffn_ref[...], w_r_ref, b_r_ref[...])
    x1_ref[...] = x1
    n2_ref[...] = n2.astype(BF16)
    rt_ref[...] = rt
    cnt_ref[...] = cnt


def _const_spec(shape):
    nd = len(shape)
    return pl.BlockSpec(shape, lambda b, c: (0,) * nd)


def _prompt_mixer_call(x, mod6, wts, tc):
    bsz, t, _ = x.shape
    n_all = bsz * t
    pt = tc + SUBLANES
    assert (pt // SUBLANES) % 2 == 1
    nc = t // tc
    kern = functools.partial(_prompt_mixer_kernel, tc=tc, pt=pt)
    in_specs = [pl.BlockSpec((None, tc, D), lambda b, c: (b, c, 0)),
                pl.BlockSpec((None, 6, D), lambda b, c: (b, 0, 0))]
    in_specs += [_const_spec(w.shape) for w in wts]
    out_shape = (jax.ShapeDtypeStruct((n_all, D), F32),
                 jax.ShapeDtypeStruct((n_all, D), BF16),
                 jax.ShapeDtypeStruct((8, n_all), F32),
                 jax.ShapeDtypeStruct((n_all // tc, 1, LANES), F32),
                 jax.ShapeDtypeStruct((bsz, NCHUNK, LANES), F32),
                 jax.ShapeDtypeStruct((bsz, NCHUNK, LANES), F32),
                 jax.ShapeDtypeStruct((bsz, CB, CONV_CH), F32))
    out_specs = (pl.BlockSpec((tc, D), lambda b, c: (b * nc + c, 0)),
                 pl.BlockSpec((tc, D), lambda b, c: (b * nc + c, 0)),
                 pl.BlockSpec((8, tc), lambda b, c: (0, b * nc + c)),
                 pl.BlockSpec((None, 1, LANES), lambda b, c: (b * nc + c, 0, 0)),
                 pl.BlockSpec((None, NCHUNK, LANES), lambda b, c: (b, 0, 0)),
                 pl.BlockSpec((None, NCHUNK, LANES), lambda b, c: (b, 0, 0)),
                 pl.BlockSpec((None, CB, CONV_CH), lambda b, c: (b, 0, 0)))
    scratch = [pltpu.VMEM((NCHUNK * pt, LANES), F32),
               pltpu.VMEM((NCHUNK * pt, LANES), F32),
               pltpu.VMEM((2, NCHUNK, LANES), F32),
               pltpu.VMEM((tc + 32, CONV_CH), F32),
               pltpu.VMEM((7, tc + 32, CONV_CH), F32),
               pltpu.VMEM((tc, CONV_CH), F32)]
    return pl.pallas_call(
        kern, grid=(bsz, nc), in_specs=in_specs, out_specs=out_specs, out_shape=out_shape,
        scratch_shapes=scratch,
        compiler_params=pltpu.CompilerParams(dimension_semantics=("arbitrary", "arbitrary"),
                                             vmem_limit_bytes=VMEM_LIMIT),
        name="prompt_mixer",
    )(x, mod6, *wts)


def _sample_mixer_kernel(x_ref, mod_ref, h0r_ref, h0i_ref, cache_ref,
                         g_mix_ref, w_in_ref, wb_ref, a_ref, wc_ref, dsk_ref,
                         w_glu_ref, b_glu_ref, w_dw_ref, b_dw_ref, ln_g_ref, ln_b_ref,
                         g_os_ref, g_oc_ref, w_out_ref, g_ffn_ref, w_r_ref, b_r_ref,
                         x1_ref, n2_ref, rt_ref, cnt_ref, hr_ref, hi_ref, glu_ref,
                         sre, sim, *, nb, nt):
    x = x_ref[...]

    def rows(i):
        m = mod_ref[:, i * D:(i + 1) * D]
        return jnp.concatenate([m] * nt, axis=0)

    sh1, sc1, gt1, sh2, sc2 = rows(0), rows(1), rows(2), rows(3), rows(4)
    u, glu = _front(x, sc1, sh1, g_mix_ref[...], w_in_ref)
    glu_ref[...] = glu

    ub = u.astype(BF16)
    for q in range(4):
        r = jnp.dot(ub[:, q * LANES:(q + 1) * LANES], wb_ref[q], preferred_element_type=F32)
        sre[:, q * SSM_W:(q + 1) * SSM_W] = r[:, :SSM_W]
        sim[:, q * SSM_W:(q + 1) * SSM_W] = r[:, SSM_W:]

    ar = a_ref[0:1, :]
    ai = a_ref[1:2, :]
    hr = h0r_ref[...]
    hi = h0i_ref[...]
    for t in range(nt):
        rs = pl.ds(t * nb, nb)
        nr = ar * hr - ai * hi + sre[rs, :]
        ni = ar * hi + ai * hr + sim[rs, :]
        sre[rs, :] = nr
        sim[rs, :] = ni
        hr, hi = nr, ni
    hr_ref[...] = hr
    hi_ref[...] = hi

    ys = []
    for q in range(4):
        acc = None
        for jj in range(4):
            j = 4 * q + jj
            lhs = jnp.concatenate([sre[:, j * LANES:(j + 1) * LANES],
                                   sim[:, j * LANES:(j + 1) * LANES]], axis=-1)
            d = jnp.dot(lhs.astype(BF16), wc_ref[j], preferred_element_type=F32)
            acc = d if acc is None else acc + d
        ys.append(acc)
    y_lin = jnp.concatenate(ys, axis=-1)
    ns = _ssm_out(y_lin, u, dsk_ref[...], w_glu_ref, b_glu_ref[...], g_os_ref[...])

    def ext(jrow):
        if jrow < CB:
            return cache_ref[jrow]
        return glu[(jrow - CB) * nb:(jrow - CB + 1) * nb, :]

    convs = []
    for t in range(nt):
        acc = None
        for k in range(KW):
            term = w_dw_ref[k:k + 1, :] * ext(t + k)
            acc = term if acc is None else acc + term
        convs.append(acc)
    conv = jnp.concatenate(convs, axis=0)
    nc = _conv_out(conv, b_dw_ref[...], ln_g_ref[...], ln_b_ref[...], g_oc_ref[...])

    x1, n2, rt, cnt = _tail(x, ns, nc, gt1, sc2, sh2, w_out_ref, g_ffn_ref[...], w_r_ref, b_r_ref[...])
    x1_ref[...] = x1
    n2_ref[...] = n2.astype(BF16)
    rt_ref[...] = rt
    cnt_ref[...] = cnt


def _sample_mixer_call(x_tm, mod_s, h0r, h0i, cache_tm, wts, nb, nt):
    n = nb * nt
    kern = functools.partial(_sample_mixer_kernel, nb=nb, nt=nt)
    out_shape = (jax.ShapeDtypeStruct((n, D), F32),
                 jax.ShapeDtypeStruct((n, D), BF16),
                 jax.ShapeDtypeStruct((8, n), F32),
                 jax.ShapeDtypeStruct((1, LANES), F32),
                 jax.ShapeDtypeStruct((nb, NSTATE), F32),
                 jax.ShapeDtypeStruct((nb, NSTATE), F32),
                 jax.ShapeDtypeStruct((n, CONV_CH), F32))
    scratch = [pltpu.VMEM((n, NSTATE), F32), pltpu.VMEM((n, NSTATE), F32)]
    return pl.pallas_call(
        kern, out_shape=out_shape, scratch_shapes=scratch,
        compiler_params=pltpu.CompilerParams(vmem_limit_bytes=VMEM_LIMIT),
        name="sample_mixer",
    )(x_tm, mod_s, h0r, h0i, cache_tm, *wts)


ROW_ALIGN = 16
MOE_TD = 512
MOE_BR = MOE_TD * 2 + NE * ROW_ALIGN
MOE_TG = 256


def _slot_positions(rt, base, before):
    t = rt.shape[1]
    e0 = rt[0:1, :]
    e1 = rt[1:2, :]
    sub = lax.broadcasted_iota(jnp.int32, (LANES, t), 0).astype(F32)
    a0 = jnp.where(sub == e0, 1.0, 0.0)
    a1 = jnp.where(sub == e1, 1.0, 0.0)
    at = a0 + a1
    rank = jnp.dot(at.astype(BF16), before, preferred_element_type=F32)
    slot = rank + base
    pos0 = jnp.sum(a0 * slot, axis=0, keepdims=True)
    pos1 = jnp.sum(a1 * slot, axis=0, keepdims=True)
    return pos0, pos1


def _segment_copies(tab_ref, t, n_tiles, buf, hbm, sems, slot, to_hbm, wait):
    for e in range(NE):
        n = pl.multiple_of(tab_ref[t * NE + e], ROW_ALIGN)
        b = pl.multiple_of(tab_ref[(n_tiles + t) * NE + e], ROW_ALIGN)
        d = pl.multiple_of(tab_ref[(2 * n_tiles + t) * NE + e], ROW_ALIGN)
        vm = buf.at[slot, pl.ds(b, n)]
        hb = hbm.at[pl.ds(d, n)]
        cp = pltpu.make_async_copy(vm, hb, sems.at[slot, e]) if to_hbm else \
            pltpu.make_async_copy(hb, vm, sems.at[slot, e])

        @pl.when(n > 0)
        def _():
            if wait:
                cp.wait()
            else:
                cp.start()


def _tile_rows(tab_ref, t, n_tiles):
    return tab_ref[3 * n_tiles * NE + 2 * NE + 1 + t]


def _one_hot_rows(r0, nrows, pos0, pos1):
    row = (lax.broadcasted_iota(jnp.int32, (nrows, MOE_TD), 0) + r0).astype(F32)
    return row == pos0, row == pos1


MOE_BLK = 256


def _dispatch_kernel(tab_ref, rtp_ref, n2p_ref, rts_ref, n2s_ref, base_ref, xs_ref, pos_ref, buf, zbuf, before,
                     sems, zsem,
                     *, n_tiles, n_ptiles, n_gtiles):
    t = pl.program_id(0)
    slot = lax.rem(t, 2)
    first_free = tab_ref[3 * n_tiles * NE + 2 * NE]

    def fill_copy(j):
        d = pl.multiple_of(j * MOE_TG, MOE_TG)
        return pltpu.make_async_copy(zbuf, xs_ref.at[pl.ds(d, MOE_TG)], zsem)

    def fill_start(j, carry):
        fill_copy(j).start()
        return carry

    def fill_wait(j, carry):
        fill_copy(j).wait()
        return carry

    @pl.when(t == 0)
    def _():
        zbuf[...] = jnp.zeros_like(zbuf)
        r = lax.broadcasted_iota(jnp.int32, (MOE_TD, MOE_TD), 0)
        c = lax.broadcasted_iota(jnp.int32, (MOE_TD, MOE_TD), 1)
        before[...] = jnp.where(r < c, 1.0, 0.0).astype(BF16)
        for phase in range(2):
            for e in range(NE):
                d = pl.multiple_of(tab_ref[3 * n_tiles * NE + e], ROW_ALIGN)
                n = pl.multiple_of(tab_ref[3 * n_tiles * NE + NE + e], ROW_ALIGN)
                cp = pltpu.make_async_copy(zbuf.at[pl.ds(0, n)], xs_ref.at[pl.ds(d, n)], sems.at[1, e])

                @pl.when(n > 0)
                def _():
                    if phase == 0:
                        cp.start()
                    else:
                        cp.wait()

        lax.fori_loop(first_free, n_gtiles, fill_start, 0)

    @pl.when(t >= 2)
    def _():
        _segment_copies(tab_ref, t - 2, n_tiles, buf, xs_ref, sems, slot, to_hbm=True, wait=True)

    is_sample = t >= n_ptiles
    rt = jnp.where(is_sample, rts_ref[...], rtp_ref[...])
    n2 = jnp.where(is_sample, n2s_ref[...], n2p_ref[...])
    pos0, pos1 = _slot_positions(rt, base_ref[...], before[...])
    pos_ref[...] = jnp.concatenate([pos0, pos1, jnp.zeros((6, MOE_TD), F32)], axis=0)
    used = _tile_rows(tab_ref, t, n_tiles)

    def group(r0, nrows):
        m0, m1 = _one_hot_rows(r0, nrows, pos0, pos1)
        q = (jnp.where(m0, 1.0, 0.0) + jnp.where(m1, 1.0, 0.0)).astype(BF16)
        buf[slot, pl.ds(r0, nrows), :] = jnp.dot(q, n2, preferred_element_type=F32).astype(BF16)

    group(0, 2 * MOE_TD)
    for r0 in range(2 * MOE_TD, MOE_BR, MOE_BLK):
        @pl.when(used > r0)
        def _():
            group(r0, MOE_BLK)

    _segment_copies(tab_ref, t, n_tiles, buf, xs_ref, sems, slot, to_hbm=True, wait=False)

    @pl.when(t == n_tiles - 1)
    def _():
        if n_tiles >= 2:
            _segment_copies(tab_ref, t - 1, n_tiles, buf, xs_ref, sems, 1 - slot, to_hbm=True, wait=True)
        _segment_copies(tab_ref, t, n_tiles, buf, xs_ref, sems, slot, to_hbm=True, wait=True)
        lax.fori_loop(first_free, n_gtiles, fill_wait, 0)


def _dispatch_call(tab, base_col, rt_p, n2_p, rt_s, n2_s, r_max):
    n_ptiles = n2_p.shape[0] // MOE_TD
    n_tiles = n_ptiles + n2_s.shape[0] // MOE_TD
    last_p = n_ptiles - 1
    return pl.pallas_call(
        functools.partial(_dispatch_kernel, n_tiles=n_tiles, n_ptiles=n_ptiles,
                          n_gtiles=r_max // MOE_TG),
        grid_spec=pltpu.PrefetchScalarGridSpec(
            num_scalar_prefetch=1, grid=(n_tiles,),
            in_specs=[pl.BlockSpec((8, MOE_TD), lambda t, tab: (0, jnp.minimum(t, last_p))),
                      pl.BlockSpec((MOE_TD, D), lambda t, tab: (jnp.minimum(t, last_p), 0)),
                      pl.BlockSpec((8, MOE_TD), lambda t, tab: (0, 0)),
                      pl.BlockSpec((MOE_TD, D), lambda t, tab: (0, 0)),
                      pl.BlockSpec((None, LANES, 1), lambda t, tab: (t, 0, 0))],
            out_specs=(pl.BlockSpec(memory_space=pl.ANY),
                       pl.BlockSpec((None, 8, MOE_TD), lambda t, tab: (t, 0, 0))),
            scratch_shapes=[pltpu.VMEM((2, MOE_BR, D), BF16),
                            pltpu.VMEM((MOE_TG, D), BF16),
                            pltpu.VMEM((MOE_TD, MOE_TD), BF16),
                            pltpu.SemaphoreType.DMA((2, NE)),
                            pltpu.SemaphoreType.DMA(())]),
        out_shape=(jax.ShapeDtypeStruct((r_max, D), BF16),
                   jax.ShapeDtypeStruct((n_tiles, 8, MOE_TD), F32)),
        compiler_params=pltpu.CompilerParams(dimension_semantics=("arbitrary",),
                                             vmem_limit_bytes=VMEM_LIMIT),
        name="moe_dispatch",
    )(tab, rt_p, n2_p, rt_s, n2_s, base_col)


MOE_CK = 6


def _experts_kernel(ctab_ref, xs_ref, w1_ref, w3_ref, w2_ref, ys_ref,
                    w1b, w3b, w2b, xbuf, ybuf, zbuf, in_sem, out_sem, zsem, *, n_gtiles, nc_max):
    e = pl.program_id(0)
    c0 = ctab_ref[e]
    c1 = ctab_ref[e + 1]
    n_chunks = ctab_ref[NE]
    first_free = ctab_ref[NE + 1 + 2 * nc_max]

    def fill_copy(j):
        d = pl.multiple_of(j * MOE_TG, MOE_TG)
        return pltpu.make_async_copy(zbuf, ys_ref.at[pl.ds(d, MOE_TG)], zsem)

    def fill_start(j, carry):
        fill_copy(j).start()
        return carry

    def fill_wait(j, carry):
        fill_copy(j).wait()
        return carry

    def span(c):
        r = pl.multiple_of(ctab_ref[NE + 1 + c], MOE_TG)
        n = pl.multiple_of(ctab_ref[NE + 1 + nc_max + c] * MOE_TG, MOE_TG)
        return r, n

    def in_copy(c, slot):
        r, n = span(c)
        return pltpu.make_async_copy(xs_ref.at[pl.ds(r, n)], xbuf.at[slot, pl.ds(0, n)], in_sem.at[slot])

    def out_copy(c, slot):
        r, n = span(c)
        return pltpu.make_async_copy(ybuf.at[slot, pl.ds(0, n)], ys_ref.at[pl.ds(r, n)], out_sem.at[slot])

    @pl.when(e == 0)
    def _():
        zbuf[...] = jnp.zeros_like(zbuf)
        lax.fori_loop(first_free, n_gtiles, fill_start, 0)

        @pl.when(n_chunks > 0)
        def _():
            in_copy(0, 0).start()

    def compute(slot, rows):
        x = xbuf[slot, pl.ds(0, rows), :]
        a = jnp.dot(x, w1b[...], preferred_element_type=F32)
        b = jnp.dot(x, w3b[...], preferred_element_type=F32)
        hid = a * _sigmoid(a) * b
        y = jnp.dot(hid.astype(BF16), w2b[...], preferred_element_type=F32)
        ybuf[slot, pl.ds(0, rows), :] = y.astype(BF16)

    @pl.when(c1 > c0)
    def _():
        w1b[...] = w1_ref[...].astype(BF16)
        w3b[...] = w3_ref[...].astype(BF16)
        w2b[...] = w2_ref[...].astype(BF16)

        def chunk(c, carry):
            slot = lax.rem(c, 2)

            @pl.when(c + 1 < n_chunks)
            def _():
                in_copy(c + 1, 1 - slot).start()

            in_copy(c, slot).wait()

            @pl.when(c >= 2)
            def _():
                out_copy(c - 2, slot).wait()

            k = ctab_ref[NE + 1 + nc_max + c]
            for kk in range(1, MOE_CK + 1):
                @pl.when(k == kk)
                def _():
                    compute(slot, kk * MOE_TG)

            out_copy(c, slot).start()
            return carry

        lax.fori_loop(c0, c1, chunk, 0)

    @pl.when(e == NE - 1)
    def _():
        @pl.when(n_chunks >= 2)
        def _():
            out_copy(n_chunks - 2, lax.rem(n_chunks, 2)).wait()

        @pl.when(n_chunks >= 1)
        def _():
            out_copy(n_chunks - 1, lax.rem(n_chunks - 1, 2)).wait()

        lax.fori_loop(first_free, n_gtiles, fill_wait, 0)


def _experts_call(ctab, xs, w1, w3, w2, nc_max):
    r_max = xs.shape[0]
    w_map = lambda e, ctab: (e, 0, 0)
    ring = pltpu.VMEM((2, MOE_CK * MOE_TG, D), BF16)
    return pl.pallas_call(
        functools.partial(_experts_kernel, n_gtiles=r_max // MOE_TG, nc_max=nc_max),
        grid_spec=pltpu.PrefetchScalarGridSpec(
            num_scalar_prefetch=1, grid=(NE,),
            in_specs=[pl.BlockSpec(memory_space=pl.ANY),
                      pl.BlockSpec((None, D, DE), w_map),
                      pl.BlockSpec((None, D, DE), w_map),
                      pl.BlockSpec((None, DE, D), w_map)],
            out_specs=pl.BlockSpec(memory_space=pl.ANY),
            scratch_shapes=[pltpu.VMEM((D, DE), BF16), pltpu.VMEM((D, DE), BF16),
                            pltpu.VMEM((DE, D), BF16), ring, ring,
                            pltpu.VMEM((MOE_TG, D), BF16),
                            pltpu.SemaphoreType.DMA((2,)), pltpu.SemaphoreType.DMA((2,)),
                            pltpu.SemaphoreType.DMA(())]),
        out_shape=jax.ShapeDtypeStruct((r_max, D), BF16),
        compiler_params=pltpu.CompilerParams(dimension_semantics=("arbitrary",),
                                             vmem_limit_bytes=VMEM_LIMIT),
        name="moe_experts",
    )(ctab, xs, w1, w3, w2)


def _combine_kernel(tab_ref, rt_ref, pos_ref, x1_ref, gt2_ref, gf_ref, ys_ref, y_ref, buf, acc, sems,
                    *, n_tiles, t_off):
    i = pl.program_id(0)
    t = i + t_off
    slot = lax.rem(i, 2)

    @pl.when(i == 0)
    def _():
        buf[...] = jnp.zeros_like(buf)
        _segment_copies(tab_ref, t, n_tiles, buf, ys_ref, sems, slot, to_hbm=False, wait=False)

    @pl.when(i + 1 < pl.num_programs(0))
    def _():
        _segment_copies(tab_ref, t + 1, n_tiles, buf, ys_ref, sems, 1 - slot, to_hbm=False, wait=False)

    _segment_copies(tab_ref, t, n_tiles, buf, ys_ref, sems, slot, to_hbm=False, wait=True)

    rt = rt_ref[...]
    pos0 = pos_ref[0:1, :]
    pos1 = pos_ref[1:2, :]
    used = _tile_rows(tab_ref, t, n_tiles)

    def ungroup(r0, nrows):
        m0, m1 = _one_hot_rows(r0, nrows, pos0, pos1)
        q = (jnp.where(m0, 1.0, 0.0) + jnp.where(m1, 1.0, 0.0)).astype(BF16)
        gw = jnp.sum(jnp.where(m0, rt[2:3, :], 0.0) + jnp.where(m1, rt[3:4, :], 0.0),
                     axis=1, keepdims=True)
        yv = (buf[slot, pl.ds(r0, nrows), :].astype(F32) * gw).astype(BF16)
        return lax.dot_general(q, yv, (((0,), (0,)), ((), ())), preferred_element_type=F32)

    acc[...] = ungroup(0, 2 * MOE_TD)
    for r0 in range(2 * MOE_TD, MOE_BR, MOE_BLK):
        @pl.when(used > r0)
        def _():
            acc[...] += ungroup(r0, MOE_BLK)

    gt2 = gt2_ref[...]
    if gt2.shape[0] not in (1, MOE_TD):
        gt2 = jnp.concatenate([gt2] * (MOE_TD // gt2.shape[0]), axis=0)
    xo = x1_ref[...] + gt2 * acc[...]
    y_ref[...] = _rms(xo) * gf_ref[...]


def _combine_call(tab, rt, pos, x1, gt2, gt2_spec, g_final, ys, n_tiles, t_off):
    n_out_tiles = x1.shape[0] // MOE_TD
    return pl.pallas_call(
        functools.partial(_combine_kernel, n_tiles=n_tiles, t_off=t_off),
        grid_spec=pltpu.PrefetchScalarGridSpec(
            num_scalar_prefetch=1, grid=(n_out_tiles,),
            in_specs=[pl.BlockSpec((8, MOE_TD), lambda t, tab: (0, t)),
                      pl.BlockSpec((None, 8, MOE_TD), lambda t, tab: (t + t_off, 0, 0)),
                      pl.BlockSpec((MOE_TD, D), lambda t, tab: (t, 0)),
                      gt2_spec,
                      pl.BlockSpec((1, D), lambda t, tab: (0, 0)),
                      pl.BlockSpec(memory_space=pl.ANY)],
            out_specs=pl.BlockSpec((MOE_TD, D), lambda t, tab: (t, 0)),
            scratch_shapes=[pltpu.VMEM((2, MOE_BR, D), BF16),
                            pltpu.VMEM((MOE_TD, D), F32),
                            pltpu.SemaphoreType.DMA((2, NE))]),
        out_shape=jax.ShapeDtypeStruct((n_out_tiles * MOE_TD, D), F32),
        compiler_params=pltpu.CompilerParams(dimension_semantics=("arbitrary",),
                                             vmem_limit_bytes=VMEM_LIMIT),
        name="moe_combine",
    )(tab, rt, pos, x1, gt2, g_final.reshape(1, D), ys)


def _moe_plan(cnt, nc_max):
    cnt_al = (cnt + ROW_ALIGN - 1) // ROW_ALIGN * ROW_ALIGN
    seg_rows = cnt_al.sum(axis=0)
    seg_pad = (seg_rows + MOE_TG - 1) // MOE_TG * MOE_TG
    seg_start = jnp.cumsum(seg_pad) - seg_pad
    dst = seg_start[None, :] + jnp.cumsum(cnt_al, axis=0) - cnt_al
    boff = jnp.cumsum(cnt_al, axis=1) - cnt_al
    tile_end = jnp.cumsum(seg_pad // MOE_TG)
    n_active = tile_end[-1:].astype(jnp.int32)
    tab = jnp.concatenate([cnt_al.ravel(), boff.ravel(), dst.ravel(),
                           seg_start + seg_rows, seg_pad - seg_rows, n_active,
                           cnt_al.sum(axis=1)]).astype(jnp.int32)
    nt = seg_pad // MOE_TG
    nfull = nt // MOE_CK
    rem = nt % MOE_CK
    nch = nfull + (rem > 0)
    cend = jnp.cumsum(nch)
    cstart = cend - nch
    c = jnp.arange(nc_max, dtype=jnp.int32)
    ce = jnp.minimum(jnp.sum(c[:, None] >= cend[None, :], axis=1), NE - 1)
    local = c - cstart[ce]
    valid = c < cend[-1]
    ck = jnp.where(valid, jnp.where(local < nfull[ce], MOE_CK, rem[ce]), 0)
    crow = jnp.where(valid, seg_start[ce] + local * (MOE_CK * MOE_TG), 0)
    ctab = jnp.concatenate([cstart, cend[-1:], crow, ck, n_active]).astype(jnp.int32)
    base_col = jnp.pad(boff, ((0, 0), (0, LANES - NE))).astype(F32)[:, :, None]
    return tab, ctab, base_col


def kernel(x_prompt, x_sample, c_prompt, c_sample, state_ssm_re, state_ssm_im, cache_conv, w_ada, b_ada, g_norm_mix, w_in, ssm_a_re, ssm_a_im, ssm_log_dt, ssm_b_re, ssm_b_im, ssm_c_re, ssm_c_im, ssm_d, w_ssm_glu, b_ssm_glu, w_dw, b_dw, ln_conv_g, ln_conv_b, g_out_ssm, g_out_conv, w_out, g_norm_ffn, w_router_grp, b_router_grp, w_router_exp, b_router_exp, w_exp_gate, w_exp_up, w_exp_down, g_final):
    depth = w_ada.shape[0]
    assert depth == 1
    bsz, seq, _ = x_prompt.shape
    nb, nt, _ = x_sample.shape

    n_c = bsz + nb
    c_pad = -n_c % 16
    c_all = jnp.concatenate([c_prompt, c_sample, jnp.zeros((c_pad, D), F32)], axis=0)
    mod_p, mod_s = _mod_call(c_all, w_ada[0], b_ada[0], bsz, nb)
    mod_p = mod_p.reshape(bsz, 6, D)

    ab_re, ab_im, bb_re, bb_im, c_im_neg = _ssm_prep_call(
        ssm_a_re[0], ssm_a_im[0], ssm_log_dt[0], ssm_b_re[0], ssm_b_im[0], ssm_c_im[0])
    wb, wc = _block_diag_weights(bb_re, bb_im, ssm_c_re[0], c_im_neg)
    a_tok = jnp.stack([ab_re.reshape(NCHUNK, LANES), ab_im.reshape(NCHUNK, LANES)])
    a_row = jnp.stack([ab_re.reshape(NSTATE), ab_im.reshape(NSTATE)])

    w_r = jnp.concatenate([w_router_exp[0].reshape(D, NE), w_router_grp[0],
                           jnp.zeros((D, LANES - NE - NG), F32)], axis=1)
    w_r_hi = w_r.astype(BF16)
    w_r = jnp.concatenate([w_r_hi, (w_r - w_r_hi.astype(F32)).astype(BF16)], axis=1)
    b_r = jnp.concatenate([b_router_exp[0].reshape(NE), b_router_grp[0],
                           jnp.zeros((LANES - NE - NG,), F32)]).reshape(1, LANES)
    w_dw_p = jnp.concatenate([w_dw[0], jnp.zeros((1, CONV_CH), F32)], axis=0)

    row = lambda v: v.reshape(1, -1)
    common_a = (row(g_norm_mix[0]), w_in[0].astype(BF16), wb)
    common_b = (wc, row(ssm_d[0].reshape(SSM_W)), w_ssm_glu[0].astype(BF16), row(b_ssm_glu[0]),
                w_dw_p, row(b_dw[0]), row(ln_conv_g[0]), row(ln_conv_b[0]),
                row(g_out_ssm[0]), row(g_out_conv[0]), w_out[0].astype(BF16),
                row(g_norm_ffn[0]), w_r, b_r)

    n_p = bsz * seq
    n_s = nb * nt
    n_all = n_p + n_s
    assert n_s == MOE_TD and seq % MOE_TD == 0 and MOE_TD % PROMPT_TC == 0
    wts_p = common_a + (a_tok,) + common_b
    x1_p, n2_p, rt_p, cnt_p, hr_p, hi_p, cache_p = _prompt_mixer_call(x_prompt, mod_p, wts_p, PROMPT_TC)

    x_tm = jnp.transpose(x_sample, (1, 0, 2)).reshape(nt * nb, D)
    cache_tm = jnp.transpose(cache_conv[0], (1, 0, 2))
    wts_s = common_a + (a_row,) + common_b
    x1_s, n2_s, rt_s, cnt_s, hr_s, hi_s, glu_s = _sample_mixer_call(
        x_tm, mod_s, state_ssm_re[0].reshape(nb, NSTATE), state_ssm_im[0].reshape(nb, NSTATE),
        cache_tm, wts_s, nb, nt)

    n_ptiles = n_p // MOE_TD
    n_tiles = n_all // MOE_TD
    r_max = -(-(2 * n_all + n_tiles * NE * (ROW_ALIGN - 1) + NE * (MOE_TG - ROW_ALIGN)) // MOE_TG) * MOE_TG
    cnt = jnp.concatenate([cnt_p.reshape(n_ptiles, MOE_TD // PROMPT_TC, LANES).sum(axis=1), cnt_s])
    nc_max = r_max // MOE_TG // MOE_CK + NE
    tab, ctab, base_col = _moe_plan(cnt[:, :NE].astype(jnp.int32), nc_max)
    xs, pos = _dispatch_call(tab, base_col, rt_p, n2_p, rt_s, n2_s, r_max)
    ys = _experts_call(ctab, xs, w_exp_gate[0], w_exp_up[0], w_exp_down[0], nc_max)
    tiles_per_b = seq // MOE_TD
    gt2_p = mod_p[:, 5:6, :]
    y_p = _combine_call(tab, rt_p, pos, x1_p, gt2_p,
                        pl.BlockSpec((None, 1, D), lambda t, tab: (t // tiles_per_b, 0, 0)),
                        g_final, ys, n_tiles, 0)
    y_s = _combine_call(tab, rt_s, pos, x1_s, mod_s,
                        pl.BlockSpec((nb, D), lambda t, tab: (0, 5)),
                        g_final, ys, n_tiles, n_ptiles)

    y_prompt = y_p.reshape(bsz, seq, D)
    y_sample = jnp.transpose(y_s.reshape(nt, nb, D), (1, 0, 2))
    new_cache_s = jnp.concatenate(
        [cache_conv[0][:, nt:, :], jnp.transpose(glu_s.reshape(nt, nb, CONV_CH), (1, 0, 2))], axis=1)
    return (y_prompt, y_sample,
            hr_p.reshape(1, bsz, G, P), hi_p.reshape(1, bsz, G, P), cache_p[None],
            hr_s.reshape(1, nb, G, P), hi_s.reshape(1, nb, G, P), new_cache_s[None])
```

```python
import functools

import jax
import jax.numpy as jnp
import numpy as np
from jax import lax
from jax.experimental import pallas as pl
from jax.experimental.pallas import tpu as pltpu

F32 = jnp.float32
BF16 = jnp.bfloat16

D = 1024
SSM_W = 512
CONV_CH = 512
G = 32
H = 16
P = 64
KW = 31
CB = KW - 1
NE = 32
NG = 4
EPG = 8
DE = 512
EPS = 1e-6
LANES = 128
SUBLANES = 8
NSTATE = G * P
NCHUNK = NSTATE // LANES

PROMPT_TC = 512
SCAN_GROUP = 8
V7X_VMEM_BYTES = 64 * 1024 * 1024
VMEM_LIMIT = V7X_VMEM_BYTES - 8 * 1024 * 1024


def _rms(x):
    return x * lax.rsqrt(jnp.mean(x * x, axis=-1, keepdims=True) + EPS)


def _sigmoid(x):
    return 0.5 * jnp.tanh(0.5 * x) + 0.5


def _gelu_tanh(y):
    c = np.sqrt(2.0 / np.pi).astype(np.float32)
    return y * (0.5 * (1.0 + jnp.tanh(c * (y + 0.044715 * (y * y * y)))))


def _bdot(a, b):
    return jnp.dot(a.astype(BF16), b, preferred_element_type=F32)


def _mod_kernel(c_ref, w_ref, b_ref, op_ref, os_ref):
    c = c_ref[...]
    s = c * _sigmoid(c)
    n = s.shape[0]
    s_hi = s.astype(BF16)
    lhs = jnp.concatenate([s_hi, (s - s_hi.astype(F32)).astype(BF16)], axis=0)
    w = w_ref[...]
    w_hi = w.astype(BF16)
    w_lo = (w - w_hi.astype(F32)).astype(BF16)
    p_hi = jnp.dot(lhs, w_hi, preferred_element_type=F32)
    p_lo = jnp.dot(lhs, w_lo, preferred_element_type=F32)
    res = (p_hi[:n] + p_lo[:n]) + (p_hi[n:] + p_lo[n:]) + b_ref[...]
    n_p = op_ref.shape[0]
    op_ref[...] = res[:n_p]
    os_ref[...] = res[n_p:n_p + os_ref.shape[0]]


def _mod_call(c_all, w_ada, b_ada, n_p, n_s):
    n = c_all.shape[0]
    tn = 512
    return pl.pallas_call(
        _mod_kernel,
        grid=(6 * D // tn,),
        in_specs=[pl.BlockSpec((n, D), lambda j: (0, 0)),
                  pl.BlockSpec((D, tn), lambda j: (0, j)),
                  pl.BlockSpec((1, tn), lambda j: (0, j))],
        out_specs=(pl.BlockSpec((n_p, tn), lambda j: (0, j)),
                   pl.BlockSpec((n_s, tn), lambda j: (0, j))),
        out_shape=(jax.ShapeDtypeStruct((n_p, 6 * D), F32),
                   jax.ShapeDtypeStruct((n_s, 6 * D), F32)),
        compiler_params=pltpu.CompilerParams(dimension_semantics=("arbitrary",)),
        name="mod",
    )(c_all, w_ada, b_ada.reshape(1, 6 * D))


def _ssm_prep_kernel(a_re, a_im, log_dt, b_re, b_im, c_im,
                     ab_re_o, ab_im_o, bb_re_o, bb_im_o, cneg_o):
    lam_re = jnp.minimum(a_re[...], -1e-4)
    lam_im = a_im[...]
    dt = jnp.exp(log_dt[...])
    mag = jnp.exp(lam_re * dt)
    ab_re = mag * jnp.cos(lam_im * dt)
    ab_im = mag * jnp.sin(lam_im * dt)
    den = lam_re * lam_re + lam_im * lam_im
    num_re = ab_re - 1.0
    coef_re = (num_re * lam_re + ab_im * lam_im) / den
    coef_im = (ab_im * lam_re - num_re * lam_im) / den
    ab_re_o[...] = ab_re
    ab_im_o[...] = ab_im
    br = b_re[...]
    bi = b_im[...]
    bb_re_o[...] = coef_re * br - coef_im * bi
    bb_im_o[...] = coef_re * bi + coef_im * br
    cneg_o[...] = -c_im[...]


def _ssm_prep_call(a_re, a_im, log_dt, b_re, b_im, c_im):
    flat = lambda v: v.reshape(1, NSTATE)
    b_hs = lambda v: jnp.transpose(v, (2, 0, 1)).reshape(H, NSTATE)
    dt_row = jnp.broadcast_to(log_dt[:, None], (G, P)).reshape(1, NSTATE)
    ab_re, ab_im, bb_re, bb_im, cneg = pl.pallas_call(
        _ssm_prep_kernel,
        out_shape=(jax.ShapeDtypeStruct((1, NSTATE), F32), jax.ShapeDtypeStruct((1, NSTATE), F32),
                   jax.ShapeDtypeStruct((H, NSTATE), F32), jax.ShapeDtypeStruct((H, NSTATE), F32),
                   jax.ShapeDtypeStruct((G * H, P), F32)),
        name="ssm_prep",
    )(flat(a_re), flat(a_im), dt_row, b_hs(b_re), b_hs(b_im), c_im.reshape(G * H, P))
    ghp = lambda v: jnp.transpose(v.reshape(H, G, P), (1, 0, 2))
    return (ab_re.reshape(G, P), ab_im.reshape(G, P), ghp(bb_re), ghp(bb_im), cneg.reshape(G, H, P))


def _block_diag_weights(bb_re, bb_im, c_re, c_im_neg):
    eye8 = jnp.eye(8, dtype=F32)
    eye4 = jnp.eye(4, dtype=F32)
    eye2 = jnp.eye(2, dtype=F32)

    def wb_part(bb):
        x = bb.reshape(4, 8, H, P)
        return jnp.einsum('qghp,gk->qghkp', x, eye8).reshape(4, 8 * H, 8 * P)

    wb = jnp.concatenate([wb_part(bb_re), wb_part(bb_im)], axis=-1).astype(BF16)

    def wc_part(c):
        x = c.reshape(4, 4, 2, H, P)
        y = jnp.einsum('qjghp,jk,gl->qjgpklh', x, eye4, eye2)
        return y.reshape(NCHUNK, 2 * P, 4 * 2 * H)

    wc = jnp.concatenate([wc_part(c_re), wc_part(c_im_neg)], axis=1).astype(BF16)
    return wb, wc


def _front(x, sc1, sh1, g_mix, w_in_ref):
    n = _rms(x) * g_mix * (1.0 + sc1) + sh1
    proj = _bdot(n, w_in_ref[...])
    u = proj[:, :SSM_W]
    glu = proj[:, SSM_W:SSM_W + CONV_CH] * _sigmoid(proj[:, SSM_W + CONV_CH:])
    return u, glu


def _ssm_out(y_lin, u, d_skip, w_glu_ref, b_glu, g_out_ssm):
    y = _gelu_tanh(y_lin + d_skip * u)
    ys = y * _sigmoid(_bdot(y, w_glu_ref[...]) + b_glu)
    return _rms(ys) * g_out_ssm


def _conv_out(conv, b_dw, ln_g, ln_b, g_out_conv):
    c = conv + b_dw
    mu = jnp.mean(c, axis=-1, keepdims=True)
    cc = c - mu
    var = jnp.mean(cc * cc, axis=-1, keepdims=True)
    ln = cc * lax.rsqrt(var + EPS) * ln_g + ln_b
    yc = ln * _sigmoid(ln)
    return _rms(yc) * g_out_conv


def _route(n2, w_r_ref, b_r):
    rows = n2.shape[0]
    n_hi = n2.astype(BF16)
    n_lo = (n2 - n_hi.astype(F32)).astype(BF16)
    parts = jnp.dot(jnp.concatenate([n_hi, n_lo], axis=0), w_r_ref[...], preferred_element_type=F32)
    lg = (parts[:rows, :LANES] + parts[:rows, LANES:]) + (parts[rows:, :LANES] + parts[rows:, LANES:]) + b_r
    lane = lax.broadcasted_iota(jnp.int32, (rows, LANES), 1).astype(F32)
    ninf = -jnp.inf
    big = 1e9
    gmask = jnp.logical_and(lane >= NE, lane < NE + NG)
    gl = jnp.where(gmask, lg, ninf)
    gmax = jnp.max(gl, axis=-1, keepdims=True)
    gsum = jnp.sum(jnp.where(gmask, jnp.exp(gl - gmax), 0.0), axis=-1, keepdims=True)
    p_top = 1.0 / gsum
    gi = jnp.min(jnp.where(gl == gmax, lane, big), axis=-1, keepdims=True) - NE
    lo = gi * EPG
    emask = jnp.logical_and(lane >= lo, lane < lo + EPG)
    el = jnp.where(emask, lg, ninf)
    m1 = jnp.max(el, axis=-1, keepdims=True)
    i1 = jnp.min(jnp.where(el == m1, lane, big), axis=-1, keepdims=True)
    el2 = jnp.where(lane == i1, ninf, el)
    m2 = jnp.max(el2, axis=-1, keepdims=True)
    i2 = jnp.min(jnp.where(el2 == m2, lane, big), axis=-1, keepdims=True)
    e2 = jnp.exp(m2 - m1)
    den = 1.0 + e2
    w1 = p_top / den
    w2 = p_top * e2 / den
    cnt = jnp.sum(jnp.where(lane == i1, 1.0, 0.0) + jnp.where(lane == i2, 1.0, 0.0),
                  axis=0, keepdims=True)
    cols = (jnp.where(lane == 0.0, i1, 0.0) + jnp.where(lane == 1.0, i2, 0.0)
            + jnp.where(lane == 2.0, w1, 0.0) + jnp.where(lane == 3.0, w2, 0.0))
    return cols.T[0:8, :], cnt


def _tail(x, ns, nc, gt1, sc2, sh2, w_out_ref, g_ffn, w_r_ref, b_r):
    merged = _bdot(jnp.concatenate([ns, nc], axis=-1), w_out_ref[...])
    x1 = x + gt1 * merged
    n2 = _rms(x1) * g_ffn * (1.0 + sc2) + sh2
    rt, cnt = _route(n2, w_r_ref, b_r)
    return x1, n2, rt, cnt


def _prompt_mixer_kernel(x_ref, mod_ref, g_mix_ref, w_in_ref, wb_ref, a_ref, wc_ref, dsk_ref,
                         w_glu_ref, b_glu_ref, w_dw_ref, b_dw_ref, ln_g_ref, ln_b_ref,
                         g_os_ref, g_oc_ref, w_out_ref, g_ffn_ref, w_r_ref, b_r_ref,
                         x1_ref, n2_ref, rt_ref, cnt_ref, hr_ref, hi_ref, cache_ref,
                         sre, sim, hst, ebuf, eshift, ncbuf, *, tc, pt):
    c = pl.program_id(1)
    nc_chunks = pl.num_programs(1)

    @pl.when(c == 0)
    def _():
        hst[...] = jnp.zeros_like(hst)
        ebuf[pl.ds(0, 32), :] = jnp.zeros((32, CONV_CH), F32)

    x = x_ref[...]
    mod = mod_ref[...]
    sh1, sc1, gt1 = mod[0:1], mod[1:2], mod[2:3]
    sh2, sc2 = mod[3:4], mod[4:5]

    u, glu = _front(x, sc1, sh1, g_mix_ref[...], w_in_ref)
    ebuf[pl.ds(32, tc), :] = glu

    for s in range(1, 8):
        eshift[s - 1, pl.ds(0, tc + 24), :] = ebuf[pl.ds(s, tc + 24), :]
    rb = 64
    convs = []
    for r0 in range(0, tc, rb):
        acc = None
        for k in range(KW):
            a8, s = (k + 2) // 8 * 8, (k + 2) % 8
            win = ebuf[pl.ds(r0 + a8, rb), :] if s == 0 else eshift[s - 1, pl.ds(r0 + a8, rb), :]
            term = w_dw_ref[k:k + 1, :] * win
            acc = term if acc is None else acc + term
        convs.append(acc)
    conv = jnp.concatenate(convs, axis=0)
    ncbuf[...] = _conv_out(conv, b_dw_ref[...], ln_g_ref[...], ln_b_ref[...], g_oc_ref[...])

    @pl.when(c == nc_chunks - 1)
    def _():
        cache_ref[...] = ebuf[pl.ds(tc + 2, CB), :]

    ebuf[pl.ds(0, 32), :] = ebuf[pl.ds(tc, 32), :]

    ub = u.astype(BF16)
    for q in range(4):
        r = jnp.dot(ub[:, q * LANES:(q + 1) * LANES], wb_ref[q], preferred_element_type=F32)
        for k in range(4):
            j = 4 * q + k
            sre[pl.ds(j * pt, tc), :] = r[:, k * LANES:(k + 1) * LANES]
            sim[pl.ds(j * pt, tc), :] = r[:, SSM_W + k * LANES:SSM_W + (k + 1) * LANES]

    ar0, ar1 = a_ref[0, 0:8, :], a_ref[0, 8:16, :]
    ai0, ai1 = a_ref[1, 0:8, :], a_ref[1, 8:16, :]

    def step(g, carry):
        hr0, hr1, hi0, hi1 = carry
        t0 = g * SCAN_GROUP
        idx = [(pl.ds(t0 + u, 8, stride=pt), pl.ds(t0 + u + 8 * pt, 8, stride=pt))
               for u in range(SCAN_GROUP)]
        bu = [(sre[i0, :], sre[i1, :], sim[i0, :], sim[i1, :]) for (i0, i1) in idx]
        hs = []
        for br0, br1, bi0, bi1 in bu:
            nr0 = ar0 * hr0 - ai0 * hi0 + br0
            ni0 = ar0 * hi0 + ai0 * hr0 + bi0
            nr1 = ar1 * hr1 - ai1 * hi1 + br1
            ni1 = ar1 * hi1 + ai1 * hr1 + bi1
            hr0, hr1, hi0, hi1 = nr0, nr1, ni0, ni1
            hs.append((nr0, nr1, ni0, ni1))
        for (i0, i1), (nr0, nr1, ni0, ni1) in zip(idx, hs):
            sre[i0, :] = nr0
            sim[i0, :] = ni0
            sre[i1, :] = nr1
            sim[i1, :] = ni1
        return hr0, hr1, hi0, hi1

    init = (hst[0, 0:8, :], hst[0, 8:16, :], hst[1, 0:8, :], hst[1, 8:16, :])
    hr0, hr1, hi0, hi1 = lax.fori_loop(0, tc // SCAN_GROUP, step, init)
    hst[0, 0:8, :] = hr0
    hst[0, 8:16, :] = hr1
    hst[1, 0:8, :] = hi0
    hst[1, 8:16, :] = hi1

    @pl.when(c == nc_chunks - 1)
    def _():
        hr_ref[...] = hst[0]
        hi_ref[...] = hst[1]

    ys = []
    for q in range(4):
        acc = None
        for jj in range(4):
            j = 4 * q + jj
            lhs = jnp.concatenate([sre[pl.ds(j * pt, tc), :], sim[pl.ds(j * pt, tc), :]], axis=-1)
            d = jnp.dot(lhs.astype(BF16), wc_ref[j], preferred_element_type=F32)
            acc = d if acc is None else acc + d
        ys.append(acc)
    y_lin = jnp.concatenate(ys, axis=-1)
    ns = _ssm_out(y_lin, u, dsk_ref[...], w_glu_ref, b_glu_ref[...], g_os_ref[...])
    nc = ncbuf[...]

    x1, n2, rt, cnt = _tail(x, ns, nc, gt1, sc2, sh2, w_out_ref, g_ffn_ref[...], w_r_ref, b_r_ref[...])
    x1_ref[...] = x1
    n2_ref[...] = n2.astype(BF16)
    rt_ref[...] = rt
    cnt_ref[...] = cnt


def _const_spec(shape):
    nd = len(shape)
    return pl.BlockSpec(shape, lambda b, c: (0,) * nd)


def _prompt_mixer_call(x, mod6, wts, tc):
    bsz, t, _ = x.shape
    n_all = bsz * t
    pt = tc + SUBLANES
    assert (pt // SUBLANES) % 2 == 1
    nc = t // tc
    kern = functools.partial(_prompt_mixer_kernel, tc=tc, pt=pt)
    in_specs = [pl.BlockSpec((None, tc, D), lambda b, c: (b, c, 0)),
                pl.BlockSpec((None, 6, D), lambda b, c: (b, 0, 0))]
    in_specs += [_const_spec(w.shape) for w in wts]
    out_shape = (jax.ShapeDtypeStruct((n_all, D), F32),
                 jax.ShapeDtypeStruct((n_all, D), BF16),
                 jax.ShapeDtypeStruct((8, n_all), F32),
                 jax.ShapeDtypeStruct((n_all // tc, 1, LANES), F32),
                 jax.ShapeDtypeStruct((bsz, NCHUNK, LANES), F32),
                 jax.ShapeDtypeStruct((bsz, NCHUNK, LANES), F32),
                 jax.ShapeDtypeStruct((bsz, CB, CONV_CH), F32))
    out_specs = (pl.BlockSpec((tc, D), lambda b, c: (b * nc + c, 0)),
                 pl.BlockSpec((tc, D), lambda b, c: (b * nc + c, 0)),
                 pl.BlockSpec((8, tc), lambda b, c: (0, b * nc + c)),
                 pl.BlockSpec((None, 1, LANES), lambda b, c: (b * nc + c, 0, 0)),
                 pl.BlockSpec((None, NCHUNK, LANES), lambda b, c: (b, 0, 0)),
                 pl.BlockSpec((None, NCHUNK, LANES), lambda b, c: (b, 0, 0)),
                 pl.BlockSpec((None, CB, CONV_CH), lambda b, c: (b, 0, 0)))
    scratch = [pltpu.VMEM((NCHUNK * pt, LANES), F32),
               pltpu.VMEM((NCHUNK * pt, LANES), F32),
               pltpu.VMEM((2, NCHUNK, LANES), F32),
               pltpu.VMEM((tc + 32, CONV_CH), F32),
               pltpu.VMEM((7, tc + 32, CONV_CH), F32),
               pltpu.VMEM((tc, CONV_CH), F32)]
    return pl.pallas_call(
        kern, grid=(bsz, nc), in_specs=in_specs, out_specs=out_specs, out_shape=out_shape,
        scratch_shapes=scratch,
        compiler_params=pltpu.CompilerParams(dimension_semantics=("arbitrary", "arbitrary"),
                                             vmem_limit_bytes=VMEM_LIMIT),
        name="prompt_mixer",
    )(x, mod6, *wts)


def _sample_mixer_kernel(x_ref, mod_ref, h0r_ref, h0i_ref, cache_ref,
                         g_mix_ref, w_in_ref, wb_ref, a_ref, wc_ref, dsk_ref,
                         w_glu_ref, b_glu_ref, w_dw_ref, b_dw_ref, ln_g_ref, ln_b_ref,
                         g_os_ref, g_oc_ref, w_out_ref, g_ffn_ref, w_r_ref, b_r_ref,
                         x1_ref, n2_ref, rt_ref, cnt_ref, hr_ref, hi_ref, glu_ref,
                         sre, sim, *, nb, nt):
    x = x_ref[...]

    def rows(i):
        m = mod_ref[:, i * D:(i + 1) * D]
        return jnp.concatenate([m] * nt, axis=0)

    sh1, sc1, gt1, sh2, sc2 = rows(0), rows(1), rows(2), rows(3), rows(4)
    u, glu = _front(x, sc1, sh1, g_mix_ref[...], w_in_ref)
    glu_ref[...] = glu

    ub = u.astype(BF16)
    for q in range(4):
        r = jnp.dot(ub[:, q * LANES:(q + 1) * LANES], wb_ref[q], preferred_element_type=F32)
        sre[:, q * SSM_W:(q + 1) * SSM_W] = r[:, :SSM_W]
        sim[:, q * SSM_W:(q + 1) * SSM_W] = r[:, SSM_W:]

    ar = a_ref[0:1, :]
    ai = a_ref[1:2, :]
    hr = h0r_ref[...]
    hi = h0i_ref[...]
    for t in range(nt):
        rs = pl.ds(t * nb, nb)
        nr = ar * hr - ai * hi + sre[rs, :]
        ni = ar * hi + ai * hr + sim[rs, :]
        sre[rs, :] = nr
        sim[rs, :] = ni
        hr, hi = nr, ni
    hr_ref[...] = hr
    hi_ref[...] = hi

    ys = []
    for q in range(4):
        acc = None
        for jj in range(4):
            j = 4 * q + jj
            lhs = jnp.concatenate([sre[:, j * LANES:(j + 1) * LANES],
                                   sim[:, j * LANES:(j + 1) * LANES]], axis=-1)
            d = jnp.dot(lhs.astype(BF16), wc_ref[j], preferred_element_type=F32)
            acc = d if acc is None else acc + d
        ys.append(acc)
    y_lin = jnp.concatenate(ys, axis=-1)
    ns = _ssm_out(y_lin, u, dsk_ref[...], w_glu_ref, b_glu_ref[...], g_os_ref[...])

    def ext(jrow):
        if jrow < CB:
            return cache_ref[jrow]
        return glu[(jrow - CB) * nb:(jrow - CB + 1) * nb, :]

    convs = []
    for t in range(nt):
        acc = None
        for k in range(KW):
            term = w_dw_ref[k:k + 1, :] * ext(t + k)
            acc = term if acc is None else acc + term
        convs.append(acc)
    conv = jnp.concatenate(convs, axis=0)
    nc = _conv_out(conv, b_dw_ref[...], ln_g_ref[...], ln_b_ref[...], g_oc_ref[...])

    x1, n2, rt, cnt = _tail(x, ns, nc, gt1, sc2, sh2, w_out_ref, g_ffn_ref[...], w_r_ref, b_r_ref[...])
    x1_ref[...] = x1
    n2_ref[...] = n2.astype(BF16)
    rt_ref[...] = rt
    cnt_ref[...] = cnt


def _sample_mixer_call(x_tm, mod_s, h0r, h0i, cache_tm, wts, nb, nt):
    n = nb * nt
    kern = functools.partial(_sample_mixer_kernel, nb=nb, nt=nt)
    out_shape = (jax.ShapeDtypeStruct((n, D), F32),
                 jax.ShapeDtypeStruct((n, D), BF16),
                 jax.ShapeDtypeStruct((8, n), F32),
                 jax.ShapeDtypeStruct((1, LANES), F32),
                 jax.ShapeDtypeStruct((nb, NSTATE), F32),
                 jax.ShapeDtypeStruct((nb, NSTATE), F32),
                 jax.ShapeDtypeStruct((n, CONV_CH), F32))
    scratch = [pltpu.VMEM((n, NSTATE), F32), pltpu.VMEM((n, NSTATE), F32)]
    return pl.pallas_call(
        kern, out_shape=out_shape, scratch_shapes=scratch,
        compiler_params=pltpu.CompilerParams(vmem_limit_bytes=VMEM_LIMIT),
        name="sample_mixer",
    )(x_tm, mod_s, h0r, h0i, cache_tm, *wts)


ROW_ALIGN = 16
MOE_TD = 512
MOE_BR = MOE_TD * 2 + NE * ROW_ALIGN
MOE_TG = 256


def _slot_positions(rt, base, before):
    t = rt.shape[1]
    e0 = rt[0:1, :]
    e1 = rt[1:2, :]
    sub = lax.broadcasted_iota(jnp.int32, (LANES, t), 0).astype(F32)
    a0 = jnp.where(sub == e0, 1.0, 0.0)
    a1 = jnp.where(sub == e1, 1.0, 0.0)
    at = a0 + a1
    rank = jnp.dot(at.astype(BF16), before, preferred_element_type=F32)
    slot = rank + base
    pos0 = jnp.sum(a0 * slot, axis=0, keepdims=True)
    pos1 = jnp.sum(a1 * slot, axis=0, keepdims=True)
    return pos0, pos1


def _segment_copies(tab_ref, t, n_tiles, buf, hbm, sems, slot, to_hbm, wait):
    for e in range(NE):
        n = pl.multiple_of(tab_ref[t * NE + e], ROW_ALIGN)
        b = pl.multiple_of(tab_ref[(n_tiles + t) * NE + e], ROW_ALIGN)
        d = pl.multiple_of(tab_ref[(2 * n_tiles + t) * NE + e], ROW_ALIGN)
        vm = buf.at[slot, pl.ds(b, n)]
        hb = hbm.at[pl.ds(d, n)]
        cp = pltpu.make_async_copy(vm, hb, sems.at[slot, e]) if to_hbm else \
            pltpu.make_async_copy(hb, vm, sems.at[slot, e])

        @pl.when(n > 0)
        def _():
            if wait:
                cp.wait()
            else:
                cp.start()


def _tile_rows(tab_ref, t, n_tiles):
    return tab_ref[3 * n_tiles * NE + 2 * NE + 1 + t]


def _one_hot_rows(r0, nrows, pos0, pos1):
    row = (lax.broadcasted_iota(jnp.int32, (nrows, MOE_TD), 0) + r0).astype(F32)
    return row == pos0, row == pos1


MOE_BLK = 256


def _dispatch_kernel(tab_ref, rtp_ref, n2p_ref, rts_ref, n2s_ref, base_ref, xs_ref, pos_ref, buf, zbuf, before,
                     sems, zsem,
                     *, n_tiles, n_ptiles, n_gtiles):
    t = pl.program_id(0)
    slot = lax.rem(t, 2)
    first_free = tab_ref[3 * n_tiles * NE + 2 * NE]

    def fill_copy(j):
        d = pl.multiple_of(j * MOE_TG, MOE_TG)
        return pltpu.make_async_copy(zbuf, xs_ref.at[pl.ds(d, MOE_TG)], zsem)

    def fill_start(j, carry):
        fill_copy(j).start()
        return carry

    def fill_wait(j, carry):
        fill_copy(j).wait()
        return carry

    @pl.when(t == 0)
    def _():
        zbuf[...] = jnp.zeros_like(zbuf)
        r = lax.broadcasted_iota(jnp.int32, (MOE_TD, MOE_TD), 0)
        c = lax.broadcasted_iota(jnp.int32, (MOE_TD, MOE_TD), 1)
        before[...] = jnp.where(r < c, 1.0, 0.0).astype(BF16)
        for phase in range(2):
            for e in range(NE):
                d = pl.multiple_of(tab_ref[3 * n_tiles * NE + e], ROW_ALIGN)
                n = pl.multiple_of(tab_ref[3 * n_tiles * NE + NE + e], ROW_ALIGN)
                cp = pltpu.make_async_copy(zbuf.at[pl.ds(0, n)], xs_ref.at[pl.ds(d, n)], sems.at[1, e])

                @pl.when(n > 0)
                def _():
                    if phase == 0:
                        cp.start()
                    else:
                        cp.wait()

        lax.fori_loop(first_free, n_gtiles, fill_start, 0)

    @pl.when(t >= 2)
    def _():
        _segment_copies(tab_ref, t - 2, n_tiles, buf, xs_ref, sems, slot, to_hbm=True, wait=True)

    is_sample = t >= n_ptiles
    rt = jnp.where(is_sample, rts_ref[...], rtp_ref[...])
    n2 = jnp.where(is_sample, n2s_ref[...], n2p_ref[...])
    pos0, pos1 = _slot_positions(rt, base_ref[...], before[...])
    pos_ref[...] = jnp.concatenate([pos0, pos1, jnp.zeros((6, MOE_TD), F32)], axis=0)
    used = _tile_rows(tab_ref, t, n_tiles)

    def group(r0, nrows):
        m0, m1 = _one_hot_rows(r0, nrows, pos0, pos1)
        q = (jnp.where(m0, 1.0, 0.0) + jnp.where(m1, 1.0, 0.0)).astype(BF16)
        buf[slot, pl.ds(r0, nrows), :] = jnp.dot(q, n2, preferred_element_type=F32).astype(BF16)

    group(0, 2 * MOE_TD)
    for r0 in range(2 * MOE_TD, MOE_BR, MOE_BLK):
        @pl.when(used > r0)
        def _():
            group(r0, MOE_BLK)

    _segment_copies(tab_ref, t, n_tiles, buf, xs_ref, sems, slot, to_hbm=True, wait=False)

    @pl.when(t == n_tiles - 1)
    def _():
        if n_tiles >= 2:
            _segment_copies(tab_ref, t - 1, n_tiles, buf, xs_ref, sems, 1 - slot, to_hbm=True, wait=True)
        _segment_copies(tab_ref, t, n_tiles, buf, xs_ref, sems, slot, to_hbm=True, wait=True)
        lax.fori_loop(first_free, n_gtiles, fill_wait, 0)


def _dispatch_call(tab, base_col, rt_p, n2_p, rt_s, n2_s, r_max):
    n_ptiles = n2_p.shape[0] // MOE_TD
    n_tiles = n_ptiles + n2_s.shape[0] // MOE_TD
    last_p = n_ptiles - 1
    return pl.pallas_call(
        functools.partial(_dispatch_kernel, n_tiles=n_tiles, n_ptiles=n_ptiles,
                          n_gtiles=r_max // MOE_TG),
        grid_spec=pltpu.PrefetchScalarGridSpec(
            num_scalar_prefetch=1, grid=(n_tiles,),
            in_specs=[pl.BlockSpec((8, MOE_TD), lambda t, tab: (0, jnp.minimum(t, last_p))),
                      pl.BlockSpec((MOE_TD, D), lambda t, tab: (jnp.minimum(t, last_p), 0)),
                      pl.BlockSpec((8, MOE_TD), lambda t, tab: (0, 0)),
                      pl.BlockSpec((MOE_TD, D), lambda t, tab: (0, 0)),
                      pl.BlockSpec((None, LANES, 1), lambda t, tab: (t, 0, 0))],
            out_specs=(pl.BlockSpec(memory_space=pl.ANY),
                       pl.BlockSpec((None, 8, MOE_TD), lambda t, tab: (t, 0, 0))),
            scratch_shapes=[pltpu.VMEM((2, MOE_BR, D), BF16),
                            pltpu.VMEM((MOE_TG, D), BF16),
                            pltpu.VMEM((MOE_TD, MOE_TD), BF16),
                            pltpu.SemaphoreType.DMA((2, NE)),
                            pltpu.SemaphoreType.DMA(())]),
        out_shape=(jax.ShapeDtypeStruct((r_max, D), BF16),
                   jax.ShapeDtypeStruct((n_tiles, 8, MOE_TD), F32)),
        compiler_params=pltpu.CompilerParams(dimension_semantics=("arbitrary",),
                                             vmem_limit_bytes=VMEM_LIMIT),
        name="moe_dispatch",
    )(tab, rt_p, n2_p, rt_s, n2_s, base_col)


MOE_CK = 6


def _experts_kernel(ctab_ref, xs_ref, w1_ref, w3_ref, w2_ref, ys_ref,
                    xbuf, ybuf, zbuf, in_sem, out_sem, zsem, *, n_gtiles, nc_max):
    e = pl.program_id(0)
    c0 = ctab_ref[e]
    c1 = ctab_ref[e + 1]
    n_chunks = ctab_ref[NE]
    first_free = ctab_ref[NE + 1 + 2 * nc_max]

    def fill_copy(j):
        d = pl.multiple_of(j * MOE_TG, MOE_TG)
        return pltpu.make_async_copy(zbuf, ys_ref.at[pl.ds(d, MOE_TG)], zsem)

    def fill_start(j, carry):
        fill_copy(j).start()
        return carry

    def fill_wait(j, carry):
        fill_copy(j).wait()
        return carry

    def span(c):
        r = pl.multiple_of(ctab_ref[NE + 1 + c], MOE_TG)
        n = pl.multiple_of(ctab_ref[NE + 1 + nc_max + c] * MOE_TG, MOE_TG)
        return r, n

    def in_copy(c, slot):
        r, n = span(c)
        return pltpu.make_async_copy(xs_ref.at[pl.ds(r, n)], xbuf.at[slot, pl.ds(0, n)], in_sem.at[slot])

    def out_copy(c, slot):
        r, n = span(c)
        return pltpu.make_async_copy(ybuf.at[slot, pl.ds(0, n)], ys_ref.at[pl.ds(r, n)], out_sem.at[slot])

    @pl.when(e == 0)
    def _():
        zbuf[...] = jnp.zeros_like(zbuf)
        lax.fori_loop(first_free, n_gtiles, fill_start, 0)

        @pl.when(n_chunks > 0)
        def _():
            in_copy(0, 0).start()

    def compute(slot, rows):
        x = xbuf[slot, pl.ds(0, rows), :]
        a = jnp.dot(x, w1_ref[...].astype(BF16), preferred_element_type=F32)
        b = jnp.dot(x, w3_ref[...].astype(BF16), preferred_element_type=F32)
        hid = a * _sigmoid(a) * b
        y = jnp.dot(hid.astype(BF16), w2_ref[...].astype(BF16), preferred_element_type=F32)
        ybuf[slot, pl.ds(0, rows), :] = y.astype(BF16)

    @pl.when(c1 > c0)
    def _():
        def chunk(c, carry):
            slot = lax.rem(c, 2)

            @pl.when(c + 1 < n_chunks)
            def _():
                in_copy(c + 1, 1 - slot).start()

            in_copy(c, slot).wait()

            @pl.when(c >= 2)
            def _():
                out_copy(c - 2, slot).wait()

            k = ctab_ref[NE + 1 + nc_max + c]
            for kk in range(1, MOE_CK + 1):
                @pl.when(k == kk)
                def _():
                    compute(slot, kk * MOE_TG)

            out_copy(c, slot).start()
            return carry

        lax.fori_loop(c0, c1, chunk, 0)

    @pl.when(e == NE - 1)
    def _():
        @pl.when(n_chunks >= 2)
        def _():
            out_copy(n_chunks - 2, lax.rem(n_chunks, 2)).wait()

        @pl.when(n_chunks >= 1)
        def _():
            out_copy(n_chunks - 1, lax.rem(n_chunks - 1, 2)).wait()

        lax.fori_loop(first_free, n_gtiles, fill_wait, 0)


def _experts_call(ctab, xs, w1, w3, w2, nc_max):
    r_max = xs.shape[0]
    w_map = lambda e, ctab: (e, 0, 0)
    ring = pltpu.VMEM((2, MOE_CK * MOE_TG, D), BF16)
    return pl.pallas_call(
        functools.partial(_experts_kernel, n_gtiles=r_max // MOE_TG, nc_max=nc_max),
        grid_spec=pltpu.PrefetchScalarGridSpec(
            num_scalar_prefetch=1, grid=(NE,),
            in_specs=[pl.BlockSpec(memory_space=pl.ANY),
                      pl.BlockSpec((None, D, DE), w_map),
                      pl.BlockSpec((None, D, DE), w_map),
                      pl.BlockSpec((None, DE, D), w_map)],
            out_specs=pl.BlockSpec(memory_space=pl.ANY),
            scratch_shapes=[ring, ring,
                            pltpu.VMEM((MOE_TG, D), BF16),
                            pltpu.SemaphoreType.DMA((2,)), pltpu.SemaphoreType.DMA((2,)),
                            pltpu.SemaphoreType.DMA(())]),
        out_shape=jax.ShapeDtypeStruct((r_max, D), BF16),
        compiler_params=pltpu.CompilerParams(dimension_semantics=("arbitrary",),
                                             vmem_limit_bytes=VMEM_LIMIT),
        name="moe_experts",
    )(ctab, xs, w1, w3, w2)


def _combine_kernel(tab_ref, rt_ref, pos_ref, x1_ref, gt2_ref, gf_ref, ys_ref, y_ref, buf, acc, sems,
                    *, n_tiles, t_off):
    i = pl.program_id(0)
    t = i + t_off
    slot = lax.rem(i, 2)

    @pl.when(i == 0)
    def _():
        buf[...] = jnp.zeros_like(buf)
        _segment_copies(tab_ref, t, n_tiles, buf, ys_ref, sems, slot, to_hbm=False, wait=False)

    @pl.when(i + 1 < pl.num_programs(0))
    def _():
        _segment_copies(tab_ref, t + 1, n_tiles, buf, ys_ref, sems, 1 - slot, to_hbm=False, wait=False)

    _segment_copies(tab_ref, t, n_tiles, buf, ys_ref, sems, slot, to_hbm=False, wait=True)

    rt = rt_ref[...]
    pos0 = pos_ref[0:1, :]
    pos1 = pos_ref[1:2, :]
    used = _tile_rows(tab_ref, t, n_tiles)

    def ungroup(r0, nrows):
        m0, m1 = _one_hot_rows(r0, nrows, pos0, pos1)
        q = (jnp.where(m0, 1.0, 0.0) + jnp.where(m1, 1.0, 0.0)).astype(BF16)
        gw = jnp.sum(jnp.where(m0, rt[2:3, :], 0.0) + jnp.where(m1, rt[3:4, :], 0.0),
                     axis=1, keepdims=True)
        yv = (buf[slot, pl.ds(r0, nrows), :].astype(F32) * gw).astype(BF16)
        return lax.dot_general(q, yv, (((0,), (0,)), ((), ())), preferred_element_type=F32)

    acc[...] = ungroup(0, 2 * MOE_TD)
    for r0 in range(2 * MOE_TD, MOE_BR, MOE_BLK):
        @pl.when(used > r0)
        def _():
            acc[...] += ungroup(r0, MOE_BLK)

    gt2 = gt2_ref[...]
    if gt2.shape[0] not in (1, MOE_TD):
        gt2 = jnp.concatenate([gt2] * (MOE_TD // gt2.shape[0]), axis=0)
    xo = x1_ref[...] + gt2 * acc[...]
    y_ref[...] = _rms(xo) * gf_ref[...]


def _combine_call(tab, rt, pos, x1, gt2, gt2_spec, g_final, ys, n_tiles, t_off):
    n_out_tiles = x1.shape[0] // MOE_TD
    return pl.pallas_call(
        functools.partial(_combine_kernel, n_tiles=n_tiles, t_off=t_off),
        grid_spec=pltpu.PrefetchScalarGridSpec(
            num_scalar_prefetch=1, grid=(n_out_tiles,),
            in_specs=[pl.BlockSpec((8, MOE_TD), lambda t, tab: (0, t)),
                      pl.BlockSpec((None, 8, MOE_TD), lambda t, tab: (t + t_off, 0, 0)),
                      pl.BlockSpec((MOE_TD, D), lambda t, tab: (t, 0)),
                      gt2_spec,
                      pl.BlockSpec((1, D), lambda t, tab: (0, 0)),
                      pl.BlockSpec(memory_space=pl.ANY)],
            out_specs=pl.BlockSpec((MOE_TD, D), lambda t, tab: (t, 0)),
            scratch_shapes=[pltpu.VMEM((2, MOE_BR, D), BF16),
                            pltpu.VMEM((MOE_TD, D), F32),
                            pltpu.SemaphoreType.DMA((2, NE))]),
        out_shape=jax.ShapeDtypeStruct((n_out_tiles * MOE_TD, D), F32),
        compiler_params=pltpu.CompilerParams(dimension_semantics=("arbitrary",),
                                             vmem_limit_bytes=VMEM_LIMIT),
        name="moe_combine",
    )(tab, rt, pos, x1, gt2, g_final.reshape(1, D), ys)


def _moe_plan(cnt, nc_max):
    cnt_al = (cnt + ROW_ALIGN - 1) // ROW_ALIGN * ROW_ALIGN
    seg_rows = cnt_al.sum(axis=0)
    seg_pad = (seg_rows + MOE_TG - 1) // MOE_TG * MOE_TG
    seg_start = jnp.cumsum(seg_pad) - seg_pad
    dst = seg_start[None, :] + jnp.cumsum(cnt_al, axis=0) - cnt_al
    boff = jnp.cumsum(cnt_al, axis=1) - cnt_al
    tile_end = jnp.cumsum(seg_pad // MOE_TG)
    n_active = tile_end[-1:].astype(jnp.int32)
    tab = jnp.concatenate([cnt_al.ravel(), boff.ravel(), dst.ravel(),
                           seg_start + seg_rows, seg_pad - seg_rows, n_active,
                           cnt_al.sum(axis=1)]).astype(jnp.int32)
    nt = seg_pad // MOE_TG
    nfull = nt // MOE_CK
    rem = nt % MOE_CK
    nch = nfull + (rem > 0)
    cend = jnp.cumsum(nch)
    cstart = cend - nch
    c = jnp.arange(nc_max, dtype=jnp.int32)
    ce = jnp.minimum(jnp.sum(c[:, None] >= cend[None, :], axis=1), NE - 1)
    local = c - cstart[ce]
    valid = c < cend[-1]
    ck = jnp.where(valid, jnp.where(local < nfull[ce], MOE_CK, rem[ce]), 0)
    crow = jnp.where(valid, seg_start[ce] + local * (MOE_CK * MOE_TG), 0)
    ctab = jnp.concatenate([cstart, cend[-1:], crow, ck, n_active]).astype(jnp.int32)
    base_col = jnp.pad(boff, ((0, 0), (0, LANES - NE))).astype(F32)[:, :, None]
    return tab, ctab, base_col


def kernel(x_prompt, x_sample, c_prompt, c_sample, state_ssm_re, state_ssm_im, cache_conv, w_ada, b_ada, g_norm_mix, w_in, ssm_a_re, ssm_a_im, ssm_log_dt, ssm_b_re, ssm_b_im, ssm_c_re, ssm_c_im, ssm_d, w_ssm_glu, b_ssm_glu, w_dw, b_dw, ln_conv_g, ln_conv_b, g_out_ssm, g_out_conv, w_out, g_norm_ffn, w_router_grp, b_router_grp, w_router_exp, b_router_exp, w_exp_gate, w_exp_up, w_exp_down, g_final):
    depth = w_ada.shape[0]
    assert depth == 1
    bsz, seq, _ = x_prompt.shape
    nb, nt, _ = x_sample.shape

    n_c = bsz + nb
    c_pad = -n_c % 16
    c_all = jnp.concatenate([c_prompt, c_sample, jnp.zeros((c_pad, D), F32)], axis=0)
    mod_p, mod_s = _mod_call(c_all, w_ada[0], b_ada[0], bsz, nb)
    mod_p = mod_p.reshape(bsz, 6, D)

    ab_re, ab_im, bb_re, bb_im, c_im_neg = _ssm_prep_call(
        ssm_a_re[0], ssm_a_im[0], ssm_log_dt[0], ssm_b_re[0], ssm_b_im[0], ssm_c_im[0])
    wb, wc = _block_diag_weights(bb_re, bb_im, ssm_c_re[0], c_im_neg)
    a_tok = jnp.stack([ab_re.reshape(NCHUNK, LANES), ab_im.reshape(NCHUNK, LANES)])
    a_row = jnp.stack([ab_re.reshape(NSTATE), ab_im.reshape(NSTATE)])

    w_r = jnp.concatenate([w_router_exp[0].reshape(D, NE), w_router_grp[0],
                           jnp.zeros((D, LANES - NE - NG), F32)], axis=1)
    w_r_hi = w_r.astype(BF16)
    w_r = jnp.concatenate([w_r_hi, (w_r - w_r_hi.astype(F32)).astype(BF16)], axis=1)
    b_r = jnp.concatenate([b_router_exp[0].reshape(NE), b_router_grp[0],
                           jnp.zeros((LANES - NE - NG,), F32)]).reshape(1, LANES)
    w_dw_p = jnp.concatenate([w_dw[0], jnp.zeros((1, CONV_CH), F32)], axis=0)

    row = lambda v: v.reshape(1, -1)
    common_a = (row(g_norm_mix[0]), w_in[0].astype(BF16), wb)
    common_b = (wc, row(ssm_d[0].reshape(SSM_W)), w_ssm_glu[0].astype(BF16), row(b_ssm_glu[0]),
                w_dw_p, row(b_dw[0]), row(ln_conv_g[0]), row(ln_conv_b[0]),
                row(g_out_ssm[0]), row(g_out_conv[0]), w_out[0].astype(BF16),
                row(g_norm_ffn[0]), w_r, b_r)

    n_p = bsz * seq
    n_s = nb * nt
    n_all = n_p + n_s
    assert n_s == MOE_TD and seq % MOE_TD == 0 and MOE_TD % PROMPT_TC == 0
    wts_p = common_a + (a_tok,) + common_b
    x1_p, n2_p, rt_p, cnt_p, hr_p, hi_p, cache_p = _prompt_mixer_call(x_prompt, mod_p, wts_p, PROMPT_TC)

    x_tm = jnp.transpose(x_sample, (1, 0, 2)).reshape(nt * nb, D)
    cache_tm = jnp.transpose(cache_conv[0], (1, 0, 2))
    wts_s = common_a + (a_row,) + common_b
    x1_s, n2_s, rt_s, cnt_s, hr_s, hi_s, glu_s = _sample_mixer_call(
        x_tm, mod_s, state_ssm_re[0].reshape(nb, NSTATE), state_ssm_im[0].reshape(nb, NSTATE),
        cache_tm, wts_s, nb, nt)

    n_ptiles = n_p // MOE_TD
    n_tiles = n_all // MOE_TD
    r_max = -(-(2 * n_all + n_tiles * NE * (ROW_ALIGN - 1) + NE * (MOE_TG - ROW_ALIGN)) // MOE_TG) * MOE_TG
    cnt = jnp.concatenate([cnt_p.reshape(n_ptiles, MOE_TD // PROMPT_TC, LANES).sum(axis=1), cnt_s])
    nc_max = r_max // MOE_TG // MOE_CK + NE
    tab, ctab, base_col = _moe_plan(cnt[:, :NE].astype(jnp.int32), nc_max)
    xs, pos = _dispatch_call(tab, base_col, rt_p, n2_p, rt_s, n2_s, r_max)
    ys = _experts_call(ctab, xs, w_exp_gate[0], w_exp_up[0], w_exp_down[0], nc_max)
    tiles_per_b = seq // MOE_TD
    gt2_p = mod_p[:, 5:6, :]
    y_p = _combine_call(tab, rt_p, pos, x1_p, gt2_p,
                        pl.BlockSpec((None, 1, D), lambda t, tab: (t // tiles_per_b, 0, 0)),
                        g_final, ys, n_tiles, 0)
    y_s = _combine_call(tab, rt_s, pos, x1_s, mod_s,
                        pl.BlockSpec((nb, D), lambda t, tab: (0, 5)),
                        g_final, ys, n_tiles, n_ptiles)

    y_prompt = y_p.reshape(bsz, seq, D)
    y_sample = jnp.transpose(y_s.reshape(nt, nb, D), (1, 0, 2))
    new_cache_s = jnp.concatenate(
        [cache_conv[0][:, nt:, :], jnp.transpose(glu_s.reshape(nt, nb, CONV_CH), (1, 0, 2))], axis=1)
    return (y_prompt, y_sample,
            hr_p.reshape(1, bsz, G, P), hi_p.reshape(1, bsz, G, P), cache_p[None],
            hr_s.reshape(1, nb, G, P), hi_s.reshape(1, nb, G, P), new_cache_s[None])
```

```python
import functools

import jax
import jax.numpy as jnp
import numpy as np
from jax import lax
from jax.experimental import pallas as pl
from jax.experimental.pallas import tpu as pltpu

F32 = jnp.float32
BF16 = jnp.bfloat16

D = 1024
SSM_W = 512
CONV_CH = 512
G = 32
H = 16
P = 64
KW = 31
CB = KW - 1
NE = 32
NG = 4
EPG = 8
DE = 512
EPS = 1e-6
LANES = 128
SUBLANES = 8
NSTATE = G * P
NCHUNK = NSTATE // LANES

PROMPT_TC = 512
SCAN_GROUP = 8
V7X_VMEM_BYTES = 64 * 1024 * 1024
VMEM_LIMIT = V7X_VMEM_BYTES - 8 * 1024 * 1024


def _rms(x):
    return x * lax.rsqrt(jnp.mean(x * x, axis=-1, keepdims=True) + EPS)


def _sigmoid(x):
    return 0.5 * jnp.tanh(0.5 * x) + 0.5


def _gelu_tanh(y):
    c = np.sqrt(2.0 / np.pi).astype(np.float32)
    return y * (0.5 * (1.0 + jnp.tanh(c * (y + 0.044715 * (y * y * y)))))


def _bdot(a, b):
    return jnp.dot(a.astype(BF16), b, preferred_element_type=F32)


def _mod_kernel(c_ref, w_ref, b_ref, op_ref, os_ref):
    c = c_ref[...]
    s = c * _sigmoid(c)
    n = s.shape[0]
    s_hi = s.astype(BF16)
    lhs = jnp.concatenate([s_hi, (s - s_hi.astype(F32)).astype(BF16)], axis=0)
    w = w_ref[...]
    w_hi = w.astype(BF16)
    w_lo = (w - w_hi.astype(F32)).astype(BF16)
    p_hi = jnp.dot(lhs, w_hi, preferred_element_type=F32)
    p_lo = jnp.dot(lhs, w_lo, preferred_element_type=F32)
    res = (p_hi[:n] + p_lo[:n]) + (p_hi[n:] + p_lo[n:]) + b_ref[...]
    n_p = op_ref.shape[0]
    op_ref[...] = res[:n_p]
    os_ref[...] = res[n_p:n_p + os_ref.shape[0]]


def _mod_call(c_all, w_ada, b_ada, n_p, n_s):
    n = c_all.shape[0]
    tn = 1024
    return pl.pallas_call(
        _mod_kernel,
        grid=(6 * D // tn,),
        in_specs=[pl.BlockSpec((n, D), lambda j: (0, 0)),
                  pl.BlockSpec((D, tn), lambda j: (0, j)),
                  pl.BlockSpec((1, tn), lambda j: (0, j))],
        out_specs=(pl.BlockSpec((n_p, tn), lambda j: (0, j)),
                   pl.BlockSpec((n_s, tn), lambda j: (0, j))),
        out_shape=(jax.ShapeDtypeStruct((n_p, 6 * D), F32),
                   jax.ShapeDtypeStruct((n_s, 6 * D), F32)),
        compiler_params=pltpu.CompilerParams(dimension_semantics=("arbitrary",),
                                             vmem_limit_bytes=VMEM_LIMIT),
        name="mod",
    )(c_all, w_ada, b_ada.reshape(1, 6 * D))


def _ssm_prep_kernel(a_re, a_im, log_dt, b_re, b_im, c_im,
                     ab_re_o, ab_im_o, bb_re_o, bb_im_o, cneg_o):
    lam_re = jnp.minimum(a_re[...], -1e-4)
    lam_im = a_im[...]
    dt = jnp.exp(log_dt[...])
    mag = jnp.exp(lam_re * dt)
    ab_re = mag * jnp.cos(lam_im * dt)
    ab_im = mag * jnp.sin(lam_im * dt)
    den = lam_re * lam_re + lam_im * lam_im
    num_re = ab_re - 1.0
    coef_re = (num_re * lam_re + ab_im * lam_im) / den
    coef_im = (ab_im * lam_re - num_re * lam_im) / den
    ab_re_o[...] = ab_re
    ab_im_o[...] = ab_im
    br = b_re[...]
    bi = b_im[...]
    bb_re_o[...] = coef_re * br - coef_im * bi
    bb_im_o[...] = coef_re * bi + coef_im * br
    cneg_o[...] = -c_im[...]


def _ssm_prep_call(a_re, a_im, log_dt, b_re, b_im, c_im):
    flat = lambda v: v.reshape(1, NSTATE)
    b_hs = lambda v: jnp.transpose(v, (2, 0, 1)).reshape(H, NSTATE)
    dt_row = jnp.broadcast_to(log_dt[:, None], (G, P)).reshape(1, NSTATE)
    ab_re, ab_im, bb_re, bb_im, cneg = pl.pallas_call(
        _ssm_prep_kernel,
        out_shape=(jax.ShapeDtypeStruct((1, NSTATE), F32), jax.ShapeDtypeStruct((1, NSTATE), F32),
                   jax.ShapeDtypeStruct((H, NSTATE), F32), jax.ShapeDtypeStruct((H, NSTATE), F32),
                   jax.ShapeDtypeStruct((G * H, P), F32)),
        name="ssm_prep",
    )(flat(a_re), flat(a_im), dt_row, b_hs(b_re), b_hs(b_im), c_im.reshape(G * H, P))
    ghp = lambda v: jnp.transpose(v.reshape(H, G, P), (1, 0, 2))
    return (ab_re.reshape(G, P), ab_im.reshape(G, P), ghp(bb_re), ghp(bb_im), cneg.reshape(G, H, P))


def _block_diag_weights(bb_re, bb_im, c_re, c_im_neg):
    eye8 = jnp.eye(8, dtype=F32)
    eye4 = jnp.eye(4, dtype=F32)
    eye2 = jnp.eye(2, dtype=F32)

    def wb_part(bb):
        x = bb.reshape(4, 8, H, P)
        return jnp.einsum('qghp,gk->qghkp', x, eye8).reshape(4, 8 * H, 8 * P)

    wb = jnp.concatenate([wb_part(bb_re), wb_part(bb_im)], axis=-1).astype(BF16)

    def wc_part(c):
        x = c.reshape(4, 4, 2, H, P)
        y = jnp.einsum('qjghp,jk,gl->qjgpklh', x, eye4, eye2)
        return y.reshape(NCHUNK, 2 * P, 4 * 2 * H)

    wc = jnp.concatenate([wc_part(c_re), wc_part(c_im_neg)], axis=1).astype(BF16)
    return wb, wc


def _front(x, sc1, sh1, g_mix, w_in_ref):
    n = _rms(x) * g_mix * (1.0 + sc1) + sh1
    proj = _bdot(n, w_in_ref[...])
    u = proj[:, :SSM_W]
    glu = proj[:, SSM_W:SSM_W + CONV_CH] * _sigmoid(proj[:, SSM_W + CONV_CH:])
    return u, glu


def _ssm_out(y_lin, u, d_skip, w_glu_ref, b_glu, g_out_ssm):
    y = _gelu_tanh(y_lin + d_skip * u)
    ys = y * _sigmoid(_bdot(y, w_glu_ref[...]) + b_glu)
    return _rms(ys) * g_out_ssm


def _conv_out(conv, b_dw, ln_g, ln_b, g_out_conv):
    c = conv + b_dw
    mu = jnp.mean(c, axis=-1, keepdims=True)
    cc = c - mu
    var = jnp.mean(cc * cc, axis=-1, keepdims=True)
    ln = cc * lax.rsqrt(var + EPS) * ln_g + ln_b
    yc = ln * _sigmoid(ln)
    return _rms(yc) * g_out_conv


def _route(n2, w_r_ref, b_r):
    rows = n2.shape[0]
    n_hi = n2.astype(BF16)
    n_lo = (n2 - n_hi.astype(F32)).astype(BF16)
    parts = jnp.dot(jnp.concatenate([n_hi, n_lo], axis=0), w_r_ref[...], preferred_element_type=F32)
    lg = (parts[:rows, :LANES] + parts[:rows, LANES:]) + (parts[rows:, :LANES] + parts[rows:, LANES:]) + b_r
    lane = lax.broadcasted_iota(jnp.int32, (rows, LANES), 1).astype(F32)
    ninf = -jnp.inf
    big = 1e9
    gmask = jnp.logical_and(lane >= NE, lane < NE + NG)
    gl = jnp.where(gmask, lg, ninf)
    gmax = jnp.max(gl, axis=-1, keepdims=True)
    gsum = jnp.sum(jnp.where(gmask, jnp.exp(gl - gmax), 0.0), axis=-1, keepdims=True)
    p_top = 1.0 / gsum
    gi = jnp.min(jnp.where(gl == gmax, lane, big), axis=-1, keepdims=True) - NE
    lo = gi * EPG
    emask = jnp.logical_and(lane >= lo, lane < lo + EPG)
    el = jnp.where(emask, lg, ninf)
    m1 = jnp.max(el, axis=-1, keepdims=True)
    i1 = jnp.min(jnp.where(el == m1, lane, big), axis=-1, keepdims=True)
    el2 = jnp.where(lane == i1, ninf, el)
    m2 = jnp.max(el2, axis=-1, keepdims=True)
    i2 = jnp.min(jnp.where(el2 == m2, lane, big), axis=-1, keepdims=True)
    e2 = jnp.exp(m2 - m1)
    den = 1.0 + e2
    w1 = p_top / den
    w2 = p_top * e2 / den
    cnt = jnp.sum(jnp.where(lane == i1, 1.0, 0.0) + jnp.where(lane == i2, 1.0, 0.0),
                  axis=0, keepdims=True)
    cols = (jnp.where(lane == 0.0, i1, 0.0) + jnp.where(lane == 1.0, i2, 0.0)
            + jnp.where(lane == 2.0, w1, 0.0) + jnp.where(lane == 3.0, w2, 0.0))
    return cols.T[0:8, :], cnt


def _tail(x, ns, nc, gt1, sc2, sh2, w_out_ref, g_ffn, w_r_ref, b_r):
    merged = _bdot(jnp.concatenate([ns, nc], axis=-1), w_out_ref[...])
    x1 = x + gt1 * merged
    n2 = _rms(x1) * g_ffn * (1.0 + sc2) + sh2
    rt, cnt = _route(n2, w_r_ref, b_r)
    return x1, n2, rt, cnt


def _prompt_mixer_kernel(x_ref, mod_ref, g_mix_ref, w_in_ref, wb_ref, a_ref, wc_ref, dsk_ref,
                         w_glu_ref, b_glu_ref, w_dw_ref, b_dw_ref, ln_g_ref, ln_b_ref,
                         g_os_ref, g_oc_ref, w_out_ref, g_ffn_ref, w_r_ref, b_r_ref,
                         x1_ref, n2_ref, rt_ref, cnt_ref, hr_ref, hi_ref, cache_ref,
                         sre, sim, hst, ebuf, eshift, ncbuf, *, tc, pt):
    c = pl.program_id(1)
    nc_chunks = pl.num_programs(1)

    @pl.when(c == 0)
    def _():
        hst[...] = jnp.zeros_like(hst)
        ebuf[pl.ds(0, 32), :] = jnp.zeros((32, CONV_CH), F32)

    x = x_ref[...]
    mod = mod_ref[...]
    sh1, sc1, gt1 = mod[0:1], mod[1:2], mod[2:3]
    sh2, sc2 = mod[3:4], mod[4:5]

    u, glu = _front(x, sc1, sh1, g_mix_ref[...], w_in_ref)
    ebuf[pl.ds(32, tc), :] = glu

    for s in range(1, 8):
        eshift[s - 1, pl.ds(0, tc + 24), :] = ebuf[pl.ds(s, tc + 24), :]
    rb = 64
    convs = []
    for r0 in range(0, tc, rb):
        acc = None
        for k in range(KW):
            a8, s = (k + 2) // 8 * 8, (k + 2) % 8
            win = ebuf[pl.ds(r0 + a8, rb), :] if s == 0 else eshift[s - 1, pl.ds(r0 + a8, rb), :]
            term = w_dw_ref[k:k + 1, :] * win
            acc = term if acc is None else acc + term
        convs.append(acc)
    conv = jnp.concatenate(convs, axis=0)
    ncbuf[...] = _conv_out(conv, b_dw_ref[...], ln_g_ref[...], ln_b_ref[...], g_oc_ref[...])

    @pl.when(c == nc_chunks - 1)
    def _():
        cache_ref[...] = ebuf[pl.ds(tc + 2, CB), :]

    ebuf[pl.ds(0, 32), :] = ebuf[pl.ds(tc, 32), :]

    ub = u.astype(BF16)
    for q in range(4):
        r = jnp.dot(ub[:, q * LANES:(q + 1) * LANES], wb_ref[q], preferred_element_type=F32)
        for k in range(4):
            j = 4 * q + k
            sre[pl.ds(j * pt, tc), :] = r[:, k * LANES:(k + 1) * LANES]
            sim[pl.ds(j * pt, tc), :] = r[:, SSM_W + k * LANES:SSM_W + (k + 1) * LANES]

    ar0, ar1 = a_ref[0, 0:8, :], a_ref[0, 8:16, :]
    ai0, ai1 = a_ref[1, 0:8, :], a_ref[1, 8:16, :]

    def step(g, carry):
        hr0, hr1, hi0, hi1 = carry
        t0 = g * SCAN_GROUP
        idx = [(pl.ds(t0 + u, 8, stride=pt), pl.ds(t0 + u + 8 * pt, 8, stride=pt))
               for u in range(SCAN_GROUP)]
        bu = [(sre[i0, :], sre[i1, :], sim[i0, :], sim[i1, :]) for (i0, i1) in idx]
        hs = []
        for br0, br1, bi0, bi1 in bu:
            nr0 = ar0 * hr0 - ai0 * hi0 + br0
            ni0 = ar0 * hi0 + ai0 * hr0 + bi0
            nr1 = ar1 * hr1 - ai1 * hi1 + br1
            ni1 = ar1 * hi1 + ai1 * hr1 + bi1
            hr0, hr1, hi0, hi1 = nr0, nr1, ni0, ni1
            hs.append((nr0, nr1, ni0, ni1))
        for (i0, i1), (nr0, nr1, ni0, ni1) in zip(idx, hs):
            sre[i0, :] = nr0
            sim[i0, :] = ni0
            sre[i1, :] = nr1
            sim[i1, :] = ni1
        return hr0, hr1, hi0, hi1

    init = (hst[0, 0:8, :], hst[0, 8:16, :], hst[1, 0:8, :], hst[1, 8:16, :])
    hr0, hr1, hi0, hi1 = lax.fori_loop(0, tc // SCAN_GROUP, step, init)
    hst[0, 0:8, :] = hr0
    hst[0, 8:16, :] = hr1
    hst[1, 0:8, :] = hi0
    hst[1, 8:16, :] = hi1

    @pl.when(c == nc_chunks - 1)
    def _():
        hr_ref[...] = hst[0]
        hi_ref[...] = hst[1]

    ys = []
    for q in range(4):
        acc = None
        for jj in range(4):
            j = 4 * q + jj
            lhs = jnp.concatenate([sre[pl.ds(j * pt, tc), :], sim[pl.ds(j * pt, tc), :]], axis=-1)
            d = jnp.dot(lhs.astype(BF16), wc_ref[j], preferred_element_type=F32)
            acc = d if acc is None else acc + d
        ys.append(acc)
    y_lin = jnp.concatenate(ys, axis=-1)
    ns = _ssm_out(y_lin, u, dsk_ref[...], w_glu_ref, b_glu_ref[...], g_os_ref[...])
    nc = ncbuf[...]

    x1, n2, rt, cnt = _tail(x, ns, nc, gt1, sc2, sh2, w_out_ref, g_ffn_ref[...], w_r_ref, b_r_ref[...])
    x1_ref[...] = x1
    n2_ref[...] = n2.astype(BF16)
    rt_ref[...] = rt
    cnt_ref[...] = cnt


def _const_spec(shape):
    nd = len(shape)
    return pl.BlockSpec(shape, lambda b, c: (0,) * nd)


def _prompt_mixer_call(x, mod6, wts, tc):
    bsz, t, _ = x.shape
    n_all = bsz * t
    pt = tc + SUBLANES
    assert (pt // SUBLANES) % 2 == 1
    nc = t // tc
    kern = functools.partial(_prompt_mixer_kernel, tc=tc, pt=pt)
    in_specs = [pl.BlockSpec((None, tc, D), lambda b, c: (b, c, 0)),
                pl.BlockSpec((None, 6, D), lambda b, c: (b, 0, 0))]
    in_specs += [_const_spec(w.shape) for w in wts]
    out_shape = (jax.ShapeDtypeStruct((n_all, D), F32),
                 jax.ShapeDtypeStruct((n_all, D), BF16),
                 jax.ShapeDtypeStruct((8, n_all), F32),
                 jax.ShapeDtypeStruct((n_all // tc, 1, LANES), F32),
                 jax.ShapeDtypeStruct((bsz, NCHUNK, LANES), F32),
                 jax.ShapeDtypeStruct((bsz, NCHUNK, LANES), F32),
                 jax.ShapeDtypeStruct((bsz, CB, CONV_CH), F32))
    out_specs = (pl.BlockSpec((tc, D), lambda b, c: (b * nc + c, 0)),
                 pl.BlockSpec((tc, D), lambda b, c: (b * nc + c, 0)),
                 pl.BlockSpec((8, tc), lambda b, c: (0, b * nc + c)),
                 pl.BlockSpec((None, 1, LANES), lambda b, c: (b * nc + c, 0, 0)),
                 pl.BlockSpec((None, NCHUNK, LANES), lambda b, c: (b, 0, 0)),
                 pl.BlockSpec((None, NCHUNK, LANES), lambda b, c: (b, 0, 0)),
                 pl.BlockSpec((None, CB, CONV_CH), lambda b, c: (b, 0, 0)))
    scratch = [pltpu.VMEM((NCHUNK * pt, LANES), F32),
               pltpu.VMEM((NCHUNK * pt, LANES), F32),
               pltpu.VMEM((2, NCHUNK, LANES), F32),
               pltpu.VMEM((tc + 32, CONV_CH), F32),
               pltpu.VMEM((7, tc + 32, CONV_CH), F32),
               pltpu.VMEM((tc, CONV_CH), F32)]
    return pl.pallas_call(
        kern, grid=(bsz, nc), in_specs=in_specs, out_specs=out_specs, out_shape=out_shape,
        scratch_shapes=scratch,
        compiler_params=pltpu.CompilerParams(dimension_semantics=("arbitrary", "arbitrary"),
                                             vmem_limit_bytes=VMEM_LIMIT),
        name="prompt_mixer",
    )(x, mod6, *wts)


def _sample_mixer_kernel(x_ref, mod_ref, h0r_ref, h0i_ref, cache_ref,
                         g_mix_ref, w_in_ref, wb_ref, a_ref, wc_ref, dsk_ref,
                         w_glu_ref, b_glu_ref, w_dw_ref, b_dw_ref, ln_g_ref, ln_b_ref,
                         g_os_ref, g_oc_ref, w_out_ref, g_ffn_ref, w_r_ref, b_r_ref,
                         x1_ref, n2_ref, rt_ref, cnt_ref, hr_ref, hi_ref, glu_ref,
                         sre, sim, *, nb, nt):
    x = x_ref[...]

    def rows(i):
        m = mod_ref[:, i * D:(i + 1) * D]
        return jnp.concatenate([m] * nt, axis=0)

    sh1, sc1, gt1, sh2, sc2 = rows(0), rows(1), rows(2), rows(3), rows(4)
    u, glu = _front(x, sc1, sh1, g_mix_ref[...], w_in_ref)
    glu_ref[...] = glu

    ub = u.astype(BF16)
    for q in range(4):
        r = jnp.dot(ub[:, q * LANES:(q + 1) * LANES], wb_ref[q], preferred_element_type=F32)
        sre[:, q * SSM_W:(q + 1) * SSM_W] = r[:, :SSM_W]
        sim[:, q * SSM_W:(q + 1) * SSM_W] = r[:, SSM_W:]

    ar = a_ref[0:1, :]
    ai = a_ref[1:2, :]
    hr = h0r_ref[...]
    hi = h0i_ref[...]
    for t in range(nt):
        rs = pl.ds(t * nb, nb)
        nr = ar * hr - ai * hi + sre[rs, :]
        ni = ar * hi + ai * hr + sim[rs, :]
        sre[rs, :] = nr
        sim[rs, :] = ni
        hr, hi = nr, ni
    hr_ref[...] = hr
    hi_ref[...] = hi

    ys = []
    for q in range(4):
        acc = None
        for jj in range(4):
            j = 4 * q + jj
            lhs = jnp.concatenate([sre[:, j * LANES:(j + 1) * LANES],
                                   sim[:, j * LANES:(j + 1) * LANES]], axis=-1)
            d = jnp.dot(lhs.astype(BF16), wc_ref[j], preferred_element_type=F32)
            acc = d if acc is None else acc + d
        ys.append(acc)
    y_lin = jnp.concatenate(ys, axis=-1)
    ns = _ssm_out(y_lin, u, dsk_ref[...], w_glu_ref, b_glu_ref[...], g_os_ref[...])

    def ext(jrow):
        if jrow < CB:
            return cache_ref[jrow]
        return glu[(jrow - CB) * nb:(jrow - CB + 1) * nb, :]

    convs = []
    for t in range(nt):
        acc = None
        for k in range(KW):
            term = w_dw_ref[k:k + 1, :] * ext(t + k)
            acc = term if acc is None else acc + term
        convs.append(acc)
    conv = jnp.concatenate(convs, axis=0)
    nc = _conv_out(conv, b_dw_ref[...], ln_g_ref[...], ln_b_ref[...], g_oc_ref[...])

    x1, n2, rt, cnt = _tail(x, ns, nc, gt1, sc2, sh2, w_out_ref, g_ffn_ref[...], w_r_ref, b_r_ref[...])
    x1_ref[...] = x1
    n2_ref[...] = n2.astype(BF16)
    rt_ref[...] = rt
    cnt_ref[...] = cnt


def _sample_mixer_call(x_tm, mod_s, h0r, h0i, cache_tm, wts, nb, nt):
    n = nb * nt
    kern = functools.partial(_sample_mixer_kernel, nb=nb, nt=nt)
    out_shape = (jax.ShapeDtypeStruct((n, D), F32),
                 jax.ShapeDtypeStruct((n, D), BF16),
                 jax.ShapeDtypeStruct((8, n), F32),
                 jax.ShapeDtypeStruct((1, LANES), F32),
                 jax.ShapeDtypeStruct((nb, NSTATE), F32),
                 jax.ShapeDtypeStruct((nb, NSTATE), F32),
                 jax.ShapeDtypeStruct((n, CONV_CH), F32))
    scratch = [pltpu.VMEM((n, NSTATE), F32), pltpu.VMEM((n, NSTATE), F32)]
    return pl.pallas_call(
        kern, out_shape=out_shape, scratch_shapes=scratch,
        compiler_params=pltpu.CompilerParams(vmem_limit_bytes=VMEM_LIMIT),
        name="sample_mixer",
    )(x_tm, mod_s, h0r, h0i, cache_tm, *wts)


ROW_ALIGN = 16
MOE_TD = 512
MOE_BR = MOE_TD * 2 + NE * ROW_ALIGN
MOE_TG = 256


def _slot_positions(rt, base, before):
    t = rt.shape[1]
    e0 = rt[0:1, :]
    e1 = rt[1:2, :]
    sub = lax.broadcasted_iota(jnp.int32, (LANES, t), 0).astype(F32)
    a0 = jnp.where(sub == e0, 1.0, 0.0)
    a1 = jnp.where(sub == e1, 1.0, 0.0)
    at = a0 + a1
    rank = jnp.dot(at.astype(BF16), before, preferred_element_type=F32)
    slot = rank + base
    pos0 = jnp.sum(a0 * slot, axis=0, keepdims=True)
    pos1 = jnp.sum(a1 * slot, axis=0, keepdims=True)
    return pos0, pos1


def _segment_copies(tab_ref, t, n_tiles, buf, hbm, sems, slot, to_hbm, wait):
    for e in range(NE):
        n = pl.multiple_of(tab_ref[t * NE + e], ROW_ALIGN)
        b = pl.multiple_of(tab_ref[(n_tiles + t) * NE + e], ROW_ALIGN)
        d = pl.multiple_of(tab_ref[(2 * n_tiles + t) * NE + e], ROW_ALIGN)
        vm = buf.at[slot, pl.ds(b, n)]
        hb = hbm.at[pl.ds(d, n)]
        cp = pltpu.make_async_copy(vm, hb, sems.at[slot, e]) if to_hbm else \
            pltpu.make_async_copy(hb, vm, sems.at[slot, e])

        @pl.when(n > 0)
        def _():
            if wait:
                cp.wait()
            else:
                cp.start()


def _tile_rows(tab_ref, t, n_tiles):
    return tab_ref[3 * n_tiles * NE + 2 * NE + 1 + t]


def _one_hot_rows(r0, nrows, pos0, pos1):
    row = (lax.broadcasted_iota(jnp.int32, (nrows, MOE_TD), 0) + r0).astype(F32)
    return row == pos0, row == pos1


MOE_BLK = 256


def _dispatch_kernel(tab_ref, rtp_ref, n2p_ref, rts_ref, n2s_ref, base_ref, xs_ref, pos_ref, buf, zbuf, before,
                     sems, zsem,
                     *, n_tiles, n_ptiles, n_gtiles):
    t = pl.program_id(0)
    slot = lax.rem(t, 2)
    first_free = tab_ref[3 * n_tiles * NE + 2 * NE]

    def fill_copy(j):
        d = pl.multiple_of(j * MOE_TG, MOE_TG)
        return pltpu.make_async_copy(zbuf, xs_ref.at[pl.ds(d, MOE_TG)], zsem)

    def fill_start(j, carry):
        fill_copy(j).start()
        return carry

    def fill_wait(j, carry):
        fill_copy(j).wait()
        return carry

    @pl.when(t == 0)
    def _():
        zbuf[...] = jnp.zeros_like(zbuf)
        r = lax.broadcasted_iota(jnp.int32, (MOE_TD, MOE_TD), 0)
        c = lax.broadcasted_iota(jnp.int32, (MOE_TD, MOE_TD), 1)
        before[...] = jnp.where(r < c, 1.0, 0.0).astype(BF16)
        for phase in range(2):
            for e in range(NE):
                d = pl.multiple_of(tab_ref[3 * n_tiles * NE + e], ROW_ALIGN)
                n = pl.multiple_of(tab_ref[3 * n_tiles * NE + NE + e], ROW_ALIGN)
                cp = pltpu.make_async_copy(zbuf.at[pl.ds(0, n)], xs_ref.at[pl.ds(d, n)], sems.at[1, e])

                @pl.when(n > 0)
                def _():
                    if phase == 0:
                        cp.start()
                    else:
                        cp.wait()

        lax.fori_loop(first_free, n_gtiles, fill_start, 0)

    @pl.when(t >= 2)
    def _():
        _segment_copies(tab_ref, t - 2, n_tiles, buf, xs_ref, sems, slot, to_hbm=True, wait=True)

    is_sample = t >= n_ptiles
    rt = jnp.where(is_sample, rts_ref[...], rtp_ref[...])
    n2 = jnp.where(is_sample, n2s_ref[...], n2p_ref[...])
    pos0, pos1 = _slot_positions(rt, base_ref[...], before[...])
    pos_ref[...] = jnp.concatenate([pos0, pos1, jnp.zeros((6, MOE_TD), F32)], axis=0)
    used = _tile_rows(tab_ref, t, n_tiles)

    def group(r0, nrows):
        m0, m1 = _one_hot_rows(r0, nrows, pos0, pos1)
        q = (jnp.where(m0, 1.0, 0.0) + jnp.where(m1, 1.0, 0.0)).astype(BF16)
        buf[slot, pl.ds(r0, nrows), :] = jnp.dot(q, n2, preferred_element_type=F32).astype(BF16)

    group(0, 2 * MOE_TD)
    for r0 in range(2 * MOE_TD, MOE_BR, MOE_BLK):
        @pl.when(used > r0)
        def _():
            group(r0, MOE_BLK)

    _segment_copies(tab_ref, t, n_tiles, buf, xs_ref, sems, slot, to_hbm=True, wait=False)

    @pl.when(t == n_tiles - 1)
    def _():
        if n_tiles >= 2:
            _segment_copies(tab_ref, t - 1, n_tiles, buf, xs_ref, sems, 1 - slot, to_hbm=True, wait=True)
        _segment_copies(tab_ref, t, n_tiles, buf, xs_ref, sems, slot, to_hbm=True, wait=True)
        lax.fori_loop(first_free, n_gtiles, fill_wait, 0)


def _dispatch_call(tab, base_col, rt_p, n2_p, rt_s, n2_s, r_max):
    n_ptiles = n2_p.shape[0] // MOE_TD
    n_tiles = n_ptiles + n2_s.shape[0] // MOE_TD
    last_p = n_ptiles - 1
    return pl.pallas_call(
        functools.partial(_dispatch_kernel, n_tiles=n_tiles, n_ptiles=n_ptiles,
                          n_gtiles=r_max // MOE_TG),
        grid_spec=pltpu.PrefetchScalarGridSpec(
            num_scalar_prefetch=1, grid=(n_tiles,),
            in_specs=[pl.BlockSpec((8, MOE_TD), lambda t, tab: (0, jnp.minimum(t, last_p))),
                      pl.BlockSpec((MOE_TD, D), lambda t, tab: (jnp.minimum(t, last_p), 0)),
                      pl.BlockSpec((8, MOE_TD), lambda t, tab: (0, 0)),
                      pl.BlockSpec((MOE_TD, D), lambda t, tab: (0, 0)),
                      pl.BlockSpec((None, LANES, 1), lambda t, tab: (t, 0, 0))],
            out_specs=(pl.BlockSpec(memory_space=pl.ANY),
                       pl.BlockSpec((None, 8, MOE_TD), lambda t, tab: (t, 0, 0))),
            scratch_shapes=[pltpu.VMEM((2, MOE_BR, D), BF16),
                            pltpu.VMEM((MOE_TG, D), BF16),
                            pltpu.VMEM((MOE_TD, MOE_TD), BF16),
                            pltpu.SemaphoreType.DMA((2, NE)),
                            pltpu.SemaphoreType.DMA(())]),
        out_shape=(jax.ShapeDtypeStruct((r_max, D), BF16),
                   jax.ShapeDtypeStruct((n_tiles, 8, MOE_TD), F32)),
        compiler_params=pltpu.CompilerParams(dimension_semantics=("arbitrary",),
                                             vmem_limit_bytes=VMEM_LIMIT),
        name="moe_dispatch",
    )(tab, rt_p, n2_p, rt_s, n2_s, base_col)


MOE_CK = 6


def _experts_kernel(ctab_ref, xs_ref, w1_ref, w3_ref, w2_ref, ys_ref,
                    xbuf, ybuf, zbuf, in_sem, out_sem, zsem, *, n_gtiles, nc_max):
    e = pl.program_id(0)
    c0 = ctab_ref[e]
    c1 = ctab_ref[e + 1]
    n_chunks = ctab_ref[NE]
    first_free = ctab_ref[NE + 1 + 2 * nc_max]

    def fill_copy(j):
        d = pl.multiple_of(j * MOE_TG, MOE_TG)
        return pltpu.make_async_copy(zbuf, ys_ref.at[pl.ds(d, MOE_TG)], zsem)

    def fill_start(j, carry):
        fill_copy(j).start()
        return carry

    def fill_wait(j, carry):
        fill_copy(j).wait()
        return carry

    def span(c):
        r = pl.multiple_of(ctab_ref[NE + 1 + c], MOE_TG)
        n = pl.multiple_of(ctab_ref[NE + 1 + nc_max + c] * MOE_TG, MOE_TG)
        return r, n

    def in_copy(c, slot):
        r, n = span(c)
        return pltpu.make_async_copy(xs_ref.at[pl.ds(r, n)], xbuf.at[slot, pl.ds(0, n)], in_sem.at[slot])

    def out_copy(c, slot):
        r, n = span(c)
        return pltpu.make_async_copy(ybuf.at[slot, pl.ds(0, n)], ys_ref.at[pl.ds(r, n)], out_sem.at[slot])

    @pl.when(e == 0)
    def _():
        zbuf[...] = jnp.zeros_like(zbuf)
        lax.fori_loop(first_free, n_gtiles, fill_start, 0)

        @pl.when(n_chunks > 0)
        def _():
            in_copy(0, 0).start()

    def compute(slot, rows):
        x = xbuf[slot, pl.ds(0, rows), :]
        a = jnp.dot(x, w1_ref[...].astype(BF16), preferred_element_type=F32)
        b = jnp.dot(x, w3_ref[...].astype(BF16), preferred_element_type=F32)
        hid = a * _sigmoid(a) * b
        y = jnp.dot(hid.astype(BF16), w2_ref[...].astype(BF16), preferred_element_type=F32)
        ybuf[slot, pl.ds(0, rows), :] = y.astype(BF16)

    @pl.when(c1 > c0)
    def _():
        def chunk(c, carry):
            slot = lax.rem(c, 2)

            @pl.when(c + 1 < n_chunks)
            def _():
                in_copy(c + 1, 1 - slot).start()

            in_copy(c, slot).wait()

            @pl.when(c >= 2)
            def _():
                out_copy(c - 2, slot).wait()

            k = ctab_ref[NE + 1 + nc_max + c]
            for kk in range(1, MOE_CK + 1):
                @pl.when(k == kk)
                def _():
                    compute(slot, kk * MOE_TG)

            out_copy(c, slot).start()
            return carry

        lax.fori_loop(c0, c1, chunk, 0)

    @pl.when(e == NE - 1)
    def _():
        @pl.when(n_chunks >= 2)
        def _():
            out_copy(n_chunks - 2, lax.rem(n_chunks, 2)).wait()

        @pl.when(n_chunks >= 1)
        def _():
            out_copy(n_chunks - 1, lax.rem(n_chunks - 1, 2)).wait()

        lax.fori_loop(first_free, n_gtiles, fill_wait, 0)


def _experts_call(ctab, xs, w1, w3, w2, nc_max):
    r_max = xs.shape[0]
    w_map = lambda e, ctab: (e, 0, 0)
    ring = pltpu.VMEM((2, MOE_CK * MOE_TG, D), BF16)
    return pl.pallas_call(
        functools.partial(_experts_kernel, n_gtiles=r_max // MOE_TG, nc_max=nc_max),
        grid_spec=pltpu.PrefetchScalarGridSpec(
            num_scalar_prefetch=1, grid=(NE,),
            in_specs=[pl.BlockSpec(memory_space=pl.ANY),
                      pl.BlockSpec((None, D, DE), w_map),
                      pl.BlockSpec((None, D, DE), w_map),
                      pl.BlockSpec((None, DE, D), w_map)],
            out_specs=pl.BlockSpec(memory_space=pl.ANY),
            scratch_shapes=[ring, ring,
                            pltpu.VMEM((MOE_TG, D), BF16),
                            pltpu.SemaphoreType.DMA((2,)), pltpu.SemaphoreType.DMA((2,)),
                            pltpu.SemaphoreType.DMA(())]),
        out_shape=jax.ShapeDtypeStruct((r_max, D), BF16),
        compiler_params=pltpu.CompilerParams(dimension_semantics=("arbitrary",),
                                             vmem_limit_bytes=VMEM_LIMIT),
        name="moe_experts",
    )(ctab, xs, w1, w3, w2)


def _combine_kernel(tab_ref, rt_ref, pos_ref, x1_ref, gt2_ref, gf_ref, ys_ref, y_ref, buf, acc, sems,
                    *, n_tiles, t_off):
    i = pl.program_id(0)
    t = i + t_off
    slot = lax.rem(i, 2)

    @pl.when(i == 0)
    def _():
        buf[...] = jnp.zeros_like(buf)
        _segment_copies(tab_ref, t, n_tiles, buf, ys_ref, sems, slot, to_hbm=False, wait=False)

    @pl.when(i + 1 < pl.num_programs(0))
    def _():
        _segment_copies(tab_ref, t + 1, n_tiles, buf, ys_ref, sems, 1 - slot, to_hbm=False, wait=False)

    _segment_copies(tab_ref, t, n_tiles, buf, ys_ref, sems, slot, to_hbm=False, wait=True)

    rt = rt_ref[...]
    pos0 = pos_ref[0:1, :]
    pos1 = pos_ref[1:2, :]
    used = _tile_rows(tab_ref, t, n_tiles)

    def ungroup(r0, nrows):
        m0, m1 = _one_hot_rows(r0, nrows, pos0, pos1)
        q = (jnp.where(m0, 1.0, 0.0) + jnp.where(m1, 1.0, 0.0)).astype(BF16)
        gw = jnp.sum(jnp.where(m0, rt[2:3, :], 0.0) + jnp.where(m1, rt[3:4, :], 0.0),
                     axis=1, keepdims=True)
        yv = (buf[slot, pl.ds(r0, nrows), :].astype(F32) * gw).astype(BF16)
        return lax.dot_general(q, yv, (((0,), (0,)), ((), ())), preferred_element_type=F32)

    acc[...] = ungroup(0, 2 * MOE_TD)
    for r0 in range(2 * MOE_TD, MOE_BR, MOE_BLK):
        @pl.when(used > r0)
        def _():
            acc[...] += ungroup(r0, MOE_BLK)

    gt2 = gt2_ref[...]
    if gt2.shape[0] not in (1, MOE_TD):
        gt2 = jnp.concatenate([gt2] * (MOE_TD // gt2.shape[0]), axis=0)
    xo = x1_ref[...] + gt2 * acc[...]
    y_ref[...] = _rms(xo) * gf_ref[...]


def _combine_call(tab, rt, pos, x1, gt2, gt2_spec, g_final, ys, n_tiles, t_off):
    n_out_tiles = x1.shape[0] // MOE_TD
    return pl.pallas_call(
        functools.partial(_combine_kernel, n_tiles=n_tiles, t_off=t_off),
        grid_spec=pltpu.PrefetchScalarGridSpec(
            num_scalar_prefetch=1, grid=(n_out_tiles,),
            in_specs=[pl.BlockSpec((8, MOE_TD), lambda t, tab: (0, t)),
                      pl.BlockSpec((None, 8, MOE_TD), lambda t, tab: (t + t_off, 0, 0)),
                      pl.BlockSpec((MOE_TD, D), lambda t, tab: (t, 0)),
                      gt2_spec,
                      pl.BlockSpec((1, D), lambda t, tab: (0, 0)),
                      pl.BlockSpec(memory_space=pl.ANY)],
            out_specs=pl.BlockSpec((MOE_TD, D), lambda t, tab: (t, 0)),
            scratch_shapes=[pltpu.VMEM((2, MOE_BR, D), BF16),
                            pltpu.VMEM((MOE_TD, D), F32),
                            pltpu.SemaphoreType.DMA((2, NE))]),
        out_shape=jax.ShapeDtypeStruct((n_out_tiles * MOE_TD, D), F32),
        compiler_params=pltpu.CompilerParams(dimension_semantics=("arbitrary",),
                                             vmem_limit_bytes=VMEM_LIMIT),
        name="moe_combine",
    )(tab, rt, pos, x1, gt2, g_final.reshape(1, D), ys)


def _moe_plan(cnt, nc_max):
    cnt_al = (cnt + ROW_ALIGN - 1) // ROW_ALIGN * ROW_ALIGN
    seg_rows = cnt_al.sum(axis=0)
    seg_pad = (seg_rows + MOE_TG - 1) // MOE_TG * MOE_TG
    seg_start = jnp.cumsum(seg_pad) - seg_pad
    dst = seg_start[None, :] + jnp.cumsum(cnt_al, axis=0) - cnt_al
    boff = jnp.cumsum(cnt_al, axis=1) - cnt_al
    tile_end = jnp.cumsum(seg_pad // MOE_TG)
    n_active = tile_end[-1:].astype(jnp.int32)
    tab = jnp.concatenate([cnt_al.ravel(), boff.ravel(), dst.ravel(),
                           seg_start + seg_rows, seg_pad - seg_rows, n_active,
                           cnt_al.sum(axis=1)]).astype(jnp.int32)
    nt = seg_pad // MOE_TG
    nfull = nt // MOE_CK
    rem = nt % MOE_CK
    nch = nfull + (rem > 0)
    cend = jnp.cumsum(nch)
    cstart = cend - nch
    c = jnp.arange(nc_max, dtype=jnp.int32)
    ce = jnp.minimum(jnp.sum(c[:, None] >= cend[None, :], axis=1), NE - 1)
    local = c - cstart[ce]
    valid = c < cend[-1]
    ck = jnp.where(valid, jnp.where(local < nfull[ce], MOE_CK, rem[ce]), 0)
    crow = jnp.where(valid, seg_start[ce] + local * (MOE_CK * MOE_TG), 0)
    ctab = jnp.concatenate([cstart, cend[-1:], crow, ck, n_active]).astype(jnp.int32)
    base_col = jnp.pad(boff, ((0, 0), (0, LANES - NE))).astype(F32)[:, :, None]
    return tab, ctab, base_col


def kernel(x_prompt, x_sample, c_prompt, c_sample, state_ssm_re, state_ssm_im, cache_conv, w_ada, b_ada, g_norm_mix, w_in, ssm_a_re, ssm_a_im, ssm_log_dt, ssm_b_re, ssm_b_im, ssm_c_re, ssm_c_im, ssm_d, w_ssm_glu, b_ssm_glu, w_dw, b_dw, ln_conv_g, ln_conv_b, g_out_ssm, g_out_conv, w_out, g_norm_ffn, w_router_grp, b_router_grp, w_router_exp, b_router_exp, w_exp_gate, w_exp_up, w_exp_down, g_final):
    depth = w_ada.shape[0]
    assert depth == 1
    bsz, seq, _ = x_prompt.shape
    nb, nt, _ = x_sample.shape

    n_c = bsz + nb
    c_pad = -n_c % 16
    c_all = jnp.concatenate([c_prompt, c_sample, jnp.zeros((c_pad, D), F32)], axis=0)
    mod_p, mod_s = _mod_call(c_all, w_ada[0], b_ada[0], bsz, nb)
    mod_p = mod_p.reshape(bsz, 6, D)

    ab_re, ab_im, bb_re, bb_im, c_im_neg = _ssm_prep_call(
        ssm_a_re[0], ssm_a_im[0], ssm_log_dt[0], ssm_b_re[0], ssm_b_im[0], ssm_c_im[0])
    wb, wc = _block_diag_weights(bb_re, bb_im, ssm_c_re[0], c_im_neg)
    a_tok = jnp.stack([ab_re.reshape(NCHUNK, LANES), ab_im.reshape(NCHUNK, LANES)])
    a_row = jnp.stack([ab_re.reshape(NSTATE), ab_im.reshape(NSTATE)])

    w_r = jnp.concatenate([w_router_exp[0].reshape(D, NE), w_router_grp[0],
                           jnp.zeros((D, LANES - NE - NG), F32)], axis=1)
    w_r_hi = w_r.astype(BF16)
    w_r = jnp.concatenate([w_r_hi, (w_r - w_r_hi.astype(F32)).astype(BF16)], axis=1)
    b_r = jnp.concatenate([b_router_exp[0].reshape(NE), b_router_grp[0],
                           jnp.zeros((LANES - NE - NG,), F32)]).reshape(1, LANES)
    w_dw_p = jnp.concatenate([w_dw[0], jnp.zeros((1, CONV_CH), F32)], axis=0)

    row = lambda v: v.reshape(1, -1)
    common_a = (row(g_norm_mix[0]), w_in[0].astype(BF16), wb)
    common_b = (wc, row(ssm_d[0].reshape(SSM_W)), w_ssm_glu[0].astype(BF16), row(b_ssm_glu[0]),
                w_dw_p, row(b_dw[0]), row(ln_conv_g[0]), row(ln_conv_b[0]),
                row(g_out_ssm[0]), row(g_out_conv[0]), w_out[0].astype(BF16),
                row(g_norm_ffn[0]), w_r, b_r)

    n_p = bsz * seq
    n_s = nb * nt
    n_all = n_p + n_s
    assert n_s == MOE_TD and seq % MOE_TD == 0 and MOE_TD % PROMPT_TC == 0
    wts_p = common_a + (a_tok,) + common_b
    x1_p, n2_p, rt_p, cnt_p, hr_p, hi_p, cache_p = _prompt_mixer_call(x_prompt, mod_p, wts_p, PROMPT_TC)

    x_tm = jnp.transpose(x_sample, (1, 0, 2)).reshape(nt * nb, D)
    cache_tm = jnp.transpose(cache_conv[0], (1, 0, 2))
    wts_s = common_a + (a_row,) + common_b
    x1_s, n2_s, rt_s, cnt_s, hr_s, hi_s, glu_s = _sample_mixer_call(
        x_tm, mod_s, state_ssm_re[0].reshape(nb, NSTATE), state_ssm_im[0].reshape(nb, NSTATE),
        cache_tm, wts_s, nb, nt)

    n_ptiles = n_p // MOE_TD
    n_tiles = n_all // MOE_TD
    r_max = -(-(2 * n_all + n_tiles * NE * (ROW_ALIGN - 1) + NE * (MOE_TG - ROW_ALIGN)) // MOE_TG) * MOE_TG
    cnt = jnp.concatenate([cnt_p.reshape(n_ptiles, MOE_TD // PROMPT_TC, LANES).sum(axis=1), cnt_s])
    nc_max = r_max // MOE_TG // MOE_CK + NE
    tab, ctab, base_col = _moe_plan(cnt[:, :NE].astype(jnp.int32), nc_max)
    xs, pos = _dispatch_call(tab, base_col, rt_p, n2_p, rt_s, n2_s, r_max)
    ys = _experts_call(ctab, xs, w_exp_gate[0], w_exp_up[0], w_exp_down[0], nc_max)
    tiles_per_b = seq // MOE_TD
    gt2_p = mod_p[:, 5:6, :]
    y_p = _combine_call(tab, rt_p, pos, x1_p, gt2_p,
                        pl.BlockSpec((None, 1, D), lambda t, tab: (t // tiles_per_b, 0, 0)),
                        g_final, ys, n_tiles, 0)
    y_s = _combine_call(tab, rt_s, pos, x1_s, mod_s,
                        pl.BlockSpec((nb, D), lambda t, tab: (0, 5)),
                        g_final, ys, n_tiles, n_ptiles)

    y_prompt = y_p.reshape(bsz, seq, D)
    y_sample = jnp.transpose(y_s.reshape(nt, nb, D), (1, 0, 2))
    new_cache_s = jnp.concatenate(
        [cache_conv[0][:, nt:, :], jnp.transpose(glu_s.reshape(nt, nb, CONV_CH), (1, 0, 2))], axis=1)
    return (y_prompt, y_sample,
            hr_p.reshape(1, bsz, G, P), hi_p.reshape(1, bsz, G, P), cache_p[None],
            hr_s.reshape(1, nb, G, P), hi_s.reshape(1, nb, G, P), new_cache_s[None])
```

```python
import functools

import jax
import jax.numpy as jnp
import numpy as np
from jax import lax
from jax.experimental import pallas as pl
from jax.experimental.pallas import tpu as pltpu

F32 = jnp.float32
BF16 = jnp.bfloat16

D = 1024
SSM_W = 512
CONV_CH = 512
G = 32
H = 16
P = 64
KW = 31
CB = KW - 1
NE = 32
NG = 4
EPG = 8
DE = 512
EPS = 1e-6
LANES = 128
SUBLANES = 8
NSTATE = G * P
NCHUNK = NSTATE // LANES

PROMPT_TC = 512
SCAN_GROUP = 8
V7X_VMEM_BYTES = 64 * 1024 * 1024
VMEM_LIMIT = V7X_VMEM_BYTES - 8 * 1024 * 1024


def _rms(x):
    return x * lax.rsqrt(jnp.mean(x * x, axis=-1, keepdims=True) + EPS)


def _sigmoid(x):
    return 0.5 * jnp.tanh(0.5 * x) + 0.5


def _gelu_tanh(y):
    c = np.sqrt(2.0 / np.pi).astype(np.float32)
    return y * (0.5 * (1.0 + jnp.tanh(c * (y + 0.044715 * (y * y * y)))))


def _bdot(a, b):
    return jnp.dot(a.astype(BF16), b, preferred_element_type=F32)


def _mod_kernel(c_ref, w_ref, b_ref, op_ref, os_ref):
    c = c_ref[...]
    s = c * _sigmoid(c)
    n = s.shape[0]
    s_hi = s.astype(BF16)
    lhs = jnp.concatenate([s_hi, (s - s_hi.astype(F32)).astype(BF16)], axis=0)
    w = w_ref[...]
    w_hi = w.astype(BF16)
    w_lo = (w - w_hi.astype(F32)).astype(BF16)
    p_hi = jnp.dot(lhs, w_hi, preferred_element_type=F32)
    p_lo = jnp.dot(lhs, w_lo, preferred_element_type=F32)
    res = (p_hi[:n] + p_lo[:n]) + (p_hi[n:] + p_lo[n:]) + b_ref[...]
    n_p = op_ref.shape[0]
    op_ref[...] = res[:n_p]
    os_ref[...] = res[n_p:n_p + os_ref.shape[0]]


def _mod_call(c_all, w_ada, b_ada, n_p, n_s):
    n = c_all.shape[0]
    tn = 1024
    return pl.pallas_call(
        _mod_kernel,
        grid=(6 * D // tn,),
        in_specs=[pl.BlockSpec((n, D), lambda j: (0, 0)),
                  pl.BlockSpec((D, tn), lambda j: (0, j)),
                  pl.BlockSpec((1, tn), lambda j: (0, j))],
        out_specs=(pl.BlockSpec((n_p, tn), lambda j: (0, j)),
                   pl.BlockSpec((n_s, tn), lambda j: (0, j))),
        out_shape=(jax.ShapeDtypeStruct((n_p, 6 * D), F32),
                   jax.ShapeDtypeStruct((n_s, 6 * D), F32)),
        compiler_params=pltpu.CompilerParams(dimension_semantics=("arbitrary",),
                                             vmem_limit_bytes=VMEM_LIMIT),
        name="mod",
    )(c_all, w_ada, b_ada.reshape(1, 6 * D))


def _ssm_prep_kernel(a_re, a_im, log_dt, b_re, b_im, c_im,
                     ab_re_o, ab_im_o, bb_re_o, bb_im_o, cneg_o):
    lam_re = jnp.minimum(a_re[...], -1e-4)
    lam_im = a_im[...]
    dt = jnp.exp(log_dt[...])
    mag = jnp.exp(lam_re * dt)
    ab_re = mag * jnp.cos(lam_im * dt)
    ab_im = mag * jnp.sin(lam_im * dt)
    den = lam_re * lam_re + lam_im * lam_im
    num_re = ab_re - 1.0
    coef_re = (num_re * lam_re + ab_im * lam_im) / den
    coef_im = (ab_im * lam_re - num_re * lam_im) / den
    ab_re_o[...] = ab_re
    ab_im_o[...] = ab_im
    br = b_re[...]
    bi = b_im[...]
    bb_re_o[...] = coef_re * br - coef_im * bi
    bb_im_o[...] = coef_re * bi + coef_im * br
    cneg_o[...] = -c_im[...]


def _ssm_prep_call(a_re, a_im, log_dt, b_re, b_im, c_im):
    flat = lambda v: v.reshape(1, NSTATE)
    b_hs = lambda v: jnp.transpose(v, (2, 0, 1)).reshape(H, NSTATE)
    dt_row = jnp.broadcast_to(log_dt[:, None], (G, P)).reshape(1, NSTATE)
    ab_re, ab_im, bb_re, bb_im, cneg = pl.pallas_call(
        _ssm_prep_kernel,
        out_shape=(jax.ShapeDtypeStruct((1, NSTATE), F32), jax.ShapeDtypeStruct((1, NSTATE), F32),
                   jax.ShapeDtypeStruct((H, NSTATE), F32), jax.ShapeDtypeStruct((H, NSTATE), F32),
                   jax.ShapeDtypeStruct((G * H, P), F32)),
        name="ssm_prep",
    )(flat(a_re), flat(a_im), dt_row, b_hs(b_re), b_hs(b_im), c_im.reshape(G * H, P))
    ghp = lambda v: jnp.transpose(v.reshape(H, G, P), (1, 0, 2))
    return (ab_re.reshape(G, P), ab_im.reshape(G, P), ghp(bb_re), ghp(bb_im), cneg.reshape(G, H, P))


def _block_diag_weights(bb_re, bb_im, c_re, c_im_neg):
    eye8 = jnp.eye(8, dtype=F32)
    eye4 = jnp.eye(4, dtype=F32)
    eye2 = jnp.eye(2, dtype=F32)

    def wb_part(bb):
        x = bb.reshape(4, 8, H, P)
        return jnp.einsum('qghp,gk->qghkp', x, eye8).reshape(4, 8 * H, 8 * P)

    wb = jnp.concatenate([wb_part(bb_re), wb_part(bb_im)], axis=-1).astype(BF16)

    def wc_part(c):
        x = c.reshape(4, 4, 2, H, P)
        y = jnp.einsum('qjghp,jk,gl->qjgpklh', x, eye4, eye2)
        return y.reshape(NCHUNK, 2 * P, 4 * 2 * H)

    wc = jnp.concatenate([wc_part(c_re), wc_part(c_im_neg)], axis=1).astype(BF16)
    return wb, wc


def _front(x, sc1, sh1, g_mix, w_in_ref):
    n = _rms(x) * g_mix * (1.0 + sc1) + sh1
    proj = _bdot(n, w_in_ref[...])
    u = proj[:, :SSM_W]
    glu = proj[:, SSM_W:SSM_W + CONV_CH] * _sigmoid(proj[:, SSM_W + CONV_CH:])
    return u, glu


def _ssm_out(y_lin, u, d_skip, w_glu_ref, b_glu, g_out_ssm):
    y = _gelu_tanh(y_lin + d_skip * u)
    ys = y * _sigmoid(_bdot(y, w_glu_ref[...]) + b_glu)
    return _rms(ys) * g_out_ssm


def _conv_out(conv, b_dw, ln_g, ln_b, g_out_conv):
    c = conv + b_dw
    mu = jnp.mean(c, axis=-1, keepdims=True)
    cc = c - mu
    var = jnp.mean(cc * cc, axis=-1, keepdims=True)
    ln = cc * lax.rsqrt(var + EPS) * ln_g + ln_b
    yc = ln * _sigmoid(ln)
    return _rms(yc) * g_out_conv


def _route(n2, w_r_ref, b_r):
    rows = n2.shape[0]
    n_hi = n2.astype(BF16)
    n_lo = (n2 - n_hi.astype(F32)).astype(BF16)
    parts = jnp.dot(jnp.concatenate([n_hi, n_lo], axis=0), w_r_ref[...], preferred_element_type=F32)
    lg = (parts[:rows, :LANES] + parts[:rows, LANES:]) + (parts[rows:, :LANES] + parts[rows:, LANES:]) + b_r
    lane = lax.broadcasted_iota(jnp.int32, (rows, LANES), 1).astype(F32)
    ninf = -jnp.inf
    big = 1e9
    gmask = jnp.logical_and(lane >= NE, lane < NE + NG)
    gl = jnp.where(gmask, lg, ninf)
    gmax = jnp.max(gl, axis=-1, keepdims=True)
    gsum = jnp.sum(jnp.where(gmask, jnp.exp(gl - gmax), 0.0), axis=-1, keepdims=True)
    p_top = 1.0 / gsum
    gi = jnp.min(jnp.where(gl == gmax, lane, big), axis=-1, keepdims=True) - NE
    lo = gi * EPG
    emask = jnp.logical_and(lane >= lo, lane < lo + EPG)
    el = jnp.where(emask, lg, ninf)
    m1 = jnp.max(el, axis=-1, keepdims=True)
    i1 = jnp.min(jnp.where(el == m1, lane, big), axis=-1, keepdims=True)
    el2 = jnp.where(lane == i1, ninf, el)
    m2 = jnp.max(el2, axis=-1, keepdims=True)
    i2 = jnp.min(jnp.where(el2 == m2, lane, big), axis=-1, keepdims=True)
    e2 = jnp.exp(m2 - m1)
    den = 1.0 + e2
    w1 = p_top / den
    w2 = p_top * e2 / den
    cnt = jnp.sum(jnp.where(lane == i1, 1.0, 0.0) + jnp.where(lane == i2, 1.0, 0.0),
                  axis=0, keepdims=True)
    cols = (jnp.where(lane == 0.0, i1, 0.0) + jnp.where(lane == 1.0, i2, 0.0)
            + jnp.where(lane == 2.0, w1, 0.0) + jnp.where(lane == 3.0, w2, 0.0))
    return cols.T[0:8, :], cnt


def _tail(x, ns, nc, gt1, sc2, sh2, w_out_ref, g_ffn, w_r_ref, b_r):
    merged = _bdot(jnp.concatenate([ns, nc], axis=-1), w_out_ref[...])
    x1 = x + gt1 * merged
    n2 = _rms(x1) * g_ffn * (1.0 + sc2) + sh2
    rt, cnt = _route(n2, w_r_ref, b_r)
    return x1, n2, rt, cnt


def _prompt_mixer_kernel(x_ref, mod_ref, g_mix_ref, w_in_ref, wb_ref, a_ref, wc_ref, dsk_ref,
                         w_glu_ref, b_glu_ref, w_dw_ref, b_dw_ref, ln_g_ref, ln_b_ref,
                         g_os_ref, g_oc_ref, w_out_ref, g_ffn_ref, w_r_ref, b_r_ref,
                         x1_ref, n2_ref, rt_ref, cnt_ref, hr_ref, hi_ref, cache_ref,
                         sre, sim, hst, ebuf, eshift, ncbuf, *, tc, pt):
    c = pl.program_id(1)
    nc_chunks = pl.num_programs(1)

    @pl.when(c == 0)
    def _():
        hst[...] = jnp.zeros_like(hst)
        ebuf[pl.ds(0, 32), :] = jnp.zeros((32, CONV_CH), F32)

    x = x_ref[...]
    mod = mod_ref[...]
    sh1, sc1, gt1 = mod[0:1], mod[1:2], mod[2:3]
    sh2, sc2 = mod[3:4], mod[4:5]

    u, glu = _front(x, sc1, sh1, g_mix_ref[...], w_in_ref)
    ebuf[pl.ds(32, tc), :] = glu

    for s in range(1, 8):
        eshift[s - 1, pl.ds(0, tc + 24), :] = ebuf[pl.ds(s, tc + 24), :]
    rb = 64
    convs = []
    for r0 in range(0, tc, rb):
        acc = None
        for k in range(KW):
            a8, s = (k + 2) // 8 * 8, (k + 2) % 8
            win = ebuf[pl.ds(r0 + a8, rb), :] if s == 0 else eshift[s - 1, pl.ds(r0 + a8, rb), :]
            term = w_dw_ref[k:k + 1, :] * win
            acc = term if acc is None else acc + term
        convs.append(acc)
    conv = jnp.concatenate(convs, axis=0)
    ncbuf[...] = _conv_out(conv, b_dw_ref[...], ln_g_ref[...], ln_b_ref[...], g_oc_ref[...])

    @pl.when(c == nc_chunks - 1)
    def _():
        cache_ref[...] = ebuf[pl.ds(tc + 2, CB), :]

    ebuf[pl.ds(0, 32), :] = ebuf[pl.ds(tc, 32), :]

    ub = u.astype(BF16)
    for q in range(4):
        r = jnp.dot(ub[:, q * LANES:(q + 1) * LANES], wb_ref[q], preferred_element_type=F32)
        for k in range(4):
            j = 4 * q + k
            sre[pl.ds(j * pt, tc), :] = r[:, k * LANES:(k + 1) * LANES]
            sim[pl.ds(j * pt, tc), :] = r[:, SSM_W + k * LANES:SSM_W + (k + 1) * LANES]

    ar0, ar1 = a_ref[0, 0:8, :], a_ref[0, 8:16, :]
    ai0, ai1 = a_ref[1, 0:8, :], a_ref[1, 8:16, :]

    def step(g, carry):
        hr0, hr1, hi0, hi1 = carry
        t0 = g * SCAN_GROUP
        idx = [(pl.ds(t0 + u, 8, stride=pt), pl.ds(t0 + u + 8 * pt, 8, stride=pt))
               for u in range(SCAN_GROUP)]
        bu = [(sre[i0, :], sre[i1, :], sim[i0, :], sim[i1, :]) for (i0, i1) in idx]
        hs = []
        for br0, br1, bi0, bi1 in bu:
            nr0 = ar0 * hr0 - ai0 * hi0 + br0
            ni0 = ar0 * hi0 + ai0 * hr0 + bi0
            nr1 = ar1 * hr1 - ai1 * hi1 + br1
            ni1 = ar1 * hi1 + ai1 * hr1 + bi1
            hr0, hr1, hi0, hi1 = nr0, nr1, ni0, ni1
            hs.append((nr0, nr1, ni0, ni1))
        for (i0, i1), (nr0, nr1, ni0, ni1) in zip(idx, hs):
            sre[i0, :] = nr0
            sim[i0, :] = ni0
            sre[i1, :] = nr1
            sim[i1, :] = ni1
        return hr0, hr1, hi0, hi1

    init = (hst[0, 0:8, :], hst[0, 8:16, :], hst[1, 0:8, :], hst[1, 8:16, :])
    hr0, hr1, hi0, hi1 = lax.fori_loop(0, tc // SCAN_GROUP, step, init)
    hst[0, 0:8, :] = hr0
    hst[0, 8:16, :] = hr1
    hst[1, 0:8, :] = hi0
    hst[1, 8:16, :] = hi1

    @pl.when(c == nc_chunks - 1)
    def _():
        hr_ref[...] = hst[0]
        hi_ref[...] = hst[1]

    ys = []
    for q in range(4):
        acc = None
        for jj in range(4):
            j = 4 * q + jj
            lhs = jnp.concatenate([sre[pl.ds(j * pt, tc), :], sim[pl.ds(j * pt, tc), :]], axis=-1)
            d = jnp.dot(lhs.astype(BF16), wc_ref[j], preferred_element_type=F32)
            acc = d if acc is None else acc + d
        ys.append(acc)
    y_lin = jnp.concatenate(ys, axis=-1)
    ns = _ssm_out(y_lin, u, dsk_ref[...], w_glu_ref, b_glu_ref[...], g_os_ref[...])
    nc = ncbuf[...]

    x1, n2, rt, cnt = _tail(x, ns, nc, gt1, sc2, sh2, w_out_ref, g_ffn_ref[...], w_r_ref, b_r_ref[...])
    x1_ref[...] = x1
    n2_ref[...] = n2.astype(BF16)
    rt_ref[...] = rt
    cnt_ref[...] = cnt


def _const_spec(shape):
    nd = len(shape)
    return pl.BlockSpec(shape, lambda b, c: (0,) * nd)


def _prompt_mixer_call(x, mod6, wts, tc):
    bsz, t, _ = x.shape
    n_all = bsz * t
    pt = tc + SUBLANES
    assert (pt // SUBLANES) % 2 == 1
    nc = t // tc
    kern = functools.partial(_prompt_mixer_kernel, tc=tc, pt=pt)
    in_specs = [pl.BlockSpec((None, tc, D), lambda b, c: (b, c, 0)),
                pl.BlockSpec((None, 6, D), lambda b, c: (b, 0, 0))]
    in_specs += [_const_spec(w.shape) for w in wts]
    out_shape = (jax.ShapeDtypeStruct((n_all, D), F32),
                 jax.ShapeDtypeStruct((n_all, D), BF16),
                 jax.ShapeDtypeStruct((8, n_all), F32),
                 jax.ShapeDtypeStruct((n_all // tc, 1, LANES), F32),
                 jax.ShapeDtypeStruct((bsz, NCHUNK, LANES), F32),
                 jax.ShapeDtypeStruct((bsz, NCHUNK, LANES), F32),
                 jax.ShapeDtypeStruct((bsz, CB, CONV_CH), F32))
    out_specs = (pl.BlockSpec((tc, D), lambda b, c: (b * nc + c, 0)),
                 pl.BlockSpec((tc, D), lambda b, c: (b * nc + c, 0)),
                 pl.BlockSpec((8, tc), lambda b, c: (0, b * nc + c)),
                 pl.BlockSpec((None, 1, LANES), lambda b, c: (b * nc + c, 0, 0)),
                 pl.BlockSpec((None, NCHUNK, LANES), lambda b, c: (b, 0, 0)),
                 pl.BlockSpec((None, NCHUNK, LANES), lambda b, c: (b, 0, 0)),
                 pl.BlockSpec((None, CB, CONV_CH), lambda b, c: (b, 0, 0)))
    scratch = [pltpu.VMEM((NCHUNK * pt, LANES), F32),
               pltpu.VMEM((NCHUNK * pt, LANES), F32),
               pltpu.VMEM((2, NCHUNK, LANES), F32),
               pltpu.VMEM((tc + 32, CONV_CH), F32),
               pltpu.VMEM((7, tc + 32, CONV_CH), F32),
               pltpu.VMEM((tc, CONV_CH), F32)]
    return pl.pallas_call(
        kern, grid=(bsz, nc), in_specs=in_specs, out_specs=out_specs, out_shape=out_shape,
        scratch_shapes=scratch,
        compiler_params=pltpu.CompilerParams(dimension_semantics=("arbitrary", "arbitrary"),
                                             vmem_limit_bytes=VMEM_LIMIT),
        name="prompt_mixer",
    )(x, mod6, *wts)


def _sample_mixer_kernel(x_ref, mod_ref, h0r_ref, h0i_ref, cache_ref,
                         g_mix_ref, w_in_ref, wb_ref, a_ref, wc_ref, dsk_ref,
                         w_glu_ref, b_glu_ref, w_dw_ref, b_dw_ref, ln_g_ref, ln_b_ref,
                         g_os_ref, g_oc_ref, w_out_ref, g_ffn_ref, w_r_ref, b_r_ref,
                         x1_ref, n2_ref, rt_ref, cnt_ref, hr_ref, hi_ref, glu_ref,
                         sre, sim, *, nb, nt):
    x = x_ref[...]

    def rows(i):
        m = mod_ref[:, i * D:(i + 1) * D]
        return jnp.concatenate([m] * nt, axis=0)

    sh1, sc1, gt1, sh2, sc2 = rows(0), rows(1), rows(2), rows(3), rows(4)
    u, glu = _front(x, sc1, sh1, g_mix_ref[...], w_in_ref)
    glu_ref[...] = glu

    ub = u.astype(BF16)
    for q in range(4):
        r = jnp.dot(ub[:, q * LANES:(q + 1) * LANES], wb_ref[q], preferred_element_type=F32)
        sre[:, q * SSM_W:(q + 1) * SSM_W] = r[:, :SSM_W]
        sim[:, q * SSM_W:(q + 1) * SSM_W] = r[:, SSM_W:]

    ar = a_ref[0:1, :]
    ai = a_ref[1:2, :]
    hr = h0r_ref[...]
    hi = h0i_ref[...]
    for t in range(nt):
        rs = pl.ds(t * nb, nb)
        nr = ar * hr - ai * hi + sre[rs, :]
        ni = ar * hi + ai * hr + sim[rs, :]
        sre[rs, :] = nr
        sim[rs, :] = ni
        hr, hi = nr, ni
    hr_ref[...] = hr
    hi_ref[...] = hi

    ys = []
    for q in range(4):
        acc = None
        for jj in range(4):
            j = 4 * q + jj
            lhs = jnp.concatenate([sre[:, j * LANES:(j + 1) * LANES],
                                   sim[:, j * LANES:(j + 1) * LANES]], axis=-1)
            d = jnp.dot(lhs.astype(BF16), wc_ref[j], preferred_element_type=F32)
            acc = d if acc is None else acc + d
        ys.append(acc)
    y_lin = jnp.concatenate(ys, axis=-1)
    ns = _ssm_out(y_lin, u, dsk_ref[...], w_glu_ref, b_glu_ref[...], g_os_ref[...])

    def ext(jrow):
        if jrow < CB:
            return cache_ref[jrow]
        return glu[(jrow - CB) * nb:(jrow - CB + 1) * nb, :]

    convs = []
    for t in range(nt):
        acc = None
        for k in range(KW):
            term = w_dw_ref[k:k + 1, :] * ext(t + k)
            acc = term if acc is None else acc + term
        convs.append(acc)
    conv = jnp.concatenate(convs, axis=0)
    nc = _conv_out(conv, b_dw_ref[...], ln_g_ref[...], ln_b_ref[...], g_oc_ref[...])

    x1, n2, rt, cnt = _tail(x, ns, nc, gt1, sc2, sh2, w_out_ref, g_ffn_ref[...], w_r_ref, b_r_ref[...])
    x1_ref[...] = x1
    n2_ref[...] = n2.astype(BF16)
    rt_ref[...] = rt
    cnt_ref[...] = cnt


def _sample_mixer_call(x_tm, mod_s, h0r, h0i, cache_tm, wts, nb, nt):
    n = nb * nt
    kern = functools.partial(_sample_mixer_kernel, nb=nb, nt=nt)
    out_shape = (jax.ShapeDtypeStruct((n, D), F32),
                 jax.ShapeDtypeStruct((n, D), BF16),
                 jax.ShapeDtypeStruct((8, n), F32),
                 jax.ShapeDtypeStruct((1, LANES), F32),
                 jax.ShapeDtypeStruct((nb, NSTATE), F32),
                 jax.ShapeDtypeStruct((nb, NSTATE), F32),
                 jax.ShapeDtypeStruct((n, CONV_CH), F32))
    scratch = [pltpu.VMEM((n, NSTATE), F32), pltpu.VMEM((n, NSTATE), F32)]
    return pl.pallas_call(
        kern, out_shape=out_shape, scratch_shapes=scratch,
        compiler_params=pltpu.CompilerParams(vmem_limit_bytes=VMEM_LIMIT),
        name="sample_mixer",
    )(x_tm, mod_s, h0r, h0i, cache_tm, *wts)


ROW_ALIGN = 16
MOE_TD = 512
MOE_BR = MOE_TD * 2 + NE * ROW_ALIGN
MOE_TG = 256


def _slot_positions(rt, base, before):
    t = rt.shape[1]
    e0 = rt[0:1, :]
    e1 = rt[1:2, :]
    sub = lax.broadcasted_iota(jnp.int32, (LANES, t), 0).astype(F32)
    a0 = jnp.where(sub == e0, 1.0, 0.0)
    a1 = jnp.where(sub == e1, 1.0, 0.0)
    at = a0 + a1
    rank = jnp.dot(at.astype(BF16), before, preferred_element_type=F32)
    slot = rank + base
    pos0 = jnp.sum(a0 * slot, axis=0, keepdims=True)
    pos1 = jnp.sum(a1 * slot, axis=0, keepdims=True)
    return pos0, pos1


def _segment_copies(tab_ref, t, n_tiles, buf, hbm, sems, slot, to_hbm, wait):
    for e in range(NE):
        n = pl.multiple_of(tab_ref[t * NE + e], ROW_ALIGN)
        b = pl.multiple_of(tab_ref[(n_tiles + t) * NE + e], ROW_ALIGN)
        d = pl.multiple_of(tab_ref[(2 * n_tiles + t) * NE + e], ROW_ALIGN)
        vm = buf.at[slot, pl.ds(b, n)]
        hb = hbm.at[pl.ds(d, n)]
        cp = pltpu.make_async_copy(vm, hb, sems.at[slot, e]) if to_hbm else \
            pltpu.make_async_copy(hb, vm, sems.at[slot, e])

        @pl.when(n > 0)
        def _():
            if wait:
                cp.wait()
            else:
                cp.start()


def _tile_rows(tab_ref, t, n_tiles):
    return tab_ref[3 * n_tiles * NE + 2 * NE + 1 + t]


def _one_hot_rows(r0, nrows, pos0, pos1):
    row = (lax.broadcasted_iota(jnp.int32, (nrows, MOE_TD), 0) + r0).astype(F32)
    return row == pos0, row == pos1


MOE_BLK = 128
MOE_MAIN = 2 * MOE_TD + 3 * MOE_BLK


def _dispatch_kernel(tab_ref, rtp_ref, n2p_ref, rts_ref, n2s_ref, base_ref, xs_ref, pos_ref, buf, zbuf, before,
                     sems, zsem,
                     *, n_tiles, n_ptiles, n_gtiles):
    t = pl.program_id(0)
    slot = lax.rem(t, 2)
    first_free = tab_ref[3 * n_tiles * NE + 2 * NE]

    def fill_copy(j):
        d = pl.multiple_of(j * MOE_TG, MOE_TG)
        return pltpu.make_async_copy(zbuf, xs_ref.at[pl.ds(d, MOE_TG)], zsem)

    def fill_start(j, carry):
        fill_copy(j).start()
        return carry

    def fill_wait(j, carry):
        fill_copy(j).wait()
        return carry

    @pl.when(t == 0)
    def _():
        zbuf[...] = jnp.zeros_like(zbuf)
        r = lax.broadcasted_iota(jnp.int32, (MOE_TD, MOE_TD), 0)
        c = lax.broadcasted_iota(jnp.int32, (MOE_TD, MOE_TD), 1)
        before[...] = jnp.where(r < c, 1.0, 0.0).astype(BF16)
        for phase in range(2):
            for e in range(NE):
                d = pl.multiple_of(tab_ref[3 * n_tiles * NE + e], ROW_ALIGN)
                n = pl.multiple_of(tab_ref[3 * n_tiles * NE + NE + e], ROW_ALIGN)
                cp = pltpu.make_async_copy(zbuf.at[pl.ds(0, n)], xs_ref.at[pl.ds(d, n)], sems.at[1, e])

                @pl.when(n > 0)
                def _():
                    if phase == 0:
                        cp.start()
                    else:
                        cp.wait()

        lax.fori_loop(first_free, n_gtiles, fill_start, 0)

    @pl.when(t >= 2)
    def _():
        _segment_copies(tab_ref, t - 2, n_tiles, buf, xs_ref, sems, slot, to_hbm=True, wait=True)

    is_sample = t >= n_ptiles
    rt = jnp.where(is_sample, rts_ref[...], rtp_ref[...])
    n2 = jnp.where(is_sample, n2s_ref[...], n2p_ref[...])
    pos0, pos1 = _slot_positions(rt, base_ref[...], before[...])
    pos_ref[...] = jnp.concatenate([pos0, pos1, jnp.zeros((6, MOE_TD), F32)], axis=0)
    used = _tile_rows(tab_ref, t, n_tiles)

    def group(r0, nrows):
        m0, m1 = _one_hot_rows(r0, nrows, pos0, pos1)
        q = (jnp.where(m0, 1.0, 0.0) + jnp.where(m1, 1.0, 0.0)).astype(BF16)
        buf[slot, pl.ds(r0, nrows), :] = jnp.dot(q, n2, preferred_element_type=F32).astype(BF16)

    group(0, MOE_MAIN)
    for r0 in range(MOE_MAIN, MOE_BR, MOE_BLK):
        @pl.when(used > r0)
        def _():
            group(r0, MOE_BLK)

    _segment_copies(tab_ref, t, n_tiles, buf, xs_ref, sems, slot, to_hbm=True, wait=False)

    @pl.when(t == n_tiles - 1)
    def _():
        if n_tiles >= 2:
            _segment_copies(tab_ref, t - 1, n_tiles, buf, xs_ref, sems, 1 - slot, to_hbm=True, wait=True)
        _segment_copies(tab_ref, t, n_tiles, buf, xs_ref, sems, slot, to_hbm=True, wait=True)
        lax.fori_loop(first_free, n_gtiles, fill_wait, 0)


def _dispatch_call(tab, base_col, rt_p, n2_p, rt_s, n2_s, r_max):
    n_ptiles = n2_p.shape[0] // MOE_TD
    n_tiles = n_ptiles + n2_s.shape[0] // MOE_TD
    last_p = n_ptiles - 1
    return pl.pallas_call(
        functools.partial(_dispatch_kernel, n_tiles=n_tiles, n_ptiles=n_ptiles,
                          n_gtiles=r_max // MOE_TG),
        grid_spec=pltpu.PrefetchScalarGridSpec(
            num_scalar_prefetch=1, grid=(n_tiles,),
            in_specs=[pl.BlockSpec((8, MOE_TD), lambda t, tab: (0, jnp.minimum(t, last_p))),
                      pl.BlockSpec((MOE_TD, D), lambda t, tab: (jnp.minimum(t, last_p), 0)),
                      pl.BlockSpec((8, MOE_TD), lambda t, tab: (0, 0)),
                      pl.BlockSpec((MOE_TD, D), lambda t, tab: (0, 0)),
                      pl.BlockSpec((None, LANES, 1), lambda t, tab: (t, 0, 0))],
            out_specs=(pl.BlockSpec(memory_space=pl.ANY),
                       pl.BlockSpec((None, 8, MOE_TD), lambda t, tab: (t, 0, 0))),
            scratch_shapes=[pltpu.VMEM((2, MOE_BR, D), BF16),
                            pltpu.VMEM((MOE_TG, D), BF16),
                            pltpu.VMEM((MOE_TD, MOE_TD), BF16),
                            pltpu.SemaphoreType.DMA((2, NE)),
                            pltpu.SemaphoreType.DMA(())]),
        out_shape=(jax.ShapeDtypeStruct((r_max, D), BF16),
                   jax.ShapeDtypeStruct((n_tiles, 8, MOE_TD), F32)),
        compiler_params=pltpu.CompilerParams(dimension_semantics=("arbitrary",),
                                             vmem_limit_bytes=VMEM_LIMIT),
        name="moe_dispatch",
    )(tab, rt_p, n2_p, rt_s, n2_s, base_col)


MOE_CK = 6


def _experts_kernel(ctab_ref, xs_ref, w1_ref, w3_ref, w2_ref, ys_ref,
                    xbuf, ybuf, zbuf, in_sem, out_sem, zsem, *, n_gtiles, nc_max):
    e = pl.program_id(0)
    c0 = ctab_ref[e]
    c1 = ctab_ref[e + 1]
    n_chunks = ctab_ref[NE]
    first_free = ctab_ref[NE + 1 + 2 * nc_max]

    def fill_copy(j):
        d = pl.multiple_of(j * MOE_TG, MOE_TG)
        return pltpu.make_async_copy(zbuf, ys_ref.at[pl.ds(d, MOE_TG)], zsem)

    def fill_start(j, carry):
        fill_copy(j).start()
        return carry

    def fill_wait(j, carry):
        fill_copy(j).wait()
        return carry

    def span(c):
        r = pl.multiple_of(ctab_ref[NE + 1 + c], MOE_TG)
        n = pl.multiple_of(ctab_ref[NE + 1 + nc_max + c] * MOE_TG, MOE_TG)
        return r, n

    def in_copy(c, slot):
        r, n = span(c)
        return pltpu.make_async_copy(xs_ref.at[pl.ds(r, n)], xbuf.at[slot, pl.ds(0, n)], in_sem.at[slot])

    def out_copy(c, slot):
        r, n = span(c)
        return pltpu.make_async_copy(ybuf.at[slot, pl.ds(0, n)], ys_ref.at[pl.ds(r, n)], out_sem.at[slot])

    @pl.when(e == 0)
    def _():
        zbuf[...] = jnp.zeros_like(zbuf)
        lax.fori_loop(first_free, n_gtiles, fill_start, 0)

        @pl.when(n_chunks > 0)
        def _():
            in_copy(0, 0).start()

    def compute(slot, rows):
        x = xbuf[slot, pl.ds(0, rows), :]
        a = jnp.dot(x, w1_ref[...].astype(BF16), preferred_element_type=F32)
        b = jnp.dot(x, w3_ref[...].astype(BF16), preferred_element_type=F32)
        hid = a * _sigmoid(a) * b
        y = jnp.dot(hid.astype(BF16), w2_ref[...].astype(BF16), preferred_element_type=F32)
        ybuf[slot, pl.ds(0, rows), :] = y.astype(BF16)

    @pl.when(c1 > c0)
    def _():
        def chunk(c, carry):
            slot = lax.rem(c, 2)

            @pl.when(c + 1 < n_chunks)
            def _():
                in_copy(c + 1, 1 - slot).start()

            in_copy(c, slot).wait()

            @pl.when(c >= 2)
            def _():
                out_copy(c - 2, slot).wait()

            k = ctab_ref[NE + 1 + nc_max + c]
            for kk in range(1, MOE_CK + 1):
                @pl.when(k == kk)
                def _():
                    compute(slot, kk * MOE_TG)

            out_copy(c, slot).start()
            return carry

        lax.fori_loop(c0, c1, chunk, 0)

    @pl.when(e == NE - 1)
    def _():
        @pl.when(n_chunks >= 2)
        def _():
            out_copy(n_chunks - 2, lax.rem(n_chunks, 2)).wait()

        @pl.when(n_chunks >= 1)
        def _():
            out_copy(n_chunks - 1, lax.rem(n_chunks - 1, 2)).wait()

        lax.fori_loop(first_free, n_gtiles, fill_wait, 0)


def _experts_call(ctab, xs, w1, w3, w2, nc_max):
    r_max = xs.shape[0]
    w_map = lambda e, ctab: (e, 0, 0)
    ring = pltpu.VMEM((2, MOE_CK * MOE_TG, D), BF16)
    return pl.pallas_call(
        functools.partial(_experts_kernel, n_gtiles=r_max // MOE_TG, nc_max=nc_max),
        grid_spec=pltpu.PrefetchScalarGridSpec(
            num_scalar_prefetch=1, grid=(NE,),
            in_specs=[pl.BlockSpec(memory_space=pl.ANY),
                      pl.BlockSpec((None, D, DE), w_map),
                      pl.BlockSpec((None, D, DE), w_map),
                      pl.BlockSpec((None, DE, D), w_map)],
            out_specs=pl.BlockSpec(memory_space=pl.ANY),
            scratch_shapes=[ring, ring,
                            pltpu.VMEM((MOE_TG, D), BF16),
                            pltpu.SemaphoreType.DMA((2,)), pltpu.SemaphoreType.DMA((2,)),
                            pltpu.SemaphoreType.DMA(())]),
        out_shape=jax.ShapeDtypeStruct((r_max, D), BF16),
        compiler_params=pltpu.CompilerParams(dimension_semantics=("arbitrary",),
                                             vmem_limit_bytes=VMEM_LIMIT),
        name="moe_experts",
    )(ctab, xs, w1, w3, w2)


def _combine_kernel(tab_ref, rt_ref, pos_ref, x1_ref, gt2_ref, gf_ref, ys_ref, y_ref, buf, sems,
                    *, n_tiles, t_off):
    i = pl.program_id(0)
    t = i + t_off
    slot = lax.rem(i, 2)

    @pl.when(i == 0)
    def _():
        buf[...] = jnp.zeros_like(buf)
        _segment_copies(tab_ref, t, n_tiles, buf, ys_ref, sems, slot, to_hbm=False, wait=False)

    @pl.when(i + 1 < pl.num_programs(0))
    def _():
        _segment_copies(tab_ref, t + 1, n_tiles, buf, ys_ref, sems, 1 - slot, to_hbm=False, wait=False)

    _segment_copies(tab_ref, t, n_tiles, buf, ys_ref, sems, slot, to_hbm=False, wait=True)

    rt = rt_ref[...]
    pos0 = pos_ref[0:1, :]
    pos1 = pos_ref[1:2, :]
    used = _tile_rows(tab_ref, t, n_tiles)

    def ungroup(r0, nrows):
        m0, m1 = _one_hot_rows(r0, nrows, pos0, pos1)
        q = (jnp.where(m0, 1.0, 0.0) + jnp.where(m1, 1.0, 0.0)).astype(BF16)
        gw = jnp.sum(jnp.where(m0, rt[2:3, :], 0.0) + jnp.where(m1, rt[3:4, :], 0.0),
                     axis=1, keepdims=True)
        yv = (buf[slot, pl.ds(r0, nrows), :].astype(F32) * gw).astype(BF16)
        return lax.dot_general(q, yv, (((0,), (0,)), ((), ())), preferred_element_type=F32)

    gt2 = gt2_ref[...]
    if gt2.shape[0] not in (1, MOE_TD):
        gt2 = jnp.concatenate([gt2] * (MOE_TD // gt2.shape[0]), axis=0)

    def finish(moe):
        xo = x1_ref[...] + gt2 * moe
        y_ref[...] = _rms(xo) * gf_ref[...]

    finish(ungroup(0, MOE_MAIN))

    @pl.when(used > MOE_MAIN)
    def _():
        moe = ungroup(0, MOE_MAIN)
        for r0 in range(MOE_MAIN, MOE_BR, MOE_BLK):
            moe = moe + ungroup(r0, MOE_BLK)
        finish(moe)


def _combine_call(tab, rt, pos, x1, gt2, gt2_spec, g_final, ys, n_tiles, t_off):
    n_out_tiles = x1.shape[0] // MOE_TD
    return pl.pallas_call(
        functools.partial(_combine_kernel, n_tiles=n_tiles, t_off=t_off),
        grid_spec=pltpu.PrefetchScalarGridSpec(
            num_scalar_prefetch=1, grid=(n_out_tiles,),
            in_specs=[pl.BlockSpec((8, MOE_TD), lambda t, tab: (0, t)),
                      pl.BlockSpec((None, 8, MOE_TD), lambda t, tab: (t + t_off, 0, 0)),
                      pl.BlockSpec((MOE_TD, D), lambda t, tab: (t, 0)),
                      gt2_spec,
                      pl.BlockSpec((1, D), lambda t, tab: (0, 0)),
                      pl.BlockSpec(memory_space=pl.ANY)],
            out_specs=pl.BlockSpec((MOE_TD, D), lambda t, tab: (t, 0)),
            scratch_shapes=[pltpu.VMEM((2, MOE_BR, D), BF16),
                            pltpu.SemaphoreType.DMA((2, NE))]),
        out_shape=jax.ShapeDtypeStruct((n_out_tiles * MOE_TD, D), F32),
        compiler_params=pltpu.CompilerParams(dimension_semantics=("arbitrary",),
                                             vmem_limit_bytes=VMEM_LIMIT),
        name="moe_combine",
    )(tab, rt, pos, x1, gt2, g_final.reshape(1, D), ys)


def _moe_plan(cnt, nc_max):
    cnt_al = (cnt + ROW_ALIGN - 1) // ROW_ALIGN * ROW_ALIGN
    seg_rows = cnt_al.sum(axis=0)
    seg_pad = (seg_rows + MOE_TG - 1) // MOE_TG * MOE_TG
    seg_start = jnp.cumsum(seg_pad) - seg_pad
    dst = seg_start[None, :] + jnp.cumsum(cnt_al, axis=0) - cnt_al
    boff = jnp.cumsum(cnt_al, axis=1) - cnt_al
    tile_end = jnp.cumsum(seg_pad // MOE_TG)
    n_active = tile_end[-1:].astype(jnp.int32)
    tab = jnp.concatenate([cnt_al.ravel(), boff.ravel(), dst.ravel(),
                           seg_start + seg_rows, seg_pad - seg_rows, n_active,
                           cnt_al.sum(axis=1)]).astype(jnp.int32)
    nt = seg_pad // MOE_TG
    nfull = nt // MOE_CK
    rem = nt % MOE_CK
    nch = nfull + (rem > 0)
    cend = jnp.cumsum(nch)
    cstart = cend - nch
    c = jnp.arange(nc_max, dtype=jnp.int32)
    ce = jnp.minimum(jnp.sum(c[:, None] >= cend[None, :], axis=1), NE - 1)
    local = c - cstart[ce]
    valid = c < cend[-1]
    ck = jnp.where(valid, jnp.where(local < nfull[ce], MOE_CK, rem[ce]), 0)
    crow = jnp.where(valid, seg_start[ce] + local * (MOE_CK * MOE_TG), 0)
    ctab = jnp.concatenate([cstart, cend[-1:], crow, ck, n_active]).astype(jnp.int32)
    base_col = jnp.pad(boff, ((0, 0), (0, LANES - NE))).astype(F32)[:, :, None]
    return tab, ctab, base_col


def kernel(x_prompt, x_sample, c_prompt, c_sample, state_ssm_re, state_ssm_im, cache_conv, w_ada, b_ada, g_norm_mix, w_in, ssm_a_re, ssm_a_im, ssm_log_dt, ssm_b_re, ssm_b_im, ssm_c_re, ssm_c_im, ssm_d, w_ssm_glu, b_ssm_glu, w_dw, b_dw, ln_conv_g, ln_conv_b, g_out_ssm, g_out_conv, w_out, g_norm_ffn, w_router_grp, b_router_grp, w_router_exp, b_router_exp, w_exp_gate, w_exp_up, w_exp_down, g_final):
    depth = w_ada.shape[0]
    assert depth == 1
    bsz, seq, _ = x_prompt.shape
    nb, nt, _ = x_sample.shape

    n_c = bsz + nb
    c_pad = -n_c % 16
    c_all = jnp.concatenate([c_prompt, c_sample, jnp.zeros((c_pad, D), F32)], axis=0)
    mod_p, mod_s = _mod_call(c_all, w_ada[0], b_ada[0], bsz, nb)
    mod_p = mod_p.reshape(bsz, 6, D)

    ab_re, ab_im, bb_re, bb_im, c_im_neg = _ssm_prep_call(
        ssm_a_re[0], ssm_a_im[0], ssm_log_dt[0], ssm_b_re[0], ssm_b_im[0], ssm_c_im[0])
    wb, wc = _block_diag_weights(bb_re, bb_im, ssm_c_re[0], c_im_neg)
    a_tok = jnp.stack([ab_re.reshape(NCHUNK, LANES), ab_im.reshape(NCHUNK, LANES)])
    a_row = jnp.stack([ab_re.reshape(NSTATE), ab_im.reshape(NSTATE)])

    w_r = jnp.concatenate([w_router_exp[0].reshape(D, NE), w_router_grp[0],
                           jnp.zeros((D, LANES - NE - NG), F32)], axis=1)
    w_r_hi = w_r.astype(BF16)
    w_r = jnp.concatenate([w_r_hi, (w_r - w_r_hi.astype(F32)).astype(BF16)], axis=1)
    b_r = jnp.concatenate([b_router_exp[0].reshape(NE), b_router_grp[0],
                           jnp.zeros((LANES - NE - NG,), F32)]).reshape(1, LANES)
    w_dw_p = jnp.concatenate([w_dw[0], jnp.zeros((1, CONV_CH), F32)], axis=0)

    row = lambda v: v.reshape(1, -1)
    common_a = (row(g_norm_mix[0]), w_in[0].astype(BF16), wb)
    common_b = (wc, row(ssm_d[0].reshape(SSM_W)), w_ssm_glu[0].astype(BF16), row(b_ssm_glu[0]),
                w_dw_p, row(b_dw[0]), row(ln_conv_g[0]), row(ln_conv_b[0]),
                row(g_out_ssm[0]), row(g_out_conv[0]), w_out[0].astype(BF16),
                row(g_norm_ffn[0]), w_r, b_r)

    n_p = bsz * seq
    n_s = nb * nt
    n_all = n_p + n_s
    assert n_s == MOE_TD and seq % MOE_TD == 0 and MOE_TD % PROMPT_TC == 0
    wts_p = common_a + (a_tok,) + common_b
    x1_p, n2_p, rt_p, cnt_p, hr_p, hi_p, cache_p = _prompt_mixer_call(x_prompt, mod_p, wts_p, PROMPT_TC)

    x_tm = jnp.transpose(x_sample, (1, 0, 2)).reshape(nt * nb, D)
    cache_tm = jnp.transpose(cache_conv[0], (1, 0, 2))
    wts_s = common_a + (a_row,) + common_b
    x1_s, n2_s, rt_s, cnt_s, hr_s, hi_s, glu_s = _sample_mixer_call(
        x_tm, mod_s, state_ssm_re[0].reshape(nb, NSTATE), state_ssm_im[0].reshape(nb, NSTATE),
        cache_tm, wts_s, nb, nt)

    n_ptiles = n_p // MOE_TD
    n_tiles = n_all // MOE_TD
    r_max = -(-(2 * n_all + n_tiles * NE * (ROW_ALIGN - 1) + NE * (MOE_TG - ROW_ALIGN)) // MOE_TG) * MOE_TG
    cnt = jnp.concatenate([cnt_p.reshape(n_ptiles, MOE_TD // PROMPT_TC, LANES).sum(axis=1), cnt_s])
    nc_max = r_max // MOE_TG // MOE_CK + NE
    tab, ctab, base_col = _moe_plan(cnt[:, :NE].astype(jnp.int32), nc_max)
    xs, pos = _dispatch_call(tab, base_col, rt_p, n2_p, rt_s, n2_s, r_max)
    ys = _experts_call(ctab, xs, w_exp_gate[0], w_exp_up[0], w_exp_down[0], nc_max)
    tiles_per_b = seq // MOE_TD
    gt2_p = mod_p[:, 5:6, :]
    y_p = _combine_call(tab, rt_p, pos, x1_p, gt2_p,
                        pl.BlockSpec((None, 1, D), lambda t, tab: (t // tiles_per_b, 0, 0)),
                        g_final, ys, n_tiles, 0)
    y_s = _combine_call(tab, rt_s, pos, x1_s, mod_s,
                        pl.BlockSpec((nb, D), lambda t, tab: (0, 5)),
                        g_final, ys, n_tiles, n_ptiles)

    y_prompt = y_p.reshape(bsz, seq, D)
    y_sample = jnp.transpose(y_s.reshape(nt, nb, D), (1, 0, 2))
    new_cache_s = jnp.concatenate(
        [cache_conv[0][:, nt:, :], jnp.transpose(glu_s.reshape(nt, nb, CONV_CH), (1, 0, 2))], axis=1)
    return (y_prompt, y_sample,
            hr_p.reshape(1, bsz, G, P), hi_p.reshape(1, bsz, G, P), cache_p[None],
            hr_s.reshape(1, nb, G, P), hi_s.reshape(1, nb, G, P), new_cache_s[None])
```

```python
import functools

import jax
import jax.numpy as jnp
import numpy as np
from jax import lax
from jax.experimental import pallas as pl
from jax.experimental.pallas import tpu as pltpu

F32 = jnp.float32
BF16 = jnp.bfloat16

D = 1024
SSM_W = 512
CONV_CH = 512
G = 32
H = 16
P = 64
KW = 31
CB = KW - 1
NE = 32
NG = 4
EPG = 8
DE = 512
EPS = 1e-6
LANES = 128
SUBLANES = 8
NSTATE = G * P
NCHUNK = NSTATE // LANES

PROMPT_TC = 512
SCAN_GROUP = 8
V7X_VMEM_BYTES = 64 * 1024 * 1024
VMEM_LIMIT = V7X_VMEM_BYTES - 8 * 1024 * 1024


def _rms(x):
    return x * lax.rsqrt(jnp.mean(x * x, axis=-1, keepdims=True) + EPS)


def _sigmoid(x):
    return 0.5 * jnp.tanh(0.5 * x) + 0.5


def _gelu_tanh(y):
    c = np.sqrt(2.0 / np.pi).astype(np.float32)
    return y * (0.5 * (1.0 + jnp.tanh(c * (y + 0.044715 * (y * y * y)))))


def _bdot(a, b):
    return jnp.dot(a.astype(BF16), b, preferred_element_type=F32)


def _mod_kernel(c_ref, w_ref, b_ref, op_ref, os_ref):
    c = c_ref[...]
    s = c * _sigmoid(c)
    n = s.shape[0]
    s_hi = s.astype(BF16)
    lhs = jnp.concatenate([s_hi, (s - s_hi.astype(F32)).astype(BF16)], axis=0)
    w = w_ref[...]
    w_hi = w.astype(BF16)
    w_lo = (w - w_hi.astype(F32)).astype(BF16)
    p_hi = jnp.dot(lhs, w_hi, preferred_element_type=F32)
    p_lo = jnp.dot(lhs, w_lo, preferred_element_type=F32)
    res = (p_hi[:n] + p_lo[:n]) + (p_hi[n:] + p_lo[n:]) + b_ref[...]
    n_p = op_ref.shape[0]
    op_ref[...] = res[:n_p]
    os_ref[...] = res[n_p:n_p + os_ref.shape[0]]


def _mod_call(c_all, w_ada, b_ada, n_p, n_s):
    n = c_all.shape[0]
    tn = 1024
    return pl.pallas_call(
        _mod_kernel,
        grid=(6 * D // tn,),
        in_specs=[pl.BlockSpec((n, D), lambda j: (0, 0)),
                  pl.BlockSpec((D, tn), lambda j: (0, j)),
                  pl.BlockSpec((1, tn), lambda j: (0, j))],
        out_specs=(pl.BlockSpec((n_p, tn), lambda j: (0, j)),
                   pl.BlockSpec((n_s, tn), lambda j: (0, j))),
        out_shape=(jax.ShapeDtypeStruct((n_p, 6 * D), F32),
                   jax.ShapeDtypeStruct((n_s, 6 * D), F32)),
        compiler_params=pltpu.CompilerParams(dimension_semantics=("arbitrary",),
                                             vmem_limit_bytes=VMEM_LIMIT),
        name="mod",
    )(c_all, w_ada, b_ada.reshape(1, 6 * D))


def _ssm_prep_kernel(a_re, a_im, log_dt, b_re, b_im, c_im,
                     ab_re_o, ab_im_o, bb_re_o, bb_im_o, cneg_o):
    lam_re = jnp.minimum(a_re[...], -1e-4)
    lam_im = a_im[...]
    dt = jnp.exp(log_dt[...])
    mag = jnp.exp(lam_re * dt)
    ab_re = mag * jnp.cos(lam_im * dt)
    ab_im = mag * jnp.sin(lam_im * dt)
    den = lam_re * lam_re + lam_im * lam_im
    num_re = ab_re - 1.0
    coef_re = (num_re * lam_re + ab_im * lam_im) / den
    coef_im = (ab_im * lam_re - num_re * lam_im) / den
    ab_re_o[...] = ab_re
    ab_im_o[...] = ab_im
    br = b_re[...]
    bi = b_im[...]
    bb_re_o[...] = coef_re * br - coef_im * bi
    bb_im_o[...] = coef_re * bi + coef_im * br
    cneg_o[...] = -c_im[...]


def _ssm_prep_call(a_re, a_im, log_dt, b_re, b_im, c_im):
    flat = lambda v: v.reshape(1, NSTATE)
    b_hs = lambda v: jnp.transpose(v, (2, 0, 1)).reshape(H, NSTATE)
    dt_row = jnp.broadcast_to(log_dt[:, None], (G, P)).reshape(1, NSTATE)
    ab_re, ab_im, bb_re, bb_im, cneg = pl.pallas_call(
        _ssm_prep_kernel,
        out_shape=(jax.ShapeDtypeStruct((1, NSTATE), F32), jax.ShapeDtypeStruct((1, NSTATE), F32),
                   jax.ShapeDtypeStruct((H, NSTATE), F32), jax.ShapeDtypeStruct((H, NSTATE), F32),
                   jax.ShapeDtypeStruct((G * H, P), F32)),
        name="ssm_prep",
    )(flat(a_re), flat(a_im), dt_row, b_hs(b_re), b_hs(b_im), c_im.reshape(G * H, P))
    ghp = lambda v: jnp.transpose(v.reshape(H, G, P), (1, 0, 2))
    return (ab_re.reshape(G, P), ab_im.reshape(G, P), ghp(bb_re), ghp(bb_im), cneg.reshape(G, H, P))


def _block_diag_weights(bb_re, bb_im, c_re, c_im_neg):
    eye8 = jnp.eye(8, dtype=F32)
    eye4 = jnp.eye(4, dtype=F32)
    eye2 = jnp.eye(2, dtype=F32)

    def wb_part(bb):
        x = bb.reshape(4, 8, H, P)
        return jnp.einsum('qghp,gk->qghkp', x, eye8).reshape(4, 8 * H, 8 * P)

    wb = jnp.concatenate([wb_part(bb_re), wb_part(bb_im)], axis=-1).astype(BF16)

    def wc_part(c):
        x = c.reshape(4, 4, 2, H, P)
        y = jnp.einsum('qjghp,jk,gl->qjgpklh', x, eye4, eye2)
        return y.reshape(NCHUNK, 2 * P, 4 * 2 * H)

    wc = jnp.concatenate([wc_part(c_re), wc_part(c_im_neg)], axis=1).astype(BF16)
    return wb, wc


def _front(x, sc1, sh1, g_mix, w_in_ref):
    n = _rms(x) * g_mix * (1.0 + sc1) + sh1
    proj = _bdot(n, w_in_ref[...])
    u = proj[:, :SSM_W]
    glu = proj[:, SSM_W:SSM_W + CONV_CH] * _sigmoid(proj[:, SSM_W + CONV_CH:])
    return u, glu


def _ssm_out(y_lin, u, d_skip, w_glu_ref, b_glu, g_out_ssm):
    y = _gelu_tanh(y_lin + d_skip * u)
    ys = y * _sigmoid(_bdot(y, w_glu_ref[...]) + b_glu)
    return _rms(ys) * g_out_ssm


def _conv_out(conv, b_dw, ln_g, ln_b, g_out_conv):
    c = conv + b_dw
    mu = jnp.mean(c, axis=-1, keepdims=True)
    cc = c - mu
    var = jnp.mean(cc * cc, axis=-1, keepdims=True)
    ln = cc * lax.rsqrt(var + EPS) * ln_g + ln_b
    yc = ln * _sigmoid(ln)
    return _rms(yc) * g_out_conv


def _route(n2, w_r_ref, b_r):
    rows = n2.shape[0]
    n_hi = n2.astype(BF16)
    n_lo = (n2 - n_hi.astype(F32)).astype(BF16)
    parts = jnp.dot(jnp.concatenate([n_hi, n_lo], axis=0), w_r_ref[...], preferred_element_type=F32)
    lg = (parts[:rows, :LANES] + parts[:rows, LANES:]) + (parts[rows:, :LANES] + parts[rows:, LANES:]) + b_r
    lane = lax.broadcasted_iota(jnp.int32, (rows, LANES), 1).astype(F32)
    ninf = -jnp.inf
    big = 1e9
    gmask = jnp.logical_and(lane >= NE, lane < NE + NG)
    gl = jnp.where(gmask, lg, ninf)
    gmax = jnp.max(gl, axis=-1, keepdims=True)
    gsum = jnp.sum(jnp.where(gmask, jnp.exp(gl - gmax), 0.0), axis=-1, keepdims=True)
    p_top = 1.0 / gsum
    gi = jnp.min(jnp.where(gl == gmax, lane, big), axis=-1, keepdims=True) - NE
    lo = gi * EPG
    emask = jnp.logical_and(lane >= lo, lane < lo + EPG)
    el = jnp.where(emask, lg, ninf)
    m1 = jnp.max(el, axis=-1, keepdims=True)
    i1 = jnp.min(jnp.where(el == m1, lane, big), axis=-1, keepdims=True)
    el2 = jnp.where(lane == i1, ninf, el)
    m2 = jnp.max(el2, axis=-1, keepdims=True)
    i2 = jnp.min(jnp.where(el2 == m2, lane, big), axis=-1, keepdims=True)
    e2 = jnp.exp(m2 - m1)
    den = 1.0 + e2
    w1 = p_top / den
    w2 = p_top * e2 / den
    cnt = jnp.sum(jnp.where(lane == i1, 1.0, 0.0) + jnp.where(lane == i2, 1.0, 0.0),
                  axis=0, keepdims=True)
    cols = (jnp.where(lane == 0.0, i1, 0.0) + jnp.where(lane == 1.0, i2, 0.0)
            + jnp.where(lane == 2.0, w1, 0.0) + jnp.where(lane == 3.0, w2, 0.0))
    return cols.T[0:8, :], cnt


def _tail(x, ns, nc, gt1, sc2, sh2, w_out_ref, g_ffn, w_r_ref, b_r):
    merged = _bdot(jnp.concatenate([ns, nc], axis=-1), w_out_ref[...])
    x1 = x + gt1 * merged
    n2 = _rms(x1) * g_ffn * (1.0 + sc2) + sh2
    rt, cnt = _route(n2, w_r_ref, b_r)
    return x1, n2, rt, cnt


def _prompt_mixer_kernel(x_ref, mod_ref, g_mix_ref, w_in_ref, wb_ref, a_ref, wc_ref, dsk_ref,
                         w_glu_ref, b_glu_ref, w_dw_ref, b_dw_ref, ln_g_ref, ln_b_ref,
                         g_os_ref, g_oc_ref, w_out_ref, g_ffn_ref, w_r_ref, b_r_ref,
                         x1_ref, n2_ref, rt_ref, cnt_ref, hr_ref, hi_ref, cache_ref,
                         sre, sim, hst, ebuf, eshift, ncbuf, *, tc, pt):
    c = pl.program_id(1)
    nc_chunks = pl.num_programs(1)

    @pl.when(c == 0)
    def _():
        hst[...] = jnp.zeros_like(hst)
        ebuf[pl.ds(0, 32), :] = jnp.zeros((32, CONV_CH), F32)

    x = x_ref[...]
    mod = mod_ref[...]
    sh1, sc1, gt1 = mod[0:1], mod[1:2], mod[2:3]
    sh2, sc2 = mod[3:4], mod[4:5]

    u, glu = _front(x, sc1, sh1, g_mix_ref[...], w_in_ref)
    ebuf[pl.ds(32, tc), :] = glu

    for s in range(1, 8):
        eshift[s - 1, pl.ds(0, tc + 24), :] = ebuf[pl.ds(s, tc + 24), :]
    rb = 64
    convs = []
    for r0 in range(0, tc, rb):
        acc = None
        for k in range(KW):
            a8, s = (k + 2) // 8 * 8, (k + 2) % 8
            win = ebuf[pl.ds(r0 + a8, rb), :] if s == 0 else eshift[s - 1, pl.ds(r0 + a8, rb), :]
            term = w_dw_ref[k:k + 1, :] * win
            acc = term if acc is None else acc + term
        convs.append(acc)
    conv = jnp.concatenate(convs, axis=0)
    ncbuf[...] = _conv_out(conv, b_dw_ref[...], ln_g_ref[...], ln_b_ref[...], g_oc_ref[...])

    @pl.when(c == nc_chunks - 1)
    def _():
        cache_ref[...] = ebuf[pl.ds(tc + 2, CB), :]

    ebuf[pl.ds(0, 32), :] = ebuf[pl.ds(tc, 32), :]

    ub = u.astype(BF16)
    for q in range(4):
        r = jnp.dot(ub[:, q * LANES:(q + 1) * LANES], wb_ref[q], preferred_element_type=F32)
        for k in range(4):
            j = 4 * q + k
            sre[pl.ds(j * pt, tc), :] = r[:, k * LANES:(k + 1) * LANES]
            sim[pl.ds(j * pt, tc), :] = r[:, SSM_W + k * LANES:SSM_W + (k + 1) * LANES]

    ar0, ar1 = a_ref[0, 0:8, :], a_ref[0, 8:16, :]
    ai0, ai1 = a_ref[1, 0:8, :], a_ref[1, 8:16, :]

    def step(g, carry):
        hr0, hr1, hi0, hi1 = carry
        t0 = g * SCAN_GROUP
        idx = [(pl.ds(t0 + u, 8, stride=pt), pl.ds(t0 + u + 8 * pt, 8, stride=pt))
               for u in range(SCAN_GROUP)]
        bu = [(sre[i0, :], sre[i1, :], sim[i0, :], sim[i1, :]) for (i0, i1) in idx]
        hs = []
        for br0, br1, bi0, bi1 in bu:
            nr0 = ar0 * hr0 - ai0 * hi0 + br0
            ni0 = ar0 * hi0 + ai0 * hr0 + bi0
            nr1 = ar1 * hr1 - ai1 * hi1 + br1
            ni1 = ar1 * hi1 + ai1 * hr1 + bi1
            hr0, hr1, hi0, hi1 = nr0, nr1, ni0, ni1
            hs.append((nr0, nr1, ni0, ni1))
        for (i0, i1), (nr0, nr1, ni0, ni1) in zip(idx, hs):
            sre[i0, :] = nr0
            sim[i0, :] = ni0
            sre[i1, :] = nr1
            sim[i1, :] = ni1
        return hr0, hr1, hi0, hi1

    init = (hst[0, 0:8, :], hst[0, 8:16, :], hst[1, 0:8, :], hst[1, 8:16, :])
    hr0, hr1, hi0, hi1 = lax.fori_loop(0, tc // SCAN_GROUP, step, init)
    hst[0, 0:8, :] = hr0
    hst[0, 8:16, :] = hr1
    hst[1, 0:8, :] = hi0
    hst[1, 8:16, :] = hi1

    @pl.when(c == nc_chunks - 1)
    def _():
        hr_ref[...] = hst[0]
        hi_ref[...] = hst[1]

    ys = []
    for q in range(4):
        acc = None
        for jj in range(4):
            j = 4 * q + jj
            lhs = jnp.concatenate([sre[pl.ds(j * pt, tc), :], sim[pl.ds(j * pt, tc), :]], axis=-1)
            d = jnp.dot(lhs.astype(BF16), wc_ref[j], preferred_element_type=F32)
            acc = d if acc is None else acc + d
        ys.append(acc)
    y_lin = jnp.concatenate(ys, axis=-1)
    ns = _ssm_out(y_lin, u, dsk_ref[...], w_glu_ref, b_glu_ref[...], g_os_ref[...])
    nc = ncbuf[...]

    x1, n2, rt, cnt = _tail(x, ns, nc, gt1, sc2, sh2, w_out_ref, g_ffn_ref[...], w_r_ref, b_r_ref[...])
    x1_ref[...] = x1
    n2_ref[...] = n2.astype(BF16)
    rt_ref[...] = rt
    cnt_ref[...] = cnt


def _const_spec(shape):
    nd = len(shape)
    return pl.BlockSpec(shape, lambda b, c: (0,) * nd)


def _prompt_mixer_call(x, mod6, wts, tc):
    bsz, t, _ = x.shape
    n_all = bsz * t
    pt = tc + SUBLANES
    assert (pt // SUBLANES) % 2 == 1
    nc = t // tc
    kern = functools.partial(_prompt_mixer_kernel, tc=tc, pt=pt)
    in_specs = [pl.BlockSpec((None, tc, D), lambda b, c: (b, c, 0)),
                pl.BlockSpec((None, 6, D), lambda b, c: (b, 0, 0))]
    in_specs += [_const_spec(w.shape) for w in wts]
    out_shape = (jax.ShapeDtypeStruct((n_all, D), F32),
                 jax.ShapeDtypeStruct((n_all, D), BF16),
                 jax.ShapeDtypeStruct((8, n_all), F32),
                 jax.ShapeDtypeStruct((n_all // tc, 1, LANES), F32),
                 jax.ShapeDtypeStruct((bsz, NCHUNK, LANES), F32),
                 jax.ShapeDtypeStruct((bsz, NCHUNK, LANES), F32),
                 jax.ShapeDtypeStruct((bsz, CB, CONV_CH), F32))
    out_specs = (pl.BlockSpec((tc, D), lambda b, c: (b * nc + c, 0)),
                 pl.BlockSpec((tc, D), lambda b, c: (b * nc + c, 0)),
                 pl.BlockSpec((8, tc), lambda b, c: (0, b * nc + c)),
                 pl.BlockSpec((None, 1, LANES), lambda b, c: (b * nc + c, 0, 0)),
                 pl.BlockSpec((None, NCHUNK, LANES), lambda b, c: (b, 0, 0)),
                 pl.BlockSpec((None, NCHUNK, LANES), lambda b, c: (b, 0, 0)),
                 pl.BlockSpec((None, CB, CONV_CH), lambda b, c: (b, 0, 0)))
    scratch = [pltpu.VMEM((NCHUNK * pt, LANES), F32),
               pltpu.VMEM((NCHUNK * pt, LANES), F32),
               pltpu.VMEM((2, NCHUNK, LANES), F32),
               pltpu.VMEM((tc + 32, CONV_CH), F32),
               pltpu.VMEM((7, tc + 32, CONV_CH), F32),
               pltpu.VMEM((tc, CONV_CH), F32)]
    return pl.pallas_call(
        kern, grid=(bsz, nc), in_specs=in_specs, out_specs=out_specs, out_shape=out_shape,
        scratch_shapes=scratch,
        compiler_params=pltpu.CompilerParams(dimension_semantics=("arbitrary", "arbitrary"),
                                             vmem_limit_bytes=VMEM_LIMIT),
        name="prompt_mixer",
    )(x, mod6, *wts)


def _sample_mixer_kernel(x_ref, mod_ref, h0r_ref, h0i_ref, cache_ref,
                         g_mix_ref, w_in_ref, wb_ref, a_ref, wc_ref, dsk_ref,
                         w_glu_ref, b_glu_ref, w_dw_ref, b_dw_ref, ln_g_ref, ln_b_ref,
                         g_os_ref, g_oc_ref, w_out_ref, g_ffn_ref, w_r_ref, b_r_ref,
                         x1_ref, n2_ref, rt_ref, cnt_ref, hr_ref, hi_ref, glu_ref,
                         sre, sim, *, nb, nt):
    x = x_ref[...]

    def rows(i):
        m = mod_ref[:, i * D:(i + 1) * D]
        return jnp.concatenate([m] * nt, axis=0)

    sh1, sc1, gt1, sh2, sc2 = rows(0), rows(1), rows(2), rows(3), rows(4)
    u, glu = _front(x, sc1, sh1, g_mix_ref[...], w_in_ref)
    glu_ref[...] = glu

    ub = u.astype(BF16)
    for q in range(4):
        r = jnp.dot(ub[:, q * LANES:(q + 1) * LANES], wb_ref[q], preferred_element_type=F32)
        sre[:, q * SSM_W:(q + 1) * SSM_W] = r[:, :SSM_W]
        sim[:, q * SSM_W:(q + 1) * SSM_W] = r[:, SSM_W:]

    ar = a_ref[0:1, :]
    ai = a_ref[1:2, :]
    hr = h0r_ref[...]
    hi = h0i_ref[...]
    for t in range(nt):
        rs = pl.ds(t * nb, nb)
        nr = ar * hr - ai * hi + sre[rs, :]
        ni = ar * hi + ai * hr + sim[rs, :]
        sre[rs, :] = nr
        sim[rs, :] = ni
        hr, hi = nr, ni
    hr_ref[...] = hr
    hi_ref[...] = hi

    ys = []
    for q in range(4):
        acc = None
        for jj in range(4):
            j = 4 * q + jj
            lhs = jnp.concatenate([sre[:, j * LANES:(j + 1) * LANES],
                                   sim[:, j * LANES:(j + 1) * LANES]], axis=-1)
            d = jnp.dot(lhs.astype(BF16), wc_ref[j], preferred_element_type=F32)
            acc = d if acc is None else acc + d
        ys.append(acc)
    y_lin = jnp.concatenate(ys, axis=-1)
    ns = _ssm_out(y_lin, u, dsk_ref[...], w_glu_ref, b_glu_ref[...], g_os_ref[...])

    def ext(jrow):
        if jrow < CB:
            return cache_ref[jrow]
        return glu[(jrow - CB) * nb:(jrow - CB + 1) * nb, :]

    convs = []
    for t in range(nt):
        acc = None
        for k in range(KW):
            term = w_dw_ref[k:k + 1, :] * ext(t + k)
            acc = term if acc is None else acc + term
        convs.append(acc)
    conv = jnp.concatenate(convs, axis=0)
    nc = _conv_out(conv, b_dw_ref[...], ln_g_ref[...], ln_b_ref[...], g_oc_ref[...])

    x1, n2, rt, cnt = _tail(x, ns, nc, gt1, sc2, sh2, w_out_ref, g_ffn_ref[...], w_r_ref, b_r_ref[...])
    x1_ref[...] = x1
    n2_ref[...] = n2.astype(BF16)
    rt_ref[...] = rt
    cnt_ref[...] = cnt


def _sample_mixer_call(x_tm, mod_s, h0r, h0i, cache_tm, wts, nb, nt):
    n = nb * nt
    kern = functools.partial(_sample_mixer_kernel, nb=nb, nt=nt)
    out_shape = (jax.ShapeDtypeStruct((n, D), F32),
                 jax.ShapeDtypeStruct((n, D), BF16),
                 jax.ShapeDtypeStruct((8, n), F32),
                 jax.ShapeDtypeStruct((1, LANES), F32),
                 jax.ShapeDtypeStruct((nb, NSTATE), F32),
                 jax.ShapeDtypeStruct((nb, NSTATE), F32),
                 jax.ShapeDtypeStruct((n, CONV_CH), F32))
    scratch = [pltpu.VMEM((n, NSTATE), F32), pltpu.VMEM((n, NSTATE), F32)]
    return pl.pallas_call(
        kern, out_shape=out_shape, scratch_shapes=scratch,
        compiler_params=pltpu.CompilerParams(vmem_limit_bytes=VMEM_LIMIT),
        name="sample_mixer",
    )(x_tm, mod_s, h0r, h0i, cache_tm, *wts)


ROW_ALIGN = 16
MOE_TD = 512
MOE_BR = MOE_TD * 2 + NE * ROW_ALIGN
MOE_TG = 256


def _slot_positions(rt, base, before):
    t = rt.shape[1]
    e0 = rt[0:1, :]
    e1 = rt[1:2, :]
    sub = lax.broadcasted_iota(jnp.int32, (LANES, t), 0).astype(F32)
    a0 = jnp.where(sub == e0, 1.0, 0.0)
    a1 = jnp.where(sub == e1, 1.0, 0.0)
    at = a0 + a1
    rank = jnp.dot(at.astype(BF16), before, preferred_element_type=F32)
    slot = rank + base
    pos0 = jnp.sum(a0 * slot, axis=0, keepdims=True)
    pos1 = jnp.sum(a1 * slot, axis=0, keepdims=True)
    return pos0, pos1


def _segment_copies(tab_ref, t, n_tiles, buf, hbm, sems, slot, to_hbm, wait):
    for e in range(NE):
        n = pl.multiple_of(tab_ref[t * NE + e], ROW_ALIGN)
        b = pl.multiple_of(tab_ref[(n_tiles + t) * NE + e], ROW_ALIGN)
        d = pl.multiple_of(tab_ref[(2 * n_tiles + t) * NE + e], ROW_ALIGN)
        vm = buf.at[slot, pl.ds(b, n)]
        hb = hbm.at[pl.ds(d, n)]
        cp = pltpu.make_async_copy(vm, hb, sems.at[slot, e]) if to_hbm else \
            pltpu.make_async_copy(hb, vm, sems.at[slot, e])

        @pl.when(n > 0)
        def _():
            if wait:
                cp.wait()
            else:
                cp.start()


def _tile_rows(tab_ref, t, n_tiles):
    return tab_ref[3 * n_tiles * NE + 2 * NE + 1 + t]


def _one_hot_rows(r0, nrows, pos0, pos1):
    row = (lax.broadcasted_iota(jnp.int32, (nrows, MOE_TD), 0) + r0).astype(F32)
    return row == pos0, row == pos1


MOE_BLK = 64
MOE_MAIN = 2 * MOE_TD + 5 * MOE_BLK


def _dispatch_kernel(tab_ref, rtp_ref, n2p_ref, rts_ref, n2s_ref, base_ref, xs_ref, pos_ref, buf, zbuf, before,
                     sems, zsem,
                     *, n_tiles, n_ptiles, n_gtiles):
    t = pl.program_id(0)
    slot = lax.rem(t, 2)
    first_free = tab_ref[3 * n_tiles * NE + 2 * NE]

    def fill_copy(j):
        d = pl.multiple_of(j * MOE_TG, MOE_TG)
        return pltpu.make_async_copy(zbuf, xs_ref.at[pl.ds(d, MOE_TG)], zsem)

    def fill_start(j, carry):
        fill_copy(j).start()
        return carry

    def fill_wait(j, carry):
        fill_copy(j).wait()
        return carry

    @pl.when(t == 0)
    def _():
        zbuf[...] = jnp.zeros_like(zbuf)
        r = lax.broadcasted_iota(jnp.int32, (MOE_TD, MOE_TD), 0)
        c = lax.broadcasted_iota(jnp.int32, (MOE_TD, MOE_TD), 1)
        before[...] = jnp.where(r < c, 1.0, 0.0).astype(BF16)
        for phase in range(2):
            for e in range(NE):
                d = pl.multiple_of(tab_ref[3 * n_tiles * NE + e], ROW_ALIGN)
                n = pl.multiple_of(tab_ref[3 * n_tiles * NE + NE + e], ROW_ALIGN)
                cp = pltpu.make_async_copy(zbuf.at[pl.ds(0, n)], xs_ref.at[pl.ds(d, n)], sems.at[1, e])

                @pl.when(n > 0)
                def _():
                    if phase == 0:
                        cp.start()
                    else:
                        cp.wait()

        lax.fori_loop(first_free, n_gtiles, fill_start, 0)

    @pl.when(t >= 2)
    def _():
        _segment_copies(tab_ref, t - 2, n_tiles, buf, xs_ref, sems, slot, to_hbm=True, wait=True)

    is_sample = t >= n_ptiles
    rt = jnp.where(is_sample, rts_ref[...], rtp_ref[...])
    n2 = jnp.where(is_sample, n2s_ref[...], n2p_ref[...])
    pos0, pos1 = _slot_positions(rt, base_ref[...], before[...])
    pos_ref[...] = jnp.concatenate([pos0, pos1, jnp.zeros((6, MOE_TD), F32)], axis=0)
    used = _tile_rows(tab_ref, t, n_tiles)

    def group(r0, nrows):
        m0, m1 = _one_hot_rows(r0, nrows, pos0, pos1)
        q = (jnp.where(m0, 1.0, 0.0) + jnp.where(m1, 1.0, 0.0)).astype(BF16)
        buf[slot, pl.ds(r0, nrows), :] = jnp.dot(q, n2, preferred_element_type=F32).astype(BF16)

    group(0, MOE_MAIN)
    for r0 in range(MOE_MAIN, MOE_BR, MOE_BLK):
        @pl.when(used > r0)
        def _():
            group(r0, MOE_BLK)

    _segment_copies(tab_ref, t, n_tiles, buf, xs_ref, sems, slot, to_hbm=True, wait=False)

    @pl.when(t == n_tiles - 1)
    def _():
        if n_tiles >= 2:
            _segment_copies(tab_ref, t - 1, n_tiles, buf, xs_ref, sems, 1 - slot, to_hbm=True, wait=True)
        _segment_copies(tab_ref, t, n_tiles, buf, xs_ref, sems, slot, to_hbm=True, wait=True)
        lax.fori_loop(first_free, n_gtiles, fill_wait, 0)


def _dispatch_call(tab, base_col, rt_p, n2_p, rt_s, n2_s, r_max):
    n_ptiles = n2_p.shape[0] // MOE_TD
    n_tiles = n_ptiles + n2_s.shape[0] // MOE_TD
    last_p = n_ptiles - 1
    return pl.pallas_call(
        functools.partial(_dispatch_kernel, n_tiles=n_tiles, n_ptiles=n_ptiles,
                          n_gtiles=r_max // MOE_TG),
        grid_spec=pltpu.PrefetchScalarGridSpec(
            num_scalar_prefetch=1, grid=(n_tiles,),
            in_specs=[pl.BlockSpec((8, MOE_TD), lambda t, tab: (0, jnp.minimum(t, last_p))),
                      pl.BlockSpec((MOE_TD, D), lambda t, tab: (jnp.minimum(t, last_p), 0)),
                      pl.BlockSpec((8, MOE_TD), lambda t, tab: (0, 0)),
                      pl.BlockSpec((MOE_TD, D), lambda t, tab: (0, 0)),
                      pl.BlockSpec((None, LANES, 1), lambda t, tab: (t, 0, 0))],
            out_specs=(pl.BlockSpec(memory_space=pl.ANY),
                       pl.BlockSpec((None, 8, MOE_TD), lambda t, tab: (t, 0, 0))),
            scratch_shapes=[pltpu.VMEM((2, MOE_BR, D), BF16),
                            pltpu.VMEM((MOE_TG, D), BF16),
                            pltpu.VMEM((MOE_TD, MOE_TD), BF16),
                            pltpu.SemaphoreType.DMA((2, NE)),
                            pltpu.SemaphoreType.DMA(())]),
        out_shape=(jax.ShapeDtypeStruct((r_max, D), BF16),
                   jax.ShapeDtypeStruct((n_tiles, 8, MOE_TD), F32)),
        compiler_params=pltpu.CompilerParams(dimension_semantics=("arbitrary",),
                                             vmem_limit_bytes=VMEM_LIMIT),
        name="moe_dispatch",
    )(tab, rt_p, n2_p, rt_s, n2_s, base_col)


MOE_CK = 6


def _experts_kernel(ctab_ref, xs_ref, w1_ref, w3_ref, w2_ref, ys_ref,
                    xbuf, ybuf, zbuf, in_sem, out_sem, zsem, *, n_gtiles, nc_max):
    e = pl.program_id(0)
    c0 = ctab_ref[e]
    c1 = ctab_ref[e + 1]
    n_chunks = ctab_ref[NE]
    first_free = ctab_ref[NE + 1 + 2 * nc_max]

    def fill_copy(j):
        d = pl.multiple_of(j * MOE_TG, MOE_TG)
        return pltpu.make_async_copy(zbuf, ys_ref.at[pl.ds(d, MOE_TG)], zsem)

    def fill_start(j, carry):
        fill_copy(j).start()
        return carry

    def fill_wait(j, carry):
        fill_copy(j).wait()
        return carry

    def span(c):
        r = pl.multiple_of(ctab_ref[NE + 1 + c], MOE_TG)
        n = pl.multiple_of(ctab_ref[NE + 1 + nc_max + c] * MOE_TG, MOE_TG)
        return r, n

    def in_copy(c, slot):
        r, n = span(c)
        return pltpu.make_async_copy(xs_ref.at[pl.ds(r, n)], xbuf.at[slot, pl.ds(0, n)], in_sem.at[slot])

    def out_copy(c, slot):
        r, n = span(c)
        return pltpu.make_async_copy(ybuf.at[slot, pl.ds(0, n)], ys_ref.at[pl.ds(r, n)], out_sem.at[slot])

    @pl.when(e == 0)
    def _():
        zbuf[...] = jnp.zeros_like(zbuf)
        lax.fori_loop(first_free, n_gtiles, fill_start, 0)

        @pl.when(n_chunks > 0)
        def _():
            in_copy(0, 0).start()

    def compute(slot, rows):
        x = xbuf[slot, pl.ds(0, rows), :]
        a = jnp.dot(x, w1_ref[...].astype(BF16), preferred_element_type=F32)
        b = jnp.dot(x, w3_ref[...].astype(BF16), preferred_element_type=F32)
        hid = a * _sigmoid(a) * b
        y = jnp.dot(hid.astype(BF16), w2_ref[...].astype(BF16), preferred_element_type=F32)
        ybuf[slot, pl.ds(0, rows), :] = y.astype(BF16)

    @pl.when(c1 > c0)
    def _():
        def chunk(c, carry):
            slot = lax.rem(c, 2)

            @pl.when(c + 1 < n_chunks)
            def _():
                in_copy(c + 1, 1 - slot).start()

            in_copy(c, slot).wait()

            @pl.when(c >= 2)
            def _():
                out_copy(c - 2, slot).wait()

            k = ctab_ref[NE + 1 + nc_max + c]
            for kk in range(1, MOE_CK + 1):
                @pl.when(k == kk)
                def _():
                    compute(slot, kk * MOE_TG)

            out_copy(c, slot).start()
            return carry

        lax.fori_loop(c0, c1, chunk, 0)

    @pl.when(e == NE - 1)
    def _():
        @pl.when(n_chunks >= 2)
        def _():
            out_copy(n_chunks - 2, lax.rem(n_chunks, 2)).wait()

        @pl.when(n_chunks >= 1)
        def _():
            out_copy(n_chunks - 1, lax.rem(n_chunks - 1, 2)).wait()

        lax.fori_loop(first_free, n_gtiles, fill_wait, 0)


def _experts_call(ctab, xs, w1, w3, w2, nc_max):
    r_max = xs.shape[0]
    w_map = lambda e, ctab: (e, 0, 0)
    ring = pltpu.VMEM((2, MOE_CK * MOE_TG, D), BF16)
    return pl.pallas_call(
        functools.partial(_experts_kernel, n_gtiles=r_max // MOE_TG, nc_max=nc_max),
        grid_spec=pltpu.PrefetchScalarGridSpec(
            num_scalar_prefetch=1, grid=(NE,),
            in_specs=[pl.BlockSpec(memory_space=pl.ANY),
                      pl.BlockSpec((None, D, DE), w_map),
                      pl.BlockSpec((None, D, DE), w_map),
                      pl.BlockSpec((None, DE, D), w_map)],
            out_specs=pl.BlockSpec(memory_space=pl.ANY),
            scratch_shapes=[ring, ring,
                            pltpu.VMEM((MOE_TG, D), BF16),
                            pltpu.SemaphoreType.DMA((2,)), pltpu.SemaphoreType.DMA((2,)),
                            pltpu.SemaphoreType.DMA(())]),
        out_shape=jax.ShapeDtypeStruct((r_max, D), BF16),
        compiler_params=pltpu.CompilerParams(dimension_semantics=("arbitrary",),
                                             vmem_limit_bytes=VMEM_LIMIT),
        name="moe_experts",
    )(ctab, xs, w1, w3, w2)


def _combine_kernel(tab_ref, rt_ref, pos_ref, x1_ref, gt2_ref, gf_ref, ys_ref, y_ref, buf, sems,
                    *, n_tiles, t_off):
    i = pl.program_id(0)
    t = i + t_off
    slot = lax.rem(i, 2)

    @pl.when(i == 0)
    def _():
        buf[...] = jnp.zeros_like(buf)
        _segment_copies(tab_ref, t, n_tiles, buf, ys_ref, sems, slot, to_hbm=False, wait=False)

    @pl.when(i + 1 < pl.num_programs(0))
    def _():
        _segment_copies(tab_ref, t + 1, n_tiles, buf, ys_ref, sems, 1 - slot, to_hbm=False, wait=False)

    _segment_copies(tab_ref, t, n_tiles, buf, ys_ref, sems, slot, to_hbm=False, wait=True)

    rt = rt_ref[...]
    pos0 = pos_ref[0:1, :]
    pos1 = pos_ref[1:2, :]
    used = _tile_rows(tab_ref, t, n_tiles)

    def ungroup(r0, nrows):
        m0, m1 = _one_hot_rows(r0, nrows, pos0, pos1)
        q = (jnp.where(m0, 1.0, 0.0) + jnp.where(m1, 1.0, 0.0)).astype(BF16)
        gw = jnp.sum(jnp.where(m0, rt[2:3, :], 0.0) + jnp.where(m1, rt[3:4, :], 0.0),
                     axis=1, keepdims=True)
        yv = (buf[slot, pl.ds(r0, nrows), :].astype(F32) * gw).astype(BF16)
        return lax.dot_general(q, yv, (((0,), (0,)), ((), ())), preferred_element_type=F32)

    gt2 = gt2_ref[...]
    if gt2.shape[0] not in (1, MOE_TD):
        gt2 = jnp.concatenate([gt2] * (MOE_TD // gt2.shape[0]), axis=0)

    def finish(moe):
        xo = x1_ref[...] + gt2 * moe
        y_ref[...] = _rms(xo) * gf_ref[...]

    finish(ungroup(0, MOE_MAIN))

    @pl.when(used > MOE_MAIN)
    def _():
        moe = ungroup(0, MOE_MAIN)
        for r0 in range(MOE_MAIN, MOE_BR, MOE_BLK):
            moe = moe + ungroup(r0, MOE_BLK)
        finish(moe)


def _combine_call(tab, rt, pos, x1, gt2, gt2_spec, g_final, ys, n_tiles, t_off):
    n_out_tiles = x1.shape[0] // MOE_TD
    return pl.pallas_call(
        functools.partial(_combine_kernel, n_tiles=n_tiles, t_off=t_off),
        grid_spec=pltpu.PrefetchScalarGridSpec(
            num_scalar_prefetch=1, grid=(n_out_tiles,),
            in_specs=[pl.BlockSpec((8, MOE_TD), lambda t, tab: (0, t)),
                      pl.BlockSpec((None, 8, MOE_TD), lambda t, tab: (t + t_off, 0, 0)),
                      pl.BlockSpec((MOE_TD, D), lambda t, tab: (t, 0)),
                      gt2_spec,
                      pl.BlockSpec((1, D), lambda t, tab: (0, 0)),
                      pl.BlockSpec(memory_space=pl.ANY)],
            out_specs=pl.BlockSpec((MOE_TD, D), lambda t, tab: (t, 0)),
            scratch_shapes=[pltpu.VMEM((2, MOE_BR, D), BF16),
                            pltpu.SemaphoreType.DMA((2, NE))]),
        out_shape=jax.ShapeDtypeStruct((n_out_tiles * MOE_TD, D), F32),
        compiler_params=pltpu.CompilerParams(dimension_semantics=("arbitrary",),
                                             vmem_limit_bytes=VMEM_LIMIT),
        name="moe_combine",
    )(tab, rt, pos, x1, gt2, g_final.reshape(1, D), ys)


def _moe_plan(cnt, nc_max):
    cnt_al = (cnt + ROW_ALIGN - 1) // ROW_ALIGN * ROW_ALIGN
    seg_rows = cnt_al.sum(axis=0)
    seg_pad = (seg_rows + MOE_TG - 1) // MOE_TG * MOE_TG
    seg_start = jnp.cumsum(seg_pad) - seg_pad
    dst = seg_start[None, :] + jnp.cumsum(cnt_al, axis=0) - cnt_al
    boff = jnp.cumsum(cnt_al, axis=1) - cnt_al
    tile_end = jnp.cumsum(seg_pad // MOE_TG)
    n_active = tile_end[-1:].astype(jnp.int32)
    tab = jnp.concatenate([cnt_al.ravel(), boff.ravel(), dst.ravel(),
                           seg_start + seg_rows, seg_pad - seg_rows, n_active,
                           cnt_al.sum(axis=1)]).astype(jnp.int32)
    nt = seg_pad // MOE_TG
    nfull = nt // MOE_CK
    rem = nt % MOE_CK
    nch = nfull + (rem > 0)
    cend = jnp.cumsum(nch)
    cstart = cend - nch
    c = jnp.arange(nc_max, dtype=jnp.int32)
    ce = jnp.minimum(jnp.sum(c[:, None] >= cend[None, :], axis=1), NE - 1)
    local = c - cstart[ce]
    valid = c < cend[-1]
    ck = jnp.where(valid, jnp.where(local < nfull[ce], MOE_CK, rem[ce]), 0)
    crow = jnp.where(valid, seg_start[ce] + local * (MOE_CK * MOE_TG), 0)
    ctab = jnp.concatenate([cstart, cend[-1:], crow, ck, n_active]).astype(jnp.int32)
    base_col = jnp.pad(boff, ((0, 0), (0, LANES - NE))).astype(F32)[:, :, None]
    return tab, ctab, base_col


def kernel(x_prompt, x_sample, c_prompt, c_sample, state_ssm_re, state_ssm_im, cache_conv, w_ada, b_ada, g_norm_mix, w_in, ssm_a_re, ssm_a_im, ssm_log_dt, ssm_b_re, ssm_b_im, ssm_c_re, ssm_c_im, ssm_d, w_ssm_glu, b_ssm_glu, w_dw, b_dw, ln_conv_g, ln_conv_b, g_out_ssm, g_out_conv, w_out, g_norm_ffn, w_router_grp, b_router_grp, w_router_exp, b_router_exp, w_exp_gate, w_exp_up, w_exp_down, g_final):
    depth = w_ada.shape[0]
    assert depth == 1
    bsz, seq, _ = x_prompt.shape
    nb, nt, _ = x_sample.shape

    n_c = bsz + nb
    c_pad = -n_c % 16
    c_all = jnp.concatenate([c_prompt, c_sample, jnp.zeros((c_pad, D), F32)], axis=0)
    mod_p, mod_s = _mod_call(c_all, w_ada[0], b_ada[0], bsz, nb)
    mod_p = mod_p.reshape(bsz, 6, D)

    ab_re, ab_im, bb_re, bb_im, c_im_neg = _ssm_prep_call(
        ssm_a_re[0], ssm_a_im[0], ssm_log_dt[0], ssm_b_re[0], ssm_b_im[0], ssm_c_im[0])
    wb, wc = _block_diag_weights(bb_re, bb_im, ssm_c_re[0], c_im_neg)
    a_tok = jnp.stack([ab_re.reshape(NCHUNK, LANES), ab_im.reshape(NCHUNK, LANES)])
    a_row = jnp.stack([ab_re.reshape(NSTATE), ab_im.reshape(NSTATE)])

    w_r = jnp.concatenate([w_router_exp[0].reshape(D, NE), w_router_grp[0],
                           jnp.zeros((D, LANES - NE - NG), F32)], axis=1)
    w_r_hi = w_r.astype(BF16)
    w_r = jnp.concatenate([w_r_hi, (w_r - w_r_hi.astype(F32)).astype(BF16)], axis=1)
    b_r = jnp.concatenate([b_router_exp[0].reshape(NE), b_router_grp[0],
                           jnp.zeros((LANES - NE - NG,), F32)]).reshape(1, LANES)
    w_dw_p = jnp.concatenate([w_dw[0], jnp.zeros((1, CONV_CH), F32)], axis=0)

    row = lambda v: v.reshape(1, -1)
    common_a = (row(g_norm_mix[0]), w_in[0].astype(BF16), wb)
    common_b = (wc, row(ssm_d[0].reshape(SSM_W)), w_ssm_glu[0].astype(BF16), row(b_ssm_glu[0]),
                w_dw_p, row(b_dw[0]), row(ln_conv_g[0]), row(ln_conv_b[0]),
                row(g_out_ssm[0]), row(g_out_conv[0]), w_out[0].astype(BF16),
                row(g_norm_ffn[0]), w_r, b_r)

    n_p = bsz * seq
    n_s = nb * nt
    n_all = n_p + n_s
    assert n_s == MOE_TD and seq % MOE_TD == 0 and MOE_TD % PROMPT_TC == 0
    wts_p = common_a + (a_tok,) + common_b
    x1_p, n2_p, rt_p, cnt_p, hr_p, hi_p, cache_p = _prompt_mixer_call(x_prompt, mod_p, wts_p, PROMPT_TC)

    x_tm = jnp.transpose(x_sample, (1, 0, 2)).reshape(nt * nb, D)
    cache_tm = jnp.transpose(cache_conv[0], (1, 0, 2))
    wts_s = common_a + (a_row,) + common_b
    x1_s, n2_s, rt_s, cnt_s, hr_s, hi_s, glu_s = _sample_mixer_call(
        x_tm, mod_s, state_ssm_re[0].reshape(nb, NSTATE), state_ssm_im[0].reshape(nb, NSTATE),
        cache_tm, wts_s, nb, nt)

    n_ptiles = n_p // MOE_TD
    n_tiles = n_all // MOE_TD
    r_max = -(-(2 * n_all + n_tiles * NE * (ROW_ALIGN - 1) + NE * (MOE_TG - ROW_ALIGN)) // MOE_TG) * MOE_TG
    cnt = jnp.concatenate([cnt_p.reshape(n_ptiles, MOE_TD // PROMPT_TC, LANES).sum(axis=1), cnt_s])
    nc_max = r_max // MOE_TG // MOE_CK + NE
    tab, ctab, base_col = _moe_plan(cnt[:, :NE].astype(jnp.int32), nc_max)
    xs, pos = _dispatch_call(tab, base_col, rt_p, n2_p, rt_s, n2_s, r_max)
    ys = _experts_call(ctab, xs, w_exp_gate[0], w_exp_up[0], w_exp_down[0], nc_max)
    tiles_per_b = seq // MOE_TD
    gt2_p = mod_p[:, 5:6, :]
    y_p = _combine_call(tab, rt_p, pos, x1_p, gt2_p,
                        pl.BlockSpec((None, 1, D), lambda t, tab: (t // tiles_per_b, 0, 0)),
                        g_final, ys, n_tiles, 0)
    y_s = _combine_call(tab, rt_s, pos, x1_s, mod_s,
                        pl.BlockSpec((nb, D), lambda t, tab: (0, 5)),
                        g_final, ys, n_tiles, n_ptiles)

    y_prompt = y_p.reshape(bsz, seq, D)
    y_sample = jnp.transpose(y_s.reshape(nt, nb, D), (1, 0, 2))
    new_cache_s = jnp.concatenate(
        [cache_conv[0][:, nt:, :], jnp.transpose(glu_s.reshape(nt, nb, CONV_CH), (1, 0, 2))], axis=1)
    return (y_prompt, y_sample,
            hr_p.reshape(1, bsz, G, P), hi_p.reshape(1, bsz, G, P), cache_p[None],
            hr_s.reshape(1, nb, G, P), hi_s.reshape(1, nb, G, P), new_cache_s[None])
```

```python
import functools

import jax
import jax.numpy as jnp
import numpy as np
from jax import lax
from jax.experimental import pallas as pl
from jax.experimental.pallas import tpu as pltpu

F32 = jnp.float32
BF16 = jnp.bfloat16

D = 1024
SSM_W = 512
CONV_CH = 512
G = 32
H = 16
P = 64
KW = 31
CB = KW - 1
NE = 32
NG = 4
EPG = 8
DE = 512
EPS = 1e-6
LANES = 128
SUBLANES = 8
NSTATE = G * P
NCHUNK = NSTATE // LANES

PROMPT_TC = 512
SCAN_GROUP = 8
V7X_VMEM_BYTES = 64 * 1024 * 1024
VMEM_LIMIT = V7X_VMEM_BYTES - 8 * 1024 * 1024


def _rms(x):
    return x * lax.rsqrt(jnp.mean(x * x, axis=-1, keepdims=True) + EPS)


def _sigmoid(x):
    return 0.5 * jnp.tanh(0.5 * x) + 0.5


def _gelu_tanh(y):
    c = np.sqrt(2.0 / np.pi).astype(np.float32)
    return y * (0.5 * (1.0 + jnp.tanh(c * (y + 0.044715 * (y * y * y)))))


def _bdot(a, b):
    return jnp.dot(a.astype(BF16), b, preferred_element_type=F32)


def _mod_kernel(c_ref, w_ref, b_ref, op_ref, os_ref):
    c = c_ref[...]
    s = c * _sigmoid(c)
    n = s.shape[0]
    s_hi = s.astype(BF16)
    lhs = jnp.concatenate([s_hi, (s - s_hi.astype(F32)).astype(BF16)], axis=0)
    w = w_ref[...]
    w_hi = w.astype(BF16)
    w_lo = (w - w_hi.astype(F32)).astype(BF16)
    p_hi = jnp.dot(lhs, w_hi, preferred_element_type=F32)
    p_lo = jnp.dot(lhs, w_lo, preferred_element_type=F32)
    res = (p_hi[:n] + p_lo[:n]) + (p_hi[n:] + p_lo[n:]) + b_ref[...]
    n_p = op_ref.shape[0]
    op_ref[...] = res[:n_p]
    os_ref[...] = res[n_p:n_p + os_ref.shape[0]]


def _mod_call(c_all, w_ada, b_ada, n_p, n_s):
    n = c_all.shape[0]
    tn = 1024
    return pl.pallas_call(
        _mod_kernel,
        grid=(6 * D // tn,),
        in_specs=[pl.BlockSpec((n, D), lambda j: (0, 0)),
                  pl.BlockSpec((D, tn), lambda j: (0, j)),
                  pl.BlockSpec((1, tn), lambda j: (0, j))],
        out_specs=(pl.BlockSpec((n_p, tn), lambda j: (0, j)),
                   pl.BlockSpec((n_s, tn), lambda j: (0, j))),
        out_shape=(jax.ShapeDtypeStruct((n_p, 6 * D), F32),
                   jax.ShapeDtypeStruct((n_s, 6 * D), F32)),
        compiler_params=pltpu.CompilerParams(dimension_semantics=("arbitrary",),
                                             vmem_limit_bytes=VMEM_LIMIT),
        name="mod",
    )(c_all, w_ada, b_ada.reshape(1, 6 * D))


def _ssm_prep_kernel(a_re, a_im, log_dt, b_re, b_im, c_im,
                     ab_re_o, ab_im_o, bb_re_o, bb_im_o, cneg_o):
    lam_re = jnp.minimum(a_re[...], -1e-4)
    lam_im = a_im[...]
    dt = jnp.exp(log_dt[...])
    mag = jnp.exp(lam_re * dt)
    ab_re = mag * jnp.cos(lam_im * dt)
    ab_im = mag * jnp.sin(lam_im * dt)
    den = lam_re * lam_re + lam_im * lam_im
    num_re = ab_re - 1.0
    coef_re = (num_re * lam_re + ab_im * lam_im) / den
    coef_im = (ab_im * lam_re - num_re * lam_im) / den
    ab_re_o[...] = ab_re
    ab_im_o[...] = ab_im
    br = b_re[...]
    bi = b_im[...]
    bb_re_o[...] = coef_re * br - coef_im * bi
    bb_im_o[...] = coef_re * bi + coef_im * br
    cneg_o[...] = -c_im[...]


def _ssm_prep_call(a_re, a_im, log_dt, b_re, b_im, c_im):
    flat = lambda v: v.reshape(1, NSTATE)
    b_hs = lambda v: jnp.transpose(v, (2, 0, 1)).reshape(H, NSTATE)
    dt_row = jnp.broadcast_to(log_dt[:, None], (G, P)).reshape(1, NSTATE)
    ab_re, ab_im, bb_re, bb_im, cneg = pl.pallas_call(
        _ssm_prep_kernel,
        out_shape=(jax.ShapeDtypeStruct((1, NSTATE), F32), jax.ShapeDtypeStruct((1, NSTATE), F32),
                   jax.ShapeDtypeStruct((H, NSTATE), F32), jax.ShapeDtypeStruct((H, NSTATE), F32),
                   jax.ShapeDtypeStruct((G * H, P), F32)),
        name="ssm_prep",
    )(flat(a_re), flat(a_im), dt_row, b_hs(b_re), b_hs(b_im), c_im.reshape(G * H, P))
    ghp = lambda v: jnp.transpose(v.reshape(H, G, P), (1, 0, 2))
    return (ab_re.reshape(G, P), ab_im.reshape(G, P), ghp(bb_re), ghp(bb_im), cneg.reshape(G, H, P))


def _block_diag_weights(bb_re, bb_im, c_re, c_im_neg):
    eye8 = jnp.eye(8, dtype=F32)
    eye4 = jnp.eye(4, dtype=F32)
    eye2 = jnp.eye(2, dtype=F32)

    def wb_part(bb):
        x = bb.reshape(4, 8, H, P)
        return jnp.einsum('qghp,gk->qghkp', x, eye8).reshape(4, 8 * H, 8 * P)

    wb = jnp.concatenate([wb_part(bb_re), wb_part(bb_im)], axis=-1).astype(BF16)

    def wc_part(c):
        x = c.reshape(4, 4, 2, H, P)
        y = jnp.einsum('qjghp,jk,gl->qjgpklh', x, eye4, eye2)
        return y.reshape(NCHUNK, 2 * P, 4 * 2 * H)

    wc = jnp.concatenate([wc_part(c_re), wc_part(c_im_neg)], axis=1).astype(BF16)
    return wb, wc


def _front(x, sc1, sh1, g_mix, w_in_ref):
    n = _rms(x) * g_mix * (1.0 + sc1) + sh1
    proj = _bdot(n, w_in_ref[...])
    u = proj[:, :SSM_W]
    glu = proj[:, SSM_W:SSM_W + CONV_CH] * _sigmoid(proj[:, SSM_W + CONV_CH:])
    return u, glu


def _ssm_out(y_lin, u, d_skip, w_glu_ref, b_glu, g_out_ssm):
    y = _gelu_tanh(y_lin + d_skip * u)
    ys = y * _sigmoid(_bdot(y, w_glu_ref[...]) + b_glu)
    return _rms(ys) * g_out_ssm


def _conv_out(conv, b_dw, ln_g, ln_b, g_out_conv):
    c = conv + b_dw
    mu = jnp.mean(c, axis=-1, keepdims=True)
    cc = c - mu
    var = jnp.mean(cc * cc, axis=-1, keepdims=True)
    ln = cc * lax.rsqrt(var + EPS) * ln_g + ln_b
    yc = ln * _sigmoid(ln)
    return _rms(yc) * g_out_conv


def _route(n2, w_r_ref, b_r):
    rows = n2.shape[0]
    n_hi = n2.astype(BF16)
    n_lo = (n2 - n_hi.astype(F32)).astype(BF16)
    parts = jnp.dot(jnp.concatenate([n_hi, n_lo], axis=0), w_r_ref[...], preferred_element_type=F32)
    lg = (parts[:rows, :LANES] + parts[:rows, LANES:]) + (parts[rows:, :LANES] + parts[rows:, LANES:]) + b_r
    lane = lax.broadcasted_iota(jnp.int32, (rows, LANES), 1).astype(F32)
    ninf = -jnp.inf
    big = 1e9
    gmask = jnp.logical_and(lane >= NE, lane < NE + NG)
    gl = jnp.where(gmask, lg, ninf)
    gmax = jnp.max(gl, axis=-1, keepdims=True)
    gsum = jnp.sum(jnp.where(gmask, jnp.exp(gl - gmax), 0.0), axis=-1, keepdims=True)
    p_top = 1.0 / gsum
    gi = jnp.min(jnp.where(gl == gmax, lane, big), axis=-1, keepdims=True) - NE
    lo = gi * EPG
    emask = jnp.logical_and(lane >= lo, lane < lo + EPG)
    el = jnp.where(emask, lg, ninf)
    m1 = jnp.max(el, axis=-1, keepdims=True)
    i1 = jnp.min(jnp.where(el == m1, lane, big), axis=-1, keepdims=True)
    el2 = jnp.where(lane == i1, ninf, el)
    m2 = jnp.max(el2, axis=-1, keepdims=True)
    i2 = jnp.min(jnp.where(el2 == m2, lane, big), axis=-1, keepdims=True)
    e2 = jnp.exp(m2 - m1)
    den = 1.0 + e2
    w1 = p_top / den
    w2 = p_top * e2 / den
    cnt = jnp.sum(jnp.where(lane == i1, 1.0, 0.0) + jnp.where(lane == i2, 1.0, 0.0),
                  axis=0, keepdims=True)
    cols = (jnp.where(lane == 0.0, i1, 0.0) + jnp.where(lane == 1.0, i2, 0.0)
            + jnp.where(lane == 2.0, w1, 0.0) + jnp.where(lane == 3.0, w2, 0.0))
    return cols.T[0:8, :], cnt


def _tail(x, ns, nc, gt1, sc2, sh2, w_out_ref, g_ffn, w_r_ref, b_r):
    merged = _bdot(jnp.concatenate([ns, nc], axis=-1), w_out_ref[...])
    x1 = x + gt1 * merged
    n2 = _rms(x1) * g_ffn * (1.0 + sc2) + sh2
    rt, cnt = _route(n2, w_r_ref, b_r)
    return x1, n2, rt, cnt


def _prompt_mixer_kernel(x_ref, mod_ref, g_mix_ref, w_in_ref, wb_ref, a_ref, wc_ref, dsk_ref,
                         w_glu_ref, b_glu_ref, w_dw_ref, b_dw_ref, ln_g_ref, ln_b_ref,
                         g_os_ref, g_oc_ref, w_out_ref, g_ffn_ref, w_r_ref, b_r_ref,
                         x1_ref, n2_ref, rt_ref, cnt_ref, hr_ref, hi_ref, cache_ref,
                         sre, sim, hst, ebuf, eshift, ncbuf, *, tc, pt):
    c = pl.program_id(1)
    nc_chunks = pl.num_programs(1)

    @pl.when(c == 0)
    def _():
        hst[...] = jnp.zeros_like(hst)
        ebuf[pl.ds(0, 32), :] = jnp.zeros((32, CONV_CH), F32)

    x = x_ref[...]
    mod = mod_ref[...]
    sh1, sc1, gt1 = mod[0:1], mod[1:2], mod[2:3]
    sh2, sc2 = mod[3:4], mod[4:5]

    u, glu = _front(x, sc1, sh1, g_mix_ref[...], w_in_ref)
    ebuf[pl.ds(32, tc), :] = glu

    for s in range(1, 8):
        eshift[s - 1, pl.ds(0, tc + 24), :] = ebuf[pl.ds(s, tc + 24), :]
    rb = 64
    convs = []
    for r0 in range(0, tc, rb):
        acc = None
        for k in range(KW):
            a8, s = (k + 2) // 8 * 8, (k + 2) % 8
            win = ebuf[pl.ds(r0 + a8, rb), :] if s == 0 else eshift[s - 1, pl.ds(r0 + a8, rb), :]
            term = w_dw_ref[k:k + 1, :] * win
            acc = term if acc is None else acc + term
        convs.append(acc)
    conv = jnp.concatenate(convs, axis=0)
    ncbuf[...] = _conv_out(conv, b_dw_ref[...], ln_g_ref[...], ln_b_ref[...], g_oc_ref[...])

    @pl.when(c == nc_chunks - 1)
    def _():
        cache_ref[...] = ebuf[pl.ds(tc + 2, CB), :]

    ebuf[pl.ds(0, 32), :] = ebuf[pl.ds(tc, 32), :]

    ub = u.astype(BF16)
    for q in range(4):
        r = jnp.dot(ub[:, q * LANES:(q + 1) * LANES], wb_ref[q], preferred_element_type=F32)
        for k in range(4):
            j = 4 * q + k
            sre[pl.ds(j * pt, tc), :] = r[:, k * LANES:(k + 1) * LANES]
            sim[pl.ds(j * pt, tc), :] = r[:, SSM_W + k * LANES:SSM_W + (k + 1) * LANES]

    ar0, ar1 = a_ref[0, 0:8, :], a_ref[0, 8:16, :]
    ai0, ai1 = a_ref[1, 0:8, :], a_ref[1, 8:16, :]

    def step(g, carry):
        hr0, hr1, hi0, hi1 = carry
        t0 = g * SCAN_GROUP
        idx = [(pl.ds(t0 + u, 8, stride=pt), pl.ds(t0 + u + 8 * pt, 8, stride=pt))
               for u in range(SCAN_GROUP)]
        bu = [(sre[i0, :], sre[i1, :], sim[i0, :], sim[i1, :]) for (i0, i1) in idx]
        hs = []
        for br0, br1, bi0, bi1 in bu:
            nr0 = ar0 * hr0 - ai0 * hi0 + br0
            ni0 = ar0 * hi0 + ai0 * hr0 + bi0
            nr1 = ar1 * hr1 - ai1 * hi1 + br1
            ni1 = ar1 * hi1 + ai1 * hr1 + bi1
            hr0, hr1, hi0, hi1 = nr0, nr1, ni0, ni1
            hs.append((nr0, nr1, ni0, ni1))
        for (i0, i1), (nr0, nr1, ni0, ni1) in zip(idx, hs):
            sre[i0, :] = nr0
            sim[i0, :] = ni0
            sre[i1, :] = nr1
            sim[i1, :] = ni1
        return hr0, hr1, hi0, hi1

    init = (hst[0, 0:8, :], hst[0, 8:16, :], hst[1, 0:8, :], hst[1, 8:16, :])
    hr0, hr1, hi0, hi1 = lax.fori_loop(0, tc // SCAN_GROUP, step, init)
    hst[0, 0:8, :] = hr0
    hst[0, 8:16, :] = hr1
    hst[1, 0:8, :] = hi0
    hst[1, 8:16, :] = hi1

    @pl.when(c == nc_chunks - 1)
    def _():
        hr_ref[...] = hst[0]
        hi_ref[...] = hst[1]

    ys = []
    for q in range(4):
        acc = None
        for jj in range(4):
            j = 4 * q + jj
            lhs = jnp.concatenate([sre[pl.ds(j * pt, tc), :], sim[pl.ds(j * pt, tc), :]], axis=-1)
            d = jnp.dot(lhs.astype(BF16), wc_ref[j], preferred_element_type=F32)
            acc = d if acc is None else acc + d
        ys.append(acc)
    y_lin = jnp.concatenate(ys, axis=-1)
    ns = _ssm_out(y_lin, u, dsk_ref[...], w_glu_ref, b_glu_ref[...], g_os_ref[...])
    nc = ncbuf[...]

    x1, n2, rt, cnt = _tail(x, ns, nc, gt1, sc2, sh2, w_out_ref, g_ffn_ref[...], w_r_ref, b_r_ref[...])
    x1_ref[...] = x1
    n2_ref[...] = n2.astype(BF16)
    rt_ref[...] = rt
    cnt_ref[...] = cnt


def _const_spec(shape):
    nd = len(shape)
    return pl.BlockSpec(shape, lambda b, c: (0,) * nd)


def _prompt_mixer_call(x, mod6, wts, tc):
    bsz, t, _ = x.shape
    n_all = bsz * t
    pt = tc + SUBLANES
    assert (pt // SUBLANES) % 2 == 1
    nc = t // tc
    kern = functools.partial(_prompt_mixer_kernel, tc=tc, pt=pt)
    in_specs = [pl.BlockSpec((None, tc, D), lambda b, c: (b, c, 0)),
                pl.BlockSpec((None, 6, D), lambda b, c: (b, 0, 0))]
    in_specs += [_const_spec(w.shape) for w in wts]
    out_shape = (jax.ShapeDtypeStruct((n_all, D), F32),
                 jax.ShapeDtypeStruct((n_all, D), BF16),
                 jax.ShapeDtypeStruct((8, n_all), F32),
                 jax.ShapeDtypeStruct((n_all // tc, 1, LANES), F32),
                 jax.ShapeDtypeStruct((bsz, NCHUNK, LANES), F32),
                 jax.ShapeDtypeStruct((bsz, NCHUNK, LANES), F32),
                 jax.ShapeDtypeStruct((bsz, CB, CONV_CH), F32))
    out_specs = (pl.BlockSpec((tc, D), lambda b, c: (b * nc + c, 0)),
                 pl.BlockSpec((tc, D), lambda b, c: (b * nc + c, 0)),
                 pl.BlockSpec((8, tc), lambda b, c: (0, b * nc + c)),
                 pl.BlockSpec((None, 1, LANES), lambda b, c: (b * nc + c, 0, 0)),
                 pl.BlockSpec((None, NCHUNK, LANES), lambda b, c: (b, 0, 0)),
                 pl.BlockSpec((None, NCHUNK, LANES), lambda b, c: (b, 0, 0)),
                 pl.BlockSpec((None, CB, CONV_CH), lambda b, c: (b, 0, 0)))
    scratch = [pltpu.VMEM((NCHUNK * pt, LANES), F32),
               pltpu.VMEM((NCHUNK * pt, LANES), F32),
               pltpu.VMEM((2, NCHUNK, LANES), F32),
               pltpu.VMEM((tc + 32, CONV_CH), F32),
               pltpu.VMEM((7, tc + 32, CONV_CH), F32),
               pltpu.VMEM((tc, CONV_CH), F32)]
    return pl.pallas_call(
        kern, grid=(bsz, nc), in_specs=in_specs, out_specs=out_specs, out_shape=out_shape,
        scratch_shapes=scratch,
        compiler_params=pltpu.CompilerParams(dimension_semantics=("arbitrary", "arbitrary"),
                                             vmem_limit_bytes=VMEM_LIMIT),
        name="prompt_mixer",
    )(x, mod6, *wts)


def _sample_mixer_kernel(x_ref, mod_ref, h0r_ref, h0i_ref, cache_ref,
                         g_mix_ref, w_in_ref, wb_ref, a_ref, wc_ref, dsk_ref,
                         w_glu_ref, b_glu_ref, w_dw_ref, b_dw_ref, ln_g_ref, ln_b_ref,
                         g_os_ref, g_oc_ref, w_out_ref, g_ffn_ref, w_r_ref, b_r_ref,
                         x1_ref, n2_ref, rt_ref, cnt_ref, hr_ref, hi_ref, glu_ref,
                         sre, sim, *, nb, nt):
    x = x_ref[...]

    def rows(i):
        m = mod_ref[:, i * D:(i + 1) * D]
        return jnp.concatenate([m] * nt, axis=0)

    sh1, sc1, gt1, sh2, sc2 = rows(0), rows(1), rows(2), rows(3), rows(4)
    u, glu = _front(x, sc1, sh1, g_mix_ref[...], w_in_ref)
    glu_ref[...] = glu

    ub = u.astype(BF16)
    for q in range(4):
        r = jnp.dot(ub[:, q * LANES:(q + 1) * LANES], wb_ref[q], preferred_element_type=F32)
        sre[:, q * SSM_W:(q + 1) * SSM_W] = r[:, :SSM_W]
        sim[:, q * SSM_W:(q + 1) * SSM_W] = r[:, SSM_W:]

    ar = a_ref[0:1, :]
    ai = a_ref[1:2, :]
    hr = h0r_ref[...]
    hi = h0i_ref[...]
    for t in range(nt):
        rs = pl.ds(t * nb, nb)
        nr = ar * hr - ai * hi + sre[rs, :]
        ni = ar * hi + ai * hr + sim[rs, :]
        sre[rs, :] = nr
        sim[rs, :] = ni
        hr, hi = nr, ni
    hr_ref[...] = hr
    hi_ref[...] = hi

    ys = []
    for q in range(4):
        acc = None
        for jj in range(4):
            j = 4 * q + jj
            lhs = jnp.concatenate([sre[:, j * LANES:(j + 1) * LANES],
                                   sim[:, j * LANES:(j + 1) * LANES]], axis=-1)
            d = jnp.dot(lhs.astype(BF16), wc_ref[j], preferred_element_type=F32)
            acc = d if acc is None else acc + d
        ys.append(acc)
    y_lin = jnp.concatenate(ys, axis=-1)
    ns = _ssm_out(y_lin, u, dsk_ref[...], w_glu_ref, b_glu_ref[...], g_os_ref[...])

    def ext(jrow):
        if jrow < CB:
            return cache_ref[jrow]
        return glu[(jrow - CB) * nb:(jrow - CB + 1) * nb, :]

    convs = []
    for t in range(nt):
        acc = None
        for k in range(KW):
            term = w_dw_ref[k:k + 1, :] * ext(t + k)
            acc = term if acc is None else acc + term
        convs.append(acc)
    conv = jnp.concatenate(convs, axis=0)
    nc = _conv_out(conv, b_dw_ref[...], ln_g_ref[...], ln_b_ref[...], g_oc_ref[...])

    x1, n2, rt, cnt = _tail(x, ns, nc, gt1, sc2, sh2, w_out_ref, g_ffn_ref[...], w_r_ref, b_r_ref[...])
    x1_ref[...] = x1
    n2_ref[...] = n2.astype(BF16)
    rt_ref[...] = rt
    cnt_ref[...] = cnt


def _sample_mixer_call(x_tm, mod_s, h0r, h0i, cache_tm, wts, nb, nt):
    n = nb * nt
    kern = functools.partial(_sample_mixer_kernel, nb=nb, nt=nt)
    out_shape = (jax.ShapeDtypeStruct((n, D), F32),
                 jax.ShapeDtypeStruct((n, D), BF16),
                 jax.ShapeDtypeStruct((8, n), F32),
                 jax.ShapeDtypeStruct((1, LANES), F32),
                 jax.ShapeDtypeStruct((nb, NSTATE), F32),
                 jax.ShapeDtypeStruct((nb, NSTATE), F32),
                 jax.ShapeDtypeStruct((n, CONV_CH), F32))
    scratch = [pltpu.VMEM((n, NSTATE), F32), pltpu.VMEM((n, NSTATE), F32)]
    return pl.pallas_call(
        kern, out_shape=out_shape, scratch_shapes=scratch,
        compiler_params=pltpu.CompilerParams(vmem_limit_bytes=VMEM_LIMIT),
        name="sample_mixer",
    )(x_tm, mod_s, h0r, h0i, cache_tm, *wts)


ROW_ALIGN = 16
MOE_TD = 512
MOE_BR = MOE_TD * 2 + NE * ROW_ALIGN
MOE_TG = 256


def _slot_positions(rt, base, before):
    t = rt.shape[1]
    e0 = rt[0:1, :]
    e1 = rt[1:2, :]
    sub = lax.broadcasted_iota(jnp.int32, (LANES, t), 0).astype(F32)
    a0 = jnp.where(sub == e0, 1.0, 0.0)
    a1 = jnp.where(sub == e1, 1.0, 0.0)
    at = a0 + a1
    rank = jnp.dot(at.astype(BF16), before, preferred_element_type=F32)
    slot = rank + base
    pos0 = jnp.sum(a0 * slot, axis=0, keepdims=True)
    pos1 = jnp.sum(a1 * slot, axis=0, keepdims=True)
    return pos0, pos1


def _segment_copies(tab_ref, t, n_tiles, buf, hbm, sems, slot, to_hbm, wait):
    for e in range(NE):
        n = pl.multiple_of(tab_ref[t * NE + e], ROW_ALIGN)
        b = pl.multiple_of(tab_ref[(n_tiles + t) * NE + e], ROW_ALIGN)
        d = pl.multiple_of(tab_ref[(2 * n_tiles + t) * NE + e], ROW_ALIGN)
        vm = buf.at[slot, pl.ds(b, n)]
        hb = hbm.at[pl.ds(d, n)]
        cp = pltpu.make_async_copy(vm, hb, sems.at[slot, e]) if to_hbm else \
            pltpu.make_async_copy(hb, vm, sems.at[slot, e])

        @pl.when(n > 0)
        def _():
            if wait:
                cp.wait()
            else:
                cp.start(priority=e % 2)


def _tile_rows(tab_ref, t, n_tiles):
    return tab_ref[3 * n_tiles * NE + 2 * NE + 1 + t]


def _one_hot_rows(r0, nrows, pos0, pos1):
    row = (lax.broadcasted_iota(jnp.int32, (nrows, MOE_TD), 0) + r0).astype(F32)
    return row == pos0, row == pos1


MOE_BLK = 128
MOE_MAIN = 2 * MOE_TD + 3 * MOE_BLK


def _dispatch_kernel(tab_ref, rtp_ref, n2p_ref, rts_ref, n2s_ref, base_ref, xs_ref, pos_ref, buf, zbuf, before,
                     sems, zsem,
                     *, n_tiles, n_ptiles, n_gtiles):
    t = pl.program_id(0)
    slot = lax.rem(t, 2)
    first_free = tab_ref[3 * n_tiles * NE + 2 * NE]

    def fill_copy(j):
        d = pl.multiple_of(j * MOE_TG, MOE_TG)
        return pltpu.make_async_copy(zbuf, xs_ref.at[pl.ds(d, MOE_TG)], zsem)

    def fill_start(j, carry):
        fill_copy(j).start()
        return carry

    def fill_wait(j, carry):
        fill_copy(j).wait()
        return carry

    @pl.when(t == 0)
    def _():
        zbuf[...] = jnp.zeros_like(zbuf)
        r = lax.broadcasted_iota(jnp.int32, (MOE_TD, MOE_TD), 0)
        c = lax.broadcasted_iota(jnp.int32, (MOE_TD, MOE_TD), 1)
        before[...] = jnp.where(r < c, 1.0, 0.0).astype(BF16)
        for phase in range(2):
            for e in range(NE):
                d = pl.multiple_of(tab_ref[3 * n_tiles * NE + e], ROW_ALIGN)
                n = pl.multiple_of(tab_ref[3 * n_tiles * NE + NE + e], ROW_ALIGN)
                cp = pltpu.make_async_copy(zbuf.at[pl.ds(0, n)], xs_ref.at[pl.ds(d, n)], sems.at[1, e])

                @pl.when(n > 0)
                def _():
                    if phase == 0:
                        cp.start()
                    else:
                        cp.wait()

        lax.fori_loop(first_free, n_gtiles, fill_start, 0)

    @pl.when(t >= 2)
    def _():
        _segment_copies(tab_ref, t - 2, n_tiles, buf, xs_ref, sems, slot, to_hbm=True, wait=True)

    is_sample = t >= n_ptiles
    rt = jnp.where(is_sample, rts_ref[...], rtp_ref[...])
    n2 = jnp.where(is_sample, n2s_ref[...], n2p_ref[...])
    pos0, pos1 = _slot_positions(rt, base_ref[...], before[...])
    pos_ref[...] = jnp.concatenate([pos0, pos1, jnp.zeros((6, MOE_TD), F32)], axis=0)
    used = _tile_rows(tab_ref, t, n_tiles)

    def group(r0, nrows):
        m0, m1 = _one_hot_rows(r0, nrows, pos0, pos1)
        q = (jnp.where(m0, 1.0, 0.0) + jnp.where(m1, 1.0, 0.0)).astype(BF16)
        buf[slot, pl.ds(r0, nrows), :] = jnp.dot(q, n2, preferred_element_type=F32).astype(BF16)

    group(0, MOE_MAIN)
    for r0 in range(MOE_MAIN, MOE_BR, MOE_BLK):
        @pl.when(used > r0)
        def _():
            group(r0, MOE_BLK)

    _segment_copies(tab_ref, t, n_tiles, buf, xs_ref, sems, slot, to_hbm=True, wait=False)

    @pl.when(t == n_tiles - 1)
    def _():
        if n_tiles >= 2:
            _segment_copies(tab_ref, t - 1, n_tiles, buf, xs_ref, sems, 1 - slot, to_hbm=True, wait=True)
        _segment_copies(tab_ref, t, n_tiles, buf, xs_ref, sems, slot, to_hbm=True, wait=True)
        lax.fori_loop(first_free, n_gtiles, fill_wait, 0)


def _dispatch_call(tab, base_col, rt_p, n2_p, rt_s, n2_s, r_max):
    n_ptiles = n2_p.shape[0] // MOE_TD
    n_tiles = n_ptiles + n2_s.shape[0] // MOE_TD
    last_p = n_ptiles - 1
    return pl.pallas_call(
        functools.partial(_dispatch_kernel, n_tiles=n_tiles, n_ptiles=n_ptiles,
                          n_gtiles=r_max // MOE_TG),
        grid_spec=pltpu.PrefetchScalarGridSpec(
            num_scalar_prefetch=1, grid=(n_tiles,),
            in_specs=[pl.BlockSpec((8, MOE_TD), lambda t, tab: (0, jnp.minimum(t, last_p))),
                      pl.BlockSpec((MOE_TD, D), lambda t, tab: (jnp.minimum(t, last_p), 0)),
                      pl.BlockSpec((8, MOE_TD), lambda t, tab: (0, 0)),
                      pl.BlockSpec((MOE_TD, D), lambda t, tab: (0, 0)),
                      pl.BlockSpec((None, LANES, 1), lambda t, tab: (t, 0, 0))],
            out_specs=(pl.BlockSpec(memory_space=pl.ANY),
                       pl.BlockSpec((None, 8, MOE_TD), lambda t, tab: (t, 0, 0))),
            scratch_shapes=[pltpu.VMEM((2, MOE_BR, D), BF16),
                            pltpu.VMEM((MOE_TG, D), BF16),
                            pltpu.VMEM((MOE_TD, MOE_TD), BF16),
                            pltpu.SemaphoreType.DMA((2, NE)),
                            pltpu.SemaphoreType.DMA(())]),
        out_shape=(jax.ShapeDtypeStruct((r_max, D), BF16),
                   jax.ShapeDtypeStruct((n_tiles, 8, MOE_TD), F32)),
        compiler_params=pltpu.CompilerParams(dimension_semantics=("arbitrary",),
                                             vmem_limit_bytes=VMEM_LIMIT),
        name="moe_dispatch",
    )(tab, rt_p, n2_p, rt_s, n2_s, base_col)


MOE_CK = 6


def _experts_kernel(ctab_ref, xs_ref, w1_ref, w3_ref, w2_ref, ys_ref,
                    xbuf, ybuf, zbuf, in_sem, out_sem, zsem, *, n_gtiles, nc_max):
    e = pl.program_id(0)
    c0 = ctab_ref[e]
    c1 = ctab_ref[e + 1]
    n_chunks = ctab_ref[NE]
    first_free = ctab_ref[NE + 1 + 2 * nc_max]

    def fill_copy(j):
        d = pl.multiple_of(j * MOE_TG, MOE_TG)
        return pltpu.make_async_copy(zbuf, ys_ref.at[pl.ds(d, MOE_TG)], zsem)

    def fill_start(j, carry):
        fill_copy(j).start()
        return carry

    def fill_wait(j, carry):
        fill_copy(j).wait()
        return carry

    def span(c):
        r = pl.multiple_of(ctab_ref[NE + 1 + c], MOE_TG)
        n = pl.multiple_of(ctab_ref[NE + 1 + nc_max + c] * MOE_TG, MOE_TG)
        return r, n

    def in_copy(c, slot):
        r, n = span(c)
        return pltpu.make_async_copy(xs_ref.at[pl.ds(r, n)], xbuf.at[slot, pl.ds(0, n)], in_sem.at[slot])

    def out_copy(c, slot):
        r, n = span(c)
        return pltpu.make_async_copy(ybuf.at[slot, pl.ds(0, n)], ys_ref.at[pl.ds(r, n)], out_sem.at[slot])

    @pl.when(e == 0)
    def _():
        zbuf[...] = jnp.zeros_like(zbuf)
        lax.fori_loop(first_free, n_gtiles, fill_start, 0)

        @pl.when(n_chunks > 0)
        def _():
            in_copy(0, 0).start()

    def compute(slot, rows):
        x = xbuf[slot, pl.ds(0, rows), :]
        a = jnp.dot(x, w1_ref[...].astype(BF16), preferred_element_type=F32)
        b = jnp.dot(x, w3_ref[...].astype(BF16), preferred_element_type=F32)
        hid = a * _sigmoid(a) * b
        y = jnp.dot(hid.astype(BF16), w2_ref[...].astype(BF16), preferred_element_type=F32)
        ybuf[slot, pl.ds(0, rows), :] = y.astype(BF16)

    @pl.when(c1 > c0)
    def _():
        def chunk(c, carry):
            slot = lax.rem(c, 2)

            @pl.when(c + 1 < n_chunks)
            def _():
                in_copy(c + 1, 1 - slot).start()

            in_copy(c, slot).wait()

            @pl.when(c >= 2)
            def _():
                out_copy(c - 2, slot).wait()

            k = ctab_ref[NE + 1 + nc_max + c]
            for kk in range(1, MOE_CK + 1):
                @pl.when(k == kk)
                def _():
                    compute(slot, kk * MOE_TG)

            out_copy(c, slot).start(priority=1)
            return carry

        lax.fori_loop(c0, c1, chunk, 0)

    @pl.when(e == NE - 1)
    def _():
        @pl.when(n_chunks >= 2)
        def _():
            out_copy(n_chunks - 2, lax.rem(n_chunks, 2)).wait()

        @pl.when(n_chunks >= 1)
        def _():
            out_copy(n_chunks - 1, lax.rem(n_chunks - 1, 2)).wait()

        lax.fori_loop(first_free, n_gtiles, fill_wait, 0)


def _experts_call(ctab, xs, w1, w3, w2, nc_max):
    r_max = xs.shape[0]
    w_map = lambda e, ctab: (e, 0, 0)
    ring = pltpu.VMEM((2, MOE_CK * MOE_TG, D), BF16)
    return pl.pallas_call(
        functools.partial(_experts_kernel, n_gtiles=r_max // MOE_TG, nc_max=nc_max),
        grid_spec=pltpu.PrefetchScalarGridSpec(
            num_scalar_prefetch=1, grid=(NE,),
            in_specs=[pl.BlockSpec(memory_space=pl.ANY),
                      pl.BlockSpec((None, D, DE), w_map),
                      pl.BlockSpec((None, D, DE), w_map),
                      pl.BlockSpec((None, DE, D), w_map)],
            out_specs=pl.BlockSpec(memory_space=pl.ANY),
            scratch_shapes=[ring, ring,
                            pltpu.VMEM((MOE_TG, D), BF16),
                            pltpu.SemaphoreType.DMA((2,)), pltpu.SemaphoreType.DMA((2,)),
                            pltpu.SemaphoreType.DMA(())]),
        out_shape=jax.ShapeDtypeStruct((r_max, D), BF16),
        compiler_params=pltpu.CompilerParams(dimension_semantics=("arbitrary",),
                                             vmem_limit_bytes=VMEM_LIMIT),
        name="moe_experts",
    )(ctab, xs, w1, w3, w2)


def _combine_kernel(tab_ref, rt_ref, pos_ref, x1_ref, gt2_ref, gf_ref, ys_ref, y_ref, buf, sems,
                    *, n_tiles, t_off):
    i = pl.program_id(0)
    t = i + t_off
    slot = lax.rem(i, 2)

    @pl.when(i == 0)
    def _():
        buf[...] = jnp.zeros_like(buf)
        _segment_copies(tab_ref, t, n_tiles, buf, ys_ref, sems, slot, to_hbm=False, wait=False)

    @pl.when(i + 1 < pl.num_programs(0))
    def _():
        _segment_copies(tab_ref, t + 1, n_tiles, buf, ys_ref, sems, 1 - slot, to_hbm=False, wait=False)

    _segment_copies(tab_ref, t, n_tiles, buf, ys_ref, sems, slot, to_hbm=False, wait=True)

    rt = rt_ref[...]
    pos0 = pos_ref[0:1, :]
    pos1 = pos_ref[1:2, :]
    used = _tile_rows(tab_ref, t, n_tiles)

    def ungroup(r0, nrows):
        m0, m1 = _one_hot_rows(r0, nrows, pos0, pos1)
        q = (jnp.where(m0, 1.0, 0.0) + jnp.where(m1, 1.0, 0.0)).astype(BF16)
        gw = jnp.sum(jnp.where(m0, rt[2:3, :], 0.0) + jnp.where(m1, rt[3:4, :], 0.0),
                     axis=1, keepdims=True)
        yv = (buf[slot, pl.ds(r0, nrows), :].astype(F32) * gw).astype(BF16)
        return lax.dot_general(q, yv, (((0,), (0,)), ((), ())), preferred_element_type=F32)

    gt2 = gt2_ref[...]
    if gt2.shape[0] not in (1, MOE_TD):
        gt2 = jnp.concatenate([gt2] * (MOE_TD // gt2.shape[0]), axis=0)

    def finish(moe):
        xo = x1_ref[...] + gt2 * moe
        y_ref[...] = _rms(xo) * gf_ref[...]

    finish(ungroup(0, MOE_MAIN))

    @pl.when(used > MOE_MAIN)
    def _():
        moe = ungroup(0, MOE_MAIN)
        for r0 in range(MOE_MAIN, MOE_BR, MOE_BLK):
            moe = moe + ungroup(r0, MOE_BLK)
        finish(moe)


def _combine_call(tab, rt, pos, x1, gt2, gt2_spec, g_final, ys, n_tiles, t_off):
    n_out_tiles = x1.shape[0] // MOE_TD
    return pl.pallas_call(
        functools.partial(_combine_kernel, n_tiles=n_tiles, t_off=t_off),
        grid_spec=pltpu.PrefetchScalarGridSpec(
            num_scalar_prefetch=1, grid=(n_out_tiles,),
            in_specs=[pl.BlockSpec((8, MOE_TD), lambda t, tab: (0, t)),
                      pl.BlockSpec((None, 8, MOE_TD), lambda t, tab: (t + t_off, 0, 0)),
                      pl.BlockSpec((MOE_TD, D), lambda t, tab: (t, 0)),
                      gt2_spec,
                      pl.BlockSpec((1, D), lambda t, tab: (0, 0)),
                      pl.BlockSpec(memory_space=pl.ANY)],
            out_specs=pl.BlockSpec((MOE_TD, D), lambda t, tab: (t, 0)),
            scratch_shapes=[pltpu.VMEM((2, MOE_BR, D), BF16),
                            pltpu.SemaphoreType.DMA((2, NE))]),
        out_shape=jax.ShapeDtypeStruct((n_out_tiles * MOE_TD, D), F32),
        compiler_params=pltpu.CompilerParams(dimension_semantics=("arbitrary",),
                                             vmem_limit_bytes=VMEM_LIMIT),
        name="moe_combine",
    )(tab, rt, pos, x1, gt2, g_final.reshape(1, D), ys)


def _moe_plan(cnt, nc_max):
    cnt_al = (cnt + ROW_ALIGN - 1) // ROW_ALIGN * ROW_ALIGN
    seg_rows = cnt_al.sum(axis=0)
    seg_pad = (seg_rows + MOE_TG - 1) // MOE_TG * MOE_TG
    seg_start = jnp.cumsum(seg_pad) - seg_pad
    dst = seg_start[None, :] + jnp.cumsum(cnt_al, axis=0) - cnt_al
    boff = jnp.cumsum(cnt_al, axis=1) - cnt_al
    tile_end = jnp.cumsum(seg_pad // MOE_TG)
    n_active = tile_end[-1:].astype(jnp.int32)
    tab = jnp.concatenate([cnt_al.ravel(), boff.ravel(), dst.ravel(),
                           seg_start + seg_rows, seg_pad - seg_rows, n_active,
                           cnt_al.sum(axis=1)]).astype(jnp.int32)
    nt = seg_pad // MOE_TG
    nfull = nt // MOE_CK
    rem = nt % MOE_CK
    nch = nfull + (rem > 0)
    cend = jnp.cumsum(nch)
    cstart = cend - nch
    c = jnp.arange(nc_max, dtype=jnp.int32)
    ce = jnp.minimum(jnp.sum(c[:, None] >= cend[None, :], axis=1), NE - 1)
    local = c - cstart[ce]
    valid = c < cend[-1]
    ck = jnp.where(valid, jnp.where(local < nfull[ce], MOE_CK, rem[ce]), 0)
    crow = jnp.where(valid, seg_start[ce] + local * (MOE_CK * MOE_TG), 0)
    ctab = jnp.concatenate([cstart, cend[-1:], crow, ck, n_active]).astype(jnp.int32)
    base_col = jnp.pad(boff, ((0, 0), (0, LANES - NE))).astype(F32)[:, :, None]
    return tab, ctab, base_col


def kernel(x_prompt, x_sample, c_prompt, c_sample, state_ssm_re, state_ssm_im, cache_conv, w_ada, b_ada, g_norm_mix, w_in, ssm_a_re, ssm_a_im, ssm_log_dt, ssm_b_re, ssm_b_im, ssm_c_re, ssm_c_im, ssm_d, w_ssm_glu, b_ssm_glu, w_dw, b_dw, ln_conv_g, ln_conv_b, g_out_ssm, g_out_conv, w_out, g_norm_ffn, w_router_grp, b_router_grp, w_router_exp, b_router_exp, w_exp_gate, w_exp_up, w_exp_down, g_final):
    depth = w_ada.shape[0]
    assert depth == 1
    bsz, seq, _ = x_prompt.shape
    nb, nt, _ = x_sample.shape

    n_c = bsz + nb
    c_pad = -n_c % 16
    c_all = jnp.concatenate([c_prompt, c_sample, jnp.zeros((c_pad, D), F32)], axis=0)
    mod_p, mod_s = _mod_call(c_all, w_ada[0], b_ada[0], bsz, nb)
    mod_p = mod_p.reshape(bsz, 6, D)

    ab_re, ab_im, bb_re, bb_im, c_im_neg = _ssm_prep_call(
        ssm_a_re[0], ssm_a_im[0], ssm_log_dt[0], ssm_b_re[0], ssm_b_im[0], ssm_c_im[0])
    wb, wc = _block_diag_weights(bb_re, bb_im, ssm_c_re[0], c_im_neg)
    a_tok = jnp.stack([ab_re.reshape(NCHUNK, LANES), ab_im.reshape(NCHUNK, LANES)])
    a_row = jnp.stack([ab_re.reshape(NSTATE), ab_im.reshape(NSTATE)])

    w_r = jnp.concatenate([w_router_exp[0].reshape(D, NE), w_router_grp[0],
                           jnp.zeros((D, LANES - NE - NG), F32)], axis=1)
    w_r_hi = w_r.astype(BF16)
    w_r = jnp.concatenate([w_r_hi, (w_r - w_r_hi.astype(F32)).astype(BF16)], axis=1)
    b_r = jnp.concatenate([b_router_exp[0].reshape(NE), b_router_grp[0],
                           jnp.zeros((LANES - NE - NG,), F32)]).reshape(1, LANES)
    w_dw_p = jnp.concatenate([w_dw[0], jnp.zeros((1, CONV_CH), F32)], axis=0)

    row = lambda v: v.reshape(1, -1)
    common_a = (row(g_norm_mix[0]), w_in[0].astype(BF16), wb)
    common_b = (wc, row(ssm_d[0].reshape(SSM_W)), w_ssm_glu[0].astype(BF16), row(b_ssm_glu[0]),
                w_dw_p, row(b_dw[0]), row(ln_conv_g[0]), row(ln_conv_b[0]),
                row(g_out_ssm[0]), row(g_out_conv[0]), w_out[0].astype(BF16),
                row(g_norm_ffn[0]), w_r, b_r)

    n_p = bsz * seq
    n_s = nb * nt
    n_all = n_p + n_s
    assert n_s == MOE_TD and seq % MOE_TD == 0 and MOE_TD % PROMPT_TC == 0
    wts_p = common_a + (a_tok,) + common_b
    x1_p, n2_p, rt_p, cnt_p, hr_p, hi_p, cache_p = _prompt_mixer_call(x_prompt, mod_p, wts_p, PROMPT_TC)

    x_tm = jnp.transpose(x_sample, (1, 0, 2)).reshape(nt * nb, D)
    cache_tm = jnp.transpose(cache_conv[0], (1, 0, 2))
    wts_s = common_a + (a_row,) + common_b
    x1_s, n2_s, rt_s, cnt_s, hr_s, hi_s, glu_s = _sample_mixer_call(
        x_tm, mod_s, state_ssm_re[0].reshape(nb, NSTATE), state_ssm_im[0].reshape(nb, NSTATE),
        cache_tm, wts_s, nb, nt)

    n_ptiles = n_p // MOE_TD
    n_tiles = n_all // MOE_TD
    r_max = -(-(2 * n_all + n_tiles * NE * (ROW_ALIGN - 1) + NE * (MOE_TG - ROW_ALIGN)) // MOE_TG) * MOE_TG
    cnt = jnp.concatenate([cnt_p.reshape(n_ptiles, MOE_TD // PROMPT_TC, LANES).sum(axis=1), cnt_s])
    nc_max = r_max // MOE_TG // MOE_CK + NE
    tab, ctab, base_col = _moe_plan(cnt[:, :NE].astype(jnp.int32), nc_max)
    xs, pos = _dispatch_call(tab, base_col, rt_p, n2_p, rt_s, n2_s, r_max)
    ys = _experts_call(ctab, xs, w_exp_gate[0], w_exp_up[0], w_exp_down[0], nc_max)
    tiles_per_b = seq // MOE_TD
    gt2_p = mod_p[:, 5:6, :]
    y_p = _combine_call(tab, rt_p, pos, x1_p, gt2_p,
                        pl.BlockSpec((None, 1, D), lambda t, tab: (t // tiles_per_b, 0, 0)),
                        g_final, ys, n_tiles, 0)
    y_s = _combine_call(tab, rt_s, pos, x1_s, mod_s,
                        pl.BlockSpec((nb, D), lambda t, tab: (0, 5)),
                        g_final, ys, n_tiles, n_ptiles)

    y_prompt = y_p.reshape(bsz, seq, D)
    y_sample = jnp.transpose(y_s.reshape(nt, nb, D), (1, 0, 2))
    new_cache_s = jnp.concatenate(
        [cache_conv[0][:, nt:, :], jnp.transpose(glu_s.reshape(nt, nb, CONV_CH), (1, 0, 2))], axis=1)
    return (y_prompt, y_sample,
            hr_p.reshape(1, bsz, G, P), hi_p.reshape(1, bsz, G, P), cache_p[None],
            hr_s.reshape(1, nb, G, P), hi_s.reshape(1, nb, G, P), new_cache_s[None])
```

```python
import functools

import jax
import jax.numpy as jnp
import numpy as np
from jax import lax
from jax.experimental import pallas as pl
from jax.experimental.pallas import tpu as pltpu

F32 = jnp.float32
BF16 = jnp.bfloat16

D = 1024
SSM_W = 512
CONV_CH = 512
G = 32
H = 16
P = 64
KW = 31
CB = KW - 1
NE = 32
NG = 4
EPG = 8
DE = 512
EPS = 1e-6
LANES = 128
SUBLANES = 8
NSTATE = G * P
NCHUNK = NSTATE // LANES

PROMPT_TC = 512
SCAN_GROUP = 8
V7X_VMEM_BYTES = 64 * 1024 * 1024
VMEM_LIMIT = V7X_VMEM_BYTES - 8 * 1024 * 1024


def _rms(x):
    return x * lax.rsqrt(jnp.mean(x * x, axis=-1, keepdims=True) + EPS)


def _sigmoid(x):
    return 0.5 * jnp.tanh(0.5 * x) + 0.5


def _gelu_tanh(y):
    c = np.sqrt(2.0 / np.pi).astype(np.float32)
    return y * (0.5 * (1.0 + jnp.tanh(c * (y + 0.044715 * (y * y * y)))))


def _bdot(a, b):
    return jnp.dot(a.astype(BF16), b, preferred_element_type=F32)


def _mod_kernel(c_ref, w_ref, b_ref, op_ref, os_ref):
    c = c_ref[...]
    s = c * _sigmoid(c)
    n = s.shape[0]
    s_hi = s.astype(BF16)
    lhs = jnp.concatenate([s_hi, (s - s_hi.astype(F32)).astype(BF16)], axis=0)
    w = w_ref[...]
    w_hi = w.astype(BF16)
    w_lo = (w - w_hi.astype(F32)).astype(BF16)
    p_hi = jnp.dot(lhs, w_hi, preferred_element_type=F32)
    p_lo = jnp.dot(lhs, w_lo, preferred_element_type=F32)
    res = (p_hi[:n] + p_lo[:n]) + (p_hi[n:] + p_lo[n:]) + b_ref[...]
    n_p = op_ref.shape[0]
    op_ref[...] = res[:n_p]
    os_ref[...] = res[n_p:n_p + os_ref.shape[0]]


def _mod_call(c_all, w_ada, b_ada, n_p, n_s):
    n = c_all.shape[0]
    tn = 1024
    return pl.pallas_call(
        _mod_kernel,
        grid=(6 * D // tn,),
        in_specs=[pl.BlockSpec((n, D), lambda j: (0, 0)),
                  pl.BlockSpec((D, tn), lambda j: (0, j)),
                  pl.BlockSpec((1, tn), lambda j: (0, j))],
        out_specs=(pl.BlockSpec((n_p, tn), lambda j: (0, j)),
                   pl.BlockSpec((n_s, tn), lambda j: (0, j))),
        out_shape=(jax.ShapeDtypeStruct((n_p, 6 * D), F32),
                   jax.ShapeDtypeStruct((n_s, 6 * D), F32)),
        compiler_params=pltpu.CompilerParams(dimension_semantics=("arbitrary",),
                                             vmem_limit_bytes=VMEM_LIMIT),
        name="mod",
    )(c_all, w_ada, b_ada.reshape(1, 6 * D))


def _ssm_prep_kernel(a_re, a_im, log_dt, b_re, b_im, c_im,
                     ab_re_o, ab_im_o, bb_re_o, bb_im_o, cneg_o):
    lam_re = jnp.minimum(a_re[...], -1e-4)
    lam_im = a_im[...]
    dt = jnp.exp(log_dt[...])
    mag = jnp.exp(lam_re * dt)
    ab_re = mag * jnp.cos(lam_im * dt)
    ab_im = mag * jnp.sin(lam_im * dt)
    den = lam_re * lam_re + lam_im * lam_im
    num_re = ab_re - 1.0
    coef_re = (num_re * lam_re + ab_im * lam_im) / den
    coef_im = (ab_im * lam_re - num_re * lam_im) / den
    ab_re_o[...] = ab_re
    ab_im_o[...] = ab_im
    br = b_re[...]
    bi = b_im[...]
    bb_re_o[...] = coef_re * br - coef_im * bi
    bb_im_o[...] = coef_re * bi + coef_im * br
    cneg_o[...] = -c_im[...]


def _ssm_prep_call(a_re, a_im, log_dt, b_re, b_im, c_im):
    flat = lambda v: v.reshape(1, NSTATE)
    b_hs = lambda v: jnp.transpose(v, (2, 0, 1)).reshape(H, NSTATE)
    dt_row = jnp.broadcast_to(log_dt[:, None], (G, P)).reshape(1, NSTATE)
    ab_re, ab_im, bb_re, bb_im, cneg = pl.pallas_call(
        _ssm_prep_kernel,
        out_shape=(jax.ShapeDtypeStruct((1, NSTATE), F32), jax.ShapeDtypeStruct((1, NSTATE), F32),
                   jax.ShapeDtypeStruct((H, NSTATE), F32), jax.ShapeDtypeStruct((H, NSTATE), F32),
                   jax.ShapeDtypeStruct((G * H, P), F32)),
        name="ssm_prep",
    )(flat(a_re), flat(a_im), dt_row, b_hs(b_re), b_hs(b_im), c_im.reshape(G * H, P))
    ghp = lambda v: jnp.transpose(v.reshape(H, G, P), (1, 0, 2))
    return (ab_re.reshape(G, P), ab_im.reshape(G, P), ghp(bb_re), ghp(bb_im), cneg.reshape(G, H, P))


def _block_diag_weights(bb_re, bb_im, c_re, c_im_neg):
    eye8 = jnp.eye(8, dtype=F32)
    eye4 = jnp.eye(4, dtype=F32)
    eye2 = jnp.eye(2, dtype=F32)

    def wb_part(bb):
        x = bb.reshape(4, 8, H, P)
        return jnp.einsum('qghp,gk->qghkp', x, eye8).reshape(4, 8 * H, 8 * P)

    wb = jnp.concatenate([wb_part(bb_re), wb_part(bb_im)], axis=-1).astype(BF16)

    def wc_part(c):
        x = c.reshape(4, 4, 2, H, P)
        y = jnp.einsum('qjghp,jk,gl->qjgpklh', x, eye4, eye2)
        return y.reshape(NCHUNK, 2 * P, 4 * 2 * H)

    wc = jnp.concatenate([wc_part(c_re), wc_part(c_im_neg)], axis=1).astype(BF16)
    return wb, wc


def _front(x, sc1, sh1, g_mix, w_in_ref):
    n = _rms(x) * g_mix * (1.0 + sc1) + sh1
    proj = _bdot(n, w_in_ref[...])
    u = proj[:, :SSM_W]
    glu = proj[:, SSM_W:SSM_W + CONV_CH] * _sigmoid(proj[:, SSM_W + CONV_CH:])
    return u, glu


def _ssm_out(y_lin, u, d_skip, w_glu_ref, b_glu, g_out_ssm):
    y = _gelu_tanh(y_lin + d_skip * u)
    ys = y * _sigmoid(_bdot(y, w_glu_ref[...]) + b_glu)
    return _rms(ys) * g_out_ssm


def _conv_out(conv, b_dw, ln_g, ln_b, g_out_conv):
    c = conv + b_dw
    mu = jnp.mean(c, axis=-1, keepdims=True)
    cc = c - mu
    var = jnp.mean(cc * cc, axis=-1, keepdims=True)
    ln = cc * lax.rsqrt(var + EPS) * ln_g + ln_b
    yc = ln * _sigmoid(ln)
    return _rms(yc) * g_out_conv


def _route(n2, w_r_ref, b_r):
    rows = n2.shape[0]
    n_hi = n2.astype(BF16)
    n_lo = (n2 - n_hi.astype(F32)).astype(BF16)
    parts = jnp.dot(jnp.concatenate([n_hi, n_lo], axis=0), w_r_ref[...], preferred_element_type=F32)
    lg = (parts[:rows, :LANES] + parts[:rows, LANES:]) + (parts[rows:, :LANES] + parts[rows:, LANES:]) + b_r
    lane = lax.broadcasted_iota(jnp.int32, (rows, LANES), 1).astype(F32)
    ninf = -jnp.inf
    big = 1e9
    gmask = jnp.logical_and(lane >= NE, lane < NE + NG)
    gl = jnp.where(gmask, lg, ninf)
    gmax = jnp.max(gl, axis=-1, keepdims=True)
    gsum = jnp.sum(jnp.where(gmask, jnp.exp(gl - gmax), 0.0), axis=-1, keepdims=True)
    p_top = 1.0 / gsum
    gi = jnp.min(jnp.where(gl == gmax, lane, big), axis=-1, keepdims=True) - NE
    lo = gi * EPG
    emask = jnp.logical_and(lane >= lo, lane < lo + EPG)
    el = jnp.where(emask, lg, ninf)
    m1 = jnp.max(el, axis=-1, keepdims=True)
    i1 = jnp.min(jnp.where(el == m1, lane, big), axis=-1, keepdims=True)
    el2 = jnp.where(lane == i1, ninf, el)
    m2 = jnp.max(el2, axis=-1, keepdims=True)
    i2 = jnp.min(jnp.where(el2 == m2, lane, big), axis=-1, keepdims=True)
    e2 = jnp.exp(m2 - m1)
    den = 1.0 + e2
    w1 = p_top / den
    w2 = p_top * e2 / den
    cnt = jnp.sum(jnp.where(lane == i1, 1.0, 0.0) + jnp.where(lane == i2, 1.0, 0.0),
                  axis=0, keepdims=True)
    cols = (jnp.where(lane == 0.0, i1, 0.0) + jnp.where(lane == 1.0, i2, 0.0)
            + jnp.where(lane == 2.0, w1, 0.0) + jnp.where(lane == 3.0, w2, 0.0))
    return cols.T[0:8, :], cnt


def _tail(x, ns, nc, gt1, sc2, sh2, w_out_ref, g_ffn, w_r_ref, b_r):
    merged = _bdot(jnp.concatenate([ns, nc], axis=-1), w_out_ref[...])
    x1 = x + gt1 * merged
    n2 = _rms(x1) * g_ffn * (1.0 + sc2) + sh2
    rt, cnt = _route(n2, w_r_ref, b_r)
    return x1, n2, rt, cnt


def _prompt_mixer_kernel(x_ref, mod_ref, g_mix_ref, w_in_ref, wb_ref, a_ref, wc_ref, dsk_ref,
                         w_glu_ref, b_glu_ref, w_dw_ref, b_dw_ref, ln_g_ref, ln_b_ref,
                         g_os_ref, g_oc_ref, w_out_ref, g_ffn_ref, w_r_ref, b_r_ref,
                         x1_ref, n2_ref, rt_ref, cnt_ref, hr_ref, hi_ref, cache_ref,
                         sre, sim, hst, ebuf, eshift, ncbuf, *, tc, pt):
    c = pl.program_id(1)
    nc_chunks = pl.num_programs(1)

    @pl.when(c == 0)
    def _():
        hst[...] = jnp.zeros_like(hst)
        ebuf[pl.ds(0, 32), :] = jnp.zeros((32, CONV_CH), F32)

    x = x_ref[...]
    mod = mod_ref[...]
    sh1, sc1, gt1 = mod[0:1], mod[1:2], mod[2:3]
    sh2, sc2 = mod[3:4], mod[4:5]

    u, glu = _front(x, sc1, sh1, g_mix_ref[...], w_in_ref)
    ebuf[pl.ds(32, tc), :] = glu

    for s in range(1, 8):
        eshift[s - 1, pl.ds(0, tc + 24), :] = ebuf[pl.ds(s, tc + 24), :]
    rb = 64
    convs = []
    for r0 in range(0, tc, rb):
        acc = None
        for k in range(KW):
            a8, s = (k + 2) // 8 * 8, (k + 2) % 8
            win = ebuf[pl.ds(r0 + a8, rb), :] if s == 0 else eshift[s - 1, pl.ds(r0 + a8, rb), :]
            term = w_dw_ref[k:k + 1, :] * win
            acc = term if acc is None else acc + term
        convs.append(acc)
    conv = jnp.concatenate(convs, axis=0)
    ncbuf[...] = _conv_out(conv, b_dw_ref[...], ln_g_ref[...], ln_b_ref[...], g_oc_ref[...])

    @pl.when(c == nc_chunks - 1)
    def _():
        cache_ref[...] = ebuf[pl.ds(tc + 2, CB), :]

    ebuf[pl.ds(0, 32), :] = ebuf[pl.ds(tc, 32), :]

    ub = u.astype(BF16)
    for q in range(4):
        r = jnp.dot(ub[:, q * LANES:(q + 1) * LANES], wb_ref[q], preferred_element_type=F32)
        for k in range(4):
            j = 4 * q + k
            sre[pl.ds(j * pt, tc), :] = r[:, k * LANES:(k + 1) * LANES]
            sim[pl.ds(j * pt, tc), :] = r[:, SSM_W + k * LANES:SSM_W + (k + 1) * LANES]

    ar0, ar1 = a_ref[0, 0:8, :], a_ref[0, 8:16, :]
    ai0, ai1 = a_ref[1, 0:8, :], a_ref[1, 8:16, :]

    def step(g, carry):
        hr0, hr1, hi0, hi1 = carry
        t0 = g * SCAN_GROUP
        idx = [(pl.ds(t0 + u, 8, stride=pt), pl.ds(t0 + u + 8 * pt, 8, stride=pt))
               for u in range(SCAN_GROUP)]
        bu = [(sre[i0, :], sre[i1, :], sim[i0, :], sim[i1, :]) for (i0, i1) in idx]
        hs = []
        for br0, br1, bi0, bi1 in bu:
            nr0 = ar0 * hr0 - ai0 * hi0 + br0
            ni0 = ar0 * hi0 + ai0 * hr0 + bi0
            nr1 = ar1 * hr1 - ai1 * hi1 + br1
            ni1 = ar1 * hi1 + ai1 * hr1 + bi1
            hr0, hr1, hi0, hi1 = nr0, nr1, ni0, ni1
            hs.append((nr0, nr1, ni0, ni1))
        for (i0, i1), (nr0, nr1, ni0, ni1) in zip(idx, hs):
            sre[i0, :] = nr0
            sim[i0, :] = ni0
            sre[i1, :] = nr1
            sim[i1, :] = ni1
        return hr0, hr1, hi0, hi1

    init = (hst[0, 0:8, :], hst[0, 8:16, :], hst[1, 0:8, :], hst[1, 8:16, :])
    hr0, hr1, hi0, hi1 = lax.fori_loop(0, tc // SCAN_GROUP, step, init)
    hst[0, 0:8, :] = hr0
    hst[0, 8:16, :] = hr1
    hst[1, 0:8, :] = hi0
    hst[1, 8:16, :] = hi1

    @pl.when(c == nc_chunks - 1)
    def _():
        hr_ref[...] = hst[0]
        hi_ref[...] = hst[1]

    ys = []
    for q in range(4):
        acc = None
        for jj in range(4):
            j = 4 * q + jj
            lhs = jnp.concatenate([sre[pl.ds(j * pt, tc), :], sim[pl.ds(j * pt, tc), :]], axis=-1)
            d = jnp.dot(lhs.astype(BF16), wc_ref[j], preferred_element_type=F32)
            acc = d if acc is None else acc + d
        ys.append(acc)
    y_lin = jnp.concatenate(ys, axis=-1)
    ns = _ssm_out(y_lin, u, dsk_ref[...], w_glu_ref, b_glu_ref[...], g_os_ref[...])
    nc = ncbuf[...]

    x1, n2, rt, cnt = _tail(x, ns, nc, gt1, sc2, sh2, w_out_ref, g_ffn_ref[...], w_r_ref, b_r_ref[...])
    x1_ref[...] = x1
    n2_ref[...] = n2.astype(BF16)
    rt_ref[...] = rt
    cnt_ref[...] = cnt


def _const_spec(shape):
    nd = len(shape)
    return pl.BlockSpec(shape, lambda b, c: (0,) * nd)


def _prompt_mixer_call(x, mod6, wts, tc):
    bsz, t, _ = x.shape
    n_all = bsz * t
    pt = tc + SUBLANES
    assert (pt // SUBLANES) % 2 == 1
    nc = t // tc
    kern = functools.partial(_prompt_mixer_kernel, tc=tc, pt=pt)
    in_specs = [pl.BlockSpec((None, tc, D), lambda b, c: (b, c, 0)),
                pl.BlockSpec((None, 6, D), lambda b, c: (b, 0, 0))]
    in_specs += [_const_spec(w.shape) for w in wts]
    out_shape = (jax.ShapeDtypeStruct((n_all, D), F32),
                 jax.ShapeDtypeStruct((n_all, D), BF16),
                 jax.ShapeDtypeStruct((8, n_all), F32),
                 jax.ShapeDtypeStruct((n_all // tc, 1, LANES), F32),
                 jax.ShapeDtypeStruct((bsz, NCHUNK, LANES), F32),
                 jax.ShapeDtypeStruct((bsz, NCHUNK, LANES), F32),
                 jax.ShapeDtypeStruct((bsz, CB, CONV_CH), F32))
    out_specs = (pl.BlockSpec((tc, D), lambda b, c: (b * nc + c, 0)),
                 pl.BlockSpec((tc, D), lambda b, c: (b * nc + c, 0)),
                 pl.BlockSpec((8, tc), lambda b, c: (0, b * nc + c)),
                 pl.BlockSpec((None, 1, LANES), lambda b, c: (b * nc + c, 0, 0)),
                 pl.BlockSpec((None, NCHUNK, LANES), lambda b, c: (b, 0, 0)),
                 pl.BlockSpec((None, NCHUNK, LANES), lambda b, c: (b, 0, 0)),
                 pl.BlockSpec((None, CB, CONV_CH), lambda b, c: (b, 0, 0)))
    scratch = [pltpu.VMEM((NCHUNK * pt, LANES), F32),
               pltpu.VMEM((NCHUNK * pt, LANES), F32),
               pltpu.VMEM((2, NCHUNK, LANES), F32),
               pltpu.VMEM((tc + 32, CONV_CH), F32),
               pltpu.VMEM((7, tc + 32, CONV_CH), F32),
               pltpu.VMEM((tc, CONV_CH), F32)]
    return pl.pallas_call(
        kern, grid=(bsz, nc), in_specs=in_specs, out_specs=out_specs, out_shape=out_shape,
        scratch_shapes=scratch,
        compiler_params=pltpu.CompilerParams(dimension_semantics=("arbitrary", "arbitrary"),
                                             vmem_limit_bytes=VMEM_LIMIT),
        name="prompt_mixer",
    )(x, mod6, *wts)


def _sample_mixer_kernel(x_ref, mod_ref, h0r_ref, h0i_ref, cache_ref,
                         g_mix_ref, w_in_ref, wb_ref, a_ref, wc_ref, dsk_ref,
                         w_glu_ref, b_glu_ref, w_dw_ref, b_dw_ref, ln_g_ref, ln_b_ref,
                         g_os_ref, g_oc_ref, w_out_ref, g_ffn_ref, w_r_ref, b_r_ref,
                         x1_ref, n2_ref, rt_ref, cnt_ref, hr_ref, hi_ref, glu_ref,
                         sre, sim, *, nb, nt):
    x = x_ref[...]

    def rows(i):
        m = mod_ref[:, i * D:(i + 1) * D]
        return jnp.concatenate([m] * nt, axis=0)

    sh1, sc1, gt1, sh2, sc2 = rows(0), rows(1), rows(2), rows(3), rows(4)
    u, glu = _front(x, sc1, sh1, g_mix_ref[...], w_in_ref)
    glu_ref[...] = glu

    ub = u.astype(BF16)
    for q in range(4):
        r = jnp.dot(ub[:, q * LANES:(q + 1) * LANES], wb_ref[q], preferred_element_type=F32)
        sre[:, q * SSM_W:(q + 1) * SSM_W] = r[:, :SSM_W]
        sim[:, q * SSM_W:(q + 1) * SSM_W] = r[:, SSM_W:]

    ar = a_ref[0:1, :]
    ai = a_ref[1:2, :]
    hr = h0r_ref[...]
    hi = h0i_ref[...]
    for t in range(nt):
        rs = pl.ds(t * nb, nb)
        nr = ar * hr - ai * hi + sre[rs, :]
        ni = ar * hi + ai * hr + sim[rs, :]
        sre[rs, :] = nr
        sim[rs, :] = ni
        hr, hi = nr, ni
    hr_ref[...] = hr
    hi_ref[...] = hi

    ys = []
    for q in range(4):
        acc = None
        for jj in range(4):
            j = 4 * q + jj
            lhs = jnp.concatenate([sre[:, j * LANES:(j + 1) * LANES],
                                   sim[:, j * LANES:(j + 1) * LANES]], axis=-1)
            d = jnp.dot(lhs.astype(BF16), wc_ref[j], preferred_element_type=F32)
            acc = d if acc is None else acc + d
        ys.append(acc)
    y_lin = jnp.concatenate(ys, axis=-1)
    ns = _ssm_out(y_lin, u, dsk_ref[...], w_glu_ref, b_glu_ref[...], g_os_ref[...])

    def ext(jrow):
        if jrow < CB:
            return cache_ref[jrow]
        return glu[(jrow - CB) * nb:(jrow - CB + 1) * nb, :]

    convs = []
    for t in range(nt):
        acc = None
        for k in range(KW):
            term = w_dw_ref[k:k + 1, :] * ext(t + k)
            acc = term if acc is None else acc + term
        convs.append(acc)
    conv = jnp.concatenate(convs, axis=0)
    nc = _conv_out(conv, b_dw_ref[...], ln_g_ref[...], ln_b_ref[...], g_oc_ref[...])

    x1, n2, rt, cnt = _tail(x, ns, nc, gt1, sc2, sh2, w_out_ref, g_ffn_ref[...], w_r_ref, b_r_ref[...])
    x1_ref[...] = x1
    n2_ref[...] = n2.astype(BF16)
    rt_ref[...] = rt
    cnt_ref[...] = cnt


def _sample_mixer_call(x_tm, mod_s, h0r, h0i, cache_tm, wts, nb, nt):
    n = nb * nt
    kern = functools.partial(_sample_mixer_kernel, nb=nb, nt=nt)
    out_shape = (jax.ShapeDtypeStruct((n, D), F32),
                 jax.ShapeDtypeStruct((n, D), BF16),
                 jax.ShapeDtypeStruct((8, n), F32),
                 jax.ShapeDtypeStruct((1, LANES), F32),
                 jax.ShapeDtypeStruct((nb, NSTATE), F32),
                 jax.ShapeDtypeStruct((nb, NSTATE), F32),
                 jax.ShapeDtypeStruct((n, CONV_CH), F32))
    scratch = [pltpu.VMEM((n, NSTATE), F32), pltpu.VMEM((n, NSTATE), F32)]
    return pl.pallas_call(
        kern, out_shape=out_shape, scratch_shapes=scratch,
        compiler_params=pltpu.CompilerParams(vmem_limit_bytes=VMEM_LIMIT),
        name="sample_mixer",
    )(x_tm, mod_s, h0r, h0i, cache_tm, *wts)


ROW_ALIGN = 16
MOE_TD = 512
MOE_BR = MOE_TD * 2 + NE * ROW_ALIGN
MOE_TG = 256


def _slot_positions(rt, base, before):
    t = rt.shape[1]
    e0 = rt[0:1, :]
    e1 = rt[1:2, :]
    sub = lax.broadcasted_iota(jnp.int32, (LANES, t), 0).astype(F32)
    a0 = jnp.where(sub == e0, 1.0, 0.0)
    a1 = jnp.where(sub == e1, 1.0, 0.0)
    at = a0 + a1
    rank = jnp.dot(at.astype(BF16), before, preferred_element_type=F32)
    slot = rank + base
    pos0 = jnp.sum(a0 * slot, axis=0, keepdims=True)
    pos1 = jnp.sum(a1 * slot, axis=0, keepdims=True)
    return pos0, pos1


def _segment_copies(tab_ref, t, n_tiles, buf, hbm, sems, slot, to_hbm, wait):
    for e in range(NE):
        n = pl.multiple_of(tab_ref[t * NE + e], ROW_ALIGN)
        b = pl.multiple_of(tab_ref[(n_tiles + t) * NE + e], ROW_ALIGN)
        d = pl.multiple_of(tab_ref[(2 * n_tiles + t) * NE + e], ROW_ALIGN)
        vm = buf.at[slot, pl.ds(b, n)]
        hb = hbm.at[pl.ds(d, n)]
        cp = pltpu.make_async_copy(vm, hb, sems.at[slot, e]) if to_hbm else \
            pltpu.make_async_copy(hb, vm, sems.at[slot, e])

        @pl.when(n > 0)
        def _():
            if wait:
                cp.wait()
            else:
                cp.start()


def _tile_rows(tab_ref, t, n_tiles):
    return tab_ref[3 * n_tiles * NE + 2 * NE + 1 + t]


def _one_hot_rows(r0, nrows, pos0, pos1):
    row = (lax.broadcasted_iota(jnp.int32, (nrows, MOE_TD), 0) + r0).astype(F32)
    return row == pos0, row == pos1


MOE_BLK = 128
MOE_MAIN = 2 * MOE_TD + 3 * MOE_BLK


def _dispatch_kernel(tab_ref, rtp_ref, n2p_ref, rts_ref, n2s_ref, base_ref, xs_ref, pos_ref, buf, zbuf, before,
                     sems, zsem, tsem,
                     *, n_tiles, n_ptiles, n_gtiles):
    t = pl.program_id(0)
    slot = lax.rem(t, 2)
    first_free = tab_ref[3 * n_tiles * NE + 2 * NE]

    def fill_copy(j):
        d = pl.multiple_of(j * MOE_TG, MOE_TG)
        return pltpu.make_async_copy(zbuf, xs_ref.at[pl.ds(d, MOE_TG)], zsem)

    def fill_start(j, carry):
        fill_copy(j).start()
        return carry

    def fill_wait(j, carry):
        fill_copy(j).wait()
        return carry

    def segment_tails(wait):
        for e in range(NE):
            d = pl.multiple_of(tab_ref[3 * n_tiles * NE + e], ROW_ALIGN)
            n = pl.multiple_of(tab_ref[3 * n_tiles * NE + NE + e], ROW_ALIGN)
            cp = pltpu.make_async_copy(zbuf.at[pl.ds(0, n)], xs_ref.at[pl.ds(d, n)], tsem.at[e])

            @pl.when(n > 0)
            def _():
                if wait:
                    cp.wait()
                else:
                    cp.start()

    @pl.when(t == 0)
    def _():
        zbuf[...] = jnp.zeros_like(zbuf)
        r = lax.broadcasted_iota(jnp.int32, (MOE_TD, MOE_TD), 0)
        c = lax.broadcasted_iota(jnp.int32, (MOE_TD, MOE_TD), 1)
        before[...] = jnp.where(r < c, 1.0, 0.0).astype(BF16)
        segment_tails(wait=False)
        lax.fori_loop(first_free, n_gtiles, fill_start, 0)

    @pl.when(t >= 2)
    def _():
        _segment_copies(tab_ref, t - 2, n_tiles, buf, xs_ref, sems, slot, to_hbm=True, wait=True)

    is_sample = t >= n_ptiles
    rt = jnp.where(is_sample, rts_ref[...], rtp_ref[...])
    n2 = jnp.where(is_sample, n2s_ref[...], n2p_ref[...])
    pos0, pos1 = _slot_positions(rt, base_ref[...], before[...])
    pos_ref[...] = jnp.concatenate([pos0, pos1, jnp.zeros((6, MOE_TD), F32)], axis=0)
    used = _tile_rows(tab_ref, t, n_tiles)

    def group(r0, nrows):
        m0, m1 = _one_hot_rows(r0, nrows, pos0, pos1)
        q = (jnp.where(m0, 1.0, 0.0) + jnp.where(m1, 1.0, 0.0)).astype(BF16)
        buf[slot, pl.ds(r0, nrows), :] = jnp.dot(q, n2, preferred_element_type=F32).astype(BF16)

    group(0, MOE_MAIN)
    for r0 in range(MOE_MAIN, MOE_BR, MOE_BLK):
        @pl.when(used > r0)
        def _():
            group(r0, MOE_BLK)

    _segment_copies(tab_ref, t, n_tiles, buf, xs_ref, sems, slot, to_hbm=True, wait=False)

    @pl.when(t == n_tiles - 1)
    def _():
        if n_tiles >= 2:
            _segment_copies(tab_ref, t - 1, n_tiles, buf, xs_ref, sems, 1 - slot, to_hbm=True, wait=True)
        _segment_copies(tab_ref, t, n_tiles, buf, xs_ref, sems, slot, to_hbm=True, wait=True)
        segment_tails(wait=True)
        lax.fori_loop(first_free, n_gtiles, fill_wait, 0)


def _dispatch_call(tab, base_col, rt_p, n2_p, rt_s, n2_s, r_max):
    n_ptiles = n2_p.shape[0] // MOE_TD
    n_tiles = n_ptiles + n2_s.shape[0] // MOE_TD
    last_p = n_ptiles - 1
    return pl.pallas_call(
        functools.partial(_dispatch_kernel, n_tiles=n_tiles, n_ptiles=n_ptiles,
                          n_gtiles=r_max // MOE_TG),
        grid_spec=pltpu.PrefetchScalarGridSpec(
            num_scalar_prefetch=1, grid=(n_tiles,),
            in_specs=[pl.BlockSpec((8, MOE_TD), lambda t, tab: (0, jnp.minimum(t, last_p))),
                      pl.BlockSpec((MOE_TD, D), lambda t, tab: (jnp.minimum(t, last_p), 0)),
                      pl.BlockSpec((8, MOE_TD), lambda t, tab: (0, 0)),
                      pl.BlockSpec((MOE_TD, D), lambda t, tab: (0, 0)),
                      pl.BlockSpec((None, LANES, 1), lambda t, tab: (t, 0, 0))],
            out_specs=(pl.BlockSpec(memory_space=pl.ANY),
                       pl.BlockSpec((None, 8, MOE_TD), lambda t, tab: (t, 0, 0))),
            scratch_shapes=[pltpu.VMEM((2, MOE_BR, D), BF16),
                            pltpu.VMEM((MOE_TG, D), BF16),
                            pltpu.VMEM((MOE_TD, MOE_TD), BF16),
                            pltpu.SemaphoreType.DMA((2, NE)),
                            pltpu.SemaphoreType.DMA(()),
                            pltpu.SemaphoreType.DMA((NE,))]),
        out_shape=(jax.ShapeDtypeStruct((r_max, D), BF16),
                   jax.ShapeDtypeStruct((n_tiles, 8, MOE_TD), F32)),
        compiler_params=pltpu.CompilerParams(dimension_semantics=("arbitrary",),
                                             vmem_limit_bytes=VMEM_LIMIT),
        name="moe_dispatch",
    )(tab, rt_p, n2_p, rt_s, n2_s, base_col)


MOE_CK = 6


def _experts_kernel(ctab_ref, xs_ref, w1_ref, w3_ref, w2_ref, ys_ref,
                    xbuf, ybuf, zbuf, in_sem, out_sem, zsem, *, n_gtiles, nc_max):
    e = pl.program_id(0)
    c0 = ctab_ref[e]
    c1 = ctab_ref[e + 1]
    n_chunks = ctab_ref[NE]
    first_free = ctab_ref[NE + 1 + 2 * nc_max]

    def fill_copy(j):
        d = pl.multiple_of(j * MOE_TG, MOE_TG)
        return pltpu.make_async_copy(zbuf, ys_ref.at[pl.ds(d, MOE_TG)], zsem)

    def fill_start(j, carry):
        fill_copy(j).start()
        return carry

    def fill_wait(j, carry):
        fill_copy(j).wait()
        return carry

    def span(c):
        r = pl.multiple_of(ctab_ref[NE + 1 + c], MOE_TG)
        n = pl.multiple_of(ctab_ref[NE + 1 + nc_max + c] * MOE_TG, MOE_TG)
        return r, n

    def in_copy(c, slot):
        r, n = span(c)
        return pltpu.make_async_copy(xs_ref.at[pl.ds(r, n)], xbuf.at[slot, pl.ds(0, n)], in_sem.at[slot])

    def out_copy(c, slot):
        r, n = span(c)
        return pltpu.make_async_copy(ybuf.at[slot, pl.ds(0, n)], ys_ref.at[pl.ds(r, n)], out_sem.at[slot])

    @pl.when(e == 0)
    def _():
        zbuf[...] = jnp.zeros_like(zbuf)
        lax.fori_loop(first_free, n_gtiles, fill_start, 0)

        @pl.when(n_chunks > 0)
        def _():
            in_copy(0, 0).start()

    def compute(slot, rows):
        x = xbuf[slot, pl.ds(0, rows), :]
        a = jnp.dot(x, w1_ref[...].astype(BF16), preferred_element_type=F32)
        b = jnp.dot(x, w3_ref[...].astype(BF16), preferred_element_type=F32)
        hid = a * _sigmoid(a) * b
        y = jnp.dot(hid.astype(BF16), w2_ref[...].astype(BF16), preferred_element_type=F32)
        ybuf[slot, pl.ds(0, rows), :] = y.astype(BF16)

    @pl.when(c1 > c0)
    def _():
        def chunk(c, carry):
            slot = lax.rem(c, 2)

            @pl.when(c + 1 < n_chunks)
            def _():
                in_copy(c + 1, 1 - slot).start()

            in_copy(c, slot).wait()

            @pl.when(c >= 2)
            def _():
                out_copy(c - 2, slot).wait()

            k = ctab_ref[NE + 1 + nc_max + c]
            for kk in range(1, MOE_CK + 1):
                @pl.when(k == kk)
                def _():
                    compute(slot, kk * MOE_TG)

            out_copy(c, slot).start()
            return carry

        lax.fori_loop(c0, c1, chunk, 0)

    @pl.when(e == NE - 1)
    def _():
        @pl.when(n_chunks >= 2)
        def _():
            out_copy(n_chunks - 2, lax.rem(n_chunks, 2)).wait()

        @pl.when(n_chunks >= 1)
        def _():
            out_copy(n_chunks - 1, lax.rem(n_chunks - 1, 2)).wait()

        lax.fori_loop(first_free, n_gtiles, fill_wait, 0)


def _experts_call(ctab, xs, w1, w3, w2, nc_max):
    r_max = xs.shape[0]
    w_map = lambda e, ctab: (e, 0, 0)
    ring = pltpu.VMEM((2, MOE_CK * MOE_TG, D), BF16)
    return pl.pallas_call(
        functools.partial(_experts_kernel, n_gtiles=r_max // MOE_TG, nc_max=nc_max),
        grid_spec=pltpu.PrefetchScalarGridSpec(
            num_scalar_prefetch=1, grid=(NE,),
            in_specs=[pl.BlockSpec(memory_space=pl.ANY),
                      pl.BlockSpec((None, D, DE), w_map),
                      pl.BlockSpec((None, D, DE), w_map),
                      pl.BlockSpec((None, DE, D), w_map)],
            out_specs=pl.BlockSpec(memory_space=pl.ANY),
            scratch_shapes=[ring, ring,
                            pltpu.VMEM((MOE_TG, D), BF16),
                            pltpu.SemaphoreType.DMA((2,)), pltpu.SemaphoreType.DMA((2,)),
                            pltpu.SemaphoreType.DMA(())]),
        out_shape=jax.ShapeDtypeStruct((r_max, D), BF16),
        compiler_params=pltpu.CompilerParams(dimension_semantics=("arbitrary",),
                                             vmem_limit_bytes=VMEM_LIMIT),
        name="moe_experts",
    )(ctab, xs, w1, w3, w2)


def _combine_kernel(tab_ref, rt_ref, pos_ref, x1_ref, gt2_ref, gf_ref, ys_ref, y_ref, buf, sems,
                    *, n_tiles, t_off):
    i = pl.program_id(0)
    t = i + t_off
    slot = lax.rem(i, 2)

    @pl.when(i == 0)
    def _():
        buf[...] = jnp.zeros_like(buf)
        _segment_copies(tab_ref, t, n_tiles, buf, ys_ref, sems, slot, to_hbm=False, wait=False)

    @pl.when(i + 1 < pl.num_programs(0))
    def _():
        _segment_copies(tab_ref, t + 1, n_tiles, buf, ys_ref, sems, 1 - slot, to_hbm=False, wait=False)

    _segment_copies(tab_ref, t, n_tiles, buf, ys_ref, sems, slot, to_hbm=False, wait=True)

    rt = rt_ref[...]
    pos0 = pos_ref[0:1, :]
    pos1 = pos_ref[1:2, :]
    used = _tile_rows(tab_ref, t, n_tiles)

    def ungroup(r0, nrows):
        m0, m1 = _one_hot_rows(r0, nrows, pos0, pos1)
        q = (jnp.where(m0, 1.0, 0.0) + jnp.where(m1, 1.0, 0.0)).astype(BF16)
        gw = jnp.sum(jnp.where(m0, rt[2:3, :], 0.0) + jnp.where(m1, rt[3:4, :], 0.0),
                     axis=1, keepdims=True)
        yv = (buf[slot, pl.ds(r0, nrows), :].astype(F32) * gw).astype(BF16)
        return lax.dot_general(q, yv, (((0,), (0,)), ((), ())), preferred_element_type=F32)

    gt2 = gt2_ref[...]
    if gt2.shape[0] not in (1, MOE_TD):
        gt2 = jnp.concatenate([gt2] * (MOE_TD // gt2.shape[0]), axis=0)

    def finish(moe):
        xo = x1_ref[...] + gt2 * moe
        y_ref[...] = _rms(xo) * gf_ref[...]

    finish(ungroup(0, MOE_MAIN))

    @pl.when(used > MOE_MAIN)
    def _():
        moe = ungroup(0, MOE_MAIN)
        for r0 in range(MOE_MAIN, MOE_BR, MOE_BLK):
            moe = moe + ungroup(r0, MOE_BLK)
        finish(moe)


def _combine_call(tab, rt, pos, x1, gt2, gt2_spec, g_final, ys, n_tiles, t_off):
    n_out_tiles = x1.shape[0] // MOE_TD
    return pl.pallas_call(
        functools.partial(_combine_kernel, n_tiles=n_tiles, t_off=t_off),
        grid_spec=pltpu.PrefetchScalarGridSpec(
            num_scalar_prefetch=1, grid=(n_out_tiles,),
            in_specs=[pl.BlockSpec((8, MOE_TD), lambda t, tab: (0, t)),
                      pl.BlockSpec((None, 8, MOE_TD), lambda t, tab: (t + t_off, 0, 0)),
                      pl.BlockSpec((MOE_TD, D), lambda t, tab: (t, 0)),
                      gt2_spec,
                      pl.BlockSpec((1, D), lambda t, tab: (0, 0)),
                      pl.BlockSpec(memory_space=pl.ANY)],
            out_specs=pl.BlockSpec((MOE_TD, D), lambda t, tab: (t, 0)),
            scratch_shapes=[pltpu.VMEM((2, MOE_BR, D), BF16),
                            pltpu.SemaphoreType.DMA((2, NE))]),
        out_shape=jax.ShapeDtypeStruct((n_out_tiles * MOE_TD, D), F32),
        compiler_params=pltpu.CompilerParams(dimension_semantics=("arbitrary",),
                                             vmem_limit_bytes=VMEM_LIMIT),
        name="moe_combine",
    )(tab, rt, pos, x1, gt2, g_final.reshape(1, D), ys)


def _moe_plan(cnt, nc_max):
    cnt_al = (cnt + ROW_ALIGN - 1) // ROW_ALIGN * ROW_ALIGN
    seg_rows = cnt_al.sum(axis=0)
    seg_pad = (seg_rows + MOE_TG - 1) // MOE_TG * MOE_TG
    seg_start = jnp.cumsum(seg_pad) - seg_pad
    dst = seg_start[None, :] + jnp.cumsum(cnt_al, axis=0) - cnt_al
    boff = jnp.cumsum(cnt_al, axis=1) - cnt_al
    tile_end = jnp.cumsum(seg_pad // MOE_TG)
    n_active = tile_end[-1:].astype(jnp.int32)
    tab = jnp.concatenate([cnt_al.ravel(), boff.ravel(), dst.ravel(),
                           seg_start + seg_rows, seg_pad - seg_rows, n_active,
                           cnt_al.sum(axis=1)]).astype(jnp.int32)
    nt = seg_pad // MOE_TG
    nfull = nt // MOE_CK
    rem = nt % MOE_CK
    nch = nfull + (rem > 0)
    cend = jnp.cumsum(nch)
    cstart = cend - nch
    c = jnp.arange(nc_max, dtype=jnp.int32)
    ce = jnp.minimum(jnp.sum(c[:, None] >= cend[None, :], axis=1), NE - 1)
    local = c - cstart[ce]
    valid = c < cend[-1]
    ck = jnp.where(valid, jnp.where(local < nfull[ce], MOE_CK, rem[ce]), 0)
    crow = jnp.where(valid, seg_start[ce] + local * (MOE_CK * MOE_TG), 0)
    ctab = jnp.concatenate([cstart, cend[-1:], crow, ck, n_active]).astype(jnp.int32)
    base_col = jnp.pad(boff, ((0, 0), (0, LANES - NE))).astype(F32)[:, :, None]
    return tab, ctab, base_col


def kernel(x_prompt, x_sample, c_prompt, c_sample, state_ssm_re, state_ssm_im, cache_conv, w_ada, b_ada, g_norm_mix, w_in, ssm_a_re, ssm_a_im, ssm_log_dt, ssm_b_re, ssm_b_im, ssm_c_re, ssm_c_im, ssm_d, w_ssm_glu, b_ssm_glu, w_dw, b_dw, ln_conv_g, ln_conv_b, g_out_ssm, g_out_conv, w_out, g_norm_ffn, w_router_grp, b_router_grp, w_router_exp, b_router_exp, w_exp_gate, w_exp_up, w_exp_down, g_final):
    depth = w_ada.shape[0]
    assert depth == 1
    bsz, seq, _ = x_prompt.shape
    nb, nt, _ = x_sample.shape

    n_c = bsz + nb
    c_pad = -n_c % 16
    c_all = jnp.concatenate([c_prompt, c_sample, jnp.zeros((c_pad, D), F32)], axis=0)
    mod_p, mod_s = _mod_call(c_all, w_ada[0], b_ada[0], bsz, nb)
    mod_p = mod_p.reshape(bsz, 6, D)

    ab_re, ab_im, bb_re, bb_im, c_im_neg = _ssm_prep_call(
        ssm_a_re[0], ssm_a_im[0], ssm_log_dt[0], ssm_b_re[0], ssm_b_im[0], ssm_c_im[0])
    wb, wc = _block_diag_weights(bb_re, bb_im, ssm_c_re[0], c_im_neg)
    a_tok = jnp.stack([ab_re.reshape(NCHUNK, LANES), ab_im.reshape(NCHUNK, LANES)])
    a_row = jnp.stack([ab_re.reshape(NSTATE), ab_im.reshape(NSTATE)])

    w_r = jnp.concatenate([w_router_exp[0].reshape(D, NE), w_router_grp[0],
                           jnp.zeros((D, LANES - NE - NG), F32)], axis=1)
    w_r_hi = w_r.astype(BF16)
    w_r = jnp.concatenate([w_r_hi, (w_r - w_r_hi.astype(F32)).astype(BF16)], axis=1)
    b_r = jnp.concatenate([b_router_exp[0].reshape(NE), b_router_grp[0],
                           jnp.zeros((LANES - NE - NG,), F32)]).reshape(1, LANES)
    w_dw_p = jnp.concatenate([w_dw[0], jnp.zeros((1, CONV_CH), F32)], axis=0)

    row = lambda v: v.reshape(1, -1)
    common_a = (row(g_norm_mix[0]), w_in[0].astype(BF16), wb)
    common_b = (wc, row(ssm_d[0].reshape(SSM_W)), w_ssm_glu[0].astype(BF16), row(b_ssm_glu[0]),
                w_dw_p, row(b_dw[0]), row(ln_conv_g[0]), row(ln_conv_b[0]),
                row(g_out_ssm[0]), row(g_out_conv[0]), w_out[0].astype(BF16),
                row(g_norm_ffn[0]), w_r, b_r)

    n_p = bsz * seq
    n_s = nb * nt
    n_all = n_p + n_s
    assert n_s == MOE_TD and seq % MOE_TD == 0 and MOE_TD % PROMPT_TC == 0
    wts_p = common_a + (a_tok,) + common_b
    x1_p, n2_p, rt_p, cnt_p, hr_p, hi_p, cache_p = _prompt_mixer_call(x_prompt, mod_p, wts_p, PROMPT_TC)

    x_tm = jnp.transpose(x_sample, (1, 0, 2)).reshape(nt * nb, D)
    cache_tm = jnp.transpose(cache_conv[0], (1, 0, 2))
    wts_s = common_a + (a_row,) + common_b
    x1_s, n2_s, rt_s, cnt_s, hr_s, hi_s, glu_s = _sample_mixer_call(
        x_tm, mod_s, state_ssm_re[0].reshape(nb, NSTATE), state_ssm_im[0].reshape(nb, NSTATE),
        cache_tm, wts_s, nb, nt)

    n_ptiles = n_p // MOE_TD
    n_tiles = n_all // MOE_TD
    r_max = -(-(2 * n_all + n_tiles * NE * (ROW_ALIGN - 1) + NE * (MOE_TG - ROW_ALIGN)) // MOE_TG) * MOE_TG
    cnt = jnp.concatenate([cnt_p.reshape(n_ptiles, MOE_TD // PROMPT_TC, LANES).sum(axis=1), cnt_s])
    nc_max = r_max // MOE_TG // MOE_CK + NE
    tab, ctab, base_col = _moe_plan(cnt[:, :NE].astype(jnp.int32), nc_max)
    xs, pos = _dispatch_call(tab, base_col, rt_p, n2_p, rt_s, n2_s, r_max)
    ys = _experts_call(ctab, xs, w_exp_gate[0], w_exp_up[0], w_exp_down[0], nc_max)
    tiles_per_b = seq // MOE_TD
    gt2_p = mod_p[:, 5:6, :]
    y_p = _combine_call(tab, rt_p, pos, x1_p, gt2_p,
                        pl.BlockSpec((None, 1, D), lambda t, tab: (t // tiles_per_b, 0, 0)),
                        g_final, ys, n_tiles, 0)
    y_s = _combine_call(tab, rt_s, pos, x1_s, mod_s,
                        pl.BlockSpec((nb, D), lambda t, tab: (0, 5)),
                        g_final, ys, n_tiles, n_ptiles)

    y_prompt = y_p.reshape(bsz, seq, D)
    y_sample = jnp.transpose(y_s.reshape(nt, nb, D), (1, 0, 2))
    new_cache_s = jnp.concatenate(
        [cache_conv[0][:, nt:, :], jnp.transpose(glu_s.reshape(nt, nb, CONV_CH), (1, 0, 2))], axis=1)
    return (y_prompt, y_sample,
            hr_p.reshape(1, bsz, G, P), hi_p.reshape(1, bsz, G, P), cache_p[None],
            hr_s.reshape(1, nb, G, P), hi_s.reshape(1, nb, G, P), new_cache_s[None])
```
